```python
import math
import jax, jax.numpy as jnp
from jax import lax
import numpy as np

D_MODEL = 1024
BATCH = 4
SEQ = 4096
DEPTH = 2
DEC_BATCH = 128
DEC_SEQ = 8
PAST_LEN = 8192
PAGE_SIZE = 128

N_META = 16
N_A_LAYERS = DEPTH // 2
N_B_LAYERS = DEPTH - N_A_LAYERS
A_HEADS = 8
A_DK = D_MODEL // A_HEADS
A_DV = D_MODEL // A_HEADS
A_CHUNK = 64
B_Q_HEADS = 16
B_KV_HEADS = 4
B_HEAD_DIM = D_MODEL // B_Q_HEADS
B_GROUP = B_Q_HEADS // B_KV_HEADS
WINDOW = 128
ATT_BLOCK = 128
ROPE_DIM = B_HEAD_DIM // 4
ROPE_THETA = 500000.0
N_GROUPS = 4
EXPERTS_PER_GROUP = 8
N_EXPERTS = N_GROUPS * EXPERTS_PER_GROUP
TOP_K = 2
D_EXPERT = D_MODEL // 2
EXPERT_BLOCK = 128
RMS_EPS = 1e-6

kernel_name = "yoco_hgrn2_swa_sink_hmoe_step"

F32 = jnp.float32


def rms_norm(x, gain):
    xf = x.astype(F32)
    y = xf * lax.rsqrt(jnp.mean(xf * xf, axis=-1, keepdims=True) + RMS_EPS)
    return (y * gain.astype(F32)).astype(x.dtype)


def rope_partial(x, pos):
    half = ROPE_DIM // 2
    inv = jnp.exp(-math.log(ROPE_THETA) * jnp.arange(half, dtype=F32) * (2.0 / ROPE_DIM))
    ang = pos.astype(F32)[:, None] * inv[None, :]
    cos = jnp.cos(ang)[:, None, :]
    sin = jnp.sin(ang)[:, None, :]
    xr = x[..., :ROPE_DIM].astype(F32)
    x1, x2 = xr[..., :half], xr[..., half:]
    rot = jnp.concatenate([x1 * cos - x2 * sin, x2 * cos + x1 * sin], axis=-1)
    return jnp.concatenate([rot.astype(x.dtype), x[..., ROPE_DIM:]], axis=-1)


def hgrn2_scan(q, k, v, logf, s0, chunk):
    bsz, length, heads, _ = q.shape
    n = length // chunk

    def to_chunks(a):
        return a.reshape(bsz, n, chunk, heads, a.shape[-1]).transpose(1, 0, 3, 2, 4)

    causal = jnp.tril(jnp.ones((chunk, chunk), bool))[:, :, None]

    def step(S, inp):
        qi, ki, vi, gi = inp
        b = jnp.cumsum(gi, axis=-2)
        diff = b[..., :, None, :] - b[..., None, :, :]
        decay = jnp.where(causal, jnp.exp(jnp.where(causal, diff, 0.0)), 0.0)
        attn = jnp.einsum('bhtd,bhsd,bhtsd->bhts', qi, ki, decay)
        o = jnp.einsum('bhts,bhsv->bhtv', attn, vi) + jnp.einsum('bhtd,bhdv->bhtv', qi * jnp.exp(b), S)
        b_last = b[..., -1, :]
        S_new = jnp.exp(b_last)[..., None] * S + jnp.einsum(
            'bhsd,bhsv->bhdv', ki * jnp.exp(b_last[..., None, :] - b), vi)
        return S_new, o

    s_fin, oc = lax.scan(step, s0, (to_chunks(q), to_chunks(k), to_chunks(v), to_chunks(logf)))
    o = oc.transpose(1, 0, 3, 2, 4).reshape(bsz, length, heads, v.shape[-1])
    return o, s_fin


def hgrn2_mixer(h, w_in, lb, o_gain, w_out, s0, chunk, pad_front):
    bsz, length, _ = h.shape
    hk = A_HEADS * A_DK
    hv = A_HEADS * A_DV
    q, f, i, g = jnp.split(h @ w_in, [hk, 2 * hk, 2 * hk + hv], axis=-1)
    forget = lb + (1.0 - lb) * jax.nn.sigmoid(f.astype(F32))
    qf = jax.nn.silu(q.astype(F32))
    kf = 1.0 - forget
    logf = jnp.log(forget)
    vf = i.astype(F32)
    pad_back = (-(pad_front + length)) % chunk

    def prep(a, d):
        return jnp.pad(a.reshape(bsz, length, A_HEADS, d), ((0, 0), (pad_front, pad_back), (0, 0), (0, 0)))

    o, s_fin = hgrn2_scan(prep(qf, A_DK), prep(kf, A_DK), prep(vf, A_DV), prep(logf, A_DK),
                          s0.astype(F32), chunk)
    o = o[:, pad_front:pad_front + length].astype(h.dtype)
    o = rms_norm(o, o_gain).reshape(bsz, length, hv) * jax.nn.silu(g)
    return o @ w_out, s_fin.astype(h.dtype)


def shared_kv(h, pos, kv_norm, kv_w, k_norm):
    bsz, length, _ = h.shape
    k, v = jnp.split(rms_norm(h, kv_norm) @ kv_w, 2, axis=-1)
    k = k.reshape(bsz, length, B_KV_HEADS, B_HEAD_DIM)
    v = v.reshape(bsz, length, B_KV_HEADS, B_HEAD_DIM)
    return rope_partial(rms_norm(k, k_norm), pos), v


def query_side(h, pos, norm_gain, wq, q_norm):
    bsz, length, _ = h.shape
    q = (rms_norm(h, norm_gain) @ wq).reshape(bsz, length, B_Q_HEADS, B_HEAD_DIM)
    return rope_partial(rms_norm(q, q_norm), pos)


def sink_softmax(scores, mask, sink):
    s = jnp.where(mask, scores, -jnp.inf)
    m = jnp.maximum(jnp.max(s, axis=-1, keepdims=True), sink)
    p = jnp.exp(s - m)
    return p / (jnp.sum(p, axis=-1, keepdims=True) + jnp.exp(sink - m))


def window_attn_prompt(q, k, v, sinks):
    bsz, length = q.shape[:2]
    n = -(-length // ATT_BLOCK)
    lpad = n * ATT_BLOCK
    padw = ((0, 0), (0, lpad - length), (0, 0), (0, 0))
    qb = jnp.pad(q, padw).reshape(bsz, n, ATT_BLOCK, B_KV_HEADS, B_GROUP, B_HEAD_DIM)
    kb = jnp.pad(k, padw).reshape(bsz, n, ATT_BLOCK, B_KV_HEADS, B_HEAD_DIM)
    vb = jnp.pad(v, padw).reshape(bsz, n, ATT_BLOCK, B_KV_HEADS, B_HEAD_DIM)

    def with_prev(a):
        prev = jnp.pad(a[:, :-1], ((0, 0), (1, 0), (0, 0), (0, 0), (0, 0)))
        return jnp.concatenate([prev, a], axis=2)

    kk, vv = with_prev(kb), with_prev(vb)
    qpos = jnp.arange(lpad, dtype=jnp.int32).reshape(n, ATT_BLOCK)
    kpos = jnp.concatenate([qpos - ATT_BLOCK, qpos], axis=1)
    rel = qpos[:, :, None] - kpos[:, None, :]
    mask = (rel >= 0) & (rel <= WINDOW) & (kpos[:, None, :] >= 0)
    scores = jnp.einsum('bnqkgd,bnskd->bnkgqs', qb.astype(F32), kk.astype(F32)) * (B_HEAD_DIM ** -0.5)
    sink = sinks.astype(F32).reshape(1, 1, B_KV_HEADS, B_GROUP, 1, 1)
    probs = sink_softmax(scores, mask[None, :, None, None], sink)
    out = jnp.einsum('bnkgqs,bnskd->bnqkgd', probs.astype(v.dtype), vv)
    return out.reshape(bsz, lpad, B_Q_HEADS * B_HEAD_DIM)[:, :length]


def window_attn_sample(q, k_new, v_new, k_buf, v_buf, sinks):
    bsz, t_new = q.shape[:2]
    w = k_buf.shape[1]
    kk = jnp.concatenate([k_buf, k_new.astype(k_buf.dtype)], axis=1)
    vv = jnp.concatenate([v_buf, v_new.astype(v_buf.dtype)], axis=1)
    qpos = PAST_LEN + jnp.arange(t_new, dtype=jnp.int32)
    kpos = jnp.concatenate([PAST_LEN - w + jnp.arange(w, dtype=jnp.int32), qpos])
    rel = qpos[:, None] - kpos[None, :]
    mask = (rel >= 0) & (rel <= WINDOW)
    qg = q.reshape(bsz, t_new, B_KV_HEADS, B_GROUP, B_HEAD_DIM)
    scores = jnp.einsum('btkgd,bskd->bkgts', qg.astype(F32), kk.astype(F32)) * (B_HEAD_DIM ** -0.5)
    probs = sink_softmax(scores, mask, sinks.astype(F32).reshape(1, B_KV_HEADS, B_GROUP, 1, 1))
    out = jnp.einsum('bkgts,bskd->btkgd', probs.astype(vv.dtype), vv)
    return out.reshape(bsz, t_new, B_Q_HEADS * B_HEAD_DIM)


def expert_dispatch(x, ids, weights, w13, w2):
    n_tok = x.shape[0]
    n_asg = n_tok * TOP_K
    flat_e = ids.reshape(n_asg)
    flat_tok = jnp.repeat(jnp.arange(n_tok, dtype=jnp.int32), TOP_K)
    flat_w = weights.reshape(n_asg)
    order = jnp.argsort(flat_e)
    se = flat_e[order]
    counts = jnp.bincount(flat_e, length=N_EXPERTS)
    start = jnp.cumsum(counts) - counts
    padded = (counts + EXPERT_BLOCK - 1) // EXPERT_BLOCK * EXPERT_BLOCK
    pend = jnp.cumsum(padded)
    pstart = pend - padded
    dest = pstart[se] + (jnp.arange(n_asg, dtype=jnp.int32) - start[se])
    n_blocks = -(-(n_asg + N_EXPERTS * (EXPERT_BLOCK - 1)) // EXPERT_BLOCK)
    n_slots = n_blocks * EXPERT_BLOCK
    slot_tok = jnp.full((n_slots,), n_tok, jnp.int32).at[dest].set(flat_tok[order])
    slot_w = jnp.zeros((n_slots,), F32).at[dest].set(flat_w[order])
    block_e = jnp.minimum(jnp.searchsorted(pend, jnp.arange(n_blocks, dtype=jnp.int32) * EXPERT_BLOCK,
                                           side='right'), N_EXPERTS - 1)
    x_pad = jnp.concatenate([x, jnp.zeros((1, x.shape[1]), x.dtype)], axis=0)
    xb = x_pad[slot_tok].reshape(n_blocks, EXPERT_BLOCK, x.shape[1])

    def block_ffn(args):
        xi, e = args
        a, u = jnp.split(xi @ w13[e], 2, axis=-1)
        return (jax.nn.silu(a) * u) @ w2[e]

    yb = lax.map(block_ffn, (xb, block_e))
    y = yb.reshape(n_slots, x.shape[1]) * slot_w[:, None].astype(x.dtype)
    return jax.ops.segment_sum(y, slot_tok, num_segments=n_tok + 1)[:n_tok]


def hier_moe(h, w_group, b_group, w_expert, b_expert, w13, w2):
    shp = h.shape
    x = h.reshape(-1, D_MODEL)
    n_tok = x.shape[0]
    g_logits = (x @ w_group).astype(F32) + b_group.astype(F32)
    g_sel = jnp.argmax(g_logits, axis=-1)
    g_w = jnp.take_along_axis(jax.nn.softmax(g_logits, axis=-1), g_sel[:, None], axis=1)
    e_logits = ((x @ w_expert).astype(F32) + b_expert.astype(F32)).reshape(n_tok, N_GROUPS, EXPERTS_PER_GROUP)
    e_logits = jnp.take_along_axis(e_logits, g_sel[:, None, None], axis=1)[:, 0]
    top_l, top_i = lax.top_k(e_logits, TOP_K)
    weights = g_w * jax.nn.softmax(top_l, axis=-1)
    ids = g_sel[:, None].astype(jnp.int32) * EXPERTS_PER_GROUP + top_i.astype(jnp.int32)
    return expert_dispatch(x, ids, weights, w13, w2).reshape(shp)


def setup_inputs(seed: int = 0) -> dict:
    key = jax.random.key(seed)
    ks = jax.random.split(key, 32)

    def nrm(k, shape, scale=1.0):
        return jax.random.normal(k, shape, F32) * scale

    hk = A_HEADS * A_DK
    hv = A_HEADS * A_DV
    qd = B_Q_HEADS * B_HEAD_DIM
    w_buf = min(WINDOW, PAST_LEN)
    return {
        "x_prompt": nrm(ks[0], (BATCH, SEQ, D_MODEL)),
        "x_sample": nrm(ks[1], (DEC_BATCH, DEC_SEQ, D_MODEL)),
        "state_hgrn": nrm(ks[2], (N_A_LAYERS, DEC_BATCH, A_HEADS, A_DK, A_DV), 0.5),
        "cache_k_win": nrm(ks[3], (DEC_BATCH, w_buf, B_KV_HEADS, B_HEAD_DIM)),
        "cache_v_win": nrm(ks[4], (DEC_BATCH, w_buf, B_KV_HEADS, B_HEAD_DIM)),
        "meta_tokens": nrm(ks[5], (N_META, D_MODEL)),
        "a_norm": 1.0 + nrm(ks[6], (N_A_LAYERS, D_MODEL), 0.02),
        "a_w_in": nrm(ks[7], (N_A_LAYERS, D_MODEL, 2 * hk + 2 * hv), D_MODEL ** -0.5),
        "a_lower_logits": nrm(ks[8], (N_A_LAYERS + 1, hk), 0.1),
        "a_out_norm": 1.0 + nrm(ks[9], (N_A_LAYERS, A_DV), 0.02),
        "a_w_out": nrm(ks[10], (N_A_LAYERS, hv, D_MODEL), hv ** -0.5),
        "kv_norm": 1.0 + nrm(ks[11], (D_MODEL,), 0.02),
        "kv_w": nrm(ks[12], (D_MODEL, 2 * B_KV_HEADS * B_HEAD_DIM), D_MODEL ** -0.5),
        "k_norm": 1.0 + nrm(ks[13], (B_HEAD_DIM,), 0.02),
        "b_norm": 1.0 + nrm(ks[14], (N_B_LAYERS, D_MODEL), 0.02),
        "b_wq": nrm(ks[15], (N_B_LAYERS, D_MODEL, qd), D_MODEL ** -0.5),
        "b_q_norm": 1.0 + nrm(ks[16], (N_B_LAYERS, B_HEAD_DIM), 0.02),
        "b_sinks": nrm(ks[17], (N_B_LAYERS, B_Q_HEADS), 1.0),
        "b_w_out": nrm(ks[18], (N_B_LAYERS, qd, D_MODEL), qd ** -0.5),
        "moe_norm": 1.0 + nrm(ks[19], (DEPTH, D_MODEL), 0.02),
        "moe_w_group": nrm(ks[20], (DEPTH, D_MODEL, N_GROUPS), D_MODEL ** -0.5),
        "moe_b_group": nrm(ks[21], (DEPTH, N_GROUPS), 0.01),
        "moe_w_expert": nrm(ks[22], (DEPTH, D_MODEL, N_EXPERTS), D_MODEL ** -0.5),
        "moe_b_expert": nrm(ks[23], (DEPTH, N_EXPERTS), 0.01),
        "moe_w13": nrm(ks[24], (DEPTH, N_EXPERTS, D_MODEL, 2 * D_EXPERT), D_MODEL ** -0.5),
        "moe_w2": nrm(ks[25], (DEPTH, N_EXPERTS, D_EXPERT, D_MODEL), D_EXPERT ** -0.5),
    }


def reference(x_prompt, x_sample, state_hgrn, cache_k_win, cache_v_win, meta_tokens,
              a_norm, a_w_in, a_lower_logits, a_out_norm, a_w_out,
              kv_norm, kv_w, k_norm,
              b_norm, b_wq, b_q_norm, b_sinks, b_w_out,
              moe_norm, moe_w_group, moe_b_group, moe_w_expert, moe_b_expert, moe_w13, moe_w2):
    dt = x_prompt.dtype
    n_prompt = x_prompt.shape[0]
    meta = jnp.broadcast_to(meta_tokens.astype(dt)[None], (n_prompt, N_META, D_MODEL))
    hp = jnp.concatenate([meta, x_prompt], axis=1)
    hs = x_sample
    len_p = hp.shape[1]
    len_s = hs.shape[1]
    pos_p = jnp.arange(len_p, dtype=jnp.int32)
    pos_s = PAST_LEN + jnp.arange(len_s, dtype=jnp.int32)
    lower = jnp.cumsum(jax.nn.softmax(a_lower_logits.astype(F32), axis=0), axis=0)
    s_prompt, s_sample = [], []
    k_p = v_p = k_s = v_s = None
    for layer in range(DEPTH):
        if layer < N_A_LAYERS:
            j = layer
            s0_p = jnp.zeros((n_prompt, A_HEADS, A_DK, A_DV), dt)
            o_p, st_p = hgrn2_mixer(rms_norm(hp, a_norm[j]), a_w_in[j], lower[j], a_out_norm[j], a_w_out[j],
                                    s0_p, A_CHUNK, A_CHUNK - N_META)
            o_s, st_s = hgrn2_mixer(rms_norm(hs, a_norm[j]), a_w_in[j], lower[j], a_out_norm[j], a_w_out[j],
                                    state_hgrn[j], min(A_CHUNK, len_s), 0)
            s_prompt.append(st_p)
            s_sample.append(st_s)
        else:
            j = layer - N_A_LAYERS
            q_p = query_side(hp, pos_p, b_norm[j], b_wq[j], b_q_norm[j])
            q_s = query_side(hs, pos_s, b_norm[j], b_wq[j], b_q_norm[j])
            o_p = window_attn_prompt(q_p, k_p, v_p, b_sinks[j]) @ b_w_out[j]
            o_s = window_attn_sample(q_s, k_s, v_s, cache_k_win, cache_v_win, b_sinks[j]) @ b_w_out[j]
        hp = hp + o_p
        hs = hs + o_s
        hp = hp + hier_moe(rms_norm(hp, moe_norm[layer]), moe_w_group[layer], moe_b_group[layer],
                           moe_w_expert[layer], moe_b_expert[layer], moe_w13[layer], moe_w2[layer])
        hs = hs + hier_moe(rms_norm(hs, moe_norm[layer]), moe_w_group[layer], moe_b_group[layer],
                           moe_w_expert[layer], moe_b_expert[layer], moe_w13[layer], moe_w2[layer])
        if layer == N_A_LAYERS - 1:
            k_p, v_p = shared_kv(hp, pos_p, kv_norm, kv_w, k_norm)
            k_s, v_s = shared_kv(hs, pos_s, kv_norm, kv_w, k_norm)
    w_p = min(WINDOW, len_p)
    w_s = cache_k_win.shape[1]
    k_win_s = jnp.concatenate([cache_k_win, k_s.astype(cache_k_win.dtype)], axis=1)[:, -w_s:]
    v_win_s = jnp.concatenate([cache_v_win, v_s.astype(cache_v_win.dtype)], axis=1)[:, -w_s:]
    return (hp[:, N_META:], hs, jnp.stack(s_prompt), jnp.stack(s_sample),
            k_p[:, -w_p:], v_p[:, -w_p:], k_win_s, v_win_s)
```

```python
import functools
import math

import numpy as np
import jax
import jax.numpy as jnp
from jax import lax
from jax.experimental import pallas as pl
from jax.experimental.pallas import tpu as pltpu

F32 = jnp.float32
BF16 = jnp.bfloat16

D_MODEL = 1024
BATCH = 4
SEQ = 4096
DEC_BATCH = 128
DEC_SEQ = 8
PAST_LEN = 8192
N_META = 16
A_HEADS = 8
A_DK = 128
A_DV = 128
Q_HEADS = 16
KV_HEADS = 4
HEAD_DIM = 64
KV_DIM = KV_HEADS * HEAD_DIM
WINDOW = 128
ROPE_DIM = 16
ROPE_THETA = 500000.0
N_GROUPS = 4
EXPERTS_PER_GROUP = 8
N_EXPERTS = 32
D_EXPERT = 512
RMS_EPS = 1e-6

LANES = 128
SUBLANES = 8
VMEM_LIMIT = 56 * 1024 * 1024

ROW_TILE = 256
T_PROMPT = BATCH * SEQ
T_SAMPLE = DEC_BATCH * DEC_SEQ
OFF_SAMPLE = T_PROMPT
OFF_META = T_PROMPT + T_SAMPLE
T_REAL = OFF_META + N_META
T_ALL = -(-T_REAL // ROW_TILE) * ROW_TILE
N_TILES = T_ALL // ROW_TILE

SCAN_CHUNK = 128
ATT_BLOCK = 128
EXPERT_BLOCK = 256


def _cparams(sem):
    return pltpu.CompilerParams(dimension_semantics=sem, vmem_limit_bytes=VMEM_LIMIT)


def _nt_dot(a, b):
    return lax.dot_general(a, b, (((1,), (1,)), ((), ())), preferred_element_type=F32)


def _rms(x, gain):
    ms = jnp.mean(x * x, axis=-1, keepdims=True)
    return x * lax.rsqrt(ms + RMS_EPS) * gain


def _silu(x):
    return x * jax.nn.sigmoid(x)


def _norm_matmul_kernel(x_ref, g_ref, w_ref, o_ref):
    xn = _rms(x_ref[...], g_ref[...])
    o_ref[...] = jnp.dot(xn.astype(BF16), w_ref[...], preferred_element_type=F32).astype(o_ref.dtype)


def norm_matmul(x, gain, w_bf16, out_dtype):
    t, d = x.shape
    n = w_bf16.shape[1]
    return pl.pallas_call(
        _norm_matmul_kernel,
        out_shape=jax.ShapeDtypeStruct((t, n), out_dtype),
        grid=(t // ROW_TILE,),
        in_specs=[pl.BlockSpec((ROW_TILE, d), lambda i: (i, 0)),
                  pl.BlockSpec((1, d), lambda i: (0, 0)),
                  pl.BlockSpec((d, n), lambda i: (0, 0))],
        out_specs=pl.BlockSpec((ROW_TILE, n), lambda i: (i, 0)),
        compiler_params=_cparams(("parallel",)),
        name="norm_matmul",
    )(x, gain.reshape(1, d), w_bf16)


def _head_norm_rope(y, hmean_ref, hgain, cos_t, sina_t, sinb_t):
    rows, width = y.shape
    sq = (y * y).astype(BF16)
    parts = []
    for s in range(width // 256):
        parts.append(jnp.dot(sq[:, s * 256:(s + 1) * 256], hmean_ref[...], preferred_element_type=F32))
    ms = parts[0] if len(parts) == 1 else jnp.concatenate(parts, axis=1)
    yn = y * lax.rsqrt(ms + RMS_EPS) * hgain
    reps = width // LANES
    cos_w = jnp.concatenate([cos_t] * reps, axis=1)
    sina_w = jnp.concatenate([sina_t] * reps, axis=1)
    sinb_w = jnp.concatenate([sinb_t] * reps, axis=1)
    half = ROPE_DIM // 2
    nxt = pltpu.roll(yn, width - half, 1)
    prv = pltpu.roll(yn, half, 1)
    return yn * cos_w + nxt * sina_w + prv * sinb_w


def _kv_kernel(x_ref, g_ref, w_ref, hmean_ref, hg_ref, cos_ref, sina_ref, sinb_ref, k_ref, v_ref):
    xn = _rms(x_ref[...], g_ref[...])
    z = jnp.dot(xn.astype(BF16), w_ref[...], preferred_element_type=F32)
    k = _head_norm_rope(z[:, :KV_DIM], hmean_ref, hg_ref[...], cos_ref[...], sina_ref[...], sinb_ref[...])
    k_ref[...] = k
    v_ref[...] = z[:, KV_DIM:]


def kv_project(x, gain, w_bf16, hmean, hgain_w, cos_t, sina_t, sinb_t):
    t, d = x.shape
    row = lambda i: (i, 0)
    fix = lambda i: (0, 0)
    return pl.pallas_call(
        _kv_kernel,
        out_shape=(jax.ShapeDtypeStruct((t, KV_DIM), F32), jax.ShapeDtypeStruct((t, KV_DIM), F32)),
        grid=(t // ROW_TILE,),
        in_specs=[pl.BlockSpec((ROW_TILE, d), row), pl.BlockSpec((1, d), fix),
                  pl.BlockSpec((d, 2 * KV_DIM), fix), pl.BlockSpec((256, 256), fix),
                  pl.BlockSpec((1, KV_DIM), fix),
                  pl.BlockSpec((ROW_TILE, LANES), row), pl.BlockSpec((ROW_TILE, LANES), row),
                  pl.BlockSpec((ROW_TILE, LANES), row)],
        out_specs=(pl.BlockSpec((ROW_TILE, KV_DIM), row), pl.BlockSpec((ROW_TILE, KV_DIM), row)),
        compiler_params=_cparams(("parallel",)),
        name="kv_project",
    )(x, gain.reshape(1, d), w_bf16, hmean, hgain_w, cos_t, sina_t, sinb_t)


def _q_kernel(x_ref, g_ref, w_ref, hmean_ref, hg_ref, cos_ref, sina_ref, sinb_ref, q_ref):
    xn = _rms(x_ref[...], g_ref[...])
    z = jnp.dot(xn.astype(BF16), w_ref[...], preferred_element_type=F32)
    q = _head_norm_rope(z, hmean_ref, hg_ref[...], cos_ref[...], sina_ref[...], sinb_ref[...])
    q_ref[...] = q.astype(q_ref.dtype)


def q_project(x, gain, w_bf16, hmean, hgain_w, cos_t, sina_t, sinb_t):
    t, d = x.shape
    row = lambda i: (i, 0)
    fix = lambda i: (0, 0)
    return pl.pallas_call(
        _q_kernel,
        out_shape=jax.ShapeDtypeStruct((t, d), BF16),
        grid=(t // ROW_TILE,),
        in_specs=[pl.BlockSpec((ROW_TILE, d), row), pl.BlockSpec((1, d), fix),
                  pl.BlockSpec((d, d), fix), pl.BlockSpec((256, 256), fix),
                  pl.BlockSpec((1, d), fix),
                  pl.BlockSpec((ROW_TILE, LANES), row), pl.BlockSpec((ROW_TILE, LANES), row),
                  pl.BlockSpec((ROW_TILE, LANES), row)],
        out_specs=pl.BlockSpec((ROW_TILE, d), row),
        compiler_params=_cparams(("parallel",)),
        name="q_project",
    )(x, gain.reshape(1, d), w_bf16, hmean, hgain_w, cos_t, sina_t, sinb_t)


def _matmul_residual_kernel(a_ref, w_ref, r_ref, o_ref):
    o_ref[...] = r_ref[...] + jnp.dot(a_ref[...], w_ref[...], preferred_element_type=F32)


def matmul_residual(a_bf16, w_bf16, resid):
    t, k = a_bf16.shape
    n = w_bf16.shape[1]
    return pl.pallas_call(
        _matmul_residual_kernel,
        out_shape=jax.ShapeDtypeStruct((t, n), F32),
        grid=(t // ROW_TILE,),
        in_specs=[pl.BlockSpec((ROW_TILE, k), lambda i: (i, 0)),
                  pl.BlockSpec((k, n), lambda i: (0, 0)),
                  pl.BlockSpec((ROW_TILE, n), lambda i: (i, 0))],
        out_specs=pl.BlockSpec((ROW_TILE, n), lambda i: (i, 0)),
        compiler_params=_cparams(("parallel",)),
        name="matmul_residual",
    )(a_bf16, w_bf16, resid)


def _scan_levels(c):
    levels = []
    m = c
    while m >= 2:
        levels.append(m)
        m //= 2
    return levels


def _scan_kernel(z_ref, s0_ref, lb_ref, og_ref, tri_ref, bmask_ref, o_ref, sfin_ref, s_scr, b_scr, *, chunk):
    c_idx = pl.program_id(1)
    levels = _scan_levels(chunk)
    mrows = max(chunk, 16)
    hk = A_HEADS * A_DK

    @pl.when(c_idx == 0)
    def _():
        s_scr[...] = s0_ref[0]

    row = lax.broadcasted_iota(jnp.int32, (chunk, LANES), 0)
    sub = lax.broadcasted_iota(jnp.int32, (SUBLANES, LANES), 0)
    og = og_ref[...]

    def pad_rows(x, n):
        if x.shape[0] != n:
            x = jnp.concatenate([x, jnp.zeros((n - x.shape[0], x.shape[1]), x.dtype)], axis=0)
        return x.astype(BF16)

    def head_body(h, carry):
        off = pl.multiple_of(h * LANES, LANES)
        q_raw = z_ref[:, pl.ds(off, LANES)]
        f_raw = z_ref[:, pl.ds(hk + off, LANES)]
        v = z_ref[:, pl.ds(2 * hk + off, LANES)]
        g_raw = z_ref[:, pl.ds(3 * hk + off, LANES)]
        lb = lb_ref[:, pl.ds(off, LANES)]
        forget = lb + (1.0 - lb) * jax.nn.sigmoid(f_raw)
        logf = jnp.log(forget)
        kf = 1.0 - forget
        qf = _silu(q_raw)

        hi = logf.astype(BF16).astype(F32)
        r1 = logf - hi
        mid = r1.astype(BF16).astype(F32)
        lo = r1 - mid
        parts = pad_rows(jnp.concatenate([hi, mid, lo], axis=1), LANES)
        cs = jnp.dot(tri_ref[...], parts, preferred_element_type=F32)
        b = (cs[:chunk, :LANES] + cs[:chunk, LANES:2 * LANES]) + cs[:chunk, 2 * LANES:]
        b_scr[...] = b

        def bref_for(m):
            half = m // 2
            pieces = []
            for g in range(chunk // SUBLANES):
                base = g * SUBLANES
                if m >= SUBLANES:
                    r = (base // m) * m + half - 1
                    piece = jnp.broadcast_to(b_scr[r:r + 1, :], (SUBLANES, LANES))
                else:
                    piece = jnp.broadcast_to(b_scr[base + half - 1:base + half, :], (SUBLANES, LANES))
                    for blk in range(1, SUBLANES // m):
                        r = base + blk * m + half - 1
                        piece = jnp.where(sub >= blk * m,
                                          jnp.broadcast_to(b_scr[r:r + 1, :], (SUBLANES, LANES)), piece)
                pieces.append(piece)
            return pieces[0] if len(pieces) == 1 else jnp.concatenate(pieces, axis=0)

        qf_b = pad_rows(qf, mrows)
        kf_b = pad_rows(kf, LANES)
        att = _nt_dot(qf_b, kf_b) * bmask_ref[len(levels)]
        for li, m in enumerate(levels):
            upper = (row & (m - 1)) >= (m // 2)
            e = jnp.exp(-jnp.abs(b - bref_for(m)))
            w = jnp.where(upper, qf, kf) * e
            qm = pad_rows(jnp.where(upper, w, 0.0), mrows)
            km = pad_rows(jnp.where(upper, 0.0, w), LANES)
            s = _nt_dot(qm, km)
            att = att + (s if m == chunk and chunk == LANES else s * bmask_ref[li])

        v_b = pad_rows(v, LANES)
        o_intra = jnp.dot(att.astype(BF16), v_b, preferred_element_type=F32)
        s_old = s_scr[h]
        qs = pad_rows(qf * jnp.exp(b), mrows)
        o_inter = jnp.dot(qs, s_old.astype(BF16), preferred_element_type=F32)
        o = (o_intra + o_inter)[:chunk]

        b_last = b_scr[chunk - 1:chunk, :]
        kd = kf * jnp.exp(b_last - b)
        if chunk != LANES:
            kd = jnp.concatenate([kd, jnp.zeros((LANES - chunk, LANES), F32)], axis=0)
        upd = jnp.dot(kd.T.astype(BF16), v_b, preferred_element_type=F32)
        decay_col = jnp.broadcast_to(jnp.exp(b_last), (LANES, LANES)).T
        s_scr[h] = decay_col * s_old + upd

        on = _rms(o, og) * _silu(g_raw)
        o_ref[:, pl.ds(off, LANES)] = on.astype(o_ref.dtype)
        return carry

    lax.fori_loop(0, A_HEADS, head_body, 0)

    @pl.when(c_idx == pl.num_programs(1) - 1)
    def _():
        sfin_ref[0] = s_scr[...]


def _scan_consts(chunk):
    levels = _scan_levels(chunk)
    mrows = max(chunk, 16)
    r = np.arange(LANES)
    tri = (r[None, :] <= r[:, None]).astype(np.float32)[:mrows]
    masks = []
    for m in levels:
        masks.append((r[:mrows, None] // m == r[None, :] // m).astype(np.float32))
    masks.append((r[:mrows, None] == r[None, :]).astype(np.float32))
    return jnp.asarray(tri, BF16), jnp.asarray(np.stack(masks), F32)


def hgrn2_scan(z, s0, lb, o_gain, *, row_off, n_seq, seq_len, chunk):
    n_chunks = seq_len // chunk
    blk_off = row_off // chunk
    tri, bmask = _scan_consts(chunk)
    mrows = max(chunk, 16)
    shared_s0 = s0.shape[0] == 1
    hv = A_HEADS * A_DV
    o, sfin = pl.pallas_call(
        functools.partial(_scan_kernel, chunk=chunk),
        out_shape=(jax.ShapeDtypeStruct((n_seq * seq_len, hv), BF16 if chunk % 16 == 0 else F32),
                   jax.ShapeDtypeStruct((n_seq, A_HEADS, A_DK, A_DV), F32)),
        grid=(n_seq, n_chunks),
        in_specs=[pl.BlockSpec((chunk, 4 * hv), lambda s, c: (blk_off + s * n_chunks + c, 0)),
                  pl.BlockSpec((1, A_HEADS, A_DK, A_DV), (lambda s, c: (0, 0, 0, 0)) if shared_s0
                               else (lambda s, c: (s, 0, 0, 0))),
                  pl.BlockSpec((1, hv), lambda s, c: (0, 0)),
                  pl.BlockSpec((1, A_DV), lambda s, c: (0, 0)),
                  pl.BlockSpec((mrows, LANES), lambda s, c: (0, 0)),
                  pl.BlockSpec(bmask.shape, lambda s, c: (0, 0, 0))],
        out_specs=(pl.BlockSpec((chunk, hv), lambda s, c: (s * n_chunks + c, 0)),
                   pl.BlockSpec((1, A_HEADS, A_DK, A_DV), lambda s, c: (s, 0, 0, 0))),
        scratch_shapes=[pltpu.VMEM((A_HEADS, A_DK, A_DV), F32), pltpu.VMEM((chunk, LANES), F32)],
        compiler_params=_cparams(("parallel", "arbitrary")),
        name=f"hgrn2_scan_c{chunk}",
    )(z, s0, lb.reshape(1, hv), o_gain.reshape(1, A_DV), tri, bmask)
    return o, sfin


def _attn_kernel(prev_ref, own_ref, jmin_ref, sink_ref, q_ref, kp_ref, ko_ref, vp_ref, vo_ref, o_ref):
    n = pl.program_id(0)
    jmin = jmin_ref[n]
    scale = HEAD_DIM ** -0.5
    blk = ATT_BLOCK
    t_i = lax.broadcasted_iota(jnp.int32, (blk, 4 * blk), 0)
    c_i = lax.broadcasted_iota(jnp.int32, (blk, 4 * blk), 1)
    j_i = c_i & (blk - 1)
    own = (c_i & blk) != 0
    mask = (own & (j_i <= t_i)) | (jnp.logical_not(own) & (j_i >= t_i) & (j_i >= jmin))
    lane = lax.broadcasted_iota(jnp.int32, (2 * blk, LANES), 1)
    first_half = lane < HEAD_DIM
    neg = jnp.float32(-jnp.inf)

    for kh in range(KV_HEADS):
        ksl = slice(kh * LANES, (kh + 1) * LANES)
        kk = jnp.concatenate([kp_ref[:, ksl], ko_ref[:, ksl]], axis=0)
        vv = jnp.concatenate([vp_ref[:, ksl], vo_ref[:, ksl]], axis=0)
        zero = jnp.zeros_like(kk)
        k2 = jnp.concatenate([jnp.where(first_half, kk, zero), jnp.where(first_half, zero, kk)], axis=0)
        v2 = jnp.concatenate([jnp.where(first_half, vv, zero), jnp.where(first_half, zero, vv)], axis=0)
        for pp in range(2):
            pair = kh * 2 + pp
            qs = q_ref[:, pair * LANES:(pair + 1) * LANES]
            s2 = _nt_dot(qs, k2) * scale
            s2 = jnp.where(mask, s2, neg)
            outs = []
            for hh in range(2):
                sh = s2[:, hh * 2 * blk:(hh + 1) * 2 * blk]
                sink = sink_ref[2 * pair + hh]
                m = jnp.maximum(jnp.max(sh, axis=-1, keepdims=True), sink)
                p = jnp.exp(sh - m)
                den = jnp.sum(p, axis=-1, keepdims=True) + jnp.exp(sink - m)
                outs.append((p / den).astype(BF16))
            pn = jnp.concatenate(outs, axis=1)
            o_ref[:, pair * LANES:(pair + 1) * LANES] = jnp.dot(
                pn, v2, preferred_element_type=F32).astype(o_ref.dtype)


def window_attention(q_blocks, k_blocks, v_blocks, prev_idx, own_idx, jmin, sinks):
    nb = q_blocks.shape[0] // ATT_BLOCK
    kw = 2 * KV_DIM
    grid_spec = pltpu.PrefetchScalarGridSpec(
        num_scalar_prefetch=4,
        grid=(nb,),
        in_specs=[pl.BlockSpec((ATT_BLOCK, D_MODEL), lambda n, pi, oi, jm, sk: (n, 0)),
                  pl.BlockSpec((ATT_BLOCK, kw), lambda n, pi, oi, jm, sk: (pi[n], 0)),
                  pl.BlockSpec((ATT_BLOCK, kw), lambda n, pi, oi, jm, sk: (oi[n], 0)),
                  pl.BlockSpec((ATT_BLOCK, kw), lambda n, pi, oi, jm, sk: (pi[n], 0)),
                  pl.BlockSpec((ATT_BLOCK, kw), lambda n, pi, oi, jm, sk: (oi[n], 0))],
        out_specs=pl.BlockSpec((ATT_BLOCK, D_MODEL), lambda n, pi, oi, jm, sk: (n, 0)),
    )
    return pl.pallas_call(
        _attn_kernel,
        out_shape=jax.ShapeDtypeStruct((nb * ATT_BLOCK, D_MODEL), BF16),
        grid_spec=grid_spec,
        compiler_params=_cparams(("parallel",)),
        name="window_attention",
    )(prev_idx, own_idx, jmin, sinks, q_blocks, k_blocks, k_blocks, v_blocks, v_blocks)


ROUTE_COLS = 8


def _route_kernel(x_ref, g_ref, wr_ref, br_ref, ltri_ref, xn_ref, rec_ref, cnt_ref, cnt_scr):
    i = pl.program_id(0)

    @pl.when(i == 0)
    def _():
        cnt_scr[...] = jnp.zeros_like(cnt_scr)

    xn = _rms(x_ref[...], g_ref[...])
    xn_ref[...] = xn
    logits = jnp.dot(xn, wr_ref[...], preferred_element_type=F32,
                     precision=lax.Precision.HIGHEST) + br_ref[...]
    rows = logits.shape[0]
    lane = lax.broadcasted_iota(jnp.int32, (rows, LANES), 1).astype(F32)
    neg = jnp.float32(-jnp.inf)
    big = jnp.float32(LANES)

    is_g = (lane >= N_EXPERTS) & (lane < N_EXPERTS + N_GROUPS)
    gl = jnp.where(is_g, logits, neg)
    gmax = jnp.max(gl, axis=-1, keepdims=True)
    gsel = jnp.min(jnp.where(gl == gmax, lane, big), axis=-1, keepdims=True) - N_EXPERTS
    gden = jnp.sum(jnp.where(is_g, jnp.exp(gl - gmax), 0.0), axis=-1, keepdims=True)
    gw = 1.0 / gden

    in_grp = (lane >= gsel * EXPERTS_PER_GROUP) & (lane < (gsel + 1) * EXPERTS_PER_GROUP)
    el = jnp.where(in_grp, logits, neg)
    t1 = jnp.max(el, axis=-1, keepdims=True)
    e1 = jnp.min(jnp.where(el == t1, lane, big), axis=-1, keepdims=True)
    el2 = jnp.where(lane == e1, neg, el)
    t2 = jnp.max(el2, axis=-1, keepdims=True)
    e2 = jnp.min(jnp.where(el2 == t2, lane, big), axis=-1, keepdims=True)
    x2 = jnp.exp(t2 - t1)
    w1 = gw / (1.0 + x2)
    w2 = gw * x2 / (1.0 + x2)

    oh1 = (lane == e1).astype(F32)
    oh2 = (lane == e2).astype(F32)
    oh = oh1 + oh2
    before = jnp.dot(ltri_ref[...], oh.astype(BF16), preferred_element_type=F32)
    base = cnt_scr[...] + before
    r1 = jnp.sum(base * oh1, axis=-1, keepdims=True)
    r2 = jnp.sum(base * oh2, axis=-1, keepdims=True)
    cnt_scr[...] = cnt_scr[...] + jnp.sum(oh, axis=0, keepdims=True)

    rec = jnp.where(lane == 0, e1,
          jnp.where(lane == 1, e2,
          jnp.where(lane == 2, r1,
          jnp.where(lane == 3, r2,
          jnp.where(lane == 4, w1,
          jnp.where(lane == 5, w2, 0.0))))))
    rec_ref[...] = rec
    cnt_ref[...] = cnt_scr[...]


def moe_route(h, gain, w_router, b_router, ltri):
    t, d = h.shape
    row = lambda i: (i, 0)
    fix = lambda i: (0, 0)
    return pl.pallas_call(
        _route_kernel,
        out_shape=(jax.ShapeDtypeStruct((t, d), F32), jax.ShapeDtypeStruct((t, LANES), F32),
                   jax.ShapeDtypeStruct((1, LANES), F32)),
        grid=(t // ROW_TILE,),
        in_specs=[pl.BlockSpec((ROW_TILE, d), row), pl.BlockSpec((1, d), fix),
                  pl.BlockSpec((d, LANES), fix), pl.BlockSpec((1, LANES), fix),
                  pl.BlockSpec((ROW_TILE, ROW_TILE), fix)],
        out_specs=(pl.BlockSpec((ROW_TILE, d), row), pl.BlockSpec((ROW_TILE, LANES), row),
                   pl.BlockSpec((1, LANES), fix)),
        scratch_shapes=[pltpu.VMEM((1, LANES), F32)],
        compiler_params=_cparams(("arbitrary",)),
        name="moe_route",
    )(h, gain.reshape(1, d), w_router, b_router, ltri)


def _row_copy(src, src_row, dst, dst_row, sem):
    return pltpu.make_async_copy(src.at[pl.ds(src_row, 1)], dst.at[pl.ds(dst_row, 1)], sem)


def _dispatch_kernel(dest_ref, xn_ref, xs_ref, sem):
    rows = xn_ref.shape[0]

    def issue(r, c):
        _row_copy(xn_ref, r, xs_ref, dest_ref[0, 0, r], sem).start()
        _row_copy(xn_ref, r, xs_ref, dest_ref[0, 0, rows + r], sem).start()
        return c

    lax.fori_loop(0, rows, issue, 0)
    for _ in range(2):
        pltpu.make_async_copy(xn_ref, xs_ref.at[pl.ds(0, rows)], sem).wait()


def moe_dispatch(xn, dest3, n_slots):
    t, d = xn.shape
    return pl.pallas_call(
        _dispatch_kernel,
        out_shape=jax.ShapeDtypeStruct((n_slots, d), F32),
        grid=(t // ROW_TILE,),
        in_specs=[pl.BlockSpec((1, 1, 2 * ROW_TILE), lambda i: (i, 0, 0), memory_space=pltpu.SMEM),
                  pl.BlockSpec((ROW_TILE, d), lambda i: (i, 0))],
        out_specs=pl.BlockSpec(memory_space=pl.ANY),
        scratch_shapes=[pltpu.SemaphoreType.DMA],
        compiler_params=_cparams(("arbitrary",)),
        name="moe_dispatch",
    )(dest3, xn)


def _ffn_kernel(wblk_ref, we_ref, wlo_ref, whi_ref, xs_ref, w13_ref, w2_ref, ys_ref, w13b, w2b):
    w = pl.program_id(0)
    prev = jnp.maximum(w - 1, 0)
    first_visit = (w == 0) | (wblk_ref[w] != wblk_ref[prev])
    lo = wlo_ref[w]
    hi = whi_ref[w]

    @pl.when(hi > lo)
    def _():
        @pl.when((w == 0) | (we_ref[w] != we_ref[prev]))
        def _():
            w13b[...] = w13_ref[...].astype(BF16)
            w2b[...] = w2_ref[...].astype(BF16)

        au = jnp.dot(xs_ref[...].astype(BF16), w13b[...], preferred_element_type=F32)
        hmid = _silu(au[:, :D_EXPERT]) * au[:, D_EXPERT:]
        y = jnp.dot(hmid.astype(BF16), w2b[...], preferred_element_type=F32)
        row = lax.broadcasted_iota(jnp.int32, y.shape, 0)
        mine = (row >= lo) & (row < hi)

        @pl.when(first_visit)
        def _():
            ys_ref[...] = jnp.where(mine, y, 0.0)

        @pl.when(jnp.logical_not(first_visit))
        def _():
            ys_ref[...] = jnp.where(mine, y, ys_ref[...])


def moe_ffn(xs, work, w13_all, w2_all, layer):
    n_slots, d = xs.shape
    n_work = work[0].shape[0]
    xmap = lambda w, wb, we, wlo, whi: (wb[w], 0)
    w_map = lambda w, wb, we, wlo, whi: (layer, we[w], 0, 0)
    grid_spec = pltpu.PrefetchScalarGridSpec(
        num_scalar_prefetch=4,
        grid=(n_work,),
        in_specs=[pl.BlockSpec((EXPERT_BLOCK, d), xmap),
                  pl.BlockSpec((None, None, d, 2 * D_EXPERT), w_map),
                  pl.BlockSpec((None, None, D_EXPERT, d), w_map)],
        out_specs=pl.BlockSpec((EXPERT_BLOCK, d), xmap),
        scratch_shapes=[pltpu.VMEM((d, 2 * D_EXPERT), BF16), pltpu.VMEM((D_EXPERT, d), BF16)],
    )
    return pl.pallas_call(
        _ffn_kernel,
        out_shape=jax.ShapeDtypeStruct((n_slots, d), F32),
        grid_spec=grid_spec,
        compiler_params=_cparams(("arbitrary",)),
        name="moe_ffn",
    )(*work, xs, w13_all, w2_all)


def _ffn_work_items(cnt):
    n_slots = 2 * T_ALL
    n_blocks = n_slots // EXPERT_BLOCK
    n_work = n_blocks + N_EXPERTS - 1
    end = jnp.cumsum(cnt)
    start = end - cnt
    first_blk = start // EXPERT_BLOCK
    last_blk = jnp.maximum(end - 1, start) // EXPERT_BLOCK
    n_items = jnp.where(cnt > 0, last_blk - first_blk + 1, 0)
    item_end = jnp.cumsum(n_items)
    item_start = item_end - n_items
    w = jnp.arange(n_work, dtype=jnp.int32)
    used = w < item_end[-1]
    e = jnp.searchsorted(item_end, jnp.minimum(w, item_end[-1] - 1), side='right').astype(jnp.int32)
    blk = jnp.where(used, first_blk[e] + (w - item_start[e]), n_blocks - 1).astype(jnp.int32)
    lo = jnp.maximum(start[e], blk * EXPERT_BLOCK) - blk * EXPERT_BLOCK
    hi = jnp.minimum(end[e], (blk + 1) * EXPERT_BLOCK) - blk * EXPERT_BLOCK
    lo = jnp.where(used, lo, 0).astype(jnp.int32)
    hi = jnp.where(used, hi, 0).astype(jnp.int32)
    return start, (blk, e, lo, hi)


def _combine_kernel(dest_ref, h_ref, rec_ref, ys_ref, o_ref, g1, g2, sem):
    rows = h_ref.shape[0]

    def issue(r, c):
        _row_copy(ys_ref, dest_ref[0, 0, r], g1, r, sem).start()
        _row_copy(ys_ref, dest_ref[0, 0, rows + r], g2, r, sem).start()
        return c

    lax.fori_loop(0, rows, issue, 0)
    for buf in (g1, g2):
        pltpu.make_async_copy(ys_ref.at[pl.ds(0, rows)], buf, sem).wait()
    rec = rec_ref[...]
    o_ref[...] = h_ref[...] + rec[:, 4:5] * g1[...] + rec[:, 5:6] * g2[...]


def moe_combine(h, rec, ys, dest3):
    t, d = h.shape
    row = lambda i: (i, 0)
    return pl.pallas_call(
        _combine_kernel,
        out_shape=jax.ShapeDtypeStruct((t, d), F32),
        grid=(t // ROW_TILE,),
        in_specs=[pl.BlockSpec((1, 1, 2 * ROW_TILE), lambda i: (i, 0, 0), memory_space=pltpu.SMEM),
                  pl.BlockSpec((ROW_TILE, d), row), pl.BlockSpec((ROW_TILE, LANES), row),
                  pl.BlockSpec(memory_space=pl.ANY)],
        out_specs=pl.BlockSpec((ROW_TILE, d), row),
        scratch_shapes=[pltpu.VMEM((ROW_TILE, d), F32), pltpu.VMEM((ROW_TILE, d), F32),
                        pltpu.SemaphoreType.DMA],
        compiler_params=_cparams(("arbitrary",)),
        name="moe_combine",
    )(dest3, h, rec, ys)


def hier_moe_layer(h, layer, gain, w_group, b_group, w_expert, b_expert, w13_all, w2_all, ltri):
    t = h.shape[0]
    pad = LANES - N_EXPERTS - N_GROUPS
    w_router = jnp.concatenate([w_expert, w_group, jnp.zeros((D_MODEL, pad), F32)], axis=1)
    b_router = jnp.concatenate([b_expert, b_group, jnp.zeros((pad,), F32)]).reshape(1, LANES)
    xn, rec, counts = moe_route(h, gain, w_router, b_router, ltri)

    cnt = counts[0, :N_EXPERTS].astype(jnp.int32)
    start, work = _ffn_work_items(cnt)
    e12 = rec[:, 0:2].astype(jnp.int32)
    r12 = rec[:, 2:4].astype(jnp.int32)
    dest = start[e12] + r12
    dest3 = dest.reshape(t // ROW_TILE, ROW_TILE, 2).transpose(0, 2, 1).reshape(t // ROW_TILE, 1, 2 * ROW_TILE)

    xs = moe_dispatch(xn, dest3, 2 * t)
    ys = moe_ffn(xs, work, w13_all, w2_all, layer)
    return moe_combine(h, rec, ys, dest3)


def _rope_tables(pos):
    half = ROPE_DIM // 2
    inv = jnp.exp(-math.log(ROPE_THETA) * jnp.arange(half, dtype=F32) * (2.0 / ROPE_DIM))
    ang = pos.astype(F32)[:, None] * inv[None, :]
    cos, sin = jnp.cos(ang), jnp.sin(ang)
    t = pos.shape[0]
    ones = jnp.ones((t, HEAD_DIM - ROPE_DIM), F32)
    zeros = jnp.zeros((t, HEAD_DIM - ROPE_DIM), F32)
    z8 = jnp.zeros((t, half), F32)
    cos_h = jnp.concatenate([cos, cos, ones], axis=1)
    sina_h = jnp.concatenate([-sin, z8, zeros], axis=1)
    sinb_h = jnp.concatenate([z8, sin, zeros], axis=1)
    two = lambda a: jnp.concatenate([a, a], axis=1)
    return two(cos_h), two(sina_h), two(sinb_h)


def _dup_heads(a):
    rows = a.shape[0]
    a4 = a.reshape(rows, KV_HEADS, 1, HEAD_DIM)
    return jnp.broadcast_to(a4, (rows, KV_HEADS, 2, HEAD_DIM)).reshape(rows, 2 * KV_DIM).astype(BF16)


def kernel(x_prompt, x_sample, state_hgrn, cache_k_win, cache_v_win, meta_tokens, a_norm, a_w_in, a_lower_logits, a_out_norm, a_w_out, kv_norm, kv_w, k_norm, b_norm, b_wq, b_q_norm, b_sinks, b_w_out, moe_norm, moe_w_group, moe_b_group, moe_w_expert, moe_b_expert, moe_w13, moe_w2):
    h = jnp.concatenate([x_prompt.reshape(T_PROMPT, D_MODEL), x_sample.reshape(T_SAMPLE, D_MODEL),
                         meta_tokens.astype(F32), jnp.zeros((T_ALL - T_REAL, D_MODEL), F32)], axis=0)
    pos = jnp.concatenate([jnp.tile(N_META + jnp.arange(SEQ, dtype=jnp.int32), BATCH),
                           jnp.tile(PAST_LEN + jnp.arange(DEC_SEQ, dtype=jnp.int32), DEC_BATCH),
                           jnp.arange(N_META, dtype=jnp.int32),
                           jnp.zeros((T_ALL - T_REAL,), jnp.int32)])
    cos_t, sina_t, sinb_t = _rope_tables(pos)
    r256 = np.arange(256)
    hmean = jnp.asarray((r256[:, None] // HEAD_DIM == r256[None, :] // HEAD_DIM).astype(np.float32) / HEAD_DIM, BF16)
    ltri = jnp.asarray((r256[None, :] < r256[:, None]).astype(np.float32), BF16)
    lower = jnp.cumsum(jax.nn.softmax(a_lower_logits.astype(F32), axis=0), axis=0)

    moe = functools.partial(hier_moe_layer, w13_all=moe_w13, w2_all=moe_w2, ltri=ltri)

    z = norm_matmul(h, a_norm[0], a_w_in[0].astype(BF16), F32)
    zero_state = jnp.zeros((1, A_HEADS, A_DK, A_DV), F32)
    o_meta, s_meta = hgrn2_scan(z, zero_state, lower[0], a_out_norm[0],
                                row_off=OFF_META, n_seq=1, seq_len=N_META, chunk=N_META)
    o_prompt, s_prompt = hgrn2_scan(z, s_meta, lower[0], a_out_norm[0],
                                    row_off=0, n_seq=BATCH, seq_len=SEQ, chunk=SCAN_CHUNK)
    o_sample, s_sample = hgrn2_scan(z, state_hgrn[0].astype(F32), lower[0], a_out_norm[0],
                                    row_off=OFF_SAMPLE, n_seq=DEC_BATCH, seq_len=DEC_SEQ, chunk=DEC_SEQ)
    o_all = jnp.concatenate([o_prompt, o_sample.astype(BF16), o_meta,
                             jnp.zeros((T_ALL - T_REAL, D_MODEL), BF16)], axis=0)
    h = matmul_residual(o_all, a_w_out[0].astype(BF16), h)
    h = moe(h, 0, moe_norm[0], moe_w_group[0], moe_b_group[0], moe_w_expert[0], moe_b_expert[0])

    k_all, v_all = kv_project(h, kv_norm, kv_w.astype(BF16), hmean, jnp.tile(k_norm, KV_HEADS).reshape(1, KV_DIM),
                              cos_t, sina_t, sinb_t)

    q_all = q_project(h, b_norm[0], b_wq[0].astype(BF16), hmean, jnp.tile(b_q_norm[0], Q_HEADS).reshape(1, D_MODEL),
                      cos_t, sina_t, sinb_t)
    nbp = SEQ // ATT_BLOCK

    def key_blocks(a, cache):
        ap = a[:T_PROMPT].reshape(BATCH, SEQ, KV_DIM)
        am = jnp.broadcast_to(a[OFF_META:OFF_META + N_META][None], (BATCH, N_META, KV_DIM))
        zp = jnp.zeros((BATCH, ATT_BLOCK - N_META, KV_DIM), F32)
        prompt_part = jnp.concatenate([zp, am, ap], axis=1).reshape(BATCH * (nbp + 1) * ATT_BLOCK, KV_DIM)
        a_s = a[OFF_SAMPLE:OFF_SAMPLE + T_SAMPLE].reshape(DEC_BATCH, DEC_SEQ, KV_DIM)
        zs = jnp.zeros((DEC_BATCH, ATT_BLOCK - DEC_SEQ, KV_DIM), F32)
        sample_part = jnp.concatenate([cache.reshape(DEC_BATCH, WINDOW, KV_DIM).astype(F32), a_s, zs],
                                      axis=1).reshape(DEC_BATCH * 2 * ATT_BLOCK, KV_DIM)
        return _dup_heads(jnp.concatenate([prompt_part, sample_part], axis=0))

    k_blocks = key_blocks(k_all, cache_k_win)
    v_blocks = key_blocks(v_all, cache_v_win)
    q_s = q_all[OFF_SAMPLE:OFF_SAMPLE + T_SAMPLE].reshape(DEC_BATCH, DEC_SEQ, D_MODEL)
    q_s = jnp.concatenate([q_s, jnp.zeros((DEC_BATCH, ATT_BLOCK - DEC_SEQ, D_MODEL), BF16)], axis=1)
    q_blocks = jnp.concatenate([q_all[:T_PROMPT], q_s.reshape(DEC_BATCH * ATT_BLOCK, D_MODEL)], axis=0)
    bi = np.arange(BATCH * nbp)
    prev_p = (bi // nbp) * (nbp + 1) + bi % nbp
    si = np.arange(DEC_BATCH)
    prev_s = BATCH * (nbp + 1) + 2 * si
    prev_idx = jnp.asarray(np.concatenate([prev_p, prev_s]), jnp.int32)
    own_idx = prev_idx + 1
    jmin = jnp.asarray(np.concatenate([np.where(bi % nbp == 0, ATT_BLOCK - N_META, 0),
                                       np.zeros(DEC_BATCH, np.int64)]), jnp.int32)
    att = window_attention(q_blocks, k_blocks, v_blocks, prev_idx, own_idx, jmin, b_sinks[0].astype(F32))
    att_s = att[T_PROMPT:].reshape(DEC_BATCH, ATT_BLOCK, D_MODEL)[:, :DEC_SEQ].reshape(T_SAMPLE, D_MODEL)
    att_all = jnp.concatenate([att[:T_PROMPT], att_s, jnp.zeros((T_ALL - OFF_META, D_MODEL), BF16)], axis=0)
    h = matmul_residual(att_all, b_w_out[0].astype(BF16), h)
    h = moe(h, 1, moe_norm[1], moe_w_group[1], moe_b_group[1], moe_w_expert[1], moe_b_expert[1])

    y_prompt = h[:T_PROMPT].reshape(BATCH, SEQ, D_MODEL)
    y_sample = h[OFF_SAMPLE:OFF_SAMPLE + T_SAMPLE].reshape(DEC_BATCH, DEC_SEQ, D_MODEL)
    kp = k_all[:T_PROMPT].reshape(BATCH, SEQ, KV_HEADS, HEAD_DIM)[:, -WINDOW:]
    vp = v_all[:T_PROMPT].reshape(BATCH, SEQ, KV_HEADS, HEAD_DIM)[:, -WINDOW:]
    ks = k_all[OFF_SAMPLE:OFF_SAMPLE + T_SAMPLE].reshape(DEC_BATCH, DEC_SEQ, KV_HEADS, HEAD_DIM)
    vs = v_all[OFF_SAMPLE:OFF_SAMPLE + T_SAMPLE].reshape(DEC_BATCH, DEC_SEQ, KV_HEADS, HEAD_DIM)
    k_win_s = jnp.concatenate([cache_k_win, ks], axis=1)[:, -WINDOW:]
    v_win_s = jnp.concatenate([cache_v_win, vs], axis=1)[:, -WINDOW:]
    return (y_prompt, y_sample, s_prompt[None], s_sample[None], kp, vp, k_win_s, v_win_s)
```

```python
import functools
import math

import numpy as np
import jax
import jax.numpy as jnp
from jax import lax
from jax.experimental import pallas as pl
from jax.experimental.pallas import tpu as pltpu

F32 = jnp.float32
BF16 = jnp.bfloat16

D_MODEL = 1024
BATCH = 4
SEQ = 4096
DEC_BATCH = 128
DEC_SEQ = 8
PAST_LEN = 8192
N_META = 16
A_HEADS = 8
A_DK = 128
A_DV = 128
Q_HEADS = 16
KV_HEADS = 4
HEAD_DIM = 64
KV_DIM = KV_HEADS * HEAD_DIM
WINDOW = 128
ROPE_DIM = 16
ROPE_THETA = 500000.0
N_GROUPS = 4
EXPERTS_PER_GROUP = 8
N_EXPERTS = 32
D_EXPERT = 512
RMS_EPS = 1e-6

LANES = 128
SUBLANES = 8
VMEM_LIMIT = 56 * 1024 * 1024

ROW_TILE = 256
T_PROMPT = BATCH * SEQ
T_SAMPLE = DEC_BATCH * DEC_SEQ
OFF_SAMPLE = T_PROMPT
OFF_META = T_PROMPT + T_SAMPLE
T_REAL = OFF_META + N_META
T_ALL = -(-T_REAL // ROW_TILE) * ROW_TILE
N_TILES = T_ALL // ROW_TILE

SCAN_CHUNK = 128
ATT_BLOCK = 128
EXPERT_BLOCK = 256


def _cparams(sem):
    return pltpu.CompilerParams(dimension_semantics=sem, vmem_limit_bytes=VMEM_LIMIT)


def _nt_dot(a, b):
    return lax.dot_general(a, b, (((1,), (1,)), ((), ())), preferred_element_type=F32)


def _rms(x, gain):
    ms = jnp.mean(x * x, axis=-1, keepdims=True)
    return x * lax.rsqrt(ms + RMS_EPS) * gain


def _silu(x):
    return x * jax.nn.sigmoid(x)


def _norm_matmul_kernel(x_ref, g_ref, w_ref, o_ref):
    xn = _rms(x_ref[...], g_ref[...])
    o_ref[...] = jnp.dot(xn.astype(BF16), w_ref[...], preferred_element_type=F32).astype(o_ref.dtype)


def norm_matmul(x, gain, w_bf16, out_dtype):
    t, d = x.shape
    n = w_bf16.shape[1]
    return pl.pallas_call(
        _norm_matmul_kernel,
        out_shape=jax.ShapeDtypeStruct((t, n), out_dtype),
        grid=(t // ROW_TILE,),
        in_specs=[pl.BlockSpec((ROW_TILE, d), lambda i: (i, 0)),
                  pl.BlockSpec((1, d), lambda i: (0, 0)),
                  pl.BlockSpec((d, n), lambda i: (0, 0))],
        out_specs=pl.BlockSpec((ROW_TILE, n), lambda i: (i, 0)),
        compiler_params=_cparams(("parallel",)),
        name="norm_matmul",
    )(x, gain.reshape(1, d), w_bf16)


def _head_norm_rope(y, hmean_ref, hgain, cos_t, sina_t, sinb_t):
    rows, width = y.shape
    sq = (y * y).astype(BF16)
    parts = []
    for s in range(width // 256):
        parts.append(jnp.dot(sq[:, s * 256:(s + 1) * 256], hmean_ref[...], preferred_element_type=F32))
    ms = parts[0] if len(parts) == 1 else jnp.concatenate(parts, axis=1)
    yn = y * lax.rsqrt(ms + RMS_EPS) * hgain
    reps = width // LANES
    cos_w = jnp.concatenate([cos_t] * reps, axis=1)
    sina_w = jnp.concatenate([sina_t] * reps, axis=1)
    sinb_w = jnp.concatenate([sinb_t] * reps, axis=1)
    half = ROPE_DIM // 2
    nxt = pltpu.roll(yn, width - half, 1)
    prv = pltpu.roll(yn, half, 1)
    return yn * cos_w + nxt * sina_w + prv * sinb_w


def _kv_kernel(x_ref, g_ref, w_ref, hmean_ref, hg_ref, cos_ref, sina_ref, sinb_ref, k_ref, v_ref):
    xn = _rms(x_ref[...], g_ref[...])
    z = jnp.dot(xn.astype(BF16), w_ref[...], preferred_element_type=F32)
    k = _head_norm_rope(z[:, :KV_DIM], hmean_ref, hg_ref[...], cos_ref[...], sina_ref[...], sinb_ref[...])
    k_ref[...] = k
    v_ref[...] = z[:, KV_DIM:]


def _rope_tile(i):
    tiles_per_seq = SEQ // ROW_TILE
    n_prompt_tiles = T_PROMPT // ROW_TILE
    n_sample_tiles = T_SAMPLE // ROW_TILE
    return (jnp.where(i < n_prompt_tiles, i % tiles_per_seq,
                      jnp.where(i < n_prompt_tiles + n_sample_tiles, tiles_per_seq, tiles_per_seq + 1)), 0)


def kv_project(x, gain, w_bf16, hmean, hgain_w, cos_t, sina_t, sinb_t):
    t, d = x.shape
    row = lambda i: (i, 0)
    fix = lambda i: (0, 0)
    return pl.pallas_call(
        _kv_kernel,
        out_shape=(jax.ShapeDtypeStruct((t, KV_DIM), F32), jax.ShapeDtypeStruct((t, KV_DIM), F32)),
        grid=(t // ROW_TILE,),
        in_specs=[pl.BlockSpec((ROW_TILE, d), row), pl.BlockSpec((1, d), fix),
                  pl.BlockSpec((d, 2 * KV_DIM), fix), pl.BlockSpec((256, 256), fix),
                  pl.BlockSpec((1, KV_DIM), fix),
                  pl.BlockSpec((ROW_TILE, LANES), _rope_tile), pl.BlockSpec((ROW_TILE, LANES), _rope_tile),
                  pl.BlockSpec((ROW_TILE, LANES), _rope_tile)],
        out_specs=(pl.BlockSpec((ROW_TILE, KV_DIM), row), pl.BlockSpec((ROW_TILE, KV_DIM), row)),
        compiler_params=_cparams(("parallel",)),
        name="kv_project",
    )(x, gain.reshape(1, d), w_bf16, hmean, hgain_w, cos_t, sina_t, sinb_t)


def _q_kernel(x_ref, g_ref, w_ref, hmean_ref, hg_ref, cos_ref, sina_ref, sinb_ref, q_ref):
    xn = _rms(x_ref[...], g_ref[...])
    z = jnp.dot(xn.astype(BF16), w_ref[...], preferred_element_type=F32)
    q = _head_norm_rope(z, hmean_ref, hg_ref[...], cos_ref[...], sina_ref[...], sinb_ref[...])
    q_ref[...] = q.astype(q_ref.dtype)


def q_project(x, gain, w_bf16, hmean, hgain_w, cos_t, sina_t, sinb_t):
    t, d = x.shape
    row = lambda i: (i, 0)
    fix = lambda i: (0, 0)
    return pl.pallas_call(
        _q_kernel,
        out_shape=jax.ShapeDtypeStruct((t, d), BF16),
        grid=(t // ROW_TILE,),
        in_specs=[pl.BlockSpec((ROW_TILE, d), row), pl.BlockSpec((1, d), fix),
                  pl.BlockSpec((d, d), fix), pl.BlockSpec((256, 256), fix),
                  pl.BlockSpec((1, d), fix),
                  pl.BlockSpec((ROW_TILE, LANES), _rope_tile), pl.BlockSpec((ROW_TILE, LANES), _rope_tile),
                  pl.BlockSpec((ROW_TILE, LANES), _rope_tile)],
        out_specs=pl.BlockSpec((ROW_TILE, d), row),
        compiler_params=_cparams(("parallel",)),
        name="q_project",
    )(x, gain.reshape(1, d), w_bf16, hmean, hgain_w, cos_t, sina_t, sinb_t)


def _matmul_residual_kernel(a_ref, w_ref, r_ref, o_ref):
    o_ref[...] = r_ref[...] + jnp.dot(a_ref[...], w_ref[...], preferred_element_type=F32)


def matmul_residual(a_bf16, w_bf16, resid):
    t, k = a_bf16.shape
    n = w_bf16.shape[1]
    return pl.pallas_call(
        _matmul_residual_kernel,
        out_shape=jax.ShapeDtypeStruct((t, n), F32),
        grid=(t // ROW_TILE,),
        in_specs=[pl.BlockSpec((ROW_TILE, k), lambda i: (i, 0)),
                  pl.BlockSpec((k, n), lambda i: (0, 0)),
                  pl.BlockSpec((ROW_TILE, n), lambda i: (i, 0))],
        out_specs=pl.BlockSpec((ROW_TILE, n), lambda i: (i, 0)),
        compiler_params=_cparams(("parallel",)),
        name="matmul_residual",
    )(a_bf16, w_bf16, resid)


def _scan_levels(c):
    levels = []
    m = c
    while m >= 2:
        levels.append(m)
        m //= 2
    return levels


def _scan_kernel(z_ref, s0_ref, lb_ref, og_ref, tri_ref, bmask_ref, o_ref, sfin_ref, s_scr, b_scr, *, chunk):
    c_idx = pl.program_id(1)
    levels = _scan_levels(chunk)
    mrows = max(chunk, 16)
    hk = A_HEADS * A_DK

    @pl.when(c_idx == 0)
    def _():
        s_scr[...] = s0_ref[0]

    row = lax.broadcasted_iota(jnp.int32, (chunk, LANES), 0)
    sub = lax.broadcasted_iota(jnp.int32, (SUBLANES, LANES), 0)
    og = og_ref[...]

    def pad_rows(x, n):
        if x.shape[0] != n:
            x = jnp.concatenate([x, jnp.zeros((n - x.shape[0], x.shape[1]), x.dtype)], axis=0)
        return x.astype(BF16)

    def head_body(h, carry):
        off = pl.multiple_of(h * LANES, LANES)
        q_raw = z_ref[:, pl.ds(off, LANES)]
        f_raw = z_ref[:, pl.ds(hk + off, LANES)]
        v = z_ref[:, pl.ds(2 * hk + off, LANES)]
        g_raw = z_ref[:, pl.ds(3 * hk + off, LANES)]
        lb = lb_ref[:, pl.ds(off, LANES)]
        forget = lb + (1.0 - lb) * jax.nn.sigmoid(f_raw)
        logf = jnp.log(forget)
        kf = 1.0 - forget
        qf = _silu(q_raw)

        hi = logf.astype(BF16).astype(F32)
        r1 = logf - hi
        mid = r1.astype(BF16).astype(F32)
        lo = r1 - mid
        parts = pad_rows(jnp.concatenate([hi, mid, lo], axis=1), LANES)
        cs = jnp.dot(tri_ref[...], parts, preferred_element_type=F32)
        b = (cs[:chunk, :LANES] + cs[:chunk, LANES:2 * LANES]) + cs[:chunk, 2 * LANES:]
        b_scr[h] = b
        b_rows = b_scr.at[h]

        def bref_for(m):
            half = m // 2
            pieces = []
            for g in range(chunk // SUBLANES):
                base = g * SUBLANES
                if m >= SUBLANES:
                    r = (base // m) * m + half - 1
                    piece = jnp.broadcast_to(b_rows[r:r + 1, :], (SUBLANES, LANES))
                else:
                    piece = jnp.broadcast_to(b_rows[base + half - 1:base + half, :], (SUBLANES, LANES))
                    for blk in range(1, SUBLANES // m):
                        r = base + blk * m + half - 1
                        piece = jnp.where(sub >= blk * m,
                                          jnp.broadcast_to(b_rows[r:r + 1, :], (SUBLANES, LANES)), piece)
                pieces.append(piece)
            return pieces[0] if len(pieces) == 1 else jnp.concatenate(pieces, axis=0)

        qf_b = pad_rows(qf, mrows)
        kf_b = pad_rows(kf, LANES)
        att = _nt_dot(qf_b, kf_b) * bmask_ref[len(levels)]
        for li, m in enumerate(levels):
            upper = (row & (m - 1)) >= (m // 2)
            e = jnp.exp(-jnp.abs(b - bref_for(m)))
            w = jnp.where(upper, qf, kf) * e
            qm = pad_rows(jnp.where(upper, w, 0.0), mrows)
            km = pad_rows(jnp.where(upper, 0.0, w), LANES)
            s = _nt_dot(qm, km)
            att = att + (s if m == chunk and chunk == LANES else s * bmask_ref[li])

        v_b = pad_rows(v, LANES)
        o_intra = jnp.dot(att.astype(BF16), v_b, preferred_element_type=F32)
        s_old = s_scr[h]
        qs = pad_rows(qf * jnp.exp(b), mrows)
        o_inter = jnp.dot(qs, s_old.astype(BF16), preferred_element_type=F32)
        o = (o_intra + o_inter)[:chunk]

        b_last = b_rows[chunk - 1:chunk, :]
        kd = kf * jnp.exp(b_last - b)
        if chunk != LANES:
            kd = jnp.concatenate([kd, jnp.zeros((LANES - chunk, LANES), F32)], axis=0)
        upd = jnp.dot(kd.T.astype(BF16), v_b, preferred_element_type=F32)
        decay_col = jnp.broadcast_to(jnp.exp(b_last), (LANES, LANES)).T
        s_scr[h] = decay_col * s_old + upd

        on = _rms(o, og) * _silu(g_raw)
        o_ref[:, pl.ds(off, LANES)] = on.astype(o_ref.dtype)
        return carry

    lax.fori_loop(0, A_HEADS, head_body, 0, unroll=4 if chunk == LANES else A_HEADS)

    @pl.when(c_idx == pl.num_programs(1) - 1)
    def _():
        sfin_ref[0] = s_scr[...]


def _scan_consts(chunk):
    levels = _scan_levels(chunk)
    mrows = max(chunk, 16)
    r = np.arange(LANES)
    tri = (r[None, :] <= r[:, None]).astype(np.float32)[:mrows]
    masks = []
    for m in levels:
        masks.append((r[:mrows, None] // m == r[None, :] // m).astype(np.float32))
    masks.append((r[:mrows, None] == r[None, :]).astype(np.float32))
    return jnp.asarray(tri, BF16), jnp.asarray(np.stack(masks), F32)


def hgrn2_scan(z, s0, lb, o_gain, *, row_off, n_seq, seq_len, chunk):
    n_chunks = seq_len // chunk
    blk_off = row_off // chunk
    tri, bmask = _scan_consts(chunk)
    mrows = max(chunk, 16)
    shared_s0 = s0.shape[0] == 1
    hv = A_HEADS * A_DV
    o, sfin = pl.pallas_call(
        functools.partial(_scan_kernel, chunk=chunk),
        out_shape=(jax.ShapeDtypeStruct((n_seq * seq_len, hv), BF16 if chunk % 16 == 0 else F32),
                   jax.ShapeDtypeStruct((n_seq, A_HEADS, A_DK, A_DV), F32)),
        grid=(n_seq, n_chunks),
        in_specs=[pl.BlockSpec((chunk, 4 * hv), lambda s, c: (blk_off + s * n_chunks + c, 0)),
                  pl.BlockSpec((1, A_HEADS, A_DK, A_DV), (lambda s, c: (0, 0, 0, 0)) if shared_s0
                               else (lambda s, c: (s, 0, 0, 0))),
                  pl.BlockSpec((1, hv), lambda s, c: (0, 0)),
                  pl.BlockSpec((1, A_DV), lambda s, c: (0, 0)),
                  pl.BlockSpec((mrows, LANES), lambda s, c: (0, 0)),
                  pl.BlockSpec(bmask.shape, lambda s, c: (0, 0, 0))],
        out_specs=(pl.BlockSpec((chunk, hv), lambda s, c: (s * n_chunks + c, 0)),
                   pl.BlockSpec((1, A_HEADS, A_DK, A_DV), lambda s, c: (s, 0, 0, 0))),
        scratch_shapes=[pltpu.VMEM((A_HEADS, A_DK, A_DV), F32), pltpu.VMEM((A_HEADS, chunk, LANES), F32)],
        compiler_params=_cparams(("parallel", "arbitrary")),
        name=f"hgrn2_scan_c{chunk}",
    )(z, s0, lb.reshape(1, hv), o_gain.reshape(1, A_DV), tri, bmask)
    return o, sfin


def _attn_kernel(prev_ref, own_ref, jmin_ref, sink_ref, q_ref, kp_ref, ko_ref, vp_ref, vo_ref, o_ref):
    n = pl.program_id(0)
    jmin = jmin_ref[n]
    scale = HEAD_DIM ** -0.5
    blk = ATT_BLOCK
    t_i = lax.broadcasted_iota(jnp.int32, (blk, 4 * blk), 0)
    c_i = lax.broadcasted_iota(jnp.int32, (blk, 4 * blk), 1)
    j_i = c_i & (blk - 1)
    own = (c_i & blk) != 0
    mask = (own & (j_i <= t_i)) | (jnp.logical_not(own) & (j_i >= t_i) & (j_i >= jmin))
    lane = lax.broadcasted_iota(jnp.int32, (2 * blk, LANES), 1)
    first_half = lane < HEAD_DIM
    neg = jnp.float32(-jnp.inf)

    for kh in range(KV_HEADS):
        ksl = slice(kh * LANES, (kh + 1) * LANES)
        kk = jnp.concatenate([kp_ref[:, ksl], ko_ref[:, ksl]], axis=0)
        vv = jnp.concatenate([vp_ref[:, ksl], vo_ref[:, ksl]], axis=0)
        zero = jnp.zeros_like(kk)
        k2 = jnp.concatenate([jnp.where(first_half, kk, zero), jnp.where(first_half, zero, kk)], axis=0)
        v2 = jnp.concatenate([jnp.where(first_half, vv, zero), jnp.where(first_half, zero, vv)], axis=0)
        for pp in range(2):
            pair = kh * 2 + pp
            qs = q_ref[:, pair * LANES:(pair + 1) * LANES]
            s2 = _nt_dot(qs, k2) * scale
            s2 = jnp.where(mask, s2, neg)
            outs = []
            for hh in range(2):
                sh = s2[:, hh * 2 * blk:(hh + 1) * 2 * blk]
                sink = sink_ref[2 * pair + hh]
                m = jnp.maximum(jnp.max(sh, axis=-1, keepdims=True), sink)
                p = jnp.exp(sh - m)
                den = jnp.sum(p, axis=-1, keepdims=True) + jnp.exp(sink - m)
                outs.append((p / den).astype(BF16))
            pn = jnp.concatenate(outs, axis=1)
            o_ref[:, pair * LANES:(pair + 1) * LANES] = jnp.dot(
                pn, v2, preferred_element_type=F32).astype(o_ref.dtype)


def window_attention(q_blocks, k_blocks, v_blocks, prev_idx, own_idx, jmin, sinks):
    nb = q_blocks.shape[0] // ATT_BLOCK
    kw = 2 * KV_DIM
    grid_spec = pltpu.PrefetchScalarGridSpec(
        num_scalar_prefetch=4,
        grid=(nb,),
        in_specs=[pl.BlockSpec((ATT_BLOCK, D_MODEL), lambda n, pi, oi, jm, sk: (n, 0)),
                  pl.BlockSpec((ATT_BLOCK, kw), lambda n, pi, oi, jm, sk: (pi[n], 0)),
                  pl.BlockSpec((ATT_BLOCK, kw), lambda n, pi, oi, jm, sk: (oi[n], 0)),
                  pl.BlockSpec((ATT_BLOCK, kw), lambda n, pi, oi, jm, sk: (pi[n], 0)),
                  pl.BlockSpec((ATT_BLOCK, kw), lambda n, pi, oi, jm, sk: (oi[n], 0))],
        out_specs=pl.BlockSpec((ATT_BLOCK, D_MODEL), lambda n, pi, oi, jm, sk: (n, 0)),
    )
    return pl.pallas_call(
        _attn_kernel,
        out_shape=jax.ShapeDtypeStruct((nb * ATT_BLOCK, D_MODEL), BF16),
        grid_spec=grid_spec,
        compiler_params=_cparams(("parallel",)),
        name="window_attention",
    )(prev_idx, own_idx, jmin, sinks, q_blocks, k_blocks, k_blocks, v_blocks, v_blocks)


ROUTE_COLS = 8


def _route_kernel(x_ref, g_ref, wr_ref, br_ref, ltri_ref, xn_ref, rec_ref, cnt_ref, cnt_scr):
    i = pl.program_id(0)

    @pl.when(i == 0)
    def _():
        cnt_scr[...] = jnp.zeros_like(cnt_scr)

    xn = _rms(x_ref[...], g_ref[...])
    xn_ref[...] = xn
    logits = jnp.dot(xn, wr_ref[...], preferred_element_type=F32,
                     precision=lax.Precision.HIGHEST) + br_ref[...]
    rows = logits.shape[0]
    lane = lax.broadcasted_iota(jnp.int32, (rows, LANES), 1).astype(F32)
    neg = jnp.float32(-jnp.inf)
    big = jnp.float32(LANES)

    is_g = (lane >= N_EXPERTS) & (lane < N_EXPERTS + N_GROUPS)
    gl = jnp.where(is_g, logits, neg)
    gmax = jnp.max(gl, axis=-1, keepdims=True)
    gsel = jnp.min(jnp.where(gl == gmax, lane, big), axis=-1, keepdims=True) - N_EXPERTS
    gden = jnp.sum(jnp.where(is_g, jnp.exp(gl - gmax), 0.0), axis=-1, keepdims=True)
    gw = 1.0 / gden

    in_grp = (lane >= gsel * EXPERTS_PER_GROUP) & (lane < (gsel + 1) * EXPERTS_PER_GROUP)
    el = jnp.where(in_grp, logits, neg)
    t1 = jnp.max(el, axis=-1, keepdims=True)
    e1 = jnp.min(jnp.where(el == t1, lane, big), axis=-1, keepdims=True)
    el2 = jnp.where(lane == e1, neg, el)
    t2 = jnp.max(el2, axis=-1, keepdims=True)
    e2 = jnp.min(jnp.where(el2 == t2, lane, big), axis=-1, keepdims=True)
    x2 = jnp.exp(t2 - t1)
    w1 = gw / (1.0 + x2)
    w2 = gw * x2 / (1.0 + x2)

    oh1 = (lane == e1).astype(F32)
    oh2 = (lane == e2).astype(F32)
    oh = oh1 + oh2
    before = jnp.dot(ltri_ref[...], oh.astype(BF16), preferred_element_type=F32)
    base = cnt_scr[...] + before
    r1 = jnp.sum(base * oh1, axis=-1, keepdims=True)
    r2 = jnp.sum(base * oh2, axis=-1, keepdims=True)
    cnt_scr[...] = cnt_scr[...] + jnp.sum(oh, axis=0, keepdims=True)

    rec = jnp.where(lane == 0, e1,
          jnp.where(lane == 1, e2,
          jnp.where(lane == 2, r1,
          jnp.where(lane == 3, r2,
          jnp.where(lane == 4, w1,
          jnp.where(lane == 5, w2, 0.0))))))
    rec_ref[...] = rec
    cnt_ref[...] = cnt_scr[...]


def moe_route(h, gain, w_router, b_router, ltri):
    t, d = h.shape
    row = lambda i: (i, 0)
    fix = lambda i: (0, 0)
    return pl.pallas_call(
        _route_kernel,
        out_shape=(jax.ShapeDtypeStruct((t, d), F32), jax.ShapeDtypeStruct((t, LANES), F32),
                   jax.ShapeDtypeStruct((1, LANES), F32)),
        grid=(t // ROW_TILE,),
        in_specs=[pl.BlockSpec((ROW_TILE, d), row), pl.BlockSpec((1, d), fix),
                  pl.BlockSpec((d, LANES), fix), pl.BlockSpec((1, LANES), fix),
                  pl.BlockSpec((ROW_TILE, ROW_TILE), fix)],
        out_specs=(pl.BlockSpec((ROW_TILE, d), row), pl.BlockSpec((ROW_TILE, LANES), row),
                   pl.BlockSpec((1, LANES), fix)),
        scratch_shapes=[pltpu.VMEM((1, LANES), F32)],
        compiler_params=_cparams(("arbitrary",)),
        name="moe_route",
    )(h, gain.reshape(1, d), w_router, b_router, ltri)


def _row_copy(src, src_row, dst, dst_row, sem):
    return pltpu.make_async_copy(src.at[pl.ds(src_row, 1)], dst.at[pl.ds(dst_row, 1)], sem)


def _dispatch_kernel(dest_ref, xn_ref, xs_ref, sem):
    rows = xn_ref.shape[0]

    def issue(r, c):
        _row_copy(xn_ref, r, xs_ref, dest_ref[0, 0, r], sem).start()
        _row_copy(xn_ref, r, xs_ref, dest_ref[0, 0, rows + r], sem).start()
        return c

    lax.fori_loop(0, rows, issue, 0, unroll=8)
    for _ in range(2):
        pltpu.make_async_copy(xn_ref, xs_ref.at[pl.ds(0, rows)], sem).wait()


def moe_dispatch(xn, dest3, n_slots):
    t, d = xn.shape
    return pl.pallas_call(
        _dispatch_kernel,
        out_shape=jax.ShapeDtypeStruct((n_slots, d), F32),
        grid=(t // ROW_TILE,),
        in_specs=[pl.BlockSpec((1, 1, 2 * ROW_TILE), lambda i: (i, 0, 0), memory_space=pltpu.SMEM),
                  pl.BlockSpec((ROW_TILE, d), lambda i: (i, 0))],
        out_specs=pl.BlockSpec(memory_space=pl.ANY),
        scratch_shapes=[pltpu.SemaphoreType.DMA],
        compiler_params=_cparams(("arbitrary",)),
        name="moe_dispatch",
    )(dest3, xn)


def _ffn_kernel(wblk_ref, we_ref, wlo_ref, whi_ref, xs_ref, w13_ref, w2_ref, ys_ref, w13b, w2b):
    w = pl.program_id(0)
    prev = jnp.maximum(w - 1, 0)
    first_visit = (w == 0) | (wblk_ref[w] != wblk_ref[prev])
    lo = wlo_ref[w]
    hi = whi_ref[w]

    @pl.when(hi > lo)
    def _():
        @pl.when((w == 0) | (we_ref[w] != we_ref[prev]))
        def _():
            w13b[...] = w13_ref[...].astype(BF16)
            w2b[...] = w2_ref[...].astype(BF16)

        au = jnp.dot(xs_ref[...].astype(BF16), w13b[...], preferred_element_type=F32)
        hmid = _silu(au[:, :D_EXPERT]) * au[:, D_EXPERT:]
        y = jnp.dot(hmid.astype(BF16), w2b[...], preferred_element_type=F32)
        row = lax.broadcasted_iota(jnp.int32, y.shape, 0)
        mine = (row >= lo) & (row < hi)

        @pl.when(first_visit)
        def _():
            ys_ref[...] = jnp.where(mine, y, 0.0)

        @pl.when(jnp.logical_not(first_visit))
        def _():
            ys_ref[...] = jnp.where(mine, y, ys_ref[...])


def moe_ffn(xs, work, w13_all, w2_all, layer):
    n_slots, d = xs.shape
    n_work = work[0].shape[0]
    xmap = lambda w, wb, we, wlo, whi: (wb[w], 0)
    w_map = lambda w, wb, we, wlo, whi: (layer, we[w], 0, 0)
    grid_spec = pltpu.PrefetchScalarGridSpec(
        num_scalar_prefetch=4,
        grid=(n_work,),
        in_specs=[pl.BlockSpec((EXPERT_BLOCK, d), xmap),
                  pl.BlockSpec((None, None, d, 2 * D_EXPERT), w_map),
                  pl.BlockSpec((None, None, D_EXPERT, d), w_map)],
        out_specs=pl.BlockSpec((EXPERT_BLOCK, d), xmap),
        scratch_shapes=[pltpu.VMEM((d, 2 * D_EXPERT), BF16), pltpu.VMEM((D_EXPERT, d), BF16)],
    )
    return pl.pallas_call(
        _ffn_kernel,
        out_shape=jax.ShapeDtypeStruct((n_slots, d), F32),
        grid_spec=grid_spec,
        compiler_params=_cparams(("arbitrary",)),
        name="moe_ffn",
    )(*work, xs, w13_all, w2_all)


def _ffn_work_items(cnt):
    n_slots = 2 * T_ALL
    n_blocks = n_slots // EXPERT_BLOCK
    n_work = n_blocks + N_EXPERTS - 1
    end = jnp.cumsum(cnt)
    start = end - cnt
    first_blk = start // EXPERT_BLOCK
    last_blk = jnp.maximum(end - 1, start) // EXPERT_BLOCK
    n_items = jnp.where(cnt > 0, last_blk - first_blk + 1, 0)
    item_end = jnp.cumsum(n_items)
    item_start = item_end - n_items
    w = jnp.arange(n_work, dtype=jnp.int32)
    used = w < item_end[-1]
    wq = jnp.minimum(w, item_end[-1] - 1)
    e = jnp.sum((item_end[None, :] <= wq[:, None]).astype(jnp.int32), axis=1)
    blk = jnp.where(used, first_blk[e] + (w - item_start[e]), n_blocks - 1).astype(jnp.int32)
    lo = jnp.maximum(start[e], blk * EXPERT_BLOCK) - blk * EXPERT_BLOCK
    hi = jnp.minimum(end[e], (blk + 1) * EXPERT_BLOCK) - blk * EXPERT_BLOCK
    lo = jnp.where(used, lo, 0).astype(jnp.int32)
    hi = jnp.where(used, hi, 0).astype(jnp.int32)
    return start, (blk, e, lo, hi)


def _combine_kernel(dest_ref, h_ref, rec_ref, ys_ref, o_ref, g1, g2, sem):
    rows = h_ref.shape[0]

    def issue(r, c):
        _row_copy(ys_ref, dest_ref[0, 0, r], g1, r, sem).start()
        _row_copy(ys_ref, dest_ref[0, 0, rows + r], g2, r, sem).start()
        return c

    lax.fori_loop(0, rows, issue, 0, unroll=8)
    for buf in (g1, g2):
        pltpu.make_async_copy(ys_ref.at[pl.ds(0, rows)], buf, sem).wait()
    rec = rec_ref[...]
    o_ref[...] = h_ref[...] + rec[:, 4:5] * g1[...] + rec[:, 5:6] * g2[...]


def moe_combine(h, rec, ys, dest3):
    t, d = h.shape
    row = lambda i: (i, 0)
    return pl.pallas_call(
        _combine_kernel,
        out_shape=jax.ShapeDtypeStruct((t, d), F32),
        grid=(t // ROW_TILE,),
        in_specs=[pl.BlockSpec((1, 1, 2 * ROW_TILE), lambda i: (i, 0, 0), memory_space=pltpu.SMEM),
                  pl.BlockSpec((ROW_TILE, d), row), pl.BlockSpec((ROW_TILE, LANES), row),
                  pl.BlockSpec(memory_space=pl.ANY)],
        out_specs=pl.BlockSpec((ROW_TILE, d), row),
        scratch_shapes=[pltpu.VMEM((ROW_TILE, d), F32), pltpu.VMEM((ROW_TILE, d), F32),
                        pltpu.SemaphoreType.DMA],
        compiler_params=_cparams(("arbitrary",)),
        name="moe_combine",
    )(dest3, h, rec, ys)


def hier_moe_layer(h, layer, gain, w_group, b_group, w_expert, b_expert, w13_all, w2_all, ltri):
    t = h.shape[0]
    pad = LANES - N_EXPERTS - N_GROUPS
    w_router = jnp.concatenate([w_expert, w_group, jnp.zeros((D_MODEL, pad), F32)], axis=1)
    b_router = jnp.concatenate([b_expert, b_group, jnp.zeros((pad,), F32)]).reshape(1, LANES)
    xn, rec, counts = moe_route(h, gain, w_router, b_router, ltri)

    cnt = counts[0, :N_EXPERTS].astype(jnp.int32)
    start, work = _ffn_work_items(cnt)
    e12 = rec[:, 0:2].astype(jnp.int32)
    r12 = rec[:, 2:4].astype(jnp.int32)
    sel = e12[:, :, None] == jnp.arange(N_EXPERTS, dtype=jnp.int32)[None, None, :]
    dest = jnp.sum(jnp.where(sel, start[None, None, :], 0), axis=-1) + r12
    dest3 = dest.reshape(t // ROW_TILE, ROW_TILE, 2).transpose(0, 2, 1).reshape(t // ROW_TILE, 1, 2 * ROW_TILE)

    xs = moe_dispatch(xn, dest3, 2 * t)
    ys = moe_ffn(xs, work, w13_all, w2_all, layer)
    return moe_combine(h, rec, ys, dest3)


def _rope_tables(pos):
    half = ROPE_DIM // 2
    inv = jnp.exp(-math.log(ROPE_THETA) * jnp.arange(half, dtype=F32) * (2.0 / ROPE_DIM))
    ang = pos.astype(F32)[:, None] * inv[None, :]
    cos, sin = jnp.cos(ang), jnp.sin(ang)
    t = pos.shape[0]
    ones = jnp.ones((t, HEAD_DIM - ROPE_DIM), F32)
    zeros = jnp.zeros((t, HEAD_DIM - ROPE_DIM), F32)
    z8 = jnp.zeros((t, half), F32)
    cos_h = jnp.concatenate([cos, cos, ones], axis=1)
    sina_h = jnp.concatenate([-sin, z8, zeros], axis=1)
    sinb_h = jnp.concatenate([z8, sin, zeros], axis=1)
    two = lambda a: jnp.concatenate([a, a], axis=1)
    return two(cos_h), two(sina_h), two(sinb_h)


def _dup_heads(a):
    rows = a.shape[0]
    a4 = a.reshape(rows, KV_HEADS, 1, HEAD_DIM)
    return jnp.broadcast_to(a4, (rows, KV_HEADS, 2, HEAD_DIM)).reshape(rows, 2 * KV_DIM).astype(BF16)


def kernel(x_prompt, x_sample, state_hgrn, cache_k_win, cache_v_win, meta_tokens, a_norm, a_w_in, a_lower_logits, a_out_norm, a_w_out, kv_norm, kv_w, k_norm, b_norm, b_wq, b_q_norm, b_sinks, b_w_out, moe_norm, moe_w_group, moe_b_group, moe_w_expert, moe_b_expert, moe_w13, moe_w2):
    h = jnp.concatenate([x_prompt.reshape(T_PROMPT, D_MODEL), x_sample.reshape(T_SAMPLE, D_MODEL),
                         meta_tokens.astype(F32), jnp.zeros((T_ALL - T_REAL, D_MODEL), F32)], axis=0)
    pos = jnp.concatenate([N_META + jnp.arange(SEQ, dtype=jnp.int32),
                           jnp.tile(PAST_LEN + jnp.arange(DEC_SEQ, dtype=jnp.int32), ROW_TILE // DEC_SEQ),
                           jnp.arange(N_META, dtype=jnp.int32),
                           jnp.zeros((ROW_TILE - N_META,), jnp.int32)])
    cos_t, sina_t, sinb_t = _rope_tables(pos)
    r256 = np.arange(256)
    hmean = jnp.asarray((r256[:, None] // HEAD_DIM == r256[None, :] // HEAD_DIM).astype(np.float32) / HEAD_DIM, BF16)
    ltri = jnp.asarray((r256[None, :] < r256[:, None]).astype(np.float32), BF16)
    lower = jnp.cumsum(jax.nn.softmax(a_lower_logits.astype(F32), axis=0), axis=0)

    moe = functools.partial(hier_moe_layer, w13_all=moe_w13, w2_all=moe_w2, ltri=ltri)

    z = norm_matmul(h, a_norm[0], a_w_in[0].astype(BF16), F32)
    zero_state = jnp.zeros((1, A_HEADS, A_DK, A_DV), F32)
    o_meta, s_meta = hgrn2_scan(z, zero_state, lower[0], a_out_norm[0],
                                row_off=OFF_META, n_seq=1, seq_len=N_META, chunk=N_META)
    o_prompt, s_prompt = hgrn2_scan(z, s_meta, lower[0], a_out_norm[0],
                                    row_off=0, n_seq=BATCH, seq_len=SEQ, chunk=SCAN_CHUNK)
    o_sample, s_sample = hgrn2_scan(z, state_hgrn[0].astype(F32), lower[0], a_out_norm[0],
                                    row_off=OFF_SAMPLE, n_seq=DEC_BATCH, seq_len=DEC_SEQ, chunk=DEC_SEQ)
    o_all = jnp.concatenate([o_prompt, o_sample.astype(BF16), o_meta,
                             jnp.zeros((T_ALL - T_REAL, D_MODEL), BF16)], axis=0)
    h = matmul_residual(o_all, a_w_out[0].astype(BF16), h)
    h = moe(h, 0, moe_norm[0], moe_w_group[0], moe_b_group[0], moe_w_expert[0], moe_b_expert[0])

    k_all, v_all = kv_project(h, kv_norm, kv_w.astype(BF16), hmean, jnp.tile(k_norm, KV_HEADS).reshape(1, KV_DIM),
                              cos_t, sina_t, sinb_t)

    q_all = q_project(h, b_norm[0], b_wq[0].astype(BF16), hmean, jnp.tile(b_q_norm[0], Q_HEADS).reshape(1, D_MODEL),
                      cos_t, sina_t, sinb_t)
    nbp = SEQ // ATT_BLOCK

    def key_blocks(a, cache):
        ap = a[:T_PROMPT].reshape(BATCH, SEQ, KV_DIM)
        am = jnp.broadcast_to(a[OFF_META:OFF_META + N_META][None], (BATCH, N_META, KV_DIM))
        zp = jnp.zeros((BATCH, ATT_BLOCK - N_META, KV_DIM), F32)
        prompt_part = jnp.concatenate([zp, am, ap], axis=1).reshape(BATCH * (nbp + 1) * ATT_BLOCK, KV_DIM)
        a_s = a[OFF_SAMPLE:OFF_SAMPLE + T_SAMPLE].reshape(DEC_BATCH, DEC_SEQ, KV_DIM)
        zs = jnp.zeros((DEC_BATCH, ATT_BLOCK - DEC_SEQ, KV_DIM), F32)
        sample_part = jnp.concatenate([cache.reshape(DEC_BATCH, WINDOW, KV_DIM).astype(F32), a_s, zs],
                                      axis=1).reshape(DEC_BATCH * 2 * ATT_BLOCK, KV_DIM)
        return _dup_heads(jnp.concatenate([prompt_part, sample_part], axis=0))

    k_blocks = key_blocks(k_all, cache_k_win)
    v_blocks = key_blocks(v_all, cache_v_win)
    q_s = q_all[OFF_SAMPLE:OFF_SAMPLE + T_SAMPLE].reshape(DEC_BATCH, DEC_SEQ, D_MODEL)
    q_s = jnp.concatenate([q_s, jnp.zeros((DEC_BATCH, ATT_BLOCK - DEC_SEQ, D_MODEL), BF16)], axis=1)
    q_blocks = jnp.concatenate([q_all[:T_PROMPT], q_s.reshape(DEC_BATCH * ATT_BLOCK, D_MODEL)], axis=0)
    bi = np.arange(BATCH * nbp)
    prev_p = (bi // nbp) * (nbp + 1) + bi % nbp
    si = np.arange(DEC_BATCH)
    prev_s = BATCH * (nbp + 1) + 2 * si
    prev_idx = jnp.asarray(np.concatenate([prev_p, prev_s]), jnp.int32)
    own_idx = prev_idx + 1
    jmin = jnp.asarray(np.concatenate([np.where(bi % nbp == 0, ATT_BLOCK - N_META, 0),
                                       np.zeros(DEC_BATCH, np.int64)]), jnp.int32)
    att = window_attention(q_blocks, k_blocks, v_blocks, prev_idx, own_idx, jmin, b_sinks[0].astype(F32))
    att_s = att[T_PROMPT:].reshape(DEC_BATCH, ATT_BLOCK, D_MODEL)[:, :DEC_SEQ].reshape(T_SAMPLE, D_MODEL)
    att_all = jnp.concatenate([att[:T_PROMPT], att_s, jnp.zeros((T_ALL - OFF_META, D_MODEL), BF16)], axis=0)
    h = matmul_residual(att_all, b_w_out[0].astype(BF16), h)
    h = moe(h, 1, moe_norm[1], moe_w_group[1], moe_b_group[1], moe_w_expert[1], moe_b_expert[1])

    y_prompt = h[:T_PROMPT].reshape(BATCH, SEQ, D_MODEL)
    y_sample = h[OFF_SAMPLE:OFF_SAMPLE + T_SAMPLE].reshape(DEC_BATCH, DEC_SEQ, D_MODEL)
    kp = k_all[:T_PROMPT].reshape(BATCH, SEQ, KV_HEADS, HEAD_DIM)[:, -WINDOW:]
    vp = v_all[:T_PROMPT].reshape(BATCH, SEQ, KV_HEADS, HEAD_DIM)[:, -WINDOW:]
    ks = k_all[OFF_SAMPLE:OFF_SAMPLE + T_SAMPLE].reshape(DEC_BATCH, DEC_SEQ, KV_HEADS, HEAD_DIM)
    vs = v_all[OFF_SAMPLE:OFF_SAMPLE + T_SAMPLE].reshape(DEC_BATCH, DEC_SEQ, KV_HEADS, HEAD_DIM)
    k_win_s = jnp.concatenate([cache_k_win, ks], axis=1)[:, -WINDOW:]
    v_win_s = jnp.concatenate([cache_v_win, vs], axis=1)[:, -WINDOW:]
    return (y_prompt, y_sample, s_prompt[None], s_sample[None], kp, vp, k_win_s, v_win_s)
```

```python
import functools
import math

import numpy as np
import jax
import jax.numpy as jnp
from jax import lax
from jax.experimental import pallas as pl
from jax.experimental.pallas import tpu as pltpu

F32 = jnp.float32
BF16 = jnp.bfloat16

D_MODEL = 1024
BATCH = 4
SEQ = 4096
DEC_BATCH = 128
DEC_SEQ = 8
PAST_LEN = 8192
N_META = 16
A_HEADS = 8
A_DK = 128
A_DV = 128
Q_HEADS = 16
KV_HEADS = 4
HEAD_DIM = 64
KV_DIM = KV_HEADS * HEAD_DIM
WINDOW = 128
ROPE_DIM = 16
ROPE_THETA = 500000.0
N_GROUPS = 4
EXPERTS_PER_GROUP = 8
N_EXPERTS = 32
D_EXPERT = 512
RMS_EPS = 1e-6

LANES = 128
SUBLANES = 8
VMEM_LIMIT = 56 * 1024 * 1024

ROW_TILE = 256
T_PROMPT = BATCH * SEQ
T_SAMPLE = DEC_BATCH * DEC_SEQ
OFF_SAMPLE = T_PROMPT
OFF_META = T_PROMPT + T_SAMPLE
T_REAL = OFF_META + N_META
T_ALL = -(-T_REAL // ROW_TILE) * ROW_TILE
N_TILES = T_ALL // ROW_TILE

SCAN_CHUNK = 128
ATT_BLOCK = 128
EXPERT_BLOCK = 256


def _cparams(sem):
    return pltpu.CompilerParams(dimension_semantics=sem, vmem_limit_bytes=VMEM_LIMIT)


def _nt_dot(a, b):
    return lax.dot_general(a, b, (((1,), (1,)), ((), ())), preferred_element_type=F32)


def _rms(x, gain):
    ms = jnp.mean(x * x, axis=-1, keepdims=True)
    return x * lax.rsqrt(ms + RMS_EPS) * gain


def _silu(x):
    return x * jax.nn.sigmoid(x)


def _norm_matmul_kernel(x_ref, g_ref, w_ref, o_ref):
    xn = _rms(x_ref[...], g_ref[...])
    o_ref[...] = jnp.dot(xn.astype(BF16), w_ref[...], preferred_element_type=F32).astype(o_ref.dtype)


def norm_matmul(x, gain, w_bf16, out_dtype):
    t, d = x.shape
    n = w_bf16.shape[1]
    return pl.pallas_call(
        _norm_matmul_kernel,
        out_shape=jax.ShapeDtypeStruct((t, n), out_dtype),
        grid=(t // ROW_TILE,),
        in_specs=[pl.BlockSpec((ROW_TILE, d), lambda i: (i, 0)),
                  pl.BlockSpec((1, d), lambda i: (0, 0)),
                  pl.BlockSpec((d, n), lambda i: (0, 0))],
        out_specs=pl.BlockSpec((ROW_TILE, n), lambda i: (i, 0)),
        compiler_params=_cparams(("parallel",)),
        name="norm_matmul",
    )(x, gain.reshape(1, d), w_bf16)


def _head_norm_rope(y, hmean_ref, hgain, cos_t, sina_t, sinb_t):
    rows, width = y.shape
    sq = (y * y).astype(BF16)
    parts = []
    for s in range(width // 256):
        parts.append(jnp.dot(sq[:, s * 256:(s + 1) * 256], hmean_ref[...], preferred_element_type=F32))
    ms = parts[0] if len(parts) == 1 else jnp.concatenate(parts, axis=1)
    yn = y * lax.rsqrt(ms + RMS_EPS) * hgain
    reps = width // LANES
    cos_w = jnp.concatenate([cos_t] * reps, axis=1)
    sina_w = jnp.concatenate([sina_t] * reps, axis=1)
    sinb_w = jnp.concatenate([sinb_t] * reps, axis=1)
    half = ROPE_DIM // 2
    nxt = pltpu.roll(yn, width - half, 1)
    prv = pltpu.roll(yn, half, 1)
    return yn * cos_w + nxt * sina_w + prv * sinb_w


def _kv_kernel(x_ref, g_ref, w_ref, hmean_ref, hg_ref, cos_ref, sina_ref, sinb_ref, k_ref, v_ref):
    xn = _rms(x_ref[...], g_ref[...])
    z = jnp.dot(xn.astype(BF16), w_ref[...], preferred_element_type=F32)
    k = _head_norm_rope(z[:, :KV_DIM], hmean_ref, hg_ref[...], cos_ref[...], sina_ref[...], sinb_ref[...])
    k_ref[...] = k
    v_ref[...] = z[:, KV_DIM:]


def _rope_tile(i):
    tiles_per_seq = SEQ // ROW_TILE
    n_prompt_tiles = T_PROMPT // ROW_TILE
    n_sample_tiles = T_SAMPLE // ROW_TILE
    return (jnp.where(i < n_prompt_tiles, i % tiles_per_seq,
                      jnp.where(i < n_prompt_tiles + n_sample_tiles, tiles_per_seq, tiles_per_seq + 1)), 0)


def kv_project(x, gain, w_bf16, hmean, hgain_w, cos_t, sina_t, sinb_t):
    t, d = x.shape
    row = lambda i: (i, 0)
    fix = lambda i: (0, 0)
    return pl.pallas_call(
        _kv_kernel,
        out_shape=(jax.ShapeDtypeStruct((t, KV_DIM), F32), jax.ShapeDtypeStruct((t, KV_DIM), F32)),
        grid=(t // ROW_TILE,),
        in_specs=[pl.BlockSpec((ROW_TILE, d), row), pl.BlockSpec((1, d), fix),
                  pl.BlockSpec((d, 2 * KV_DIM), fix), pl.BlockSpec((256, 256), fix),
                  pl.BlockSpec((1, KV_DIM), fix),
                  pl.BlockSpec((ROW_TILE, LANES), _rope_tile), pl.BlockSpec((ROW_TILE, LANES), _rope_tile),
                  pl.BlockSpec((ROW_TILE, LANES), _rope_tile)],
        out_specs=(pl.BlockSpec((ROW_TILE, KV_DIM), row), pl.BlockSpec((ROW_TILE, KV_DIM), row)),
        compiler_params=_cparams(("parallel",)),
        name="kv_project",
    )(x, gain.reshape(1, d), w_bf16, hmean, hgain_w, cos_t, sina_t, sinb_t)


def _q_kernel(x_ref, g_ref, w_ref, hmean_ref, hg_ref, cos_ref, sina_ref, sinb_ref, q_ref):
    xn = _rms(x_ref[...], g_ref[...])
    z = jnp.dot(xn.astype(BF16), w_ref[...], preferred_element_type=F32)
    q = _head_norm_rope(z, hmean_ref, hg_ref[...], cos_ref[...], sina_ref[...], sinb_ref[...])
    q_ref[...] = (q * HEAD_DIM ** -0.5).astype(q_ref.dtype)


def q_project(x, gain, w_bf16, hmean, hgain_w, cos_t, sina_t, sinb_t):
    t, d = x.shape
    row = lambda i: (i, 0)
    fix = lambda i: (0, 0)
    return pl.pallas_call(
        _q_kernel,
        out_shape=jax.ShapeDtypeStruct((t, d), BF16),
        grid=(t // ROW_TILE,),
        in_specs=[pl.BlockSpec((ROW_TILE, d), row), pl.BlockSpec((1, d), fix),
                  pl.BlockSpec((d, d), fix), pl.BlockSpec((256, 256), fix),
                  pl.BlockSpec((1, d), fix),
                  pl.BlockSpec((ROW_TILE, LANES), _rope_tile), pl.BlockSpec((ROW_TILE, LANES), _rope_tile),
                  pl.BlockSpec((ROW_TILE, LANES), _rope_tile)],
        out_specs=pl.BlockSpec((ROW_TILE, d), row),
        compiler_params=_cparams(("parallel",)),
        name="q_project",
    )(x, gain.reshape(1, d), w_bf16, hmean, hgain_w, cos_t, sina_t, sinb_t)


def _matmul_residual_kernel(a_ref, w_ref, r_ref, o_ref):
    o_ref[...] = r_ref[...] + jnp.dot(a_ref[...], w_ref[...], preferred_element_type=F32)


def matmul_residual(a_bf16, w_bf16, resid):
    t, k = a_bf16.shape
    n = w_bf16.shape[1]
    return pl.pallas_call(
        _matmul_residual_kernel,
        out_shape=jax.ShapeDtypeStruct((t, n), F32),
        grid=(t // ROW_TILE,),
        in_specs=[pl.BlockSpec((ROW_TILE, k), lambda i: (i, 0)),
                  pl.BlockSpec((k, n), lambda i: (0, 0)),
                  pl.BlockSpec((ROW_TILE, n), lambda i: (i, 0))],
        out_specs=pl.BlockSpec((ROW_TILE, n), lambda i: (i, 0)),
        compiler_params=_cparams(("parallel",)),
        name="matmul_residual",
    )(a_bf16, w_bf16, resid)


def _scan_levels(c):
    levels = []
    m = c
    while m >= 2:
        levels.append(m)
        m //= 2
    return levels


def _scan_kernel(z_ref, s0_ref, lb_ref, og_ref, tri_ref, bmask_ref, o_ref, sfin_ref, s_scr, b_scr, *, chunk):
    c_idx = pl.program_id(1)
    levels = _scan_levels(chunk)
    mrows = max(chunk, 16)
    hk = A_HEADS * A_DK

    @pl.when(c_idx == 0)
    def _():
        s_scr[...] = s0_ref[0]

    row = lax.broadcasted_iota(jnp.int32, (chunk, LANES), 0)
    sub = lax.broadcasted_iota(jnp.int32, (SUBLANES, LANES), 0)
    og = og_ref[...]

    def pad_rows(x, n):
        if x.shape[0] != n:
            x = jnp.concatenate([x, jnp.zeros((n - x.shape[0], x.shape[1]), x.dtype)], axis=0)
        return x.astype(BF16)

    def head_body(h, carry):
        off = pl.multiple_of(h * LANES, LANES)
        q_raw = z_ref[:, pl.ds(off, LANES)]
        f_raw = z_ref[:, pl.ds(hk + off, LANES)]
        v = z_ref[:, pl.ds(2 * hk + off, LANES)]
        g_raw = z_ref[:, pl.ds(3 * hk + off, LANES)]
        lb = lb_ref[:, pl.ds(off, LANES)]
        forget = lb + (1.0 - lb) * jax.nn.sigmoid(f_raw)
        logf = jnp.log(forget)
        kf = 1.0 - forget
        qf = _silu(q_raw)

        hi = logf.astype(BF16).astype(F32)
        r1 = logf - hi
        mid = r1.astype(BF16).astype(F32)
        lo = r1 - mid
        parts = pad_rows(jnp.concatenate([hi, mid, lo], axis=1), LANES)
        cs = jnp.dot(tri_ref[...], parts, preferred_element_type=F32)
        b = (cs[:chunk, :LANES] + cs[:chunk, LANES:2 * LANES]) + cs[:chunk, 2 * LANES:]
        b_scr[h] = b
        b_rows = b_scr.at[h]

        def bref_for(m):
            half = m // 2
            pieces = []
            for g in range(chunk // SUBLANES):
                base = g * SUBLANES
                if m >= SUBLANES:
                    r = (base // m) * m + half - 1
                    piece = jnp.broadcast_to(b_rows[r:r + 1, :], (SUBLANES, LANES))
                else:
                    piece = jnp.broadcast_to(b_rows[base + half - 1:base + half, :], (SUBLANES, LANES))
                    for blk in range(1, SUBLANES // m):
                        r = base + blk * m + half - 1
                        piece = jnp.where(sub >= blk * m,
                                          jnp.broadcast_to(b_rows[r:r + 1, :], (SUBLANES, LANES)), piece)
                pieces.append(piece)
            return pieces[0] if len(pieces) == 1 else jnp.concatenate(pieces, axis=0)

        qf_b = pad_rows(qf, mrows)
        kf_b = pad_rows(kf, LANES)
        att = _nt_dot(qf_b, kf_b) * bmask_ref[len(levels)]
        for li, m in enumerate(levels):
            upper = (row & (m - 1)) >= (m // 2)
            e = jnp.exp(-jnp.abs(b - bref_for(m)))
            w = jnp.where(upper, qf, kf) * e
            qm = pad_rows(jnp.where(upper, w, 0.0), mrows)
            km = pad_rows(jnp.where(upper, 0.0, w), LANES)
            s = _nt_dot(qm, km)
            att = att + (s if m == chunk and chunk == LANES else s * bmask_ref[li])

        v_b = pad_rows(v, LANES)
        o_intra = jnp.dot(att.astype(BF16), v_b, preferred_element_type=F32)
        s_old = s_scr[h]
        qs = pad_rows(qf * jnp.exp(b), mrows)
        o_inter = jnp.dot(qs, s_old.astype(BF16), preferred_element_type=F32)
        o = (o_intra + o_inter)[:chunk]

        b_last = b_rows[chunk - 1:chunk, :]
        kd = kf * jnp.exp(b_last - b)
        if chunk != LANES:
            kd = jnp.concatenate([kd, jnp.zeros((LANES - chunk, LANES), F32)], axis=0)
        upd = jnp.dot(kd.T.astype(BF16), v_b, preferred_element_type=F32)
        decay_col = jnp.broadcast_to(jnp.exp(b_last), (LANES, LANES)).T
        s_scr[h] = decay_col * s_old + upd

        on = _rms(o, og) * _silu(g_raw)
        o_ref[:, pl.ds(off, LANES)] = on.astype(o_ref.dtype)
        return carry

    lax.fori_loop(0, A_HEADS, head_body, 0, unroll=4 if chunk == LANES else A_HEADS)

    @pl.when(c_idx == pl.num_programs(1) - 1)
    def _():
        sfin_ref[0] = s_scr[...]


def _scan_consts(chunk):
    levels = _scan_levels(chunk)
    mrows = max(chunk, 16)
    r = np.arange(LANES)
    tri = (r[None, :] <= r[:, None]).astype(np.float32)[:mrows]
    masks = []
    for m in levels:
        masks.append((r[:mrows, None] // m == r[None, :] // m).astype(np.float32))
    masks.append((r[:mrows, None] == r[None, :]).astype(np.float32))
    return jnp.asarray(tri, BF16), jnp.asarray(np.stack(masks), F32)


def hgrn2_scan(z, s0, lb, o_gain, *, row_off, n_seq, seq_len, chunk):
    n_chunks = seq_len // chunk
    blk_off = row_off // chunk
    tri, bmask = _scan_consts(chunk)
    mrows = max(chunk, 16)
    shared_s0 = s0.shape[0] == 1
    hv = A_HEADS * A_DV
    o, sfin = pl.pallas_call(
        functools.partial(_scan_kernel, chunk=chunk),
        out_shape=(jax.ShapeDtypeStruct((n_seq * seq_len, hv), BF16 if chunk % 16 == 0 else F32),
                   jax.ShapeDtypeStruct((n_seq, A_HEADS, A_DK, A_DV), F32)),
        grid=(n_seq, n_chunks),
        in_specs=[pl.BlockSpec((chunk, 4 * hv), lambda s, c: (blk_off + s * n_chunks + c, 0)),
                  pl.BlockSpec((1, A_HEADS, A_DK, A_DV), (lambda s, c: (0, 0, 0, 0)) if shared_s0
                               else (lambda s, c: (s, 0, 0, 0))),
                  pl.BlockSpec((1, hv), lambda s, c: (0, 0)),
                  pl.BlockSpec((1, A_DV), lambda s, c: (0, 0)),
                  pl.BlockSpec((mrows, LANES), lambda s, c: (0, 0)),
                  pl.BlockSpec(bmask.shape, lambda s, c: (0, 0, 0))],
        out_specs=(pl.BlockSpec((chunk, hv), lambda s, c: (s * n_chunks + c, 0)),
                   pl.BlockSpec((1, A_HEADS, A_DK, A_DV), lambda s, c: (s, 0, 0, 0))),
        scratch_shapes=[pltpu.VMEM((A_HEADS, A_DK, A_DV), F32), pltpu.VMEM((A_HEADS, chunk, LANES), F32)],
        compiler_params=_cparams(("parallel", "arbitrary")),
        name=f"hgrn2_scan_c{chunk}",
    )(z, s0, lb.reshape(1, hv), o_gain.reshape(1, A_DV), tri, bmask)
    return o, sfin


KEYS = 2 * ATT_BLOCK


def _pair_operand(x, kh):
    slab = x[:, (kh // 2) * LANES:(kh // 2 + 1) * LANES]
    lane = lax.broadcasted_iota(jnp.int32, slab.shape, 1)
    if kh % 2 == 0:
        lo = jnp.where(lane < HEAD_DIM, slab, 0.0)
        hi = pltpu.roll(lo, HEAD_DIM, 1)
    else:
        hi = jnp.where(lane >= HEAD_DIM, slab, 0.0)
        lo = pltpu.roll(hi, HEAD_DIM, 1)
    return jnp.concatenate([lo, hi], axis=0).astype(BF16)


def _window_bias(rows, jmin):
    t_i = lax.broadcasted_iota(jnp.int32, (rows, 2 * KEYS), 0)
    c_i = lax.broadcasted_iota(jnp.int32, (rows, 2 * KEYS), 1)
    j_i = c_i & (ATT_BLOCK - 1)
    own = (c_i & ATT_BLOCK) != 0
    ok = (own & (j_i <= t_i)) | (jnp.logical_not(own) & (j_i >= t_i) & (j_i >= jmin))
    return jnp.where(ok, 0.0, -jnp.inf).astype(F32)


def _pair_softmax(s, sink_a, sink_b):
    probs, rinv = [], []
    for hh, sink in enumerate((sink_a, sink_b)):
        sh = s[:, hh * KEYS:(hh + 1) * KEYS]
        m = jnp.maximum(jnp.max(sh, axis=-1, keepdims=True), sink)
        p = jnp.exp(sh - m)
        den = jnp.sum(p, axis=-1, keepdims=True) + jnp.exp(sink - m)
        probs.append(p.astype(BF16))
        rinv.append(1.0 / den)
    lane = lax.broadcasted_iota(jnp.int32, (s.shape[0], LANES), 1)
    return jnp.concatenate(probs, axis=1), jnp.where(lane < HEAD_DIM, rinv[0], rinv[1])


def _attn_prompt_kernel(sink_ref, q_ref, kp_ref, ko_ref, vp_ref, vo_ref, km_ref, vm_ref, o_ref,
                        k2_scr, v2_scr, s_scr, p_scr, r_scr):
    n = pl.program_id(0)
    nbp = SEQ // ATT_BLOCK
    n_pairs = Q_HEADS // 2

    @pl.when(n >= BATCH * nbp)
    def _():
        o_ref[...] = jnp.zeros_like(o_ref)

    @pl.when(n < BATCH * nbp)
    def _():
        first = (n % nbp) == 0
        jmin = jnp.where(first, ATT_BLOCK - N_META, 0)
        k = jnp.concatenate([jnp.where(first, km_ref[...], kp_ref[...]), ko_ref[...]], axis=0)
        v = jnp.concatenate([jnp.where(first, vm_ref[...], vp_ref[...]), vo_ref[...]], axis=0)
        bias = _window_bias(ATT_BLOCK, jmin)
        for kh in range(KV_HEADS):
            k2_scr[kh] = _pair_operand(k, kh)
            v2_scr[kh] = _pair_operand(v, kh)
        for pair in range(n_pairs):
            s_scr[pair] = _nt_dot(q_ref[:, pair * LANES:(pair + 1) * LANES], k2_scr[pair // 2]) + bias
        for pair in range(n_pairs):
            p, rinv = _pair_softmax(s_scr[pair], sink_ref[2 * pair], sink_ref[2 * pair + 1])
            p_scr[pair] = p
            r_scr[pair] = rinv
        for pair in range(n_pairs):
            o = jnp.dot(p_scr[pair], v2_scr[pair // 2], preferred_element_type=F32) * r_scr[pair]
            o_ref[:, pair * LANES:(pair + 1) * LANES] = o.astype(o_ref.dtype)


def attention_prompt(q_all, k_all, v_all, k_meta_blk, v_meta_blk, sinks):
    n_prompt_blocks = T_PROMPT // ATT_BLOCK
    n_blocks = T_ALL // ATT_BLOCK
    n_pairs = Q_HEADS // 2
    own = lambda n, sk: (jnp.minimum(n, n_prompt_blocks - 1), 0)
    prev = lambda n, sk: (jnp.maximum(jnp.minimum(n, n_prompt_blocks - 1) - 1, 0), 0)
    fix = lambda n, sk: (0, 0)
    grid_spec = pltpu.PrefetchScalarGridSpec(
        num_scalar_prefetch=1,
        grid=(n_blocks,),
        in_specs=[pl.BlockSpec((ATT_BLOCK, D_MODEL), own),
                  pl.BlockSpec((ATT_BLOCK, KV_DIM), prev), pl.BlockSpec((ATT_BLOCK, KV_DIM), own),
                  pl.BlockSpec((ATT_BLOCK, KV_DIM), prev), pl.BlockSpec((ATT_BLOCK, KV_DIM), own),
                  pl.BlockSpec((ATT_BLOCK, KV_DIM), fix), pl.BlockSpec((ATT_BLOCK, KV_DIM), fix)],
        out_specs=pl.BlockSpec((ATT_BLOCK, D_MODEL), lambda n, sk: (n, 0)),
        scratch_shapes=[pltpu.VMEM((KV_HEADS, 2 * KEYS, LANES), BF16), pltpu.VMEM((KV_HEADS, 2 * KEYS, LANES), BF16),
                        pltpu.VMEM((n_pairs, ATT_BLOCK, 2 * KEYS), F32),
                        pltpu.VMEM((n_pairs, ATT_BLOCK, 2 * KEYS), BF16),
                        pltpu.VMEM((n_pairs, ATT_BLOCK, LANES), F32)],
    )
    return pl.pallas_call(
        _attn_prompt_kernel,
        out_shape=jax.ShapeDtypeStruct((T_ALL, D_MODEL), BF16),
        grid_spec=grid_spec,
        compiler_params=_cparams(("parallel",)),
        name="attention_prompt",
    )(sinks, q_all, k_all, k_all, v_all, v_all, k_meta_blk, v_meta_blk)


SAMPLE_GROUP = ATT_BLOCK // DEC_SEQ


def _attn_sample_kernel(sink_ref, q_ref, ck_ref, cv_ref, kn_ref, vn_ref, buf_ref, o_ref,
                        qf_scr, of_scr, k2_scr, v2_scr):
    del buf_ref
    qrows = 2 * DEC_SEQ
    qf_scr[...] = q_ref[...].astype(F32)
    bias = _window_bias(qrows, 0)
    zq = jnp.zeros((qrows - DEC_SEQ, D_MODEL), F32)
    zk = jnp.zeros((ATT_BLOCK - DEC_SEQ, KV_DIM), F32)

    def seq_body(i, carry):
        r_new = pl.multiple_of(i * DEC_SEQ, DEC_SEQ)
        r_old = pl.multiple_of(i * WINDOW, WINDOW)
        q = jnp.concatenate([qf_scr[pl.ds(r_new, DEC_SEQ), :], zq], axis=0).astype(BF16)
        k = jnp.concatenate([ck_ref[pl.ds(r_old, WINDOW), :], kn_ref[pl.ds(r_new, DEC_SEQ), :], zk], axis=0)
        v = jnp.concatenate([cv_ref[pl.ds(r_old, WINDOW), :], vn_ref[pl.ds(r_new, DEC_SEQ), :], zk], axis=0)
        for kh in range(KV_HEADS):
            k2_scr[kh] = _pair_operand(k, kh)
            v2_scr[kh] = _pair_operand(v, kh)
        scores = [_nt_dot(q[:, pair * LANES:(pair + 1) * LANES], k2_scr[pair // 2]) + bias
                  for pair in range(Q_HEADS // 2)]
        soft = [_pair_softmax(s, sink_ref[2 * pair], sink_ref[2 * pair + 1]) for pair, s in enumerate(scores)]
        for pair, (p, rinv) in enumerate(soft):
            o = jnp.dot(p, v2_scr[pair // 2], preferred_element_type=F32) * rinv
            of_scr[pl.ds(r_new, DEC_SEQ), pair * LANES:(pair + 1) * LANES] = o[:DEC_SEQ]
        return carry

    lax.fori_loop(0, SAMPLE_GROUP, seq_body, 0)
    o_ref[...] = of_scr[...].astype(o_ref.dtype)


def attention_sample(q_all, cache_k, cache_v, k_all, v_all, sinks, out_buf):
    first_blk = OFF_SAMPLE // ATT_BLOCK
    new = lambda g, sk: (first_blk + g, 0)
    old = lambda g, sk: (g, 0)
    grid_spec = pltpu.PrefetchScalarGridSpec(
        num_scalar_prefetch=1,
        grid=(DEC_BATCH // SAMPLE_GROUP,),
        in_specs=[pl.BlockSpec((ATT_BLOCK, D_MODEL), new),
                  pl.BlockSpec((SAMPLE_GROUP * WINDOW, KV_DIM), old),
                  pl.BlockSpec((SAMPLE_GROUP * WINDOW, KV_DIM), old),
                  pl.BlockSpec((ATT_BLOCK, KV_DIM), new), pl.BlockSpec((ATT_BLOCK, KV_DIM), new),
                  pl.BlockSpec(memory_space=pl.ANY)],
        out_specs=pl.BlockSpec((ATT_BLOCK, D_MODEL), new),
        scratch_shapes=[pltpu.VMEM((ATT_BLOCK, D_MODEL), F32), pltpu.VMEM((ATT_BLOCK, D_MODEL), F32),
                        pltpu.VMEM((KV_HEADS, 2 * KEYS, LANES), BF16), pltpu.VMEM((KV_HEADS, 2 * KEYS, LANES), BF16)],
    )
    return pl.pallas_call(
        _attn_sample_kernel,
        out_shape=jax.ShapeDtypeStruct(out_buf.shape, out_buf.dtype),
        grid_spec=grid_spec,
        input_output_aliases={6: 0},
        compiler_params=_cparams(("parallel",)),
        name="attention_sample",
    )(sinks, q_all, cache_k, cache_v, k_all, v_all, out_buf)


ROUTE_COLS = 8


def _route_kernel(x_ref, g_ref, wr_ref, br_ref, ltri_ref, xn_ref, rec_ref, cnt_ref, cnt_scr):
    i = pl.program_id(0)

    @pl.when(i == 0)
    def _():
        cnt_scr[...] = jnp.zeros_like(cnt_scr)

    xn = _rms(x_ref[...], g_ref[...])
    xn_ref[...] = xn
    logits = jnp.dot(xn, wr_ref[...], preferred_element_type=F32,
                     precision=lax.Precision.HIGHEST) + br_ref[...]
    rows = logits.shape[0]
    lane = lax.broadcasted_iota(jnp.int32, (rows, LANES), 1).astype(F32)
    neg = jnp.float32(-jnp.inf)
    big = jnp.float32(LANES)

    is_g = (lane >= N_EXPERTS) & (lane < N_EXPERTS + N_GROUPS)
    gl = jnp.where(is_g, logits, neg)
    gmax = jnp.max(gl, axis=-1, keepdims=True)
    gsel = jnp.min(jnp.where(gl == gmax, lane, big), axis=-1, keepdims=True) - N_EXPERTS
    gden = jnp.sum(jnp.where(is_g, jnp.exp(gl - gmax), 0.0), axis=-1, keepdims=True)
    gw = 1.0 / gden

    in_grp = (lane >= gsel * EXPERTS_PER_GROUP) & (lane < (gsel + 1) * EXPERTS_PER_GROUP)
    el = jnp.where(in_grp, logits, neg)
    t1 = jnp.max(el, axis=-1, keepdims=True)
    e1 = jnp.min(jnp.where(el == t1, lane, big), axis=-1, keepdims=True)
    el2 = jnp.where(lane == e1, neg, el)
    t2 = jnp.max(el2, axis=-1, keepdims=True)
    e2 = jnp.min(jnp.where(el2 == t2, lane, big), axis=-1, keepdims=True)
    x2 = jnp.exp(t2 - t1)
    w1 = gw / (1.0 + x2)
    w2 = gw * x2 / (1.0 + x2)

    oh1 = (lane == e1).astype(F32)
    oh2 = (lane == e2).astype(F32)
    oh = oh1 + oh2
    before = jnp.dot(ltri_ref[...], oh.astype(BF16), preferred_element_type=F32)
    base = cnt_scr[...] + before
    r1 = jnp.sum(base * oh1, axis=-1, keepdims=True)
    r2 = jnp.sum(base * oh2, axis=-1, keepdims=True)
    cnt_scr[...] = cnt_scr[...] + jnp.sum(oh, axis=0, keepdims=True)

    rec = jnp.where(lane == 0, e1,
          jnp.where(lane == 1, e2,
          jnp.where(lane == 2, r1,
          jnp.where(lane == 3, r2,
          jnp.where(lane == 4, w1,
          jnp.where(lane == 5, w2, 0.0))))))
    rec_ref[...] = rec
    cnt_ref[...] = cnt_scr[...]


def moe_route(h, gain, w_router, b_router, ltri):
    t, d = h.shape
    row = lambda i: (i, 0)
    fix = lambda i: (0, 0)
    return pl.pallas_call(
        _route_kernel,
        out_shape=(jax.ShapeDtypeStruct((t, d), F32), jax.ShapeDtypeStruct((t, LANES), F32),
                   jax.ShapeDtypeStruct((1, LANES), F32)),
        grid=(t // ROW_TILE,),
        in_specs=[pl.BlockSpec((ROW_TILE, d), row), pl.BlockSpec((1, d), fix),
                  pl.BlockSpec((d, LANES), fix), pl.BlockSpec((1, LANES), fix),
                  pl.BlockSpec((ROW_TILE, ROW_TILE), fix)],
        out_specs=(pl.BlockSpec((ROW_TILE, d), row), pl.BlockSpec((ROW_TILE, LANES), row),
                   pl.BlockSpec((1, LANES), fix)),
        scratch_shapes=[pltpu.VMEM((1, LANES), F32)],
        compiler_params=_cparams(("arbitrary",)),
        name="moe_route",
    )(h, gain.reshape(1, d), w_router, b_router, ltri)


def _row_copy(src, src_row, dst, dst_row, sem):
    return pltpu.make_async_copy(src.at[pl.ds(src_row, 1)], dst.at[pl.ds(dst_row, 1)], sem)


def _dispatch_kernel(dest_ref, xn_ref, xs_ref, sem):
    rows = xn_ref.shape[0]

    def issue(r, c):
        _row_copy(xn_ref, r, xs_ref, dest_ref[0, 0, r], sem).start()
        _row_copy(xn_ref, r, xs_ref, dest_ref[0, 0, rows + r], sem).start()
        return c

    lax.fori_loop(0, rows, issue, 0, unroll=8)
    for _ in range(2):
        pltpu.make_async_copy(xn_ref, xs_ref.at[pl.ds(0, rows)], sem).wait()


def moe_dispatch(xn, dest3, n_slots):
    t, d = xn.shape
    return pl.pallas_call(
        _dispatch_kernel,
        out_shape=jax.ShapeDtypeStruct((n_slots, d), F32),
        grid=(t // ROW_TILE,),
        in_specs=[pl.BlockSpec((1, 1, 2 * ROW_TILE), lambda i: (i, 0, 0), memory_space=pltpu.SMEM),
                  pl.BlockSpec((ROW_TILE, d), lambda i: (i, 0))],
        out_specs=pl.BlockSpec(memory_space=pl.ANY),
        scratch_shapes=[pltpu.SemaphoreType.DMA],
        compiler_params=_cparams(("arbitrary",)),
        name="moe_dispatch",
    )(dest3, xn)


def _ffn_kernel(wblk_ref, we_ref, wlo_ref, whi_ref, xs_ref, w13_ref, w2_ref, ys_ref, w13b, w2b):
    w = pl.program_id(0)
    prev = jnp.maximum(w - 1, 0)
    first_visit = (w == 0) | (wblk_ref[w] != wblk_ref[prev])
    lo = wlo_ref[w]
    hi = whi_ref[w]

    @pl.when(hi > lo)
    def _():
        @pl.when((w == 0) | (we_ref[w] != we_ref[prev]))
        def _():
            w13b[...] = w13_ref[...].astype(BF16)
            w2b[...] = w2_ref[...].astype(BF16)

        au = jnp.dot(xs_ref[...].astype(BF16), w13b[...], preferred_element_type=F32)
        hmid = _silu(au[:, :D_EXPERT]) * au[:, D_EXPERT:]
        y = jnp.dot(hmid.astype(BF16), w2b[...], preferred_element_type=F32)
        row = lax.broadcasted_iota(jnp.int32, y.shape, 0)
        mine = (row >= lo) & (row < hi)

        @pl.when(first_visit)
        def _():
            ys_ref[...] = jnp.where(mine, y, 0.0)

        @pl.when(jnp.logical_not(first_visit))
        def _():
            ys_ref[...] = jnp.where(mine, y, ys_ref[...])


def moe_ffn(xs, work, w13_all, w2_all, layer):
    n_slots, d = xs.shape
    n_work = work[0].shape[0]
    xmap = lambda w, wb, we, wlo, whi: (wb[w], 0)
    w_map = lambda w, wb, we, wlo, whi: (layer, we[w], 0, 0)
    grid_spec = pltpu.PrefetchScalarGridSpec(
        num_scalar_prefetch=4,
        grid=(n_work,),
        in_specs=[pl.BlockSpec((EXPERT_BLOCK, d), xmap),
                  pl.BlockSpec((None, None, d, 2 * D_EXPERT), w_map),
                  pl.BlockSpec((None, None, D_EXPERT, d), w_map)],
        out_specs=pl.BlockSpec((EXPERT_BLOCK, d), xmap),
        scratch_shapes=[pltpu.VMEM((d, 2 * D_EXPERT), BF16), pltpu.VMEM((D_EXPERT, d), BF16)],
    )
    return pl.pallas_call(
        _ffn_kernel,
        out_shape=jax.ShapeDtypeStruct((n_slots, d), F32),
        grid_spec=grid_spec,
        compiler_params=_cparams(("arbitrary",)),
        name="moe_ffn",
    )(*work, xs, w13_all, w2_all)


def _ffn_work_items(cnt):
    n_slots = 2 * T_ALL
    n_blocks = n_slots // EXPERT_BLOCK
    n_work = n_blocks + N_EXPERTS - 1
    end = jnp.cumsum(cnt)
    start = end - cnt
    first_blk = start // EXPERT_BLOCK
    last_blk = jnp.maximum(end - 1, start) // EXPERT_BLOCK
    n_items = jnp.where(cnt > 0, last_blk - first_blk + 1, 0)
    item_end = jnp.cumsum(n_items)
    item_start = item_end - n_items
    w = jnp.arange(n_work, dtype=jnp.int32)
    used = w < item_end[-1]
    wq = jnp.minimum(w, item_end[-1] - 1)
    e = jnp.sum((item_end[None, :] <= wq[:, None]).astype(jnp.int32), axis=1)
    blk = jnp.where(used, first_blk[e] + (w - item_start[e]), n_blocks - 1).astype(jnp.int32)
    lo = jnp.maximum(start[e], blk * EXPERT_BLOCK) - blk * EXPERT_BLOCK
    hi = jnp.minimum(end[e], (blk + 1) * EXPERT_BLOCK) - blk * EXPERT_BLOCK
    lo = jnp.where(used, lo, 0).astype(jnp.int32)
    hi = jnp.where(used, hi, 0).astype(jnp.int32)
    return start, (blk, e, lo, hi)


def _combine_kernel(dest_ref, h_ref, rec_ref, ys_ref, o_ref, g1, g2, sem):
    rows = h_ref.shape[0]

    def issue(r, c):
        _row_copy(ys_ref, dest_ref[0, 0, r], g1, r, sem).start()
        _row_copy(ys_ref, dest_ref[0, 0, rows + r], g2, r, sem).start()
        return c

    lax.fori_loop(0, rows, issue, 0, unroll=8)
    for buf in (g1, g2):
        pltpu.make_async_copy(ys_ref.at[pl.ds(0, rows)], buf, sem).wait()
    rec = rec_ref[...]
    o_ref[...] = h_ref[...] + rec[:, 4:5] * g1[...] + rec[:, 5:6] * g2[...]


def moe_combine(h, rec, ys, dest3):
    t, d = h.shape
    row = lambda i: (i, 0)
    return pl.pallas_call(
        _combine_kernel,
        out_shape=jax.ShapeDtypeStruct((t, d), F32),
        grid=(t // ROW_TILE,),
        in_specs=[pl.BlockSpec((1, 1, 2 * ROW_TILE), lambda i: (i, 0, 0), memory_space=pltpu.SMEM),
                  pl.BlockSpec((ROW_TILE, d), row), pl.BlockSpec((ROW_TILE, LANES), row),
                  pl.BlockSpec(memory_space=pl.ANY)],
        out_specs=pl.BlockSpec((ROW_TILE, d), row),
        scratch_shapes=[pltpu.VMEM((ROW_TILE, d), F32), pltpu.VMEM((ROW_TILE, d), F32),
                        pltpu.SemaphoreType.DMA],
        compiler_params=_cparams(("arbitrary",)),
        name="moe_combine",
    )(dest3, h, rec, ys)


def hier_moe_layer(h, layer, gain, w_group, b_group, w_expert, b_expert, w13_all, w2_all, ltri):
    t = h.shape[0]
    pad = LANES - N_EXPERTS - N_GROUPS
    w_router = jnp.concatenate([w_expert, w_group, jnp.zeros((D_MODEL, pad), F32)], axis=1)
    b_router = jnp.concatenate([b_expert, b_group, jnp.zeros((pad,), F32)]).reshape(1, LANES)
    xn, rec, counts = moe_route(h, gain, w_router, b_router, ltri)

    cnt = counts[0, :N_EXPERTS].astype(jnp.int32)
    start, work = _ffn_work_items(cnt)
    e12 = rec[:, 0:2].astype(jnp.int32)
    r12 = rec[:, 2:4].astype(jnp.int32)
    sel = e12[:, :, None] == jnp.arange(N_EXPERTS, dtype=jnp.int32)[None, None, :]
    dest = jnp.sum(jnp.where(sel, start[None, None, :], 0), axis=-1) + r12
    dest3 = dest.reshape(t // ROW_TILE, ROW_TILE, 2).transpose(0, 2, 1).reshape(t // ROW_TILE, 1, 2 * ROW_TILE)

    xs = moe_dispatch(xn, dest3, 2 * t)
    ys = moe_ffn(xs, work, w13_all, w2_all, layer)
    return moe_combine(h, rec, ys, dest3)


def _rope_tables(pos):
    half = ROPE_DIM // 2
    inv = jnp.exp(-math.log(ROPE_THETA) * jnp.arange(half, dtype=F32) * (2.0 / ROPE_DIM))
    ang = pos.astype(F32)[:, None] * inv[None, :]
    cos, sin = jnp.cos(ang), jnp.sin(ang)
    t = pos.shape[0]
    ones = jnp.ones((t, HEAD_DIM - ROPE_DIM), F32)
    zeros = jnp.zeros((t, HEAD_DIM - ROPE_DIM), F32)
    z8 = jnp.zeros((t, half), F32)
    cos_h = jnp.concatenate([cos, cos, ones], axis=1)
    sina_h = jnp.concatenate([-sin, z8, zeros], axis=1)
    sinb_h = jnp.concatenate([z8, sin, zeros], axis=1)
    two = lambda a: jnp.concatenate([a, a], axis=1)
    return two(cos_h), two(sina_h), two(sinb_h)


def kernel(x_prompt, x_sample, state_hgrn, cache_k_win, cache_v_win, meta_tokens, a_norm, a_w_in, a_lower_logits, a_out_norm, a_w_out, kv_norm, kv_w, k_norm, b_norm, b_wq, b_q_norm, b_sinks, b_w_out, moe_norm, moe_w_group, moe_b_group, moe_w_expert, moe_b_expert, moe_w13, moe_w2):
    h = jnp.concatenate([x_prompt.reshape(T_PROMPT, D_MODEL), x_sample.reshape(T_SAMPLE, D_MODEL),
                         meta_tokens.astype(F32), jnp.zeros((T_ALL - T_REAL, D_MODEL), F32)], axis=0)
    pos = jnp.concatenate([N_META + jnp.arange(SEQ, dtype=jnp.int32),
                           jnp.tile(PAST_LEN + jnp.arange(DEC_SEQ, dtype=jnp.int32), ROW_TILE // DEC_SEQ),
                           jnp.arange(N_META, dtype=jnp.int32),
                           jnp.zeros((ROW_TILE - N_META,), jnp.int32)])
    cos_t, sina_t, sinb_t = _rope_tables(pos)
    r256 = np.arange(256)
    hmean = jnp.asarray((r256[:, None] // HEAD_DIM == r256[None, :] // HEAD_DIM).astype(np.float32) / HEAD_DIM, BF16)
    ltri = jnp.asarray((r256[None, :] < r256[:, None]).astype(np.float32), BF16)
    lower = jnp.cumsum(jax.nn.softmax(a_lower_logits.astype(F32), axis=0), axis=0)

    moe = functools.partial(hier_moe_layer, w13_all=moe_w13, w2_all=moe_w2, ltri=ltri)

    z = norm_matmul(h, a_norm[0], a_w_in[0].astype(BF16), F32)
    zero_state = jnp.zeros((1, A_HEADS, A_DK, A_DV), F32)
    o_meta, s_meta = hgrn2_scan(z, zero_state, lower[0], a_out_norm[0],
                                row_off=OFF_META, n_seq=1, seq_len=N_META, chunk=N_META)
    o_prompt, s_prompt = hgrn2_scan(z, s_meta, lower[0], a_out_norm[0],
                                    row_off=0, n_seq=BATCH, seq_len=SEQ, chunk=SCAN_CHUNK)
    o_sample, s_sample = hgrn2_scan(z, state_hgrn[0].astype(F32), lower[0], a_out_norm[0],
                                    row_off=OFF_SAMPLE, n_seq=DEC_BATCH, seq_len=DEC_SEQ, chunk=DEC_SEQ)
    o_all = jnp.concatenate([o_prompt, o_sample.astype(BF16), o_meta,
                             jnp.zeros((T_ALL - T_REAL, D_MODEL), BF16)], axis=0)
    h = matmul_residual(o_all, a_w_out[0].astype(BF16), h)
    h = moe(h, 0, moe_norm[0], moe_w_group[0], moe_b_group[0], moe_w_expert[0], moe_b_expert[0])

    k_all, v_all = kv_project(h, kv_norm, kv_w.astype(BF16), hmean, jnp.tile(k_norm, KV_HEADS).reshape(1, KV_DIM),
                              cos_t, sina_t, sinb_t)

    q_all = q_project(h, b_norm[0], b_wq[0].astype(BF16), hmean, jnp.tile(b_q_norm[0], Q_HEADS).reshape(1, D_MODEL),
                      cos_t, sina_t, sinb_t)
    meta_blk = lambda a: jnp.concatenate([jnp.zeros((ATT_BLOCK - N_META, KV_DIM), F32),
                                          a[OFF_META:OFF_META + N_META]], axis=0)
    sinks = b_sinks[0].astype(F32)
    att_all = attention_prompt(q_all, k_all, v_all, meta_blk(k_all), meta_blk(v_all), sinks)
    att_all = attention_sample(q_all, cache_k_win.reshape(DEC_BATCH * WINDOW, KV_DIM).astype(F32),
                               cache_v_win.reshape(DEC_BATCH * WINDOW, KV_DIM).astype(F32),
                               k_all, v_all, sinks, att_all)
    h = matmul_residual(att_all, b_w_out[0].astype(BF16), h)
    h = moe(h, 1, moe_norm[1], moe_w_group[1], moe_b_group[1], moe_w_expert[1], moe_b_expert[1])

    y_prompt = h[:T_PROMPT].reshape(BATCH, SEQ, D_MODEL)
    y_sample = h[OFF_SAMPLE:OFF_SAMPLE + T_SAMPLE].reshape(DEC_BATCH, DEC_SEQ, D_MODEL)
    kp = k_all[:T_PROMPT].reshape(BATCH, SEQ, KV_HEADS, HEAD_DIM)[:, -WINDOW:]
    vp = v_all[:T_PROMPT].reshape(BATCH, SEQ, KV_HEADS, HEAD_DIM)[:, -WINDOW:]
    ks = k_all[OFF_SAMPLE:OFF_SAMPLE + T_SAMPLE].reshape(DEC_BATCH, DEC_SEQ, KV_HEADS, HEAD_DIM)
    vs = v_all[OFF_SAMPLE:OFF_SAMPLE + T_SAMPLE].reshape(DEC_BATCH, DEC_SEQ, KV_HEADS, HEAD_DIM)
    k_win_s = jnp.concatenate([cache_k_win, ks], axis=1)[:, -WINDOW:]
    v_win_s = jnp.concatenate([cache_v_win, vs], axis=1)[:, -WINDOW:]
    return (y_prompt, y_sample, s_prompt[None], s_sample[None], kp, vp, k_win_s, v_win_s)
```

```python
import functools
import math

import numpy as np
import jax
import jax.numpy as jnp
from jax import lax
from jax.experimental import pallas as pl
from jax.experimental.pallas import tpu as pltpu

F32 = jnp.float32
BF16 = jnp.bfloat16

D_MODEL = 1024
BATCH = 4
SEQ = 4096
DEC_BATCH = 128
DEC_SEQ = 8
PAST_LEN = 8192
N_META = 16
A_HEADS = 8
A_DK = 128
A_DV = 128
Q_HEADS = 16
KV_HEADS = 4
HEAD_DIM = 64
KV_DIM = KV_HEADS * HEAD_DIM
WINDOW = 128
ROPE_DIM = 16
ROPE_THETA = 500000.0
N_GROUPS = 4
EXPERTS_PER_GROUP = 8
N_EXPERTS = 32
D_EXPERT = 512
RMS_EPS = 1e-6

LANES = 128
SUBLANES = 8
VMEM_LIMIT = 56 * 1024 * 1024

ROW_TILE = 256
T_PROMPT = BATCH * SEQ
T_SAMPLE = DEC_BATCH * DEC_SEQ
OFF_SAMPLE = T_PROMPT
OFF_META = T_PROMPT + T_SAMPLE
T_REAL = OFF_META + N_META
T_ALL = -(-T_REAL // ROW_TILE) * ROW_TILE
N_TILES = T_ALL // ROW_TILE

SCAN_CHUNK = 128
ATT_BLOCK = 128
EXPERT_BLOCK = 256


def _cparams(sem):
    return pltpu.CompilerParams(dimension_semantics=sem, vmem_limit_bytes=VMEM_LIMIT)


def _nt_dot(a, b):
    return lax.dot_general(a, b, (((1,), (1,)), ((), ())), preferred_element_type=F32)


def _rms(x, gain):
    ms = jnp.mean(x * x, axis=-1, keepdims=True)
    return x * lax.rsqrt(ms + RMS_EPS) * gain


def _silu(x):
    return x * jax.nn.sigmoid(x)


N_PROMPT_TILES = T_PROMPT // ROW_TILE
N_SAMPLE_TILES = T_SAMPLE // ROW_TILE


def _parts_specs(width):
    return [pl.BlockSpec((ROW_TILE, width), lambda i: (jnp.minimum(i, N_PROMPT_TILES - 1), 0)),
            pl.BlockSpec((ROW_TILE, width), lambda i: (jnp.clip(i - N_PROMPT_TILES, 0, N_SAMPLE_TILES - 1), 0)),
            pl.BlockSpec((ROW_TILE, width), lambda i: (0, 0))]


def _pick_part(i, p_ref, s_ref, t_ref, dtype):
    return jnp.where(i < N_PROMPT_TILES, p_ref[...].astype(dtype),
                     jnp.where(i < N_PROMPT_TILES + N_SAMPLE_TILES, s_ref[...].astype(dtype),
                               t_ref[...].astype(dtype)))


def _in_proj_kernel(xp_ref, xs_ref, xt_ref, g_ref, w_ref, o_ref):
    x = _pick_part(pl.program_id(0), xp_ref, xs_ref, xt_ref, F32)
    xn = _rms(x, g_ref[...])
    o_ref[...] = jnp.dot(xn.astype(BF16), w_ref[...], preferred_element_type=F32)


def in_project(x_parts, gain, w_bf16):
    d, n = w_bf16.shape
    return pl.pallas_call(
        _in_proj_kernel,
        out_shape=jax.ShapeDtypeStruct((T_ALL, n), F32),
        grid=(N_TILES,),
        in_specs=_parts_specs(d) + [pl.BlockSpec((1, d), lambda i: (0, 0)),
                                    pl.BlockSpec((d, n), lambda i: (0, 0))],
        out_specs=pl.BlockSpec((ROW_TILE, n), lambda i: (i, 0)),
        compiler_params=_cparams(("parallel",)),
        name="in_project",
    )(*x_parts, gain.reshape(1, d), w_bf16)


def _mixer_out_kernel(ap_ref, as_ref, at_ref, w_ref, xp_ref, xs_ref, xt_ref, o_ref):
    i = pl.program_id(0)
    a = _pick_part(i, ap_ref, as_ref, at_ref, BF16)
    x = _pick_part(i, xp_ref, xs_ref, xt_ref, F32)
    o_ref[...] = x + jnp.dot(a, w_ref[...], preferred_element_type=F32)


def mixer_out(a_parts, w_bf16, x_parts):
    k, n = w_bf16.shape
    return pl.pallas_call(
        _mixer_out_kernel,
        out_shape=jax.ShapeDtypeStruct((T_ALL, n), F32),
        grid=(N_TILES,),
        in_specs=_parts_specs(k) + [pl.BlockSpec((k, n), lambda i: (0, 0))] + _parts_specs(n),
        out_specs=pl.BlockSpec((ROW_TILE, n), lambda i: (i, 0)),
        compiler_params=_cparams(("parallel",)),
        name="mixer_out",
    )(*a_parts, w_bf16, *x_parts)


def _head_norm_rope(y, hmean_ref, hgain, cos_t, sina_t, sinb_t):
    rows, width = y.shape
    sq = (y * y).astype(BF16)
    parts = []
    for s in range(width // 256):
        parts.append(jnp.dot(sq[:, s * 256:(s + 1) * 256], hmean_ref[...], preferred_element_type=F32))
    ms = parts[0] if len(parts) == 1 else jnp.concatenate(parts, axis=1)
    yn = y * lax.rsqrt(ms + RMS_EPS) * hgain
    reps = width // LANES
    cos_w = jnp.concatenate([cos_t] * reps, axis=1)
    sina_w = jnp.concatenate([sina_t] * reps, axis=1)
    sinb_w = jnp.concatenate([sinb_t] * reps, axis=1)
    half = ROPE_DIM // 2
    nxt = pltpu.roll(yn, width - half, 1)
    prv = pltpu.roll(yn, half, 1)
    return yn * cos_w + nxt * sina_w + prv * sinb_w


def _kv_kernel(x_ref, g_ref, w_ref, hmean_ref, hg_ref, cos_ref, sina_ref, sinb_ref, k_ref, v_ref):
    xn = _rms(x_ref[...], g_ref[...])
    z = jnp.dot(xn.astype(BF16), w_ref[...], preferred_element_type=F32)
    k = _head_norm_rope(z[:, :KV_DIM], hmean_ref, hg_ref[...], cos_ref[...], sina_ref[...], sinb_ref[...])
    k_ref[...] = k
    v_ref[...] = z[:, KV_DIM:]


def _rope_tile(i):
    tiles_per_seq = SEQ // ROW_TILE
    n_prompt_tiles = T_PROMPT // ROW_TILE
    n_sample_tiles = T_SAMPLE // ROW_TILE
    return (jnp.where(i < n_prompt_tiles, i % tiles_per_seq,
                      jnp.where(i < n_prompt_tiles + n_sample_tiles, tiles_per_seq, tiles_per_seq + 1)), 0)


def kv_project(x, gain, w_bf16, hmean, hgain_w, cos_t, sina_t, sinb_t):
    t, d = x.shape
    row = lambda i: (i, 0)
    fix = lambda i: (0, 0)
    return pl.pallas_call(
        _kv_kernel,
        out_shape=(jax.ShapeDtypeStruct((t, KV_DIM), F32), jax.ShapeDtypeStruct((t, KV_DIM), F32)),
        grid=(t // ROW_TILE,),
        in_specs=[pl.BlockSpec((ROW_TILE, d), row), pl.BlockSpec((1, d), fix),
                  pl.BlockSpec((d, 2 * KV_DIM), fix), pl.BlockSpec((256, 256), fix),
                  pl.BlockSpec((1, KV_DIM), fix),
                  pl.BlockSpec((ROW_TILE, LANES), _rope_tile), pl.BlockSpec((ROW_TILE, LANES), _rope_tile),
                  pl.BlockSpec((ROW_TILE, LANES), _rope_tile)],
        out_specs=(pl.BlockSpec((ROW_TILE, KV_DIM), row), pl.BlockSpec((ROW_TILE, KV_DIM), row)),
        compiler_params=_cparams(("parallel",)),
        name="kv_project",
    )(x, gain.reshape(1, d), w_bf16, hmean, hgain_w, cos_t, sina_t, sinb_t)


def _q_kernel(x_ref, g_ref, w_ref, hmean_ref, hg_ref, cos_ref, sina_ref, sinb_ref, q_ref):
    xn = _rms(x_ref[...], g_ref[...])
    z = jnp.dot(xn.astype(BF16), w_ref[...], preferred_element_type=F32)
    q = _head_norm_rope(z, hmean_ref, hg_ref[...], cos_ref[...], sina_ref[...], sinb_ref[...])
    q_ref[...] = (q * HEAD_DIM ** -0.5).astype(q_ref.dtype)


def q_project(x, gain, w_bf16, hmean, hgain_w, cos_t, sina_t, sinb_t):
    t, d = x.shape
    row = lambda i: (i, 0)
    fix = lambda i: (0, 0)
    return pl.pallas_call(
        _q_kernel,
        out_shape=jax.ShapeDtypeStruct((t, d), BF16),
        grid=(t // ROW_TILE,),
        in_specs=[pl.BlockSpec((ROW_TILE, d), row), pl.BlockSpec((1, d), fix),
                  pl.BlockSpec((d, d), fix), pl.BlockSpec((256, 256), fix),
                  pl.BlockSpec((1, d), fix),
                  pl.BlockSpec((ROW_TILE, LANES), _rope_tile), pl.BlockSpec((ROW_TILE, LANES), _rope_tile),
                  pl.BlockSpec((ROW_TILE, LANES), _rope_tile)],
        out_specs=pl.BlockSpec((ROW_TILE, d), row),
        compiler_params=_cparams(("parallel",)),
        name="q_project",
    )(x, gain.reshape(1, d), w_bf16, hmean, hgain_w, cos_t, sina_t, sinb_t)


def _matmul_residual_kernel(a_ref, w_ref, r_ref, o_ref):
    o_ref[...] = r_ref[...] + jnp.dot(a_ref[...], w_ref[...], preferred_element_type=F32)


def matmul_residual(a_bf16, w_bf16, resid):
    t, k = a_bf16.shape
    n = w_bf16.shape[1]
    return pl.pallas_call(
        _matmul_residual_kernel,
        out_shape=jax.ShapeDtypeStruct((t, n), F32),
        grid=(t // ROW_TILE,),
        in_specs=[pl.BlockSpec((ROW_TILE, k), lambda i: (i, 0)),
                  pl.BlockSpec((k, n), lambda i: (0, 0)),
                  pl.BlockSpec((ROW_TILE, n), lambda i: (i, 0))],
        out_specs=pl.BlockSpec((ROW_TILE, n), lambda i: (i, 0)),
        compiler_params=_cparams(("parallel",)),
        name="matmul_residual",
    )(a_bf16, w_bf16, resid)


def _scan_levels(c):
    levels = []
    m = c
    while m >= 2:
        levels.append(m)
        m //= 2
    return levels


def _scan_kernel(z_ref, s0_ref, lb_ref, og_ref, tri_ref, bmask_ref, o_ref, sfin_ref, s_scr, b_scr, *, chunk):
    c_idx = pl.program_id(1)
    levels = _scan_levels(chunk)
    mrows = max(chunk, 16)
    hk = A_HEADS * A_DK

    @pl.when(c_idx == 0)
    def _():
        s_scr[...] = s0_ref[0]

    row = lax.broadcasted_iota(jnp.int32, (chunk, LANES), 0)
    sub = lax.broadcasted_iota(jnp.int32, (SUBLANES, LANES), 0)
    og = og_ref[...]

    def pad_rows(x, n):
        if x.shape[0] != n:
            x = jnp.concatenate([x, jnp.zeros((n - x.shape[0], x.shape[1]), x.dtype)], axis=0)
        return x.astype(BF16)

    def head_body(h, carry):
        off = pl.multiple_of(h * LANES, LANES)
        q_raw = z_ref[:, pl.ds(off, LANES)]
        f_raw = z_ref[:, pl.ds(hk + off, LANES)]
        v = z_ref[:, pl.ds(2 * hk + off, LANES)]
        g_raw = z_ref[:, pl.ds(3 * hk + off, LANES)]
        lb = lb_ref[:, pl.ds(off, LANES)]
        forget = lb + (1.0 - lb) * jax.nn.sigmoid(f_raw)
        logf = jnp.log(forget)
        kf = 1.0 - forget
        qf = _silu(q_raw)

        hi = logf.astype(BF16).astype(F32)
        r1 = logf - hi
        mid = r1.astype(BF16).astype(F32)
        lo = r1 - mid
        parts = pad_rows(jnp.concatenate([hi, mid, lo], axis=1), LANES)
        cs = jnp.dot(tri_ref[...], parts, preferred_element_type=F32)
        b = (cs[:chunk, :LANES] + cs[:chunk, LANES:2 * LANES]) + cs[:chunk, 2 * LANES:]
        b_scr[h] = b
        b_rows = b_scr.at[h]

        def bref_for(m):
            half = m // 2
            pieces = []
            for g in range(chunk // SUBLANES):
                base = g * SUBLANES
                if m >= SUBLANES:
                    r = (base // m) * m + half - 1
                    piece = jnp.broadcast_to(b_rows[r:r + 1, :], (SUBLANES, LANES))
                else:
                    piece = jnp.broadcast_to(b_rows[base + half - 1:base + half, :], (SUBLANES, LANES))
                    for blk in range(1, SUBLANES // m):
                        r = base + blk * m + half - 1
                        piece = jnp.where(sub >= blk * m,
                                          jnp.broadcast_to(b_rows[r:r + 1, :], (SUBLANES, LANES)), piece)
                pieces.append(piece)
            return pieces[0] if len(pieces) == 1 else jnp.concatenate(pieces, axis=0)

        qf_b = pad_rows(qf, mrows)
        kf_b = pad_rows(kf, LANES)
        att = _nt_dot(qf_b, kf_b) * bmask_ref[len(levels)]
        for li, m in enumerate(levels):
            upper = (row & (m - 1)) >= (m // 2)
            e = jnp.exp(-jnp.abs(b - bref_for(m)))
            w = jnp.where(upper, qf, kf) * e
            qm = pad_rows(jnp.where(upper, w, 0.0), mrows)
            km = pad_rows(jnp.where(upper, 0.0, w), LANES)
            s = _nt_dot(qm, km)
            att = att + (s if m == chunk and chunk == LANES else s * bmask_ref[li])

        v_b = pad_rows(v, LANES)
        o_intra = jnp.dot(att.astype(BF16), v_b, preferred_element_type=F32)
        s_old = s_scr[h]
        qs = pad_rows(qf * jnp.exp(b), mrows)
        o_inter = jnp.dot(qs, s_old.astype(BF16), preferred_element_type=F32)
        o = (o_intra + o_inter)[:chunk]

        b_last = b_rows[chunk - 1:chunk, :]
        kd = kf * jnp.exp(b_last - b)
        if chunk != LANES:
            kd = jnp.concatenate([kd, jnp.zeros((LANES - chunk, LANES), F32)], axis=0)
        upd = jnp.dot(kd.T.astype(BF16), v_b, preferred_element_type=F32)
        decay_col = jnp.broadcast_to(jnp.exp(b_last), (LANES, LANES)).T
        s_scr[h] = decay_col * s_old + upd

        on = _rms(o, og) * _silu(g_raw)
        o_ref[:, pl.ds(off, LANES)] = on.astype(o_ref.dtype)
        return carry

    lax.fori_loop(0, A_HEADS, head_body, 0, unroll=4 if chunk == LANES else A_HEADS)

    @pl.when(c_idx == pl.num_programs(1) - 1)
    def _():
        sfin_ref[0] = s_scr[...]


def _scan_consts(chunk):
    levels = _scan_levels(chunk)
    mrows = max(chunk, 16)
    r = np.arange(LANES)
    tri = (r[None, :] <= r[:, None]).astype(np.float32)[:mrows]
    masks = []
    for m in levels:
        masks.append((r[:mrows, None] // m == r[None, :] // m).astype(np.float32))
    masks.append((r[:mrows, None] == r[None, :]).astype(np.float32))
    return jnp.asarray(tri, BF16), jnp.asarray(np.stack(masks), F32)


def hgrn2_scan(z, s0, lb, o_gain, *, row_off, n_seq, seq_len, chunk):
    n_chunks = seq_len // chunk
    blk_off = row_off // chunk
    tri, bmask = _scan_consts(chunk)
    mrows = max(chunk, 16)
    shared_s0 = s0.shape[0] == 1
    hv = A_HEADS * A_DV
    o, sfin = pl.pallas_call(
        functools.partial(_scan_kernel, chunk=chunk),
        out_shape=(jax.ShapeDtypeStruct((n_seq * seq_len, hv), BF16 if chunk % 16 == 0 else F32),
                   jax.ShapeDtypeStruct((n_seq, A_HEADS, A_DK, A_DV), F32)),
        grid=(n_seq, n_chunks),
        in_specs=[pl.BlockSpec((chunk, 4 * hv), lambda s, c: (blk_off + s * n_chunks + c, 0)),
                  pl.BlockSpec((1, A_HEADS, A_DK, A_DV), (lambda s, c: (0, 0, 0, 0)) if shared_s0
                               else (lambda s, c: (s, 0, 0, 0))),
                  pl.BlockSpec((1, hv), lambda s, c: (0, 0)),
                  pl.BlockSpec((1, A_DV), lambda s, c: (0, 0)),
                  pl.BlockSpec((mrows, LANES), lambda s, c: (0, 0)),
                  pl.BlockSpec(bmask.shape, lambda s, c: (0, 0, 0))],
        out_specs=(pl.BlockSpec((chunk, hv), lambda s, c: (s * n_chunks + c, 0)),
                   pl.BlockSpec((1, A_HEADS, A_DK, A_DV), lambda s, c: (s, 0, 0, 0))),
        scratch_shapes=[pltpu.VMEM((A_HEADS, A_DK, A_DV), F32), pltpu.VMEM((A_HEADS, chunk, LANES), F32)],
        compiler_params=_cparams(("parallel", "arbitrary")),
        name=f"hgrn2_scan_c{chunk}",
    )(z, s0, lb.reshape(1, hv), o_gain.reshape(1, A_DV), tri, bmask)
    return o, sfin


KEYS = 2 * ATT_BLOCK


def _pair_operand(x, kh):
    slab = x[:, (kh // 2) * LANES:(kh // 2 + 1) * LANES]
    lane = lax.broadcasted_iota(jnp.int32, slab.shape, 1)
    if kh % 2 == 0:
        lo = jnp.where(lane < HEAD_DIM, slab, 0.0)
        hi = pltpu.roll(lo, HEAD_DIM, 1)
    else:
        hi = jnp.where(lane >= HEAD_DIM, slab, 0.0)
        lo = pltpu.roll(hi, HEAD_DIM, 1)
    return jnp.concatenate([lo, hi], axis=0).astype(BF16)


def _window_bias(rows, jmin):
    t_i = lax.broadcasted_iota(jnp.int32, (rows, 2 * KEYS), 0)
    c_i = lax.broadcasted_iota(jnp.int32, (rows, 2 * KEYS), 1)
    j_i = c_i & (ATT_BLOCK - 1)
    own = (c_i & ATT_BLOCK) != 0
    ok = (own & (j_i <= t_i)) | (jnp.logical_not(own) & (j_i >= t_i) & (j_i >= jmin))
    return jnp.where(ok, 0.0, -jnp.inf).astype(F32)


def _pair_softmax(s, sink_a, sink_b):
    probs, rinv = [], []
    for hh, sink in enumerate((sink_a, sink_b)):
        sh = s[:, hh * KEYS:(hh + 1) * KEYS]
        m = jnp.maximum(jnp.max(sh, axis=-1, keepdims=True), sink)
        p = jnp.exp(sh - m)
        den = jnp.sum(p, axis=-1, keepdims=True) + jnp.exp(sink - m)
        probs.append(p.astype(BF16))
        rinv.append(1.0 / den)
    lane = lax.broadcasted_iota(jnp.int32, (s.shape[0], LANES), 1)
    return jnp.concatenate(probs, axis=1), jnp.where(lane < HEAD_DIM, rinv[0], rinv[1])


def _attn_prompt_kernel(sink_ref, q_ref, kp_ref, ko_ref, vp_ref, vo_ref, km_ref, vm_ref, o_ref,
                        k2_scr, v2_scr, s_scr, p_scr, r_scr):
    n = pl.program_id(0)
    nbp = SEQ // ATT_BLOCK
    n_pairs = Q_HEADS // 2

    @pl.when(n >= BATCH * nbp)
    def _():
        o_ref[...] = jnp.zeros_like(o_ref)

    @pl.when(n < BATCH * nbp)
    def _():
        first = (n % nbp) == 0
        jmin = jnp.where(first, ATT_BLOCK - N_META, 0)
        k = jnp.concatenate([jnp.where(first, km_ref[...], kp_ref[...]), ko_ref[...]], axis=0)
        v = jnp.concatenate([jnp.where(first, vm_ref[...], vp_ref[...]), vo_ref[...]], axis=0)
        bias = _window_bias(ATT_BLOCK, jmin)
        for kh in range(KV_HEADS):
            k2_scr[kh] = _pair_operand(k, kh)
            v2_scr[kh] = _pair_operand(v, kh)
        for pair in range(n_pairs):
            s_scr[pair] = _nt_dot(q_ref[:, pair * LANES:(pair + 1) * LANES], k2_scr[pair // 2]) + bias
        for pair in range(n_pairs):
            p, rinv = _pair_softmax(s_scr[pair], sink_ref[2 * pair], sink_ref[2 * pair + 1])
            p_scr[pair] = p
            r_scr[pair] = rinv
        for pair in range(n_pairs):
            o = jnp.dot(p_scr[pair], v2_scr[pair // 2], preferred_element_type=F32) * r_scr[pair]
            o_ref[:, pair * LANES:(pair + 1) * LANES] = o.astype(o_ref.dtype)


def attention_prompt(q_all, k_all, v_all, k_meta_blk, v_meta_blk, sinks):
    n_prompt_blocks = T_PROMPT // ATT_BLOCK
    n_blocks = T_ALL // ATT_BLOCK
    n_pairs = Q_HEADS // 2
    own = lambda n, sk: (jnp.minimum(n, n_prompt_blocks - 1), 0)
    prev = lambda n, sk: (jnp.maximum(jnp.minimum(n, n_prompt_blocks - 1) - 1, 0), 0)
    fix = lambda n, sk: (0, 0)
    grid_spec = pltpu.PrefetchScalarGridSpec(
        num_scalar_prefetch=1,
        grid=(n_blocks,),
        in_specs=[pl.BlockSpec((ATT_BLOCK, D_MODEL), own),
                  pl.BlockSpec((ATT_BLOCK, KV_DIM), prev), pl.BlockSpec((ATT_BLOCK, KV_DIM), own),
                  pl.BlockSpec((ATT_BLOCK, KV_DIM), prev), pl.BlockSpec((ATT_BLOCK, KV_DIM), own),
                  pl.BlockSpec((ATT_BLOCK, KV_DIM), fix), pl.BlockSpec((ATT_BLOCK, KV_DIM), fix)],
        out_specs=pl.BlockSpec((ATT_BLOCK, D_MODEL), lambda n, sk: (n, 0)),
        scratch_shapes=[pltpu.VMEM((KV_HEADS, 2 * KEYS, LANES), BF16), pltpu.VMEM((KV_HEADS, 2 * KEYS, LANES), BF16),
                        pltpu.VMEM((n_pairs, ATT_BLOCK, 2 * KEYS), F32),
                        pltpu.VMEM((n_pairs, ATT_BLOCK, 2 * KEYS), BF16),
                        pltpu.VMEM((n_pairs, ATT_BLOCK, LANES), F32)],
    )
    return pl.pallas_call(
        _attn_prompt_kernel,
        out_shape=jax.ShapeDtypeStruct((T_ALL, D_MODEL), BF16),
        grid_spec=grid_spec,
        compiler_params=_cparams(("parallel",)),
        name="attention_prompt",
    )(sinks, q_all, k_all, k_all, v_all, v_all, k_meta_blk, v_meta_blk)


SAMPLE_GROUP = ATT_BLOCK // DEC_SEQ


def _attn_sample_kernel(sink_ref, q_ref, ck_ref, cv_ref, kn_ref, vn_ref, buf_ref, o_ref,
                        qf_scr, of_scr, k2_scr, v2_scr):
    del buf_ref
    qrows = 2 * DEC_SEQ
    qf_scr[...] = q_ref[...].astype(F32)
    bias = _window_bias(qrows, 0)
    zq = jnp.zeros((qrows - DEC_SEQ, D_MODEL), F32)
    zk = jnp.zeros((ATT_BLOCK - DEC_SEQ, KV_DIM), F32)

    def seq_body(i, carry):
        r_new = pl.multiple_of(i * DEC_SEQ, DEC_SEQ)
        r_old = pl.multiple_of(i * WINDOW, WINDOW)
        q = jnp.concatenate([qf_scr[pl.ds(r_new, DEC_SEQ), :], zq], axis=0).astype(BF16)
        k = jnp.concatenate([ck_ref[pl.ds(r_old, WINDOW), :], kn_ref[pl.ds(r_new, DEC_SEQ), :], zk], axis=0)
        v = jnp.concatenate([cv_ref[pl.ds(r_old, WINDOW), :], vn_ref[pl.ds(r_new, DEC_SEQ), :], zk], axis=0)
        for kh in range(KV_HEADS):
            k2_scr[kh] = _pair_operand(k, kh)
            v2_scr[kh] = _pair_operand(v, kh)
        scores = [_nt_dot(q[:, pair * LANES:(pair + 1) * LANES], k2_scr[pair // 2]) + bias
                  for pair in range(Q_HEADS // 2)]
        soft = [_pair_softmax(s, sink_ref[2 * pair], sink_ref[2 * pair + 1]) for pair, s in enumerate(scores)]
        for pair, (p, rinv) in enumerate(soft):
            o = jnp.dot(p, v2_scr[pair // 2], preferred_element_type=F32) * rinv
            of_scr[pl.ds(r_new, DEC_SEQ), pair * LANES:(pair + 1) * LANES] = o[:DEC_SEQ]
        return carry

    lax.fori_loop(0, SAMPLE_GROUP, seq_body, 0)
    o_ref[...] = of_scr[...].astype(o_ref.dtype)


def attention_sample(q_all, cache_k, cache_v, k_all, v_all, sinks, out_buf):
    first_blk = OFF_SAMPLE // ATT_BLOCK
    new = lambda g, sk: (first_blk + g, 0)
    old = lambda g, sk: (g, 0)
    grid_spec = pltpu.PrefetchScalarGridSpec(
        num_scalar_prefetch=1,
        grid=(DEC_BATCH // SAMPLE_GROUP,),
        in_specs=[pl.BlockSpec((ATT_BLOCK, D_MODEL), new),
                  pl.BlockSpec((SAMPLE_GROUP * WINDOW, KV_DIM), old),
                  pl.BlockSpec((SAMPLE_GROUP * WINDOW, KV_DIM), old),
                  pl.BlockSpec((ATT_BLOCK, KV_DIM), new), pl.BlockSpec((ATT_BLOCK, KV_DIM), new),
                  pl.BlockSpec(memory_space=pl.ANY)],
        out_specs=pl.BlockSpec((ATT_BLOCK, D_MODEL), new),
        scratch_shapes=[pltpu.VMEM((ATT_BLOCK, D_MODEL), F32), pltpu.VMEM((ATT_BLOCK, D_MODEL), F32),
                        pltpu.VMEM((KV_HEADS, 2 * KEYS, LANES), BF16), pltpu.VMEM((KV_HEADS, 2 * KEYS, LANES), BF16)],
    )
    return pl.pallas_call(
        _attn_sample_kernel,
        out_shape=jax.ShapeDtypeStruct(out_buf.shape, out_buf.dtype),
        grid_spec=grid_spec,
        input_output_aliases={6: 0},
        compiler_params=_cparams(("parallel",)),
        name="attention_sample",
    )(sinks, q_all, cache_k, cache_v, k_all, v_all, out_buf)


ROUTE_COLS = 8


def _route_kernel(x_ref, g_ref, wr_ref, br_ref, ltri_ref, xn_ref, rec_ref, cnt_ref, cnt_scr):
    i = pl.program_id(0)

    @pl.when(i == 0)
    def _():
        cnt_scr[...] = jnp.zeros_like(cnt_scr)

    xn = _rms(x_ref[...], g_ref[...])
    xn_ref[...] = xn
    logits = jnp.dot(xn, wr_ref[...], preferred_element_type=F32,
                     precision=lax.Precision.HIGHEST) + br_ref[...]
    rows = logits.shape[0]
    lane = lax.broadcasted_iota(jnp.int32, (rows, LANES), 1).astype(F32)
    neg = jnp.float32(-jnp.inf)
    big = jnp.float32(LANES)

    is_g = (lane >= N_EXPERTS) & (lane < N_EXPERTS + N_GROUPS)
    gl = jnp.where(is_g, logits, neg)
    gmax = jnp.max(gl, axis=-1, keepdims=True)
    gsel = jnp.min(jnp.where(gl == gmax, lane, big), axis=-1, keepdims=True) - N_EXPERTS
    gden = jnp.sum(jnp.where(is_g, jnp.exp(gl - gmax), 0.0), axis=-1, keepdims=True)
    gw = 1.0 / gden

    in_grp = (lane >= gsel * EXPERTS_PER_GROUP) & (lane < (gsel + 1) * EXPERTS_PER_GROUP)
    el = jnp.where(in_grp, logits, neg)
    t1 = jnp.max(el, axis=-1, keepdims=True)
    e1 = jnp.min(jnp.where(el == t1, lane, big), axis=-1, keepdims=True)
    el2 = jnp.where(lane == e1, neg, el)
    t2 = jnp.max(el2, axis=-1, keepdims=True)
    e2 = jnp.min(jnp.where(el2 == t2, lane, big), axis=-1, keepdims=True)
    x2 = jnp.exp(t2 - t1)
    w1 = gw / (1.0 + x2)
    w2 = gw * x2 / (1.0 + x2)

    oh1 = (lane == e1).astype(F32)
    oh2 = (lane == e2).astype(F32)
    oh = oh1 + oh2
    before = jnp.dot(ltri_ref[...], oh.astype(BF16), preferred_element_type=F32)
    base = cnt_scr[...] + before
    r1 = jnp.sum(base * oh1, axis=-1, keepdims=True)
    r2 = jnp.sum(base * oh2, axis=-1, keepdims=True)
    cnt_scr[...] = cnt_scr[...] + jnp.sum(oh, axis=0, keepdims=True)

    rec = jnp.where(lane == 0, e1,
          jnp.where(lane == 1, e2,
          jnp.where(lane == 2, r1,
          jnp.where(lane == 3, r2,
          jnp.where(lane == 4, w1,
          jnp.where(lane == 5, w2, 0.0))))))
    rec_ref[...] = rec
    cnt_ref[...] = cnt_scr[...]


def moe_route(h, gain, w_router, b_router, ltri):
    t, d = h.shape
    row = lambda i: (i, 0)
    fix = lambda i: (0, 0)
    return pl.pallas_call(
        _route_kernel,
        out_shape=(jax.ShapeDtypeStruct((t, d), F32), jax.ShapeDtypeStruct((t, LANES), F32),
                   jax.ShapeDtypeStruct((1, LANES), F32)),
        grid=(t // ROW_TILE,),
        in_specs=[pl.BlockSpec((ROW_TILE, d), row), pl.BlockSpec((1, d), fix),
                  pl.BlockSpec((d, LANES), fix), pl.BlockSpec((1, LANES), fix),
                  pl.BlockSpec((ROW_TILE, ROW_TILE), fix)],
        out_specs=(pl.BlockSpec((ROW_TILE, d), row), pl.BlockSpec((ROW_TILE, LANES), row),
                   pl.BlockSpec((1, LANES), fix)),
        scratch_shapes=[pltpu.VMEM((1, LANES), F32)],
        compiler_params=_cparams(("arbitrary",)),
        name="moe_route",
    )(h, gain.reshape(1, d), w_router, b_router, ltri)


def _row_copy(src, src_row, dst, dst_row, sem):
    return pltpu.make_async_copy(src.at[pl.ds(src_row, 1)], dst.at[pl.ds(dst_row, 1)], sem)


def _dispatch_kernel(dest_ref, xn_ref, xs_ref, sem):
    rows = xn_ref.shape[0]

    def issue(r, c):
        _row_copy(xn_ref, r, xs_ref, dest_ref[0, 0, r], sem).start(priority=0)
        _row_copy(xn_ref, r, xs_ref, dest_ref[0, 0, rows + r], sem).start(priority=1)
        return c

    lax.fori_loop(0, rows, issue, 0, unroll=8)
    for _ in range(2):
        pltpu.make_async_copy(xn_ref, xs_ref.at[pl.ds(0, rows)], sem).wait()


def moe_dispatch(xn, dest3, n_slots):
    t, d = xn.shape
    return pl.pallas_call(
        _dispatch_kernel,
        out_shape=jax.ShapeDtypeStruct((n_slots, d), F32),
        grid=(t // ROW_TILE,),
        in_specs=[pl.BlockSpec((1, 1, 2 * ROW_TILE), lambda i: (i, 0, 0), memory_space=pltpu.SMEM),
                  pl.BlockSpec((ROW_TILE, d), lambda i: (i, 0))],
        out_specs=pl.BlockSpec(memory_space=pl.ANY),
        scratch_shapes=[pltpu.SemaphoreType.DMA],
        compiler_params=_cparams(("arbitrary",)),
        name="moe_dispatch",
    )(dest3, xn)


def _ffn_kernel(wblk_ref, we_ref, wlo_ref, whi_ref, xs_ref, w13_ref, w2_ref, ys_ref, w13b, w2b):
    w = pl.program_id(0)
    prev = jnp.maximum(w - 1, 0)
    first_visit = (w == 0) | (wblk_ref[w] != wblk_ref[prev])
    lo = wlo_ref[w]
    hi = whi_ref[w]

    @pl.when(hi > lo)
    def _():
        @pl.when((w == 0) | (we_ref[w] != we_ref[prev]))
        def _():
            w13b[...] = w13_ref[...].astype(BF16)
            w2b[...] = w2_ref[...].astype(BF16)

        au = jnp.dot(xs_ref[...].astype(BF16), w13b[...], preferred_element_type=F32)
        hmid = _silu(au[:, :D_EXPERT]) * au[:, D_EXPERT:]
        y = jnp.dot(hmid.astype(BF16), w2b[...], preferred_element_type=F32)
        row = lax.broadcasted_iota(jnp.int32, y.shape, 0)
        mine = (row >= lo) & (row < hi)

        @pl.when(first_visit)
        def _():
            ys_ref[...] = jnp.where(mine, y, 0.0)

        @pl.when(jnp.logical_not(first_visit))
        def _():
            ys_ref[...] = jnp.where(mine, y, ys_ref[...])


def moe_ffn(xs, work, w13_all, w2_all, layer):
    n_slots, d = xs.shape
    n_work = work[0].shape[0]
    xmap = lambda w, wb, we, wlo, whi: (wb[w], 0)
    w_map = lambda w, wb, we, wlo, whi: (layer, we[w], 0, 0)
    grid_spec = pltpu.PrefetchScalarGridSpec(
        num_scalar_prefetch=4,
        grid=(n_work,),
        in_specs=[pl.BlockSpec((EXPERT_BLOCK, d), xmap),
                  pl.BlockSpec((None, None, d, 2 * D_EXPERT), w_map),
                  pl.BlockSpec((None, None, D_EXPERT, d), w_map)],
        out_specs=pl.BlockSpec((EXPERT_BLOCK, d), xmap),
        scratch_shapes=[pltpu.VMEM((d, 2 * D_EXPERT), BF16), pltpu.VMEM((D_EXPERT, d), BF16)],
    )
    return pl.pallas_call(
        _ffn_kernel,
        out_shape=jax.ShapeDtypeStruct((n_slots, d), F32),
        grid_spec=grid_spec,
        compiler_params=_cparams(("arbitrary",)),
        name="moe_ffn",
    )(*work, xs, w13_all, w2_all)


def _ffn_work_items(cnt):
    n_slots = 2 * T_ALL
    n_blocks = n_slots // EXPERT_BLOCK
    n_work = n_blocks + N_EXPERTS - 1
    end = jnp.cumsum(cnt)
    start = end - cnt
    first_blk = start // EXPERT_BLOCK
    last_blk = jnp.maximum(end - 1, start) // EXPERT_BLOCK
    n_items = jnp.where(cnt > 0, last_blk - first_blk + 1, 0)
    item_end = jnp.cumsum(n_items)
    item_start = item_end - n_items
    w = jnp.arange(n_work, dtype=jnp.int32)
    used = w < item_end[-1]
    wq = jnp.minimum(w, item_end[-1] - 1)
    e = jnp.sum((item_end[None, :] <= wq[:, None]).astype(jnp.int32), axis=1)
    blk = jnp.where(used, first_blk[e] + (w - item_start[e]), n_blocks - 1).astype(jnp.int32)
    lo = jnp.maximum(start[e], blk * EXPERT_BLOCK) - blk * EXPERT_BLOCK
    hi = jnp.minimum(end[e], (blk + 1) * EXPERT_BLOCK) - blk * EXPERT_BLOCK
    lo = jnp.where(used, lo, 0).astype(jnp.int32)
    hi = jnp.where(used, hi, 0).astype(jnp.int32)
    return start, (blk, e, lo, hi)


def _combine_kernel(dest_ref, h_ref, rec_ref, ys_ref, *rest, split):
    out_refs, (g1, g2, sem) = rest[:-3], rest[-3:]
    rows = h_ref.shape[0]
    i = pl.program_id(0)

    def issue(r, c):
        _row_copy(ys_ref, dest_ref[0, 0, r], g1, r, sem).start(priority=0)
        _row_copy(ys_ref, dest_ref[0, 0, rows + r], g2, r, sem).start(priority=1)
        return c

    lax.fori_loop(0, rows, issue, 0, unroll=8)
    for buf in (g1, g2):
        pltpu.make_async_copy(ys_ref.at[pl.ds(0, rows)], buf, sem).wait()
    rec = rec_ref[...]
    res = h_ref[...] + rec[:, 4:5] * g1[...] + rec[:, 5:6] * g2[...]
    if not split:
        out_refs[0][...] = res
    else:
        @pl.when(i < N_PROMPT_TILES)
        def _():
            out_refs[0][...] = res

        @pl.when((i >= N_PROMPT_TILES) & (i < N_PROMPT_TILES + N_SAMPLE_TILES))
        def _():
            out_refs[1][...] = res


def moe_combine(h, rec, ys, dest3, split=False):
    t, d = h.shape
    row = lambda i: (i, 0)
    if split:
        parts = _parts_specs(d)[:2]
        out_shape = (jax.ShapeDtypeStruct((T_PROMPT, d), F32), jax.ShapeDtypeStruct((T_SAMPLE, d), F32))
        out_specs = tuple(parts)
    else:
        out_shape = jax.ShapeDtypeStruct((t, d), F32)
        out_specs = pl.BlockSpec((ROW_TILE, d), row)
    return pl.pallas_call(
        functools.partial(_combine_kernel, split=split),
        out_shape=out_shape,
        grid=(t // ROW_TILE,),
        in_specs=[pl.BlockSpec((1, 1, 2 * ROW_TILE), lambda i: (i, 0, 0), memory_space=pltpu.SMEM),
                  pl.BlockSpec((ROW_TILE, d), row), pl.BlockSpec((ROW_TILE, LANES), row),
                  pl.BlockSpec(memory_space=pl.ANY)],
        out_specs=out_specs,
        scratch_shapes=[pltpu.VMEM((ROW_TILE, d), F32), pltpu.VMEM((ROW_TILE, d), F32),
                        pltpu.SemaphoreType.DMA],
        compiler_params=_cparams(("arbitrary",)),
        name="moe_combine",
    )(dest3, h, rec, ys)


def hier_moe_layer(h, layer, gain, w_group, b_group, w_expert, b_expert, w13_all, w2_all, ltri, split_out=False):
    t = h.shape[0]
    pad = LANES - N_EXPERTS - N_GROUPS
    w_router = jnp.concatenate([w_expert, w_group, jnp.zeros((D_MODEL, pad), F32)], axis=1)
    b_router = jnp.concatenate([b_expert, b_group, jnp.zeros((pad,), F32)]).reshape(1, LANES)
    xn, rec, counts = moe_route(h, gain, w_router, b_router, ltri)

    cnt = counts[0, :N_EXPERTS].astype(jnp.int32)
    start, work = _ffn_work_items(cnt)
    e12 = rec[:, 0:2].astype(jnp.int32)
    r12 = rec[:, 2:4].astype(jnp.int32)
    sel = e12[:, :, None] == jnp.arange(N_EXPERTS, dtype=jnp.int32)[None, None, :]
    dest = jnp.sum(jnp.where(sel, start[None, None, :], 0), axis=-1) + r12
    dest3 = dest.reshape(t // ROW_TILE, ROW_TILE, 2).transpose(0, 2, 1).reshape(t // ROW_TILE, 1, 2 * ROW_TILE)

    xs = moe_dispatch(xn, dest3, 2 * t)
    ys = moe_ffn(xs, work, w13_all, w2_all, layer)
    return moe_combine(h, rec, ys, dest3, split=split_out)


def _rope_tables(pos):
    half = ROPE_DIM // 2
    inv = jnp.exp(-math.log(ROPE_THETA) * jnp.arange(half, dtype=F32) * (2.0 / ROPE_DIM))
    ang = pos.astype(F32)[:, None] * inv[None, :]
    cos, sin = jnp.cos(ang), jnp.sin(ang)
    t = pos.shape[0]
    ones = jnp.ones((t, HEAD_DIM - ROPE_DIM), F32)
    zeros = jnp.zeros((t, HEAD_DIM - ROPE_DIM), F32)
    z8 = jnp.zeros((t, half), F32)
    cos_h = jnp.concatenate([cos, cos, ones], axis=1)
    sina_h = jnp.concatenate([-sin, z8, zeros], axis=1)
    sinb_h = jnp.concatenate([z8, sin, zeros], axis=1)
    two = lambda a: jnp.concatenate([a, a], axis=1)
    return two(cos_h), two(sina_h), two(sinb_h)


def kernel(x_prompt, x_sample, state_hgrn, cache_k_win, cache_v_win, meta_tokens, a_norm, a_w_in, a_lower_logits, a_out_norm, a_w_out, kv_norm, kv_w, k_norm, b_norm, b_wq, b_q_norm, b_sinks, b_w_out, moe_norm, moe_w_group, moe_b_group, moe_w_expert, moe_b_expert, moe_w13, moe_w2):
    tail_rows = T_ALL - OFF_META
    x_parts = (x_prompt.reshape(T_PROMPT, D_MODEL), x_sample.reshape(T_SAMPLE, D_MODEL),
               jnp.concatenate([meta_tokens.astype(F32), jnp.zeros((tail_rows - N_META, D_MODEL), F32)], axis=0))
    pos = jnp.concatenate([N_META + jnp.arange(SEQ, dtype=jnp.int32),
                           jnp.tile(PAST_LEN + jnp.arange(DEC_SEQ, dtype=jnp.int32), ROW_TILE // DEC_SEQ),
                           jnp.arange(N_META, dtype=jnp.int32),
                           jnp.zeros((ROW_TILE - N_META,), jnp.int32)])
    cos_t, sina_t, sinb_t = _rope_tables(pos)
    r256 = np.arange(256)
    hmean = jnp.asarray((r256[:, None] // HEAD_DIM == r256[None, :] // HEAD_DIM).astype(np.float32) / HEAD_DIM, BF16)
    ltri = jnp.asarray((r256[None, :] < r256[:, None]).astype(np.float32), BF16)
    lower = jnp.cumsum(jax.nn.softmax(a_lower_logits.astype(F32), axis=0), axis=0)

    moe = functools.partial(hier_moe_layer, w13_all=moe_w13, w2_all=moe_w2, ltri=ltri)

    z = in_project(x_parts, a_norm[0], a_w_in[0].astype(BF16))
    zero_state = jnp.zeros((1, A_HEADS, A_DK, A_DV), F32)
    o_meta, s_meta = hgrn2_scan(z, zero_state, lower[0], a_out_norm[0],
                                row_off=OFF_META, n_seq=1, seq_len=N_META, chunk=N_META)
    o_prompt, s_prompt = hgrn2_scan(z, s_meta, lower[0], a_out_norm[0],
                                    row_off=0, n_seq=BATCH, seq_len=SEQ, chunk=SCAN_CHUNK)
    o_sample, s_sample = hgrn2_scan(z, state_hgrn[0].astype(F32), lower[0], a_out_norm[0],
                                    row_off=OFF_SAMPLE, n_seq=DEC_BATCH, seq_len=DEC_SEQ, chunk=DEC_SEQ)
    o_tail = jnp.concatenate([o_meta, jnp.zeros((tail_rows - N_META, D_MODEL), BF16)], axis=0)
    h = mixer_out((o_prompt, o_sample, o_tail), a_w_out[0].astype(BF16), x_parts)
    h = moe(h, 0, moe_norm[0], moe_w_group[0], moe_b_group[0], moe_w_expert[0], moe_b_expert[0])

    k_all, v_all = kv_project(h, kv_norm, kv_w.astype(BF16), hmean, jnp.tile(k_norm, KV_HEADS).reshape(1, KV_DIM),
                              cos_t, sina_t, sinb_t)

    q_all = q_project(h, b_norm[0], b_wq[0].astype(BF16), hmean, jnp.tile(b_q_norm[0], Q_HEADS).reshape(1, D_MODEL),
                      cos_t, sina_t, sinb_t)
    meta_blk = lambda a: jnp.concatenate([jnp.zeros((ATT_BLOCK - N_META, KV_DIM), F32),
                                          a[OFF_META:OFF_META + N_META]], axis=0)
    sinks = b_sinks[0].astype(F32)
    att_all = attention_prompt(q_all, k_all, v_all, meta_blk(k_all), meta_blk(v_all), sinks)
    att_all = attention_sample(q_all, cache_k_win.reshape(DEC_BATCH * WINDOW, KV_DIM).astype(F32),
                               cache_v_win.reshape(DEC_BATCH * WINDOW, KV_DIM).astype(F32),
                               k_all, v_all, sinks, att_all)
    h = matmul_residual(att_all, b_w_out[0].astype(BF16), h)
    y_p, y_s = moe(h, 1, moe_norm[1], moe_w_group[1], moe_b_group[1], moe_w_expert[1], moe_b_expert[1],
                   split_out=True)

    y_prompt = y_p.reshape(BATCH, SEQ, D_MODEL)
    y_sample = y_s.reshape(DEC_BATCH, DEC_SEQ, D_MODEL)
    last = lambda a: a[:T_PROMPT].reshape(BATCH, SEQ, KV_DIM)[:, -WINDOW:].reshape(BATCH, WINDOW, KV_HEADS, HEAD_DIM)
    kp = last(k_all)
    vp = last(v_all)
    ks = k_all[OFF_SAMPLE:OFF_SAMPLE + T_SAMPLE].reshape(DEC_BATCH, DEC_SEQ, KV_HEADS, HEAD_DIM)
    vs = v_all[OFF_SAMPLE:OFF_SAMPLE + T_SAMPLE].reshape(DEC_BATCH, DEC_SEQ, KV_HEADS, HEAD_DIM)
    k_win_s = jnp.concatenate([cache_k_win, ks], axis=1)[:, -WINDOW:]
    v_win_s = jnp.concatenate([cache_v_win, vs], axis=1)[:, -WINDOW:]
    return (y_prompt, y_sample, s_prompt[None], s_sample[None], kp, vp, k_win_s, v_win_s)
```

```python
import functools
import math

import numpy as np
import jax
import jax.numpy as jnp
from jax import lax
from jax.experimental import pallas as pl
from jax.experimental.pallas import tpu as pltpu

F32 = jnp.float32
BF16 = jnp.bfloat16

D_MODEL = 1024
BATCH = 4
SEQ = 4096
DEC_BATCH = 128
DEC_SEQ = 8
PAST_LEN = 8192
N_META = 16
A_HEADS = 8
A_DK = 128
A_DV = 128
Q_HEADS = 16
KV_HEADS = 4
HEAD_DIM = 64
KV_DIM = KV_HEADS * HEAD_DIM
WINDOW = 128
ROPE_DIM = 16
ROPE_THETA = 500000.0
N_GROUPS = 4
EXPERTS_PER_GROUP = 8
N_EXPERTS = 32
D_EXPERT = 512
RMS_EPS = 1e-6

LANES = 128
SUBLANES = 8
VMEM_LIMIT = 56 * 1024 * 1024

ROW_TILE = 256
T_PROMPT = BATCH * SEQ
T_SAMPLE = DEC_BATCH * DEC_SEQ
OFF_SAMPLE = T_PROMPT
OFF_META = T_PROMPT + T_SAMPLE
T_REAL = OFF_META + N_META
T_ALL = -(-T_REAL // ROW_TILE) * ROW_TILE
N_TILES = T_ALL // ROW_TILE

SCAN_CHUNK = 128
SCAN_SAMPLE_GROUP = 8
ATT_BLOCK = 128
EXPERT_BLOCK = 256


def _cparams(sem):
    return pltpu.CompilerParams(dimension_semantics=sem, vmem_limit_bytes=VMEM_LIMIT)


def _nt_dot(a, b):
    return lax.dot_general(a, b, (((1,), (1,)), ((), ())), preferred_element_type=F32)


def _rms(x, gain):
    ms = jnp.mean(x * x, axis=-1, keepdims=True)
    return x * lax.rsqrt(ms + RMS_EPS) * gain


def _silu(x):
    return x * jax.nn.sigmoid(x)


N_PROMPT_TILES = T_PROMPT // ROW_TILE
N_SAMPLE_TILES = T_SAMPLE // ROW_TILE


def _parts_specs(width):
    return [pl.BlockSpec((ROW_TILE, width), lambda i: (jnp.minimum(i, N_PROMPT_TILES - 1), 0)),
            pl.BlockSpec((ROW_TILE, width), lambda i: (jnp.clip(i - N_PROMPT_TILES, 0, N_SAMPLE_TILES - 1), 0)),
            pl.BlockSpec((ROW_TILE, width), lambda i: (0, 0))]


def _pick_part(i, p_ref, s_ref, t_ref, dtype):
    return jnp.where(i < N_PROMPT_TILES, p_ref[...].astype(dtype),
                     jnp.where(i < N_PROMPT_TILES + N_SAMPLE_TILES, s_ref[...].astype(dtype),
                               t_ref[...].astype(dtype)))


def _in_proj_kernel(xp_ref, xs_ref, xt_ref, g_ref, w_ref, o_ref):
    x = _pick_part(pl.program_id(0), xp_ref, xs_ref, xt_ref, F32)
    xn = _rms(x, g_ref[...])
    o_ref[...] = jnp.dot(xn.astype(BF16), w_ref[...], preferred_element_type=F32)


def in_project(x_parts, gain, w_bf16):
    d, n = w_bf16.shape
    return pl.pallas_call(
        _in_proj_kernel,
        out_shape=jax.ShapeDtypeStruct((T_ALL, n), F32),
        grid=(N_TILES,),
        in_specs=_parts_specs(d) + [pl.BlockSpec((1, d), lambda i: (0, 0)),
                                    pl.BlockSpec((d, n), lambda i: (0, 0))],
        out_specs=pl.BlockSpec((ROW_TILE, n), lambda i: (i, 0)),
        compiler_params=_cparams(("parallel",)),
        name="in_project",
    )(*x_parts, gain.reshape(1, d), w_bf16)


def _mixer_out_kernel(ap_ref, as_ref, at_ref, w_ref, xp_ref, xs_ref, xt_ref, o_ref):
    i = pl.program_id(0)
    a = _pick_part(i, ap_ref, as_ref, at_ref, BF16)
    x = _pick_part(i, xp_ref, xs_ref, xt_ref, F32)
    o_ref[...] = x + jnp.dot(a, w_ref[...], preferred_element_type=F32)


def mixer_out(a_parts, w_bf16, x_parts):
    k, n = w_bf16.shape
    return pl.pallas_call(
        _mixer_out_kernel,
        out_shape=jax.ShapeDtypeStruct((T_ALL, n), F32),
        grid=(N_TILES,),
        in_specs=_parts_specs(k) + [pl.BlockSpec((k, n), lambda i: (0, 0))] + _parts_specs(n),
        out_specs=pl.BlockSpec((ROW_TILE, n), lambda i: (i, 0)),
        compiler_params=_cparams(("parallel",)),
        name="mixer_out",
    )(*a_parts, w_bf16, *x_parts)


def _head_norm_rope(y, hmean_ref, hgain, cos_t, sina_t, sinb_t):
    rows, width = y.shape
    sq = (y * y).astype(BF16)
    parts = []
    for s in range(width // 256):
        parts.append(jnp.dot(sq[:, s * 256:(s + 1) * 256], hmean_ref[...], preferred_element_type=F32))
    ms = parts[0] if len(parts) == 1 else jnp.concatenate(parts, axis=1)
    yn = y * lax.rsqrt(ms + RMS_EPS) * hgain
    reps = width // LANES
    cos_w = jnp.concatenate([cos_t] * reps, axis=1)
    sina_w = jnp.concatenate([sina_t] * reps, axis=1)
    sinb_w = jnp.concatenate([sinb_t] * reps, axis=1)
    half = ROPE_DIM // 2
    nxt = pltpu.roll(yn, width - half, 1)
    prv = pltpu.roll(yn, half, 1)
    return yn * cos_w + nxt * sina_w + prv * sinb_w


def _kv_kernel(x_ref, g_ref, w_ref, hmean_ref, hg_ref, cos_ref, sina_ref, sinb_ref, k_ref, v_ref):
    xn = _rms(x_ref[...], g_ref[...])
    z = jnp.dot(xn.astype(BF16), w_ref[...], preferred_element_type=F32)
    k = _head_norm_rope(z[:, :KV_DIM], hmean_ref, hg_ref[...], cos_ref[...], sina_ref[...], sinb_ref[...])
    k_ref[...] = k
    v_ref[...] = z[:, KV_DIM:]


def _rope_tile(i):
    tiles_per_seq = SEQ // ROW_TILE
    n_prompt_tiles = T_PROMPT // ROW_TILE
    n_sample_tiles = T_SAMPLE // ROW_TILE
    return (jnp.where(i < n_prompt_tiles, i % tiles_per_seq,
                      jnp.where(i < n_prompt_tiles + n_sample_tiles, tiles_per_seq, tiles_per_seq + 1)), 0)


def kv_project(x, gain, w_bf16, hmean, hgain_w, cos_t, sina_t, sinb_t):
    t, d = x.shape
    row = lambda i: (i, 0)
    fix = lambda i: (0, 0)
    return pl.pallas_call(
        _kv_kernel,
        out_shape=(jax.ShapeDtypeStruct((t, KV_DIM), F32), jax.ShapeDtypeStruct((t, KV_DIM), F32)),
        grid=(t // ROW_TILE,),
        in_specs=[pl.BlockSpec((ROW_TILE, d), row), pl.BlockSpec((1, d), fix),
                  pl.BlockSpec((d, 2 * KV_DIM), fix), pl.BlockSpec((256, 256), fix),
                  pl.BlockSpec((1, KV_DIM), fix),
                  pl.BlockSpec((ROW_TILE, LANES), _rope_tile), pl.BlockSpec((ROW_TILE, LANES), _rope_tile),
                  pl.BlockSpec((ROW_TILE, LANES), _rope_tile)],
        out_specs=(pl.BlockSpec((ROW_TILE, KV_DIM), row), pl.BlockSpec((ROW_TILE, KV_DIM), row)),
        compiler_params=_cparams(("parallel",)),
        name="kv_project",
    )(x, gain.reshape(1, d), w_bf16, hmean, hgain_w, cos_t, sina_t, sinb_t)


def _q_kernel(x_ref, g_ref, w_ref, hmean_ref, hg_ref, cos_ref, sina_ref, sinb_ref, q_ref):
    xn = _rms(x_ref[...], g_ref[...])
    z = jnp.dot(xn.astype(BF16), w_ref[...], preferred_element_type=F32)
    q = _head_norm_rope(z, hmean_ref, hg_ref[...], cos_ref[...], sina_ref[...], sinb_ref[...])
    q_ref[...] = (q * HEAD_DIM ** -0.5).astype(q_ref.dtype)


def q_project(x, gain, w_bf16, hmean, hgain_w, cos_t, sina_t, sinb_t):
    t, d = x.shape
    row = lambda i: (i, 0)
    fix = lambda i: (0, 0)
    return pl.pallas_call(
        _q_kernel,
        out_shape=jax.ShapeDtypeStruct((t, d), BF16),
        grid=(t // ROW_TILE,),
        in_specs=[pl.BlockSpec((ROW_TILE, d), row), pl.BlockSpec((1, d), fix),
                  pl.BlockSpec((d, d), fix), pl.BlockSpec((256, 256), fix),
                  pl.BlockSpec((1, d), fix),
                  pl.BlockSpec((ROW_TILE, LANES), _rope_tile), pl.BlockSpec((ROW_TILE, LANES), _rope_tile),
                  pl.BlockSpec((ROW_TILE, LANES), _rope_tile)],
        out_specs=pl.BlockSpec((ROW_TILE, d), row),
        compiler_params=_cparams(("parallel",)),
        name="q_project",
    )(x, gain.reshape(1, d), w_bf16, hmean, hgain_w, cos_t, sina_t, sinb_t)


def _matmul_residual_kernel(a_ref, w_ref, r_ref, o_ref):
    o_ref[...] = r_ref[...] + jnp.dot(a_ref[...], w_ref[...], preferred_element_type=F32)


def matmul_residual(a_bf16, w_bf16, resid):
    t, k = a_bf16.shape
    n = w_bf16.shape[1]
    return pl.pallas_call(
        _matmul_residual_kernel,
        out_shape=jax.ShapeDtypeStruct((t, n), F32),
        grid=(t // ROW_TILE,),
        in_specs=[pl.BlockSpec((ROW_TILE, k), lambda i: (i, 0)),
                  pl.BlockSpec((k, n), lambda i: (0, 0)),
                  pl.BlockSpec((ROW_TILE, n), lambda i: (i, 0))],
        out_specs=pl.BlockSpec((ROW_TILE, n), lambda i: (i, 0)),
        compiler_params=_cparams(("parallel",)),
        name="matmul_residual",
    )(a_bf16, w_bf16, resid)


def _scan_levels(c):
    levels = []
    m = c
    while m >= 2:
        levels.append(m)
        m //= 2
    return levels


LOG2E = 1.4426950408889634


def _scan_kernel(z_ref, s0_ref, lb_ref, og_ref, tri_ref, lmask_ref, sgn_ref, o_ref, sfin_ref, s_scr, b_scr,
                 *, rows, seq_len):
    c_idx = pl.program_id(1)
    levels = _scan_levels(seq_len)
    n_sub = rows // seq_len
    hk = A_HEADS * A_DK

    @pl.when(c_idx == 0)
    def _():
        s_scr[...] = s0_ref[...]

    sub = lax.broadcasted_iota(jnp.int32, (SUBLANES, LANES), 0)
    row = lax.broadcasted_iota(jnp.int32, (LANES, LANES), 0)
    og = og_ref[...]

    def pad_f32(x):
        if x.shape[0] == LANES:
            return x
        return jnp.concatenate([x, jnp.zeros((LANES - x.shape[0], x.shape[1]), x.dtype)], axis=0)

    def pad_rows(x):
        return pad_f32(x).astype(BF16)

    def cols(part, h):
        return slice(part * hk + h * LANES, part * hk + (h + 1) * LANES)

    def gates(h):
        lb = lb_ref[:, cols(0, h)]
        forget = lb + (1.0 - lb) * jax.nn.sigmoid(z_ref[:, cols(1, h)])
        logf = jnp.log(forget)
        hi = logf.astype(BF16).astype(F32)
        r1 = logf - hi
        mid = r1.astype(BF16).astype(F32)
        lo = r1 - mid
        cs = jnp.dot(tri_ref[...], pad_rows(jnp.concatenate([hi, mid, lo], axis=1)),
                     preferred_element_type=F32)
        b = (cs[:rows, :LANES] + cs[:rows, LANES:2 * LANES]) + cs[:rows, 2 * LANES:]
        b_scr[h] = b
        return _silu(z_ref[:, cols(0, h)]), 1.0 - forget, b

    def bref_for(h, m):
        b_rows = b_scr.at[h]
        half = m // 2
        pieces = []
        for g in range(rows // SUBLANES):
            base = g * SUBLANES
            if m >= SUBLANES:
                r = (base // m) * m + half - 1
                piece = jnp.broadcast_to(b_rows[r:r + 1, :], (SUBLANES, LANES))
            else:
                piece = jnp.broadcast_to(b_rows[base + half - 1:base + half, :], (SUBLANES, LANES))
                for blk in range(1, SUBLANES // m):
                    r = base + blk * m + half - 1
                    piece = jnp.where(sub >= blk * m,
                                      jnp.broadcast_to(b_rows[r:r + 1, :], (SUBLANES, LANES)), piece)
            pieces.append(piece)
        return pieces[0] if len(pieces) == 1 else jnp.concatenate(pieces, axis=0)

    heads = range(A_HEADS)
    qkb = [gates(h) for h in heads]
    att = [_nt_dot(pad_rows(qf), pad_rows(kf)) * lmask_ref[len(levels)] for qf, kf, _ in qkb]
    for li, m in enumerate(levels):
        for h in heads:
            qf, kf, b = qkb[h]
            sgn = sgn_ref[li]
            e = jnp.exp2((b - bref_for(h, m)) * sgn)
            w = pad_rows(jnp.where(sgn > 0, qf, kf) * e)
            att[h] = att[h] + _nt_dot(w, w) * lmask_ref[li]

    def finish(h):
        qf, kf, b = qkb[h]
        b_rows = b_scr.at[h]
        v_b = pad_rows(z_ref[:, cols(2, h)])
        o_intra = jnp.dot(att[h].astype(BF16), v_b, preferred_element_type=F32)
        eb = jnp.exp(b)
        qs = qf * eb
        b_end = [jnp.broadcast_to(b_rows[(i + 1) * seq_len - 1:(i + 1) * seq_len, :], (seq_len, LANES))
                 for i in range(n_sub)]
        b_end = b_end[0] if n_sub == 1 else jnp.concatenate(b_end, axis=0)
        kd_t = pad_f32(kf * jnp.exp(b_end - b)).T.astype(BF16)
        eb_t = pad_f32(eb).T
        qs_b = pad_rows(qs)
        o = o_intra
        for i in range(n_sub):
            s_old = s_scr[i, h]
            first, last = i * seq_len, (i + 1) * seq_len - 1
            if n_sub == 1:
                qs_i, v_i = qs_b, v_b
            else:
                mine = (row >= first) & (row <= last)
                qs_i = jnp.where(mine, qs_b, jnp.zeros_like(qs_b))
                v_i = jnp.where(mine, v_b, jnp.zeros_like(v_b))
            o = o + jnp.dot(qs_i, s_old.astype(BF16), preferred_element_type=F32)
            decay = jnp.broadcast_to(eb_t[:, last:last + 1], (LANES, LANES))
            s_scr[i, h] = decay * s_old + jnp.dot(kd_t, v_i, preferred_element_type=F32)
        o = o[:rows]

        on = _rms(o, og) * _silu(z_ref[:, cols(3, h)])
        o_ref[:, cols(0, h)] = on.astype(o_ref.dtype)

    for h in heads:
        finish(h)

    @pl.when(c_idx == pl.num_programs(1) - 1)
    def _():
        sfin_ref[...] = s_scr[...]


def _scan_consts(rows, seq_len):
    levels = _scan_levels(seq_len)
    r = np.arange(LANES)
    t, s = r[:, None], r[None, :]
    live = (t < rows) & (s < rows)
    tri = ((s <= t) & (t // seq_len == s // seq_len) & live).astype(np.float32)
    masks, sgns = [], []
    for m in levels:
        masks.append(((t // m == s // m) & (t % m >= m // 2) & (s % m < m // 2) & live).astype(np.float32))
        sgns.append(np.broadcast_to(np.where(r[:rows, None] % m >= m // 2, LOG2E, -LOG2E), (rows, LANES)))
    masks.append(((t == s) & live).astype(np.float32))
    return jnp.asarray(tri, BF16), jnp.asarray(np.stack(masks), F32), jnp.asarray(np.stack(sgns), F32)


def hgrn2_scan(z, s0, lb, o_gain, *, row_off, n_seq, seq_len, group=1):
    hv = A_HEADS * A_DV
    if seq_len > SCAN_CHUNK:
        assert group == 1
        sub_len, rows, n_chunks, n_steps = SCAN_CHUNK, SCAN_CHUNK, seq_len // SCAN_CHUNK, n_seq
    else:
        sub_len, rows, n_chunks, n_steps = seq_len, group * seq_len, 1, n_seq // group
    blk_off = row_off // rows
    tri, lmask, sgn = _scan_consts(rows, sub_len)
    shared_s0 = s0.shape[0] == 1
    fix2 = lambda s, c: (0, 0)
    fix3 = lambda s, c: (0, 0, 0)
    o, sfin = pl.pallas_call(
        functools.partial(_scan_kernel, rows=rows, seq_len=sub_len),
        out_shape=(jax.ShapeDtypeStruct((n_seq * seq_len, hv), BF16 if rows % 16 == 0 else F32),
                   jax.ShapeDtypeStruct((n_seq, A_HEADS, A_DK, A_DV), F32)),
        grid=(n_steps, n_chunks),
        in_specs=[pl.BlockSpec((rows, 4 * hv), lambda s, c: (blk_off + s * n_chunks + c, 0)),
                  pl.BlockSpec((group, A_HEADS, A_DK, A_DV), (lambda s, c: (0, 0, 0, 0)) if shared_s0
                               else (lambda s, c: (s, 0, 0, 0))),
                  pl.BlockSpec((1, hv), fix2), pl.BlockSpec((1, A_DV), fix2),
                  pl.BlockSpec((LANES, LANES), fix2), pl.BlockSpec(lmask.shape, fix3),
                  pl.BlockSpec(sgn.shape, fix3)],
        out_specs=(pl.BlockSpec((rows, hv), lambda s, c: (s * n_chunks + c, 0)),
                   pl.BlockSpec((group, A_HEADS, A_DK, A_DV), lambda s, c: (s, 0, 0, 0))),
        scratch_shapes=[pltpu.VMEM((group, A_HEADS, A_DK, A_DV), F32), pltpu.VMEM((A_HEADS, rows, LANES), F32)],
        compiler_params=_cparams(("parallel", "arbitrary")),
        name=f"hgrn2_scan_r{rows}",
    )(z, s0, lb.reshape(1, hv), o_gain.reshape(1, A_DV), tri, lmask, sgn)
    return o, sfin


KEYS = 2 * ATT_BLOCK


def _pair_operand(x, kh):
    slab = x[:, (kh // 2) * LANES:(kh // 2 + 1) * LANES]
    lane = lax.broadcasted_iota(jnp.int32, slab.shape, 1)
    if kh % 2 == 0:
        lo = jnp.where(lane < HEAD_DIM, slab, 0.0)
        hi = pltpu.roll(lo, HEAD_DIM, 1)
    else:
        hi = jnp.where(lane >= HEAD_DIM, slab, 0.0)
        lo = pltpu.roll(hi, HEAD_DIM, 1)
    return jnp.concatenate([lo, hi], axis=0).astype(BF16)


def _window_bias(rows, jmin):
    t_i = lax.broadcasted_iota(jnp.int32, (rows, 2 * KEYS), 0)
    c_i = lax.broadcasted_iota(jnp.int32, (rows, 2 * KEYS), 1)
    j_i = c_i & (ATT_BLOCK - 1)
    own = (c_i & ATT_BLOCK) != 0
    ok = (own & (j_i <= t_i)) | (jnp.logical_not(own) & (j_i >= t_i) & (j_i >= jmin))
    return jnp.where(ok, 0.0, -jnp.inf).astype(F32)


def _pair_softmax(s, sink_a, sink_b):
    probs, rinv = [], []
    for hh, sink in enumerate((sink_a, sink_b)):
        sh = s[:, hh * KEYS:(hh + 1) * KEYS]
        m = jnp.maximum(jnp.max(sh, axis=-1, keepdims=True), sink)
        p = jnp.exp(sh - m)
        den = jnp.sum(p, axis=-1, keepdims=True) + jnp.exp(sink - m)
        probs.append(p.astype(BF16))
        rinv.append(1.0 / den)
    lane = lax.broadcasted_iota(jnp.int32, (s.shape[0], LANES), 1)
    return jnp.concatenate(probs, axis=1), jnp.where(lane < HEAD_DIM, rinv[0], rinv[1])


def _attn_prompt_kernel(sink_ref, q_ref, kp_ref, ko_ref, vp_ref, vo_ref, km_ref, vm_ref, o_ref,
                        k2_scr, v2_scr, s_scr, p_scr, r_scr):
    n = pl.program_id(0)
    nbp = SEQ // ATT_BLOCK
    n_pairs = Q_HEADS // 2

    @pl.when(n >= BATCH * nbp)
    def _():
        o_ref[...] = jnp.zeros_like(o_ref)

    @pl.when(n < BATCH * nbp)
    def _():
        first = (n % nbp) == 0
        jmin = jnp.where(first, ATT_BLOCK - N_META, 0)
        k = jnp.concatenate([jnp.where(first, km_ref[...], kp_ref[...]), ko_ref[...]], axis=0)
        v = jnp.concatenate([jnp.where(first, vm_ref[...], vp_ref[...]), vo_ref[...]], axis=0)
        bias = _window_bias(ATT_BLOCK, jmin)
        for kh in range(KV_HEADS):
            k2_scr[kh] = _pair_operand(k, kh)
            v2_scr[kh] = _pair_operand(v, kh)
        for pair in range(n_pairs):
            s_scr[pair] = _nt_dot(q_ref[:, pair * LANES:(pair + 1) * LANES], k2_scr[pair // 2]) + bias
        for pair in range(n_pairs):
            p, rinv = _pair_softmax(s_scr[pair], sink_ref[2 * pair], sink_ref[2 * pair + 1])
            p_scr[pair] = p
            r_scr[pair] = rinv
        for pair in range(n_pairs):
            o = jnp.dot(p_scr[pair], v2_scr[pair // 2], preferred_element_type=F32) * r_scr[pair]
            o_ref[:, pair * LANES:(pair + 1) * LANES] = o.astype(o_ref.dtype)


def attention_prompt(q_all, k_all, v_all, k_meta_blk, v_meta_blk, sinks):
    n_prompt_blocks = T_PROMPT // ATT_BLOCK
    n_blocks = T_ALL // ATT_BLOCK
    n_pairs = Q_HEADS // 2
    own = lambda n, sk: (jnp.minimum(n, n_prompt_blocks - 1), 0)
    prev = lambda n, sk: (jnp.maximum(jnp.minimum(n, n_prompt_blocks - 1) - 1, 0), 0)
    fix = lambda n, sk: (0, 0)
    grid_spec = pltpu.PrefetchScalarGridSpec(
        num_scalar_prefetch=1,
        grid=(n_blocks,),
        in_specs=[pl.BlockSpec((ATT_BLOCK, D_MODEL), own),
                  pl.BlockSpec((ATT_BLOCK, KV_DIM), prev), pl.BlockSpec((ATT_BLOCK, KV_DIM), own),
                  pl.BlockSpec((ATT_BLOCK, KV_DIM), prev), pl.BlockSpec((ATT_BLOCK, KV_DIM), own),
                  pl.BlockSpec((ATT_BLOCK, KV_DIM), fix), pl.BlockSpec((ATT_BLOCK, KV_DIM), fix)],
        out_specs=pl.BlockSpec((ATT_BLOCK, D_MODEL), lambda n, sk: (n, 0)),
        scratch_shapes=[pltpu.VMEM((KV_HEADS, 2 * KEYS, LANES), BF16), pltpu.VMEM((KV_HEADS, 2 * KEYS, LANES), BF16),
                        pltpu.VMEM((n_pairs, ATT_BLOCK, 2 * KEYS), F32),
                        pltpu.VMEM((n_pairs, ATT_BLOCK, 2 * KEYS), BF16),
                        pltpu.VMEM((n_pairs, ATT_BLOCK, LANES), F32)],
    )
    return pl.pallas_call(
        _attn_prompt_kernel,
        out_shape=jax.ShapeDtypeStruct((T_ALL, D_MODEL), BF16),
        grid_spec=grid_spec,
        compiler_params=_cparams(("parallel",)),
        name="attention_prompt",
    )(sinks, q_all, k_all, k_all, v_all, v_all, k_meta_blk, v_meta_blk)


SAMPLE_GROUP = ATT_BLOCK // DEC_SEQ


def _attn_sample_kernel(sink_ref, q_ref, ck_ref, cv_ref, kn_ref, vn_ref, buf_ref, o_ref,
                        qf_scr, of_scr, k2_scr, v2_scr):
    del buf_ref
    qrows = 2 * DEC_SEQ
    qf_scr[...] = q_ref[...].astype(F32)
    bias = _window_bias(qrows, 0)
    zq = jnp.zeros((qrows - DEC_SEQ, D_MODEL), F32)
    zk = jnp.zeros((ATT_BLOCK - DEC_SEQ, KV_DIM), F32)

    def seq_body(i, carry):
        r_new = pl.multiple_of(i * DEC_SEQ, DEC_SEQ)
        r_old = pl.multiple_of(i * WINDOW, WINDOW)
        q = jnp.concatenate([qf_scr[pl.ds(r_new, DEC_SEQ), :], zq], axis=0).astype(BF16)
        k = jnp.concatenate([ck_ref[pl.ds(r_old, WINDOW), :], kn_ref[pl.ds(r_new, DEC_SEQ), :], zk], axis=0)
        v = jnp.concatenate([cv_ref[pl.ds(r_old, WINDOW), :], vn_ref[pl.ds(r_new, DEC_SEQ), :], zk], axis=0)
        for kh in range(KV_HEADS):
            k2_scr[kh] = _pair_operand(k, kh)
            v2_scr[kh] = _pair_operand(v, kh)
        scores = [_nt_dot(q[:, pair * LANES:(pair + 1) * LANES], k2_scr[pair // 2]) + bias
                  for pair in range(Q_HEADS // 2)]
        soft = [_pair_softmax(s, sink_ref[2 * pair], sink_ref[2 * pair + 1]) for pair, s in enumerate(scores)]
        for pair, (p, rinv) in enumerate(soft):
            o = jnp.dot(p, v2_scr[pair // 2], preferred_element_type=F32) * rinv
            of_scr[pl.ds(r_new, DEC_SEQ), pair * LANES:(pair + 1) * LANES] = o[:DEC_SEQ]
        return carry

    lax.fori_loop(0, SAMPLE_GROUP, seq_body, 0)
    o_ref[...] = of_scr[...].astype(o_ref.dtype)


def attention_sample(q_all, cache_k, cache_v, k_all, v_all, sinks, out_buf):
    first_blk = OFF_SAMPLE // ATT_BLOCK
    new = lambda g, sk: (first_blk + g, 0)
    old = lambda g, sk: (g, 0)
    grid_spec = pltpu.PrefetchScalarGridSpec(
        num_scalar_prefetch=1,
        grid=(DEC_BATCH // SAMPLE_GROUP,),
        in_specs=[pl.BlockSpec((ATT_BLOCK, D_MODEL), new),
                  pl.BlockSpec((SAMPLE_GROUP * WINDOW, KV_DIM), old),
                  pl.BlockSpec((SAMPLE_GROUP * WINDOW, KV_DIM), old),
                  pl.BlockSpec((ATT_BLOCK, KV_DIM), new), pl.BlockSpec((ATT_BLOCK, KV_DIM), new),
                  pl.BlockSpec(memory_space=pl.ANY)],
        out_specs=pl.BlockSpec((ATT_BLOCK, D_MODEL), new),
        scratch_shapes=[pltpu.VMEM((ATT_BLOCK, D_MODEL), F32), pltpu.VMEM((ATT_BLOCK, D_MODEL), F32),
                        pltpu.VMEM((KV_HEADS, 2 * KEYS, LANES), BF16), pltpu.VMEM((KV_HEADS, 2 * KEYS, LANES), BF16)],
    )
    return pl.pallas_call(
        _attn_sample_kernel,
        out_shape=jax.ShapeDtypeStruct(out_buf.shape, out_buf.dtype),
        grid_spec=grid_spec,
        input_output_aliases={6: 0},
        compiler_params=_cparams(("parallel",)),
        name="attention_sample",
    )(sinks, q_all, cache_k, cache_v, k_all, v_all, out_buf)


ROUTE_COLS = 8


def _route_kernel(x_ref, g_ref, wr_ref, br_ref, ltri_ref, xn_ref, rec_ref, cnt_ref, cnt_scr):
    i = pl.program_id(0)

    @pl.when(i == 0)
    def _():
        cnt_scr[...] = jnp.zeros_like(cnt_scr)

    xn = _rms(x_ref[...], g_ref[...])
    xn_ref[...] = xn
    logits = jnp.dot(xn, wr_ref[...], preferred_element_type=F32,
                     precision=lax.Precision.HIGHEST) + br_ref[...]
    rows = logits.shape[0]
    lane = lax.broadcasted_iota(jnp.int32, (rows, LANES), 1).astype(F32)
    neg = jnp.float32(-jnp.inf)
    big = jnp.float32(LANES)

    is_g = (lane >= N_EXPERTS) & (lane < N_EXPERTS + N_GROUPS)
    gl = jnp.where(is_g, logits, neg)
    gmax = jnp.max(gl, axis=-1, keepdims=True)
    gsel = jnp.min(jnp.where(gl == gmax, lane, big), axis=-1, keepdims=True) - N_EXPERTS
    gden = jnp.sum(jnp.where(is_g, jnp.exp(gl - gmax), 0.0), axis=-1, keepdims=True)
    gw = 1.0 / gden

    in_grp = (lane >= gsel * EXPERTS_PER_GROUP) & (lane < (gsel + 1) * EXPERTS_PER_GROUP)
    el = jnp.where(in_grp, logits, neg)
    t1 = jnp.max(el, axis=-1, keepdims=True)
    e1 = jnp.min(jnp.where(el == t1, lane, big), axis=-1, keepdims=True)
    el2 = jnp.where(lane == e1, neg, el)
    t2 = jnp.max(el2, axis=-1, keepdims=True)
    e2 = jnp.min(jnp.where(el2 == t2, lane, big), axis=-1, keepdims=True)
    x2 = jnp.exp(t2 - t1)
    w1 = gw / (1.0 + x2)
    w2 = gw * x2 / (1.0 + x2)

    oh1 = (lane == e1).astype(F32)
    oh2 = (lane == e2).astype(F32)
    oh = oh1 + oh2
    before = jnp.dot(ltri_ref[...], oh.astype(BF16), preferred_element_type=F32)
    base = cnt_scr[...] + before
    r1 = jnp.sum(base * oh1, axis=-1, keepdims=True)
    r2 = jnp.sum(base * oh2, axis=-1, keepdims=True)
    cnt_scr[...] = cnt_scr[...] + jnp.sum(oh, axis=0, keepdims=True)

    rec = jnp.where(lane == 0, e1,
          jnp.where(lane == 1, e2,
          jnp.where(lane == 2, r1,
          jnp.where(lane == 3, r2,
          jnp.where(lane == 4, w1,
          jnp.where(lane == 5, w2, 0.0))))))
    rec_ref[...] = rec
    cnt_ref[...] = cnt_scr[...]


def moe_route(h, gain, w_router, b_router, ltri):
    t, d = h.shape
    row = lambda i: (i, 0)
    fix = lambda i: (0, 0)
    return pl.pallas_call(
        _route_kernel,
        out_shape=(jax.ShapeDtypeStruct((t, d), F32), jax.ShapeDtypeStruct((t, LANES), F32),
                   jax.ShapeDtypeStruct((1, LANES), F32)),
        grid=(t // ROW_TILE,),
        in_specs=[pl.BlockSpec((ROW_TILE, d), row), pl.BlockSpec((1, d), fix),
                  pl.BlockSpec((d, LANES), fix), pl.BlockSpec((1, LANES), fix),
                  pl.BlockSpec((ROW_TILE, ROW_TILE), fix)],
        out_specs=(pl.BlockSpec((ROW_TILE, d), row), pl.BlockSpec((ROW_TILE, LANES), row),
                   pl.BlockSpec((1, LANES), fix)),
        scratch_shapes=[pltpu.VMEM((1, LANES), F32)],
        compiler_params=_cparams(("arbitrary",)),
        name="moe_route",
    )(h, gain.reshape(1, d), w_router, b_router, ltri)


def _row_copy(src, src_row, dst, dst_row, sem):
    return pltpu.make_async_copy(src.at[pl.ds(src_row, 1)], dst.at[pl.ds(dst_row, 1)], sem)


def _dispatch_kernel(dest_ref, xn_ref, xs_ref, sem):
    rows = xn_ref.shape[0]

    def issue(r, c):
        _row_copy(xn_ref, r, xs_ref, dest_ref[0, 0, r], sem).start(priority=0)
        _row_copy(xn_ref, r, xs_ref, dest_ref[0, 0, rows + r], sem).start(priority=1)
        return c

    lax.fori_loop(0, rows, issue, 0, unroll=8)
    for _ in range(2):
        pltpu.make_async_copy(xn_ref, xs_ref.at[pl.ds(0, rows)], sem).wait()


def moe_dispatch(xn, dest3, n_slots):
    t, d = xn.shape
    return pl.pallas_call(
        _dispatch_kernel,
        out_shape=jax.ShapeDtypeStruct((n_slots, d), F32),
        grid=(t // ROW_TILE,),
        in_specs=[pl.BlockSpec((1, 1, 2 * ROW_TILE), lambda i: (i, 0, 0), memory_space=pltpu.SMEM),
                  pl.BlockSpec((ROW_TILE, d), lambda i: (i, 0))],
        out_specs=pl.BlockSpec(memory_space=pl.ANY),
        scratch_shapes=[pltpu.SemaphoreType.DMA],
        compiler_params=_cparams(("arbitrary",)),
        name="moe_dispatch",
    )(dest3, xn)


def _ffn_kernel(wblk_ref, we_ref, wlo_ref, whi_ref, xs_ref, w13_ref, w2_ref, ys_ref, w13b, w2b):
    w = pl.program_id(0)
    prev = jnp.maximum(w - 1, 0)
    first_visit = (w == 0) | (wblk_ref[w] != wblk_ref[prev])
    lo = wlo_ref[w]
    hi = whi_ref[w]

    @pl.when(hi > lo)
    def _():
        @pl.when((w == 0) | (we_ref[w] != we_ref[prev]))
        def _():
            w13b[...] = w13_ref[...].astype(BF16)
            w2b[...] = w2_ref[...].astype(BF16)

        au = jnp.dot(xs_ref[...].astype(BF16), w13b[...], preferred_element_type=F32)
        hmid = _silu(au[:, :D_EXPERT]) * au[:, D_EXPERT:]
        y = jnp.dot(hmid.astype(BF16), w2b[...], preferred_element_type=F32)
        row = lax.broadcasted_iota(jnp.int32, y.shape, 0)
        mine = (row >= lo) & (row < hi)

        @pl.when(first_visit)
        def _():
            ys_ref[...] = jnp.where(mine, y, 0.0)

        @pl.when(jnp.logical_not(first_visit))
        def _():
            ys_ref[...] = jnp.where(mine, y, ys_ref[...])


def moe_ffn(xs, work, w13_all, w2_all, layer):
    n_slots, d = xs.shape
    n_work = work[0].shape[0]
    xmap = lambda w, wb, we, wlo, whi: (wb[w], 0)
    w_map = lambda w, wb, we, wlo, whi: (layer, we[w], 0, 0)
    grid_spec = pltpu.PrefetchScalarGridSpec(
        num_scalar_prefetch=4,
        grid=(n_work,),
        in_specs=[pl.BlockSpec((EXPERT_BLOCK, d), xmap),
                  pl.BlockSpec((None, None, d, 2 * D_EXPERT), w_map),
                  pl.BlockSpec((None, None, D_EXPERT, d), w_map)],
        out_specs=pl.BlockSpec((EXPERT_BLOCK, d), xmap),
        scratch_shapes=[pltpu.VMEM((d, 2 * D_EXPERT), BF16), pltpu.VMEM((D_EXPERT, d), BF16)],
    )
    return pl.pallas_call(
        _ffn_kernel,
        out_shape=jax.ShapeDtypeStruct((n_slots, d), F32),
        grid_spec=grid_spec,
        compiler_params=_cparams(("arbitrary",)),
        name="moe_ffn",
    )(*work, xs, w13_all, w2_all)


def _ffn_work_items(cnt):
    n_slots = 2 * T_ALL
    n_blocks = n_slots // EXPERT_BLOCK
    n_work = n_blocks + N_EXPERTS - 1
    end = jnp.cumsum(cnt)
    start = end - cnt
    first_blk = start // EXPERT_BLOCK
    last_blk = jnp.maximum(end - 1, start) // EXPERT_BLOCK
    n_items = jnp.where(cnt > 0, last_blk - first_blk + 1, 0)
    item_end = jnp.cumsum(n_items)
    item_start = item_end - n_items
    w = jnp.arange(n_work, dtype=jnp.int32)
    used = w < item_end[-1]
    wq = jnp.minimum(w, item_end[-1] - 1)
    e = jnp.sum((item_end[None, :] <= wq[:, None]).astype(jnp.int32), axis=1)
    blk = jnp.where(used, first_blk[e] + (w - item_start[e]), n_blocks - 1).astype(jnp.int32)
    lo = jnp.maximum(start[e], blk * EXPERT_BLOCK) - blk * EXPERT_BLOCK
    hi = jnp.minimum(end[e], (blk + 1) * EXPERT_BLOCK) - blk * EXPERT_BLOCK
    lo = jnp.where(used, lo, 0).astype(jnp.int32)
    hi = jnp.where(used, hi, 0).astype(jnp.int32)
    return start, (blk, e, lo, hi)


def _combine_kernel(dest_ref, h_ref, rec_ref, ys_ref, *rest, split):
    out_refs, (g1, g2, sem) = rest[:-3], rest[-3:]
    rows = h_ref.shape[0]
    i = pl.program_id(0)

    def issue(r, c):
        _row_copy(ys_ref, dest_ref[0, 0, r], g1, r, sem).start(priority=0)
        _row_copy(ys_ref, dest_ref[0, 0, rows + r], g2, r, sem).start(priority=1)
        return c

    lax.fori_loop(0, rows, issue, 0, unroll=8)
    for buf in (g1, g2):
        pltpu.make_async_copy(ys_ref.at[pl.ds(0, rows)], buf, sem).wait()
    rec = rec_ref[...]
    res = h_ref[...] + rec[:, 4:5] * g1[...] + rec[:, 5:6] * g2[...]
    if not split:
        out_refs[0][...] = res
    else:
        @pl.when(i < N_PROMPT_TILES)
        def _():
            out_refs[0][...] = res

        @pl.when((i >= N_PROMPT_TILES) & (i < N_PROMPT_TILES + N_SAMPLE_TILES))
        def _():
            out_refs[1][...] = res


def moe_combine(h, rec, ys, dest3, split=False):
    t, d = h.shape
    row = lambda i: (i, 0)
    if split:
        parts = _parts_specs(d)[:2]
        out_shape = (jax.ShapeDtypeStruct((T_PROMPT, d), F32), jax.ShapeDtypeStruct((T_SAMPLE, d), F32))
        out_specs = tuple(parts)
    else:
        out_shape = jax.ShapeDtypeStruct((t, d), F32)
        out_specs = pl.BlockSpec((ROW_TILE, d), row)
    return pl.pallas_call(
        functools.partial(_combine_kernel, split=split),
        out_shape=out_shape,
        grid=(t // ROW_TILE,),
        in_specs=[pl.BlockSpec((1, 1, 2 * ROW_TILE), lambda i: (i, 0, 0), memory_space=pltpu.SMEM),
                  pl.BlockSpec((ROW_TILE, d), row), pl.BlockSpec((ROW_TILE, LANES), row),
                  pl.BlockSpec(memory_space=pl.ANY)],
        out_specs=out_specs,
        scratch_shapes=[pltpu.VMEM((ROW_TILE, d), F32), pltpu.VMEM((ROW_TILE, d), F32),
                        pltpu.SemaphoreType.DMA],
        compiler_params=_cparams(("arbitrary",)),
        name="moe_combine",
    )(dest3, h, rec, ys)


def hier_moe_layer(h, layer, gain, w_group, b_group, w_expert, b_expert, w13_all, w2_all, ltri, split_out=False):
    t = h.shape[0]
    pad = LANES - N_EXPERTS - N_GROUPS
    w_router = jnp.concatenate([w_expert, w_group, jnp.zeros((D_MODEL, pad), F32)], axis=1)
    b_router = jnp.concatenate([b_expert, b_group, jnp.zeros((pad,), F32)]).reshape(1, LANES)
    xn, rec, counts = moe_route(h, gain, w_router, b_router, ltri)

    cnt = counts[0, :N_EXPERTS].astype(jnp.int32)
    start, work = _ffn_work_items(cnt)
    e12 = rec[:, 0:2].astype(jnp.int32)
    r12 = rec[:, 2:4].astype(jnp.int32)
    sel = e12[:, :, None] == jnp.arange(N_EXPERTS, dtype=jnp.int32)[None, None, :]
    dest = jnp.sum(jnp.where(sel, start[None, None, :], 0), axis=-1) + r12
    dest3 = dest.reshape(t // ROW_TILE, ROW_TILE, 2).transpose(0, 2, 1).reshape(t // ROW_TILE, 1, 2 * ROW_TILE)

    xs = moe_dispatch(xn, dest3, 2 * t)
    ys = moe_ffn(xs, work, w13_all, w2_all, layer)
    return moe_combine(h, rec, ys, dest3, split=split_out)


def _rope_tables(pos):
    half = ROPE_DIM // 2
    inv = jnp.exp(-math.log(ROPE_THETA) * jnp.arange(half, dtype=F32) * (2.0 / ROPE_DIM))
    ang = pos.astype(F32)[:, None] * inv[None, :]
    cos, sin = jnp.cos(ang), jnp.sin(ang)
    t = pos.shape[0]
    ones = jnp.ones((t, HEAD_DIM - ROPE_DIM), F32)
    zeros = jnp.zeros((t, HEAD_DIM - ROPE_DIM), F32)
    z8 = jnp.zeros((t, half), F32)
    cos_h = jnp.concatenate([cos, cos, ones], axis=1)
    sina_h = jnp.concatenate([-sin, z8, zeros], axis=1)
    sinb_h = jnp.concatenate([z8, sin, zeros], axis=1)
    two = lambda a: jnp.concatenate([a, a], axis=1)
    return two(cos_h), two(sina_h), two(sinb_h)


def kernel(x_prompt, x_sample, state_hgrn, cache_k_win, cache_v_win, meta_tokens, a_norm, a_w_in, a_lower_logits, a_out_norm, a_w_out, kv_norm, kv_w, k_norm, b_norm, b_wq, b_q_norm, b_sinks, b_w_out, moe_norm, moe_w_group, moe_b_group, moe_w_expert, moe_b_expert, moe_w13, moe_w2):
    tail_rows = T_ALL - OFF_META
    x_parts = (x_prompt.reshape(T_PROMPT, D_MODEL), x_sample.reshape(T_SAMPLE, D_MODEL),
               jnp.concatenate([meta_tokens.astype(F32), jnp.zeros((tail_rows - N_META, D_MODEL), F32)], axis=0))
    pos = jnp.concatenate([N_META + jnp.arange(SEQ, dtype=jnp.int32),
                           jnp.tile(PAST_LEN + jnp.arange(DEC_SEQ, dtype=jnp.int32), ROW_TILE // DEC_SEQ),
                           jnp.arange(N_META, dtype=jnp.int32),
                           jnp.zeros((ROW_TILE - N_META,), jnp.int32)])
    cos_t, sina_t, sinb_t = _rope_tables(pos)
    r256 = np.arange(256)
    hmean = jnp.asarray((r256[:, None] // HEAD_DIM == r256[None, :] // HEAD_DIM).astype(np.float32) / HEAD_DIM, BF16)
    ltri = jnp.asarray((r256[None, :] < r256[:, None]).astype(np.float32), BF16)
    lower = jnp.cumsum(jax.nn.softmax(a_lower_logits.astype(F32), axis=0), axis=0)

    moe = functools.partial(hier_moe_layer, w13_all=moe_w13, w2_all=moe_w2, ltri=ltri)

    z = in_project(x_parts, a_norm[0], a_w_in[0].astype(BF16))
    zero_state = jnp.zeros((1, A_HEADS, A_DK, A_DV), F32)
    o_meta, s_meta = hgrn2_scan(z, zero_state, lower[0], a_out_norm[0],
                                row_off=OFF_META, n_seq=1, seq_len=N_META)
    o_prompt, s_prompt = hgrn2_scan(z, s_meta, lower[0], a_out_norm[0], row_off=0, n_seq=BATCH, seq_len=SEQ)
    o_sample, s_sample = hgrn2_scan(z, state_hgrn[0].astype(F32), lower[0], a_out_norm[0],
                                    row_off=OFF_SAMPLE, n_seq=DEC_BATCH, seq_len=DEC_SEQ, group=SCAN_SAMPLE_GROUP)
    o_tail = jnp.concatenate([o_meta, jnp.zeros((tail_rows - N_META, D_MODEL), BF16)], axis=0)
    h = mixer_out((o_prompt, o_sample, o_tail), a_w_out[0].astype(BF16), x_parts)
    h = moe(h, 0, moe_norm[0], moe_w_group[0], moe_b_group[0], moe_w_expert[0], moe_b_expert[0])

    k_all, v_all = kv_project(h, kv_norm, kv_w.astype(BF16), hmean, jnp.tile(k_norm, KV_HEADS).reshape(1, KV_DIM),
                              cos_t, sina_t, sinb_t)

    q_all = q_project(h, b_norm[0], b_wq[0].astype(BF16), hmean, jnp.tile(b_q_norm[0], Q_HEADS).reshape(1, D_MODEL),
                      cos_t, sina_t, sinb_t)
    meta_blk = lambda a: jnp.concatenate([jnp.zeros((ATT_BLOCK - N_META, KV_DIM), F32),
                                          a[OFF_META:OFF_META + N_META]], axis=0)
    sinks = b_sinks[0].astype(F32)
    att_all = attention_prompt(q_all, k_all, v_all, meta_blk(k_all), meta_blk(v_all), sinks)
    att_all = attention_sample(q_all, cache_k_win.reshape(DEC_BATCH * WINDOW, KV_DIM).astype(F32),
                               cache_v_win.reshape(DEC_BATCH * WINDOW, KV_DIM).astype(F32),
                               k_all, v_all, sinks, att_all)
    h = matmul_residual(att_all, b_w_out[0].astype(BF16), h)
    y_p, y_s = moe(h, 1, moe_norm[1], moe_w_group[1], moe_b_group[1], moe_w_expert[1], moe_b_expert[1],
                   split_out=True)

    y_prompt = y_p.reshape(BATCH, SEQ, D_MODEL)
    y_sample = y_s.reshape(DEC_BATCH, DEC_SEQ, D_MODEL)
    last = lambda a: a[:T_PROMPT].reshape(BATCH, SEQ, KV_DIM)[:, -WINDOW:].reshape(BATCH, WINDOW, KV_HEADS, HEAD_DIM)
    kp = last(k_all)
    vp = last(v_all)
    ks = k_all[OFF_SAMPLE:OFF_SAMPLE + T_SAMPLE].reshape(DEC_BATCH, DEC_SEQ, KV_HEADS, HEAD_DIM)
    vs = v_all[OFF_SAMPLE:OFF_SAMPLE + T_SAMPLE].reshape(DEC_BATCH, DEC_SEQ, KV_HEADS, HEAD_DIM)
    k_win_s = jnp.concatenate([cache_k_win, ks], axis=1)[:, -WINDOW:]
    v_win_s = jnp.concatenate([cache_v_win, vs], axis=1)[:, -WINDOW:]
    return (y_prompt, y_sample, s_prompt[None], s_sample[None], kp, vp, k_win_s, v_win_s)
```

```python
import functools
import math

import numpy as np
import jax
import jax.numpy as jnp
from jax import lax
from jax.experimental import pallas as pl
from jax.experimental.pallas import tpu as pltpu
from jax.experimental.pallas import tpu_sc as plsc

F32 = jnp.float32
BF16 = jnp.bfloat16

D_MODEL = 1024
BATCH = 4
SEQ = 4096
DEC_BATCH = 128
DEC_SEQ = 8
PAST_LEN = 8192
N_META = 16
A_HEADS = 8
A_DK = 128
A_DV = 128
Q_HEADS = 16
KV_HEADS = 4
HEAD_DIM = 64
KV_DIM = KV_HEADS * HEAD_DIM
WINDOW = 128
ROPE_DIM = 16
ROPE_THETA = 500000.0
N_GROUPS = 4
EXPERTS_PER_GROUP = 8
N_EXPERTS = 32
D_EXPERT = 512
RMS_EPS = 1e-6

LANES = 128
SUBLANES = 8
VMEM_LIMIT = 56 * 1024 * 1024

ROW_TILE = 256
T_PROMPT = BATCH * SEQ
T_SAMPLE = DEC_BATCH * DEC_SEQ
OFF_SAMPLE = T_PROMPT
OFF_META = T_PROMPT + T_SAMPLE
T_REAL = OFF_META + N_META
T_ALL = -(-T_REAL // ROW_TILE) * ROW_TILE
N_TILES = T_ALL // ROW_TILE

SCAN_CHUNK = 128
SCAN_SAMPLE_GROUP = 8
ATT_BLOCK = 128
EXPERT_BLOCK = 256


def _cparams(sem):
    return pltpu.CompilerParams(dimension_semantics=sem, vmem_limit_bytes=VMEM_LIMIT)


def _nt_dot(a, b):
    return lax.dot_general(a, b, (((1,), (1,)), ((), ())), preferred_element_type=F32)


def _rms(x, gain):
    ms = jnp.mean(x * x, axis=-1, keepdims=True)
    return x * lax.rsqrt(ms + RMS_EPS) * gain


def _silu(x):
    return x * jax.nn.sigmoid(x)


N_PROMPT_TILES = T_PROMPT // ROW_TILE
N_SAMPLE_TILES = T_SAMPLE // ROW_TILE


def _parts_specs(width):
    return [pl.BlockSpec((ROW_TILE, width), lambda i: (jnp.minimum(i, N_PROMPT_TILES - 1), 0)),
            pl.BlockSpec((ROW_TILE, width), lambda i: (jnp.clip(i - N_PROMPT_TILES, 0, N_SAMPLE_TILES - 1), 0)),
            pl.BlockSpec((ROW_TILE, width), lambda i: (0, 0))]


def _pick_part(i, p_ref, s_ref, t_ref, dtype):
    return jnp.where(i < N_PROMPT_TILES, p_ref[...].astype(dtype),
                     jnp.where(i < N_PROMPT_TILES + N_SAMPLE_TILES, s_ref[...].astype(dtype),
                               t_ref[...].astype(dtype)))


def _in_proj_kernel(xp_ref, xs_ref, xt_ref, g_ref, w_ref, o_ref):
    x = _pick_part(pl.program_id(0), xp_ref, xs_ref, xt_ref, F32)
    xn = _rms(x, g_ref[...])
    o_ref[...] = jnp.dot(xn.astype(BF16), w_ref[...], preferred_element_type=F32)


def in_project(x_parts, gain, w_bf16):
    d, n = w_bf16.shape
    return pl.pallas_call(
        _in_proj_kernel,
        out_shape=jax.ShapeDtypeStruct((T_ALL, n), F32),
        grid=(N_TILES,),
        in_specs=_parts_specs(d) + [pl.BlockSpec((1, d), lambda i: (0, 0)),
                                    pl.BlockSpec((d, n), lambda i: (0, 0))],
        out_specs=pl.BlockSpec((ROW_TILE, n), lambda i: (i, 0)),
        compiler_params=_cparams(("parallel",)),
        name="in_project",
    )(*x_parts, gain.reshape(1, d), w_bf16)


def _mixer_out_kernel(ap_ref, as_ref, at_ref, w_ref, xp_ref, xs_ref, xt_ref, o_ref):
    i = pl.program_id(0)
    a = _pick_part(i, ap_ref, as_ref, at_ref, BF16)
    x = _pick_part(i, xp_ref, xs_ref, xt_ref, F32)
    o_ref[...] = x + jnp.dot(a, w_ref[...], preferred_element_type=F32)


def mixer_out(a_parts, w_bf16, x_parts):
    k, n = w_bf16.shape
    return pl.pallas_call(
        _mixer_out_kernel,
        out_shape=jax.ShapeDtypeStruct((T_ALL, n), F32),
        grid=(N_TILES,),
        in_specs=_parts_specs(k) + [pl.BlockSpec((k, n), lambda i: (0, 0))] + _parts_specs(n),
        out_specs=pl.BlockSpec((ROW_TILE, n), lambda i: (i, 0)),
        compiler_params=_cparams(("parallel",)),
        name="mixer_out",
    )(*a_parts, w_bf16, *x_parts)


def _head_norm_rope(y, hmean_ref, hgain, cos_t, sina_t, sinb_t):
    rows, width = y.shape
    sq = (y * y).astype(BF16)
    parts = []
    for s in range(width // 256):
        parts.append(jnp.dot(sq[:, s * 256:(s + 1) * 256], hmean_ref[...], preferred_element_type=F32))
    ms = parts[0] if len(parts) == 1 else jnp.concatenate(parts, axis=1)
    yn = y * lax.rsqrt(ms + RMS_EPS) * hgain
    reps = width // LANES
    cos_w = jnp.concatenate([cos_t] * reps, axis=1)
    sina_w = jnp.concatenate([sina_t] * reps, axis=1)
    sinb_w = jnp.concatenate([sinb_t] * reps, axis=1)
    half = ROPE_DIM // 2
    nxt = pltpu.roll(yn, width - half, 1)
    prv = pltpu.roll(yn, half, 1)
    return yn * cos_w + nxt * sina_w + prv * sinb_w


def _kv_kernel(x_ref, g_ref, w_ref, hmean_ref, hg_ref, cos_ref, sina_ref, sinb_ref, k_ref, v_ref):
    xn = _rms(x_ref[...], g_ref[...])
    z = jnp.dot(xn.astype(BF16), w_ref[...], preferred_element_type=F32)
    k = _head_norm_rope(z[:, :KV_DIM], hmean_ref, hg_ref[...], cos_ref[...], sina_ref[...], sinb_ref[...])
    k_ref[...] = k
    v_ref[...] = z[:, KV_DIM:]


def _rope_tile(i):
    tiles_per_seq = SEQ // ROW_TILE
    n_prompt_tiles = T_PROMPT // ROW_TILE
    n_sample_tiles = T_SAMPLE // ROW_TILE
    return (jnp.where(i < n_prompt_tiles, i % tiles_per_seq,
                      jnp.where(i < n_prompt_tiles + n_sample_tiles, tiles_per_seq, tiles_per_seq + 1)), 0)


def kv_project(x, gain, w_bf16, hmean, hgain_w, cos_t, sina_t, sinb_t):
    t, d = x.shape
    row = lambda i: (i, 0)
    fix = lambda i: (0, 0)
    return pl.pallas_call(
        _kv_kernel,
        out_shape=(jax.ShapeDtypeStruct((t, KV_DIM), F32), jax.ShapeDtypeStruct((t, KV_DIM), F32)),
        grid=(t // ROW_TILE,),
        in_specs=[pl.BlockSpec((ROW_TILE, d), row), pl.BlockSpec((1, d), fix),
                  pl.BlockSpec((d, 2 * KV_DIM), fix), pl.BlockSpec((256, 256), fix),
                  pl.BlockSpec((1, KV_DIM), fix),
                  pl.BlockSpec((ROW_TILE, LANES), _rope_tile), pl.BlockSpec((ROW_TILE, LANES), _rope_tile),
                  pl.BlockSpec((ROW_TILE, LANES), _rope_tile)],
        out_specs=(pl.BlockSpec((ROW_TILE, KV_DIM), row), pl.BlockSpec((ROW_TILE, KV_DIM), row)),
        compiler_params=_cparams(("parallel",)),
        name="kv_project",
    )(x, gain.reshape(1, d), w_bf16, hmean, hgain_w, cos_t, sina_t, sinb_t)


def _q_kernel(x_ref, g_ref, w_ref, hmean_ref, hg_ref, cos_ref, sina_ref, sinb_ref, q_ref):
    xn = _rms(x_ref[...], g_ref[...])
    z = jnp.dot(xn.astype(BF16), w_ref[...], preferred_element_type=F32)
    q = _head_norm_rope(z, hmean_ref, hg_ref[...], cos_ref[...], sina_ref[...], sinb_ref[...])
    q_ref[...] = (q * HEAD_DIM ** -0.5).astype(q_ref.dtype)


def q_project(x, gain, w_bf16, hmean, hgain_w, cos_t, sina_t, sinb_t):
    t, d = x.shape
    row = lambda i: (i, 0)
    fix = lambda i: (0, 0)
    return pl.pallas_call(
        _q_kernel,
        out_shape=jax.ShapeDtypeStruct((t, d), BF16),
        grid=(t // ROW_TILE,),
        in_specs=[pl.BlockSpec((ROW_TILE, d), row), pl.BlockSpec((1, d), fix),
                  pl.BlockSpec((d, d), fix), pl.BlockSpec((256, 256), fix),
                  pl.BlockSpec((1, d), fix),
                  pl.BlockSpec((ROW_TILE, LANES), _rope_tile), pl.BlockSpec((ROW_TILE, LANES), _rope_tile),
                  pl.BlockSpec((ROW_TILE, LANES), _rope_tile)],
        out_specs=pl.BlockSpec((ROW_TILE, d), row),
        compiler_params=_cparams(("parallel",)),
        name="q_project",
    )(x, gain.reshape(1, d), w_bf16, hmean, hgain_w, cos_t, sina_t, sinb_t)


def _matmul_residual_kernel(a_ref, w_ref, r_ref, o_ref):
    o_ref[...] = r_ref[...] + jnp.dot(a_ref[...], w_ref[...], preferred_element_type=F32)


def matmul_residual(a_bf16, w_bf16, resid):
    t, k = a_bf16.shape
    n = w_bf16.shape[1]
    return pl.pallas_call(
        _matmul_residual_kernel,
        out_shape=jax.ShapeDtypeStruct((t, n), F32),
        grid=(t // ROW_TILE,),
        in_specs=[pl.BlockSpec((ROW_TILE, k), lambda i: (i, 0)),
                  pl.BlockSpec((k, n), lambda i: (0, 0)),
                  pl.BlockSpec((ROW_TILE, n), lambda i: (i, 0))],
        out_specs=pl.BlockSpec((ROW_TILE, n), lambda i: (i, 0)),
        compiler_params=_cparams(("parallel",)),
        name="matmul_residual",
    )(a_bf16, w_bf16, resid)


def _scan_levels(c):
    levels = []
    m = c
    while m >= 2:
        levels.append(m)
        m //= 2
    return levels


LOG2E = 1.4426950408889634


def _scan_kernel(z_ref, s0_ref, lb_ref, og_ref, tri_ref, lmask_ref, sgn_ref, o_ref, sfin_ref, s_scr, b_scr,
                 *, rows, seq_len):
    c_idx = pl.program_id(1)
    levels = _scan_levels(seq_len)
    n_sub = rows // seq_len
    hk = A_HEADS * A_DK

    @pl.when(c_idx == 0)
    def _():
        s_scr[...] = s0_ref[...]

    sub = lax.broadcasted_iota(jnp.int32, (SUBLANES, LANES), 0)
    row = lax.broadcasted_iota(jnp.int32, (LANES, LANES), 0)
    og = og_ref[...]

    def pad_f32(x):
        if x.shape[0] == LANES:
            return x
        return jnp.concatenate([x, jnp.zeros((LANES - x.shape[0], x.shape[1]), x.dtype)], axis=0)

    def pad_rows(x):
        return pad_f32(x).astype(BF16)

    def cols(part, h):
        return slice(part * hk + h * LANES, part * hk + (h + 1) * LANES)

    def gates(h):
        lb = lb_ref[:, cols(0, h)]
        forget = lb + (1.0 - lb) * jax.nn.sigmoid(z_ref[:, cols(1, h)])
        logf = jnp.log(forget)
        hi = logf.astype(BF16).astype(F32)
        r1 = logf - hi
        mid = r1.astype(BF16).astype(F32)
        lo = r1 - mid
        cs = jnp.dot(tri_ref[...], pad_rows(jnp.concatenate([hi, mid, lo], axis=1)),
                     preferred_element_type=F32)
        b = (cs[:rows, :LANES] + cs[:rows, LANES:2 * LANES]) + cs[:rows, 2 * LANES:]
        b_scr[h] = b
        return _silu(z_ref[:, cols(0, h)]), 1.0 - forget, b

    def bref_for(h, m):
        b_rows = b_scr.at[h]
        half = m // 2
        pieces = []
        for g in range(rows // SUBLANES):
            base = g * SUBLANES
            if m >= SUBLANES:
                r = (base // m) * m + half - 1
                piece = jnp.broadcast_to(b_rows[r:r + 1, :], (SUBLANES, LANES))
            else:
                piece = jnp.broadcast_to(b_rows[base + half - 1:base + half, :], (SUBLANES, LANES))
                for blk in range(1, SUBLANES // m):
                    r = base + blk * m + half - 1
                    piece = jnp.where(sub >= blk * m,
                                      jnp.broadcast_to(b_rows[r:r + 1, :], (SUBLANES, LANES)), piece)
            pieces.append(piece)
        return pieces[0] if len(pieces) == 1 else jnp.concatenate(pieces, axis=0)

    heads = range(A_HEADS)
    qkb = [gates(h) for h in heads]
    att = [_nt_dot(pad_rows(qf), pad_rows(kf)) * lmask_ref[len(levels)] for qf, kf, _ in qkb]
    for li, m in enumerate(levels):
        for h in heads:
            qf, kf, b = qkb[h]
            sgn = sgn_ref[li]
            e = jnp.exp2((b - bref_for(h, m)) * sgn)
            w = pad_rows(jnp.where(sgn > 0, qf, kf) * e)
            att[h] = att[h] + _nt_dot(w, w) * lmask_ref[li]

    def finish(h):
        qf, kf, b = qkb[h]
        b_rows = b_scr.at[h]
        v_b = pad_rows(z_ref[:, cols(2, h)])
        o_intra = jnp.dot(att[h].astype(BF16), v_b, preferred_element_type=F32)
        eb = jnp.exp(b)
        qs = qf * eb
        b_end = [jnp.broadcast_to(b_rows[(i + 1) * seq_len - 1:(i + 1) * seq_len, :], (seq_len, LANES))
                 for i in range(n_sub)]
        b_end = b_end[0] if n_sub == 1 else jnp.concatenate(b_end, axis=0)
        kd_t = pad_f32(kf * jnp.exp(b_end - b)).T.astype(BF16)
        eb_t = pad_f32(eb).T
        qs_b = pad_rows(qs)
        o = o_intra
        for i in range(n_sub):
            s_old = s_scr[i, h]
            first, last = i * seq_len, (i + 1) * seq_len - 1
            if n_sub == 1:
                qs_i, v_i = qs_b, v_b
            else:
                mine = (row >= first) & (row <= last)
                qs_i = jnp.where(mine, qs_b, jnp.zeros_like(qs_b))
                v_i = jnp.where(mine, v_b, jnp.zeros_like(v_b))
            o = o + jnp.dot(qs_i, s_old.astype(BF16), preferred_element_type=F32)
            decay = jnp.broadcast_to(eb_t[:, last:last + 1], (LANES, LANES))
            s_scr[i, h] = decay * s_old + jnp.dot(kd_t, v_i, preferred_element_type=F32)
        o = o[:rows]

        on = _rms(o, og) * _silu(z_ref[:, cols(3, h)])
        o_ref[:, cols(0, h)] = on.astype(o_ref.dtype)

    for h in heads:
        finish(h)

    @pl.when(c_idx == pl.num_programs(1) - 1)
    def _():
        sfin_ref[...] = s_scr[...]


def _scan_consts(rows, seq_len):
    levels = _scan_levels(seq_len)
    r = np.arange(LANES)
    t, s = r[:, None], r[None, :]
    live = (t < rows) & (s < rows)
    tri = ((s <= t) & (t // seq_len == s // seq_len) & live).astype(np.float32)
    masks, sgns = [], []
    for m in levels:
        masks.append(((t // m == s // m) & (t % m >= m // 2) & (s % m < m // 2) & live).astype(np.float32))
        sgns.append(np.broadcast_to(np.where(r[:rows, None] % m >= m // 2, LOG2E, -LOG2E), (rows, LANES)))
    masks.append(((t == s) & live).astype(np.float32))
    return jnp.asarray(tri, BF16), jnp.asarray(np.stack(masks), F32), jnp.asarray(np.stack(sgns), F32)


def hgrn2_scan(z, s0, lb, o_gain, *, row_off, n_seq, seq_len, group=1):
    hv = A_HEADS * A_DV
    if seq_len > SCAN_CHUNK:
        assert group == 1
        sub_len, rows, n_chunks, n_steps = SCAN_CHUNK, SCAN_CHUNK, seq_len // SCAN_CHUNK, n_seq
    else:
        sub_len, rows, n_chunks, n_steps = seq_len, group * seq_len, 1, n_seq // group
    blk_off = row_off // rows
    tri, lmask, sgn = _scan_consts(rows, sub_len)
    shared_s0 = s0.shape[0] == 1
    fix2 = lambda s, c: (0, 0)
    fix3 = lambda s, c: (0, 0, 0)
    o, sfin = pl.pallas_call(
        functools.partial(_scan_kernel, rows=rows, seq_len=sub_len),
        out_shape=(jax.ShapeDtypeStruct((n_seq * seq_len, hv), BF16 if rows % 16 == 0 else F32),
                   jax.ShapeDtypeStruct((n_seq, A_HEADS, A_DK, A_DV), F32)),
        grid=(n_steps, n_chunks),
        in_specs=[pl.BlockSpec((rows, 4 * hv), lambda s, c: (blk_off + s * n_chunks + c, 0)),
                  pl.BlockSpec((group, A_HEADS, A_DK, A_DV), (lambda s, c: (0, 0, 0, 0)) if shared_s0
                               else (lambda s, c: (s, 0, 0, 0))),
                  pl.BlockSpec((1, hv), fix2), pl.BlockSpec((1, A_DV), fix2),
                  pl.BlockSpec((LANES, LANES), fix2), pl.BlockSpec(lmask.shape, fix3),
                  pl.BlockSpec(sgn.shape, fix3)],
        out_specs=(pl.BlockSpec((rows, hv), lambda s, c: (s * n_chunks + c, 0)),
                   pl.BlockSpec((group, A_HEADS, A_DK, A_DV), lambda s, c: (s, 0, 0, 0))),
        scratch_shapes=[pltpu.VMEM((group, A_HEADS, A_DK, A_DV), F32), pltpu.VMEM((A_HEADS, rows, LANES), F32)],
        compiler_params=_cparams(("parallel", "arbitrary")),
        name=f"hgrn2_scan_r{rows}",
    )(z, s0, lb.reshape(1, hv), o_gain.reshape(1, A_DV), tri, lmask, sgn)
    return o, sfin


KEYS = 2 * ATT_BLOCK


def _pair_operand(x, kh):
    slab = x[:, (kh // 2) * LANES:(kh // 2 + 1) * LANES]
    lane = lax.broadcasted_iota(jnp.int32, slab.shape, 1)
    if kh % 2 == 0:
        lo = jnp.where(lane < HEAD_DIM, slab, 0.0)
        hi = pltpu.roll(lo, HEAD_DIM, 1)
    else:
        hi = jnp.where(lane >= HEAD_DIM, slab, 0.0)
        lo = pltpu.roll(hi, HEAD_DIM, 1)
    return jnp.concatenate([lo, hi], axis=0).astype(BF16)


def _window_bias(rows, jmin):
    t_i = lax.broadcasted_iota(jnp.int32, (rows, 2 * KEYS), 0)
    c_i = lax.broadcasted_iota(jnp.int32, (rows, 2 * KEYS), 1)
    j_i = c_i & (ATT_BLOCK - 1)
    own = (c_i & ATT_BLOCK) != 0
    ok = (own & (j_i <= t_i)) | (jnp.logical_not(own) & (j_i >= t_i) & (j_i >= jmin))
    return jnp.where(ok, 0.0, -jnp.inf).astype(F32)


def _pair_softmax(s, sink_a, sink_b):
    probs, rinv = [], []
    for hh, sink in enumerate((sink_a, sink_b)):
        sh = s[:, hh * KEYS:(hh + 1) * KEYS]
        m = jnp.maximum(jnp.max(sh, axis=-1, keepdims=True), sink)
        p = jnp.exp(sh - m)
        den = jnp.sum(p, axis=-1, keepdims=True) + jnp.exp(sink - m)
        probs.append(p.astype(BF16))
        rinv.append(1.0 / den)
    lane = lax.broadcasted_iota(jnp.int32, (s.shape[0], LANES), 1)
    return jnp.concatenate(probs, axis=1), jnp.where(lane < HEAD_DIM, rinv[0], rinv[1])


def _attn_prompt_kernel(sink_ref, q_ref, kp_ref, ko_ref, vp_ref, vo_ref, km_ref, vm_ref, o_ref,
                        k2_scr, v2_scr, s_scr, p_scr, r_scr):
    n = pl.program_id(0)
    nbp = SEQ // ATT_BLOCK
    n_pairs = Q_HEADS // 2

    @pl.when(n >= BATCH * nbp)
    def _():
        o_ref[...] = jnp.zeros_like(o_ref)

    @pl.when(n < BATCH * nbp)
    def _():
        first = (n % nbp) == 0
        jmin = jnp.where(first, ATT_BLOCK - N_META, 0)
        k = jnp.concatenate([jnp.where(first, km_ref[...], kp_ref[...]), ko_ref[...]], axis=0)
        v = jnp.concatenate([jnp.where(first, vm_ref[...], vp_ref[...]), vo_ref[...]], axis=0)
        bias = _window_bias(ATT_BLOCK, jmin)
        for kh in range(KV_HEADS):
            k2_scr[kh] = _pair_operand(k, kh)
            v2_scr[kh] = _pair_operand(v, kh)
        for pair in range(n_pairs):
            s_scr[pair] = _nt_dot(q_ref[:, pair * LANES:(pair + 1) * LANES], k2_scr[pair // 2]) + bias
        for pair in range(n_pairs):
            p, rinv = _pair_softmax(s_scr[pair], sink_ref[2 * pair], sink_ref[2 * pair + 1])
            p_scr[pair] = p
            r_scr[pair] = rinv
        for pair in range(n_pairs):
            o = jnp.dot(p_scr[pair], v2_scr[pair // 2], preferred_element_type=F32) * r_scr[pair]
            o_ref[:, pair * LANES:(pair + 1) * LANES] = o.astype(o_ref.dtype)


def attention_prompt(q_all, k_all, v_all, k_meta_blk, v_meta_blk, sinks):
    n_prompt_blocks = T_PROMPT // ATT_BLOCK
    n_blocks = T_ALL // ATT_BLOCK
    n_pairs = Q_HEADS // 2
    own = lambda n, sk: (jnp.minimum(n, n_prompt_blocks - 1), 0)
    prev = lambda n, sk: (jnp.maximum(jnp.minimum(n, n_prompt_blocks - 1) - 1, 0), 0)
    fix = lambda n, sk: (0, 0)
    grid_spec = pltpu.PrefetchScalarGridSpec(
        num_scalar_prefetch=1,
        grid=(n_blocks,),
        in_specs=[pl.BlockSpec((ATT_BLOCK, D_MODEL), own),
                  pl.BlockSpec((ATT_BLOCK, KV_DIM), prev), pl.BlockSpec((ATT_BLOCK, KV_DIM), own),
                  pl.BlockSpec((ATT_BLOCK, KV_DIM), prev), pl.BlockSpec((ATT_BLOCK, KV_DIM), own),
                  pl.BlockSpec((ATT_BLOCK, KV_DIM), fix), pl.BlockSpec((ATT_BLOCK, KV_DIM), fix)],
        out_specs=pl.BlockSpec((ATT_BLOCK, D_MODEL), lambda n, sk: (n, 0)),
        scratch_shapes=[pltpu.VMEM((KV_HEADS, 2 * KEYS, LANES), BF16), pltpu.VMEM((KV_HEADS, 2 * KEYS, LANES), BF16),
                        pltpu.VMEM((n_pairs, ATT_BLOCK, 2 * KEYS), F32),
                        pltpu.VMEM((n_pairs, ATT_BLOCK, 2 * KEYS), BF16),
                        pltpu.VMEM((n_pairs, ATT_BLOCK, LANES), F32)],
    )
    return pl.pallas_call(
        _attn_prompt_kernel,
        out_shape=jax.ShapeDtypeStruct((T_ALL, D_MODEL), BF16),
        grid_spec=grid_spec,
        compiler_params=_cparams(("parallel",)),
        name="attention_prompt",
    )(sinks, q_all, k_all, k_all, v_all, v_all, k_meta_blk, v_meta_blk)


SAMPLE_GROUP = ATT_BLOCK // DEC_SEQ


def _attn_sample_kernel(sink_ref, q_ref, ck_ref, cv_ref, kn_ref, vn_ref, buf_ref, o_ref,
                        qf_scr, of_scr, k2_scr, v2_scr):
    del buf_ref
    qrows = 2 * DEC_SEQ
    qf_scr[...] = q_ref[...].astype(F32)
    bias = _window_bias(qrows, 0)
    zq = jnp.zeros((qrows - DEC_SEQ, D_MODEL), F32)
    zk = jnp.zeros((ATT_BLOCK - DEC_SEQ, KV_DIM), F32)

    def seq_body(i, carry):
        r_new = pl.multiple_of(i * DEC_SEQ, DEC_SEQ)
        r_old = pl.multiple_of(i * WINDOW, WINDOW)
        q = jnp.concatenate([qf_scr[pl.ds(r_new, DEC_SEQ), :], zq], axis=0).astype(BF16)
        k = jnp.concatenate([ck_ref[pl.ds(r_old, WINDOW), :], kn_ref[pl.ds(r_new, DEC_SEQ), :], zk], axis=0)
        v = jnp.concatenate([cv_ref[pl.ds(r_old, WINDOW), :], vn_ref[pl.ds(r_new, DEC_SEQ), :], zk], axis=0)
        for kh in range(KV_HEADS):
            k2_scr[kh] = _pair_operand(k, kh)
            v2_scr[kh] = _pair_operand(v, kh)
        scores = [_nt_dot(q[:, pair * LANES:(pair + 1) * LANES], k2_scr[pair // 2]) + bias
                  for pair in range(Q_HEADS // 2)]
        soft = [_pair_softmax(s, sink_ref[2 * pair], sink_ref[2 * pair + 1]) for pair, s in enumerate(scores)]
        for pair, (p, rinv) in enumerate(soft):
            o = jnp.dot(p, v2_scr[pair // 2], preferred_element_type=F32) * rinv
            of_scr[pl.ds(r_new, DEC_SEQ), pair * LANES:(pair + 1) * LANES] = o[:DEC_SEQ]
        return carry

    lax.fori_loop(0, SAMPLE_GROUP, seq_body, 0)
    o_ref[...] = of_scr[...].astype(o_ref.dtype)


def attention_sample(q_all, cache_k, cache_v, k_all, v_all, sinks, out_buf):
    first_blk = OFF_SAMPLE // ATT_BLOCK
    new = lambda g, sk: (first_blk + g, 0)
    old = lambda g, sk: (g, 0)
    grid_spec = pltpu.PrefetchScalarGridSpec(
        num_scalar_prefetch=1,
        grid=(DEC_BATCH // SAMPLE_GROUP,),
        in_specs=[pl.BlockSpec((ATT_BLOCK, D_MODEL), new),
                  pl.BlockSpec((SAMPLE_GROUP * WINDOW, KV_DIM), old),
                  pl.BlockSpec((SAMPLE_GROUP * WINDOW, KV_DIM), old),
                  pl.BlockSpec((ATT_BLOCK, KV_DIM), new), pl.BlockSpec((ATT_BLOCK, KV_DIM), new),
                  pl.BlockSpec(memory_space=pl.ANY)],
        out_specs=pl.BlockSpec((ATT_BLOCK, D_MODEL), new),
        scratch_shapes=[pltpu.VMEM((ATT_BLOCK, D_MODEL), F32), pltpu.VMEM((ATT_BLOCK, D_MODEL), F32),
                        pltpu.VMEM((KV_HEADS, 2 * KEYS, LANES), BF16), pltpu.VMEM((KV_HEADS, 2 * KEYS, LANES), BF16)],
    )
    return pl.pallas_call(
        _attn_sample_kernel,
        out_shape=jax.ShapeDtypeStruct(out_buf.shape, out_buf.dtype),
        grid_spec=grid_spec,
        input_output_aliases={6: 0},
        compiler_params=_cparams(("parallel",)),
        name="attention_sample",
    )(sinks, q_all, cache_k, cache_v, k_all, v_all, out_buf)


ROUTE_COLS = 8


def _route_kernel(x_ref, g_ref, wr_ref, br_ref, ltri_ref, xn_ref, rec_ref, cnt_ref, cnt_scr):
    i = pl.program_id(0)

    @pl.when(i == 0)
    def _():
        cnt_scr[...] = jnp.zeros_like(cnt_scr)

    xn = _rms(x_ref[...], g_ref[...])
    xn_ref[...] = xn
    logits = jnp.dot(xn, wr_ref[...], preferred_element_type=F32,
                     precision=lax.Precision.HIGHEST) + br_ref[...]
    rows = logits.shape[0]
    lane = lax.broadcasted_iota(jnp.int32, (rows, LANES), 1).astype(F32)
    neg = jnp.float32(-jnp.inf)
    big = jnp.float32(LANES)

    is_g = (lane >= N_EXPERTS) & (lane < N_EXPERTS + N_GROUPS)
    gl = jnp.where(is_g, logits, neg)
    gmax = jnp.max(gl, axis=-1, keepdims=True)
    gsel = jnp.min(jnp.where(gl == gmax, lane, big), axis=-1, keepdims=True) - N_EXPERTS
    gden = jnp.sum(jnp.where(is_g, jnp.exp(gl - gmax), 0.0), axis=-1, keepdims=True)
    gw = 1.0 / gden

    in_grp = (lane >= gsel * EXPERTS_PER_GROUP) & (lane < (gsel + 1) * EXPERTS_PER_GROUP)
    el = jnp.where(in_grp, logits, neg)
    t1 = jnp.max(el, axis=-1, keepdims=True)
    e1 = jnp.min(jnp.where(el == t1, lane, big), axis=-1, keepdims=True)
    el2 = jnp.where(lane == e1, neg, el)
    t2 = jnp.max(el2, axis=-1, keepdims=True)
    e2 = jnp.min(jnp.where(el2 == t2, lane, big), axis=-1, keepdims=True)
    x2 = jnp.exp(t2 - t1)
    w1 = gw / (1.0 + x2)
    w2 = gw * x2 / (1.0 + x2)

    oh1 = (lane == e1).astype(F32)
    oh2 = (lane == e2).astype(F32)
    oh = oh1 + oh2
    before = jnp.dot(ltri_ref[...], oh.astype(BF16), preferred_element_type=F32)
    base = cnt_scr[...] + before
    r1 = jnp.sum(base * oh1, axis=-1, keepdims=True)
    r2 = jnp.sum(base * oh2, axis=-1, keepdims=True)
    cnt_scr[...] = cnt_scr[...] + jnp.sum(oh, axis=0, keepdims=True)

    rec = jnp.where(lane == 0, e1,
          jnp.where(lane == 1, e2,
          jnp.where(lane == 2, r1,
          jnp.where(lane == 3, r2,
          jnp.where(lane == 4, w1,
          jnp.where(lane == 5, w2, 0.0))))))
    rec_ref[...] = rec
    cnt_ref[...] = cnt_scr[...]


def moe_route(h, gain, w_router, b_router, ltri):
    t, d = h.shape
    row = lambda i: (i, 0)
    fix = lambda i: (0, 0)
    return pl.pallas_call(
        _route_kernel,
        out_shape=(jax.ShapeDtypeStruct((t, d), F32), jax.ShapeDtypeStruct((t, LANES), F32),
                   jax.ShapeDtypeStruct((1, LANES), F32)),
        grid=(t // ROW_TILE,),
        in_specs=[pl.BlockSpec((ROW_TILE, d), row), pl.BlockSpec((1, d), fix),
                  pl.BlockSpec((d, LANES), fix), pl.BlockSpec((1, LANES), fix),
                  pl.BlockSpec((ROW_TILE, ROW_TILE), fix)],
        out_specs=(pl.BlockSpec((ROW_TILE, d), row), pl.BlockSpec((ROW_TILE, LANES), row),
                   pl.BlockSpec((1, LANES), fix)),
        scratch_shapes=[pltpu.VMEM((1, LANES), F32)],
        compiler_params=_cparams(("arbitrary",)),
        name="moe_route",
    )(h, gain.reshape(1, d), w_router, b_router, ltri)


def _row_copy(src, src_row, dst, dst_row, sem):
    return pltpu.make_async_copy(src.at[pl.ds(src_row, 1)], dst.at[pl.ds(dst_row, 1)], sem)


def _dispatch_kernel(dest_ref, xn_ref, xs_ref, sem):
    rows = xn_ref.shape[0]

    def issue(r, c):
        _row_copy(xn_ref, r, xs_ref, dest_ref[0, 0, r], sem).start(priority=0)
        _row_copy(xn_ref, r, xs_ref, dest_ref[0, 0, rows + r], sem).start(priority=1)
        return c

    lax.fori_loop(0, rows, issue, 0, unroll=8)
    for _ in range(2):
        pltpu.make_async_copy(xn_ref, xs_ref.at[pl.ds(0, rows)], sem).wait()


def moe_dispatch(xn, dest3, n_slots):
    t, d = xn.shape
    return pl.pallas_call(
        _dispatch_kernel,
        out_shape=jax.ShapeDtypeStruct((n_slots, d), F32),
        grid=(t // ROW_TILE,),
        in_specs=[pl.BlockSpec((1, 1, 2 * ROW_TILE), lambda i: (i, 0, 0), memory_space=pltpu.SMEM),
                  pl.BlockSpec((ROW_TILE, d), lambda i: (i, 0))],
        out_specs=pl.BlockSpec(memory_space=pl.ANY),
        scratch_shapes=[pltpu.SemaphoreType.DMA],
        compiler_params=_cparams(("arbitrary",)),
        name="moe_dispatch",
    )(dest3, xn)


SC_WINDOW = 32
SC_INDEX_WINDOW = 128


def _sc_mesh():
    return plsc.VectorSubcoreMesh(core_axis_name="core", subcore_axis_name="subcore")


def moe_dispatch_sc(xn, dest_a, dest_b, n_slots):
    t, d = xn.shape

    @pl.kernel(out_type=jax.ShapeDtypeStruct((n_slots, d), xn.dtype), mesh=_sc_mesh(),
               scratch_types=[pltpu.VMEM((SC_WINDOW, d), xn.dtype)], name="moe_dispatch_sc")
    def run(x_hbm, id_hbm, da_hbm, db_hbm, o_hbm, buf):
        def body(id_vmem, da_vmem, db_vmem):
            for j in range(SC_INDEX_WINDOW // SC_WINDOW):
                part = pl.ds(j * SC_WINDOW, SC_WINDOW)
                pltpu.sync_copy(x_hbm.at[id_vmem.at[0, part]], buf)
                pltpu.sync_copy(buf, o_hbm.at[da_vmem.at[0, part]])
                pltpu.sync_copy(buf, o_hbm.at[db_vmem.at[0, part]])

        idx_spec = pl.BlockSpec((1, SC_INDEX_WINDOW), lambda i: (0, i))
        pltpu.emit_pipeline(
            body,
            grid=(t // SC_INDEX_WINDOW,),
            in_specs=[idx_spec, idx_spec, idx_spec],
            out_specs=[],
            core_axis_name=("core", "subcore"),
            dimension_semantics=(pltpu.PARALLEL,),
        )(id_hbm, da_hbm, db_hbm)

    return run(xn, jnp.arange(t, dtype=jnp.int32).reshape(1, t), dest_a, dest_b)


def moe_gather_sc(ys, dest_a, dest_b):
    d = ys.shape[1]
    t = dest_a.shape[1]
    out = jax.ShapeDtypeStruct((t, d), ys.dtype)

    @pl.kernel(out_type=(out, out), mesh=_sc_mesh(), scratch_types=[pltpu.VMEM((SC_WINDOW, d), ys.dtype)],
               name="moe_gather_sc")
    def run(y_hbm, id_hbm, da_hbm, db_hbm, ga_hbm, gb_hbm, buf):
        def body(id_vmem, da_vmem, db_vmem):
            for j in range(SC_INDEX_WINDOW // SC_WINDOW):
                part = pl.ds(j * SC_WINDOW, SC_WINDOW)
                pltpu.sync_copy(y_hbm.at[da_vmem.at[0, part]], buf)
                pltpu.sync_copy(buf, ga_hbm.at[id_vmem.at[0, part]])
                pltpu.sync_copy(y_hbm.at[db_vmem.at[0, part]], buf)
                pltpu.sync_copy(buf, gb_hbm.at[id_vmem.at[0, part]])

        idx_spec = pl.BlockSpec((1, SC_INDEX_WINDOW), lambda i: (0, i))
        pltpu.emit_pipeline(
            body,
            grid=(t // SC_INDEX_WINDOW,),
            in_specs=[idx_spec, idx_spec, idx_spec],
            out_specs=[],
            core_axis_name=("core", "subcore"),
            dimension_semantics=(pltpu.PARALLEL,),
        )(id_hbm, da_hbm, db_hbm)

    return run(ys, jnp.arange(t, dtype=jnp.int32).reshape(1, t), dest_a, dest_b)


def _combine_dense_kernel(h_ref, rec_ref, ga_ref, gb_ref, *out_refs, split):
    i = pl.program_id(0)
    rec = rec_ref[...]
    res = h_ref[...] + rec[:, 4:5] * ga_ref[...] + rec[:, 5:6] * gb_ref[...]
    if not split:
        out_refs[0][...] = res
    else:
        @pl.when(i < N_PROMPT_TILES)
        def _():
            out_refs[0][...] = res

        @pl.when((i >= N_PROMPT_TILES) & (i < N_PROMPT_TILES + N_SAMPLE_TILES))
        def _():
            out_refs[1][...] = res


def moe_combine_dense(h, rec, ga, gb, split=False):
    t, d = h.shape
    row = lambda i: (i, 0)
    if split:
        out_shape = (jax.ShapeDtypeStruct((T_PROMPT, d), F32), jax.ShapeDtypeStruct((T_SAMPLE, d), F32))
        out_specs = tuple(_parts_specs(d)[:2])
    else:
        out_shape = jax.ShapeDtypeStruct((t, d), F32)
        out_specs = pl.BlockSpec((ROW_TILE, d), row)
    return pl.pallas_call(
        functools.partial(_combine_dense_kernel, split=split),
        out_shape=out_shape,
        grid=(t // ROW_TILE,),
        in_specs=[pl.BlockSpec((ROW_TILE, d), row), pl.BlockSpec((ROW_TILE, LANES), row),
                  pl.BlockSpec((ROW_TILE, d), row), pl.BlockSpec((ROW_TILE, d), row)],
        out_specs=out_specs,
        compiler_params=_cparams(("arbitrary",)),
        name="moe_combine_dense",
    )(h, rec, ga, gb)


def _ffn_kernel(wblk_ref, we_ref, wlo_ref, whi_ref, xs_ref, w13_ref, w2_ref, ys_ref, w13b, w2b):
    w = pl.program_id(0)
    prev = jnp.maximum(w - 1, 0)
    first_visit = (w == 0) | (wblk_ref[w] != wblk_ref[prev])
    lo = wlo_ref[w]
    hi = whi_ref[w]

    @pl.when(hi > lo)
    def _():
        @pl.when((w == 0) | (we_ref[w] != we_ref[prev]))
        def _():
            w13b[...] = w13_ref[...].astype(BF16)
            w2b[...] = w2_ref[...].astype(BF16)

        au = jnp.dot(xs_ref[...].astype(BF16), w13b[...], preferred_element_type=F32)
        hmid = _silu(au[:, :D_EXPERT]) * au[:, D_EXPERT:]
        y = jnp.dot(hmid.astype(BF16), w2b[...], preferred_element_type=F32)
        row = lax.broadcasted_iota(jnp.int32, y.shape, 0)
        mine = (row >= lo) & (row < hi)

        @pl.when(first_visit)
        def _():
            ys_ref[...] = jnp.where(mine, y, 0.0)

        @pl.when(jnp.logical_not(first_visit))
        def _():
            ys_ref[...] = jnp.where(mine, y, ys_ref[...])


def moe_ffn(xs, work, w13_all, w2_all, layer):
    n_slots, d = xs.shape
    n_work = work[0].shape[0]
    xmap = lambda w, wb, we, wlo, whi: (wb[w], 0)
    w_map = lambda w, wb, we, wlo, whi: (layer, we[w], 0, 0)
    grid_spec = pltpu.PrefetchScalarGridSpec(
        num_scalar_prefetch=4,
        grid=(n_work,),
        in_specs=[pl.BlockSpec((EXPERT_BLOCK, d), xmap),
                  pl.BlockSpec((None, None, d, 2 * D_EXPERT), w_map),
                  pl.BlockSpec((None, None, D_EXPERT, d), w_map)],
        out_specs=pl.BlockSpec((EXPERT_BLOCK, d), xmap),
        scratch_shapes=[pltpu.VMEM((d, 2 * D_EXPERT), BF16), pltpu.VMEM((D_EXPERT, d), BF16)],
    )
    return pl.pallas_call(
        _ffn_kernel,
        out_shape=jax.ShapeDtypeStruct((n_slots, d), F32),
        grid_spec=grid_spec,
        compiler_params=_cparams(("arbitrary",)),
        name="moe_ffn",
    )(*work, xs, w13_all, w2_all)


def _ffn_work_items(cnt):
    n_slots = 2 * T_ALL
    n_blocks = n_slots // EXPERT_BLOCK
    n_work = n_blocks + N_EXPERTS - 1
    end = jnp.cumsum(cnt)
    start = end - cnt
    first_blk = start // EXPERT_BLOCK
    last_blk = jnp.maximum(end - 1, start) // EXPERT_BLOCK
    n_items = jnp.where(cnt > 0, last_blk - first_blk + 1, 0)
    item_end = jnp.cumsum(n_items)
    item_start = item_end - n_items
    w = jnp.arange(n_work, dtype=jnp.int32)
    used = w < item_end[-1]
    wq = jnp.minimum(w, item_end[-1] - 1)
    e = jnp.sum((item_end[None, :] <= wq[:, None]).astype(jnp.int32), axis=1)
    blk = jnp.where(used, first_blk[e] + (w - item_start[e]), n_blocks - 1).astype(jnp.int32)
    lo = jnp.maximum(start[e], blk * EXPERT_BLOCK) - blk * EXPERT_BLOCK
    hi = jnp.minimum(end[e], (blk + 1) * EXPERT_BLOCK) - blk * EXPERT_BLOCK
    lo = jnp.where(used, lo, 0).astype(jnp.int32)
    hi = jnp.where(used, hi, 0).astype(jnp.int32)
    return start, (blk, e, lo, hi)


def _combine_kernel(dest_ref, h_ref, rec_ref, ys_ref, *rest, split):
    out_refs, (g1, g2, sem) = rest[:-3], rest[-3:]
    rows = h_ref.shape[0]
    i = pl.program_id(0)

    def issue(r, c):
        _row_copy(ys_ref, dest_ref[0, 0, r], g1, r, sem).start(priority=0)
        _row_copy(ys_ref, dest_ref[0, 0, rows + r], g2, r, sem).start(priority=1)
        return c

    lax.fori_loop(0, rows, issue, 0, unroll=8)
    for buf in (g1, g2):
        pltpu.make_async_copy(ys_ref.at[pl.ds(0, rows)], buf, sem).wait()
    rec = rec_ref[...]
    res = h_ref[...] + rec[:, 4:5] * g1[...] + rec[:, 5:6] * g2[...]
    if not split:
        out_refs[0][...] = res
    else:
        @pl.when(i < N_PROMPT_TILES)
        def _():
            out_refs[0][...] = res

        @pl.when((i >= N_PROMPT_TILES) & (i < N_PROMPT_TILES + N_SAMPLE_TILES))
        def _():
            out_refs[1][...] = res


def moe_combine(h, rec, ys, dest3, split=False):
    t, d = h.shape
    row = lambda i: (i, 0)
    if split:
        parts = _parts_specs(d)[:2]
        out_shape = (jax.ShapeDtypeStruct((T_PROMPT, d), F32), jax.ShapeDtypeStruct((T_SAMPLE, d), F32))
        out_specs = tuple(parts)
    else:
        out_shape = jax.ShapeDtypeStruct((t, d), F32)
        out_specs = pl.BlockSpec((ROW_TILE, d), row)
    return pl.pallas_call(
        functools.partial(_combine_kernel, split=split),
        out_shape=out_shape,
        grid=(t // ROW_TILE,),
        in_specs=[pl.BlockSpec((1, 1, 2 * ROW_TILE), lambda i: (i, 0, 0), memory_space=pltpu.SMEM),
                  pl.BlockSpec((ROW_TILE, d), row), pl.BlockSpec((ROW_TILE, LANES), row),
                  pl.BlockSpec(memory_space=pl.ANY)],
        out_specs=out_specs,
        scratch_shapes=[pltpu.VMEM((ROW_TILE, d), F32), pltpu.VMEM((ROW_TILE, d), F32),
                        pltpu.SemaphoreType.DMA],
        compiler_params=_cparams(("arbitrary",)),
        name="moe_combine",
    )(dest3, h, rec, ys)


def hier_moe_layer(h, layer, gain, w_group, b_group, w_expert, b_expert, w13_all, w2_all, ltri, split_out=False):
    t = h.shape[0]
    pad = LANES - N_EXPERTS - N_GROUPS
    w_router = jnp.concatenate([w_expert, w_group, jnp.zeros((D_MODEL, pad), F32)], axis=1)
    b_router = jnp.concatenate([b_expert, b_group, jnp.zeros((pad,), F32)]).reshape(1, LANES)
    xn, rec, counts = moe_route(h, gain, w_router, b_router, ltri)

    cnt = counts[0, :N_EXPERTS].astype(jnp.int32)
    start, work = _ffn_work_items(cnt)
    e12 = rec[:, 0:2].astype(jnp.int32)
    r12 = rec[:, 2:4].astype(jnp.int32)
    sel = e12[:, :, None] == jnp.arange(N_EXPERTS, dtype=jnp.int32)[None, None, :]
    dest = jnp.sum(jnp.where(sel, start[None, None, :], 0), axis=-1) + r12
    dest_a = dest[:, 0].reshape(1, t)
    dest_b = dest[:, 1].reshape(1, t)

    xs = moe_dispatch_sc(xn, dest_a, dest_b, 2 * t)
    ys = moe_ffn(xs, work, w13_all, w2_all, layer)
    ga, gb = moe_gather_sc(ys, dest_a, dest_b)
    return moe_combine_dense(h, rec, ga, gb, split=split_out)


def _rope_tables(pos):
    half = ROPE_DIM // 2
    inv = jnp.exp(-math.log(ROPE_THETA) * jnp.arange(half, dtype=F32) * (2.0 / ROPE_DIM))
    ang = pos.astype(F32)[:, None] * inv[None, :]
    cos, sin = jnp.cos(ang), jnp.sin(ang)
    t = pos.shape[0]
    ones = jnp.ones((t, HEAD_DIM - ROPE_DIM), F32)
    zeros = jnp.zeros((t, HEAD_DIM - ROPE_DIM), F32)
    z8 = jnp.zeros((t, half), F32)
    cos_h = jnp.concatenate([cos, cos, ones], axis=1)
    sina_h = jnp.concatenate([-sin, z8, zeros], axis=1)
    sinb_h = jnp.concatenate([z8, sin, zeros], axis=1)
    two = lambda a: jnp.concatenate([a, a], axis=1)
    return two(cos_h), two(sina_h), two(sinb_h)


def kernel(x_prompt, x_sample, state_hgrn, cache_k_win, cache_v_win, meta_tokens, a_norm, a_w_in, a_lower_logits, a_out_norm, a_w_out, kv_norm, kv_w, k_norm, b_norm, b_wq, b_q_norm, b_sinks, b_w_out, moe_norm, moe_w_group, moe_b_group, moe_w_expert, moe_b_expert, moe_w13, moe_w2):
    tail_rows = T_ALL - OFF_META
    x_parts = (x_prompt.reshape(T_PROMPT, D_MODEL), x_sample.reshape(T_SAMPLE, D_MODEL),
               jnp.concatenate([meta_tokens.astype(F32), jnp.zeros((tail_rows - N_META, D_MODEL), F32)], axis=0))
    pos = jnp.concatenate([N_META + jnp.arange(SEQ, dtype=jnp.int32),
                           jnp.tile(PAST_LEN + jnp.arange(DEC_SEQ, dtype=jnp.int32), ROW_TILE // DEC_SEQ),
                           jnp.arange(N_META, dtype=jnp.int32),
                           jnp.zeros((ROW_TILE - N_META,), jnp.int32)])
    cos_t, sina_t, sinb_t = _rope_tables(pos)
    r256 = np.arange(256)
    hmean = jnp.asarray((r256[:, None] // HEAD_DIM == r256[None, :] // HEAD_DIM).astype(np.float32) / HEAD_DIM, BF16)
    ltri = jnp.asarray((r256[None, :] < r256[:, None]).astype(np.float32), BF16)
    lower = jnp.cumsum(jax.nn.softmax(a_lower_logits.astype(F32), axis=0), axis=0)

    moe = functools.partial(hier_moe_layer, w13_all=moe_w13, w2_all=moe_w2, ltri=ltri)

    z = in_project(x_parts, a_norm[0], a_w_in[0].astype(BF16))
    zero_state = jnp.zeros((1, A_HEADS, A_DK, A_DV), F32)
    o_meta, s_meta = hgrn2_scan(z, zero_state, lower[0], a_out_norm[0],
                                row_off=OFF_META, n_seq=1, seq_len=N_META)
    o_prompt, s_prompt = hgrn2_scan(z, s_meta, lower[0], a_out_norm[0], row_off=0, n_seq=BATCH, seq_len=SEQ)
    o_sample, s_sample = hgrn2_scan(z, state_hgrn[0].astype(F32), lower[0], a_out_norm[0],
                                    row_off=OFF_SAMPLE, n_seq=DEC_BATCH, seq_len=DEC_SEQ, group=SCAN_SAMPLE_GROUP)
    o_tail = jnp.concatenate([o_meta, jnp.zeros((tail_rows - N_META, D_MODEL), BF16)], axis=0)
    h = mixer_out((o_prompt, o_sample, o_tail), a_w_out[0].astype(BF16), x_parts)
    h = moe(h, 0, moe_norm[0], moe_w_group[0], moe_b_group[0], moe_w_expert[0], moe_b_expert[0])

    k_all, v_all = kv_project(h, kv_norm, kv_w.astype(BF16), hmean, jnp.tile(k_norm, KV_HEADS).reshape(1, KV_DIM),
                              cos_t, sina_t, sinb_t)

    q_all = q_project(h, b_norm[0], b_wq[0].astype(BF16), hmean, jnp.tile(b_q_norm[0], Q_HEADS).reshape(1, D_MODEL),
                      cos_t, sina_t, sinb_t)
    meta_blk = lambda a: jnp.concatenate([jnp.zeros((ATT_BLOCK - N_META, KV_DIM), F32),
                                          a[OFF_META:OFF_META + N_META]], axis=0)
    sinks = b_sinks[0].astype(F32)
    att_all = attention_prompt(q_all, k_all, v_all, meta_blk(k_all), meta_blk(v_all), sinks)
    att_all = attention_sample(q_all, cache_k_win.reshape(DEC_BATCH * WINDOW, KV_DIM).astype(F32),
                               cache_v_win.reshape(DEC_BATCH * WINDOW, KV_DIM).astype(F32),
                               k_all, v_all, sinks, att_all)
    h = matmul_residual(att_all, b_w_out[0].astype(BF16), h)
    y_p, y_s = moe(h, 1, moe_norm[1], moe_w_group[1], moe_b_group[1], moe_w_expert[1], moe_b_expert[1],
                   split_out=True)

    y_prompt = y_p.reshape(BATCH, SEQ, D_MODEL)
    y_sample = y_s.reshape(DEC_BATCH, DEC_SEQ, D_MODEL)
    last = lambda a: a[:T_PROMPT].reshape(BATCH, SEQ, KV_DIM)[:, -WINDOW:].reshape(BATCH, WINDOW, KV_HEADS, HEAD_DIM)
    kp = last(k_all)
    vp = last(v_all)
    ks = k_all[OFF_SAMPLE:OFF_SAMPLE + T_SAMPLE].reshape(DEC_BATCH, DEC_SEQ, KV_HEADS, HEAD_DIM)
    vs = v_all[OFF_SAMPLE:OFF_SAMPLE + T_SAMPLE].reshape(DEC_BATCH, DEC_SEQ, KV_HEADS, HEAD_DIM)
    k_win_s = jnp.concatenate([cache_k_win, ks], axis=1)[:, -WINDOW:]
    v_win_s = jnp.concatenate([cache_v_win, vs], axis=1)[:, -WINDOW:]
    return (y_prompt, y_sample, s_prompt[None], s_sample[None], kp, vp, k_win_s, v_win_s)
```

```python
import functools
import math

import numpy as np
import jax
import jax.numpy as jnp
from jax import lax
from jax.experimental import pallas as pl
from jax.experimental.pallas import tpu as pltpu
from jax.experimental.pallas import tpu_sc as plsc

F32 = jnp.float32
BF16 = jnp.bfloat16

D_MODEL = 1024
BATCH = 4
SEQ = 4096
DEC_BATCH = 128
DEC_SEQ = 8
PAST_LEN = 8192
N_META = 16
A_HEADS = 8
A_DK = 128
A_DV = 128
Q_HEADS = 16
KV_HEADS = 4
HEAD_DIM = 64
KV_DIM = KV_HEADS * HEAD_DIM
WINDOW = 128
ROPE_DIM = 16
ROPE_THETA = 500000.0
N_GROUPS = 4
EXPERTS_PER_GROUP = 8
N_EXPERTS = 32
D_EXPERT = 512
RMS_EPS = 1e-6

LANES = 128
SUBLANES = 8
VMEM_LIMIT = 56 * 1024 * 1024

ROW_TILE = 256
T_PROMPT = BATCH * SEQ
T_SAMPLE = DEC_BATCH * DEC_SEQ
OFF_SAMPLE = T_PROMPT
OFF_META = T_PROMPT + T_SAMPLE
T_REAL = OFF_META + N_META
T_ALL = -(-T_REAL // ROW_TILE) * ROW_TILE
N_TILES = T_ALL // ROW_TILE

SCAN_CHUNK = 128
SCAN_SAMPLE_GROUP = 8
ATT_BLOCK = 128
EXPERT_BLOCK = 256
FFN_CHUNKS = 2


def _cparams(sem):
    return pltpu.CompilerParams(dimension_semantics=sem, vmem_limit_bytes=VMEM_LIMIT)


def _nt_dot(a, b):
    return lax.dot_general(a, b, (((1,), (1,)), ((), ())), preferred_element_type=F32)


def _rms(x, gain):
    ms = jnp.mean(x * x, axis=-1, keepdims=True)
    return x * lax.rsqrt(ms + RMS_EPS) * gain


def _silu(x):
    return x * jax.nn.sigmoid(x)


N_PROMPT_TILES = T_PROMPT // ROW_TILE
N_SAMPLE_TILES = T_SAMPLE // ROW_TILE


def _parts_specs(width):
    return [pl.BlockSpec((ROW_TILE, width), lambda i: (jnp.minimum(i, N_PROMPT_TILES - 1), 0)),
            pl.BlockSpec((ROW_TILE, width), lambda i: (jnp.clip(i - N_PROMPT_TILES, 0, N_SAMPLE_TILES - 1), 0)),
            pl.BlockSpec((ROW_TILE, width), lambda i: (0, 0))]


def _pick_part(i, p_ref, s_ref, t_ref, dtype):
    return jnp.where(i < N_PROMPT_TILES, p_ref[...].astype(dtype),
                     jnp.where(i < N_PROMPT_TILES + N_SAMPLE_TILES, s_ref[...].astype(dtype),
                               t_ref[...].astype(dtype)))


def _in_proj_kernel(xp_ref, xs_ref, xt_ref, g_ref, w_ref, o_ref):
    x = _pick_part(pl.program_id(0), xp_ref, xs_ref, xt_ref, F32)
    xn = _rms(x, g_ref[...])
    o_ref[...] = jnp.dot(xn.astype(BF16), w_ref[...], preferred_element_type=F32)


def in_project(x_parts, gain, w_bf16):
    d, n = w_bf16.shape
    return pl.pallas_call(
        _in_proj_kernel,
        out_shape=jax.ShapeDtypeStruct((T_ALL, n), F32),
        grid=(N_TILES,),
        in_specs=_parts_specs(d) + [pl.BlockSpec((1, d), lambda i: (0, 0)),
                                    pl.BlockSpec((d, n), lambda i: (0, 0))],
        out_specs=pl.BlockSpec((ROW_TILE, n), lambda i: (i, 0)),
        compiler_params=_cparams(("parallel",)),
        name="in_project",
    )(*x_parts, gain.reshape(1, d), w_bf16)


def _mixer_out_kernel(ap_ref, as_ref, at_ref, w_ref, xp_ref, xs_ref, xt_ref, o_ref):
    i = pl.program_id(0)
    a = _pick_part(i, ap_ref, as_ref, at_ref, BF16)
    x = _pick_part(i, xp_ref, xs_ref, xt_ref, F32)
    o_ref[...] = x + jnp.dot(a, w_ref[...], preferred_element_type=F32)


def mixer_out(a_parts, w_bf16, x_parts):
    k, n = w_bf16.shape
    return pl.pallas_call(
        _mixer_out_kernel,
        out_shape=jax.ShapeDtypeStruct((T_ALL, n), F32),
        grid=(N_TILES,),
        in_specs=_parts_specs(k) + [pl.BlockSpec((k, n), lambda i: (0, 0))] + _parts_specs(n),
        out_specs=pl.BlockSpec((ROW_TILE, n), lambda i: (i, 0)),
        compiler_params=_cparams(("parallel",)),
        name="mixer_out",
    )(*a_parts, w_bf16, *x_parts)


def _head_norm_rope(y, hmean_ref, hgain, cos_t, sina_t, sinb_t):
    rows, width = y.shape
    sq = (y * y).astype(BF16)
    parts = []
    for s in range(width // 256):
        parts.append(jnp.dot(sq[:, s * 256:(s + 1) * 256], hmean_ref[...], preferred_element_type=F32))
    ms = parts[0] if len(parts) == 1 else jnp.concatenate(parts, axis=1)
    yn = y * lax.rsqrt(ms + RMS_EPS) * hgain
    reps = width // LANES
    cos_w = jnp.concatenate([cos_t] * reps, axis=1)
    sina_w = jnp.concatenate([sina_t] * reps, axis=1)
    sinb_w = jnp.concatenate([sinb_t] * reps, axis=1)
    half = ROPE_DIM // 2
    nxt = pltpu.roll(yn, width - half, 1)
    prv = pltpu.roll(yn, half, 1)
    return yn * cos_w + nxt * sina_w + prv * sinb_w


def _kv_kernel(x_ref, g_ref, w_ref, hmean_ref, hg_ref, cos_ref, sina_ref, sinb_ref, k_ref, v_ref):
    xn = _rms(x_ref[...], g_ref[...])
    z = jnp.dot(xn.astype(BF16), w_ref[...], preferred_element_type=F32)
    k = _head_norm_rope(z[:, :KV_DIM], hmean_ref, hg_ref[...], cos_ref[...], sina_ref[...], sinb_ref[...])
    k_ref[...] = k
    v_ref[...] = z[:, KV_DIM:]


def _rope_tile(i):
    tiles_per_seq = SEQ // ROW_TILE
    n_prompt_tiles = T_PROMPT // ROW_TILE
    n_sample_tiles = T_SAMPLE // ROW_TILE
    return (jnp.where(i < n_prompt_tiles, i % tiles_per_seq,
                      jnp.where(i < n_prompt_tiles + n_sample_tiles, tiles_per_seq, tiles_per_seq + 1)), 0)


def kv_project(x, gain, w_bf16, hmean, hgain_w, cos_t, sina_t, sinb_t):
    t, d = x.shape
    row = lambda i: (i, 0)
    fix = lambda i: (0, 0)
    return pl.pallas_call(
        _kv_kernel,
        out_shape=(jax.ShapeDtypeStruct((t, KV_DIM), F32), jax.ShapeDtypeStruct((t, KV_DIM), F32)),
        grid=(t // ROW_TILE,),
        in_specs=[pl.BlockSpec((ROW_TILE, d), row), pl.BlockSpec((1, d), fix),
                  pl.BlockSpec((d, 2 * KV_DIM), fix), pl.BlockSpec((256, 256), fix),
                  pl.BlockSpec((1, KV_DIM), fix),
                  pl.BlockSpec((ROW_TILE, LANES), _rope_tile), pl.BlockSpec((ROW_TILE, LANES), _rope_tile),
                  pl.BlockSpec((ROW_TILE, LANES), _rope_tile)],
        out_specs=(pl.BlockSpec((ROW_TILE, KV_DIM), row), pl.BlockSpec((ROW_TILE, KV_DIM), row)),
        compiler_params=_cparams(("parallel",)),
        name="kv_project",
    )(x, gain.reshape(1, d), w_bf16, hmean, hgain_w, cos_t, sina_t, sinb_t)


def _q_kernel(x_ref, g_ref, w_ref, hmean_ref, hg_ref, cos_ref, sina_ref, sinb_ref, q_ref):
    xn = _rms(x_ref[...], g_ref[...])
    z = jnp.dot(xn.astype(BF16), w_ref[...], preferred_element_type=F32)
    q = _head_norm_rope(z, hmean_ref, hg_ref[...], cos_ref[...], sina_ref[...], sinb_ref[...])
    q_ref[...] = (q * HEAD_DIM ** -0.5).astype(q_ref.dtype)


def q_project(x, gain, w_bf16, hmean, hgain_w, cos_t, sina_t, sinb_t):
    t, d = x.shape
    row = lambda i: (i, 0)
    fix = lambda i: (0, 0)
    return pl.pallas_call(
        _q_kernel,
        out_shape=jax.ShapeDtypeStruct((t, d), BF16),
        grid=(t // ROW_TILE,),
        in_specs=[pl.BlockSpec((ROW_TILE, d), row), pl.BlockSpec((1, d), fix),
                  pl.BlockSpec((d, d), fix), pl.BlockSpec((256, 256), fix),
                  pl.BlockSpec((1, d), fix),
                  pl.BlockSpec((ROW_TILE, LANES), _rope_tile), pl.BlockSpec((ROW_TILE, LANES), _rope_tile),
                  pl.BlockSpec((ROW_TILE, LANES), _rope_tile)],
        out_specs=pl.BlockSpec((ROW_TILE, d), row),
        compiler_params=_cparams(("parallel",)),
        name="q_project",
    )(x, gain.reshape(1, d), w_bf16, hmean, hgain_w, cos_t, sina_t, sinb_t)


def _matmul_residual_kernel(a_ref, w_ref, r_ref, o_ref):
    o_ref[...] = r_ref[...] + jnp.dot(a_ref[...], w_ref[...], preferred_element_type=F32)


def matmul_residual(a_bf16, w_bf16, resid):
    t, k = a_bf16.shape
    n = w_bf16.shape[1]
    return pl.pallas_call(
        _matmul_residual_kernel,
        out_shape=jax.ShapeDtypeStruct((t, n), F32),
        grid=(t // ROW_TILE,),
        in_specs=[pl.BlockSpec((ROW_TILE, k), lambda i: (i, 0)),
                  pl.BlockSpec((k, n), lambda i: (0, 0)),
                  pl.BlockSpec((ROW_TILE, n), lambda i: (i, 0))],
        out_specs=pl.BlockSpec((ROW_TILE, n), lambda i: (i, 0)),
        compiler_params=_cparams(("parallel",)),
        name="matmul_residual",
    )(a_bf16, w_bf16, resid)


def _scan_levels(c):
    levels = []
    m = c
    while m >= 2:
        levels.append(m)
        m //= 2
    return levels


LOG2E = 1.4426950408889634


def _scan_kernel(z_ref, s0_ref, lb_ref, og_ref, tri_ref, lmask_ref, sgn_ref, o_ref, sfin_ref, s_scr, b_scr,
                 *, rows, seq_len):
    c_idx = pl.program_id(1)
    levels = _scan_levels(seq_len)
    n_sub = rows // seq_len
    hk = A_HEADS * A_DK

    @pl.when(c_idx == 0)
    def _():
        s_scr[...] = s0_ref[...]

    sub = lax.broadcasted_iota(jnp.int32, (SUBLANES, LANES), 0)
    row = lax.broadcasted_iota(jnp.int32, (LANES, LANES), 0)
    og = og_ref[...]

    def pad_f32(x):
        if x.shape[0] == LANES:
            return x
        return jnp.concatenate([x, jnp.zeros((LANES - x.shape[0], x.shape[1]), x.dtype)], axis=0)

    def pad_rows(x):
        return pad_f32(x).astype(BF16)

    def cols(part, h):
        return slice(part * hk + h * LANES, part * hk + (h + 1) * LANES)

    def gates(h):
        lb = lb_ref[:, cols(0, h)]
        forget = lb + (1.0 - lb) * jax.nn.sigmoid(z_ref[:, cols(1, h)])
        logf = jnp.log(forget)
        hi = logf.astype(BF16).astype(F32)
        r1 = logf - hi
        mid = r1.astype(BF16).astype(F32)
        lo = r1 - mid
        cs = jnp.dot(tri_ref[...], pad_rows(jnp.concatenate([hi, mid, lo], axis=1)),
                     preferred_element_type=F32)
        b = (cs[:rows, :LANES] + cs[:rows, LANES:2 * LANES]) + cs[:rows, 2 * LANES:]
        b_scr[h] = b
        return _silu(z_ref[:, cols(0, h)]), 1.0 - forget, b

    def bref_for(h, m):
        b_rows = b_scr.at[h]
        half = m // 2
        pieces = []
        for g in range(rows // SUBLANES):
            base = g * SUBLANES
            if m >= SUBLANES:
                r = (base // m) * m + half - 1
                piece = jnp.broadcast_to(b_rows[r:r + 1, :], (SUBLANES, LANES))
            else:
                piece = jnp.broadcast_to(b_rows[base + half - 1:base + half, :], (SUBLANES, LANES))
                for blk in range(1, SUBLANES // m):
                    r = base + blk * m + half - 1
                    piece = jnp.where(sub >= blk * m,
                                      jnp.broadcast_to(b_rows[r:r + 1, :], (SUBLANES, LANES)), piece)
            pieces.append(piece)
        return pieces[0] if len(pieces) == 1 else jnp.concatenate(pieces, axis=0)

    heads = range(A_HEADS)
    qkb = [gates(h) for h in heads]
    att = [_nt_dot(pad_rows(qf), pad_rows(kf)) * lmask_ref[len(levels)] for qf, kf, _ in qkb]
    for li, m in enumerate(levels):
        for h in heads:
            qf, kf, b = qkb[h]
            sgn = sgn_ref[li]
            e = jnp.exp2((b - bref_for(h, m)) * sgn)
            w = pad_rows(jnp.where(sgn > 0, qf, kf) * e)
            att[h] = att[h] + _nt_dot(w, w) * lmask_ref[li]

    def finish(h):
        qf, kf, b = qkb[h]
        b_rows = b_scr.at[h]
        v_b = pad_rows(z_ref[:, cols(2, h)])
        o_intra = jnp.dot(att[h].astype(BF16), v_b, preferred_element_type=F32)
        eb = jnp.exp(b)
        qs = qf * eb
        b_end = [jnp.broadcast_to(b_rows[(i + 1) * seq_len - 1:(i + 1) * seq_len, :], (seq_len, LANES))
                 for i in range(n_sub)]
        b_end = b_end[0] if n_sub == 1 else jnp.concatenate(b_end, axis=0)
        kd_t = pad_f32(kf * jnp.exp(b_end - b)).T.astype(BF16)
        eb_t = pad_f32(eb).T
        qs_b = pad_rows(qs)
        o = o_intra
        for i in range(n_sub):
            s_old = s_scr[i, h]
            first, last = i * seq_len, (i + 1) * seq_len - 1
            if n_sub == 1:
                qs_i, v_i = qs_b, v_b
            else:
                mine = (row >= first) & (row <= last)
                qs_i = jnp.where(mine, qs_b, jnp.zeros_like(qs_b))
                v_i = jnp.where(mine, v_b, jnp.zeros_like(v_b))
            o = o + jnp.dot(qs_i, s_old.astype(BF16), preferred_element_type=F32)
            decay = jnp.broadcast_to(eb_t[:, last:last + 1], (LANES, LANES))
            s_scr[i, h] = decay * s_old + jnp.dot(kd_t, v_i, preferred_element_type=F32)
        o = o[:rows]

        on = _rms(o, og) * _silu(z_ref[:, cols(3, h)])
        o_ref[:, cols(0, h)] = on.astype(o_ref.dtype)

    for h in heads:
        finish(h)

    @pl.when(c_idx == pl.num_programs(1) - 1)
    def _():
        sfin_ref[...] = s_scr[...]


def _scan_consts(rows, seq_len):
    levels = _scan_levels(seq_len)
    r = np.arange(LANES)
    t, s = r[:, None], r[None, :]
    live = (t < rows) & (s < rows)
    tri = ((s <= t) & (t // seq_len == s // seq_len) & live).astype(np.float32)
    masks, sgns = [], []
    for m in levels:
        masks.append(((t // m == s // m) & (t % m >= m // 2) & (s % m < m // 2) & live).astype(np.float32))
        sgns.append(np.broadcast_to(np.where(r[:rows, None] % m >= m // 2, LOG2E, -LOG2E), (rows, LANES)))
    masks.append(((t == s) & live).astype(np.float32))
    return jnp.asarray(tri, BF16), jnp.asarray(np.stack(masks), F32), jnp.asarray(np.stack(sgns), F32)


def hgrn2_scan(z, s0, lb, o_gain, *, row_off, n_seq, seq_len, group=1):
    hv = A_HEADS * A_DV
    if seq_len > SCAN_CHUNK:
        assert group == 1
        sub_len, rows, n_chunks, n_steps = SCAN_CHUNK, SCAN_CHUNK, seq_len // SCAN_CHUNK, n_seq
    else:
        sub_len, rows, n_chunks, n_steps = seq_len, group * seq_len, 1, n_seq // group
    blk_off = row_off // rows
    tri, lmask, sgn = _scan_consts(rows, sub_len)
    shared_s0 = s0.shape[0] == 1
    fix2 = lambda s, c: (0, 0)
    fix3 = lambda s, c: (0, 0, 0)
    o, sfin = pl.pallas_call(
        functools.partial(_scan_kernel, rows=rows, seq_len=sub_len),
        out_shape=(jax.ShapeDtypeStruct((n_seq * seq_len, hv), BF16 if rows % 16 == 0 else F32),
                   jax.ShapeDtypeStruct((n_seq, A_HEADS, A_DK, A_DV), F32)),
        grid=(n_steps, n_chunks),
        in_specs=[pl.BlockSpec((rows, 4 * hv), lambda s, c: (blk_off + s * n_chunks + c, 0)),
                  pl.BlockSpec((group, A_HEADS, A_DK, A_DV), (lambda s, c: (0, 0, 0, 0)) if shared_s0
                               else (lambda s, c: (s, 0, 0, 0))),
                  pl.BlockSpec((1, hv), fix2), pl.BlockSpec((1, A_DV), fix2),
                  pl.BlockSpec((LANES, LANES), fix2), pl.BlockSpec(lmask.shape, fix3),
                  pl.BlockSpec(sgn.shape, fix3)],
        out_specs=(pl.BlockSpec((rows, hv), lambda s, c: (s * n_chunks + c, 0)),
                   pl.BlockSpec((group, A_HEADS, A_DK, A_DV), lambda s, c: (s, 0, 0, 0))),
        scratch_shapes=[pltpu.VMEM((group, A_HEADS, A_DK, A_DV), F32), pltpu.VMEM((A_HEADS, rows, LANES), F32)],
        compiler_params=_cparams(("parallel", "arbitrary")),
        name=f"hgrn2_scan_r{rows}",
    )(z, s0, lb.reshape(1, hv), o_gain.reshape(1, A_DV), tri, lmask, sgn)
    return o, sfin


KEYS = 2 * ATT_BLOCK


def _pair_operand(x, kh):
    slab = x[:, (kh // 2) * LANES:(kh // 2 + 1) * LANES]
    lane = lax.broadcasted_iota(jnp.int32, slab.shape, 1)
    if kh % 2 == 0:
        lo = jnp.where(lane < HEAD_DIM, slab, 0.0)
        hi = pltpu.roll(lo, HEAD_DIM, 1)
    else:
        hi = jnp.where(lane >= HEAD_DIM, slab, 0.0)
        lo = pltpu.roll(hi, HEAD_DIM, 1)
    return jnp.concatenate([lo, hi], axis=0).astype(BF16)


def _window_bias(rows, jmin):
    t_i = lax.broadcasted_iota(jnp.int32, (rows, 2 * KEYS), 0)
    c_i = lax.broadcasted_iota(jnp.int32, (rows, 2 * KEYS), 1)
    j_i = c_i & (ATT_BLOCK - 1)
    own = (c_i & ATT_BLOCK) != 0
    ok = (own & (j_i <= t_i)) | (jnp.logical_not(own) & (j_i >= t_i) & (j_i >= jmin))
    return jnp.where(ok, 0.0, -jnp.inf).astype(F32)


def _pair_softmax(s, sink_a, sink_b):
    probs, rinv = [], []
    for hh, sink in enumerate((sink_a, sink_b)):
        sh = s[:, hh * KEYS:(hh + 1) * KEYS]
        m = jnp.maximum(jnp.max(sh, axis=-1, keepdims=True), sink)
        p = jnp.exp(sh - m)
        den = jnp.sum(p, axis=-1, keepdims=True) + jnp.exp(sink - m)
        probs.append(p.astype(BF16))
        rinv.append(1.0 / den)
    lane = lax.broadcasted_iota(jnp.int32, (s.shape[0], LANES), 1)
    return jnp.concatenate(probs, axis=1), jnp.where(lane < HEAD_DIM, rinv[0], rinv[1])


def _attn_prompt_kernel(sink_ref, q_ref, kp_ref, ko_ref, vp_ref, vo_ref, km_ref, vm_ref, o_ref,
                        k2_scr, v2_scr, s_scr, p_scr, r_scr):
    n = pl.program_id(0)
    nbp = SEQ // ATT_BLOCK
    n_pairs = Q_HEADS // 2

    @pl.when(n >= BATCH * nbp)
    def _():
        o_ref[...] = jnp.zeros_like(o_ref)

    @pl.when(n < BATCH * nbp)
    def _():
        first = (n % nbp) == 0
        jmin = jnp.where(first, ATT_BLOCK - N_META, 0)
        k = jnp.concatenate([jnp.where(first, km_ref[...], kp_ref[...]), ko_ref[...]], axis=0)
        v = jnp.concatenate([jnp.where(first, vm_ref[...], vp_ref[...]), vo_ref[...]], axis=0)
        bias = _window_bias(ATT_BLOCK, jmin)
        for kh in range(KV_HEADS):
            k2_scr[kh] = _pair_operand(k, kh)
            v2_scr[kh] = _pair_operand(v, kh)
        for pair in range(n_pairs):
            s_scr[pair] = _nt_dot(q_ref[:, pair * LANES:(pair + 1) * LANES], k2_scr[pair // 2]) + bias
        for pair in range(n_pairs):
            p, rinv = _pair_softmax(s_scr[pair], sink_ref[2 * pair], sink_ref[2 * pair + 1])
            p_scr[pair] = p
            r_scr[pair] = rinv
        for pair in range(n_pairs):
            o = jnp.dot(p_scr[pair], v2_scr[pair // 2], preferred_element_type=F32) * r_scr[pair]
            o_ref[:, pair * LANES:(pair + 1) * LANES] = o.astype(o_ref.dtype)


def attention_prompt(q_all, k_all, v_all, k_meta_blk, v_meta_blk, sinks):
    n_prompt_blocks = T_PROMPT // ATT_BLOCK
    n_blocks = T_ALL // ATT_BLOCK
    n_pairs = Q_HEADS // 2
    own = lambda n, sk: (jnp.minimum(n, n_prompt_blocks - 1), 0)
    prev = lambda n, sk: (jnp.maximum(jnp.minimum(n, n_prompt_blocks - 1) - 1, 0), 0)
    fix = lambda n, sk: (0, 0)
    grid_spec = pltpu.PrefetchScalarGridSpec(
        num_scalar_prefetch=1,
        grid=(n_blocks,),
        in_specs=[pl.BlockSpec((ATT_BLOCK, D_MODEL), own),
                  pl.BlockSpec((ATT_BLOCK, KV_DIM), prev), pl.BlockSpec((ATT_BLOCK, KV_DIM), own),
                  pl.BlockSpec((ATT_BLOCK, KV_DIM), prev), pl.BlockSpec((ATT_BLOCK, KV_DIM), own),
                  pl.BlockSpec((ATT_BLOCK, KV_DIM), fix), pl.BlockSpec((ATT_BLOCK, KV_DIM), fix)],
        out_specs=pl.BlockSpec((ATT_BLOCK, D_MODEL), lambda n, sk: (n, 0)),
        scratch_shapes=[pltpu.VMEM((KV_HEADS, 2 * KEYS, LANES), BF16), pltpu.VMEM((KV_HEADS, 2 * KEYS, LANES), BF16),
                        pltpu.VMEM((n_pairs, ATT_BLOCK, 2 * KEYS), F32),
                        pltpu.VMEM((n_pairs, ATT_BLOCK, 2 * KEYS), BF16),
                        pltpu.VMEM((n_pairs, ATT_BLOCK, LANES), F32)],
    )
    return pl.pallas_call(
        _attn_prompt_kernel,
        out_shape=jax.ShapeDtypeStruct((T_ALL, D_MODEL), BF16),
        grid_spec=grid_spec,
        compiler_params=_cparams(("parallel",)),
        name="attention_prompt",
    )(sinks, q_all, k_all, k_all, v_all, v_all, k_meta_blk, v_meta_blk)


SAMPLE_GROUP = ATT_BLOCK // DEC_SEQ


def _attn_sample_kernel(sink_ref, q_ref, ck_ref, cv_ref, kn_ref, vn_ref, buf_ref, o_ref,
                        qf_scr, of_scr, k2_scr, v2_scr):
    del buf_ref
    qrows = 2 * DEC_SEQ
    qf_scr[...] = q_ref[...].astype(F32)
    bias = _window_bias(qrows, 0)
    zq = jnp.zeros((qrows - DEC_SEQ, D_MODEL), F32)
    zk = jnp.zeros((ATT_BLOCK - DEC_SEQ, KV_DIM), F32)

    def seq_body(i, carry):
        r_new = pl.multiple_of(i * DEC_SEQ, DEC_SEQ)
        r_old = pl.multiple_of(i * WINDOW, WINDOW)
        q = jnp.concatenate([qf_scr[pl.ds(r_new, DEC_SEQ), :], zq], axis=0).astype(BF16)
        k = jnp.concatenate([ck_ref[pl.ds(r_old, WINDOW), :], kn_ref[pl.ds(r_new, DEC_SEQ), :], zk], axis=0)
        v = jnp.concatenate([cv_ref[pl.ds(r_old, WINDOW), :], vn_ref[pl.ds(r_new, DEC_SEQ), :], zk], axis=0)
        for kh in range(KV_HEADS):
            k2_scr[kh] = _pair_operand(k, kh)
            v2_scr[kh] = _pair_operand(v, kh)
        scores = [_nt_dot(q[:, pair * LANES:(pair + 1) * LANES], k2_scr[pair // 2]) + bias
                  for pair in range(Q_HEADS // 2)]
        soft = [_pair_softmax(s, sink_ref[2 * pair], sink_ref[2 * pair + 1]) for pair, s in enumerate(scores)]
        for pair, (p, rinv) in enumerate(soft):
            o = jnp.dot(p, v2_scr[pair // 2], preferred_element_type=F32) * rinv
            of_scr[pl.ds(r_new, DEC_SEQ), pair * LANES:(pair + 1) * LANES] = o[:DEC_SEQ]
        return carry

    lax.fori_loop(0, SAMPLE_GROUP, seq_body, 0)
    o_ref[...] = of_scr[...].astype(o_ref.dtype)


def attention_sample(q_all, cache_k, cache_v, k_all, v_all, sinks, out_buf):
    first_blk = OFF_SAMPLE // ATT_BLOCK
    new = lambda g, sk: (first_blk + g, 0)
    old = lambda g, sk: (g, 0)
    grid_spec = pltpu.PrefetchScalarGridSpec(
        num_scalar_prefetch=1,
        grid=(DEC_BATCH // SAMPLE_GROUP,),
        in_specs=[pl.BlockSpec((ATT_BLOCK, D_MODEL), new),
                  pl.BlockSpec((SAMPLE_GROUP * WINDOW, KV_DIM), old),
                  pl.BlockSpec((SAMPLE_GROUP * WINDOW, KV_DIM), old),
                  pl.BlockSpec((ATT_BLOCK, KV_DIM), new), pl.BlockSpec((ATT_BLOCK, KV_DIM), new),
                  pl.BlockSpec(memory_space=pl.ANY)],
        out_specs=pl.BlockSpec((ATT_BLOCK, D_MODEL), new),
        scratch_shapes=[pltpu.VMEM((ATT_BLOCK, D_MODEL), F32), pltpu.VMEM((ATT_BLOCK, D_MODEL), F32),
                        pltpu.VMEM((KV_HEADS, 2 * KEYS, LANES), BF16), pltpu.VMEM((KV_HEADS, 2 * KEYS, LANES), BF16)],
    )
    return pl.pallas_call(
        _attn_sample_kernel,
        out_shape=jax.ShapeDtypeStruct(out_buf.shape, out_buf.dtype),
        grid_spec=grid_spec,
        input_output_aliases={6: 0},
        compiler_params=_cparams(("parallel",)),
        name="attention_sample",
    )(sinks, q_all, cache_k, cache_v, k_all, v_all, out_buf)


ROUTE_COLS = 8


def _route_kernel(x_ref, g_ref, wh_ref, wl_ref, br_ref, ltri_ref, xn_ref, rec_ref, rect_ref, cnt_ref, cnt_scr):
    i = pl.program_id(0)

    @pl.when(i == 0)
    def _():
        cnt_scr[...] = jnp.zeros_like(cnt_scr)

    xn = _rms(x_ref[...], g_ref[...])
    xn_ref[...] = xn
    xh = xn.astype(BF16)
    xl = (xn - xh.astype(F32)).astype(BF16)
    logits = (jnp.dot(xh, wh_ref[...], preferred_element_type=F32)
              + (jnp.dot(xh, wl_ref[...], preferred_element_type=F32)
                 + jnp.dot(xl, wh_ref[...], preferred_element_type=F32))) + br_ref[...]
    rows = logits.shape[0]
    lane = lax.broadcasted_iota(jnp.int32, (rows, LANES), 1).astype(F32)
    neg = jnp.float32(-jnp.inf)
    big = jnp.float32(LANES)

    is_g = (lane >= N_EXPERTS) & (lane < N_EXPERTS + N_GROUPS)
    gl = jnp.where(is_g, logits, neg)
    gmax = jnp.max(gl, axis=-1, keepdims=True)
    gsel = jnp.min(jnp.where(gl == gmax, lane, big), axis=-1, keepdims=True) - N_EXPERTS
    gden = jnp.sum(jnp.where(is_g, jnp.exp(gl - gmax), 0.0), axis=-1, keepdims=True)
    gw = 1.0 / gden

    in_grp = (lane >= gsel * EXPERTS_PER_GROUP) & (lane < (gsel + 1) * EXPERTS_PER_GROUP)
    el = jnp.where(in_grp, logits, neg)
    t1 = jnp.max(el, axis=-1, keepdims=True)
    e1 = jnp.min(jnp.where(el == t1, lane, big), axis=-1, keepdims=True)
    el2 = jnp.where(lane == e1, neg, el)
    t2 = jnp.max(el2, axis=-1, keepdims=True)
    e2 = jnp.min(jnp.where(el2 == t2, lane, big), axis=-1, keepdims=True)
    x2 = jnp.exp(t2 - t1)
    w1 = gw / (1.0 + x2)
    w2 = gw * x2 / (1.0 + x2)

    oh1 = (lane == e1).astype(F32)
    oh2 = (lane == e2).astype(F32)
    oh = oh1 + oh2
    before = jnp.dot(ltri_ref[...], oh.astype(BF16), preferred_element_type=F32)
    base = cnt_scr[...] + before
    r1 = jnp.sum(base * oh1, axis=-1, keepdims=True)
    r2 = jnp.sum(base * oh2, axis=-1, keepdims=True)
    cnt_scr[...] = cnt_scr[...] + jnp.sum(oh, axis=0, keepdims=True)

    rec = jnp.where(lane == 0, e1,
          jnp.where(lane == 1, e2,
          jnp.where(lane == 2, r1,
          jnp.where(lane == 3, r2,
          jnp.where(lane == 4, w1,
          jnp.where(lane == 5, w2, 0.0))))))
    rec_ref[...] = rec
    rect_ref[...] = jnp.concatenate([rec[:LANES].T[:ROUTE_COLS], rec[LANES:].T[:ROUTE_COLS]], axis=1)
    cnt_ref[...] = cnt_scr[...]


def moe_route(h, gain, w_router, b_router, ltri):
    t, d = h.shape
    row = lambda i: (i, 0)
    fix = lambda i: (0, 0)
    w_hi = w_router.astype(BF16)
    w_lo = (w_router - w_hi.astype(F32)).astype(BF16)
    return pl.pallas_call(
        _route_kernel,
        out_shape=(jax.ShapeDtypeStruct((t, d), F32), jax.ShapeDtypeStruct((t, LANES), F32),
                   jax.ShapeDtypeStruct((ROUTE_COLS, t), F32), jax.ShapeDtypeStruct((1, LANES), F32)),
        grid=(t // ROW_TILE,),
        in_specs=[pl.BlockSpec((ROW_TILE, d), row), pl.BlockSpec((1, d), fix),
                  pl.BlockSpec((d, LANES), fix), pl.BlockSpec((d, LANES), fix), pl.BlockSpec((1, LANES), fix),
                  pl.BlockSpec((ROW_TILE, ROW_TILE), fix)],
        out_specs=(pl.BlockSpec((ROW_TILE, d), row), pl.BlockSpec((ROW_TILE, LANES), row),
                   pl.BlockSpec((ROUTE_COLS, ROW_TILE), lambda i: (0, i)), pl.BlockSpec((1, LANES), fix)),
        scratch_shapes=[pltpu.VMEM((1, LANES), F32)],
        compiler_params=_cparams(("arbitrary",)),
        name="moe_route",
    )(h, gain.reshape(1, d), w_hi, w_lo, b_router, ltri)


def _row_copy(src, src_row, dst, dst_row, sem):
    return pltpu.make_async_copy(src.at[pl.ds(src_row, 1)], dst.at[pl.ds(dst_row, 1)], sem)


def _dispatch_kernel(dest_ref, xn_ref, xs_ref, sem):
    rows = xn_ref.shape[0]

    def issue(r, c):
        _row_copy(xn_ref, r, xs_ref, dest_ref[0, 0, r], sem).start(priority=0)
        _row_copy(xn_ref, r, xs_ref, dest_ref[0, 0, rows + r], sem).start(priority=1)
        return c

    lax.fori_loop(0, rows, issue, 0, unroll=8)
    for _ in range(2):
        pltpu.make_async_copy(xn_ref, xs_ref.at[pl.ds(0, rows)], sem).wait()


def moe_dispatch(xn, dest3, n_slots):
    t, d = xn.shape
    return pl.pallas_call(
        _dispatch_kernel,
        out_shape=jax.ShapeDtypeStruct((n_slots, d), F32),
        grid=(t // ROW_TILE,),
        in_specs=[pl.BlockSpec((1, 1, 2 * ROW_TILE), lambda i: (i, 0, 0), memory_space=pltpu.SMEM),
                  pl.BlockSpec((ROW_TILE, d), lambda i: (i, 0))],
        out_specs=pl.BlockSpec(memory_space=pl.ANY),
        scratch_shapes=[pltpu.SemaphoreType.DMA],
        compiler_params=_cparams(("arbitrary",)),
        name="moe_dispatch",
    )(dest3, xn)


SC_WINDOW = 32
SC_INDEX_WINDOW = 128


def _sc_mesh():
    return plsc.VectorSubcoreMesh(core_axis_name="core", subcore_axis_name="subcore")


def moe_dispatch_sc(xn, dest_a, dest_b, n_slots):
    t, d = xn.shape

    @pl.kernel(out_type=jax.ShapeDtypeStruct((n_slots, d), xn.dtype), mesh=_sc_mesh(),
               scratch_types=[pltpu.VMEM((SC_WINDOW, d), xn.dtype)], name="moe_dispatch_sc")
    def run(x_hbm, id_hbm, da_hbm, db_hbm, o_hbm, buf):
        def body(id_vmem, da_vmem, db_vmem):
            for j in range(SC_INDEX_WINDOW // SC_WINDOW):
                part = pl.ds(j * SC_WINDOW, SC_WINDOW)
                pltpu.sync_copy(x_hbm.at[id_vmem.at[0, part]], buf)
                pltpu.sync_copy(buf, o_hbm.at[da_vmem.at[0, part]])
                pltpu.sync_copy(buf, o_hbm.at[db_vmem.at[0, part]])

        idx_spec = pl.BlockSpec((1, SC_INDEX_WINDOW), lambda i: (0, i))
        pltpu.emit_pipeline(
            body,
            grid=(t // SC_INDEX_WINDOW,),
            in_specs=[idx_spec, idx_spec, idx_spec],
            out_specs=[],
            core_axis_name=("core", "subcore"),
            dimension_semantics=(pltpu.PARALLEL,),
        )(id_hbm, da_hbm, db_hbm)

    return run(xn, jnp.arange(t, dtype=jnp.int32).reshape(1, t), dest_a, dest_b)


def moe_gather_sc(ys, dest_a, dest_b):
    d = ys.shape[1]
    t = dest_a.shape[1]
    out = jax.ShapeDtypeStruct((t, d), ys.dtype)

    @pl.kernel(out_type=(out, out), mesh=_sc_mesh(), scratch_types=[pltpu.VMEM((SC_WINDOW, d), ys.dtype)],
               name="moe_gather_sc")
    def run(y_hbm, id_hbm, da_hbm, db_hbm, ga_hbm, gb_hbm, buf):
        def body(id_vmem, da_vmem, db_vmem):
            for j in range(SC_INDEX_WINDOW // SC_WINDOW):
                part = pl.ds(j * SC_WINDOW, SC_WINDOW)
                pltpu.sync_copy(y_hbm.at[da_vmem.at[0, part]], buf)
                pltpu.sync_copy(buf, ga_hbm.at[id_vmem.at[0, part]])
                pltpu.sync_copy(y_hbm.at[db_vmem.at[0, part]], buf)
                pltpu.sync_copy(buf, gb_hbm.at[id_vmem.at[0, part]])

        idx_spec = pl.BlockSpec((1, SC_INDEX_WINDOW), lambda i: (0, i))
        pltpu.emit_pipeline(
            body,
            grid=(t // SC_INDEX_WINDOW,),
            in_specs=[idx_spec, idx_spec, idx_spec],
            out_specs=[],
            core_axis_name=("core", "subcore"),
            dimension_semantics=(pltpu.PARALLEL,),
        )(id_hbm, da_hbm, db_hbm)

    return run(ys, jnp.arange(t, dtype=jnp.int32).reshape(1, t), dest_a, dest_b)


def _combine_dense_kernel(h_ref, rec_ref, ga_ref, gb_ref, *out_refs, split):
    i = pl.program_id(0)
    rec = rec_ref[...]
    res = h_ref[...] + rec[:, 4:5] * ga_ref[...] + rec[:, 5:6] * gb_ref[...]
    if not split:
        out_refs[0][...] = res
    else:
        @pl.when(i < N_PROMPT_TILES)
        def _():
            out_refs[0][...] = res

        @pl.when((i >= N_PROMPT_TILES) & (i < N_PROMPT_TILES + N_SAMPLE_TILES))
        def _():
            out_refs[1][...] = res


def moe_combine_dense(h, rec, ga, gb, split=False):
    t, d = h.shape
    row = lambda i: (i, 0)
    if split:
        out_shape = (jax.ShapeDtypeStruct((T_PROMPT, d), F32), jax.ShapeDtypeStruct((T_SAMPLE, d), F32))
        out_specs = tuple(_parts_specs(d)[:2])
    else:
        out_shape = jax.ShapeDtypeStruct((t, d), F32)
        out_specs = pl.BlockSpec((ROW_TILE, d), row)
    return pl.pallas_call(
        functools.partial(_combine_dense_kernel, split=split),
        out_shape=out_shape,
        grid=(t // ROW_TILE,),
        in_specs=[pl.BlockSpec((ROW_TILE, d), row), pl.BlockSpec((ROW_TILE, LANES), row),
                  pl.BlockSpec((ROW_TILE, d), row), pl.BlockSpec((ROW_TILE, d), row)],
        out_specs=out_specs,
        compiler_params=_cparams(("arbitrary",)),
        name="moe_combine_dense",
    )(h, rec, ga, gb)


def _ffn_kernel(wblk_ref, we_ref, wlo_ref, whi_ref, xs_ref, w13_ref, w2_ref, ys_ref, w13b, w2b):
    w = pl.program_id(0)
    prev = jnp.maximum(w - 1, 0)
    first_visit = (w == 0) | (wblk_ref[w] != wblk_ref[prev])
    lo = wlo_ref[w]
    hi = whi_ref[w]

    @pl.when(hi > lo)
    def _():
        @pl.when((w == 0) | (we_ref[w] != we_ref[prev]))
        def _():
            w13b[...] = w13_ref[...].astype(BF16)
            w2b[...] = w2_ref[...].astype(BF16)

        x = xs_ref[...].astype(BF16)
        cw = D_EXPERT // FFN_CHUNKS
        gate_up = []
        for c in range(FFN_CHUNKS):
            a = jnp.dot(x, w13b[:, c * cw:(c + 1) * cw], preferred_element_type=F32)
            u = jnp.dot(x, w13b[:, D_EXPERT + c * cw:D_EXPERT + (c + 1) * cw], preferred_element_type=F32)
            gate_up.append((a, u))
        hmid = jnp.concatenate([(_silu(a) * u).astype(BF16) for a, u in gate_up], axis=1)
        y = jnp.dot(hmid, w2b[...], preferred_element_type=F32)
        whole = (lo == 0) & (hi == EXPERT_BLOCK)

        @pl.when(whole)
        def _():
            ys_ref[...] = y

        row = lax.broadcasted_iota(jnp.int32, y.shape, 0)
        mine = (row >= lo) & (row < hi)

        @pl.when(jnp.logical_not(whole) & first_visit)
        def _():
            ys_ref[...] = jnp.where(mine, y, 0.0)

        @pl.when(jnp.logical_not(whole) & jnp.logical_not(first_visit))
        def _():
            ys_ref[...] = jnp.where(mine, y, ys_ref[...])


def moe_ffn(xs, work, w13_all, w2_all, layer):
    n_slots, d = xs.shape
    n_work = work[0].shape[0]
    xmap = lambda w, wb, we, wlo, whi: (wb[w], 0)
    w_map = lambda w, wb, we, wlo, whi: (layer, we[w], 0, 0)
    grid_spec = pltpu.PrefetchScalarGridSpec(
        num_scalar_prefetch=4,
        grid=(n_work,),
        in_specs=[pl.BlockSpec((EXPERT_BLOCK, d), xmap),
                  pl.BlockSpec((None, None, d, 2 * D_EXPERT), w_map),
                  pl.BlockSpec((None, None, D_EXPERT, d), w_map)],
        out_specs=pl.BlockSpec((EXPERT_BLOCK, d), xmap),
        scratch_shapes=[pltpu.VMEM((d, 2 * D_EXPERT), BF16), pltpu.VMEM((D_EXPERT, d), BF16)],
    )
    return pl.pallas_call(
        _ffn_kernel,
        out_shape=jax.ShapeDtypeStruct((n_slots, d), F32),
        grid_spec=grid_spec,
        compiler_params=_cparams(("arbitrary",)),
        name="moe_ffn",
    )(*work, xs, w13_all, w2_all)


def _ffn_work_items(cnt):
    n_slots = 2 * T_ALL
    n_blocks = n_slots // EXPERT_BLOCK
    n_work = n_blocks + N_EXPERTS - 1
    end = jnp.cumsum(cnt)
    start = end - cnt
    first_blk = start // EXPERT_BLOCK
    last_blk = jnp.maximum(end - 1, start) // EXPERT_BLOCK
    n_items = jnp.where(cnt > 0, last_blk - first_blk + 1, 0)
    item_end = jnp.cumsum(n_items)
    item_start = item_end - n_items
    w = jnp.arange(n_work, dtype=jnp.int32)
    used = w < item_end[-1]
    wq = jnp.minimum(w, item_end[-1] - 1)
    e = jnp.sum((item_end[None, :] <= wq[:, None]).astype(jnp.int32), axis=1)
    blk = jnp.where(used, first_blk[e] + (w - item_start[e]), n_blocks - 1).astype(jnp.int32)
    lo = jnp.maximum(start[e], blk * EXPERT_BLOCK) - blk * EXPERT_BLOCK
    hi = jnp.minimum(end[e], (blk + 1) * EXPERT_BLOCK) - blk * EXPERT_BLOCK
    lo = jnp.where(used, lo, 0).astype(jnp.int32)
    hi = jnp.where(used, hi, 0).astype(jnp.int32)
    return start, (blk, e, lo, hi)


def _combine_kernel(dest_ref, h_ref, rec_ref, ys_ref, *rest, split):
    out_refs, (g1, g2, sem) = rest[:-3], rest[-3:]
    rows = h_ref.shape[0]
    i = pl.program_id(0)

    def issue(r, c):
        _row_copy(ys_ref, dest_ref[0, 0, r], g1, r, sem).start(priority=0)
        _row_copy(ys_ref, dest_ref[0, 0, rows + r], g2, r, sem).start(priority=1)
        return c

    lax.fori_loop(0, rows, issue, 0, unroll=8)
    for buf in (g1, g2):
        pltpu.make_async_copy(ys_ref.at[pl.ds(0, rows)], buf, sem).wait()
    rec = rec_ref[...]
    res = h_ref[...] + rec[:, 4:5] * g1[...] + rec[:, 5:6] * g2[...]
    if not split:
        out_refs[0][...] = res
    else:
        @pl.when(i < N_PROMPT_TILES)
        def _():
            out_refs[0][...] = res

        @pl.when((i >= N_PROMPT_TILES) & (i < N_PROMPT_TILES + N_SAMPLE_TILES))
        def _():
            out_refs[1][...] = res


def moe_combine(h, rec, ys, dest3, split=False):
    t, d = h.shape
    row = lambda i: (i, 0)
    if split:
        parts = _parts_specs(d)[:2]
        out_shape = (jax.ShapeDtypeStruct((T_PROMPT, d), F32), jax.ShapeDtypeStruct((T_SAMPLE, d), F32))
        out_specs = tuple(parts)
    else:
        out_shape = jax.ShapeDtypeStruct((t, d), F32)
        out_specs = pl.BlockSpec((ROW_TILE, d), row)
    return pl.pallas_call(
        functools.partial(_combine_kernel, split=split),
        out_shape=out_shape,
        grid=(t // ROW_TILE,),
        in_specs=[pl.BlockSpec((1, 1, 2 * ROW_TILE), lambda i: (i, 0, 0), memory_space=pltpu.SMEM),
                  pl.BlockSpec((ROW_TILE, d), row), pl.BlockSpec((ROW_TILE, LANES), row),
                  pl.BlockSpec(memory_space=pl.ANY)],
        out_specs=out_specs,
        scratch_shapes=[pltpu.VMEM((ROW_TILE, d), F32), pltpu.VMEM((ROW_TILE, d), F32),
                        pltpu.SemaphoreType.DMA],
        compiler_params=_cparams(("arbitrary",)),
        name="moe_combine",
    )(dest3, h, rec, ys)


def hier_moe_layer(h, layer, gain, w_group, b_group, w_expert, b_expert, w13_all, w2_all, ltri, split_out=False):
    t = h.shape[0]
    pad = LANES - N_EXPERTS - N_GROUPS
    w_router = jnp.concatenate([w_expert, w_group, jnp.zeros((D_MODEL, pad), F32)], axis=1)
    b_router = jnp.concatenate([b_expert, b_group, jnp.zeros((pad,), F32)]).reshape(1, LANES)
    xn, rec, rect, counts = moe_route(h, gain, w_router, b_router, ltri)

    cnt = counts[0, :N_EXPERTS].astype(jnp.int32)
    start, work = _ffn_work_items(cnt)
    experts = jnp.arange(N_EXPERTS, dtype=jnp.int32)[:, None]

    def slot_of(e_row, rank_row):
        first = jnp.sum(jnp.where(e_row.astype(jnp.int32)[None, :] == experts, start[:, None], 0), axis=0)
        return (first + rank_row.astype(jnp.int32)).reshape(1, t)

    dest_a = slot_of(rect[0], rect[2])
    dest_b = slot_of(rect[1], rect[3])

    xs = moe_dispatch_sc(xn, dest_a, dest_b, 2 * t)
    ys = moe_ffn(xs, work, w13_all, w2_all, layer)
    ga, gb = moe_gather_sc(ys, dest_a, dest_b)
    return moe_combine_dense(h, rec, ga, gb, split=split_out)


def _rope_tables(pos):
    half = ROPE_DIM // 2
    inv = jnp.exp(-math.log(ROPE_THETA) * jnp.arange(half, dtype=F32) * (2.0 / ROPE_DIM))
    ang = pos.astype(F32)[:, None] * inv[None, :]
    cos, sin = jnp.cos(ang), jnp.sin(ang)
    t = pos.shape[0]
    ones = jnp.ones((t, HEAD_DIM - ROPE_DIM), F32)
    zeros = jnp.zeros((t, HEAD_DIM - ROPE_DIM), F32)
    z8 = jnp.zeros((t, half), F32)
    cos_h = jnp.concatenate([cos, cos, ones], axis=1)
    sina_h = jnp.concatenate([-sin, z8, zeros], axis=1)
    sinb_h = jnp.concatenate([z8, sin, zeros], axis=1)
    two = lambda a: jnp.concatenate([a, a], axis=1)
    return two(cos_h), two(sina_h), two(sinb_h)


def kernel(x_prompt, x_sample, state_hgrn, cache_k_win, cache_v_win, meta_tokens, a_norm, a_w_in, a_lower_logits, a_out_norm, a_w_out, kv_norm, kv_w, k_norm, b_norm, b_wq, b_q_norm, b_sinks, b_w_out, moe_norm, moe_w_group, moe_b_group, moe_w_expert, moe_b_expert, moe_w13, moe_w2):
    tail_rows = T_ALL - OFF_META
    x_parts = (x_prompt.reshape(T_PROMPT, D_MODEL), x_sample.reshape(T_SAMPLE, D_MODEL),
               jnp.concatenate([meta_tokens.astype(F32), jnp.zeros((tail_rows - N_META, D_MODEL), F32)], axis=0))
    pos = jnp.concatenate([N_META + jnp.arange(SEQ, dtype=jnp.int32),
                           jnp.tile(PAST_LEN + jnp.arange(DEC_SEQ, dtype=jnp.int32), ROW_TILE // DEC_SEQ),
                           jnp.arange(N_META, dtype=jnp.int32),
                           jnp.zeros((ROW_TILE - N_META,), jnp.int32)])
    cos_t, sina_t, sinb_t = _rope_tables(pos)
    r256 = np.arange(256)
    hmean = jnp.asarray((r256[:, None] // HEAD_DIM == r256[None, :] // HEAD_DIM).astype(np.float32) / HEAD_DIM, BF16)
    ltri = jnp.asarray((r256[None, :] < r256[:, None]).astype(np.float32), BF16)
    lower = jnp.cumsum(jax.nn.softmax(a_lower_logits.astype(F32), axis=0), axis=0)

    moe = functools.partial(hier_moe_layer, w13_all=moe_w13, w2_all=moe_w2, ltri=ltri)

    z = in_project(x_parts, a_norm[0], a_w_in[0].astype(BF16))
    zero_state = jnp.zeros((1, A_HEADS, A_DK, A_DV), F32)
    o_meta, s_meta = hgrn2_scan(z, zero_state, lower[0], a_out_norm[0],
                                row_off=OFF_META, n_seq=1, seq_len=N_META)
    o_prompt, s_prompt = hgrn2_scan(z, s_meta, lower[0], a_out_norm[0], row_off=0, n_seq=BATCH, seq_len=SEQ)
    o_sample, s_sample = hgrn2_scan(z, state_hgrn[0].astype(F32), lower[0], a_out_norm[0],
                                    row_off=OFF_SAMPLE, n_seq=DEC_BATCH, seq_len=DEC_SEQ, group=SCAN_SAMPLE_GROUP)
    o_tail = jnp.concatenate([o_meta, jnp.zeros((tail_rows - N_META, D_MODEL), BF16)], axis=0)
    h = mixer_out((o_prompt, o_sample, o_tail), a_w_out[0].astype(BF16), x_parts)
    h = moe(h, 0, moe_norm[0], moe_w_group[0], moe_b_group[0], moe_w_expert[0], moe_b_expert[0])

    k_all, v_all = kv_project(h, kv_norm, kv_w.astype(BF16), hmean, jnp.tile(k_norm, KV_HEADS).reshape(1, KV_DIM),
                              cos_t, sina_t, sinb_t)

    q_all = q_project(h, b_norm[0], b_wq[0].astype(BF16), hmean, jnp.tile(b_q_norm[0], Q_HEADS).reshape(1, D_MODEL),
                      cos_t, sina_t, sinb_t)
    meta_blk = lambda a: jnp.concatenate([jnp.zeros((ATT_BLOCK - N_META, KV_DIM), F32),
                                          a[OFF_META:OFF_META + N_META]], axis=0)
    sinks = b_sinks[0].astype(F32)
    att_all = attention_prompt(q_all, k_all, v_all, meta_blk(k_all), meta_blk(v_all), sinks)
    att_all = attention_sample(q_all, cache_k_win.reshape(DEC_BATCH * WINDOW, KV_DIM).astype(F32),
                               cache_v_win.reshape(DEC_BATCH * WINDOW, KV_DIM).astype(F32),
                               k_all, v_all, sinks, att_all)
    h = matmul_residual(att_all, b_w_out[0].astype(BF16), h)
    y_p, y_s = moe(h, 1, moe_norm[1], moe_w_group[1], moe_b_group[1], moe_w_expert[1], moe_b_expert[1],
                   split_out=True)

    y_prompt = y_p.reshape(BATCH, SEQ, D_MODEL)
    y_sample = y_s.reshape(DEC_BATCH, DEC_SEQ, D_MODEL)
    last = lambda a: jnp.stack([a[(b + 1) * SEQ - WINDOW:(b + 1) * SEQ] for b in range(BATCH)]).reshape(
        BATCH, WINDOW, KV_HEADS, HEAD_DIM)
    kp = last(k_all)
    vp = last(v_all)
    ks = k_all[OFF_SAMPLE:OFF_SAMPLE + T_SAMPLE].reshape(DEC_BATCH, DEC_SEQ, KV_HEADS, HEAD_DIM)
    vs = v_all[OFF_SAMPLE:OFF_SAMPLE + T_SAMPLE].reshape(DEC_BATCH, DEC_SEQ, KV_HEADS, HEAD_DIM)
    k_win_s = jnp.concatenate([cache_k_win, ks], axis=1)[:, -WINDOW:]
    v_win_s = jnp.concatenate([cache_v_win, vs], axis=1)[:, -WINDOW:]
    return (y_prompt, y_sample, s_prompt[None], s_sample[None], kp, vp, k_win_s, v_win_s)
```

```python
import functools
import math

import numpy as np
import jax
import jax.numpy as jnp
from jax import lax
from jax.experimental import pallas as pl
from jax.experimental.pallas import tpu as pltpu
from jax.experimental.pallas import tpu_sc as plsc

F32 = jnp.float32
BF16 = jnp.bfloat16

D_MODEL = 1024
BATCH = 4
SEQ = 4096
DEC_BATCH = 128
DEC_SEQ = 8
PAST_LEN = 8192
N_META = 16
A_HEADS = 8
A_DK = 128
A_DV = 128
Q_HEADS = 16
KV_HEADS = 4
HEAD_DIM = 64
KV_DIM = KV_HEADS * HEAD_DIM
WINDOW = 128
ROPE_DIM = 16
ROPE_THETA = 500000.0
N_GROUPS = 4
EXPERTS_PER_GROUP = 8
N_EXPERTS = 32
D_EXPERT = 512
RMS_EPS = 1e-6

LANES = 128
SUBLANES = 8
VMEM_LIMIT = 56 * 1024 * 1024

ROW_TILE = 256
T_PROMPT = BATCH * SEQ
T_SAMPLE = DEC_BATCH * DEC_SEQ
OFF_SAMPLE = T_PROMPT
OFF_META = T_PROMPT + T_SAMPLE
T_REAL = OFF_META + N_META
T_ALL = -(-T_REAL // ROW_TILE) * ROW_TILE
N_TILES = T_ALL // ROW_TILE

SCAN_CHUNK = 128
SCAN_SAMPLE_GROUP = 8
ATT_BLOCK = 128
EXPERT_BLOCK = 512
FFN_CHUNKS = 2


def _cparams(sem):
    return pltpu.CompilerParams(dimension_semantics=sem, vmem_limit_bytes=VMEM_LIMIT)


def _nt_dot(a, b):
    return lax.dot_general(a, b, (((1,), (1,)), ((), ())), preferred_element_type=F32)


def _rms(x, gain):
    ms = jnp.mean(x * x, axis=-1, keepdims=True)
    return x * lax.rsqrt(ms + RMS_EPS) * gain


def _silu(x):
    return x * jax.nn.sigmoid(x)


N_PROMPT_TILES = T_PROMPT // ROW_TILE
N_SAMPLE_TILES = T_SAMPLE // ROW_TILE


def _parts_specs(width):
    return [pl.BlockSpec((ROW_TILE, width), lambda i: (jnp.minimum(i, N_PROMPT_TILES - 1), 0)),
            pl.BlockSpec((ROW_TILE, width), lambda i: (jnp.clip(i - N_PROMPT_TILES, 0, N_SAMPLE_TILES - 1), 0)),
            pl.BlockSpec((ROW_TILE, width), lambda i: (0, 0))]


def _pick_part(i, p_ref, s_ref, t_ref, dtype):
    return jnp.where(i < N_PROMPT_TILES, p_ref[...].astype(dtype),
                     jnp.where(i < N_PROMPT_TILES + N_SAMPLE_TILES, s_ref[...].astype(dtype),
                               t_ref[...].astype(dtype)))


def _in_proj_kernel(xp_ref, xs_ref, xt_ref, g_ref, w_ref, o_ref):
    x = _pick_part(pl.program_id(0), xp_ref, xs_ref, xt_ref, F32)
    xn = _rms(x, g_ref[...])
    o_ref[...] = jnp.dot(xn.astype(BF16), w_ref[...], preferred_element_type=F32)


def in_project(x_parts, gain, w_bf16):
    d, n = w_bf16.shape
    return pl.pallas_call(
        _in_proj_kernel,
        out_shape=jax.ShapeDtypeStruct((T_ALL, n), F32),
        grid=(N_TILES,),
        in_specs=_parts_specs(d) + [pl.BlockSpec((1, d), lambda i: (0, 0)),
                                    pl.BlockSpec((d, n), lambda i: (0, 0))],
        out_specs=pl.BlockSpec((ROW_TILE, n), lambda i: (i, 0)),
        compiler_params=_cparams(("parallel",)),
        name="in_project",
    )(*x_parts, gain.reshape(1, d), w_bf16)


def _mixer_out_kernel(ap_ref, as_ref, at_ref, w_ref, xp_ref, xs_ref, xt_ref, o_ref):
    i = pl.program_id(0)
    a = _pick_part(i, ap_ref, as_ref, at_ref, BF16)
    x = _pick_part(i, xp_ref, xs_ref, xt_ref, F32)
    o_ref[...] = x + jnp.dot(a, w_ref[...], preferred_element_type=F32)


def mixer_out(a_parts, w_bf16, x_parts):
    k, n = w_bf16.shape
    return pl.pallas_call(
        _mixer_out_kernel,
        out_shape=jax.ShapeDtypeStruct((T_ALL, n), F32),
        grid=(N_TILES,),
        in_specs=_parts_specs(k) + [pl.BlockSpec((k, n), lambda i: (0, 0))] + _parts_specs(n),
        out_specs=pl.BlockSpec((ROW_TILE, n), lambda i: (i, 0)),
        compiler_params=_cparams(("parallel",)),
        name="mixer_out",
    )(*a_parts, w_bf16, *x_parts)


def _head_norm_rope(y, hmean_ref, hgain, cos_t, sina_t, sinb_t):
    rows, width = y.shape
    sq = (y * y).astype(BF16)
    parts = []
    for s in range(width // 256):
        parts.append(jnp.dot(sq[:, s * 256:(s + 1) * 256], hmean_ref[...], preferred_element_type=F32))
    ms = parts[0] if len(parts) == 1 else jnp.concatenate(parts, axis=1)
    yn = y * lax.rsqrt(ms + RMS_EPS) * hgain
    reps = width // LANES
    cos_w = jnp.concatenate([cos_t] * reps, axis=1)
    sina_w = jnp.concatenate([sina_t] * reps, axis=1)
    sinb_w = jnp.concatenate([sinb_t] * reps, axis=1)
    half = ROPE_DIM // 2
    nxt = pltpu.roll(yn, width - half, 1)
    prv = pltpu.roll(yn, half, 1)
    return yn * cos_w + nxt * sina_w + prv * sinb_w


def _kv_kernel(x_ref, g_ref, w_ref, hmean_ref, hg_ref, cos_ref, sina_ref, sinb_ref, k_ref, v_ref):
    xn = _rms(x_ref[...], g_ref[...])
    z = jnp.dot(xn.astype(BF16), w_ref[...], preferred_element_type=F32)
    k = _head_norm_rope(z[:, :KV_DIM], hmean_ref, hg_ref[...], cos_ref[...], sina_ref[...], sinb_ref[...])
    k_ref[...] = k
    v_ref[...] = z[:, KV_DIM:]


def _rope_tile(i):
    tiles_per_seq = SEQ // ROW_TILE
    n_prompt_tiles = T_PROMPT // ROW_TILE
    n_sample_tiles = T_SAMPLE // ROW_TILE
    return (jnp.where(i < n_prompt_tiles, i % tiles_per_seq,
                      jnp.where(i < n_prompt_tiles + n_sample_tiles, tiles_per_seq, tiles_per_seq + 1)), 0)


def kv_project(x, gain, w_bf16, hmean, hgain_w, cos_t, sina_t, sinb_t):
    t, d = x.shape
    row = lambda i: (i, 0)
    fix = lambda i: (0, 0)
    return pl.pallas_call(
        _kv_kernel,
        out_shape=(jax.ShapeDtypeStruct((t, KV_DIM), F32), jax.ShapeDtypeStruct((t, KV_DIM), F32)),
        grid=(t // ROW_TILE,),
        in_specs=[pl.BlockSpec((ROW_TILE, d), row), pl.BlockSpec((1, d), fix),
                  pl.BlockSpec((d, 2 * KV_DIM), fix), pl.BlockSpec((256, 256), fix),
                  pl.BlockSpec((1, KV_DIM), fix),
                  pl.BlockSpec((ROW_TILE, LANES), _rope_tile), pl.BlockSpec((ROW_TILE, LANES), _rope_tile),
                  pl.BlockSpec((ROW_TILE, LANES), _rope_tile)],
        out_specs=(pl.BlockSpec((ROW_TILE, KV_DIM), row), pl.BlockSpec((ROW_TILE, KV_DIM), row)),
        compiler_params=_cparams(("parallel",)),
        name="kv_project",
    )(x, gain.reshape(1, d), w_bf16, hmean, hgain_w, cos_t, sina_t, sinb_t)


def _q_kernel(x_ref, g_ref, w_ref, hmean_ref, hg_ref, cos_ref, sina_ref, sinb_ref, q_ref):
    xn = _rms(x_ref[...], g_ref[...])
    z = jnp.dot(xn.astype(BF16), w_ref[...], preferred_element_type=F32)
    q = _head_norm_rope(z, hmean_ref, hg_ref[...], cos_ref[...], sina_ref[...], sinb_ref[...])
    q_ref[...] = (q * HEAD_DIM ** -0.5).astype(q_ref.dtype)


def q_project(x, gain, w_bf16, hmean, hgain_w, cos_t, sina_t, sinb_t):
    t, d = x.shape
    row = lambda i: (i, 0)
    fix = lambda i: (0, 0)
    return pl.pallas_call(
        _q_kernel,
        out_shape=jax.ShapeDtypeStruct((t, d), BF16),
        grid=(t // ROW_TILE,),
        in_specs=[pl.BlockSpec((ROW_TILE, d), row), pl.BlockSpec((1, d), fix),
                  pl.BlockSpec((d, d), fix), pl.BlockSpec((256, 256), fix),
                  pl.BlockSpec((1, d), fix),
                  pl.BlockSpec((ROW_TILE, LANES), _rope_tile), pl.BlockSpec((ROW_TILE, LANES), _rope_tile),
                  pl.BlockSpec((ROW_TILE, LANES), _rope_tile)],
        out_specs=pl.BlockSpec((ROW_TILE, d), row),
        compiler_params=_cparams(("parallel",)),
        name="q_project",
    )(x, gain.reshape(1, d), w_bf16, hmean, hgain_w, cos_t, sina_t, sinb_t)


def _matmul_residual_kernel(a_ref, w_ref, r_ref, o_ref):
    o_ref[...] = r_ref[...] + jnp.dot(a_ref[...], w_ref[...], preferred_element_type=F32)


def matmul_residual(a_bf16, w_bf16, resid):
    t, k = a_bf16.shape
    n = w_bf16.shape[1]
    return pl.pallas_call(
        _matmul_residual_kernel,
        out_shape=jax.ShapeDtypeStruct((t, n), F32),
        grid=(t // ROW_TILE,),
        in_specs=[pl.BlockSpec((ROW_TILE, k), lambda i: (i, 0)),
                  pl.BlockSpec((k, n), lambda i: (0, 0)),
                  pl.BlockSpec((ROW_TILE, n), lambda i: (i, 0))],
        out_specs=pl.BlockSpec((ROW_TILE, n), lambda i: (i, 0)),
        compiler_params=_cparams(("parallel",)),
        name="matmul_residual",
    )(a_bf16, w_bf16, resid)


def _scan_levels(c):
    levels = []
    m = c
    while m >= 2:
        levels.append(m)
        m //= 2
    return levels


LOG2E = 1.4426950408889634


def _scan_kernel(z_ref, s0_ref, lb_ref, og_ref, tri_ref, lmask_ref, sgn_ref, o_ref, sfin_ref, s_scr, b_scr,
                 *, rows, seq_len):
    c_idx = pl.program_id(1)
    levels = _scan_levels(seq_len)
    n_sub = rows // seq_len
    hk = A_HEADS * A_DK

    @pl.when(c_idx == 0)
    def _():
        s_scr[...] = s0_ref[...]

    sub = lax.broadcasted_iota(jnp.int32, (SUBLANES, LANES), 0)
    row = lax.broadcasted_iota(jnp.int32, (LANES, LANES), 0)
    og = og_ref[...]

    def pad_f32(x):
        if x.shape[0] == LANES:
            return x
        return jnp.concatenate([x, jnp.zeros((LANES - x.shape[0], x.shape[1]), x.dtype)], axis=0)

    def pad_rows(x):
        return pad_f32(x).astype(BF16)

    def cols(part, h):
        return slice(part * hk + h * LANES, part * hk + (h + 1) * LANES)

    def gates(h):
        lb = lb_ref[:, cols(0, h)]
        forget = lb + (1.0 - lb) * jax.nn.sigmoid(z_ref[:, cols(1, h)])
        logf = jnp.log(forget)
        hi = logf.astype(BF16).astype(F32)
        r1 = logf - hi
        mid = r1.astype(BF16).astype(F32)
        lo = r1 - mid
        cs = jnp.dot(tri_ref[...], pad_rows(jnp.concatenate([hi, mid, lo], axis=1)),
                     preferred_element_type=F32)
        b = (cs[:rows, :LANES] + cs[:rows, LANES:2 * LANES]) + cs[:rows, 2 * LANES:]
        b_scr[h] = b
        return _silu(z_ref[:, cols(0, h)]), 1.0 - forget, b

    def bref_for(h, m):
        b_rows = b_scr.at[h]
        half = m // 2
        pieces = []
        for g in range(rows // SUBLANES):
            base = g * SUBLANES
            if m >= SUBLANES:
                r = (base // m) * m + half - 1
                piece = jnp.broadcast_to(b_rows[r:r + 1, :], (SUBLANES, LANES))
            else:
                piece = jnp.broadcast_to(b_rows[base + half - 1:base + half, :], (SUBLANES, LANES))
                for blk in range(1, SUBLANES // m):
                    r = base + blk * m + half - 1
                    piece = jnp.where(sub >= blk * m,
                                      jnp.broadcast_to(b_rows[r:r + 1, :], (SUBLANES, LANES)), piece)
            pieces.append(piece)
        return pieces[0] if len(pieces) == 1 else jnp.concatenate(pieces, axis=0)

    heads = range(A_HEADS)
    qkb = [gates(h) for h in heads]
    att = [_nt_dot(pad_rows(qf), pad_rows(kf)) * lmask_ref[len(levels)] for qf, kf, _ in qkb]
    for li, m in enumerate(levels):
        for h in heads:
            qf, kf, b = qkb[h]
            sgn = sgn_ref[li]
            e = jnp.exp2((b - bref_for(h, m)) * sgn)
            w = pad_rows(jnp.where(sgn > 0, qf, kf) * e)
            att[h] = att[h] + _nt_dot(w, w) * lmask_ref[li]

    def finish(h):
        qf, kf, b = qkb[h]
        b_rows = b_scr.at[h]
        v_b = pad_rows(z_ref[:, cols(2, h)])
        o_intra = jnp.dot(att[h].astype(BF16), v_b, preferred_element_type=F32)
        eb = jnp.exp(b)
        qs = qf * eb
        b_end = [jnp.broadcast_to(b_rows[(i + 1) * seq_len - 1:(i + 1) * seq_len, :], (seq_len, LANES))
                 for i in range(n_sub)]
        b_end = b_end[0] if n_sub == 1 else jnp.concatenate(b_end, axis=0)
        kd_t = pad_f32(kf * jnp.exp(b_end - b)).T.astype(BF16)
        eb_t = pad_f32(eb).T
        qs_b = pad_rows(qs)
        o = o_intra
        for i in range(n_sub):
            s_old = s_scr[i, h]
            first, last = i * seq_len, (i + 1) * seq_len - 1
            if n_sub == 1:
                qs_i, v_i = qs_b, v_b
            else:
                mine = (row >= first) & (row <= last)
                qs_i = jnp.where(mine, qs_b, jnp.zeros_like(qs_b))
                v_i = jnp.where(mine, v_b, jnp.zeros_like(v_b))
            o = o + jnp.dot(qs_i, s_old.astype(BF16), preferred_element_type=F32)
            decay = jnp.broadcast_to(eb_t[:, last:last + 1], (LANES, LANES))
            s_scr[i, h] = decay * s_old + jnp.dot(kd_t, v_i, preferred_element_type=F32)
        o = o[:rows]

        on = _rms(o, og) * _silu(z_ref[:, cols(3, h)])
        o_ref[:, cols(0, h)] = on.astype(o_ref.dtype)

    for h in heads:
        finish(h)

    @pl.when(c_idx == pl.num_programs(1) - 1)
    def _():
        sfin_ref[...] = s_scr[...]


def _scan_consts(rows, seq_len):
    levels = _scan_levels(seq_len)
    r = np.arange(LANES)
    t, s = r[:, None], r[None, :]
    live = (t < rows) & (s < rows)
    tri = ((s <= t) & (t // seq_len == s // seq_len) & live).astype(np.float32)
    masks, sgns = [], []
    for m in levels:
        masks.append(((t // m == s // m) & (t % m >= m // 2) & (s % m < m // 2) & live).astype(np.float32))
        sgns.append(np.broadcast_to(np.where(r[:rows, None] % m >= m // 2, LOG2E, -LOG2E), (rows, LANES)))
    masks.append(((t == s) & live).astype(np.float32))
    return jnp.asarray(tri, BF16), jnp.asarray(np.stack(masks), F32), jnp.asarray(np.stack(sgns), F32)


def hgrn2_scan(z, s0, lb, o_gain, *, row_off, n_seq, seq_len, group=1):
    hv = A_HEADS * A_DV
    if seq_len > SCAN_CHUNK:
        assert group == 1
        sub_len, rows, n_chunks, n_steps = SCAN_CHUNK, SCAN_CHUNK, seq_len // SCAN_CHUNK, n_seq
    else:
        sub_len, rows, n_chunks, n_steps = seq_len, group * seq_len, 1, n_seq // group
    blk_off = row_off // rows
    tri, lmask, sgn = _scan_consts(rows, sub_len)
    shared_s0 = s0.shape[0] == 1
    fix2 = lambda s, c: (0, 0)
    fix3 = lambda s, c: (0, 0, 0)
    o, sfin = pl.pallas_call(
        functools.partial(_scan_kernel, rows=rows, seq_len=sub_len),
        out_shape=(jax.ShapeDtypeStruct((n_seq * seq_len, hv), BF16 if rows % 16 == 0 else F32),
                   jax.ShapeDtypeStruct((n_seq, A_HEADS, A_DK, A_DV), F32)),
        grid=(n_steps, n_chunks),
        in_specs=[pl.BlockSpec((rows, 4 * hv), lambda s, c: (blk_off + s * n_chunks + c, 0)),
                  pl.BlockSpec((group, A_HEADS, A_DK, A_DV), (lambda s, c: (0, 0, 0, 0)) if shared_s0
                               else (lambda s, c: (s, 0, 0, 0))),
                  pl.BlockSpec((1, hv), fix2), pl.BlockSpec((1, A_DV), fix2),
                  pl.BlockSpec((LANES, LANES), fix2), pl.BlockSpec(lmask.shape, fix3),
                  pl.BlockSpec(sgn.shape, fix3)],
        out_specs=(pl.BlockSpec((rows, hv), lambda s, c: (s * n_chunks + c, 0)),
                   pl.BlockSpec((group, A_HEADS, A_DK, A_DV), lambda s, c: (s, 0, 0, 0))),
        scratch_shapes=[pltpu.VMEM((group, A_HEADS, A_DK, A_DV), F32), pltpu.VMEM((A_HEADS, rows, LANES), F32)],
        compiler_params=_cparams(("parallel", "arbitrary")),
        name=f"hgrn2_scan_r{rows}",
    )(z, s0, lb.reshape(1, hv), o_gain.reshape(1, A_DV), tri, lmask, sgn)
    return o, sfin


KEYS = 2 * ATT_BLOCK


def _pair_operand(x, kh):
    slab = x[:, (kh // 2) * LANES:(kh // 2 + 1) * LANES]
    lane = lax.broadcasted_iota(jnp.int32, slab.shape, 1)
    if kh % 2 == 0:
        lo = jnp.where(lane < HEAD_DIM, slab, 0.0)
        hi = pltpu.roll(lo, HEAD_DIM, 1)
    else:
        hi = jnp.where(lane >= HEAD_DIM, slab, 0.0)
        lo = pltpu.roll(hi, HEAD_DIM, 1)
    return jnp.concatenate([lo, hi], axis=0).astype(BF16)


def _window_bias(rows, jmin):
    t_i = lax.broadcasted_iota(jnp.int32, (rows, 2 * KEYS), 0)
    c_i = lax.broadcasted_iota(jnp.int32, (rows, 2 * KEYS), 1)
    j_i = c_i & (ATT_BLOCK - 1)
    own = (c_i & ATT_BLOCK) != 0
    ok = (own & (j_i <= t_i)) | (jnp.logical_not(own) & (j_i >= t_i) & (j_i >= jmin))
    return jnp.where(ok, 0.0, -jnp.inf).astype(F32)


def _pair_softmax(s, sink_a, sink_b):
    probs, rinv = [], []
    for hh, sink in enumerate((sink_a, sink_b)):
        sh = s[:, hh * KEYS:(hh + 1) * KEYS]
        m = jnp.maximum(jnp.max(sh, axis=-1, keepdims=True), sink)
        p = jnp.exp(sh - m)
        den = jnp.sum(p, axis=-1, keepdims=True) + jnp.exp(sink - m)
        probs.append(p.astype(BF16))
        rinv.append(1.0 / den)
    lane = lax.broadcasted_iota(jnp.int32, (s.shape[0], LANES), 1)
    return jnp.concatenate(probs, axis=1), jnp.where(lane < HEAD_DIM, rinv[0], rinv[1])


def _attn_prompt_kernel(sink_ref, q_ref, kp_ref, ko_ref, vp_ref, vo_ref, km_ref, vm_ref, o_ref,
                        k2_scr, v2_scr, s_scr, p_scr, r_scr):
    n = pl.program_id(0)
    nbp = SEQ // ATT_BLOCK
    n_pairs = Q_HEADS // 2

    @pl.when(n >= BATCH * nbp)
    def _():
        o_ref[...] = jnp.zeros_like(o_ref)

    @pl.when(n < BATCH * nbp)
    def _():
        first = (n % nbp) == 0
        jmin = jnp.where(first, ATT_BLOCK - N_META, 0)
        k = jnp.concatenate([jnp.where(first, km_ref[...], kp_ref[...]), ko_ref[...]], axis=0)
        v = jnp.concatenate([jnp.where(first, vm_ref[...], vp_ref[...]), vo_ref[...]], axis=0)
        bias = _window_bias(ATT_BLOCK, jmin)
        for kh in range(KV_HEADS):
            k2_scr[kh] = _pair_operand(k, kh)
            v2_scr[kh] = _pair_operand(v, kh)
        for pair in range(n_pairs):
            s_scr[pair] = _nt_dot(q_ref[:, pair * LANES:(pair + 1) * LANES], k2_scr[pair // 2]) + bias
        for pair in range(n_pairs):
            p, rinv = _pair_softmax(s_scr[pair], sink_ref[2 * pair], sink_ref[2 * pair + 1])
            p_scr[pair] = p
            r_scr[pair] = rinv
        for pair in range(n_pairs):
            o = jnp.dot(p_scr[pair], v2_scr[pair // 2], preferred_element_type=F32) * r_scr[pair]
            o_ref[:, pair * LANES:(pair + 1) * LANES] = o.astype(o_ref.dtype)


def attention_prompt(q_all, k_all, v_all, k_meta_blk, v_meta_blk, sinks):
    n_prompt_blocks = T_PROMPT // ATT_BLOCK
    n_blocks = T_ALL // ATT_BLOCK
    n_pairs = Q_HEADS // 2
    own = lambda n, sk: (jnp.minimum(n, n_prompt_blocks - 1), 0)
    prev = lambda n, sk: (jnp.maximum(jnp.minimum(n, n_prompt_blocks - 1) - 1, 0), 0)
    fix = lambda n, sk: (0, 0)
    grid_spec = pltpu.PrefetchScalarGridSpec(
        num_scalar_prefetch=1,
        grid=(n_blocks,),
        in_specs=[pl.BlockSpec((ATT_BLOCK, D_MODEL), own),
                  pl.BlockSpec((ATT_BLOCK, KV_DIM), prev), pl.BlockSpec((ATT_BLOCK, KV_DIM), own),
                  pl.BlockSpec((ATT_BLOCK, KV_DIM), prev), pl.BlockSpec((ATT_BLOCK, KV_DIM), own),
                  pl.BlockSpec((ATT_BLOCK, KV_DIM), fix), pl.BlockSpec((ATT_BLOCK, KV_DIM), fix)],
        out_specs=pl.BlockSpec((ATT_BLOCK, D_MODEL), lambda n, sk: (n, 0)),
        scratch_shapes=[pltpu.VMEM((KV_HEADS, 2 * KEYS, LANES), BF16), pltpu.VMEM((KV_HEADS, 2 * KEYS, LANES), BF16),
                        pltpu.VMEM((n_pairs, ATT_BLOCK, 2 * KEYS), F32),
                        pltpu.VMEM((n_pairs, ATT_BLOCK, 2 * KEYS), BF16),
                        pltpu.VMEM((n_pairs, ATT_BLOCK, LANES), F32)],
    )
    return pl.pallas_call(
        _attn_prompt_kernel,
        out_shape=jax.ShapeDtypeStruct((T_ALL, D_MODEL), BF16),
        grid_spec=grid_spec,
        compiler_params=_cparams(("parallel",)),
        name="attention_prompt",
    )(sinks, q_all, k_all, k_all, v_all, v_all, k_meta_blk, v_meta_blk)


SAMPLE_GROUP = ATT_BLOCK // DEC_SEQ


def _attn_sample_kernel(sink_ref, q_ref, ck_ref, cv_ref, kn_ref, vn_ref, buf_ref, o_ref,
                        qf_scr, of_scr, k2_scr, v2_scr):
    del buf_ref
    qrows = 2 * DEC_SEQ
    qf_scr[...] = q_ref[...].astype(F32)
    bias = _window_bias(qrows, 0)
    zq = jnp.zeros((qrows - DEC_SEQ, D_MODEL), F32)
    zk = jnp.zeros((ATT_BLOCK - DEC_SEQ, KV_DIM), F32)

    def seq_body(i, carry):
        r_new = pl.multiple_of(i * DEC_SEQ, DEC_SEQ)
        r_old = pl.multiple_of(i * WINDOW, WINDOW)
        q = jnp.concatenate([qf_scr[pl.ds(r_new, DEC_SEQ), :], zq], axis=0).astype(BF16)
        k = jnp.concatenate([ck_ref[pl.ds(r_old, WINDOW), :], kn_ref[pl.ds(r_new, DEC_SEQ), :], zk], axis=0)
        v = jnp.concatenate([cv_ref[pl.ds(r_old, WINDOW), :], vn_ref[pl.ds(r_new, DEC_SEQ), :], zk], axis=0)
        for kh in range(KV_HEADS):
            k2_scr[kh] = _pair_operand(k, kh)
            v2_scr[kh] = _pair_operand(v, kh)
        scores = [_nt_dot(q[:, pair * LANES:(pair + 1) * LANES], k2_scr[pair // 2]) + bias
                  for pair in range(Q_HEADS // 2)]
        soft = [_pair_softmax(s, sink_ref[2 * pair], sink_ref[2 * pair + 1]) for pair, s in enumerate(scores)]
        for pair, (p, rinv) in enumerate(soft):
            o = jnp.dot(p, v2_scr[pair // 2], preferred_element_type=F32) * rinv
            of_scr[pl.ds(r_new, DEC_SEQ), pair * LANES:(pair + 1) * LANES] = o[:DEC_SEQ]
        return carry

    lax.fori_loop(0, SAMPLE_GROUP, seq_body, 0)
    o_ref[...] = of_scr[...].astype(o_ref.dtype)


def attention_sample(q_all, cache_k, cache_v, k_all, v_all, sinks, out_buf):
    first_blk = OFF_SAMPLE // ATT_BLOCK
    new = lambda g, sk: (first_blk + g, 0)
    old = lambda g, sk: (g, 0)
    grid_spec = pltpu.PrefetchScalarGridSpec(
        num_scalar_prefetch=1,
        grid=(DEC_BATCH // SAMPLE_GROUP,),
        in_specs=[pl.BlockSpec((ATT_BLOCK, D_MODEL), new),
                  pl.BlockSpec((SAMPLE_GROUP * WINDOW, KV_DIM), old),
                  pl.BlockSpec((SAMPLE_GROUP * WINDOW, KV_DIM), old),
                  pl.BlockSpec((ATT_BLOCK, KV_DIM), new), pl.BlockSpec((ATT_BLOCK, KV_DIM), new),
                  pl.BlockSpec(memory_space=pl.ANY)],
        out_specs=pl.BlockSpec((ATT_BLOCK, D_MODEL), new),
        scratch_shapes=[pltpu.VMEM((ATT_BLOCK, D_MODEL), F32), pltpu.VMEM((ATT_BLOCK, D_MODEL), F32),
                        pltpu.VMEM((KV_HEADS, 2 * KEYS, LANES), BF16), pltpu.VMEM((KV_HEADS, 2 * KEYS, LANES), BF16)],
    )
    return pl.pallas_call(
        _attn_sample_kernel,
        out_shape=jax.ShapeDtypeStruct(out_buf.shape, out_buf.dtype),
        grid_spec=grid_spec,
        input_output_aliases={6: 0},
        compiler_params=_cparams(("parallel",)),
        name="attention_sample",
    )(sinks, q_all, cache_k, cache_v, k_all, v_all, out_buf)


ROUTE_COLS = 8


def _route_kernel(x_ref, g_ref, wh_ref, wl_ref, br_ref, ltri_ref, xn_ref, rec_ref, rect_ref, cnt_ref, cnt_scr):
    i = pl.program_id(0)

    @pl.when(i == 0)
    def _():
        cnt_scr[...] = jnp.zeros_like(cnt_scr)

    xn = _rms(x_ref[...], g_ref[...])
    xn_ref[...] = xn
    xh = xn.astype(BF16)
    xl = (xn - xh.astype(F32)).astype(BF16)
    logits = (jnp.dot(xh, wh_ref[...], preferred_element_type=F32)
              + (jnp.dot(xh, wl_ref[...], preferred_element_type=F32)
                 + jnp.dot(xl, wh_ref[...], preferred_element_type=F32))) + br_ref[...]
    rows = logits.shape[0]
    lane = lax.broadcasted_iota(jnp.int32, (rows, LANES), 1).astype(F32)
    neg = jnp.float32(-jnp.inf)
    big = jnp.float32(LANES)

    is_g = (lane >= N_EXPERTS) & (lane < N_EXPERTS + N_GROUPS)
    gl = jnp.where(is_g, logits, neg)
    gmax = jnp.max(gl, axis=-1, keepdims=True)
    gsel = jnp.min(jnp.where(gl == gmax, lane, big), axis=-1, keepdims=True) - N_EXPERTS
    gden = jnp.sum(jnp.where(is_g, jnp.exp(gl - gmax), 0.0), axis=-1, keepdims=True)
    gw = 1.0 / gden

    in_grp = (lane >= gsel * EXPERTS_PER_GROUP) & (lane < (gsel + 1) * EXPERTS_PER_GROUP)
    el = jnp.where(in_grp, logits, neg)
    t1 = jnp.max(el, axis=-1, keepdims=True)
    e1 = jnp.min(jnp.where(el == t1, lane, big), axis=-1, keepdims=True)
    el2 = jnp.where(lane == e1, neg, el)
    t2 = jnp.max(el2, axis=-1, keepdims=True)
    e2 = jnp.min(jnp.where(el2 == t2, lane, big), axis=-1, keepdims=True)
    x2 = jnp.exp(t2 - t1)
    w1 = gw / (1.0 + x2)
    w2 = gw * x2 / (1.0 + x2)

    oh1 = (lane == e1).astype(F32)
    oh2 = (lane == e2).astype(F32)
    oh = oh1 + oh2
    before = jnp.dot(ltri_ref[...], oh.astype(BF16), preferred_element_type=F32)
    base = cnt_scr[...] + before
    r1 = jnp.sum(base * oh1, axis=-1, keepdims=True)
    r2 = jnp.sum(base * oh2, axis=-1, keepdims=True)
    cnt_scr[...] = cnt_scr[...] + jnp.sum(oh, axis=0, keepdims=True)

    rec = jnp.where(lane == 0, e1,
          jnp.where(lane == 1, e2,
          jnp.where(lane == 2, r1,
          jnp.where(lane == 3, r2,
          jnp.where(lane == 4, w1,
          jnp.where(lane == 5, w2, 0.0))))))
    rec_ref[...] = rec
    rect_ref[...] = jnp.concatenate([rec[:LANES].T[:ROUTE_COLS], rec[LANES:].T[:ROUTE_COLS]], axis=1)
    cnt_ref[...] = cnt_scr[...]


def moe_route(h, gain, w_router, b_router, ltri):
    t, d = h.shape
    row = lambda i: (i, 0)
    fix = lambda i: (0, 0)
    w_hi = w_router.astype(BF16)
    w_lo = (w_router - w_hi.astype(F32)).astype(BF16)
    return pl.pallas_call(
        _route_kernel,
        out_shape=(jax.ShapeDtypeStruct((t, d), F32), jax.ShapeDtypeStruct((t, LANES), F32),
                   jax.ShapeDtypeStruct((ROUTE_COLS, t), F32), jax.ShapeDtypeStruct((1, LANES), F32)),
        grid=(t // ROW_TILE,),
        in_specs=[pl.BlockSpec((ROW_TILE, d), row), pl.BlockSpec((1, d), fix),
                  pl.BlockSpec((d, LANES), fix), pl.BlockSpec((d, LANES), fix), pl.BlockSpec((1, LANES), fix),
                  pl.BlockSpec((ROW_TILE, ROW_TILE), fix)],
        out_specs=(pl.BlockSpec((ROW_TILE, d), row), pl.BlockSpec((ROW_TILE, LANES), row),
                   pl.BlockSpec((ROUTE_COLS, ROW_TILE), lambda i: (0, i)), pl.BlockSpec((1, LANES), fix)),
        scratch_shapes=[pltpu.VMEM((1, LANES), F32)],
        compiler_params=_cparams(("arbitrary",)),
        name="moe_route",
    )(h, gain.reshape(1, d), w_hi, w_lo, b_router, ltri)


def _row_copy(src, src_row, dst, dst_row, sem):
    return pltpu.make_async_copy(src.at[pl.ds(src_row, 1)], dst.at[pl.ds(dst_row, 1)], sem)


def _dispatch_kernel(dest_ref, xn_ref, xs_ref, sem):
    rows = xn_ref.shape[0]

    def issue(r, c):
        _row_copy(xn_ref, r, xs_ref, dest_ref[0, 0, r], sem).start(priority=0)
        _row_copy(xn_ref, r, xs_ref, dest_ref[0, 0, rows + r], sem).start(priority=1)
        return c

    lax.fori_loop(0, rows, issue, 0, unroll=8)
    for _ in range(2):
        pltpu.make_async_copy(xn_ref, xs_ref.at[pl.ds(0, rows)], sem).wait()


def moe_dispatch(xn, dest3, n_slots):
    t, d = xn.shape
    return pl.pallas_call(
        _dispatch_kernel,
        out_shape=jax.ShapeDtypeStruct((n_slots, d), F32),
        grid=(t // ROW_TILE,),
        in_specs=[pl.BlockSpec((1, 1, 2 * ROW_TILE), lambda i: (i, 0, 0), memory_space=pltpu.SMEM),
                  pl.BlockSpec((ROW_TILE, d), lambda i: (i, 0))],
        out_specs=pl.BlockSpec(memory_space=pl.ANY),
        scratch_shapes=[pltpu.SemaphoreType.DMA],
        compiler_params=_cparams(("arbitrary",)),
        name="moe_dispatch",
    )(dest3, xn)


SC_WINDOW = 32
SC_INDEX_WINDOW = 128


def _sc_mesh():
    return plsc.VectorSubcoreMesh(core_axis_name="core", subcore_axis_name="subcore")


def moe_dispatch_sc(xn, dest_a, dest_b, n_slots):
    t, d = xn.shape

    @pl.kernel(out_type=jax.ShapeDtypeStruct((n_slots, d), xn.dtype), mesh=_sc_mesh(),
               scratch_types=[pltpu.VMEM((SC_WINDOW, d), xn.dtype)], name="moe_dispatch_sc")
    def run(x_hbm, id_hbm, da_hbm, db_hbm, o_hbm, buf):
        def body(id_vmem, da_vmem, db_vmem):
            for j in range(SC_INDEX_WINDOW // SC_WINDOW):
                part = pl.ds(j * SC_WINDOW, SC_WINDOW)
                pltpu.sync_copy(x_hbm.at[id_vmem.at[0, part]], buf)
                pltpu.sync_copy(buf, o_hbm.at[da_vmem.at[0, part]])
                pltpu.sync_copy(buf, o_hbm.at[db_vmem.at[0, part]])

        idx_spec = pl.BlockSpec((1, SC_INDEX_WINDOW), lambda i: (0, i))
        pltpu.emit_pipeline(
            body,
            grid=(t // SC_INDEX_WINDOW,),
            in_specs=[idx_spec, idx_spec, idx_spec],
            out_specs=[],
            core_axis_name=("core", "subcore"),
            dimension_semantics=(pltpu.PARALLEL,),
        )(id_hbm, da_hbm, db_hbm)

    return run(xn, jnp.arange(t, dtype=jnp.int32).reshape(1, t), dest_a, dest_b)


def moe_gather_sc(ys, dest_a, dest_b):
    d = ys.shape[1]
    t = dest_a.shape[1]
    out = jax.ShapeDtypeStruct((t, d), ys.dtype)

    @pl.kernel(out_type=(out, out), mesh=_sc_mesh(), scratch_types=[pltpu.VMEM((SC_WINDOW, d), ys.dtype)],
               name="moe_gather_sc")
    def run(y_hbm, id_hbm, da_hbm, db_hbm, ga_hbm, gb_hbm, buf):
        def body(id_vmem, da_vmem, db_vmem):
            for j in range(SC_INDEX_WINDOW // SC_WINDOW):
                part = pl.ds(j * SC_WINDOW, SC_WINDOW)
                pltpu.sync_copy(y_hbm.at[da_vmem.at[0, part]], buf)
                pltpu.sync_copy(buf, ga_hbm.at[id_vmem.at[0, part]])
                pltpu.sync_copy(y_hbm.at[db_vmem.at[0, part]], buf)
                pltpu.sync_copy(buf, gb_hbm.at[id_vmem.at[0, part]])

        idx_spec = pl.BlockSpec((1, SC_INDEX_WINDOW), lambda i: (0, i))
        pltpu.emit_pipeline(
            body,
            grid=(t // SC_INDEX_WINDOW,),
            in_specs=[idx_spec, idx_spec, idx_spec],
            out_specs=[],
            core_axis_name=("core", "subcore"),
            dimension_semantics=(pltpu.PARALLEL,),
        )(id_hbm, da_hbm, db_hbm)

    return run(ys, jnp.arange(t, dtype=jnp.int32).reshape(1, t), dest_a, dest_b)


def _combine_dense_kernel(h_ref, rec_ref, ga_ref, gb_ref, *out_refs, split):
    i = pl.program_id(0)
    rec = rec_ref[...]
    res = h_ref[...] + rec[:, 4:5] * ga_ref[...] + rec[:, 5:6] * gb_ref[...]
    if not split:
        out_refs[0][...] = res
    else:
        @pl.when(i < N_PROMPT_TILES)
        def _():
            out_refs[0][...] = res

        @pl.when((i >= N_PROMPT_TILES) & (i < N_PROMPT_TILES + N_SAMPLE_TILES))
        def _():
            out_refs[1][...] = res


def moe_combine_dense(h, rec, ga, gb, split=False):
    t, d = h.shape
    row = lambda i: (i, 0)
    if split:
        out_shape = (jax.ShapeDtypeStruct((T_PROMPT, d), F32), jax.ShapeDtypeStruct((T_SAMPLE, d), F32))
        out_specs = tuple(_parts_specs(d)[:2])
    else:
        out_shape = jax.ShapeDtypeStruct((t, d), F32)
        out_specs = pl.BlockSpec((ROW_TILE, d), row)
    return pl.pallas_call(
        functools.partial(_combine_dense_kernel, split=split),
        out_shape=out_shape,
        grid=(t // ROW_TILE,),
        in_specs=[pl.BlockSpec((ROW_TILE, d), row), pl.BlockSpec((ROW_TILE, LANES), row),
                  pl.BlockSpec((ROW_TILE, d), row), pl.BlockSpec((ROW_TILE, d), row)],
        out_specs=out_specs,
        compiler_params=_cparams(("arbitrary",)),
        name="moe_combine_dense",
    )(h, rec, ga, gb)


def _ffn_kernel(wblk_ref, we_ref, wlo_ref, whi_ref, xs_ref, w13_ref, w2_ref, ys_ref, w13b, w2b):
    w = pl.program_id(0)
    prev = jnp.maximum(w - 1, 0)
    first_visit = (w == 0) | (wblk_ref[w] != wblk_ref[prev])
    lo = wlo_ref[w]
    hi = whi_ref[w]

    def ffn(x):
        x = x.astype(BF16)
        cw = D_EXPERT // FFN_CHUNKS
        gate_up = []
        for c in range(FFN_CHUNKS):
            a = jnp.dot(x, w13b[:, c * cw:(c + 1) * cw], preferred_element_type=F32)
            u = jnp.dot(x, w13b[:, D_EXPERT + c * cw:D_EXPERT + (c + 1) * cw], preferred_element_type=F32)
            gate_up.append((a, u))
        hmid = jnp.concatenate([(_silu(a) * u).astype(BF16) for a, u in gate_up], axis=1)
        return jnp.dot(hmid, w2b[...], preferred_element_type=F32)

    @pl.when(hi > lo)
    def _():
        @pl.when((w == 0) | (we_ref[w] != we_ref[prev]))
        def _():
            w13b[...] = w13_ref[...].astype(BF16)
            w2b[...] = w2_ref[...].astype(BF16)

        whole = (lo == 0) & (hi == EXPERT_BLOCK)

        @pl.when(whole)
        def _():
            ys_ref[...] = ffn(xs_ref[...])

        half = EXPERT_BLOCK // 2
        for p in range(2):
            rows = slice(p * half, (p + 1) * half)
            touched = (lo < (p + 1) * half) & (hi > p * half)

            @pl.when(jnp.logical_not(whole) & touched)
            def _():
                y = ffn(xs_ref[rows, :])
                row = lax.broadcasted_iota(jnp.int32, y.shape, 0) + p * half
                mine = (row >= lo) & (row < hi)

                @pl.when(first_visit)
                def _():
                    ys_ref[rows, :] = jnp.where(mine, y, 0.0)

                @pl.when(jnp.logical_not(first_visit))
                def _():
                    ys_ref[rows, :] = jnp.where(mine, y, ys_ref[rows, :])

            @pl.when(jnp.logical_not(whole) & jnp.logical_not(touched) & first_visit)
            def _():
                ys_ref[rows, :] = jnp.zeros((half, ys_ref.shape[1]), F32)


def moe_ffn(xs, work, w13_all, w2_all, layer):
    n_slots, d = xs.shape
    n_work = work[0].shape[0]
    xmap = lambda w, wb, we, wlo, whi: (wb[w], 0)
    w_map = lambda w, wb, we, wlo, whi: (layer, we[w], 0, 0)
    grid_spec = pltpu.PrefetchScalarGridSpec(
        num_scalar_prefetch=4,
        grid=(n_work,),
        in_specs=[pl.BlockSpec((EXPERT_BLOCK, d), xmap),
                  pl.BlockSpec((None, None, d, 2 * D_EXPERT), w_map),
                  pl.BlockSpec((None, None, D_EXPERT, d), w_map)],
        out_specs=pl.BlockSpec((EXPERT_BLOCK, d), xmap),
        scratch_shapes=[pltpu.VMEM((d, 2 * D_EXPERT), BF16), pltpu.VMEM((D_EXPERT, d), BF16)],
    )
    return pl.pallas_call(
        _ffn_kernel,
        out_shape=jax.ShapeDtypeStruct((n_slots, d), F32),
        grid_spec=grid_spec,
        compiler_params=_cparams(("arbitrary",)),
        name="moe_ffn",
    )(*work, xs, w13_all, w2_all)


def _ffn_work_items(cnt):
    n_slots = 2 * T_ALL
    n_blocks = n_slots // EXPERT_BLOCK
    n_work = n_blocks + N_EXPERTS - 1
    end = jnp.cumsum(cnt)
    start = end - cnt
    first_blk = start // EXPERT_BLOCK
    last_blk = jnp.maximum(end - 1, start) // EXPERT_BLOCK
    n_items = jnp.where(cnt > 0, last_blk - first_blk + 1, 0)
    item_end = jnp.cumsum(n_items)
    item_start = item_end - n_items
    w = jnp.arange(n_work, dtype=jnp.int32)
    used = w < item_end[-1]
    wq = jnp.minimum(w, item_end[-1] - 1)
    e = jnp.sum((item_end[:, None] <= wq[None, :]).astype(jnp.int32), axis=0)
    onehot = e[None, :] == jnp.arange(N_EXPERTS, dtype=jnp.int32)[:, None]
    of_e = lambda table: jnp.sum(jnp.where(onehot, table[:, None], 0), axis=0)
    blk = jnp.where(used, of_e(first_blk) + (w - of_e(item_start)), n_blocks - 1).astype(jnp.int32)
    lo = jnp.maximum(of_e(start), blk * EXPERT_BLOCK) - blk * EXPERT_BLOCK
    hi = jnp.minimum(of_e(end), (blk + 1) * EXPERT_BLOCK) - blk * EXPERT_BLOCK
    lo = jnp.where(used, lo, 0).astype(jnp.int32)
    hi = jnp.where(used, hi, 0).astype(jnp.int32)
    return start, (blk, e, lo, hi)


def _combine_kernel(dest_ref, h_ref, rec_ref, ys_ref, *rest, split):
    out_refs, (g1, g2, sem) = rest[:-3], rest[-3:]
    rows = h_ref.shape[0]
    i = pl.program_id(0)

    def issue(r, c):
        _row_copy(ys_ref, dest_ref[0, 0, r], g1, r, sem).start(priority=0)
        _row_copy(ys_ref, dest_ref[0, 0, rows + r], g2, r, sem).start(priority=1)
        return c

    lax.fori_loop(0, rows, issue, 0, unroll=8)
    for buf in (g1, g2):
        pltpu.make_async_copy(ys_ref.at[pl.ds(0, rows)], buf, sem).wait()
    rec = rec_ref[...]
    res = h_ref[...] + rec[:, 4:5] * g1[...] + rec[:, 5:6] * g2[...]
    if not split:
        out_refs[0][...] = res
    else:
        @pl.when(i < N_PROMPT_TILES)
        def _():
            out_refs[0][...] = res

        @pl.when((i >= N_PROMPT_TILES) & (i < N_PROMPT_TILES + N_SAMPLE_TILES))
        def _():
            out_refs[1][...] = res


def moe_combine(h, rec, ys, dest3, split=False):
    t, d = h.shape
    row = lambda i: (i, 0)
    if split:
        parts = _parts_specs(d)[:2]
        out_shape = (jax.ShapeDtypeStruct((T_PROMPT, d), F32), jax.ShapeDtypeStruct((T_SAMPLE, d), F32))
        out_specs = tuple(parts)
    else:
        out_shape = jax.ShapeDtypeStruct((t, d), F32)
        out_specs = pl.BlockSpec((ROW_TILE, d), row)
    return pl.pallas_call(
        functools.partial(_combine_kernel, split=split),
        out_shape=out_shape,
        grid=(t // ROW_TILE,),
        in_specs=[pl.BlockSpec((1, 1, 2 * ROW_TILE), lambda i: (i, 0, 0), memory_space=pltpu.SMEM),
                  pl.BlockSpec((ROW_TILE, d), row), pl.BlockSpec((ROW_TILE, LANES), row),
                  pl.BlockSpec(memory_space=pl.ANY)],
        out_specs=out_specs,
        scratch_shapes=[pltpu.VMEM((ROW_TILE, d), F32), pltpu.VMEM((ROW_TILE, d), F32),
                        pltpu.SemaphoreType.DMA],
        compiler_params=_cparams(("arbitrary",)),
        name="moe_combine",
    )(dest3, h, rec, ys)


def hier_moe_layer(h, layer, gain, w_group, b_group, w_expert, b_expert, w13_all, w2_all, ltri, split_out=False):
    t = h.shape[0]
    pad = LANES - N_EXPERTS - N_GROUPS
    w_router = jnp.concatenate([w_expert, w_group, jnp.zeros((D_MODEL, pad), F32)], axis=1)
    b_router = jnp.concatenate([b_expert, b_group, jnp.zeros((pad,), F32)]).reshape(1, LANES)
    xn, rec, rect, counts = moe_route(h, gain, w_router, b_router, ltri)

    cnt = counts[0, :N_EXPERTS].astype(jnp.int32)
    start, work = _ffn_work_items(cnt)
    experts = jnp.arange(N_EXPERTS, dtype=jnp.int32)[:, None]

    def slot_of(e_row, rank_row):
        first = jnp.sum(jnp.where(e_row.astype(jnp.int32)[None, :] == experts, start[:, None], 0), axis=0)
        return (first + rank_row.astype(jnp.int32)).reshape(1, t)

    dest_a = slot_of(rect[0], rect[2])
    dest_b = slot_of(rect[1], rect[3])

    xs = moe_dispatch_sc(xn, dest_a, dest_b, 2 * t)
    ys = moe_ffn(xs, work, w13_all, w2_all, layer)
    ga, gb = moe_gather_sc(ys, dest_a, dest_b)
    return moe_combine_dense(h, rec, ga, gb, split=split_out)


def _rope_tables(pos):
    half = ROPE_DIM // 2
    inv = jnp.exp(-math.log(ROPE_THETA) * jnp.arange(half, dtype=F32) * (2.0 / ROPE_DIM))
    ang = pos.astype(F32)[:, None] * inv[None, :]
    cos, sin = jnp.cos(ang), jnp.sin(ang)
    t = pos.shape[0]
    ones = jnp.ones((t, HEAD_DIM - ROPE_DIM), F32)
    zeros = jnp.zeros((t, HEAD_DIM - ROPE_DIM), F32)
    z8 = jnp.zeros((t, half), F32)
    cos_h = jnp.concatenate([cos, cos, ones], axis=1)
    sina_h = jnp.concatenate([-sin, z8, zeros], axis=1)
    sinb_h = jnp.concatenate([z8, sin, zeros], axis=1)
    two = lambda a: jnp.concatenate([a, a], axis=1)
    return two(cos_h), two(sina_h), two(sinb_h)


def kernel(x_prompt, x_sample, state_hgrn, cache_k_win, cache_v_win, meta_tokens, a_norm, a_w_in, a_lower_logits, a_out_norm, a_w_out, kv_norm, kv_w, k_norm, b_norm, b_wq, b_q_norm, b_sinks, b_w_out, moe_norm, moe_w_group, moe_b_group, moe_w_expert, moe_b_expert, moe_w13, moe_w2):
    tail_rows = T_ALL - OFF_META
    x_parts = (x_prompt.reshape(T_PROMPT, D_MODEL), x_sample.reshape(T_SAMPLE, D_MODEL),
               jnp.concatenate([meta_tokens.astype(F32), jnp.zeros((tail_rows - N_META, D_MODEL), F32)], axis=0))
    pos = jnp.concatenate([N_META + jnp.arange(SEQ, dtype=jnp.int32),
                           jnp.tile(PAST_LEN + jnp.arange(DEC_SEQ, dtype=jnp.int32), ROW_TILE // DEC_SEQ),
                           jnp.arange(N_META, dtype=jnp.int32),
                           jnp.zeros((ROW_TILE - N_META,), jnp.int32)])
    cos_t, sina_t, sinb_t = _rope_tables(pos)
    r256 = np.arange(256)
    hmean = jnp.asarray((r256[:, None] // HEAD_DIM == r256[None, :] // HEAD_DIM).astype(np.float32) / HEAD_DIM, BF16)
    ltri = jnp.asarray((r256[None, :] < r256[:, None]).astype(np.float32), BF16)
    lower = jnp.cumsum(jax.nn.softmax(a_lower_logits.astype(F32), axis=0), axis=0)

    moe = functools.partial(hier_moe_layer, w13_all=moe_w13, w2_all=moe_w2, ltri=ltri)

    z = in_project(x_parts, a_norm[0], a_w_in[0].astype(BF16))
    zero_state = jnp.zeros((1, A_HEADS, A_DK, A_DV), F32)
    o_meta, s_meta = hgrn2_scan(z, zero_state, lower[0], a_out_norm[0],
                                row_off=OFF_META, n_seq=1, seq_len=N_META)
    o_prompt, s_prompt = hgrn2_scan(z, s_meta, lower[0], a_out_norm[0], row_off=0, n_seq=BATCH, seq_len=SEQ)
    o_sample, s_sample = hgrn2_scan(z, state_hgrn[0].astype(F32), lower[0], a_out_norm[0],
                                    row_off=OFF_SAMPLE, n_seq=DEC_BATCH, seq_len=DEC_SEQ, group=SCAN_SAMPLE_GROUP)
    o_tail = jnp.concatenate([o_meta, jnp.zeros((tail_rows - N_META, D_MODEL), BF16)], axis=0)
    h = mixer_out((o_prompt, o_sample, o_tail), a_w_out[0].astype(BF16), x_parts)
    h = moe(h, 0, moe_norm[0], moe_w_group[0], moe_b_group[0], moe_w_expert[0], moe_b_expert[0])

    k_all, v_all = kv_project(h, kv_norm, kv_w.astype(BF16), hmean, jnp.tile(k_norm, KV_HEADS).reshape(1, KV_DIM),
                              cos_t, sina_t, sinb_t)

    q_all = q_project(h, b_norm[0], b_wq[0].astype(BF16), hmean, jnp.tile(b_q_norm[0], Q_HEADS).reshape(1, D_MODEL),
                      cos_t, sina_t, sinb_t)
    meta_blk = lambda a: jnp.concatenate([jnp.zeros((ATT_BLOCK - N_META, KV_DIM), F32),
                                          a[OFF_META:OFF_META + N_META]], axis=0)
    sinks = b_sinks[0].astype(F32)
    att_all = attention_prompt(q_all, k_all, v_all, meta_blk(k_all), meta_blk(v_all), sinks)
    att_all = attention_sample(q_all, cache_k_win.reshape(DEC_BATCH * WINDOW, KV_DIM).astype(F32),
                               cache_v_win.reshape(DEC_BATCH * WINDOW, KV_DIM).astype(F32),
                               k_all, v_all, sinks, att_all)
    h = matmul_residual(att_all, b_w_out[0].astype(BF16), h)
    y_p, y_s = moe(h, 1, moe_norm[1], moe_w_group[1], moe_b_group[1], moe_w_expert[1], moe_b_expert[1],
                   split_out=True)

    y_prompt = y_p.reshape(BATCH, SEQ, D_MODEL)
    y_sample = y_s.reshape(DEC_BATCH, DEC_SEQ, D_MODEL)
    last = lambda a: jnp.stack([a[(b + 1) * SEQ - WINDOW:(b + 1) * SEQ] for b in range(BATCH)]).reshape(
        BATCH, WINDOW, KV_HEADS, HEAD_DIM)
    kp = last(k_all)
    vp = last(v_all)
    ks = k_all[OFF_SAMPLE:OFF_SAMPLE + T_SAMPLE].reshape(DEC_BATCH, DEC_SEQ, KV_HEADS, HEAD_DIM)
    vs = v_all[OFF_SAMPLE:OFF_SAMPLE + T_SAMPLE].reshape(DEC_BATCH, DEC_SEQ, KV_HEADS, HEAD_DIM)
    k_win_s = jnp.concatenate([cache_k_win, ks], axis=1)[:, -WINDOW:]
    v_win_s = jnp.concatenate([cache_v_win, vs], axis=1)[:, -WINDOW:]
    return (y_prompt, y_sample, s_prompt[None], s_sample[None], kp, vp, k_win_s, v_win_s)
```

```python
import functools
import math

import numpy as np
import jax
import jax.numpy as jnp
from jax import lax
from jax.experimental import pallas as pl
from jax.experimental.pallas import tpu as pltpu
from jax.experimental.pallas import tpu_sc as plsc

F32 = jnp.float32
BF16 = jnp.bfloat16
U32 = jnp.uint32

D_MODEL = 1024
BATCH = 4
SEQ = 4096
DEC_BATCH = 128
DEC_SEQ = 8
PAST_LEN = 8192
N_META = 16
A_HEADS = 8
A_DK = 128
A_DV = 128
Q_HEADS = 16
KV_HEADS = 4
HEAD_DIM = 64
KV_DIM = KV_HEADS * HEAD_DIM
WINDOW = 128
ROPE_DIM = 16
ROPE_THETA = 500000.0
N_GROUPS = 4
EXPERTS_PER_GROUP = 8
N_EXPERTS = 32
D_EXPERT = 512
RMS_EPS = 1e-6

LANES = 128
SUBLANES = 8
VMEM_LIMIT = 56 * 1024 * 1024

ROW_TILE = 256
T_PROMPT = BATCH * SEQ
T_SAMPLE = DEC_BATCH * DEC_SEQ
OFF_SAMPLE = T_PROMPT
OFF_META = T_PROMPT + T_SAMPLE
T_REAL = OFF_META + N_META
T_ALL = -(-T_REAL // ROW_TILE) * ROW_TILE
N_TILES = T_ALL // ROW_TILE

SCAN_CHUNK = 128
SCAN_SAMPLE_GROUP = 8
ATT_BLOCK = 128
EXPERT_BLOCK = 512
FFN_CHUNKS = 2


def _cparams(sem):
    return pltpu.CompilerParams(dimension_semantics=sem, vmem_limit_bytes=VMEM_LIMIT)


def _nt_dot(a, b):
    return lax.dot_general(a, b, (((1,), (1,)), ((), ())), preferred_element_type=F32)


def _rms(x, gain):
    ms = jnp.mean(x * x, axis=-1, keepdims=True)
    return x * lax.rsqrt(ms + RMS_EPS) * gain


def _silu(x):
    return x * jax.nn.sigmoid(x)


def _pack_halves(x):
    w = x.shape[1] // 2
    hi = lax.bitcast_convert_type(x[:, :w].astype(BF16).astype(F32), U32)
    lo = lax.bitcast_convert_type(x[:, w:].astype(BF16).astype(F32), U32)
    return hi | (lo >> 16)


def _unpack_halves(p):
    hi = lax.bitcast_convert_type(p & jnp.uint32(0xFFFF0000), F32)
    lo = lax.bitcast_convert_type(p << 16, F32)
    return jnp.concatenate([hi, lo], axis=1)


N_PROMPT_TILES = T_PROMPT // ROW_TILE
N_SAMPLE_TILES = T_SAMPLE // ROW_TILE


def _parts_specs(width):
    return [pl.BlockSpec((ROW_TILE, width), lambda i: (jnp.minimum(i, N_PROMPT_TILES - 1), 0)),
            pl.BlockSpec((ROW_TILE, width), lambda i: (jnp.clip(i - N_PROMPT_TILES, 0, N_SAMPLE_TILES - 1), 0)),
            pl.BlockSpec((ROW_TILE, width), lambda i: (0, 0))]


def _pick_part(i, p_ref, s_ref, t_ref, dtype):
    return jnp.where(i < N_PROMPT_TILES, p_ref[...].astype(dtype),
                     jnp.where(i < N_PROMPT_TILES + N_SAMPLE_TILES, s_ref[...].astype(dtype),
                               t_ref[...].astype(dtype)))


def _in_proj_kernel(xp_ref, xs_ref, xt_ref, g_ref, w_ref, o_ref):
    x = _pick_part(pl.program_id(0), xp_ref, xs_ref, xt_ref, F32)
    xn = _rms(x, g_ref[...])
    o_ref[...] = jnp.dot(xn.astype(BF16), w_ref[...], preferred_element_type=F32)


def in_project(x_parts, gain, w_bf16):
    d, n = w_bf16.shape
    return pl.pallas_call(
        _in_proj_kernel,
        out_shape=jax.ShapeDtypeStruct((T_ALL, n), F32),
        grid=(N_TILES,),
        in_specs=_parts_specs(d) + [pl.BlockSpec((1, d), lambda i: (0, 0)),
                                    pl.BlockSpec((d, n), lambda i: (0, 0))],
        out_specs=pl.BlockSpec((ROW_TILE, n), lambda i: (i, 0)),
        compiler_params=_cparams(("parallel",)),
        name="in_project",
    )(*x_parts, gain.reshape(1, d), w_bf16)


def _mixer_out_kernel(ap_ref, as_ref, at_ref, w_ref, xp_ref, xs_ref, xt_ref, o_ref):
    i = pl.program_id(0)
    a = _pick_part(i, ap_ref, as_ref, at_ref, BF16)
    x = _pick_part(i, xp_ref, xs_ref, xt_ref, F32)
    o_ref[...] = x + jnp.dot(a, w_ref[...], preferred_element_type=F32)


def mixer_out(a_parts, w_bf16, x_parts):
    k, n = w_bf16.shape
    return pl.pallas_call(
        _mixer_out_kernel,
        out_shape=jax.ShapeDtypeStruct((T_ALL, n), F32),
        grid=(N_TILES,),
        in_specs=_parts_specs(k) + [pl.BlockSpec((k, n), lambda i: (0, 0))] + _parts_specs(n),
        out_specs=pl.BlockSpec((ROW_TILE, n), lambda i: (i, 0)),
        compiler_params=_cparams(("parallel",)),
        name="mixer_out",
    )(*a_parts, w_bf16, *x_parts)


def _head_norm_rope(y, hmean_ref, hgain, cos_t, sina_t, sinb_t):
    rows, width = y.shape
    sq = (y * y).astype(BF16)
    parts = []
    for s in range(width // 256):
        parts.append(jnp.dot(sq[:, s * 256:(s + 1) * 256], hmean_ref[...], preferred_element_type=F32))
    ms = parts[0] if len(parts) == 1 else jnp.concatenate(parts, axis=1)
    yn = y * lax.rsqrt(ms + RMS_EPS) * hgain
    reps = width // LANES
    cos_w = jnp.concatenate([cos_t] * reps, axis=1)
    sina_w = jnp.concatenate([sina_t] * reps, axis=1)
    sinb_w = jnp.concatenate([sinb_t] * reps, axis=1)
    half = ROPE_DIM // 2
    nxt = pltpu.roll(yn, width - half, 1)
    prv = pltpu.roll(yn, half, 1)
    return yn * cos_w + nxt * sina_w + prv * sinb_w


def _kv_kernel(x_ref, g_ref, w_ref, hmean_ref, hg_ref, cos_ref, sina_ref, sinb_ref, k_ref, v_ref):
    xn = _rms(x_ref[...], g_ref[...])
    z = jnp.dot(xn.astype(BF16), w_ref[...], preferred_element_type=F32)
    k = _head_norm_rope(z[:, :KV_DIM], hmean_ref, hg_ref[...], cos_ref[...], sina_ref[...], sinb_ref[...])
    k_ref[...] = k
    v_ref[...] = z[:, KV_DIM:]


def _rope_tile(i):
    tiles_per_seq = SEQ // ROW_TILE
    n_prompt_tiles = T_PROMPT // ROW_TILE
    n_sample_tiles = T_SAMPLE // ROW_TILE
    return (jnp.where(i < n_prompt_tiles, i % tiles_per_seq,
                      jnp.where(i < n_prompt_tiles + n_sample_tiles, tiles_per_seq, tiles_per_seq + 1)), 0)


def kv_project(x, gain, w_bf16, hmean, hgain_w, cos_t, sina_t, sinb_t):
    t, d = x.shape
    row = lambda i: (i, 0)
    fix = lambda i: (0, 0)
    return pl.pallas_call(
        _kv_kernel,
        out_shape=(jax.ShapeDtypeStruct((t, KV_DIM), F32), jax.ShapeDtypeStruct((t, KV_DIM), F32)),
        grid=(t // ROW_TILE,),
        in_specs=[pl.BlockSpec((ROW_TILE, d), row), pl.BlockSpec((1, d), fix),
                  pl.BlockSpec((d, 2 * KV_DIM), fix), pl.BlockSpec((256, 256), fix),
                  pl.BlockSpec((1, KV_DIM), fix),
                  pl.BlockSpec((ROW_TILE, LANES), _rope_tile), pl.BlockSpec((ROW_TILE, LANES), _rope_tile),
                  pl.BlockSpec((ROW_TILE, LANES), _rope_tile)],
        out_specs=(pl.BlockSpec((ROW_TILE, KV_DIM), row), pl.BlockSpec((ROW_TILE, KV_DIM), row)),
        compiler_params=_cparams(("parallel",)),
        name="kv_project",
    )(x, gain.reshape(1, d), w_bf16, hmean, hgain_w, cos_t, sina_t, sinb_t)


def _q_kernel(x_ref, g_ref, w_ref, hmean_ref, hg_ref, cos_ref, sina_ref, sinb_ref, q_ref):
    xn = _rms(x_ref[...], g_ref[...])
    z = jnp.dot(xn.astype(BF16), w_ref[...], preferred_element_type=F32)
    q = _head_norm_rope(z, hmean_ref, hg_ref[...], cos_ref[...], sina_ref[...], sinb_ref[...])
    q_ref[...] = (q * HEAD_DIM ** -0.5).astype(q_ref.dtype)


def q_project(x, gain, w_bf16, hmean, hgain_w, cos_t, sina_t, sinb_t):
    t, d = x.shape
    row = lambda i: (i, 0)
    fix = lambda i: (0, 0)
    return pl.pallas_call(
        _q_kernel,
        out_shape=jax.ShapeDtypeStruct((t, d), BF16),
        grid=(t // ROW_TILE,),
        in_specs=[pl.BlockSpec((ROW_TILE, d), row), pl.BlockSpec((1, d), fix),
                  pl.BlockSpec((d, d), fix), pl.BlockSpec((256, 256), fix),
                  pl.BlockSpec((1, d), fix),
                  pl.BlockSpec((ROW_TILE, LANES), _rope_tile), pl.BlockSpec((ROW_TILE, LANES), _rope_tile),
                  pl.BlockSpec((ROW_TILE, LANES), _rope_tile)],
        out_specs=pl.BlockSpec((ROW_TILE, d), row),
        compiler_params=_cparams(("parallel",)),
        name="q_project",
    )(x, gain.reshape(1, d), w_bf16, hmean, hgain_w, cos_t, sina_t, sinb_t)


def _matmul_residual_kernel(a_ref, w_ref, r_ref, o_ref):
    o_ref[...] = r_ref[...] + jnp.dot(a_ref[...], w_ref[...], preferred_element_type=F32)


def matmul_residual(a_bf16, w_bf16, resid):
    t, k = a_bf16.shape
    n = w_bf16.shape[1]
    return pl.pallas_call(
        _matmul_residual_kernel,
        out_shape=jax.ShapeDtypeStruct((t, n), F32),
        grid=(t // ROW_TILE,),
        in_specs=[pl.BlockSpec((ROW_TILE, k), lambda i: (i, 0)),
                  pl.BlockSpec((k, n), lambda i: (0, 0)),
                  pl.BlockSpec((ROW_TILE, n), lambda i: (i, 0))],
        out_specs=pl.BlockSpec((ROW_TILE, n), lambda i: (i, 0)),
        compiler_params=_cparams(("parallel",)),
        name="matmul_residual",
    )(a_bf16, w_bf16, resid)


def _scan_levels(c):
    levels = []
    m = c
    while m >= 2:
        levels.append(m)
        m //= 2
    return levels


LOG2E = 1.4426950408889634


def _scan_kernel(z_ref, s0_ref, lb_ref, og_ref, tri_ref, lmask_ref, sgn_ref, o_ref, sfin_ref, s_scr, b_scr,
                 *, rows, seq_len):
    c_idx = pl.program_id(1)
    levels = _scan_levels(seq_len)
    n_sub = rows // seq_len
    hk = A_HEADS * A_DK

    @pl.when(c_idx == 0)
    def _():
        s_scr[...] = s0_ref[...]

    sub = lax.broadcasted_iota(jnp.int32, (SUBLANES, LANES), 0)
    row = lax.broadcasted_iota(jnp.int32, (LANES, LANES), 0)
    og = og_ref[...]

    def pad_f32(x):
        if x.shape[0] == LANES:
            return x
        return jnp.concatenate([x, jnp.zeros((LANES - x.shape[0], x.shape[1]), x.dtype)], axis=0)

    def pad_rows(x):
        return pad_f32(x).astype(BF16)

    def cols(part, h):
        return slice(part * hk + h * LANES, part * hk + (h + 1) * LANES)

    def gates(h):
        lb = lb_ref[:, cols(0, h)]
        forget = lb + (1.0 - lb) * jax.nn.sigmoid(z_ref[:, cols(1, h)])
        logf = jnp.log(forget)
        hi = logf.astype(BF16).astype(F32)
        r1 = logf - hi
        mid = r1.astype(BF16).astype(F32)
        lo = r1 - mid
        cs = jnp.dot(tri_ref[...], pad_rows(jnp.concatenate([hi, mid, lo], axis=1)),
                     preferred_element_type=F32)
        b = (cs[:rows, :LANES] + cs[:rows, LANES:2 * LANES]) + cs[:rows, 2 * LANES:]
        b_scr[h] = b
        return _silu(z_ref[:, cols(0, h)]), 1.0 - forget, b

    def bref_for(h, m):
        b_rows = b_scr.at[h]
        half = m // 2
        pieces = []
        for g in range(rows // SUBLANES):
            base = g * SUBLANES
            if m >= SUBLANES:
                r = (base // m) * m + half - 1
                piece = jnp.broadcast_to(b_rows[r:r + 1, :], (SUBLANES, LANES))
            else:
                piece = jnp.broadcast_to(b_rows[base + half - 1:base + half, :], (SUBLANES, LANES))
                for blk in range(1, SUBLANES // m):
                    r = base + blk * m + half - 1
                    piece = jnp.where(sub >= blk * m,
                                      jnp.broadcast_to(b_rows[r:r + 1, :], (SUBLANES, LANES)), piece)
            pieces.append(piece)
        return pieces[0] if len(pieces) == 1 else jnp.concatenate(pieces, axis=0)

    heads = range(A_HEADS)
    qkb = [gates(h) for h in heads]
    att = [_nt_dot(pad_rows(qf), pad_rows(kf)) * lmask_ref[len(levels)] for qf, kf, _ in qkb]
    for li, m in enumerate(levels):
        for h in heads:
            qf, kf, b = qkb[h]
            sgn = sgn_ref[li]
            e = jnp.exp2((b - bref_for(h, m)) * sgn)
            w = pad_rows(jnp.where(sgn > 0, qf, kf) * e)
            att[h] = att[h] + _nt_dot(w, w) * lmask_ref[li]

    def finish(h):
        qf, kf, b = qkb[h]
        b_rows = b_scr.at[h]
        v_b = pad_rows(z_ref[:, cols(2, h)])
        o_intra = jnp.dot(att[h].astype(BF16), v_b, preferred_element_type=F32)
        eb = jnp.exp(b)
        qs = qf * eb
        b_end = [jnp.broadcast_to(b_rows[(i + 1) * seq_len - 1:(i + 1) * seq_len, :], (seq_len, LANES))
                 for i in range(n_sub)]
        b_end = b_end[0] if n_sub == 1 else jnp.concatenate(b_end, axis=0)
        kd_t = pad_f32(kf * jnp.exp(b_end - b)).T.astype(BF16)
        eb_t = pad_f32(eb).T
        qs_b = pad_rows(qs)
        o = o_intra
        for i in range(n_sub):
            s_old = s_scr[i, h]
            first, last = i * seq_len, (i + 1) * seq_len - 1
            if n_sub == 1:
                qs_i, v_i = qs_b, v_b
            else:
                mine = (row >= first) & (row <= last)
                qs_i = jnp.where(mine, qs_b, jnp.zeros_like(qs_b))
                v_i = jnp.where(mine, v_b, jnp.zeros_like(v_b))
            o = o + jnp.dot(qs_i, s_old.astype(BF16), preferred_element_type=F32)
            decay = jnp.broadcast_to(eb_t[:, last:last + 1], (LANES, LANES))
            s_scr[i, h] = decay * s_old + jnp.dot(kd_t, v_i, preferred_element_type=F32)
        o = o[:rows]

        on = _rms(o, og) * _silu(z_ref[:, cols(3, h)])
        o_ref[:, cols(0, h)] = on.astype(o_ref.dtype)

    for h in heads:
        finish(h)

    @pl.when(c_idx == pl.num_programs(1) - 1)
    def _():
        sfin_ref[...] = s_scr[...]


def _scan_consts(rows, seq_len):
    levels = _scan_levels(seq_len)
    r = np.arange(LANES)
    t, s = r[:, None], r[None, :]
    live = (t < rows) & (s < rows)
    tri = ((s <= t) & (t // seq_len == s // seq_len) & live).astype(np.float32)
    masks, sgns = [], []
    for m in levels:
        masks.append(((t // m == s // m) & (t % m >= m // 2) & (s % m < m // 2) & live).astype(np.float32))
        sgns.append(np.broadcast_to(np.where(r[:rows, None] % m >= m // 2, LOG2E, -LOG2E), (rows, LANES)))
    masks.append(((t == s) & live).astype(np.float32))
    return jnp.asarray(tri, BF16), jnp.asarray(np.stack(masks), F32), jnp.asarray(np.stack(sgns), F32)


def hgrn2_scan(z, s0, lb, o_gain, *, row_off, n_seq, seq_len, group=1):
    hv = A_HEADS * A_DV
    if seq_len > SCAN_CHUNK:
        assert group == 1
        sub_len, rows, n_chunks, n_steps = SCAN_CHUNK, SCAN_CHUNK, seq_len // SCAN_CHUNK, n_seq
    else:
        sub_len, rows, n_chunks, n_steps = seq_len, group * seq_len, 1, n_seq // group
    blk_off = row_off // rows
    tri, lmask, sgn = _scan_consts(rows, sub_len)
    shared_s0 = s0.shape[0] == 1
    fix2 = lambda s, c: (0, 0)
    fix3 = lambda s, c: (0, 0, 0)
    o, sfin = pl.pallas_call(
        functools.partial(_scan_kernel, rows=rows, seq_len=sub_len),
        out_shape=(jax.ShapeDtypeStruct((n_seq * seq_len, hv), BF16 if rows % 16 == 0 else F32),
                   jax.ShapeDtypeStruct((n_seq, A_HEADS, A_DK, A_DV), F32)),
        grid=(n_steps, n_chunks),
        in_specs=[pl.BlockSpec((rows, 4 * hv), lambda s, c: (blk_off + s * n_chunks + c, 0)),
                  pl.BlockSpec((group, A_HEADS, A_DK, A_DV), (lambda s, c: (0, 0, 0, 0)) if shared_s0
                               else (lambda s, c: (s, 0, 0, 0))),
                  pl.BlockSpec((1, hv), fix2), pl.BlockSpec((1, A_DV), fix2),
                  pl.BlockSpec((LANES, LANES), fix2), pl.BlockSpec(lmask.shape, fix3),
                  pl.BlockSpec(sgn.shape, fix3)],
        out_specs=(pl.BlockSpec((rows, hv), lambda s, c: (s * n_chunks + c, 0)),
                   pl.BlockSpec((group, A_HEADS, A_DK, A_DV), lambda s, c: (s, 0, 0, 0))),
        scratch_shapes=[pltpu.VMEM((group, A_HEADS, A_DK, A_DV), F32), pltpu.VMEM((A_HEADS, rows, LANES), F32)],
        compiler_params=_cparams(("parallel", "arbitrary")),
        name=f"hgrn2_scan_r{rows}",
    )(z, s0, lb.reshape(1, hv), o_gain.reshape(1, A_DV), tri, lmask, sgn)
    return o, sfin


KEYS = 2 * ATT_BLOCK


def _pair_operand(x, kh):
    slab = x[:, (kh // 2) * LANES:(kh // 2 + 1) * LANES]
    lane = lax.broadcasted_iota(jnp.int32, slab.shape, 1)
    if kh % 2 == 0:
        lo = jnp.where(lane < HEAD_DIM, slab, 0.0)
        hi = pltpu.roll(lo, HEAD_DIM, 1)
    else:
        hi = jnp.where(lane >= HEAD_DIM, slab, 0.0)
        lo = pltpu.roll(hi, HEAD_DIM, 1)
    return jnp.concatenate([lo, hi], axis=0).astype(BF16)


def _window_bias(rows, jmin):
    t_i = lax.broadcasted_iota(jnp.int32, (rows, 2 * KEYS), 0)
    c_i = lax.broadcasted_iota(jnp.int32, (rows, 2 * KEYS), 1)
    j_i = c_i & (ATT_BLOCK - 1)
    own = (c_i & ATT_BLOCK) != 0
    ok = (own & (j_i <= t_i)) | (jnp.logical_not(own) & (j_i >= t_i) & (j_i >= jmin))
    return jnp.where(ok, 0.0, -jnp.inf).astype(F32)


def _pair_softmax(s, sink_a, sink_b):
    probs, rinv = [], []
    for hh, sink in enumerate((sink_a, sink_b)):
        sh = s[:, hh * KEYS:(hh + 1) * KEYS]
        m = jnp.maximum(jnp.max(sh, axis=-1, keepdims=True), sink)
        p = jnp.exp(sh - m)
        den = jnp.sum(p, axis=-1, keepdims=True) + jnp.exp(sink - m)
        probs.append(p.astype(BF16))
        rinv.append(1.0 / den)
    lane = lax.broadcasted_iota(jnp.int32, (s.shape[0], LANES), 1)
    return jnp.concatenate(probs, axis=1), jnp.where(lane < HEAD_DIM, rinv[0], rinv[1])


def _attn_prompt_kernel(sink_ref, q_ref, kp_ref, ko_ref, vp_ref, vo_ref, km_ref, vm_ref, o_ref,
                        k2_scr, v2_scr, s_scr, p_scr, r_scr):
    n = pl.program_id(0)
    nbp = SEQ // ATT_BLOCK
    n_pairs = Q_HEADS // 2

    @pl.when(n >= BATCH * nbp)
    def _():
        o_ref[...] = jnp.zeros_like(o_ref)

    @pl.when(n < BATCH * nbp)
    def _():
        first = (n % nbp) == 0
        jmin = jnp.where(first, ATT_BLOCK - N_META, 0)
        k = jnp.concatenate([jnp.where(first, km_ref[...], kp_ref[...]), ko_ref[...]], axis=0)
        v = jnp.concatenate([jnp.where(first, vm_ref[...], vp_ref[...]), vo_ref[...]], axis=0)
        bias = _window_bias(ATT_BLOCK, jmin)
        for kh in range(KV_HEADS):
            k2_scr[kh] = _pair_operand(k, kh)
            v2_scr[kh] = _pair_operand(v, kh)
        for pair in range(n_pairs):
            s_scr[pair] = _nt_dot(q_ref[:, pair * LANES:(pair + 1) * LANES], k2_scr[pair // 2]) + bias
        for pair in range(n_pairs):
            p, rinv = _pair_softmax(s_scr[pair], sink_ref[2 * pair], sink_ref[2 * pair + 1])
            p_scr[pair] = p
            r_scr[pair] = rinv
        for pair in range(n_pairs):
            o = jnp.dot(p_scr[pair], v2_scr[pair // 2], preferred_element_type=F32) * r_scr[pair]
            o_ref[:, pair * LANES:(pair + 1) * LANES] = o.astype(o_ref.dtype)


def attention_prompt(q_all, k_all, v_all, k_meta_blk, v_meta_blk, sinks):
    n_prompt_blocks = T_PROMPT // ATT_BLOCK
    n_blocks = T_ALL // ATT_BLOCK
    n_pairs = Q_HEADS // 2
    own = lambda n, sk: (jnp.minimum(n, n_prompt_blocks - 1), 0)
    prev = lambda n, sk: (jnp.maximum(jnp.minimum(n, n_prompt_blocks - 1) - 1, 0), 0)
    fix = lambda n, sk: (0, 0)
    grid_spec = pltpu.PrefetchScalarGridSpec(
        num_scalar_prefetch=1,
        grid=(n_blocks,),
        in_specs=[pl.BlockSpec((ATT_BLOCK, D_MODEL), own),
                  pl.BlockSpec((ATT_BLOCK, KV_DIM), prev), pl.BlockSpec((ATT_BLOCK, KV_DIM), own),
                  pl.BlockSpec((ATT_BLOCK, KV_DIM), prev), pl.BlockSpec((ATT_BLOCK, KV_DIM), own),
                  pl.BlockSpec((ATT_BLOCK, KV_DIM), fix), pl.BlockSpec((ATT_BLOCK, KV_DIM), fix)],
        out_specs=pl.BlockSpec((ATT_BLOCK, D_MODEL), lambda n, sk: (n, 0)),
        scratch_shapes=[pltpu.VMEM((KV_HEADS, 2 * KEYS, LANES), BF16), pltpu.VMEM((KV_HEADS, 2 * KEYS, LANES), BF16),
                        pltpu.VMEM((n_pairs, ATT_BLOCK, 2 * KEYS), F32),
                        pltpu.VMEM((n_pairs, ATT_BLOCK, 2 * KEYS), BF16),
                        pltpu.VMEM((n_pairs, ATT_BLOCK, LANES), F32)],
    )
    return pl.pallas_call(
        _attn_prompt_kernel,
        out_shape=jax.ShapeDtypeStruct((T_ALL, D_MODEL), BF16),
        grid_spec=grid_spec,
        compiler_params=_cparams(("parallel",)),
        name="attention_prompt",
    )(sinks, q_all, k_all, k_all, v_all, v_all, k_meta_blk, v_meta_blk)


SAMPLE_GROUP = ATT_BLOCK // DEC_SEQ


def _attn_sample_kernel(sink_ref, q_ref, ck_ref, cv_ref, kn_ref, vn_ref, buf_ref, o_ref,
                        qf_scr, of_scr, k2_scr, v2_scr):
    del buf_ref
    qrows = 2 * DEC_SEQ
    qf_scr[...] = q_ref[...].astype(F32)
    bias = _window_bias(qrows, 0)
    zq = jnp.zeros((qrows - DEC_SEQ, D_MODEL), F32)
    zk = jnp.zeros((ATT_BLOCK - DEC_SEQ, KV_DIM), F32)

    def seq_body(i, carry):
        r_new = pl.multiple_of(i * DEC_SEQ, DEC_SEQ)
        r_old = pl.multiple_of(i * WINDOW, WINDOW)
        q = jnp.concatenate([qf_scr[pl.ds(r_new, DEC_SEQ), :], zq], axis=0).astype(BF16)
        k = jnp.concatenate([ck_ref[pl.ds(r_old, WINDOW), :], kn_ref[pl.ds(r_new, DEC_SEQ), :], zk], axis=0)
        v = jnp.concatenate([cv_ref[pl.ds(r_old, WINDOW), :], vn_ref[pl.ds(r_new, DEC_SEQ), :], zk], axis=0)
        for kh in range(KV_HEADS):
            k2_scr[kh] = _pair_operand(k, kh)
            v2_scr[kh] = _pair_operand(v, kh)
        scores = [_nt_dot(q[:, pair * LANES:(pair + 1) * LANES], k2_scr[pair // 2]) + bias
                  for pair in range(Q_HEADS // 2)]
        soft = [_pair_softmax(s, sink_ref[2 * pair], sink_ref[2 * pair + 1]) for pair, s in enumerate(scores)]
        for pair, (p, rinv) in enumerate(soft):
            o = jnp.dot(p, v2_scr[pair // 2], preferred_element_type=F32) * rinv
            of_scr[pl.ds(r_new, DEC_SEQ), pair * LANES:(pair + 1) * LANES] = o[:DEC_SEQ]
        return carry

    lax.fori_loop(0, SAMPLE_GROUP, seq_body, 0)
    o_ref[...] = of_scr[...].astype(o_ref.dtype)


def attention_sample(q_all, cache_k, cache_v, k_all, v_all, sinks, out_buf):
    first_blk = OFF_SAMPLE // ATT_BLOCK
    new = lambda g, sk: (first_blk + g, 0)
    old = lambda g, sk: (g, 0)
    grid_spec = pltpu.PrefetchScalarGridSpec(
        num_scalar_prefetch=1,
        grid=(DEC_BATCH // SAMPLE_GROUP,),
        in_specs=[pl.BlockSpec((ATT_BLOCK, D_MODEL), new),
                  pl.BlockSpec((SAMPLE_GROUP * WINDOW, KV_DIM), old),
                  pl.BlockSpec((SAMPLE_GROUP * WINDOW, KV_DIM), old),
                  pl.BlockSpec((ATT_BLOCK, KV_DIM), new), pl.BlockSpec((ATT_BLOCK, KV_DIM), new),
                  pl.BlockSpec(memory_space=pl.ANY)],
        out_specs=pl.BlockSpec((ATT_BLOCK, D_MODEL), new),
        scratch_shapes=[pltpu.VMEM((ATT_BLOCK, D_MODEL), F32), pltpu.VMEM((ATT_BLOCK, D_MODEL), F32),
                        pltpu.VMEM((KV_HEADS, 2 * KEYS, LANES), BF16), pltpu.VMEM((KV_HEADS, 2 * KEYS, LANES), BF16)],
    )
    return pl.pallas_call(
        _attn_sample_kernel,
        out_shape=jax.ShapeDtypeStruct(out_buf.shape, out_buf.dtype),
        grid_spec=grid_spec,
        input_output_aliases={6: 0},
        compiler_params=_cparams(("parallel",)),
        name="attention_sample",
    )(sinks, q_all, cache_k, cache_v, k_all, v_all, out_buf)


ROUTE_COLS = 8


def _route_kernel(x_ref, g_ref, wh_ref, wl_ref, br_ref, ltri_ref, xn_ref, rec_ref, rect_ref, cnt_ref, cnt_scr):
    i = pl.program_id(0)

    @pl.when(i == 0)
    def _():
        cnt_scr[...] = jnp.zeros_like(cnt_scr)

    xn = _rms(x_ref[...], g_ref[...])
    xn_ref[...] = _pack_halves(xn)
    xh = xn.astype(BF16)
    xl = (xn - xh.astype(F32)).astype(BF16)
    logits = (jnp.dot(xh, wh_ref[...], preferred_element_type=F32)
              + (jnp.dot(xh, wl_ref[...], preferred_element_type=F32)
                 + jnp.dot(xl, wh_ref[...], preferred_element_type=F32))) + br_ref[...]
    rows = logits.shape[0]
    lane = lax.broadcasted_iota(jnp.int32, (rows, LANES), 1).astype(F32)
    neg = jnp.float32(-jnp.inf)
    big = jnp.float32(LANES)

    is_g = (lane >= N_EXPERTS) & (lane < N_EXPERTS + N_GROUPS)
    gl = jnp.where(is_g, logits, neg)
    gmax = jnp.max(gl, axis=-1, keepdims=True)
    gsel = jnp.min(jnp.where(gl == gmax, lane, big), axis=-1, keepdims=True) - N_EXPERTS
    gden = jnp.sum(jnp.where(is_g, jnp.exp(gl - gmax), 0.0), axis=-1, keepdims=True)
    gw = 1.0 / gden

    in_grp = (lane >= gsel * EXPERTS_PER_GROUP) & (lane < (gsel + 1) * EXPERTS_PER_GROUP)
    el = jnp.where(in_grp, logits, neg)
    t1 = jnp.max(el, axis=-1, keepdims=True)
    e1 = jnp.min(jnp.where(el == t1, lane, big), axis=-1, keepdims=True)
    el2 = jnp.where(lane == e1, neg, el)
    t2 = jnp.max(el2, axis=-1, keepdims=True)
    e2 = jnp.min(jnp.where(el2 == t2, lane, big), axis=-1, keepdims=True)
    x2 = jnp.exp(t2 - t1)
    w1 = gw / (1.0 + x2)
    w2 = gw * x2 / (1.0 + x2)

    oh1 = (lane == e1).astype(F32)
    oh2 = (lane == e2).astype(F32)
    oh = oh1 + oh2
    before = jnp.dot(ltri_ref[...], oh.astype(BF16), preferred_element_type=F32)
    base = cnt_scr[...] + before
    r1 = jnp.sum(base * oh1, axis=-1, keepdims=True)
    r2 = jnp.sum(base * oh2, axis=-1, keepdims=True)
    cnt_scr[...] = cnt_scr[...] + jnp.sum(oh, axis=0, keepdims=True)

    rec = jnp.where(lane == 0, e1,
          jnp.where(lane == 1, e2,
          jnp.where(lane == 2, r1,
          jnp.where(lane == 3, r2,
          jnp.where(lane == 4, w1,
          jnp.where(lane == 5, w2, 0.0))))))
    rec_ref[...] = rec
    rect_ref[...] = jnp.concatenate([rec[:LANES].T[:ROUTE_COLS], rec[LANES:].T[:ROUTE_COLS]], axis=1)
    cnt_ref[...] = cnt_scr[...]


def moe_route(h, gain, w_router, b_router, ltri):
    t, d = h.shape
    row = lambda i: (i, 0)
    fix = lambda i: (0, 0)
    w_hi = w_router.astype(BF16)
    w_lo = (w_router - w_hi.astype(F32)).astype(BF16)
    return pl.pallas_call(
        _route_kernel,
        out_shape=(jax.ShapeDtypeStruct((t, d // 2), U32), jax.ShapeDtypeStruct((t, LANES), F32),
                   jax.ShapeDtypeStruct((ROUTE_COLS, t), F32), jax.ShapeDtypeStruct((1, LANES), F32)),
        grid=(t // ROW_TILE,),
        in_specs=[pl.BlockSpec((ROW_TILE, d), row), pl.BlockSpec((1, d), fix),
                  pl.BlockSpec((d, LANES), fix), pl.BlockSpec((d, LANES), fix), pl.BlockSpec((1, LANES), fix),
                  pl.BlockSpec((ROW_TILE, ROW_TILE), fix)],
        out_specs=(pl.BlockSpec((ROW_TILE, d // 2), row), pl.BlockSpec((ROW_TILE, LANES), row),
                   pl.BlockSpec((ROUTE_COLS, ROW_TILE), lambda i: (0, i)), pl.BlockSpec((1, LANES), fix)),
        scratch_shapes=[pltpu.VMEM((1, LANES), F32)],
        compiler_params=_cparams(("arbitrary",)),
        name="moe_route",
    )(h, gain.reshape(1, d), w_hi, w_lo, b_router, ltri)


def _row_copy(src, src_row, dst, dst_row, sem):
    return pltpu.make_async_copy(src.at[pl.ds(src_row, 1)], dst.at[pl.ds(dst_row, 1)], sem)


def _dispatch_kernel(dest_ref, xn_ref, xs_ref, sem):
    rows = xn_ref.shape[0]

    def issue(r, c):
        _row_copy(xn_ref, r, xs_ref, dest_ref[0, 0, r], sem).start(priority=0)
        _row_copy(xn_ref, r, xs_ref, dest_ref[0, 0, rows + r], sem).start(priority=1)
        return c

    lax.fori_loop(0, rows, issue, 0, unroll=8)
    for _ in range(2):
        pltpu.make_async_copy(xn_ref, xs_ref.at[pl.ds(0, rows)], sem).wait()


def moe_dispatch(xn, dest3, n_slots):
    t, d = xn.shape
    return pl.pallas_call(
        _dispatch_kernel,
        out_shape=jax.ShapeDtypeStruct((n_slots, d), F32),
        grid=(t // ROW_TILE,),
        in_specs=[pl.BlockSpec((1, 1, 2 * ROW_TILE), lambda i: (i, 0, 0), memory_space=pltpu.SMEM),
                  pl.BlockSpec((ROW_TILE, d), lambda i: (i, 0))],
        out_specs=pl.BlockSpec(memory_space=pl.ANY),
        scratch_shapes=[pltpu.SemaphoreType.DMA],
        compiler_params=_cparams(("arbitrary",)),
        name="moe_dispatch",
    )(dest3, xn)


SC_WINDOW = 64
SC_INDEX_WINDOW = 128


def _sc_mesh():
    return plsc.VectorSubcoreMesh(core_axis_name="core", subcore_axis_name="subcore")


def moe_dispatch_sc(xn, dest_a, dest_b, n_slots):
    t, d = xn.shape

    @pl.kernel(out_type=jax.ShapeDtypeStruct((n_slots, d), xn.dtype), mesh=_sc_mesh(),
               scratch_types=[pltpu.VMEM((SC_WINDOW, d), xn.dtype)], name="moe_dispatch_sc")
    def run(x_hbm, id_hbm, da_hbm, db_hbm, o_hbm, buf):
        def body(id_vmem, da_vmem, db_vmem):
            for j in range(SC_INDEX_WINDOW // SC_WINDOW):
                part = pl.ds(j * SC_WINDOW, SC_WINDOW)
                pltpu.sync_copy(x_hbm.at[id_vmem.at[0, part]], buf)
                pltpu.sync_copy(buf, o_hbm.at[da_vmem.at[0, part]])
                pltpu.sync_copy(buf, o_hbm.at[db_vmem.at[0, part]])

        idx_spec = pl.BlockSpec((1, SC_INDEX_WINDOW), lambda i: (0, i))
        pltpu.emit_pipeline(
            body,
            grid=(t // SC_INDEX_WINDOW,),
            in_specs=[idx_spec, idx_spec, idx_spec],
            out_specs=[],
            core_axis_name=("core", "subcore"),
            dimension_semantics=(pltpu.PARALLEL,),
        )(id_hbm, da_hbm, db_hbm)

    return run(xn, jnp.arange(t, dtype=jnp.int32).reshape(1, t), dest_a, dest_b)


def moe_gather_sc(ys, dest_a, dest_b):
    d = ys.shape[1]
    t = dest_a.shape[1]
    out = jax.ShapeDtypeStruct((t, d), ys.dtype)

    @pl.kernel(out_type=(out, out), mesh=_sc_mesh(), scratch_types=[pltpu.VMEM((SC_WINDOW, d), ys.dtype)],
               name="moe_gather_sc")
    def run(y_hbm, id_hbm, da_hbm, db_hbm, ga_hbm, gb_hbm, buf):
        def body(id_vmem, da_vmem, db_vmem):
            for j in range(SC_INDEX_WINDOW // SC_WINDOW):
                part = pl.ds(j * SC_WINDOW, SC_WINDOW)
                pltpu.sync_copy(y_hbm.at[da_vmem.at[0, part]], buf)
                pltpu.sync_copy(buf, ga_hbm.at[id_vmem.at[0, part]])
                pltpu.sync_copy(y_hbm.at[db_vmem.at[0, part]], buf)
                pltpu.sync_copy(buf, gb_hbm.at[id_vmem.at[0, part]])

        idx_spec = pl.BlockSpec((1, SC_INDEX_WINDOW), lambda i: (0, i))
        pltpu.emit_pipeline(
            body,
            grid=(t // SC_INDEX_WINDOW,),
            in_specs=[idx_spec, idx_spec, idx_spec],
            out_specs=[],
            core_axis_name=("core", "subcore"),
            dimension_semantics=(pltpu.PARALLEL,),
        )(id_hbm, da_hbm, db_hbm)

    return run(ys, jnp.arange(t, dtype=jnp.int32).reshape(1, t), dest_a, dest_b)


def _combine_dense_kernel(h_ref, rec_ref, ga_ref, gb_ref, *out_refs, split):
    i = pl.program_id(0)
    rec = rec_ref[...]
    res = h_ref[...] + rec[:, 4:5] * _unpack_halves(ga_ref[...]) + rec[:, 5:6] * _unpack_halves(gb_ref[...])
    if not split:
        out_refs[0][...] = res
    else:
        @pl.when(i < N_PROMPT_TILES)
        def _():
            out_refs[0][...] = res

        @pl.when((i >= N_PROMPT_TILES) & (i < N_PROMPT_TILES + N_SAMPLE_TILES))
        def _():
            out_refs[1][...] = res


def moe_combine_dense(h, rec, ga, gb, split=False):
    t, d = h.shape
    row = lambda i: (i, 0)
    if split:
        out_shape = (jax.ShapeDtypeStruct((T_PROMPT, d), F32), jax.ShapeDtypeStruct((T_SAMPLE, d), F32))
        out_specs = tuple(_parts_specs(d)[:2])
    else:
        out_shape = jax.ShapeDtypeStruct((t, d), F32)
        out_specs = pl.BlockSpec((ROW_TILE, d), row)
    return pl.pallas_call(
        functools.partial(_combine_dense_kernel, split=split),
        out_shape=out_shape,
        grid=(t // ROW_TILE,),
        in_specs=[pl.BlockSpec((ROW_TILE, d), row), pl.BlockSpec((ROW_TILE, LANES), row),
                  pl.BlockSpec((ROW_TILE, d // 2), row), pl.BlockSpec((ROW_TILE, d // 2), row)],
        out_specs=out_specs,
        compiler_params=_cparams(("arbitrary",)),
        name="moe_combine_dense",
    )(h, rec, ga, gb)


def _ffn_kernel(wblk_ref, we_ref, wlo_ref, whi_ref, xs_ref, w13_ref, w2_ref, ys_ref, w13b, w2b):
    w = pl.program_id(0)
    prev = jnp.maximum(w - 1, 0)
    first_visit = (w == 0) | (wblk_ref[w] != wblk_ref[prev])
    lo = wlo_ref[w]
    hi = whi_ref[w]

    def ffn(x):
        x = _unpack_halves(x).astype(BF16)
        cw = D_EXPERT // FFN_CHUNKS
        gate_up = []
        for c in range(FFN_CHUNKS):
            a = jnp.dot(x, w13b[:, c * cw:(c + 1) * cw], preferred_element_type=F32)
            u = jnp.dot(x, w13b[:, D_EXPERT + c * cw:D_EXPERT + (c + 1) * cw], preferred_element_type=F32)
            gate_up.append((a, u))
        hmid = jnp.concatenate([(_silu(a) * u).astype(BF16) for a, u in gate_up], axis=1)
        return _pack_halves(jnp.dot(hmid, w2b[...], preferred_element_type=F32))

    @pl.when(hi > lo)
    def _():
        @pl.when((w == 0) | (we_ref[w] != we_ref[prev]))
        def _():
            w13b[...] = w13_ref[...].astype(BF16)
            w2b[...] = w2_ref[...].astype(BF16)

        whole = (lo == 0) & (hi == EXPERT_BLOCK)

        @pl.when(whole)
        def _():
            ys_ref[...] = ffn(xs_ref[...])

        half = EXPERT_BLOCK // 2
        for p in range(2):
            rows = slice(p * half, (p + 1) * half)
            touched = (lo < (p + 1) * half) & (hi > p * half)

            @pl.when(jnp.logical_not(whole) & touched)
            def _():
                y = ffn(xs_ref[rows, :])
                row = lax.broadcasted_iota(jnp.int32, y.shape, 0) + p * half
                mine = (row >= lo) & (row < hi)

                @pl.when(first_visit)
                def _():
                    ys_ref[rows, :] = jnp.where(mine, y, jnp.zeros_like(y))

                @pl.when(jnp.logical_not(first_visit))
                def _():
                    ys_ref[rows, :] = jnp.where(mine, y, ys_ref[rows, :])

            @pl.when(jnp.logical_not(whole) & jnp.logical_not(touched) & first_visit)
            def _():
                ys_ref[rows, :] = jnp.zeros((half, ys_ref.shape[1]), U32)


def moe_ffn(xs, work, w13_all, w2_all, layer):
    n_slots, dp = xs.shape
    d = 2 * dp
    n_work = work[0].shape[0]
    xmap = lambda w, wb, we, wlo, whi: (wb[w], 0)
    w_map = lambda w, wb, we, wlo, whi: (layer, we[w], 0, 0)
    grid_spec = pltpu.PrefetchScalarGridSpec(
        num_scalar_prefetch=4,
        grid=(n_work,),
        in_specs=[pl.BlockSpec((EXPERT_BLOCK, dp), xmap),
                  pl.BlockSpec((None, None, d, 2 * D_EXPERT), w_map),
                  pl.BlockSpec((None, None, D_EXPERT, d), w_map)],
        out_specs=pl.BlockSpec((EXPERT_BLOCK, dp), xmap),
        scratch_shapes=[pltpu.VMEM((d, 2 * D_EXPERT), BF16), pltpu.VMEM((D_EXPERT, d), BF16)],
    )
    return pl.pallas_call(
        _ffn_kernel,
        out_shape=jax.ShapeDtypeStruct((n_slots, dp), U32),
        grid_spec=grid_spec,
        compiler_params=_cparams(("arbitrary",)),
        name="moe_ffn",
    )(*work, xs, w13_all, w2_all)


def _ffn_work_items(cnt):
    n_slots = 2 * T_ALL
    n_blocks = n_slots // EXPERT_BLOCK
    n_work = n_blocks + N_EXPERTS - 1
    end = jnp.cumsum(cnt)
    start = end - cnt
    first_blk = start // EXPERT_BLOCK
    last_blk = jnp.maximum(end - 1, start) // EXPERT_BLOCK
    n_items = jnp.where(cnt > 0, last_blk - first_blk + 1, 0)
    item_end = jnp.cumsum(n_items)
    item_start = item_end - n_items
    w = jnp.arange(n_work, dtype=jnp.int32)
    used = w < item_end[-1]
    wq = jnp.minimum(w, item_end[-1] - 1)
    e = jnp.sum((item_end[:, None] <= wq[None, :]).astype(jnp.int32), axis=0)
    onehot = e[None, :] == jnp.arange(N_EXPERTS, dtype=jnp.int32)[:, None]
    of_e = lambda table: jnp.sum(jnp.where(onehot, table[:, None], 0), axis=0)
    blk = jnp.where(used, of_e(first_blk) + (w - of_e(item_start)), n_blocks - 1).astype(jnp.int32)
    lo = jnp.maximum(of_e(start), blk * EXPERT_BLOCK) - blk * EXPERT_BLOCK
    hi = jnp.minimum(of_e(end), (blk + 1) * EXPERT_BLOCK) - blk * EXPERT_BLOCK
    lo = jnp.where(used, lo, 0).astype(jnp.int32)
    hi = jnp.where(used, hi, 0).astype(jnp.int32)
    return start, (blk, e, lo, hi)


def _combine_kernel(dest_ref, h_ref, rec_ref, ys_ref, *rest, split):
    out_refs, (g1, g2, sem) = rest[:-3], rest[-3:]
    rows = h_ref.shape[0]
    i = pl.program_id(0)

    def issue(r, c):
        _row_copy(ys_ref, dest_ref[0, 0, r], g1, r, sem).start(priority=0)
        _row_copy(ys_ref, dest_ref[0, 0, rows + r], g2, r, sem).start(priority=1)
        return c

    lax.fori_loop(0, rows, issue, 0, unroll=8)
    for buf in (g1, g2):
        pltpu.make_async_copy(ys_ref.at[pl.ds(0, rows)], buf, sem).wait()
    rec = rec_ref[...]
    res = h_ref[...] + rec[:, 4:5] * g1[...] + rec[:, 5:6] * g2[...]
    if not split:
        out_refs[0][...] = res
    else:
        @pl.when(i < N_PROMPT_TILES)
        def _():
            out_refs[0][...] = res

        @pl.when((i >= N_PROMPT_TILES) & (i < N_PROMPT_TILES + N_SAMPLE_TILES))
        def _():
            out_refs[1][...] = res


def moe_combine(h, rec, ys, dest3, split=False):
    t, d = h.shape
    row = lambda i: (i, 0)
    if split:
        parts = _parts_specs(d)[:2]
        out_shape = (jax.ShapeDtypeStruct((T_PROMPT, d), F32), jax.ShapeDtypeStruct((T_SAMPLE, d), F32))
        out_specs = tuple(parts)
    else:
        out_shape = jax.ShapeDtypeStruct((t, d), F32)
        out_specs = pl.BlockSpec((ROW_TILE, d), row)
    return pl.pallas_call(
        functools.partial(_combine_kernel, split=split),
        out_shape=out_shape,
        grid=(t // ROW_TILE,),
        in_specs=[pl.BlockSpec((1, 1, 2 * ROW_TILE), lambda i: (i, 0, 0), memory_space=pltpu.SMEM),
                  pl.BlockSpec((ROW_TILE, d), row), pl.BlockSpec((ROW_TILE, LANES), row),
                  pl.BlockSpec(memory_space=pl.ANY)],
        out_specs=out_specs,
        scratch_shapes=[pltpu.VMEM((ROW_TILE, d), F32), pltpu.VMEM((ROW_TILE, d), F32),
                        pltpu.SemaphoreType.DMA],
        compiler_params=_cparams(("arbitrary",)),
        name="moe_combine",
    )(dest3, h, rec, ys)


def hier_moe_layer(h, layer, gain, w_group, b_group, w_expert, b_expert, w13_all, w2_all, ltri, split_out=False):
    t = h.shape[0]
    pad = LANES - N_EXPERTS - N_GROUPS
    w_router = jnp.concatenate([w_expert, w_group, jnp.zeros((D_MODEL, pad), F32)], axis=1)
    b_router = jnp.concatenate([b_expert, b_group, jnp.zeros((pad,), F32)]).reshape(1, LANES)
    xn, rec, rect, counts = moe_route(h, gain, w_router, b_router, ltri)

    cnt = counts[0, :N_EXPERTS].astype(jnp.int32)
    start, work = _ffn_work_items(cnt)
    experts = jnp.arange(N_EXPERTS, dtype=jnp.int32)[:, None]

    def slot_of(e_row, rank_row):
        first = jnp.sum(jnp.where(e_row.astype(jnp.int32)[None, :] == experts, start[:, None], 0), axis=0)
        return (first + rank_row.astype(jnp.int32)).reshape(1, t)

    dest_a = slot_of(rect[0], rect[2])
    dest_b = slot_of(rect[1], rect[3])

    xs = moe_dispatch_sc(xn, dest_a, dest_b, 2 * t)
    ys = moe_ffn(xs, work, w13_all, w2_all, layer)
    ga, gb = moe_gather_sc(ys, dest_a, dest_b)
    return moe_combine_dense(h, rec, ga, gb, split=split_out)


def _rope_tables(pos):
    half = ROPE_DIM // 2
    inv = jnp.exp(-math.log(ROPE_THETA) * jnp.arange(half, dtype=F32) * (2.0 / ROPE_DIM))
    ang = pos.astype(F32)[:, None] * inv[None, :]
    cos, sin = jnp.cos(ang), jnp.sin(ang)
    t = pos.shape[0]
    ones = jnp.ones((t, HEAD_DIM - ROPE_DIM), F32)
    zeros = jnp.zeros((t, HEAD_DIM - ROPE_DIM), F32)
    z8 = jnp.zeros((t, half), F32)
    cos_h = jnp.concatenate([cos, cos, ones], axis=1)
    sina_h = jnp.concatenate([-sin, z8, zeros], axis=1)
    sinb_h = jnp.concatenate([z8, sin, zeros], axis=1)
    two = lambda a: jnp.concatenate([a, a], axis=1)
    return two(cos_h), two(sina_h), two(sinb_h)


def kernel(x_prompt, x_sample, state_hgrn, cache_k_win, cache_v_win, meta_tokens, a_norm, a_w_in, a_lower_logits, a_out_norm, a_w_out, kv_norm, kv_w, k_norm, b_norm, b_wq, b_q_norm, b_sinks, b_w_out, moe_norm, moe_w_group, moe_b_group, moe_w_expert, moe_b_expert, moe_w13, moe_w2):
    tail_rows = T_ALL - OFF_META
    x_parts = (x_prompt.reshape(T_PROMPT, D_MODEL), x_sample.reshape(T_SAMPLE, D_MODEL),
               jnp.concatenate([meta_tokens.astype(F32), jnp.zeros((tail_rows - N_META, D_MODEL), F32)], axis=0))
    pos = jnp.concatenate([N_META + jnp.arange(SEQ, dtype=jnp.int32),
                           jnp.tile(PAST_LEN + jnp.arange(DEC_SEQ, dtype=jnp.int32), ROW_TILE // DEC_SEQ),
                           jnp.arange(N_META, dtype=jnp.int32),
                           jnp.zeros((ROW_TILE - N_META,), jnp.int32)])
    cos_t, sina_t, sinb_t = _rope_tables(pos)
    r256 = np.arange(256)
    hmean = jnp.asarray((r256[:, None] // HEAD_DIM == r256[None, :] // HEAD_DIM).astype(np.float32) / HEAD_DIM, BF16)
    ltri = jnp.asarray((r256[None, :] < r256[:, None]).astype(np.float32), BF16)
    lower = jnp.cumsum(jax.nn.softmax(a_lower_logits.astype(F32), axis=0), axis=0)

    moe = functools.partial(hier_moe_layer, w13_all=moe_w13, w2_all=moe_w2, ltri=ltri)

    z = in_project(x_parts, a_norm[0], a_w_in[0].astype(BF16))
    zero_state = jnp.zeros((1, A_HEADS, A_DK, A_DV), F32)
    o_meta, s_meta = hgrn2_scan(z, zero_state, lower[0], a_out_norm[0],
                                row_off=OFF_META, n_seq=1, seq_len=N_META)
    o_prompt, s_prompt = hgrn2_scan(z, s_meta, lower[0], a_out_norm[0], row_off=0, n_seq=BATCH, seq_len=SEQ)
    o_sample, s_sample = hgrn2_scan(z, state_hgrn[0].astype(F32), lower[0], a_out_norm[0],
                                    row_off=OFF_SAMPLE, n_seq=DEC_BATCH, seq_len=DEC_SEQ, group=SCAN_SAMPLE_GROUP)
    o_tail = jnp.concatenate([o_meta, jnp.zeros((tail_rows - N_META, D_MODEL), BF16)], axis=0)
    h = mixer_out((o_prompt, o_sample, o_tail), a_w_out[0].astype(BF16), x_parts)
    h = moe(h, 0, moe_norm[0], moe_w_group[0], moe_b_group[0], moe_w_expert[0], moe_b_expert[0])

    k_all, v_all = kv_project(h, kv_norm, kv_w.astype(BF16), hmean, jnp.tile(k_norm, KV_HEADS).reshape(1, KV_DIM),
                              cos_t, sina_t, sinb_t)

    q_all = q_project(h, b_norm[0], b_wq[0].astype(BF16), hmean, jnp.tile(b_q_norm[0], Q_HEADS).reshape(1, D_MODEL),
                      cos_t, sina_t, sinb_t)
    meta_blk = lambda a: jnp.concatenate([jnp.zeros((ATT_BLOCK - N_META, KV_DIM), F32),
                                          a[OFF_META:OFF_META + N_META]], axis=0)
    sinks = b_sinks[0].astype(F32)
    att_all = attention_prompt(q_all, k_all, v_all, meta_blk(k_all), meta_blk(v_all), sinks)
    att_all = attention_sample(q_all, cache_k_win.reshape(DEC_BATCH * WINDOW, KV_DIM).astype(F32),
                               cache_v_win.reshape(DEC_BATCH * WINDOW, KV_DIM).astype(F32),
                               k_all, v_all, sinks, att_all)
    h = matmul_residual(att_all, b_w_out[0].astype(BF16), h)
    y_p, y_s = moe(h, 1, moe_norm[1], moe_w_group[1], moe_b_group[1], moe_w_expert[1], moe_b_expert[1],
                   split_out=True)

    y_prompt = y_p.reshape(BATCH, SEQ, D_MODEL)
    y_sample = y_s.reshape(DEC_BATCH, DEC_SEQ, D_MODEL)
    last = lambda a: jnp.stack([a[(b + 1) * SEQ - WINDOW:(b + 1) * SEQ] for b in range(BATCH)]).reshape(
        BATCH, WINDOW, KV_HEADS, HEAD_DIM)
    kp = last(k_all)
    vp = last(v_all)
    ks = k_all[OFF_SAMPLE:OFF_SAMPLE + T_SAMPLE].reshape(DEC_BATCH, DEC_SEQ, KV_HEADS, HEAD_DIM)
    vs = v_all[OFF_SAMPLE:OFF_SAMPLE + T_SAMPLE].reshape(DEC_BATCH, DEC_SEQ, KV_HEADS, HEAD_DIM)
    k_win_s = jnp.concatenate([cache_k_win, ks], axis=1)[:, -WINDOW:]
    v_win_s = jnp.concatenate([cache_v_win, vs], axis=1)[:, -WINDOW:]
    return (y_prompt, y_sample, s_prompt[None], s_sample[None], kp, vp, k_win_s, v_win_s)
```

```python
import functools
import math

import numpy as np
import jax
import jax.numpy as jnp
from jax import lax
from jax.experimental import pallas as pl
from jax.experimental.pallas import tpu as pltpu
from jax.experimental.pallas import tpu_sc as plsc

F32 = jnp.float32
BF16 = jnp.bfloat16
U32 = jnp.uint32

D_MODEL = 1024
BATCH = 4
SEQ = 4096
DEC_BATCH = 128
DEC_SEQ = 8
PAST_LEN = 8192
N_META = 16
A_HEADS = 8
A_DK = 128
A_DV = 128
Q_HEADS = 16
KV_HEADS = 4
HEAD_DIM = 64
KV_DIM = KV_HEADS * HEAD_DIM
WINDOW = 128
ROPE_DIM = 16
ROPE_THETA = 500000.0
N_GROUPS = 4
EXPERTS_PER_GROUP = 8
N_EXPERTS = 32
D_EXPERT = 512
RMS_EPS = 1e-6

LANES = 128
SUBLANES = 8
VMEM_LIMIT = 56 * 1024 * 1024

ROW_TILE = 256
T_PROMPT = BATCH * SEQ
T_SAMPLE = DEC_BATCH * DEC_SEQ
OFF_SAMPLE = T_PROMPT
OFF_META = T_PROMPT + T_SAMPLE
T_REAL = OFF_META + N_META
T_ALL = -(-T_REAL // ROW_TILE) * ROW_TILE
N_TILES = T_ALL // ROW_TILE

SCAN_CHUNK = 128
SCAN_SAMPLE_GROUP = 8
ATT_BLOCK = 128
EXPERT_BLOCK = 512
FFN_CHUNKS = 2


def _cparams(sem):
    return pltpu.CompilerParams(dimension_semantics=sem, vmem_limit_bytes=VMEM_LIMIT)


def _nt_dot(a, b):
    return lax.dot_general(a, b, (((1,), (1,)), ((), ())), preferred_element_type=F32)


def _rms(x, gain):
    ms = jnp.mean(x * x, axis=-1, keepdims=True)
    return x * lax.rsqrt(ms + RMS_EPS) * gain


def _silu(x):
    return x * jax.nn.sigmoid(x)


def _pack_halves(x):
    w = x.shape[1] // 2
    hi = lax.bitcast_convert_type(x[:, :w].astype(BF16).astype(F32), U32)
    lo = lax.bitcast_convert_type(x[:, w:].astype(BF16).astype(F32), U32)
    return hi | (lo >> 16)


def _unpack_halves(p):
    hi = lax.bitcast_convert_type(p & jnp.uint32(0xFFFF0000), F32)
    lo = lax.bitcast_convert_type(p << 16, F32)
    return jnp.concatenate([hi, lo], axis=1)


N_PROMPT_TILES = T_PROMPT // ROW_TILE
N_SAMPLE_TILES = T_SAMPLE // ROW_TILE


def _parts_specs(width):
    return [pl.BlockSpec((ROW_TILE, width), lambda i: (jnp.minimum(i, N_PROMPT_TILES - 1), 0)),
            pl.BlockSpec((ROW_TILE, width), lambda i: (jnp.clip(i - N_PROMPT_TILES, 0, N_SAMPLE_TILES - 1), 0)),
            pl.BlockSpec((ROW_TILE, width), lambda i: (0, 0))]


def _pick_part(i, p_ref, s_ref, t_ref, dtype):
    return jnp.where(i < N_PROMPT_TILES, p_ref[...].astype(dtype),
                     jnp.where(i < N_PROMPT_TILES + N_SAMPLE_TILES, s_ref[...].astype(dtype),
                               t_ref[...].astype(dtype)))


def _in_proj_kernel(xp_ref, xs_ref, xt_ref, g_ref, w_ref, o_ref):
    x = _pick_part(pl.program_id(0), xp_ref, xs_ref, xt_ref, F32)
    xn = _rms(x, g_ref[...])
    o_ref[...] = jnp.dot(xn.astype(BF16), w_ref[...], preferred_element_type=F32)


def in_project(x_parts, gain, w_bf16):
    d, n = w_bf16.shape
    return pl.pallas_call(
        _in_proj_kernel,
        out_shape=jax.ShapeDtypeStruct((T_ALL, n), F32),
        grid=(N_TILES,),
        in_specs=_parts_specs(d) + [pl.BlockSpec((1, d), lambda i: (0, 0)),
                                    pl.BlockSpec((d, n), lambda i: (0, 0))],
        out_specs=pl.BlockSpec((ROW_TILE, n), lambda i: (i, 0)),
        compiler_params=_cparams(("parallel",)),
        name="in_project",
    )(*x_parts, gain.reshape(1, d), w_bf16)


def _mixer_out_kernel(ap_ref, as_ref, at_ref, w_ref, xp_ref, xs_ref, xt_ref, o_ref):
    i = pl.program_id(0)
    a = _pick_part(i, ap_ref, as_ref, at_ref, BF16)
    x = _pick_part(i, xp_ref, xs_ref, xt_ref, F32)
    o_ref[...] = x + jnp.dot(a, w_ref[...], preferred_element_type=F32)


def mixer_out(a_parts, w_bf16, x_parts):
    k, n = w_bf16.shape
    return pl.pallas_call(
        _mixer_out_kernel,
        out_shape=jax.ShapeDtypeStruct((T_ALL, n), F32),
        grid=(N_TILES,),
        in_specs=_parts_specs(k) + [pl.BlockSpec((k, n), lambda i: (0, 0))] + _parts_specs(n),
        out_specs=pl.BlockSpec((ROW_TILE, n), lambda i: (i, 0)),
        compiler_params=_cparams(("parallel",)),
        name="mixer_out",
    )(*a_parts, w_bf16, *x_parts)


def _head_norm_rope(y, hmean_ref, hgain, cos_t, sina_t, sinb_t):
    rows, width = y.shape
    sq = (y * y).astype(BF16)
    parts = []
    for s in range(width // 256):
        parts.append(jnp.dot(sq[:, s * 256:(s + 1) * 256], hmean_ref[...], preferred_element_type=F32))
    ms = parts[0] if len(parts) == 1 else jnp.concatenate(parts, axis=1)
    yn = y * lax.rsqrt(ms + RMS_EPS) * hgain
    reps = width // LANES
    cos_w = jnp.concatenate([cos_t] * reps, axis=1)
    sina_w = jnp.concatenate([sina_t] * reps, axis=1)
    sinb_w = jnp.concatenate([sinb_t] * reps, axis=1)
    half = ROPE_DIM // 2
    nxt = pltpu.roll(yn, width - half, 1)
    prv = pltpu.roll(yn, half, 1)
    return yn * cos_w + nxt * sina_w + prv * sinb_w


def _kv_kernel(x_ref, g_ref, w_ref, hmean_ref, hg_ref, cos_ref, sina_ref, sinb_ref, k_ref, v_ref):
    xn = _rms(x_ref[...], g_ref[...])
    z = jnp.dot(xn.astype(BF16), w_ref[...], preferred_element_type=F32)
    k = _head_norm_rope(z[:, :KV_DIM], hmean_ref, hg_ref[...], cos_ref[...], sina_ref[...], sinb_ref[...])
    k_ref[...] = k
    v_ref[...] = z[:, KV_DIM:]


def _rope_tile(i):
    tiles_per_seq = SEQ // ROW_TILE
    n_prompt_tiles = T_PROMPT // ROW_TILE
    n_sample_tiles = T_SAMPLE // ROW_TILE
    return (jnp.where(i < n_prompt_tiles, i % tiles_per_seq,
                      jnp.where(i < n_prompt_tiles + n_sample_tiles, tiles_per_seq, tiles_per_seq + 1)), 0)


def kv_project(x, gain, w_bf16, hmean, hgain_w, cos_t, sina_t, sinb_t):
    t, d = x.shape
    row = lambda i: (i, 0)
    fix = lambda i: (0, 0)
    return pl.pallas_call(
        _kv_kernel,
        out_shape=(jax.ShapeDtypeStruct((t, KV_DIM), F32), jax.ShapeDtypeStruct((t, KV_DIM), F32)),
        grid=(t // ROW_TILE,),
        in_specs=[pl.BlockSpec((ROW_TILE, d), row), pl.BlockSpec((1, d), fix),
                  pl.BlockSpec((d, 2 * KV_DIM), fix), pl.BlockSpec((256, 256), fix),
                  pl.BlockSpec((1, KV_DIM), fix),
                  pl.BlockSpec((ROW_TILE, LANES), _rope_tile), pl.BlockSpec((ROW_TILE, LANES), _rope_tile),
                  pl.BlockSpec((ROW_TILE, LANES), _rope_tile)],
        out_specs=(pl.BlockSpec((ROW_TILE, KV_DIM), row), pl.BlockSpec((ROW_TILE, KV_DIM), row)),
        compiler_params=_cparams(("parallel",)),
        name="kv_project",
    )(x, gain.reshape(1, d), w_bf16, hmean, hgain_w, cos_t, sina_t, sinb_t)


def _q_kernel(x_ref, g_ref, w_ref, hmean_ref, hg_ref, cos_ref, sina_ref, sinb_ref, q_ref):
    xn = _rms(x_ref[...], g_ref[...])
    z = jnp.dot(xn.astype(BF16), w_ref[...], preferred_element_type=F32)
    q = _head_norm_rope(z, hmean_ref, hg_ref[...], cos_ref[...], sina_ref[...], sinb_ref[...])
    q_ref[...] = (q * HEAD_DIM ** -0.5).astype(q_ref.dtype)


def q_project(x, gain, w_bf16, hmean, hgain_w, cos_t, sina_t, sinb_t):
    t, d = x.shape
    row = lambda i: (i, 0)
    fix = lambda i: (0, 0)
    return pl.pallas_call(
        _q_kernel,
        out_shape=jax.ShapeDtypeStruct((t, d), BF16),
        grid=(t // ROW_TILE,),
        in_specs=[pl.BlockSpec((ROW_TILE, d), row), pl.BlockSpec((1, d), fix),
                  pl.BlockSpec((d, d), fix), pl.BlockSpec((256, 256), fix),
                  pl.BlockSpec((1, d), fix),
                  pl.BlockSpec((ROW_TILE, LANES), _rope_tile), pl.BlockSpec((ROW_TILE, LANES), _rope_tile),
                  pl.BlockSpec((ROW_TILE, LANES), _rope_tile)],
        out_specs=pl.BlockSpec((ROW_TILE, d), row),
        compiler_params=_cparams(("parallel",)),
        name="q_project",
    )(x, gain.reshape(1, d), w_bf16, hmean, hgain_w, cos_t, sina_t, sinb_t)


def _matmul_residual_kernel(a_ref, w_ref, r_ref, o_ref):
    o_ref[...] = r_ref[...] + jnp.dot(a_ref[...], w_ref[...], preferred_element_type=F32)


def matmul_residual(a_bf16, w_bf16, resid):
    t, k = a_bf16.shape
    n = w_bf16.shape[1]
    return pl.pallas_call(
        _matmul_residual_kernel,
        out_shape=jax.ShapeDtypeStruct((t, n), F32),
        grid=(t // ROW_TILE,),
        in_specs=[pl.BlockSpec((ROW_TILE, k), lambda i: (i, 0)),
                  pl.BlockSpec((k, n), lambda i: (0, 0)),
                  pl.BlockSpec((ROW_TILE, n), lambda i: (i, 0))],
        out_specs=pl.BlockSpec((ROW_TILE, n), lambda i: (i, 0)),
        compiler_params=_cparams(("parallel",)),
        name="matmul_residual",
    )(a_bf16, w_bf16, resid)


def _scan_levels(c):
    levels = []
    m = c
    while m >= 2:
        levels.append(m)
        m //= 2
    return levels


LOG2E = 1.4426950408889634


def _scan_kernel(z_ref, s0_ref, lb_ref, og_ref, tri_ref, lmask_ref, sgn_ref, o_ref, sfin_ref, s_scr, b_scr,
                 *, rows, seq_len):
    c_idx = pl.program_id(1)
    levels = _scan_levels(seq_len)
    n_sub = rows // seq_len
    hk = A_HEADS * A_DK

    @pl.when(c_idx == 0)
    def _():
        s_scr[...] = s0_ref[...]

    sub = lax.broadcasted_iota(jnp.int32, (SUBLANES, LANES), 0)
    row = lax.broadcasted_iota(jnp.int32, (LANES, LANES), 0)
    og = og_ref[...]

    def pad_f32(x):
        if x.shape[0] == LANES:
            return x
        return jnp.concatenate([x, jnp.zeros((LANES - x.shape[0], x.shape[1]), x.dtype)], axis=0)

    def pad_rows(x):
        return pad_f32(x).astype(BF16)

    def cols(part, h):
        return slice(part * hk + h * LANES, part * hk + (h + 1) * LANES)

    def gates(h):
        lb = lb_ref[:, cols(0, h)]
        forget = lb + (1.0 - lb) * jax.nn.sigmoid(z_ref[:, cols(1, h)])
        logf = jnp.log(forget)
        hi = logf.astype(BF16).astype(F32)
        r1 = logf - hi
        mid = r1.astype(BF16).astype(F32)
        lo = r1 - mid
        cs = jnp.dot(tri_ref[...], pad_rows(jnp.concatenate([hi, mid, lo], axis=1)),
                     preferred_element_type=F32)
        b = (cs[:rows, :LANES] + cs[:rows, LANES:2 * LANES]) + cs[:rows, 2 * LANES:]
        b_scr[h] = b
        return _silu(z_ref[:, cols(0, h)]), 1.0 - forget, b

    def bref_for(h, m):
        b_rows = b_scr.at[h]
        half = m // 2
        pieces = []
        for g in range(rows // SUBLANES):
            base = g * SUBLANES
            if m >= SUBLANES:
                r = (base // m) * m + half - 1
                piece = jnp.broadcast_to(b_rows[r:r + 1, :], (SUBLANES, LANES))
            else:
                piece = jnp.broadcast_to(b_rows[base + half - 1:base + half, :], (SUBLANES, LANES))
                for blk in range(1, SUBLANES // m):
                    r = base + blk * m + half - 1
                    piece = jnp.where(sub >= blk * m,
                                      jnp.broadcast_to(b_rows[r:r + 1, :], (SUBLANES, LANES)), piece)
            pieces.append(piece)
        return pieces[0] if len(pieces) == 1 else jnp.concatenate(pieces, axis=0)

    heads = range(A_HEADS)
    qkb = [gates(h) for h in heads]
    att = [_nt_dot(pad_rows(qf), pad_rows(kf)) * lmask_ref[len(levels)] for qf, kf, _ in qkb]
    for li, m in enumerate(levels):
        for h in heads:
            qf, kf, b = qkb[h]
            sgn = sgn_ref[li]
            e = jnp.exp2((b - bref_for(h, m)) * sgn)
            w = pad_rows(jnp.where(sgn > 0, qf, kf) * e)
            att[h] = att[h] + _nt_dot(w, w) * lmask_ref[li]

    def finish(h):
        qf, kf, b = qkb[h]
        b_rows = b_scr.at[h]
        v_b = pad_rows(z_ref[:, cols(2, h)])
        o_intra = jnp.dot(att[h].astype(BF16), v_b, preferred_element_type=F32)
        eb = jnp.exp(b)
        qs = qf * eb
        b_end = [jnp.broadcast_to(b_rows[(i + 1) * seq_len - 1:(i + 1) * seq_len, :], (seq_len, LANES))
                 for i in range(n_sub)]
        b_end = b_end[0] if n_sub == 1 else jnp.concatenate(b_end, axis=0)
        kd_t = pad_f32(kf * jnp.exp(b_end - b)).T.astype(BF16)
        eb_t = pad_f32(eb).T
        qs_b = pad_rows(qs)
        o = o_intra
        for i in range(n_sub):
            s_old = s_scr[i, h]
            first, last = i * seq_len, (i + 1) * seq_len - 1
            if n_sub == 1:
                qs_i, v_i = qs_b, v_b
            else:
                mine = (row >= first) & (row <= last)
                qs_i = jnp.where(mine, qs_b, jnp.zeros_like(qs_b))
                v_i = jnp.where(mine, v_b, jnp.zeros_like(v_b))
            o = o + jnp.dot(qs_i, s_old.astype(BF16), preferred_element_type=F32)
            decay = jnp.broadcast_to(eb_t[:, last:last + 1], (LANES, LANES))
            s_scr[i, h] = decay * s_old + jnp.dot(kd_t, v_i, preferred_element_type=F32)
        o = o[:rows]

        on = _rms(o, og) * _silu(z_ref[:, cols(3, h)])
        o_ref[:, cols(0, h)] = on.astype(o_ref.dtype)

    for h in heads:
        finish(h)

    @pl.when(c_idx == pl.num_programs(1) - 1)
    def _():
        sfin_ref[...] = s_scr[...]


def _scan_consts(rows, seq_len):
    levels = _scan_levels(seq_len)
    r = np.arange(LANES)
    t, s = r[:, None], r[None, :]
    live = (t < rows) & (s < rows)
    tri = ((s <= t) & (t // seq_len == s // seq_len) & live).astype(np.float32)
    masks, sgns = [], []
    for m in levels:
        masks.append(((t // m == s // m) & (t % m >= m // 2) & (s % m < m // 2) & live).astype(np.float32))
        sgns.append(np.broadcast_to(np.where(r[:rows, None] % m >= m // 2, LOG2E, -LOG2E), (rows, LANES)))
    masks.append(((t == s) & live).astype(np.float32))
    return jnp.asarray(tri, BF16), jnp.asarray(np.stack(masks), F32), jnp.asarray(np.stack(sgns), F32)


def hgrn2_scan(z, s0, lb, o_gain, *, row_off, n_seq, seq_len, group=1):
    hv = A_HEADS * A_DV
    if seq_len > SCAN_CHUNK:
        assert group == 1
        sub_len, rows, n_chunks, n_steps = SCAN_CHUNK, SCAN_CHUNK, seq_len // SCAN_CHUNK, n_seq
    else:
        sub_len, rows, n_chunks, n_steps = seq_len, group * seq_len, 1, n_seq // group
    blk_off = row_off // rows
    tri, lmask, sgn = _scan_consts(rows, sub_len)
    shared_s0 = s0.shape[0] == 1
    fix2 = lambda s, c: (0, 0)
    fix3 = lambda s, c: (0, 0, 0)
    o, sfin = pl.pallas_call(
        functools.partial(_scan_kernel, rows=rows, seq_len=sub_len),
        out_shape=(jax.ShapeDtypeStruct((n_seq * seq_len, hv), BF16 if rows % 16 == 0 else F32),
                   jax.ShapeDtypeStruct((n_seq, A_HEADS, A_DK, A_DV), F32)),
        grid=(n_steps, n_chunks),
        in_specs=[pl.BlockSpec((rows, 4 * hv), lambda s, c: (blk_off + s * n_chunks + c, 0)),
                  pl.BlockSpec((group, A_HEADS, A_DK, A_DV), (lambda s, c: (0, 0, 0, 0)) if shared_s0
                               else (lambda s, c: (s, 0, 0, 0))),
                  pl.BlockSpec((1, hv), fix2), pl.BlockSpec((1, A_DV), fix2),
                  pl.BlockSpec((LANES, LANES), fix2), pl.BlockSpec(lmask.shape, fix3),
                  pl.BlockSpec(sgn.shape, fix3)],
        out_specs=(pl.BlockSpec((rows, hv), lambda s, c: (s * n_chunks + c, 0)),
                   pl.BlockSpec((group, A_HEADS, A_DK, A_DV), lambda s, c: (s, 0, 0, 0))),
        scratch_shapes=[pltpu.VMEM((group, A_HEADS, A_DK, A_DV), F32), pltpu.VMEM((A_HEADS, rows, LANES), F32)],
        compiler_params=_cparams(("parallel", "arbitrary")),
        name=f"hgrn2_scan_r{rows}",
    )(z, s0, lb.reshape(1, hv), o_gain.reshape(1, A_DV), tri, lmask, sgn)
    return o, sfin


KEYS = 2 * ATT_BLOCK


def _pair_operand(x, kh):
    slab = x[:, (kh // 2) * LANES:(kh // 2 + 1) * LANES]
    lane = lax.broadcasted_iota(jnp.int32, slab.shape, 1)
    if kh % 2 == 0:
        lo = jnp.where(lane < HEAD_DIM, slab, 0.0)
        hi = pltpu.roll(lo, HEAD_DIM, 1)
    else:
        hi = jnp.where(lane >= HEAD_DIM, slab, 0.0)
        lo = pltpu.roll(hi, HEAD_DIM, 1)
    return jnp.concatenate([lo, hi], axis=0).astype(BF16)


def _window_bias(rows, jmin):
    t_i = lax.broadcasted_iota(jnp.int32, (rows, 2 * KEYS), 0)
    c_i = lax.broadcasted_iota(jnp.int32, (rows, 2 * KEYS), 1)
    j_i = c_i & (ATT_BLOCK - 1)
    own = (c_i & ATT_BLOCK) != 0
    ok = (own & (j_i <= t_i)) | (jnp.logical_not(own) & (j_i >= t_i) & (j_i >= jmin))
    return jnp.where(ok, 0.0, -jnp.inf).astype(F32)


def _pair_softmax(s, sink_a, sink_b):
    probs, rinv = [], []
    for hh, sink in enumerate((sink_a, sink_b)):
        sh = s[:, hh * KEYS:(hh + 1) * KEYS]
        m = jnp.maximum(jnp.max(sh, axis=-1, keepdims=True), sink)
        p = jnp.exp(sh - m)
        den = jnp.sum(p, axis=-1, keepdims=True) + jnp.exp(sink - m)
        probs.append(p.astype(BF16))
        rinv.append(1.0 / den)
    lane = lax.broadcasted_iota(jnp.int32, (s.shape[0], LANES), 1)
    return jnp.concatenate(probs, axis=1), jnp.where(lane < HEAD_DIM, rinv[0], rinv[1])


def _attn_prompt_kernel(sink_ref, q_ref, kp_ref, ko_ref, vp_ref, vo_ref, km_ref, vm_ref, o_ref,
                        k2_scr, v2_scr, s_scr, p_scr, r_scr):
    n = pl.program_id(0)
    nbp = SEQ // ATT_BLOCK
    n_pairs = Q_HEADS // 2

    @pl.when(n >= BATCH * nbp)
    def _():
        o_ref[...] = jnp.zeros_like(o_ref)

    @pl.when(n < BATCH * nbp)
    def _():
        first = (n % nbp) == 0
        jmin = jnp.where(first, ATT_BLOCK - N_META, 0)
        k = jnp.concatenate([jnp.where(first, km_ref[...], kp_ref[...]), ko_ref[...]], axis=0)
        v = jnp.concatenate([jnp.where(first, vm_ref[...], vp_ref[...]), vo_ref[...]], axis=0)
        bias = _window_bias(ATT_BLOCK, jmin)
        for kh in range(KV_HEADS):
            k2_scr[kh] = _pair_operand(k, kh)
            v2_scr[kh] = _pair_operand(v, kh)
        for pair in range(n_pairs):
            s_scr[pair] = _nt_dot(q_ref[:, pair * LANES:(pair + 1) * LANES], k2_scr[pair // 2]) + bias
        for pair in range(n_pairs):
            p, rinv = _pair_softmax(s_scr[pair], sink_ref[2 * pair], sink_ref[2 * pair + 1])
            p_scr[pair] = p
            r_scr[pair] = rinv
        for pair in range(n_pairs):
            o = jnp.dot(p_scr[pair], v2_scr[pair // 2], preferred_element_type=F32) * r_scr[pair]
            o_ref[:, pair * LANES:(pair + 1) * LANES] = o.astype(o_ref.dtype)


def attention_prompt(q_all, k_all, v_all, k_meta_blk, v_meta_blk, sinks):
    n_prompt_blocks = T_PROMPT // ATT_BLOCK
    n_blocks = T_ALL // ATT_BLOCK
    n_pairs = Q_HEADS // 2
    own = lambda n, sk: (jnp.minimum(n, n_prompt_blocks - 1), 0)
    prev = lambda n, sk: (jnp.maximum(jnp.minimum(n, n_prompt_blocks - 1) - 1, 0), 0)
    fix = lambda n, sk: (0, 0)
    grid_spec = pltpu.PrefetchScalarGridSpec(
        num_scalar_prefetch=1,
        grid=(n_blocks,),
        in_specs=[pl.BlockSpec((ATT_BLOCK, D_MODEL), own),
                  pl.BlockSpec((ATT_BLOCK, KV_DIM), prev), pl.BlockSpec((ATT_BLOCK, KV_DIM), own),
                  pl.BlockSpec((ATT_BLOCK, KV_DIM), prev), pl.BlockSpec((ATT_BLOCK, KV_DIM), own),
                  pl.BlockSpec((ATT_BLOCK, KV_DIM), fix), pl.BlockSpec((ATT_BLOCK, KV_DIM), fix)],
        out_specs=pl.BlockSpec((ATT_BLOCK, D_MODEL), lambda n, sk: (n, 0)),
        scratch_shapes=[pltpu.VMEM((KV_HEADS, 2 * KEYS, LANES), BF16), pltpu.VMEM((KV_HEADS, 2 * KEYS, LANES), BF16),
                        pltpu.VMEM((n_pairs, ATT_BLOCK, 2 * KEYS), F32),
                        pltpu.VMEM((n_pairs, ATT_BLOCK, 2 * KEYS), BF16),
                        pltpu.VMEM((n_pairs, ATT_BLOCK, LANES), F32)],
    )
    return pl.pallas_call(
        _attn_prompt_kernel,
        out_shape=jax.ShapeDtypeStruct((T_ALL, D_MODEL), BF16),
        grid_spec=grid_spec,
        compiler_params=_cparams(("parallel",)),
        name="attention_prompt",
    )(sinks, q_all, k_all, k_all, v_all, v_all, k_meta_blk, v_meta_blk)


SAMPLE_GROUP = ATT_BLOCK // DEC_SEQ


def _attn_sample_kernel(sink_ref, q_ref, ck_ref, cv_ref, kn_ref, vn_ref, buf_ref, o_ref,
                        qf_scr, of_scr, k2_scr, v2_scr):
    del buf_ref
    qrows = 2 * DEC_SEQ
    qf_scr[...] = q_ref[...].astype(F32)
    bias = _window_bias(qrows, 0)
    zq = jnp.zeros((qrows - DEC_SEQ, D_MODEL), F32)
    zk = jnp.zeros((ATT_BLOCK - DEC_SEQ, KV_DIM), F32)

    def seq_body(i, carry):
        r_new = pl.multiple_of(i * DEC_SEQ, DEC_SEQ)
        r_old = pl.multiple_of(i * WINDOW, WINDOW)
        q = jnp.concatenate([qf_scr[pl.ds(r_new, DEC_SEQ), :], zq], axis=0).astype(BF16)
        k = jnp.concatenate([ck_ref[pl.ds(r_old, WINDOW), :], kn_ref[pl.ds(r_new, DEC_SEQ), :], zk], axis=0)
        v = jnp.concatenate([cv_ref[pl.ds(r_old, WINDOW), :], vn_ref[pl.ds(r_new, DEC_SEQ), :], zk], axis=0)
        for kh in range(KV_HEADS):
            k2_scr[kh] = _pair_operand(k, kh)
            v2_scr[kh] = _pair_operand(v, kh)
        scores = [_nt_dot(q[:, pair * LANES:(pair + 1) * LANES], k2_scr[pair // 2]) + bias
                  for pair in range(Q_HEADS // 2)]
        soft = [_pair_softmax(s, sink_ref[2 * pair], sink_ref[2 * pair + 1]) for pair, s in enumerate(scores)]
        for pair, (p, rinv) in enumerate(soft):
            o = jnp.dot(p, v2_scr[pair // 2], preferred_element_type=F32) * rinv
            of_scr[pl.ds(r_new, DEC_SEQ), pair * LANES:(pair + 1) * LANES] = o[:DEC_SEQ]
        return carry

    lax.fori_loop(0, SAMPLE_GROUP, seq_body, 0)
    o_ref[...] = of_scr[...].astype(o_ref.dtype)


def attention_sample(q_all, cache_k, cache_v, k_all, v_all, sinks, out_buf):
    first_blk = OFF_SAMPLE // ATT_BLOCK
    new = lambda g, sk: (first_blk + g, 0)
    old = lambda g, sk: (g, 0)
    grid_spec = pltpu.PrefetchScalarGridSpec(
        num_scalar_prefetch=1,
        grid=(DEC_BATCH // SAMPLE_GROUP,),
        in_specs=[pl.BlockSpec((ATT_BLOCK, D_MODEL), new),
                  pl.BlockSpec((SAMPLE_GROUP * WINDOW, KV_DIM), old),
                  pl.BlockSpec((SAMPLE_GROUP * WINDOW, KV_DIM), old),
                  pl.BlockSpec((ATT_BLOCK, KV_DIM), new), pl.BlockSpec((ATT_BLOCK, KV_DIM), new),
                  pl.BlockSpec(memory_space=pl.ANY)],
        out_specs=pl.BlockSpec((ATT_BLOCK, D_MODEL), new),
        scratch_shapes=[pltpu.VMEM((ATT_BLOCK, D_MODEL), F32), pltpu.VMEM((ATT_BLOCK, D_MODEL), F32),
                        pltpu.VMEM((KV_HEADS, 2 * KEYS, LANES), BF16), pltpu.VMEM((KV_HEADS, 2 * KEYS, LANES), BF16)],
    )
    return pl.pallas_call(
        _attn_sample_kernel,
        out_shape=jax.ShapeDtypeStruct(out_buf.shape, out_buf.dtype),
        grid_spec=grid_spec,
        input_output_aliases={6: 0},
        compiler_params=_cparams(("parallel",)),
        name="attention_sample",
    )(sinks, q_all, cache_k, cache_v, k_all, v_all, out_buf)


ROUTE_COLS = 8


def _route_kernel(*refs, parts):
    i = pl.program_id(0)
    if parts:
        (ap_ref, as_ref, at_ref, w_ref, xp_ref, xs_ref, xt_ref), refs = refs[:7], refs[7:]
        a = _pick_part(i, ap_ref, as_ref, at_ref, BF16)
        x = _pick_part(i, xp_ref, xs_ref, xt_ref, F32)
    else:
        (a_ref, w_ref, x_ref), refs = refs[:3], refs[3:]
        a, x = a_ref[...], x_ref[...]
    g_ref, wh_ref, wl_ref, br_ref, ltri_ref, h_ref, xn_ref, rec_ref, rect_ref, cnt_ref, cnt_scr = refs

    @pl.when(i == 0)
    def _():
        cnt_scr[...] = jnp.zeros_like(cnt_scr)

    h = x + jnp.dot(a, w_ref[...], preferred_element_type=F32)
    h_ref[...] = h
    xn = _rms(h, g_ref[...])
    xn_ref[...] = _pack_halves(xn)
    xh = xn.astype(BF16)
    xl = (xn - xh.astype(F32)).astype(BF16)
    logits = (jnp.dot(xh, wh_ref[...], preferred_element_type=F32)
              + (jnp.dot(xh, wl_ref[...], preferred_element_type=F32)
                 + jnp.dot(xl, wh_ref[...], preferred_element_type=F32))) + br_ref[...]
    rows = logits.shape[0]
    lane = lax.broadcasted_iota(jnp.int32, (rows, LANES), 1).astype(F32)
    neg = jnp.float32(-jnp.inf)
    big = jnp.float32(LANES)

    is_g = (lane >= N_EXPERTS) & (lane < N_EXPERTS + N_GROUPS)
    gl = jnp.where(is_g, logits, neg)
    gmax = jnp.max(gl, axis=-1, keepdims=True)
    gsel = jnp.min(jnp.where(gl == gmax, lane, big), axis=-1, keepdims=True) - N_EXPERTS
    gden = jnp.sum(jnp.where(is_g, jnp.exp(gl - gmax), 0.0), axis=-1, keepdims=True)
    gw = 1.0 / gden

    in_grp = (lane >= gsel * EXPERTS_PER_GROUP) & (lane < (gsel + 1) * EXPERTS_PER_GROUP)
    el = jnp.where(in_grp, logits, neg)
    t1 = jnp.max(el, axis=-1, keepdims=True)
    e1 = jnp.min(jnp.where(el == t1, lane, big), axis=-1, keepdims=True)
    el2 = jnp.where(lane == e1, neg, el)
    t2 = jnp.max(el2, axis=-1, keepdims=True)
    e2 = jnp.min(jnp.where(el2 == t2, lane, big), axis=-1, keepdims=True)
    x2 = jnp.exp(t2 - t1)
    w1 = gw / (1.0 + x2)
    w2 = gw * x2 / (1.0 + x2)

    oh1 = (lane == e1).astype(F32)
    oh2 = (lane == e2).astype(F32)
    oh = oh1 + oh2
    before = jnp.dot(ltri_ref[...], oh.astype(BF16), preferred_element_type=F32)
    base = cnt_scr[...] + before
    r1 = jnp.sum(base * oh1, axis=-1, keepdims=True)
    r2 = jnp.sum(base * oh2, axis=-1, keepdims=True)
    cnt_scr[...] = cnt_scr[...] + jnp.sum(oh, axis=0, keepdims=True)

    rec = jnp.where(lane == 0, e1,
          jnp.where(lane == 1, e2,
          jnp.where(lane == 2, r1,
          jnp.where(lane == 3, r2,
          jnp.where(lane == 4, w1,
          jnp.where(lane == 5, w2, 0.0))))))
    rec_ref[...] = rec
    rect_ref[...] = jnp.concatenate([rec[:LANES].T[:ROUTE_COLS], rec[LANES:].T[:ROUTE_COLS]], axis=1)
    cnt_ref[...] = cnt_scr[...]


def moe_route(a, w_out_bf16, x, gain, w_router, b_router, ltri):
    parts = isinstance(a, tuple)
    t, d = T_ALL, D_MODEL
    row = lambda i: (i, 0)
    fix = lambda i: (0, 0)
    w_hi = w_router.astype(BF16)
    w_lo = (w_router - w_hi.astype(F32)).astype(BF16)
    w_spec = pl.BlockSpec((d, d), fix)
    if parts:
        pre_specs = _parts_specs(d) + [w_spec] + _parts_specs(d)
        pre_args = (*a, w_out_bf16, *x)
    else:
        pre_specs = [pl.BlockSpec((ROW_TILE, d), row), w_spec, pl.BlockSpec((ROW_TILE, d), row)]
        pre_args = (a, w_out_bf16, x)
    return pl.pallas_call(
        functools.partial(_route_kernel, parts=parts),
        out_shape=(jax.ShapeDtypeStruct((t, d), F32),
                   jax.ShapeDtypeStruct((t, d // 2), U32), jax.ShapeDtypeStruct((t, LANES), F32),
                   jax.ShapeDtypeStruct((ROUTE_COLS, t), F32), jax.ShapeDtypeStruct((1, LANES), F32)),
        grid=(t // ROW_TILE,),
        in_specs=pre_specs + [pl.BlockSpec((1, d), fix),
                              pl.BlockSpec((d, LANES), fix), pl.BlockSpec((d, LANES), fix),
                              pl.BlockSpec((1, LANES), fix), pl.BlockSpec((ROW_TILE, ROW_TILE), fix)],
        out_specs=(pl.BlockSpec((ROW_TILE, d), row),
                   pl.BlockSpec((ROW_TILE, d // 2), row), pl.BlockSpec((ROW_TILE, LANES), row),
                   pl.BlockSpec((ROUTE_COLS, ROW_TILE), lambda i: (0, i)), pl.BlockSpec((1, LANES), fix)),
        scratch_shapes=[pltpu.VMEM((1, LANES), F32)],
        compiler_params=_cparams(("arbitrary",)),
        name="moe_route",
    )(*pre_args, gain.reshape(1, d), w_hi, w_lo, b_router, ltri)


def _row_copy(src, src_row, dst, dst_row, sem):
    return pltpu.make_async_copy(src.at[pl.ds(src_row, 1)], dst.at[pl.ds(dst_row, 1)], sem)


def _dispatch_kernel(dest_ref, xn_ref, xs_ref, sem):
    rows = xn_ref.shape[0]

    def issue(r, c):
        _row_copy(xn_ref, r, xs_ref, dest_ref[0, 0, r], sem).start(priority=0)
        _row_copy(xn_ref, r, xs_ref, dest_ref[0, 0, rows + r], sem).start(priority=1)
        return c

    lax.fori_loop(0, rows, issue, 0, unroll=8)
    for _ in range(2):
        pltpu.make_async_copy(xn_ref, xs_ref.at[pl.ds(0, rows)], sem).wait()


def moe_dispatch(xn, dest3, n_slots):
    t, d = xn.shape
    return pl.pallas_call(
        _dispatch_kernel,
        out_shape=jax.ShapeDtypeStruct((n_slots, d), F32),
        grid=(t // ROW_TILE,),
        in_specs=[pl.BlockSpec((1, 1, 2 * ROW_TILE), lambda i: (i, 0, 0), memory_space=pltpu.SMEM),
                  pl.BlockSpec((ROW_TILE, d), lambda i: (i, 0))],
        out_specs=pl.BlockSpec(memory_space=pl.ANY),
        scratch_shapes=[pltpu.SemaphoreType.DMA],
        compiler_params=_cparams(("arbitrary",)),
        name="moe_dispatch",
    )(dest3, xn)


SC_WINDOW = 64
SC_INDEX_WINDOW = 128


def _sc_mesh():
    return plsc.VectorSubcoreMesh(core_axis_name="core", subcore_axis_name="subcore")


def moe_dispatch_sc(xn, dest_a, dest_b, n_slots):
    t, d = xn.shape

    @pl.kernel(out_type=jax.ShapeDtypeStruct((n_slots, d), xn.dtype), mesh=_sc_mesh(),
               scratch_types=[pltpu.VMEM((SC_WINDOW, d), xn.dtype)], name="moe_dispatch_sc")
    def run(x_hbm, id_hbm, da_hbm, db_hbm, o_hbm, buf):
        def body(id_vmem, da_vmem, db_vmem):
            for j in range(SC_INDEX_WINDOW // SC_WINDOW):
                part = pl.ds(j * SC_WINDOW, SC_WINDOW)
                pltpu.sync_copy(x_hbm.at[id_vmem.at[0, part]], buf)
                pltpu.sync_copy(buf, o_hbm.at[da_vmem.at[0, part]])
                pltpu.sync_copy(buf, o_hbm.at[db_vmem.at[0, part]])

        idx_spec = pl.BlockSpec((1, SC_INDEX_WINDOW), lambda i: (0, i))
        pltpu.emit_pipeline(
            body,
            grid=(t // SC_INDEX_WINDOW,),
            in_specs=[idx_spec, idx_spec, idx_spec],
            out_specs=[],
            core_axis_name=("core", "subcore"),
            dimension_semantics=(pltpu.PARALLEL,),
        )(id_hbm, da_hbm, db_hbm)

    return run(xn, jnp.arange(t, dtype=jnp.int32).reshape(1, t), dest_a, dest_b)


def moe_gather_sc(ys, dest_a, dest_b):
    d = ys.shape[1]
    t = dest_a.shape[1]
    out = jax.ShapeDtypeStruct((t, d), ys.dtype)

    @pl.kernel(out_type=(out, out), mesh=_sc_mesh(), scratch_types=[pltpu.VMEM((SC_WINDOW, d), ys.dtype)],
               name="moe_gather_sc")
    def run(y_hbm, id_hbm, da_hbm, db_hbm, ga_hbm, gb_hbm, buf):
        def body(id_vmem, da_vmem, db_vmem):
            for j in range(SC_INDEX_WINDOW // SC_WINDOW):
                part = pl.ds(j * SC_WINDOW, SC_WINDOW)
                pltpu.sync_copy(y_hbm.at[da_vmem.at[0, part]], buf)
                pltpu.sync_copy(buf, ga_hbm.at[id_vmem.at[0, part]])
                pltpu.sync_copy(y_hbm.at[db_vmem.at[0, part]], buf)
                pltpu.sync_copy(buf, gb_hbm.at[id_vmem.at[0, part]])

        idx_spec = pl.BlockSpec((1, SC_INDEX_WINDOW), lambda i: (0, i))
        pltpu.emit_pipeline(
            body,
            grid=(t // SC_INDEX_WINDOW,),
            in_specs=[idx_spec, idx_spec, idx_spec],
            out_specs=[],
            core_axis_name=("core", "subcore"),
            dimension_semantics=(pltpu.PARALLEL,),
        )(id_hbm, da_hbm, db_hbm)

    return run(ys, jnp.arange(t, dtype=jnp.int32).reshape(1, t), dest_a, dest_b)


def _combine_dense_kernel(h_ref, rec_ref, ga_ref, gb_ref, *out_refs, split):
    i = pl.program_id(0)
    rec = rec_ref[...]
    res = h_ref[...] + rec[:, 4:5] * _unpack_halves(ga_ref[...]) + rec[:, 5:6] * _unpack_halves(gb_ref[...])
    if not split:
        out_refs[0][...] = res
    else:
        @pl.when(i < N_PROMPT_TILES)
        def _():
            out_refs[0][...] = res

        @pl.when((i >= N_PROMPT_TILES) & (i < N_PROMPT_TILES + N_SAMPLE_TILES))
        def _():
            out_refs[1][...] = res


def moe_combine_dense(h, rec, ga, gb, split=False):
    t, d = h.shape
    row = lambda i: (i, 0)
    if split:
        out_shape = (jax.ShapeDtypeStruct((T_PROMPT, d), F32), jax.ShapeDtypeStruct((T_SAMPLE, d), F32))
        out_specs = tuple(_parts_specs(d)[:2])
    else:
        out_shape = jax.ShapeDtypeStruct((t, d), F32)
        out_specs = pl.BlockSpec((ROW_TILE, d), row)
    return pl.pallas_call(
        functools.partial(_combine_dense_kernel, split=split),
        out_shape=out_shape,
        grid=(t // ROW_TILE,),
        in_specs=[pl.BlockSpec((ROW_TILE, d), row), pl.BlockSpec((ROW_TILE, LANES), row),
                  pl.BlockSpec((ROW_TILE, d // 2), row), pl.BlockSpec((ROW_TILE, d // 2), row)],
        out_specs=out_specs,
        compiler_params=_cparams(("arbitrary",)),
        name="moe_combine_dense",
    )(h, rec, ga, gb)


def _combine_kv_q_kernel(h_ref, rec_ref, ga_ref, gb_ref, gkv_ref, wkv_ref, gq_ref, wq_ref, hmean_ref,
                         hgk_ref, hgq_ref, cos_ref, sina_ref, sinb_ref, ho_ref, k_ref, v_ref, q_ref):
    rec = rec_ref[...]
    h = h_ref[...] + rec[:, 4:5] * _unpack_halves(ga_ref[...]) + rec[:, 5:6] * _unpack_halves(gb_ref[...])
    ho_ref[...] = h
    xhat = h * lax.rsqrt(jnp.mean(h * h, axis=-1, keepdims=True) + RMS_EPS)
    tables = (cos_ref[...], sina_ref[...], sinb_ref[...])
    zkv = jnp.dot((xhat * gkv_ref[...]).astype(BF16), wkv_ref[...], preferred_element_type=F32)
    k_ref[...] = _head_norm_rope(zkv[:, :KV_DIM], hmean_ref, hgk_ref[...], *tables)
    v_ref[...] = zkv[:, KV_DIM:]
    zq = jnp.dot((xhat * gq_ref[...]).astype(BF16), wq_ref[...], preferred_element_type=F32)
    q = _head_norm_rope(zq, hmean_ref, hgq_ref[...], *tables)
    q_ref[...] = (q * HEAD_DIM ** -0.5).astype(q_ref.dtype)


def moe_combine_kv_q(h, rec, ga, gb, kv_gain, kv_w_bf16, q_gain, wq_bf16, hmean, k_hgain, q_hgain,
                     cos_t, sina_t, sinb_t):
    t, d = h.shape
    row = lambda i: (i, 0)
    fix = lambda i: (0, 0)
    rope = pl.BlockSpec((ROW_TILE, LANES), _rope_tile)
    return pl.pallas_call(
        _combine_kv_q_kernel,
        out_shape=(jax.ShapeDtypeStruct((t, d), F32), jax.ShapeDtypeStruct((t, KV_DIM), F32),
                   jax.ShapeDtypeStruct((t, KV_DIM), F32), jax.ShapeDtypeStruct((t, d), BF16)),
        grid=(t // ROW_TILE,),
        in_specs=[pl.BlockSpec((ROW_TILE, d), row), pl.BlockSpec((ROW_TILE, LANES), row),
                  pl.BlockSpec((ROW_TILE, d // 2), row), pl.BlockSpec((ROW_TILE, d // 2), row),
                  pl.BlockSpec((1, d), fix), pl.BlockSpec((d, 2 * KV_DIM), fix),
                  pl.BlockSpec((1, d), fix), pl.BlockSpec((d, d), fix),
                  pl.BlockSpec((256, 256), fix), pl.BlockSpec((1, KV_DIM), fix), pl.BlockSpec((1, d), fix),
                  rope, rope, rope],
        out_specs=(pl.BlockSpec((ROW_TILE, d), row), pl.BlockSpec((ROW_TILE, KV_DIM), row),
                   pl.BlockSpec((ROW_TILE, KV_DIM), row), pl.BlockSpec((ROW_TILE, d), row)),
        compiler_params=_cparams(("parallel",)),
        name="moe_combine_kv_q",
    )(h, rec, ga, gb, kv_gain.reshape(1, d), kv_w_bf16, q_gain.reshape(1, d), wq_bf16, hmean, k_hgain, q_hgain,
      cos_t, sina_t, sinb_t)


def _ffn_kernel(wblk_ref, we_ref, wlo_ref, whi_ref, xs_ref, w13_ref, w2_ref, ys_ref, w13b, w2b):
    w = pl.program_id(0)
    prev = jnp.maximum(w - 1, 0)
    first_visit = (w == 0) | (wblk_ref[w] != wblk_ref[prev])
    lo = wlo_ref[w]
    hi = whi_ref[w]

    def ffn(x):
        x = _unpack_halves(x).astype(BF16)
        cw = D_EXPERT // FFN_CHUNKS
        gate_up = []
        for c in range(FFN_CHUNKS):
            a = jnp.dot(x, w13b[:, c * cw:(c + 1) * cw], preferred_element_type=F32)
            u = jnp.dot(x, w13b[:, D_EXPERT + c * cw:D_EXPERT + (c + 1) * cw], preferred_element_type=F32)
            gate_up.append((a, u))
        hmid = jnp.concatenate([(_silu(a) * u).astype(BF16) for a, u in gate_up], axis=1)
        return _pack_halves(jnp.dot(hmid, w2b[...], preferred_element_type=F32))

    @pl.when(hi > lo)
    def _():
        @pl.when((w == 0) | (we_ref[w] != we_ref[prev]))
        def _():
            w13b[...] = w13_ref[...].astype(BF16)
            w2b[...] = w2_ref[...].astype(BF16)

        whole = (lo == 0) & (hi == EXPERT_BLOCK)

        @pl.when(whole)
        def _():
            ys_ref[...] = ffn(xs_ref[...])

        half = EXPERT_BLOCK // 2
        for p in range(2):
            rows = slice(p * half, (p + 1) * half)
            touched = (lo < (p + 1) * half) & (hi > p * half)

            @pl.when(jnp.logical_not(whole) & touched)
            def _():
                y = ffn(xs_ref[rows, :])
                row = lax.broadcasted_iota(jnp.int32, y.shape, 0) + p * half
                mine = (row >= lo) & (row < hi)

                @pl.when(first_visit)
                def _():
                    ys_ref[rows, :] = jnp.where(mine, y, jnp.zeros_like(y))

                @pl.when(jnp.logical_not(first_visit))
                def _():
                    ys_ref[rows, :] = jnp.where(mine, y, ys_ref[rows, :])

            @pl.when(jnp.logical_not(whole) & jnp.logical_not(touched) & first_visit)
            def _():
                ys_ref[rows, :] = jnp.zeros((half, ys_ref.shape[1]), U32)


def moe_ffn(xs, work, w13_all, w2_all, layer):
    n_slots, dp = xs.shape
    d = 2 * dp
    n_work = work[0].shape[0]
    xmap = lambda w, wb, we, wlo, whi: (wb[w], 0)
    w_map = lambda w, wb, we, wlo, whi: (layer, we[w], 0, 0)
    grid_spec = pltpu.PrefetchScalarGridSpec(
        num_scalar_prefetch=4,
        grid=(n_work,),
        in_specs=[pl.BlockSpec((EXPERT_BLOCK, dp), xmap),
                  pl.BlockSpec((None, None, d, 2 * D_EXPERT), w_map),
                  pl.BlockSpec((None, None, D_EXPERT, d), w_map)],
        out_specs=pl.BlockSpec((EXPERT_BLOCK, dp), xmap),
        scratch_shapes=[pltpu.VMEM((d, 2 * D_EXPERT), BF16), pltpu.VMEM((D_EXPERT, d), BF16)],
    )
    return pl.pallas_call(
        _ffn_kernel,
        out_shape=jax.ShapeDtypeStruct((n_slots, dp), U32),
        grid_spec=grid_spec,
        compiler_params=_cparams(("arbitrary",)),
        name="moe_ffn",
    )(*work, xs, w13_all, w2_all)


def _ffn_work_items(cnt):
    n_slots = 2 * T_ALL
    n_blocks = n_slots // EXPERT_BLOCK
    n_work = n_blocks + N_EXPERTS - 1
    end = jnp.cumsum(cnt)
    start = end - cnt
    first_blk = start // EXPERT_BLOCK
    last_blk = jnp.maximum(end - 1, start) // EXPERT_BLOCK
    n_items = jnp.where(cnt > 0, last_blk - first_blk + 1, 0)
    item_end = jnp.cumsum(n_items)
    item_start = item_end - n_items
    w = jnp.arange(n_work, dtype=jnp.int32)
    used = w < item_end[-1]
    wq = jnp.minimum(w, item_end[-1] - 1)
    e = jnp.sum((item_end[:, None] <= wq[None, :]).astype(jnp.int32), axis=0)
    onehot = e[None, :] == jnp.arange(N_EXPERTS, dtype=jnp.int32)[:, None]
    of_e = lambda table: jnp.sum(jnp.where(onehot, table[:, None], 0), axis=0)
    blk = jnp.where(used, of_e(first_blk) + (w - of_e(item_start)), n_blocks - 1).astype(jnp.int32)
    lo = jnp.maximum(of_e(start), blk * EXPERT_BLOCK) - blk * EXPERT_BLOCK
    hi = jnp.minimum(of_e(end), (blk + 1) * EXPERT_BLOCK) - blk * EXPERT_BLOCK
    lo = jnp.where(used, lo, 0).astype(jnp.int32)
    hi = jnp.where(used, hi, 0).astype(jnp.int32)
    return start, (blk, e, lo, hi)


def _combine_kernel(dest_ref, h_ref, rec_ref, ys_ref, *rest, split):
    out_refs, (g1, g2, sem) = rest[:-3], rest[-3:]
    rows = h_ref.shape[0]
    i = pl.program_id(0)

    def issue(r, c):
        _row_copy(ys_ref, dest_ref[0, 0, r], g1, r, sem).start(priority=0)
        _row_copy(ys_ref, dest_ref[0, 0, rows + r], g2, r, sem).start(priority=1)
        return c

    lax.fori_loop(0, rows, issue, 0, unroll=8)
    for buf in (g1, g2):
        pltpu.make_async_copy(ys_ref.at[pl.ds(0, rows)], buf, sem).wait()
    rec = rec_ref[...]
    res = h_ref[...] + rec[:, 4:5] * g1[...] + rec[:, 5:6] * g2[...]
    if not split:
        out_refs[0][...] = res
    else:
        @pl.when(i < N_PROMPT_TILES)
        def _():
            out_refs[0][...] = res

        @pl.when((i >= N_PROMPT_TILES) & (i < N_PROMPT_TILES + N_SAMPLE_TILES))
        def _():
            out_refs[1][...] = res


def moe_combine(h, rec, ys, dest3, split=False):
    t, d = h.shape
    row = lambda i: (i, 0)
    if split:
        parts = _parts_specs(d)[:2]
        out_shape = (jax.ShapeDtypeStruct((T_PROMPT, d), F32), jax.ShapeDtypeStruct((T_SAMPLE, d), F32))
        out_specs = tuple(parts)
    else:
        out_shape = jax.ShapeDtypeStruct((t, d), F32)
        out_specs = pl.BlockSpec((ROW_TILE, d), row)
    return pl.pallas_call(
        functools.partial(_combine_kernel, split=split),
        out_shape=out_shape,
        grid=(t // ROW_TILE,),
        in_specs=[pl.BlockSpec((1, 1, 2 * ROW_TILE), lambda i: (i, 0, 0), memory_space=pltpu.SMEM),
                  pl.BlockSpec((ROW_TILE, d), row), pl.BlockSpec((ROW_TILE, LANES), row),
                  pl.BlockSpec(memory_space=pl.ANY)],
        out_specs=out_specs,
        scratch_shapes=[pltpu.VMEM((ROW_TILE, d), F32), pltpu.VMEM((ROW_TILE, d), F32),
                        pltpu.SemaphoreType.DMA],
        compiler_params=_cparams(("arbitrary",)),
        name="moe_combine",
    )(dest3, h, rec, ys)


def hier_moe_layer(a, w_out_bf16, x, layer, gain, w_group, b_group, w_expert, b_expert, w13_all, w2_all, ltri,
                   finish):
    t = T_ALL
    pad = LANES - N_EXPERTS - N_GROUPS
    w_router = jnp.concatenate([w_expert, w_group, jnp.zeros((D_MODEL, pad), F32)], axis=1)
    b_router = jnp.concatenate([b_expert, b_group, jnp.zeros((pad,), F32)]).reshape(1, LANES)
    h, xn, rec, rect, counts = moe_route(a, w_out_bf16, x, gain, w_router, b_router, ltri)

    cnt = counts[0, :N_EXPERTS].astype(jnp.int32)
    start, work = _ffn_work_items(cnt)
    experts = jnp.arange(N_EXPERTS, dtype=jnp.int32)[:, None]

    def slot_of(e_row, rank_row):
        first = jnp.sum(jnp.where(e_row.astype(jnp.int32)[None, :] == experts, start[:, None], 0), axis=0)
        return (first + rank_row.astype(jnp.int32)).reshape(1, t)

    dest_a = slot_of(rect[0], rect[2])
    dest_b = slot_of(rect[1], rect[3])

    xs = moe_dispatch_sc(xn, dest_a, dest_b, 2 * t)
    ys = moe_ffn(xs, work, w13_all, w2_all, layer)
    ga, gb = moe_gather_sc(ys, dest_a, dest_b)
    return finish(h, rec, ga, gb)


def _rope_tables(pos):
    half = ROPE_DIM // 2
    lane = np.arange(LANES) % HEAD_DIM
    rotary = lane < ROPE_DIM
    inv = jnp.where(rotary, jnp.exp(-math.log(ROPE_THETA) * jnp.asarray(lane % half, F32) * (2.0 / ROPE_DIM)), 0.0)
    ang = pos.astype(F32)[:, None] * inv[None, :]
    cos, sin = jnp.cos(ang), jnp.sin(ang)
    first = jnp.asarray(lane < half)
    second = jnp.asarray(rotary & (lane >= half))
    return cos, jnp.where(first, -sin, 0.0), jnp.where(second, sin, 0.0)


def kernel(x_prompt, x_sample, state_hgrn, cache_k_win, cache_v_win, meta_tokens, a_norm, a_w_in, a_lower_logits, a_out_norm, a_w_out, kv_norm, kv_w, k_norm, b_norm, b_wq, b_q_norm, b_sinks, b_w_out, moe_norm, moe_w_group, moe_b_group, moe_w_expert, moe_b_expert, moe_w13, moe_w2):
    tail_rows = T_ALL - OFF_META
    x_parts = (x_prompt.reshape(T_PROMPT, D_MODEL), x_sample.reshape(T_SAMPLE, D_MODEL),
               jnp.concatenate([meta_tokens.astype(F32), jnp.zeros((tail_rows - N_META, D_MODEL), F32)], axis=0))
    pos = jnp.concatenate([N_META + jnp.arange(SEQ, dtype=jnp.int32),
                           jnp.tile(PAST_LEN + jnp.arange(DEC_SEQ, dtype=jnp.int32), ROW_TILE // DEC_SEQ),
                           jnp.arange(N_META, dtype=jnp.int32),
                           jnp.zeros((ROW_TILE - N_META,), jnp.int32)])
    cos_t, sina_t, sinb_t = _rope_tables(pos)
    r256 = np.arange(256)
    hmean = jnp.asarray((r256[:, None] // HEAD_DIM == r256[None, :] // HEAD_DIM).astype(np.float32) / HEAD_DIM, BF16)
    ltri = jnp.asarray((r256[None, :] < r256[:, None]).astype(np.float32), BF16)
    lower = jnp.cumsum(jax.nn.softmax(a_lower_logits.astype(F32), axis=0), axis=0)

    moe = functools.partial(hier_moe_layer, w13_all=moe_w13, w2_all=moe_w2, ltri=ltri)

    z = in_project(x_parts, a_norm[0], a_w_in[0].astype(BF16))
    zero_state = jnp.zeros((1, A_HEADS, A_DK, A_DV), F32)
    o_meta, s_meta = hgrn2_scan(z, zero_state, lower[0], a_out_norm[0],
                                row_off=OFF_META, n_seq=1, seq_len=N_META)
    o_prompt, s_prompt = hgrn2_scan(z, s_meta, lower[0], a_out_norm[0], row_off=0, n_seq=BATCH, seq_len=SEQ)
    o_sample, s_sample = hgrn2_scan(z, state_hgrn[0].astype(F32), lower[0], a_out_norm[0],
                                    row_off=OFF_SAMPLE, n_seq=DEC_BATCH, seq_len=DEC_SEQ, group=SCAN_SAMPLE_GROUP)
    o_tail = jnp.concatenate([o_meta, jnp.zeros((tail_rows - N_META, D_MODEL), BF16)], axis=0)
    finish0 = functools.partial(
        moe_combine_kv_q, kv_gain=kv_norm, kv_w_bf16=kv_w.astype(BF16), q_gain=b_norm[0], wq_bf16=b_wq[0].astype(BF16),
        hmean=hmean, k_hgain=jnp.tile(k_norm, KV_HEADS).reshape(1, KV_DIM),
        q_hgain=jnp.tile(b_q_norm[0], Q_HEADS).reshape(1, D_MODEL), cos_t=cos_t, sina_t=sina_t, sinb_t=sinb_t)
    h, k_all, v_all, q_all = moe((o_prompt, o_sample, o_tail), a_w_out[0].astype(BF16), x_parts, 0, moe_norm[0],
                                 moe_w_group[0], moe_b_group[0], moe_w_expert[0], moe_b_expert[0], finish=finish0)

    meta_blk = lambda a: jnp.concatenate([jnp.zeros((ATT_BLOCK - N_META, KV_DIM), F32),
                                          a[OFF_META:OFF_META + N_META]], axis=0)
    sinks = b_sinks[0].astype(F32)
    att_all = attention_prompt(q_all, k_all, v_all, meta_blk(k_all), meta_blk(v_all), sinks)
    att_all = attention_sample(q_all, cache_k_win.reshape(DEC_BATCH * WINDOW, KV_DIM).astype(F32),
                               cache_v_win.reshape(DEC_BATCH * WINDOW, KV_DIM).astype(F32),
                               k_all, v_all, sinks, att_all)
    y_p, y_s = moe(att_all, b_w_out[0].astype(BF16), h, 1, moe_norm[1], moe_w_group[1], moe_b_group[1],
                   moe_w_expert[1], moe_b_expert[1], finish=functools.partial(moe_combine_dense, split=True))

    y_prompt = y_p.reshape(BATCH, SEQ, D_MODEL)
    y_sample = y_s.reshape(DEC_BATCH, DEC_SEQ, D_MODEL)
    last = lambda a: jnp.stack([a[(b + 1) * SEQ - WINDOW:(b + 1) * SEQ] for b in range(BATCH)]).reshape(
        BATCH, WINDOW, KV_HEADS, HEAD_DIM)
    kp = last(k_all)
    vp = last(v_all)
    ks = k_all[OFF_SAMPLE:OFF_SAMPLE + T_SAMPLE].reshape(DEC_BATCH, DEC_SEQ, KV_HEADS, HEAD_DIM)
    vs = v_all[OFF_SAMPLE:OFF_SAMPLE + T_SAMPLE].reshape(DEC_BATCH, DEC_SEQ, KV_HEADS, HEAD_DIM)
    k_win_s = jnp.concatenate([cache_k_win, ks], axis=1)[:, -WINDOW:]
    v_win_s = jnp.concatenate([cache_v_win, vs], axis=1)[:, -WINDOW:]
    return (y_prompt, y_sample, s_prompt[None], s_sample[None], kp, vp, k_win_s, v_win_s)
```

```python
import functools
import math

import numpy as np
import jax
import jax.numpy as jnp
from jax import lax
from jax.experimental import pallas as pl
from jax.experimental.pallas import tpu as pltpu
from jax.experimental.pallas import tpu_sc as plsc

F32 = jnp.float32
BF16 = jnp.bfloat16
U32 = jnp.uint32

D_MODEL = 1024
BATCH = 4
SEQ = 4096
DEC_BATCH = 128
DEC_SEQ = 8
PAST_LEN = 8192
N_META = 16
A_HEADS = 8
A_DK = 128
A_DV = 128
Q_HEADS = 16
KV_HEADS = 4
HEAD_DIM = 64
KV_DIM = KV_HEADS * HEAD_DIM
WINDOW = 128
ROPE_DIM = 16
ROPE_THETA = 500000.0
N_GROUPS = 4
EXPERTS_PER_GROUP = 8
N_EXPERTS = 32
D_EXPERT = 512
RMS_EPS = 1e-6

LANES = 128
SUBLANES = 8
VMEM_LIMIT = 56 * 1024 * 1024

ROW_TILE = 512
T_PROMPT = BATCH * SEQ
T_SAMPLE = DEC_BATCH * DEC_SEQ
OFF_SAMPLE = T_PROMPT
OFF_META = T_PROMPT + T_SAMPLE
T_REAL = OFF_META + N_META
T_ALL = -(-T_REAL // ROW_TILE) * ROW_TILE
N_TILES = T_ALL // ROW_TILE

SCAN_CHUNK = 128
SCAN_SAMPLE_GROUP = 8
ATT_BLOCK = 128
EXPERT_BLOCK = 512
FFN_CHUNKS = 2


def _cparams(sem):
    return pltpu.CompilerParams(dimension_semantics=sem, vmem_limit_bytes=VMEM_LIMIT)


def _nt_dot(a, b):
    return lax.dot_general(a, b, (((1,), (1,)), ((), ())), preferred_element_type=F32)


def _rms(x, gain):
    ms = jnp.mean(x * x, axis=-1, keepdims=True)
    return x * lax.rsqrt(ms + RMS_EPS) * gain


def _silu(x):
    return x * jax.nn.sigmoid(x)


def _pack_halves(x):
    w = x.shape[1] // 2
    hi = lax.bitcast_convert_type(x[:, :w].astype(BF16).astype(F32), U32)
    lo = lax.bitcast_convert_type(x[:, w:].astype(BF16).astype(F32), U32)
    return hi | (lo >> 16)


def _unpack_halves(p):
    hi = lax.bitcast_convert_type(p & jnp.uint32(0xFFFF0000), F32)
    lo = lax.bitcast_convert_type(p << 16, F32)
    return jnp.concatenate([hi, lo], axis=1)


N_PROMPT_TILES = T_PROMPT // ROW_TILE
N_SAMPLE_TILES = T_SAMPLE // ROW_TILE


def _parts_specs(width):
    return [pl.BlockSpec((ROW_TILE, width), lambda i: (jnp.minimum(i, N_PROMPT_TILES - 1), 0)),
            pl.BlockSpec((ROW_TILE, width), lambda i: (jnp.clip(i - N_PROMPT_TILES, 0, N_SAMPLE_TILES - 1), 0)),
            pl.BlockSpec((ROW_TILE, width), lambda i: (0, 0))]


def _pick_part(i, p_ref, s_ref, t_ref, dtype):
    return jnp.where(i < N_PROMPT_TILES, p_ref[...].astype(dtype),
                     jnp.where(i < N_PROMPT_TILES + N_SAMPLE_TILES, s_ref[...].astype(dtype),
                               t_ref[...].astype(dtype)))


def _in_proj_kernel(xp_ref, xs_ref, xt_ref, g_ref, w_ref, o_ref):
    x = _pick_part(pl.program_id(0), xp_ref, xs_ref, xt_ref, F32)
    xn = _rms(x, g_ref[...])
    o_ref[...] = jnp.dot(xn.astype(BF16), w_ref[...], preferred_element_type=F32)


def in_project(x_parts, gain, w_bf16):
    d, n = w_bf16.shape
    return pl.pallas_call(
        _in_proj_kernel,
        out_shape=jax.ShapeDtypeStruct((T_ALL, n), F32),
        grid=(N_TILES,),
        in_specs=_parts_specs(d) + [pl.BlockSpec((1, d), lambda i: (0, 0)),
                                    pl.BlockSpec((d, n), lambda i: (0, 0))],
        out_specs=pl.BlockSpec((ROW_TILE, n), lambda i: (i, 0)),
        compiler_params=_cparams(("parallel",)),
        name="in_project",
    )(*x_parts, gain.reshape(1, d), w_bf16)


def _mixer_out_kernel(ap_ref, as_ref, at_ref, w_ref, xp_ref, xs_ref, xt_ref, o_ref):
    i = pl.program_id(0)
    a = _pick_part(i, ap_ref, as_ref, at_ref, BF16)
    x = _pick_part(i, xp_ref, xs_ref, xt_ref, F32)
    o_ref[...] = x + jnp.dot(a, w_ref[...], preferred_element_type=F32)


def mixer_out(a_parts, w_bf16, x_parts):
    k, n = w_bf16.shape
    return pl.pallas_call(
        _mixer_out_kernel,
        out_shape=jax.ShapeDtypeStruct((T_ALL, n), F32),
        grid=(N_TILES,),
        in_specs=_parts_specs(k) + [pl.BlockSpec((k, n), lambda i: (0, 0))] + _parts_specs(n),
        out_specs=pl.BlockSpec((ROW_TILE, n), lambda i: (i, 0)),
        compiler_params=_cparams(("parallel",)),
        name="mixer_out",
    )(*a_parts, w_bf16, *x_parts)


def _head_norm_rope(y, hmean_ref, hgain, cos_t, sina_t, sinb_t):
    rows, width = y.shape
    sq = (y * y).astype(BF16)
    parts = []
    for s in range(width // 256):
        parts.append(jnp.dot(sq[:, s * 256:(s + 1) * 256], hmean_ref[...], preferred_element_type=F32))
    ms = parts[0] if len(parts) == 1 else jnp.concatenate(parts, axis=1)
    yn = y * lax.rsqrt(ms + RMS_EPS) * hgain
    reps = width // LANES
    cos_w = jnp.concatenate([cos_t] * reps, axis=1)
    sina_w = jnp.concatenate([sina_t] * reps, axis=1)
    sinb_w = jnp.concatenate([sinb_t] * reps, axis=1)
    half = ROPE_DIM // 2
    nxt = pltpu.roll(yn, width - half, 1)
    prv = pltpu.roll(yn, half, 1)
    return yn * cos_w + nxt * sina_w + prv * sinb_w


def _kv_kernel(x_ref, g_ref, w_ref, hmean_ref, hg_ref, cos_ref, sina_ref, sinb_ref, k_ref, v_ref):
    xn = _rms(x_ref[...], g_ref[...])
    z = jnp.dot(xn.astype(BF16), w_ref[...], preferred_element_type=F32)
    k = _head_norm_rope(z[:, :KV_DIM], hmean_ref, hg_ref[...], cos_ref[...], sina_ref[...], sinb_ref[...])
    k_ref[...] = k
    v_ref[...] = z[:, KV_DIM:]


def _rope_tile(i):
    tiles_per_seq = SEQ // ROW_TILE
    n_prompt_tiles = T_PROMPT // ROW_TILE
    n_sample_tiles = T_SAMPLE // ROW_TILE
    return (jnp.where(i < n_prompt_tiles, i % tiles_per_seq,
                      jnp.where(i < n_prompt_tiles + n_sample_tiles, tiles_per_seq, tiles_per_seq + 1)), 0)


def kv_project(x, gain, w_bf16, hmean, hgain_w, cos_t, sina_t, sinb_t):
    t, d = x.shape
    row = lambda i: (i, 0)
    fix = lambda i: (0, 0)
    return pl.pallas_call(
        _kv_kernel,
        out_shape=(jax.ShapeDtypeStruct((t, KV_DIM), F32), jax.ShapeDtypeStruct((t, KV_DIM), F32)),
        grid=(t // ROW_TILE,),
        in_specs=[pl.BlockSpec((ROW_TILE, d), row), pl.BlockSpec((1, d), fix),
                  pl.BlockSpec((d, 2 * KV_DIM), fix), pl.BlockSpec((256, 256), fix),
                  pl.BlockSpec((1, KV_DIM), fix),
                  pl.BlockSpec((ROW_TILE, LANES), _rope_tile), pl.BlockSpec((ROW_TILE, LANES), _rope_tile),
                  pl.BlockSpec((ROW_TILE, LANES), _rope_tile)],
        out_specs=(pl.BlockSpec((ROW_TILE, KV_DIM), row), pl.BlockSpec((ROW_TILE, KV_DIM), row)),
        compiler_params=_cparams(("parallel",)),
        name="kv_project",
    )(x, gain.reshape(1, d), w_bf16, hmean, hgain_w, cos_t, sina_t, sinb_t)


def _q_kernel(x_ref, g_ref, w_ref, hmean_ref, hg_ref, cos_ref, sina_ref, sinb_ref, q_ref):
    xn = _rms(x_ref[...], g_ref[...])
    z = jnp.dot(xn.astype(BF16), w_ref[...], preferred_element_type=F32)
    q = _head_norm_rope(z, hmean_ref, hg_ref[...], cos_ref[...], sina_ref[...], sinb_ref[...])
    q_ref[...] = (q * HEAD_DIM ** -0.5).astype(q_ref.dtype)


def q_project(x, gain, w_bf16, hmean, hgain_w, cos_t, sina_t, sinb_t):
    t, d = x.shape
    row = lambda i: (i, 0)
    fix = lambda i: (0, 0)
    return pl.pallas_call(
        _q_kernel,
        out_shape=jax.ShapeDtypeStruct((t, d), BF16),
        grid=(t // ROW_TILE,),
        in_specs=[pl.BlockSpec((ROW_TILE, d), row), pl.BlockSpec((1, d), fix),
                  pl.BlockSpec((d, d), fix), pl.BlockSpec((256, 256), fix),
                  pl.BlockSpec((1, d), fix),
                  pl.BlockSpec((ROW_TILE, LANES), _rope_tile), pl.BlockSpec((ROW_TILE, LANES), _rope_tile),
                  pl.BlockSpec((ROW_TILE, LANES), _rope_tile)],
        out_specs=pl.BlockSpec((ROW_TILE, d), row),
        compiler_params=_cparams(("parallel",)),
        name="q_project",
    )(x, gain.reshape(1, d), w_bf16, hmean, hgain_w, cos_t, sina_t, sinb_t)


def _matmul_residual_kernel(a_ref, w_ref, r_ref, o_ref):
    o_ref[...] = r_ref[...] + jnp.dot(a_ref[...], w_ref[...], preferred_element_type=F32)


def matmul_residual(a_bf16, w_bf16, resid):
    t, k = a_bf16.shape
    n = w_bf16.shape[1]
    return pl.pallas_call(
        _matmul_residual_kernel,
        out_shape=jax.ShapeDtypeStruct((t, n), F32),
        grid=(t // ROW_TILE,),
        in_specs=[pl.BlockSpec((ROW_TILE, k), lambda i: (i, 0)),
                  pl.BlockSpec((k, n), lambda i: (0, 0)),
                  pl.BlockSpec((ROW_TILE, n), lambda i: (i, 0))],
        out_specs=pl.BlockSpec((ROW_TILE, n), lambda i: (i, 0)),
        compiler_params=_cparams(("parallel",)),
        name="matmul_residual",
    )(a_bf16, w_bf16, resid)


def _scan_levels(c):
    levels = []
    m = c
    while m >= 2:
        levels.append(m)
        m //= 2
    return levels


LOG2E = 1.4426950408889634


def _scan_kernel(z_ref, s0_ref, lb_ref, og_ref, tri_ref, lmask_ref, sgn_ref, o_ref, sfin_ref, s_scr, b_scr,
                 *, rows, seq_len):
    c_idx = pl.program_id(1)
    levels = _scan_levels(seq_len)
    n_sub = rows // seq_len
    hk = A_HEADS * A_DK

    @pl.when(c_idx == 0)
    def _():
        s_scr[...] = s0_ref[...]

    sub = lax.broadcasted_iota(jnp.int32, (SUBLANES, LANES), 0)
    row = lax.broadcasted_iota(jnp.int32, (LANES, LANES), 0)
    og = og_ref[...]

    def pad_f32(x):
        if x.shape[0] == LANES:
            return x
        return jnp.concatenate([x, jnp.zeros((LANES - x.shape[0], x.shape[1]), x.dtype)], axis=0)

    def pad_rows(x):
        return pad_f32(x).astype(BF16)

    def cols(part, h):
        return slice(part * hk + h * LANES, part * hk + (h + 1) * LANES)

    def gates(h):
        lb = lb_ref[:, cols(0, h)]
        forget = lb + (1.0 - lb) * jax.nn.sigmoid(z_ref[:, cols(1, h)])
        logf = jnp.log(forget)
        hi = logf.astype(BF16).astype(F32)
        r1 = logf - hi
        mid = r1.astype(BF16).astype(F32)
        lo = r1 - mid
        cs = jnp.dot(tri_ref[...], pad_rows(jnp.concatenate([hi, mid, lo], axis=1)),
                     preferred_element_type=F32)
        b = (cs[:rows, :LANES] + cs[:rows, LANES:2 * LANES]) + cs[:rows, 2 * LANES:]
        b_scr[h] = b
        return _silu(z_ref[:, cols(0, h)]), 1.0 - forget, b

    def bref_for(h, m):
        b_rows = b_scr.at[h]
        half = m // 2
        pieces = []
        for g in range(rows // SUBLANES):
            base = g * SUBLANES
            if m >= SUBLANES:
                r = (base // m) * m + half - 1
                piece = jnp.broadcast_to(b_rows[r:r + 1, :], (SUBLANES, LANES))
            else:
                piece = jnp.broadcast_to(b_rows[base + half - 1:base + half, :], (SUBLANES, LANES))
                for blk in range(1, SUBLANES // m):
                    r = base + blk * m + half - 1
                    piece = jnp.where(sub >= blk * m,
                                      jnp.broadcast_to(b_rows[r:r + 1, :], (SUBLANES, LANES)), piece)
            pieces.append(piece)
        return pieces[0] if len(pieces) == 1 else jnp.concatenate(pieces, axis=0)

    heads = range(A_HEADS)
    qkb = [gates(h) for h in heads]
    att = [_nt_dot(pad_rows(qf), pad_rows(kf)) * lmask_ref[len(levels)] for qf, kf, _ in qkb]
    for li, m in enumerate(levels):
        for h in heads:
            qf, kf, b = qkb[h]
            sgn = sgn_ref[li]
            e = jnp.exp2((b - bref_for(h, m)) * sgn)
            w = pad_rows(jnp.where(sgn > 0, qf, kf) * e)
            att[h] = att[h] + _nt_dot(w, w) * lmask_ref[li]

    def finish(h):
        qf, kf, b = qkb[h]
        b_rows = b_scr.at[h]
        v_b = pad_rows(z_ref[:, cols(2, h)])
        o_intra = jnp.dot(att[h].astype(BF16), v_b, preferred_element_type=F32)
        eb = jnp.exp(b)
        qs = qf * eb
        b_end = [jnp.broadcast_to(b_rows[(i + 1) * seq_len - 1:(i + 1) * seq_len, :], (seq_len, LANES))
                 for i in range(n_sub)]
        b_end = b_end[0] if n_sub == 1 else jnp.concatenate(b_end, axis=0)
        kd_t = pad_f32(kf * jnp.exp(b_end - b)).T.astype(BF16)
        eb_t = pad_f32(eb).T
        qs_b = pad_rows(qs)
        o = o_intra
        for i in range(n_sub):
            s_old = s_scr[i, h]
            first, last = i * seq_len, (i + 1) * seq_len - 1
            if n_sub == 1:
                qs_i, v_i = qs_b, v_b
            else:
                mine = (row >= first) & (row <= last)
                qs_i = jnp.where(mine, qs_b, jnp.zeros_like(qs_b))
                v_i = jnp.where(mine, v_b, jnp.zeros_like(v_b))
            o = o + jnp.dot(qs_i, s_old.astype(BF16), preferred_element_type=F32)
            decay = jnp.broadcast_to(eb_t[:, last:last + 1], (LANES, LANES))
            s_scr[i, h] = decay * s_old + jnp.dot(kd_t, v_i, preferred_element_type=F32)
        o = o[:rows]

        on = _rms(o, og) * _silu(z_ref[:, cols(3, h)])
        o_ref[:, cols(0, h)] = on.astype(o_ref.dtype)

    for h in heads:
        finish(h)

    @pl.when(c_idx == pl.num_programs(1) - 1)
    def _():
        sfin_ref[...] = s_scr[...]


def _scan_consts(rows, seq_len):
    levels = _scan_levels(seq_len)
    r = np.arange(LANES)
    t, s = r[:, None], r[None, :]
    live = (t < rows) & (s < rows)
    tri = ((s <= t) & (t // seq_len == s // seq_len) & live).astype(np.float32)
    masks, sgns = [], []
    for m in levels:
        masks.append(((t // m == s // m) & (t % m >= m // 2) & (s % m < m // 2) & live).astype(np.float32))
        sgns.append(np.broadcast_to(np.where(r[:rows, None] % m >= m // 2, LOG2E, -LOG2E), (rows, LANES)))
    masks.append(((t == s) & live).astype(np.float32))
    return jnp.asarray(tri, BF16), jnp.asarray(np.stack(masks), F32), jnp.asarray(np.stack(sgns), F32)


def hgrn2_scan(z, s0, lb, o_gain, *, row_off, n_seq, seq_len, group=1):
    hv = A_HEADS * A_DV
    if seq_len > SCAN_CHUNK:
        assert group == 1
        sub_len, rows, n_chunks, n_steps = SCAN_CHUNK, SCAN_CHUNK, seq_len // SCAN_CHUNK, n_seq
    else:
        sub_len, rows, n_chunks, n_steps = seq_len, group * seq_len, 1, n_seq // group
    blk_off = row_off // rows
    tri, lmask, sgn = _scan_consts(rows, sub_len)
    shared_s0 = s0.shape[0] == 1
    fix2 = lambda s, c: (0, 0)
    fix3 = lambda s, c: (0, 0, 0)
    o, sfin = pl.pallas_call(
        functools.partial(_scan_kernel, rows=rows, seq_len=sub_len),
        out_shape=(jax.ShapeDtypeStruct((n_seq * seq_len, hv), BF16 if rows % 16 == 0 else F32),
                   jax.ShapeDtypeStruct((n_seq, A_HEADS, A_DK, A_DV), F32)),
        grid=(n_steps, n_chunks),
        in_specs=[pl.BlockSpec((rows, 4 * hv), lambda s, c: (blk_off + s * n_chunks + c, 0)),
                  pl.BlockSpec((group, A_HEADS, A_DK, A_DV), (lambda s, c: (0, 0, 0, 0)) if shared_s0
                               else (lambda s, c: (s, 0, 0, 0))),
                  pl.BlockSpec((1, hv), fix2), pl.BlockSpec((1, A_DV), fix2),
                  pl.BlockSpec((LANES, LANES), fix2), pl.BlockSpec(lmask.shape, fix3),
                  pl.BlockSpec(sgn.shape, fix3)],
        out_specs=(pl.BlockSpec((rows, hv), lambda s, c: (s * n_chunks + c, 0)),
                   pl.BlockSpec((group, A_HEADS, A_DK, A_DV), lambda s, c: (s, 0, 0, 0))),
        scratch_shapes=[pltpu.VMEM((group, A_HEADS, A_DK, A_DV), F32), pltpu.VMEM((A_HEADS, rows, LANES), F32)],
        compiler_params=_cparams(("parallel", "arbitrary")),
        name=f"hgrn2_scan_r{rows}",
    )(z, s0, lb.reshape(1, hv), o_gain.reshape(1, A_DV), tri, lmask, sgn)
    return o, sfin


KEYS = 2 * ATT_BLOCK


def _pair_operand(x, kh):
    slab = x[:, (kh // 2) * LANES:(kh // 2 + 1) * LANES]
    lane = lax.broadcasted_iota(jnp.int32, slab.shape, 1)
    if kh % 2 == 0:
        lo = jnp.where(lane < HEAD_DIM, slab, 0.0)
        hi = pltpu.roll(lo, HEAD_DIM, 1)
    else:
        hi = jnp.where(lane >= HEAD_DIM, slab, 0.0)
        lo = pltpu.roll(hi, HEAD_DIM, 1)
    return jnp.concatenate([lo, hi], axis=0).astype(BF16)


def _window_bias(rows, jmin):
    t_i = lax.broadcasted_iota(jnp.int32, (rows, 2 * KEYS), 0)
    c_i = lax.broadcasted_iota(jnp.int32, (rows, 2 * KEYS), 1)
    j_i = c_i & (ATT_BLOCK - 1)
    own = (c_i & ATT_BLOCK) != 0
    ok = (own & (j_i <= t_i)) | (jnp.logical_not(own) & (j_i >= t_i) & (j_i >= jmin))
    return jnp.where(ok, 0.0, -jnp.inf).astype(F32)


def _pair_softmax(s, sink_a, sink_b):
    probs, rinv = [], []
    for hh, sink in enumerate((sink_a, sink_b)):
        sh = s[:, hh * KEYS:(hh + 1) * KEYS]
        m = jnp.maximum(jnp.max(sh, axis=-1, keepdims=True), sink)
        p = jnp.exp(sh - m)
        den = jnp.sum(p, axis=-1, keepdims=True) + jnp.exp(sink - m)
        probs.append(p.astype(BF16))
        rinv.append(1.0 / den)
    lane = lax.broadcasted_iota(jnp.int32, (s.shape[0], LANES), 1)
    return jnp.concatenate(probs, axis=1), jnp.where(lane < HEAD_DIM, rinv[0], rinv[1])


def _attn_prompt_kernel(sink_ref, q_ref, kp_ref, ko_ref, vp_ref, vo_ref, km_ref, vm_ref, o_ref,
                        k2_scr, v2_scr, s_scr, p_scr, r_scr):
    n = pl.program_id(0)
    nbp = SEQ // ATT_BLOCK
    n_pairs = Q_HEADS // 2

    @pl.when(n >= BATCH * nbp)
    def _():
        o_ref[...] = jnp.zeros_like(o_ref)

    @pl.when(n < BATCH * nbp)
    def _():
        first = (n % nbp) == 0
        jmin = jnp.where(first, ATT_BLOCK - N_META, 0)
        k = jnp.concatenate([jnp.where(first, km_ref[...], kp_ref[...]), ko_ref[...]], axis=0)
        v = jnp.concatenate([jnp.where(first, vm_ref[...], vp_ref[...]), vo_ref[...]], axis=0)
        bias = _window_bias(ATT_BLOCK, jmin)
        for kh in range(KV_HEADS):
            k2_scr[kh] = _pair_operand(k, kh)
            v2_scr[kh] = _pair_operand(v, kh)
        for pair in range(n_pairs):
            s_scr[pair] = _nt_dot(q_ref[:, pair * LANES:(pair + 1) * LANES], k2_scr[pair // 2]) + bias
        for pair in range(n_pairs):
            p, rinv = _pair_softmax(s_scr[pair], sink_ref[2 * pair], sink_ref[2 * pair + 1])
            p_scr[pair] = p
            r_scr[pair] = rinv
        for pair in range(n_pairs):
            o = jnp.dot(p_scr[pair], v2_scr[pair // 2], preferred_element_type=F32) * r_scr[pair]
            o_ref[:, pair * LANES:(pair + 1) * LANES] = o.astype(o_ref.dtype)


def attention_prompt(q_all, k_all, v_all, k_meta_blk, v_meta_blk, sinks):
    n_prompt_blocks = T_PROMPT // ATT_BLOCK
    n_blocks = T_ALL // ATT_BLOCK
    n_pairs = Q_HEADS // 2
    own = lambda n, sk: (jnp.minimum(n, n_prompt_blocks - 1), 0)
    prev = lambda n, sk: (jnp.maximum(jnp.minimum(n, n_prompt_blocks - 1) - 1, 0), 0)
    fix = lambda n, sk: (0, 0)
    grid_spec = pltpu.PrefetchScalarGridSpec(
        num_scalar_prefetch=1,
        grid=(n_blocks,),
        in_specs=[pl.BlockSpec((ATT_BLOCK, D_MODEL), own),
                  pl.BlockSpec((ATT_BLOCK, KV_DIM), prev), pl.BlockSpec((ATT_BLOCK, KV_DIM), own),
                  pl.BlockSpec((ATT_BLOCK, KV_DIM), prev), pl.BlockSpec((ATT_BLOCK, KV_DIM), own),
                  pl.BlockSpec((ATT_BLOCK, KV_DIM), fix), pl.BlockSpec((ATT_BLOCK, KV_DIM), fix)],
        out_specs=pl.BlockSpec((ATT_BLOCK, D_MODEL), lambda n, sk: (n, 0)),
        scratch_shapes=[pltpu.VMEM((KV_HEADS, 2 * KEYS, LANES), BF16), pltpu.VMEM((KV_HEADS, 2 * KEYS, LANES), BF16),
                        pltpu.VMEM((n_pairs, ATT_BLOCK, 2 * KEYS), F32),
                        pltpu.VMEM((n_pairs, ATT_BLOCK, 2 * KEYS), BF16),
                        pltpu.VMEM((n_pairs, ATT_BLOCK, LANES), F32)],
    )
    return pl.pallas_call(
        _attn_prompt_kernel,
        out_shape=jax.ShapeDtypeStruct((T_ALL, D_MODEL), BF16),
        grid_spec=grid_spec,
        compiler_params=_cparams(("parallel",)),
        name="attention_prompt",
    )(sinks, q_all, k_all, k_all, v_all, v_all, k_meta_blk, v_meta_blk)


SAMPLE_GROUP = ATT_BLOCK // DEC_SEQ


def _attn_sample_kernel(sink_ref, q_ref, ck_ref, cv_ref, kn_ref, vn_ref, buf_ref, o_ref,
                        qf_scr, of_scr, k2_scr, v2_scr):
    del buf_ref
    qrows = 2 * DEC_SEQ
    qf_scr[...] = q_ref[...].astype(F32)
    bias = _window_bias(qrows, 0)
    zq = jnp.zeros((qrows - DEC_SEQ, D_MODEL), F32)
    zk = jnp.zeros((ATT_BLOCK - DEC_SEQ, KV_DIM), F32)

    def seq_body(i, carry):
        r_new = pl.multiple_of(i * DEC_SEQ, DEC_SEQ)
        r_old = pl.multiple_of(i * WINDOW, WINDOW)
        q = jnp.concatenate([qf_scr[pl.ds(r_new, DEC_SEQ), :], zq], axis=0).astype(BF16)
        k = jnp.concatenate([ck_ref[pl.ds(r_old, WINDOW), :], kn_ref[pl.ds(r_new, DEC_SEQ), :], zk], axis=0)
        v = jnp.concatenate([cv_ref[pl.ds(r_old, WINDOW), :], vn_ref[pl.ds(r_new, DEC_SEQ), :], zk], axis=0)
        for kh in range(KV_HEADS):
            k2_scr[kh] = _pair_operand(k, kh)
            v2_scr[kh] = _pair_operand(v, kh)
        scores = [_nt_dot(q[:, pair * LANES:(pair + 1) * LANES], k2_scr[pair // 2]) + bias
                  for pair in range(Q_HEADS // 2)]
        soft = [_pair_softmax(s, sink_ref[2 * pair], sink_ref[2 * pair + 1]) for pair, s in enumerate(scores)]
        for pair, (p, rinv) in enumerate(soft):
            o = jnp.dot(p, v2_scr[pair // 2], preferred_element_type=F32) * rinv
            of_scr[pl.ds(r_new, DEC_SEQ), pair * LANES:(pair + 1) * LANES] = o[:DEC_SEQ]
        return carry

    lax.fori_loop(0, SAMPLE_GROUP, seq_body, 0)
    o_ref[...] = of_scr[...].astype(o_ref.dtype)


def attention_sample(q_all, cache_k, cache_v, k_all, v_all, sinks, out_buf):
    first_blk = OFF_SAMPLE // ATT_BLOCK
    new = lambda g, sk: (first_blk + g, 0)
    old = lambda g, sk: (g, 0)
    grid_spec = pltpu.PrefetchScalarGridSpec(
        num_scalar_prefetch=1,
        grid=(DEC_BATCH // SAMPLE_GROUP,),
        in_specs=[pl.BlockSpec((ATT_BLOCK, D_MODEL), new),
                  pl.BlockSpec((SAMPLE_GROUP * WINDOW, KV_DIM), old),
                  pl.BlockSpec((SAMPLE_GROUP * WINDOW, KV_DIM), old),
                  pl.BlockSpec((ATT_BLOCK, KV_DIM), new), pl.BlockSpec((ATT_BLOCK, KV_DIM), new),
                  pl.BlockSpec(memory_space=pl.ANY)],
        out_specs=pl.BlockSpec((ATT_BLOCK, D_MODEL), new),
        scratch_shapes=[pltpu.VMEM((ATT_BLOCK, D_MODEL), F32), pltpu.VMEM((ATT_BLOCK, D_MODEL), F32),
                        pltpu.VMEM((KV_HEADS, 2 * KEYS, LANES), BF16), pltpu.VMEM((KV_HEADS, 2 * KEYS, LANES), BF16)],
    )
    return pl.pallas_call(
        _attn_sample_kernel,
        out_shape=jax.ShapeDtypeStruct(out_buf.shape, out_buf.dtype),
        grid_spec=grid_spec,
        input_output_aliases={6: 0},
        compiler_params=_cparams(("parallel",)),
        name="attention_sample",
    )(sinks, q_all, cache_k, cache_v, k_all, v_all, out_buf)


ROUTE_COLS = 8
ROUTE_ROWS = 48


def _route_kernel(*refs, parts):
    i = pl.program_id(0)
    if parts:
        (ap_ref, as_ref, at_ref, w_ref, xp_ref, xs_ref, xt_ref), refs = refs[:7], refs[7:]
        a = _pick_part(i, ap_ref, as_ref, at_ref, BF16)
        x = _pick_part(i, xp_ref, xs_ref, xt_ref, F32)
    else:
        (a_ref, w_ref, x_ref), refs = refs[:3], refs[3:]
        a, x = a_ref[...], x_ref[...]
    g_ref, wh_ref, wl_ref, br_ref, utri_ref, h_ref, xn_ref, rec_ref, rect_ref, cnt_ref, cnt_scr = refs

    @pl.when(i == 0)
    def _():
        cnt_scr[...] = jnp.zeros_like(cnt_scr)

    h = x + jnp.dot(a, w_ref[...], preferred_element_type=F32)
    h_ref[...] = h
    xn = _rms(h, g_ref[...])
    xn_ref[...] = _pack_halves(xn)
    xh = xn.astype(BF16)
    xl = (xn - xh.astype(F32)).astype(BF16)
    logits = (_nt_dot(wh_ref[...], xh) + (_nt_dot(wl_ref[...], xh) + _nt_dot(wh_ref[...], xl)))[:ROUTE_ROWS]
    logits = logits + br_ref[...]
    tokens = logits.shape[1]
    rid = lax.broadcasted_iota(jnp.int32, (ROUTE_ROWS, tokens), 0).astype(F32)
    neg = jnp.float32(-jnp.inf)
    big = jnp.float32(ROUTE_ROWS)

    is_g = (rid >= N_EXPERTS) & (rid < N_EXPERTS + N_GROUPS)
    gl = jnp.where(is_g, logits, neg)
    gmax = jnp.max(gl, axis=0, keepdims=True)
    gsel = jnp.min(jnp.where(gl == gmax, rid, big), axis=0, keepdims=True) - N_EXPERTS
    gden = jnp.sum(jnp.where(is_g, jnp.exp(gl - gmax), 0.0), axis=0, keepdims=True)
    gw = 1.0 / gden

    in_grp = (rid >= gsel * EXPERTS_PER_GROUP) & (rid < (gsel + 1) * EXPERTS_PER_GROUP)
    el = jnp.where(in_grp, logits, neg)
    t1 = jnp.max(el, axis=0, keepdims=True)
    e1 = jnp.min(jnp.where(el == t1, rid, big), axis=0, keepdims=True)
    el2 = jnp.where(rid == e1, neg, el)
    t2 = jnp.max(el2, axis=0, keepdims=True)
    e2 = jnp.min(jnp.where(el2 == t2, rid, big), axis=0, keepdims=True)
    x2 = jnp.exp(t2 - t1)
    w1 = gw / (1.0 + x2)
    w2 = gw * x2 / (1.0 + x2)

    oh1 = (rid == e1).astype(F32)
    oh2 = (rid == e2).astype(F32)
    oh = oh1 + oh2
    before = jnp.dot(oh.astype(BF16), utri_ref[...], preferred_element_type=F32)
    base = cnt_scr[...] + before
    r1 = jnp.sum(base * oh1, axis=0, keepdims=True)
    r2 = jnp.sum(base * oh2, axis=0, keepdims=True)
    cnt_scr[...] = cnt_scr[...] + jnp.sum(oh, axis=1, keepdims=True)

    zero = jnp.zeros_like(w1)
    rect = jnp.concatenate([e1, e2, r1, r2, w1, w2, zero, zero], axis=0)
    rect_ref[...] = rect
    wide = jnp.concatenate([rect, jnp.zeros((LANES - ROUTE_COLS, tokens), F32)], axis=0)
    rec_ref[...] = jnp.concatenate([wide[:, t0:t0 + LANES].T for t0 in range(0, tokens, LANES)], axis=0)
    cnt_ref[...] = cnt_scr[...]


def moe_route(a, w_out_bf16, x, gain, w_router, b_router, utri):
    parts = isinstance(a, tuple)
    t, d = T_ALL, D_MODEL
    row = lambda i: (i, 0)
    fix = lambda i: (0, 0)
    w_t = w_router.T
    w_hi = w_t.astype(BF16)
    w_lo = (w_t - w_hi.astype(F32)).astype(BF16)
    w_spec = pl.BlockSpec((d, d), fix)
    if parts:
        pre_specs = _parts_specs(d) + [w_spec] + _parts_specs(d)
        pre_args = (*a, w_out_bf16, *x)
    else:
        pre_specs = [pl.BlockSpec((ROW_TILE, d), row), w_spec, pl.BlockSpec((ROW_TILE, d), row)]
        pre_args = (a, w_out_bf16, x)
    return pl.pallas_call(
        functools.partial(_route_kernel, parts=parts),
        out_shape=(jax.ShapeDtypeStruct((t, d), F32),
                   jax.ShapeDtypeStruct((t, d // 2), U32), jax.ShapeDtypeStruct((t, LANES), F32),
                   jax.ShapeDtypeStruct((ROUTE_COLS, t), F32), jax.ShapeDtypeStruct((ROUTE_ROWS, 1), F32)),
        grid=(t // ROW_TILE,),
        in_specs=pre_specs + [pl.BlockSpec((1, d), fix),
                              pl.BlockSpec((LANES, d), fix), pl.BlockSpec((LANES, d), fix),
                              pl.BlockSpec((ROUTE_ROWS, 1), fix), pl.BlockSpec((ROW_TILE, ROW_TILE), fix)],
        out_specs=(pl.BlockSpec((ROW_TILE, d), row),
                   pl.BlockSpec((ROW_TILE, d // 2), row), pl.BlockSpec((ROW_TILE, LANES), row),
                   pl.BlockSpec((ROUTE_COLS, ROW_TILE), lambda i: (0, i)), pl.BlockSpec((ROUTE_ROWS, 1), fix)),
        scratch_shapes=[pltpu.VMEM((ROUTE_ROWS, 1), F32)],
        compiler_params=_cparams(("arbitrary",)),
        name="moe_route",
    )(*pre_args, gain.reshape(1, d), w_hi, w_lo, b_router, utri)


def _row_copy(src, src_row, dst, dst_row, sem):
    return pltpu.make_async_copy(src.at[pl.ds(src_row, 1)], dst.at[pl.ds(dst_row, 1)], sem)


def _dispatch_kernel(dest_ref, xn_ref, xs_ref, sem):
    rows = xn_ref.shape[0]

    def issue(r, c):
        _row_copy(xn_ref, r, xs_ref, dest_ref[0, 0, r], sem).start(priority=0)
        _row_copy(xn_ref, r, xs_ref, dest_ref[0, 0, rows + r], sem).start(priority=1)
        return c

    lax.fori_loop(0, rows, issue, 0, unroll=8)
    for _ in range(2):
        pltpu.make_async_copy(xn_ref, xs_ref.at[pl.ds(0, rows)], sem).wait()


def moe_dispatch(xn, dest3, n_slots):
    t, d = xn.shape
    return pl.pallas_call(
        _dispatch_kernel,
        out_shape=jax.ShapeDtypeStruct((n_slots, d), F32),
        grid=(t // ROW_TILE,),
        in_specs=[pl.BlockSpec((1, 1, 2 * ROW_TILE), lambda i: (i, 0, 0), memory_space=pltpu.SMEM),
                  pl.BlockSpec((ROW_TILE, d), lambda i: (i, 0))],
        out_specs=pl.BlockSpec(memory_space=pl.ANY),
        scratch_shapes=[pltpu.SemaphoreType.DMA],
        compiler_params=_cparams(("arbitrary",)),
        name="moe_dispatch",
    )(dest3, xn)


SC_WINDOW = 64
SC_INDEX_WINDOW = 128


def _sc_mesh():
    return plsc.VectorSubcoreMesh(core_axis_name="core", subcore_axis_name="subcore")


def moe_dispatch_sc(xn, dest_a, dest_b, n_slots):
    t, d = xn.shape

    @pl.kernel(out_type=jax.ShapeDtypeStruct((n_slots, d), xn.dtype), mesh=_sc_mesh(),
               scratch_types=[pltpu.VMEM((SC_WINDOW, d), xn.dtype)], name="moe_dispatch_sc")
    def run(x_hbm, id_hbm, da_hbm, db_hbm, o_hbm, buf):
        def body(id_vmem, da_vmem, db_vmem):
            for j in range(SC_INDEX_WINDOW // SC_WINDOW):
                part = pl.ds(j * SC_WINDOW, SC_WINDOW)
                pltpu.sync_copy(x_hbm.at[id_vmem.at[0, part]], buf)
                pltpu.sync_copy(buf, o_hbm.at[da_vmem.at[0, part]])
                pltpu.sync_copy(buf, o_hbm.at[db_vmem.at[0, part]])

        idx_spec = pl.BlockSpec((1, SC_INDEX_WINDOW), lambda i: (0, i))
        pltpu.emit_pipeline(
            body,
            grid=(t // SC_INDEX_WINDOW,),
            in_specs=[idx_spec, idx_spec, idx_spec],
            out_specs=[],
            core_axis_name=("core", "subcore"),
            dimension_semantics=(pltpu.PARALLEL,),
        )(id_hbm, da_hbm, db_hbm)

    return run(xn, jnp.arange(t, dtype=jnp.int32).reshape(1, t), dest_a, dest_b)


def moe_gather_sc(ys, dest_a, dest_b):
    d = ys.shape[1]
    t = dest_a.shape[1]
    out = jax.ShapeDtypeStruct((t, d), ys.dtype)

    @pl.kernel(out_type=(out, out), mesh=_sc_mesh(), scratch_types=[pltpu.VMEM((SC_WINDOW, d), ys.dtype)],
               name="moe_gather_sc")
    def run(y_hbm, id_hbm, da_hbm, db_hbm, ga_hbm, gb_hbm, buf):
        def body(id_vmem, da_vmem, db_vmem):
            for j in range(SC_INDEX_WINDOW // SC_WINDOW):
                part = pl.ds(j * SC_WINDOW, SC_WINDOW)
                pltpu.sync_copy(y_hbm.at[da_vmem.at[0, part]], buf)
                pltpu.sync_copy(buf, ga_hbm.at[id_vmem.at[0, part]])
                pltpu.sync_copy(y_hbm.at[db_vmem.at[0, part]], buf)
                pltpu.sync_copy(buf, gb_hbm.at[id_vmem.at[0, part]])

        idx_spec = pl.BlockSpec((1, SC_INDEX_WINDOW), lambda i: (0, i))
        pltpu.emit_pipeline(
            body,
            grid=(t // SC_INDEX_WINDOW,),
            in_specs=[idx_spec, idx_spec, idx_spec],
            out_specs=[],
            core_axis_name=("core", "subcore"),
            dimension_semantics=(pltpu.PARALLEL,),
        )(id_hbm, da_hbm, db_hbm)

    return run(ys, jnp.arange(t, dtype=jnp.int32).reshape(1, t), dest_a, dest_b)


def _combine_dense_kernel(h_ref, rec_ref, ga_ref, gb_ref, *out_refs, split):
    i = pl.program_id(0)
    rec = rec_ref[...]
    res = h_ref[...] + rec[:, 4:5] * _unpack_halves(ga_ref[...]) + rec[:, 5:6] * _unpack_halves(gb_ref[...])
    if not split:
        out_refs[0][...] = res
    else:
        @pl.when(i < N_PROMPT_TILES)
        def _():
            out_refs[0][...] = res

        @pl.when((i >= N_PROMPT_TILES) & (i < N_PROMPT_TILES + N_SAMPLE_TILES))
        def _():
            out_refs[1][...] = res


def moe_combine_dense(h, rec, ga, gb, split=False):
    t, d = h.shape
    row = lambda i: (i, 0)
    if split:
        out_shape = (jax.ShapeDtypeStruct((T_PROMPT, d), F32), jax.ShapeDtypeStruct((T_SAMPLE, d), F32))
        out_specs = tuple(_parts_specs(d)[:2])
    else:
        out_shape = jax.ShapeDtypeStruct((t, d), F32)
        out_specs = pl.BlockSpec((ROW_TILE, d), row)
    return pl.pallas_call(
        functools.partial(_combine_dense_kernel, split=split),
        out_shape=out_shape,
        grid=(t // ROW_TILE,),
        in_specs=[pl.BlockSpec((ROW_TILE, d), row), pl.BlockSpec((ROW_TILE, LANES), row),
                  pl.BlockSpec((ROW_TILE, d // 2), row), pl.BlockSpec((ROW_TILE, d // 2), row)],
        out_specs=out_specs,
        compiler_params=_cparams(("arbitrary",)),
        name="moe_combine_dense",
    )(h, rec, ga, gb)


def _combine_kv_q_kernel(h_ref, rec_ref, ga_ref, gb_ref, gkv_ref, wkv_ref, gq_ref, wq_ref, hmean_ref,
                         hgk_ref, hgq_ref, cos_ref, sina_ref, sinb_ref, ho_ref, k_ref, v_ref, q_ref):
    rec = rec_ref[...]
    h = h_ref[...] + rec[:, 4:5] * _unpack_halves(ga_ref[...]) + rec[:, 5:6] * _unpack_halves(gb_ref[...])
    ho_ref[...] = h
    xhat = h * lax.rsqrt(jnp.mean(h * h, axis=-1, keepdims=True) + RMS_EPS)
    tables = (cos_ref[...], sina_ref[...], sinb_ref[...])
    zkv = jnp.dot((xhat * gkv_ref[...]).astype(BF16), wkv_ref[...], preferred_element_type=F32)
    k_ref[...] = _head_norm_rope(zkv[:, :KV_DIM], hmean_ref, hgk_ref[...], *tables)
    v_ref[...] = zkv[:, KV_DIM:]
    zq = jnp.dot((xhat * gq_ref[...]).astype(BF16), wq_ref[...], preferred_element_type=F32)
    q = _head_norm_rope(zq, hmean_ref, hgq_ref[...], *tables)
    q_ref[...] = (q * HEAD_DIM ** -0.5).astype(q_ref.dtype)


def moe_combine_kv_q(h, rec, ga, gb, kv_gain, kv_w_bf16, q_gain, wq_bf16, hmean, k_hgain, q_hgain,
                     cos_t, sina_t, sinb_t):
    t, d = h.shape
    row = lambda i: (i, 0)
    fix = lambda i: (0, 0)
    rope = pl.BlockSpec((ROW_TILE, LANES), _rope_tile)
    return pl.pallas_call(
        _combine_kv_q_kernel,
        out_shape=(jax.ShapeDtypeStruct((t, d), F32), jax.ShapeDtypeStruct((t, KV_DIM), F32),
                   jax.ShapeDtypeStruct((t, KV_DIM), F32), jax.ShapeDtypeStruct((t, d), BF16)),
        grid=(t // ROW_TILE,),
        in_specs=[pl.BlockSpec((ROW_TILE, d), row), pl.BlockSpec((ROW_TILE, LANES), row),
                  pl.BlockSpec((ROW_TILE, d // 2), row), pl.BlockSpec((ROW_TILE, d // 2), row),
                  pl.BlockSpec((1, d), fix), pl.BlockSpec((d, 2 * KV_DIM), fix),
                  pl.BlockSpec((1, d), fix), pl.BlockSpec((d, d), fix),
                  pl.BlockSpec((256, 256), fix), pl.BlockSpec((1, KV_DIM), fix), pl.BlockSpec((1, d), fix),
                  rope, rope, rope],
        out_specs=(pl.BlockSpec((ROW_TILE, d), row), pl.BlockSpec((ROW_TILE, KV_DIM), row),
                   pl.BlockSpec((ROW_TILE, KV_DIM), row), pl.BlockSpec((ROW_TILE, d), row)),
        compiler_params=_cparams(("parallel",)),
        name="moe_combine_kv_q",
    )(h, rec, ga, gb, kv_gain.reshape(1, d), kv_w_bf16, q_gain.reshape(1, d), wq_bf16, hmean, k_hgain, q_hgain,
      cos_t, sina_t, sinb_t)


def _ffn_kernel(wblk_ref, we_ref, wlo_ref, whi_ref, xs_ref, w13_ref, w2_ref, ys_ref, w13b, w2b):
    w = pl.program_id(0)
    prev = jnp.maximum(w - 1, 0)
    first_visit = (w == 0) | (wblk_ref[w] != wblk_ref[prev])
    lo = wlo_ref[w]
    hi = whi_ref[w]

    def ffn(x):
        x = _unpack_halves(x).astype(BF16)
        cw = D_EXPERT // FFN_CHUNKS
        gate_up = []
        for c in range(FFN_CHUNKS):
            a = jnp.dot(x, w13b[:, c * cw:(c + 1) * cw], preferred_element_type=F32)
            u = jnp.dot(x, w13b[:, D_EXPERT + c * cw:D_EXPERT + (c + 1) * cw], preferred_element_type=F32)
            gate_up.append((a, u))
        hmid = jnp.concatenate([(_silu(a) * u).astype(BF16) for a, u in gate_up], axis=1)
        return _pack_halves(jnp.dot(hmid, w2b[...], preferred_element_type=F32))

    @pl.when(hi > lo)
    def _():
        @pl.when((w == 0) | (we_ref[w] != we_ref[prev]))
        def _():
            w13b[...] = w13_ref[...].astype(BF16)
            w2b[...] = w2_ref[...].astype(BF16)

        whole = (lo == 0) & (hi == EXPERT_BLOCK)

        @pl.when(whole)
        def _():
            ys_ref[...] = ffn(xs_ref[...])

        half = EXPERT_BLOCK // 2
        for p in range(2):
            rows = slice(p * half, (p + 1) * half)
            touched = (lo < (p + 1) * half) & (hi > p * half)

            @pl.when(jnp.logical_not(whole) & touched)
            def _():
                y = ffn(xs_ref[rows, :])
                row = lax.broadcasted_iota(jnp.int32, y.shape, 0) + p * half
                mine = (row >= lo) & (row < hi)

                @pl.when(first_visit)
                def _():
                    ys_ref[rows, :] = jnp.where(mine, y, jnp.zeros_like(y))

                @pl.when(jnp.logical_not(first_visit))
                def _():
                    ys_ref[rows, :] = jnp.where(mine, y, ys_ref[rows, :])

            @pl.when(jnp.logical_not(whole) & jnp.logical_not(touched) & first_visit)
            def _():
                ys_ref[rows, :] = jnp.zeros((half, ys_ref.shape[1]), U32)


def moe_ffn(xs, work, w13_all, w2_all, layer):
    n_slots, dp = xs.shape
    d = 2 * dp
    n_work = work[0].shape[0]
    xmap = lambda w, wb, we, wlo, whi: (wb[w], 0)
    w_map = lambda w, wb, we, wlo, whi: (layer, we[w], 0, 0)
    grid_spec = pltpu.PrefetchScalarGridSpec(
        num_scalar_prefetch=4,
        grid=(n_work,),
        in_specs=[pl.BlockSpec((EXPERT_BLOCK, dp), xmap),
                  pl.BlockSpec((None, None, d, 2 * D_EXPERT), w_map),
                  pl.BlockSpec((None, None, D_EXPERT, d), w_map)],
        out_specs=pl.BlockSpec((EXPERT_BLOCK, dp), xmap),
        scratch_shapes=[pltpu.VMEM((d, 2 * D_EXPERT), BF16), pltpu.VMEM((D_EXPERT, d), BF16)],
    )
    return pl.pallas_call(
        _ffn_kernel,
        out_shape=jax.ShapeDtypeStruct((n_slots, dp), U32),
        grid_spec=grid_spec,
        compiler_params=_cparams(("arbitrary",)),
        name="moe_ffn",
    )(*work, xs, w13_all, w2_all)


def _ffn_work_items(cnt):
    n_slots = 2 * T_ALL
    n_blocks = n_slots // EXPERT_BLOCK
    n_work = n_blocks + N_EXPERTS - 1
    end = jnp.cumsum(cnt)
    start = end - cnt
    first_blk = start // EXPERT_BLOCK
    last_blk = jnp.maximum(end - 1, start) // EXPERT_BLOCK
    n_items = jnp.where(cnt > 0, last_blk - first_blk + 1, 0)
    item_end = jnp.cumsum(n_items)
    item_start = item_end - n_items
    w = jnp.arange(n_work, dtype=jnp.int32)
    used = w < item_end[-1]
    wq = jnp.minimum(w, item_end[-1] - 1)
    e = jnp.sum((item_end[:, None] <= wq[None, :]).astype(jnp.int32), axis=0)
    onehot = e[None, :] == jnp.arange(N_EXPERTS, dtype=jnp.int32)[:, None]
    of_e = lambda table: jnp.sum(jnp.where(onehot, table[:, None], 0), axis=0)
    blk = jnp.where(used, of_e(first_blk) + (w - of_e(item_start)), n_blocks - 1).astype(jnp.int32)
    lo = jnp.maximum(of_e(start), blk * EXPERT_BLOCK) - blk * EXPERT_BLOCK
    hi = jnp.minimum(of_e(end), (blk + 1) * EXPERT_BLOCK) - blk * EXPERT_BLOCK
    lo = jnp.where(used, lo, 0).astype(jnp.int32)
    hi = jnp.where(used, hi, 0).astype(jnp.int32)
    return start, (blk, e, lo, hi)


def _combine_kernel(dest_ref, h_ref, rec_ref, ys_ref, *rest, split):
    out_refs, (g1, g2, sem) = rest[:-3], rest[-3:]
    rows = h_ref.shape[0]
    i = pl.program_id(0)

    def issue(r, c):
        _row_copy(ys_ref, dest_ref[0, 0, r], g1, r, sem).start(priority=0)
        _row_copy(ys_ref, dest_ref[0, 0, rows + r], g2, r, sem).start(priority=1)
        return c

    lax.fori_loop(0, rows, issue, 0, unroll=8)
    for buf in (g1, g2):
        pltpu.make_async_copy(ys_ref.at[pl.ds(0, rows)], buf, sem).wait()
    rec = rec_ref[...]
    res = h_ref[...] + rec[:, 4:5] * g1[...] + rec[:, 5:6] * g2[...]
    if not split:
        out_refs[0][...] = res
    else:
        @pl.when(i < N_PROMPT_TILES)
        def _():
            out_refs[0][...] = res

        @pl.when((i >= N_PROMPT_TILES) & (i < N_PROMPT_TILES + N_SAMPLE_TILES))
        def _():
            out_refs[1][...] = res


def moe_combine(h, rec, ys, dest3, split=False):
    t, d = h.shape
    row = lambda i: (i, 0)
    if split:
        parts = _parts_specs(d)[:2]
        out_shape = (jax.ShapeDtypeStruct((T_PROMPT, d), F32), jax.ShapeDtypeStruct((T_SAMPLE, d), F32))
        out_specs = tuple(parts)
    else:
        out_shape = jax.ShapeDtypeStruct((t, d), F32)
        out_specs = pl.BlockSpec((ROW_TILE, d), row)
    return pl.pallas_call(
        functools.partial(_combine_kernel, split=split),
        out_shape=out_shape,
        grid=(t // ROW_TILE,),
        in_specs=[pl.BlockSpec((1, 1, 2 * ROW_TILE), lambda i: (i, 0, 0), memory_space=pltpu.SMEM),
                  pl.BlockSpec((ROW_TILE, d), row), pl.BlockSpec((ROW_TILE, LANES), row),
                  pl.BlockSpec(memory_space=pl.ANY)],
        out_specs=out_specs,
        scratch_shapes=[pltpu.VMEM((ROW_TILE, d), F32), pltpu.VMEM((ROW_TILE, d), F32),
                        pltpu.SemaphoreType.DMA],
        compiler_params=_cparams(("arbitrary",)),
        name="moe_combine",
    )(dest3, h, rec, ys)


def hier_moe_layer(a, w_out_bf16, x, layer, gain, w_group, b_group, w_expert, b_expert, w13_all, w2_all, utri,
                   finish):
    t = T_ALL
    pad = LANES - N_EXPERTS - N_GROUPS
    w_router = jnp.concatenate([w_expert, w_group, jnp.zeros((D_MODEL, pad), F32)], axis=1)
    b_router = jnp.concatenate([b_expert, b_group, jnp.zeros((pad,), F32)])[:ROUTE_ROWS].reshape(ROUTE_ROWS, 1)
    h, xn, rec, rect, counts = moe_route(a, w_out_bf16, x, gain, w_router, b_router, utri)

    cnt = counts[:N_EXPERTS, 0].astype(jnp.int32)
    start, work = _ffn_work_items(cnt)
    experts = jnp.arange(N_EXPERTS, dtype=jnp.int32)[:, None]

    def slot_of(e_row, rank_row):
        first = jnp.sum(jnp.where(e_row.astype(jnp.int32)[None, :] == experts, start[:, None], 0), axis=0)
        return (first + rank_row.astype(jnp.int32)).reshape(1, t)

    dest_a = slot_of(rect[0], rect[2])
    dest_b = slot_of(rect[1], rect[3])

    xs = moe_dispatch_sc(xn, dest_a, dest_b, 2 * t)
    ys = moe_ffn(xs, work, w13_all, w2_all, layer)
    ga, gb = moe_gather_sc(ys, dest_a, dest_b)
    return finish(h, rec, ga, gb)


def _rope_tables(pos):
    half = ROPE_DIM // 2
    lane = np.arange(LANES) % HEAD_DIM
    rotary = lane < ROPE_DIM
    inv = jnp.where(rotary, jnp.exp(-math.log(ROPE_THETA) * jnp.asarray(lane % half, F32) * (2.0 / ROPE_DIM)), 0.0)
    ang = pos.astype(F32)[:, None] * inv[None, :]
    cos, sin = jnp.cos(ang), jnp.sin(ang)
    first = jnp.asarray(lane < half)
    second = jnp.asarray(rotary & (lane >= half))
    return cos, jnp.where(first, -sin, 0.0), jnp.where(second, sin, 0.0)


def kernel(x_prompt, x_sample, state_hgrn, cache_k_win, cache_v_win, meta_tokens, a_norm, a_w_in, a_lower_logits, a_out_norm, a_w_out, kv_norm, kv_w, k_norm, b_norm, b_wq, b_q_norm, b_sinks, b_w_out, moe_norm, moe_w_group, moe_b_group, moe_w_expert, moe_b_expert, moe_w13, moe_w2):
    tail_rows = T_ALL - OFF_META
    x_parts = (x_prompt.reshape(T_PROMPT, D_MODEL), x_sample.reshape(T_SAMPLE, D_MODEL),
               jnp.concatenate([meta_tokens.astype(F32), jnp.zeros((tail_rows - N_META, D_MODEL), F32)], axis=0))
    pos = jnp.concatenate([N_META + jnp.arange(SEQ, dtype=jnp.int32),
                           jnp.tile(PAST_LEN + jnp.arange(DEC_SEQ, dtype=jnp.int32), ROW_TILE // DEC_SEQ),
                           jnp.arange(N_META, dtype=jnp.int32),
                           jnp.zeros((ROW_TILE - N_META,), jnp.int32)])
    cos_t, sina_t, sinb_t = _rope_tables(pos)
    r256 = np.arange(256)
    hmean = jnp.asarray((r256[:, None] // HEAD_DIM == r256[None, :] // HEAD_DIM).astype(np.float32) / HEAD_DIM, BF16)
    rt = np.arange(ROW_TILE)
    utri = jnp.asarray((rt[:, None] < rt[None, :]).astype(np.float32), BF16)
    lower = jnp.cumsum(jax.nn.softmax(a_lower_logits.astype(F32), axis=0), axis=0)

    moe = functools.partial(hier_moe_layer, w13_all=moe_w13, w2_all=moe_w2, utri=utri)

    z = in_project(x_parts, a_norm[0], a_w_in[0].astype(BF16))
    zero_state = jnp.zeros((1, A_HEADS, A_DK, A_DV), F32)
    o_meta, s_meta = hgrn2_scan(z, zero_state, lower[0], a_out_norm[0],
                                row_off=OFF_META, n_seq=1, seq_len=N_META)
    o_prompt, s_prompt = hgrn2_scan(z, s_meta, lower[0], a_out_norm[0], row_off=0, n_seq=BATCH, seq_len=SEQ)
    o_sample, s_sample = hgrn2_scan(z, state_hgrn[0].astype(F32), lower[0], a_out_norm[0],
                                    row_off=OFF_SAMPLE, n_seq=DEC_BATCH, seq_len=DEC_SEQ, group=SCAN_SAMPLE_GROUP)
    o_tail = jnp.concatenate([o_meta, jnp.zeros((tail_rows - N_META, D_MODEL), BF16)], axis=0)
    finish0 = functools.partial(
        moe_combine_kv_q, kv_gain=kv_norm, kv_w_bf16=kv_w.astype(BF16), q_gain=b_norm[0], wq_bf16=b_wq[0].astype(BF16),
        hmean=hmean, k_hgain=jnp.tile(k_norm, KV_HEADS).reshape(1, KV_DIM),
        q_hgain=jnp.tile(b_q_norm[0], Q_HEADS).reshape(1, D_MODEL), cos_t=cos_t, sina_t=sina_t, sinb_t=sinb_t)
    h, k_all, v_all, q_all = moe((o_prompt, o_sample, o_tail), a_w_out[0].astype(BF16), x_parts, 0, moe_norm[0],
                                 moe_w_group[0], moe_b_group[0], moe_w_expert[0], moe_b_expert[0], finish=finish0)

    meta_blk = lambda a: jnp.concatenate([jnp.zeros((ATT_BLOCK - N_META, KV_DIM), F32),
                                          a[OFF_META:OFF_META + N_META]], axis=0)
    sinks = b_sinks[0].astype(F32)
    att_all = attention_prompt(q_all, k_all, v_all, meta_blk(k_all), meta_blk(v_all), sinks)
    att_all = attention_sample(q_all, cache_k_win.reshape(DEC_BATCH * WINDOW, KV_DIM).astype(F32),
                               cache_v_win.reshape(DEC_BATCH * WINDOW, KV_DIM).astype(F32),
                               k_all, v_all, sinks, att_all)
    y_p, y_s = moe(att_all, b_w_out[0].astype(BF16), h, 1, moe_norm[1], moe_w_group[1], moe_b_group[1],
                   moe_w_expert[1], moe_b_expert[1], finish=functools.partial(moe_combine_dense, split=True))

    y_prompt = y_p.reshape(BATCH, SEQ, D_MODEL)
    y_sample = y_s.reshape(DEC_BATCH, DEC_SEQ, D_MODEL)
    last = lambda a: jnp.stack([a[(b + 1) * SEQ - WINDOW:(b + 1) * SEQ] for b in range(BATCH)]).reshape(
        BATCH, WINDOW, KV_HEADS, HEAD_DIM)
    kp = last(k_all)
    vp = last(v_all)
    ks = k_all[OFF_SAMPLE:OFF_SAMPLE + T_SAMPLE].reshape(DEC_BATCH, DEC_SEQ, KV_HEADS, HEAD_DIM)
    vs = v_all[OFF_SAMPLE:OFF_SAMPLE + T_SAMPLE].reshape(DEC_BATCH, DEC_SEQ, KV_HEADS, HEAD_DIM)
    k_win_s = jnp.concatenate([cache_k_win, ks], axis=1)[:, -WINDOW:]
    v_win_s = jnp.concatenate([cache_v_win, vs], axis=1)[:, -WINDOW:]
    return (y_prompt, y_sample, s_prompt[None], s_sample[None], kp, vp, k_win_s, v_win_s)
```

```python
import functools
import math

import numpy as np
import jax
import jax.numpy as jnp
from jax import lax
from jax.experimental import pallas as pl
from jax.experimental.pallas import tpu as pltpu
from jax.experimental.pallas import tpu_sc as plsc

F32 = jnp.float32
BF16 = jnp.bfloat16
U32 = jnp.uint32

D_MODEL = 1024
BATCH = 4
SEQ = 4096
DEC_BATCH = 128
DEC_SEQ = 8
PAST_LEN = 8192
N_META = 16
A_HEADS = 8
A_DK = 128
A_DV = 128
Q_HEADS = 16
KV_HEADS = 4
HEAD_DIM = 64
KV_DIM = KV_HEADS * HEAD_DIM
WINDOW = 128
ROPE_DIM = 16
ROPE_THETA = 500000.0
N_GROUPS = 4
EXPERTS_PER_GROUP = 8
N_EXPERTS = 32
D_EXPERT = 512
RMS_EPS = 1e-6

LANES = 128
SUBLANES = 8
VMEM_LIMIT = 56 * 1024 * 1024

ROW_TILE = 512
T_PROMPT = BATCH * SEQ
T_SAMPLE = DEC_BATCH * DEC_SEQ
OFF_SAMPLE = T_PROMPT
OFF_META = T_PROMPT + T_SAMPLE
T_REAL = OFF_META + N_META
T_ALL = -(-T_REAL // ROW_TILE) * ROW_TILE
N_TILES = T_ALL // ROW_TILE

SCAN_CHUNK = 128
SCAN_SAMPLE_GROUP = 8
ATT_BLOCK = 128
EXPERT_BLOCK = 512
FFN_CHUNKS = 2


def _cparams(sem):
    return pltpu.CompilerParams(dimension_semantics=sem, vmem_limit_bytes=VMEM_LIMIT)


def _nt_dot(a, b):
    return lax.dot_general(a, b, (((1,), (1,)), ((), ())), preferred_element_type=F32)


def _rms(x, gain):
    ms = jnp.mean(x * x, axis=-1, keepdims=True)
    return x * lax.rsqrt(ms + RMS_EPS) * gain


def _silu(x):
    return x * jax.nn.sigmoid(x)


def _pack_halves(x):
    w = x.shape[1] // 2
    hi = lax.bitcast_convert_type(x[:, :w].astype(BF16).astype(F32), U32)
    lo = lax.bitcast_convert_type(x[:, w:].astype(BF16).astype(F32), U32)
    return hi | (lo >> 16)


def _unpack_halves(p):
    hi = lax.bitcast_convert_type(p & jnp.uint32(0xFFFF0000), F32)
    lo = lax.bitcast_convert_type(p << 16, F32)
    return jnp.concatenate([hi, lo], axis=1)


N_PROMPT_TILES = T_PROMPT // ROW_TILE
N_SAMPLE_TILES = T_SAMPLE // ROW_TILE


def _parts_specs(width):
    return [pl.BlockSpec((ROW_TILE, width), lambda i: (jnp.minimum(i, N_PROMPT_TILES - 1), 0)),
            pl.BlockSpec((ROW_TILE, width), lambda i: (jnp.clip(i - N_PROMPT_TILES, 0, N_SAMPLE_TILES - 1), 0)),
            pl.BlockSpec((ROW_TILE, width), lambda i: (0, 0))]


def _pick_part(i, p_ref, s_ref, t_ref, dtype):
    return jnp.where(i < N_PROMPT_TILES, p_ref[...].astype(dtype),
                     jnp.where(i < N_PROMPT_TILES + N_SAMPLE_TILES, s_ref[...].astype(dtype),
                               t_ref[...].astype(dtype)))


def _in_proj_kernel(xp_ref, xs_ref, xt_ref, g_ref, w_ref, o_ref):
    x = _pick_part(pl.program_id(0), xp_ref, xs_ref, xt_ref, F32)
    xn = _rms(x, g_ref[...])
    o_ref[...] = jnp.dot(xn.astype(BF16), w_ref[...], preferred_element_type=F32)


def in_project(x_parts, gain, w_bf16):
    d, n = w_bf16.shape
    return pl.pallas_call(
        _in_proj_kernel,
        out_shape=jax.ShapeDtypeStruct((T_ALL, n), F32),
        grid=(N_TILES,),
        in_specs=_parts_specs(d) + [pl.BlockSpec((1, d), lambda i: (0, 0)),
                                    pl.BlockSpec((d, n), lambda i: (0, 0))],
        out_specs=pl.BlockSpec((ROW_TILE, n), lambda i: (i, 0)),
        compiler_params=_cparams(("parallel",)),
        name="in_project",
    )(*x_parts, gain.reshape(1, d), w_bf16)


def _mixer_out_kernel(ap_ref, as_ref, at_ref, w_ref, xp_ref, xs_ref, xt_ref, o_ref):
    i = pl.program_id(0)
    a = _pick_part(i, ap_ref, as_ref, at_ref, BF16)
    x = _pick_part(i, xp_ref, xs_ref, xt_ref, F32)
    o_ref[...] = x + jnp.dot(a, w_ref[...], preferred_element_type=F32)


def mixer_out(a_parts, w_bf16, x_parts):
    k, n = w_bf16.shape
    return pl.pallas_call(
        _mixer_out_kernel,
        out_shape=jax.ShapeDtypeStruct((T_ALL, n), F32),
        grid=(N_TILES,),
        in_specs=_parts_specs(k) + [pl.BlockSpec((k, n), lambda i: (0, 0))] + _parts_specs(n),
        out_specs=pl.BlockSpec((ROW_TILE, n), lambda i: (i, 0)),
        compiler_params=_cparams(("parallel",)),
        name="mixer_out",
    )(*a_parts, w_bf16, *x_parts)


def _head_norm_rope(y, hmean_ref, hgain, cos_t, sina_t, sinb_t):
    rows, width = y.shape
    sq = (y * y).astype(BF16)
    parts = []
    for s in range(width // 256):
        parts.append(jnp.dot(sq[:, s * 256:(s + 1) * 256], hmean_ref[...], preferred_element_type=F32))
    ms = parts[0] if len(parts) == 1 else jnp.concatenate(parts, axis=1)
    yn = y * lax.rsqrt(ms + RMS_EPS) * hgain
    reps = width // LANES
    cos_w = jnp.concatenate([cos_t] * reps, axis=1)
    sina_w = jnp.concatenate([sina_t] * reps, axis=1)
    sinb_w = jnp.concatenate([sinb_t] * reps, axis=1)
    half = ROPE_DIM // 2
    nxt = pltpu.roll(yn, width - half, 1)
    prv = pltpu.roll(yn, half, 1)
    return yn * cos_w + nxt * sina_w + prv * sinb_w


def _kv_kernel(x_ref, g_ref, w_ref, hmean_ref, hg_ref, cos_ref, sina_ref, sinb_ref, k_ref, v_ref):
    xn = _rms(x_ref[...], g_ref[...])
    z = jnp.dot(xn.astype(BF16), w_ref[...], preferred_element_type=F32)
    k = _head_norm_rope(z[:, :KV_DIM], hmean_ref, hg_ref[...], cos_ref[...], sina_ref[...], sinb_ref[...])
    k_ref[...] = k
    v_ref[...] = z[:, KV_DIM:]


def _rope_tile(i):
    tiles_per_seq = SEQ // ROW_TILE
    n_prompt_tiles = T_PROMPT // ROW_TILE
    n_sample_tiles = T_SAMPLE // ROW_TILE
    return (jnp.where(i < n_prompt_tiles, i % tiles_per_seq,
                      jnp.where(i < n_prompt_tiles + n_sample_tiles, tiles_per_seq, tiles_per_seq + 1)), 0)


def kv_project(x, gain, w_bf16, hmean, hgain_w, cos_t, sina_t, sinb_t):
    t, d = x.shape
    row = lambda i: (i, 0)
    fix = lambda i: (0, 0)
    return pl.pallas_call(
        _kv_kernel,
        out_shape=(jax.ShapeDtypeStruct((t, KV_DIM), F32), jax.ShapeDtypeStruct((t, KV_DIM), F32)),
        grid=(t // ROW_TILE,),
        in_specs=[pl.BlockSpec((ROW_TILE, d), row), pl.BlockSpec((1, d), fix),
                  pl.BlockSpec((d, 2 * KV_DIM), fix), pl.BlockSpec((256, 256), fix),
                  pl.BlockSpec((1, KV_DIM), fix),
                  pl.BlockSpec((ROW_TILE, LANES), _rope_tile), pl.BlockSpec((ROW_TILE, LANES), _rope_tile),
                  pl.BlockSpec((ROW_TILE, LANES), _rope_tile)],
        out_specs=(pl.BlockSpec((ROW_TILE, KV_DIM), row), pl.BlockSpec((ROW_TILE, KV_DIM), row)),
        compiler_params=_cparams(("parallel",)),
        name="kv_project",
    )(x, gain.reshape(1, d), w_bf16, hmean, hgain_w, cos_t, sina_t, sinb_t)


def _q_kernel(x_ref, g_ref, w_ref, hmean_ref, hg_ref, cos_ref, sina_ref, sinb_ref, q_ref):
    xn = _rms(x_ref[...], g_ref[...])
    z = jnp.dot(xn.astype(BF16), w_ref[...], preferred_element_type=F32)
    q = _head_norm_rope(z, hmean_ref, hg_ref[...], cos_ref[...], sina_ref[...], sinb_ref[...])
    q_ref[...] = (q * HEAD_DIM ** -0.5).astype(q_ref.dtype)


def q_project(x, gain, w_bf16, hmean, hgain_w, cos_t, sina_t, sinb_t):
    t, d = x.shape
    row = lambda i: (i, 0)
    fix = lambda i: (0, 0)
    return pl.pallas_call(
        _q_kernel,
        out_shape=jax.ShapeDtypeStruct((t, d), BF16),
        grid=(t // ROW_TILE,),
        in_specs=[pl.BlockSpec((ROW_TILE, d), row), pl.BlockSpec((1, d), fix),
                  pl.BlockSpec((d, d), fix), pl.BlockSpec((256, 256), fix),
                  pl.BlockSpec((1, d), fix),
                  pl.BlockSpec((ROW_TILE, LANES), _rope_tile), pl.BlockSpec((ROW_TILE, LANES), _rope_tile),
                  pl.BlockSpec((ROW_TILE, LANES), _rope_tile)],
        out_specs=pl.BlockSpec((ROW_TILE, d), row),
        compiler_params=_cparams(("parallel",)),
        name="q_project",
    )(x, gain.reshape(1, d), w_bf16, hmean, hgain_w, cos_t, sina_t, sinb_t)


def _matmul_residual_kernel(a_ref, w_ref, r_ref, o_ref):
    o_ref[...] = r_ref[...] + jnp.dot(a_ref[...], w_ref[...], preferred_element_type=F32)


def matmul_residual(a_bf16, w_bf16, resid):
    t, k = a_bf16.shape
    n = w_bf16.shape[1]
    return pl.pallas_call(
        _matmul_residual_kernel,
        out_shape=jax.ShapeDtypeStruct((t, n), F32),
        grid=(t // ROW_TILE,),
        in_specs=[pl.BlockSpec((ROW_TILE, k), lambda i: (i, 0)),
                  pl.BlockSpec((k, n), lambda i: (0, 0)),
                  pl.BlockSpec((ROW_TILE, n), lambda i: (i, 0))],
        out_specs=pl.BlockSpec((ROW_TILE, n), lambda i: (i, 0)),
        compiler_params=_cparams(("parallel",)),
        name="matmul_residual",
    )(a_bf16, w_bf16, resid)


def _scan_levels(c):
    levels = []
    m = c
    while m >= 2:
        levels.append(m)
        m //= 2
    return levels


LOG2E = 1.4426950408889634


def _scan_kernel(z_ref, s0_ref, lb_ref, og_ref, tri_ref, lmask_ref, sgn_ref, o_ref, sfin_ref, s_scr, b_scr,
                 *, rows, seq_len):
    c_idx = pl.program_id(1)
    levels = _scan_levels(seq_len)
    n_sub = rows // seq_len
    hk = A_HEADS * A_DK

    @pl.when(c_idx == 0)
    def _():
        s_scr[...] = s0_ref[...]

    sub = lax.broadcasted_iota(jnp.int32, (SUBLANES, LANES), 0)
    row = lax.broadcasted_iota(jnp.int32, (LANES, LANES), 0)
    og = og_ref[...]

    def pad_f32(x):
        if x.shape[0] == LANES:
            return x
        return jnp.concatenate([x, jnp.zeros((LANES - x.shape[0], x.shape[1]), x.dtype)], axis=0)

    def pad_rows(x):
        return pad_f32(x).astype(BF16)

    def cols(part, h):
        return slice(part * hk + h * LANES, part * hk + (h + 1) * LANES)

    def gates(h):
        lb = lb_ref[:, cols(0, h)]
        forget = lb + (1.0 - lb) * jax.nn.sigmoid(z_ref[:, cols(1, h)])
        logf = jnp.log(forget)
        hi = logf.astype(BF16).astype(F32)
        r1 = logf - hi
        mid = r1.astype(BF16).astype(F32)
        lo = r1 - mid
        cs = jnp.dot(tri_ref[...], pad_rows(jnp.concatenate([hi, mid, lo], axis=1)),
                     preferred_element_type=F32)
        b = (cs[:rows, :LANES] + cs[:rows, LANES:2 * LANES]) + cs[:rows, 2 * LANES:]
        b_scr[h] = b
        return _silu(z_ref[:, cols(0, h)]), 1.0 - forget, b

    def bref_for(h, m):
        b_rows = b_scr.at[h]
        half = m // 2
        pieces = []
        for g in range(rows // SUBLANES):
            base = g * SUBLANES
            if m >= SUBLANES:
                r = (base // m) * m + half - 1
                piece = jnp.broadcast_to(b_rows[r:r + 1, :], (SUBLANES, LANES))
            else:
                piece = jnp.broadcast_to(b_rows[base + half - 1:base + half, :], (SUBLANES, LANES))
                for blk in range(1, SUBLANES // m):
                    r = base + blk * m + half - 1
                    piece = jnp.where(sub >= blk * m,
                                      jnp.broadcast_to(b_rows[r:r + 1, :], (SUBLANES, LANES)), piece)
            pieces.append(piece)
        return pieces[0] if len(pieces) == 1 else jnp.concatenate(pieces, axis=0)

    heads = range(A_HEADS)
    qkb = [gates(h) for h in heads]
    att = [_nt_dot(pad_rows(qf), pad_rows(kf)) * lmask_ref[len(levels)] for qf, kf, _ in qkb]
    for li, m in enumerate(levels):
        for h in heads:
            qf, kf, b = qkb[h]
            sgn = sgn_ref[li]
            e = jnp.exp2((b - bref_for(h, m)) * sgn)
            w = pad_rows(jnp.where(sgn > 0, qf, kf) * e)
            att[h] = att[h] + _nt_dot(w, w) * lmask_ref[li]

    def finish(h):
        qf, kf, b = qkb[h]
        b_rows = b_scr.at[h]
        v_b = pad_rows(z_ref[:, cols(2, h)])
        o_intra = jnp.dot(att[h].astype(BF16), v_b, preferred_element_type=F32)
        eb = jnp.exp(b)
        qs = qf * eb
        b_end = [jnp.broadcast_to(b_rows[(i + 1) * seq_len - 1:(i + 1) * seq_len, :], (seq_len, LANES))
                 for i in range(n_sub)]
        b_end = b_end[0] if n_sub == 1 else jnp.concatenate(b_end, axis=0)
        kd_t = pad_f32(kf * jnp.exp(b_end - b)).T.astype(BF16)
        eb_t = pad_f32(eb).T
        qs_b = pad_rows(qs)
        o = o_intra
        for i in range(n_sub):
            s_old = s_scr[i, h]
            first, last = i * seq_len, (i + 1) * seq_len - 1
            if n_sub == 1:
                qs_i, v_i = qs_b, v_b
            else:
                mine = (row >= first) & (row <= last)
                qs_i = jnp.where(mine, qs_b, jnp.zeros_like(qs_b))
                v_i = jnp.where(mine, v_b, jnp.zeros_like(v_b))
            o = o + jnp.dot(qs_i, s_old.astype(BF16), preferred_element_type=F32)
            decay = jnp.broadcast_to(eb_t[:, last:last + 1], (LANES, LANES))
            s_scr[i, h] = decay * s_old + jnp.dot(kd_t, v_i, preferred_element_type=F32)
        o = o[:rows]

        on = _rms(o, og) * _silu(z_ref[:, cols(3, h)])
        o_ref[:, cols(0, h)] = on.astype(o_ref.dtype)

    for h in heads:
        finish(h)

    @pl.when(c_idx == pl.num_programs(1) - 1)
    def _():
        sfin_ref[...] = s_scr[...]


def _scan_consts(rows, seq_len):
    levels = _scan_levels(seq_len)
    r = np.arange(LANES)
    t, s = r[:, None], r[None, :]
    live = (t < rows) & (s < rows)
    tri = ((s <= t) & (t // seq_len == s // seq_len) & live).astype(np.float32)
    masks, sgns = [], []
    for m in levels:
        masks.append(((t // m == s // m) & (t % m >= m // 2) & (s % m < m // 2) & live).astype(np.float32))
        sgns.append(np.broadcast_to(np.where(r[:rows, None] % m >= m // 2, LOG2E, -LOG2E), (rows, LANES)))
    masks.append(((t == s) & live).astype(np.float32))
    return jnp.asarray(tri, BF16), jnp.asarray(np.stack(masks), F32), jnp.asarray(np.stack(sgns), F32)


def hgrn2_scan(z, s0, lb, o_gain, *, row_off, n_seq, seq_len, group=1):
    hv = A_HEADS * A_DV
    if seq_len > SCAN_CHUNK:
        assert group == 1
        sub_len, rows, n_chunks, n_steps = SCAN_CHUNK, SCAN_CHUNK, seq_len // SCAN_CHUNK, n_seq
    else:
        sub_len, rows, n_chunks, n_steps = seq_len, group * seq_len, 1, n_seq // group
    blk_off = row_off // rows
    tri, lmask, sgn = _scan_consts(rows, sub_len)
    shared_s0 = s0.shape[0] == 1
    fix2 = lambda s, c: (0, 0)
    fix3 = lambda s, c: (0, 0, 0)
    o, sfin = pl.pallas_call(
        functools.partial(_scan_kernel, rows=rows, seq_len=sub_len),
        out_shape=(jax.ShapeDtypeStruct((n_seq * seq_len, hv), BF16 if rows % 16 == 0 else F32),
                   jax.ShapeDtypeStruct((n_seq, A_HEADS, A_DK, A_DV), F32)),
        grid=(n_steps, n_chunks),
        in_specs=[pl.BlockSpec((rows, 4 * hv), lambda s, c: (blk_off + s * n_chunks + c, 0)),
                  pl.BlockSpec((group, A_HEADS, A_DK, A_DV), (lambda s, c: (0, 0, 0, 0)) if shared_s0
                               else (lambda s, c: (s, 0, 0, 0))),
                  pl.BlockSpec((1, hv), fix2), pl.BlockSpec((1, A_DV), fix2),
                  pl.BlockSpec((LANES, LANES), fix2), pl.BlockSpec(lmask.shape, fix3),
                  pl.BlockSpec(sgn.shape, fix3)],
        out_specs=(pl.BlockSpec((rows, hv), lambda s, c: (s * n_chunks + c, 0)),
                   pl.BlockSpec((group, A_HEADS, A_DK, A_DV), lambda s, c: (s, 0, 0, 0))),
        scratch_shapes=[pltpu.VMEM((group, A_HEADS, A_DK, A_DV), F32), pltpu.VMEM((A_HEADS, rows, LANES), F32)],
        compiler_params=_cparams(("parallel", "arbitrary")),
        name=f"hgrn2_scan_r{rows}",
    )(z, s0, lb.reshape(1, hv), o_gain.reshape(1, A_DV), tri, lmask, sgn)
    return o, sfin


KEYS = 2 * ATT_BLOCK


def _pair_operand(x, kh):
    slab = x[:, (kh // 2) * LANES:(kh // 2 + 1) * LANES]
    lane = lax.broadcasted_iota(jnp.int32, slab.shape, 1)
    if kh % 2 == 0:
        lo = jnp.where(lane < HEAD_DIM, slab, 0.0)
        hi = pltpu.roll(lo, HEAD_DIM, 1)
    else:
        hi = jnp.where(lane >= HEAD_DIM, slab, 0.0)
        lo = pltpu.roll(hi, HEAD_DIM, 1)
    return jnp.concatenate([lo, hi], axis=0).astype(BF16)


def _window_bias(rows, jmin):
    t_i = lax.broadcasted_iota(jnp.int32, (rows, 2 * KEYS), 0)
    c_i = lax.broadcasted_iota(jnp.int32, (rows, 2 * KEYS), 1)
    j_i = c_i & (ATT_BLOCK - 1)
    own = (c_i & ATT_BLOCK) != 0
    ok = (own & (j_i <= t_i)) | (jnp.logical_not(own) & (j_i >= t_i) & (j_i >= jmin))
    return jnp.where(ok, 0.0, -jnp.inf).astype(F32)


def _pair_softmax(s, sink_a, sink_b):
    probs, rinv = [], []
    for hh, sink in enumerate((sink_a, sink_b)):
        sh = s[:, hh * KEYS:(hh + 1) * KEYS]
        m = jnp.maximum(jnp.max(sh, axis=-1, keepdims=True), sink)
        p = jnp.exp(sh - m)
        den = jnp.sum(p, axis=-1, keepdims=True) + jnp.exp(sink - m)
        probs.append(p.astype(BF16))
        rinv.append(1.0 / den)
    lane = lax.broadcasted_iota(jnp.int32, (s.shape[0], LANES), 1)
    return jnp.concatenate(probs, axis=1), jnp.where(lane < HEAD_DIM, rinv[0], rinv[1])


def _attn_prompt_kernel(sink_ref, q_ref, kp_ref, ko_ref, vp_ref, vo_ref, km_ref, vm_ref, o_ref,
                        k2_scr, v2_scr, s_scr, p_scr, r_scr):
    n = pl.program_id(0)
    nbp = SEQ // ATT_BLOCK
    n_pairs = Q_HEADS // 2

    @pl.when(n >= BATCH * nbp)
    def _():
        o_ref[...] = jnp.zeros_like(o_ref)

    @pl.when(n < BATCH * nbp)
    def _():
        first = (n % nbp) == 0
        jmin = jnp.where(first, ATT_BLOCK - N_META, 0)
        k = jnp.concatenate([jnp.where(first, km_ref[...], kp_ref[...]), ko_ref[...]], axis=0)
        v = jnp.concatenate([jnp.where(first, vm_ref[...], vp_ref[...]), vo_ref[...]], axis=0)
        bias = _window_bias(ATT_BLOCK, jmin)
        for kh in range(KV_HEADS):
            k2_scr[kh] = _pair_operand(k, kh)
            v2_scr[kh] = _pair_operand(v, kh)
        for pair in range(n_pairs):
            s_scr[pair] = _nt_dot(q_ref[:, pair * LANES:(pair + 1) * LANES], k2_scr[pair // 2]) + bias
        for pair in range(n_pairs):
            p, rinv = _pair_softmax(s_scr[pair], sink_ref[2 * pair], sink_ref[2 * pair + 1])
            p_scr[pair] = p
            r_scr[pair] = rinv
        for pair in range(n_pairs):
            o = jnp.dot(p_scr[pair], v2_scr[pair // 2], preferred_element_type=F32) * r_scr[pair]
            o_ref[:, pair * LANES:(pair + 1) * LANES] = o.astype(o_ref.dtype)


def attention_prompt(q_all, k_all, v_all, k_meta_blk, v_meta_blk, sinks):
    n_prompt_blocks = T_PROMPT // ATT_BLOCK
    n_blocks = T_ALL // ATT_BLOCK
    n_pairs = Q_HEADS // 2
    own = lambda n, sk: (jnp.minimum(n, n_prompt_blocks - 1), 0)
    prev = lambda n, sk: (jnp.maximum(jnp.minimum(n, n_prompt_blocks - 1) - 1, 0), 0)
    fix = lambda n, sk: (0, 0)
    grid_spec = pltpu.PrefetchScalarGridSpec(
        num_scalar_prefetch=1,
        grid=(n_blocks,),
        in_specs=[pl.BlockSpec((ATT_BLOCK, D_MODEL), own),
                  pl.BlockSpec((ATT_BLOCK, KV_DIM), prev), pl.BlockSpec((ATT_BLOCK, KV_DIM), own),
                  pl.BlockSpec((ATT_BLOCK, KV_DIM), prev), pl.BlockSpec((ATT_BLOCK, KV_DIM), own),
                  pl.BlockSpec((ATT_BLOCK, KV_DIM), fix), pl.BlockSpec((ATT_BLOCK, KV_DIM), fix)],
        out_specs=pl.BlockSpec((ATT_BLOCK, D_MODEL), lambda n, sk: (n, 0)),
        scratch_shapes=[pltpu.VMEM((KV_HEADS, 2 * KEYS, LANES), BF16), pltpu.VMEM((KV_HEADS, 2 * KEYS, LANES), BF16),
                        pltpu.VMEM((n_pairs, ATT_BLOCK, 2 * KEYS), F32),
                        pltpu.VMEM((n_pairs, ATT_BLOCK, 2 * KEYS), BF16),
                        pltpu.VMEM((n_pairs, ATT_BLOCK, LANES), F32)],
    )
    return pl.pallas_call(
        _attn_prompt_kernel,
        out_shape=jax.ShapeDtypeStruct((T_ALL, D_MODEL), BF16),
        grid_spec=grid_spec,
        compiler_params=_cparams(("parallel",)),
        name="attention_prompt",
    )(sinks, q_all, k_all, k_all, v_all, v_all, k_meta_blk, v_meta_blk)


SAMPLE_GROUP = ATT_BLOCK // DEC_SEQ


def _attn_sample_kernel(sink_ref, q_ref, ck_ref, cv_ref, kn_ref, vn_ref, buf_ref, o_ref, kw_ref, vw_ref,
                        qf_scr, of_scr, k2_scr, v2_scr):
    del buf_ref
    qrows = 2 * DEC_SEQ
    qf_scr[...] = q_ref[...].astype(F32)
    bias = _window_bias(qrows, 0)
    zq = jnp.zeros((qrows - DEC_SEQ, D_MODEL), F32)
    zk = jnp.zeros((ATT_BLOCK - DEC_SEQ, KV_DIM), F32)

    def seq_body(i, carry):
        r_new = pl.multiple_of(i * DEC_SEQ, DEC_SEQ)
        q = jnp.concatenate([qf_scr[pl.ds(r_new, DEC_SEQ), :], zq], axis=0).astype(BF16)
        old = lambda c_ref: jnp.concatenate([c_ref[i, :, kh, :] for kh in range(KV_HEADS)], axis=1)
        k = jnp.concatenate([old(ck_ref), kn_ref[pl.ds(r_new, DEC_SEQ), :], zk], axis=0)
        v = jnp.concatenate([old(cv_ref), vn_ref[pl.ds(r_new, DEC_SEQ), :], zk], axis=0)
        for c_ref, n_ref, w_ref in ((ck_ref, kn_ref, kw_ref), (cv_ref, vn_ref, vw_ref)):
            w_ref[i, 0:WINDOW - DEC_SEQ] = c_ref[i, DEC_SEQ:WINDOW]
            for kh in range(KV_HEADS):
                w_ref[i, WINDOW - DEC_SEQ:WINDOW, kh, :] = n_ref[pl.ds(r_new, DEC_SEQ),
                                                                 kh * HEAD_DIM:(kh + 1) * HEAD_DIM]
        for kh in range(KV_HEADS):
            k2_scr[kh] = _pair_operand(k, kh)
            v2_scr[kh] = _pair_operand(v, kh)
        scores = [_nt_dot(q[:, pair * LANES:(pair + 1) * LANES], k2_scr[pair // 2]) + bias
                  for pair in range(Q_HEADS // 2)]
        soft = [_pair_softmax(s, sink_ref[2 * pair], sink_ref[2 * pair + 1]) for pair, s in enumerate(scores)]
        for pair, (p, rinv) in enumerate(soft):
            o = jnp.dot(p, v2_scr[pair // 2], preferred_element_type=F32) * rinv
            of_scr[pl.ds(r_new, DEC_SEQ), pair * LANES:(pair + 1) * LANES] = o[:DEC_SEQ]
        return carry

    lax.fori_loop(0, SAMPLE_GROUP, seq_body, 0)
    o_ref[...] = of_scr[...].astype(o_ref.dtype)


def attention_sample(q_all, cache_k, cache_v, k_all, v_all, sinks, out_buf):
    first_blk = OFF_SAMPLE // ATT_BLOCK
    new = lambda g, sk: (first_blk + g, 0)
    old = pl.BlockSpec((SAMPLE_GROUP, WINDOW, KV_HEADS, HEAD_DIM), lambda g, sk: (g, 0, 0, 0))
    grid_spec = pltpu.PrefetchScalarGridSpec(
        num_scalar_prefetch=1,
        grid=(DEC_BATCH // SAMPLE_GROUP,),
        in_specs=[pl.BlockSpec((ATT_BLOCK, D_MODEL), new),
                  old, old,
                  pl.BlockSpec((ATT_BLOCK, KV_DIM), new), pl.BlockSpec((ATT_BLOCK, KV_DIM), new),
                  pl.BlockSpec(memory_space=pl.ANY)],
        out_specs=(pl.BlockSpec((ATT_BLOCK, D_MODEL), new), old, old),
        scratch_shapes=[pltpu.VMEM((ATT_BLOCK, D_MODEL), F32), pltpu.VMEM((ATT_BLOCK, D_MODEL), F32),
                        pltpu.VMEM((KV_HEADS, 2 * KEYS, LANES), BF16), pltpu.VMEM((KV_HEADS, 2 * KEYS, LANES), BF16)],
    )
    window = jax.ShapeDtypeStruct(cache_k.shape, cache_k.dtype)
    return pl.pallas_call(
        _attn_sample_kernel,
        out_shape=(jax.ShapeDtypeStruct(out_buf.shape, out_buf.dtype), window, window),
        grid_spec=grid_spec,
        input_output_aliases={6: 0},
        compiler_params=_cparams(("parallel",)),
        name="attention_sample",
    )(sinks, q_all, cache_k, cache_v, k_all, v_all, out_buf)


ROUTE_COLS = 8
ROUTE_ROWS = 48


def _route_kernel(*refs, parts):
    i = pl.program_id(0)
    if parts:
        (ap_ref, as_ref, at_ref, w_ref, xp_ref, xs_ref, xt_ref), refs = refs[:7], refs[7:]
        a = _pick_part(i, ap_ref, as_ref, at_ref, BF16)
        x = _pick_part(i, xp_ref, xs_ref, xt_ref, F32)
    else:
        (a_ref, w_ref, x_ref), refs = refs[:3], refs[3:]
        a, x = a_ref[...], x_ref[...]
    g_ref, wh_ref, wl_ref, br_ref, utri_ref, h_ref, xn_ref, rec_ref, rect_ref, cnt_ref, cnt_scr = refs

    @pl.when(i == 0)
    def _():
        cnt_scr[...] = jnp.zeros_like(cnt_scr)

    h = x + jnp.dot(a, w_ref[...], preferred_element_type=F32)
    h_ref[...] = h
    xn = _rms(h, g_ref[...])
    xn_ref[...] = _pack_halves(xn)
    xh = xn.astype(BF16)
    xl = (xn - xh.astype(F32)).astype(BF16)
    logits = (_nt_dot(wh_ref[...], xh) + (_nt_dot(wl_ref[...], xh) + _nt_dot(wh_ref[...], xl)))[:ROUTE_ROWS]
    logits = logits + br_ref[...]
    tokens = logits.shape[1]
    rid = lax.broadcasted_iota(jnp.int32, (ROUTE_ROWS, tokens), 0).astype(F32)
    neg = jnp.float32(-jnp.inf)
    big = jnp.float32(ROUTE_ROWS)

    is_g = (rid >= N_EXPERTS) & (rid < N_EXPERTS + N_GROUPS)
    gl = jnp.where(is_g, logits, neg)
    gmax = jnp.max(gl, axis=0, keepdims=True)
    gsel = jnp.min(jnp.where(gl == gmax, rid, big), axis=0, keepdims=True) - N_EXPERTS
    gden = jnp.sum(jnp.where(is_g, jnp.exp(gl - gmax), 0.0), axis=0, keepdims=True)
    gw = 1.0 / gden

    in_grp = (rid >= gsel * EXPERTS_PER_GROUP) & (rid < (gsel + 1) * EXPERTS_PER_GROUP)
    el = jnp.where(in_grp, logits, neg)
    t1 = jnp.max(el, axis=0, keepdims=True)
    e1 = jnp.min(jnp.where(el == t1, rid, big), axis=0, keepdims=True)
    el2 = jnp.where(rid == e1, neg, el)
    t2 = jnp.max(el2, axis=0, keepdims=True)
    e2 = jnp.min(jnp.where(el2 == t2, rid, big), axis=0, keepdims=True)
    x2 = jnp.exp(t2 - t1)
    w1 = gw / (1.0 + x2)
    w2 = gw * x2 / (1.0 + x2)

    oh1 = (rid == e1).astype(F32)
    oh2 = (rid == e2).astype(F32)
    oh = oh1 + oh2
    before = jnp.dot(oh.astype(BF16), utri_ref[...], preferred_element_type=F32)
    base = cnt_scr[...] + before
    r1 = jnp.sum(base * oh1, axis=0, keepdims=True)
    r2 = jnp.sum(base * oh2, axis=0, keepdims=True)
    cnt_scr[...] = cnt_scr[...] + jnp.sum(oh, axis=1, keepdims=True)

    zero = jnp.zeros_like(w1)
    rect = jnp.concatenate([e1, e2, r1, r2, w1, w2, zero, zero], axis=0)
    rect_ref[...] = rect
    wide = jnp.concatenate([rect, jnp.zeros((LANES - ROUTE_COLS, tokens), F32)], axis=0)
    rec_ref[...] = jnp.concatenate([wide[:, t0:t0 + LANES].T for t0 in range(0, tokens, LANES)], axis=0)
    cnt_ref[...] = cnt_scr[...]


def moe_route(a, w_out_bf16, x, gain, w_router, b_router, utri):
    parts = isinstance(a, tuple)
    t, d = T_ALL, D_MODEL
    row = lambda i: (i, 0)
    fix = lambda i: (0, 0)
    w_t = w_router.T
    w_hi = w_t.astype(BF16)
    w_lo = (w_t - w_hi.astype(F32)).astype(BF16)
    w_spec = pl.BlockSpec((d, d), fix)
    if parts:
        pre_specs = _parts_specs(d) + [w_spec] + _parts_specs(d)
        pre_args = (*a, w_out_bf16, *x)
    else:
        pre_specs = [pl.BlockSpec((ROW_TILE, d), row), w_spec, pl.BlockSpec((ROW_TILE, d), row)]
        pre_args = (a, w_out_bf16, x)
    return pl.pallas_call(
        functools.partial(_route_kernel, parts=parts),
        out_shape=(jax.ShapeDtypeStruct((t, d), F32),
                   jax.ShapeDtypeStruct((t, d // 2), U32), jax.ShapeDtypeStruct((t, LANES), F32),
                   jax.ShapeDtypeStruct((ROUTE_COLS, t), F32), jax.ShapeDtypeStruct((ROUTE_ROWS, 1), F32)),
        grid=(t // ROW_TILE,),
        in_specs=pre_specs + [pl.BlockSpec((1, d), fix),
                              pl.BlockSpec((LANES, d), fix), pl.BlockSpec((LANES, d), fix),
                              pl.BlockSpec((ROUTE_ROWS, 1), fix), pl.BlockSpec((ROW_TILE, ROW_TILE), fix)],
        out_specs=(pl.BlockSpec((ROW_TILE, d), row),
                   pl.BlockSpec((ROW_TILE, d // 2), row), pl.BlockSpec((ROW_TILE, LANES), row),
                   pl.BlockSpec((ROUTE_COLS, ROW_TILE), lambda i: (0, i)), pl.BlockSpec((ROUTE_ROWS, 1), fix)),
        scratch_shapes=[pltpu.VMEM((ROUTE_ROWS, 1), F32)],
        compiler_params=_cparams(("arbitrary",)),
        name="moe_route",
    )(*pre_args, gain.reshape(1, d), w_hi, w_lo, b_router, utri)


def _row_copy(src, src_row, dst, dst_row, sem):
    return pltpu.make_async_copy(src.at[pl.ds(src_row, 1)], dst.at[pl.ds(dst_row, 1)], sem)


def _dispatch_kernel(dest_ref, xn_ref, xs_ref, sem):
    rows = xn_ref.shape[0]

    def issue(r, c):
        _row_copy(xn_ref, r, xs_ref, dest_ref[0, 0, r], sem).start(priority=0)
        _row_copy(xn_ref, r, xs_ref, dest_ref[0, 0, rows + r], sem).start(priority=1)
        return c

    lax.fori_loop(0, rows, issue, 0, unroll=8)
    for _ in range(2):
        pltpu.make_async_copy(xn_ref, xs_ref.at[pl.ds(0, rows)], sem).wait()


def moe_dispatch(xn, dest3, n_slots):
    t, d = xn.shape
    return pl.pallas_call(
        _dispatch_kernel,
        out_shape=jax.ShapeDtypeStruct((n_slots, d), F32),
        grid=(t // ROW_TILE,),
        in_specs=[pl.BlockSpec((1, 1, 2 * ROW_TILE), lambda i: (i, 0, 0), memory_space=pltpu.SMEM),
                  pl.BlockSpec((ROW_TILE, d), lambda i: (i, 0))],
        out_specs=pl.BlockSpec(memory_space=pl.ANY),
        scratch_shapes=[pltpu.SemaphoreType.DMA],
        compiler_params=_cparams(("arbitrary",)),
        name="moe_dispatch",
    )(dest3, xn)


SC_WINDOW = 64
SC_INDEX_WINDOW = 128


def _sc_mesh():
    return plsc.VectorSubcoreMesh(core_axis_name="core", subcore_axis_name="subcore")


def moe_dispatch_sc(xn, dest_a, dest_b, n_slots):
    t, d = xn.shape

    @pl.kernel(out_type=jax.ShapeDtypeStruct((n_slots, d), xn.dtype), mesh=_sc_mesh(),
               scratch_types=[pltpu.VMEM((SC_WINDOW, d), xn.dtype)], name="moe_dispatch_sc")
    def run(x_hbm, id_hbm, da_hbm, db_hbm, o_hbm, buf):
        def body(id_vmem, da_vmem, db_vmem):
            for j in range(SC_INDEX_WINDOW // SC_WINDOW):
                part = pl.ds(j * SC_WINDOW, SC_WINDOW)
                pltpu.sync_copy(x_hbm.at[id_vmem.at[0, part]], buf)
                pltpu.sync_copy(buf, o_hbm.at[da_vmem.at[0, part]])
                pltpu.sync_copy(buf, o_hbm.at[db_vmem.at[0, part]])

        idx_spec = pl.BlockSpec((1, SC_INDEX_WINDOW), lambda i: (0, i))
        pltpu.emit_pipeline(
            body,
            grid=(t // SC_INDEX_WINDOW,),
            in_specs=[idx_spec, idx_spec, idx_spec],
            out_specs=[],
            core_axis_name=("core", "subcore"),
            dimension_semantics=(pltpu.PARALLEL,),
        )(id_hbm, da_hbm, db_hbm)

    return run(xn, jnp.arange(t, dtype=jnp.int32).reshape(1, t), dest_a, dest_b)


def moe_gather_sc(ys, dest_a, dest_b):
    d = ys.shape[1]
    t = dest_a.shape[1]
    out = jax.ShapeDtypeStruct((t, d), ys.dtype)

    @pl.kernel(out_type=(out, out), mesh=_sc_mesh(), scratch_types=[pltpu.VMEM((SC_WINDOW, d), ys.dtype)],
               name="moe_gather_sc")
    def run(y_hbm, id_hbm, da_hbm, db_hbm, ga_hbm, gb_hbm, buf):
        def body(id_vmem, da_vmem, db_vmem):
            for j in range(SC_INDEX_WINDOW // SC_WINDOW):
                part = pl.ds(j * SC_WINDOW, SC_WINDOW)
                pltpu.sync_copy(y_hbm.at[da_vmem.at[0, part]], buf)
                pltpu.sync_copy(buf, ga_hbm.at[id_vmem.at[0, part]])
                pltpu.sync_copy(y_hbm.at[db_vmem.at[0, part]], buf)
                pltpu.sync_copy(buf, gb_hbm.at[id_vmem.at[0, part]])

        idx_spec = pl.BlockSpec((1, SC_INDEX_WINDOW), lambda i: (0, i))
        pltpu.emit_pipeline(
            body,
            grid=(t // SC_INDEX_WINDOW,),
            in_specs=[idx_spec, idx_spec, idx_spec],
            out_specs=[],
            core_axis_name=("core", "subcore"),
            dimension_semantics=(pltpu.PARALLEL,),
        )(id_hbm, da_hbm, db_hbm)

    return run(ys, jnp.arange(t, dtype=jnp.int32).reshape(1, t), dest_a, dest_b)


def _combine_dense_kernel(h_ref, rec_ref, ga_ref, gb_ref, *out_refs, split):
    i = pl.program_id(0)
    rec = rec_ref[...]
    res = h_ref[...] + rec[:, 4:5] * _unpack_halves(ga_ref[...]) + rec[:, 5:6] * _unpack_halves(gb_ref[...])
    if not split:
        out_refs[0][...] = res
    else:
        @pl.when(i < N_PROMPT_TILES)
        def _():
            out_refs[0][...] = res

        @pl.when((i >= N_PROMPT_TILES) & (i < N_PROMPT_TILES + N_SAMPLE_TILES))
        def _():
            out_refs[1][...] = res


def moe_combine_dense(h, rec, ga, gb, split=False):
    t, d = h.shape
    row = lambda i: (i, 0)
    if split:
        out_shape = (jax.ShapeDtypeStruct((T_PROMPT, d), F32), jax.ShapeDtypeStruct((T_SAMPLE, d), F32))
        out_specs = tuple(_parts_specs(d)[:2])
    else:
        out_shape = jax.ShapeDtypeStruct((t, d), F32)
        out_specs = pl.BlockSpec((ROW_TILE, d), row)
    return pl.pallas_call(
        functools.partial(_combine_dense_kernel, split=split),
        out_shape=out_shape,
        grid=(t // ROW_TILE,),
        in_specs=[pl.BlockSpec((ROW_TILE, d), row), pl.BlockSpec((ROW_TILE, LANES), row),
                  pl.BlockSpec((ROW_TILE, d // 2), row), pl.BlockSpec((ROW_TILE, d // 2), row)],
        out_specs=out_specs,
        compiler_params=_cparams(("arbitrary",)),
        name="moe_combine_dense",
    )(h, rec, ga, gb)


def _combine_kv_q_kernel(h_ref, rec_ref, ga_ref, gb_ref, gkv_ref, wkv_ref, gq_ref, wq_ref, hmean_ref,
                         hgk_ref, hgq_ref, cos_ref, sina_ref, sinb_ref, ho_ref, k_ref, v_ref, q_ref):
    rec = rec_ref[...]
    h = h_ref[...] + rec[:, 4:5] * _unpack_halves(ga_ref[...]) + rec[:, 5:6] * _unpack_halves(gb_ref[...])
    ho_ref[...] = h
    xhat = h * lax.rsqrt(jnp.mean(h * h, axis=-1, keepdims=True) + RMS_EPS)
    tables = (cos_ref[...], sina_ref[...], sinb_ref[...])
    zkv = jnp.dot((xhat * gkv_ref[...]).astype(BF16), wkv_ref[...], preferred_element_type=F32)
    k_ref[...] = _head_norm_rope(zkv[:, :KV_DIM], hmean_ref, hgk_ref[...], *tables)
    v_ref[...] = zkv[:, KV_DIM:]
    zq = jnp.dot((xhat * gq_ref[...]).astype(BF16), wq_ref[...], preferred_element_type=F32)
    q = _head_norm_rope(zq, hmean_ref, hgq_ref[...], *tables)
    q_ref[...] = (q * HEAD_DIM ** -0.5).astype(q_ref.dtype)


def moe_combine_kv_q(h, rec, ga, gb, kv_gain, kv_w_bf16, q_gain, wq_bf16, hmean, k_hgain, q_hgain,
                     cos_t, sina_t, sinb_t):
    t, d = h.shape
    row = lambda i: (i, 0)
    fix = lambda i: (0, 0)
    rope = pl.BlockSpec((ROW_TILE, LANES), _rope_tile)
    return pl.pallas_call(
        _combine_kv_q_kernel,
        out_shape=(jax.ShapeDtypeStruct((t, d), F32), jax.ShapeDtypeStruct((t, KV_DIM), F32),
                   jax.ShapeDtypeStruct((t, KV_DIM), F32), jax.ShapeDtypeStruct((t, d), BF16)),
        grid=(t // ROW_TILE,),
        in_specs=[pl.BlockSpec((ROW_TILE, d), row), pl.BlockSpec((ROW_TILE, LANES), row),
                  pl.BlockSpec((ROW_TILE, d // 2), row), pl.BlockSpec((ROW_TILE, d // 2), row),
                  pl.BlockSpec((1, d), fix), pl.BlockSpec((d, 2 * KV_DIM), fix),
                  pl.BlockSpec((1, d), fix), pl.BlockSpec((d, d), fix),
                  pl.BlockSpec((256, 256), fix), pl.BlockSpec((1, KV_DIM), fix), pl.BlockSpec((1, d), fix),
                  rope, rope, rope],
        out_specs=(pl.BlockSpec((ROW_TILE, d), row), pl.BlockSpec((ROW_TILE, KV_DIM), row),
                   pl.BlockSpec((ROW_TILE, KV_DIM), row), pl.BlockSpec((ROW_TILE, d), row)),
        compiler_params=_cparams(("parallel",)),
        name="moe_combine_kv_q",
    )(h, rec, ga, gb, kv_gain.reshape(1, d), kv_w_bf16, q_gain.reshape(1, d), wq_bf16, hmean, k_hgain, q_hgain,
      cos_t, sina_t, sinb_t)


def _ffn_kernel(wblk_ref, we_ref, wlo_ref, whi_ref, xs_ref, w13_ref, w2_ref, ys_ref, w13b, w2b):
    w = pl.program_id(0)
    prev = jnp.maximum(w - 1, 0)
    first_visit = (w == 0) | (wblk_ref[w] != wblk_ref[prev])
    lo = wlo_ref[w]
    hi = whi_ref[w]

    def ffn(x):
        x = _unpack_halves(x).astype(BF16)
        cw = D_EXPERT // FFN_CHUNKS
        gate_up = []
        for c in range(FFN_CHUNKS):
            a = jnp.dot(x, w13b[:, c * cw:(c + 1) * cw], preferred_element_type=F32)
            u = jnp.dot(x, w13b[:, D_EXPERT + c * cw:D_EXPERT + (c + 1) * cw], preferred_element_type=F32)
            gate_up.append((a, u))
        hmid = jnp.concatenate([(_silu(a) * u).astype(BF16) for a, u in gate_up], axis=1)
        return _pack_halves(jnp.dot(hmid, w2b[...], preferred_element_type=F32))

    @pl.when(hi > lo)
    def _():
        @pl.when((w == 0) | (we_ref[w] != we_ref[prev]))
        def _():
            w13b[...] = w13_ref[...].astype(BF16)
            w2b[...] = w2_ref[...].astype(BF16)

        whole = (lo == 0) & (hi == EXPERT_BLOCK)

        @pl.when(whole)
        def _():
            ys_ref[...] = ffn(xs_ref[...])

        half = EXPERT_BLOCK // 2
        for p in range(2):
            rows = slice(p * half, (p + 1) * half)
            touched = (lo < (p + 1) * half) & (hi > p * half)

            @pl.when(jnp.logical_not(whole) & touched)
            def _():
                y = ffn(xs_ref[rows, :])
                row = lax.broadcasted_iota(jnp.int32, y.shape, 0) + p * half
                mine = (row >= lo) & (row < hi)

                @pl.when(first_visit)
                def _():
                    ys_ref[rows, :] = jnp.where(mine, y, jnp.zeros_like(y))

                @pl.when(jnp.logical_not(first_visit))
                def _():
                    ys_ref[rows, :] = jnp.where(mine, y, ys_ref[rows, :])

            @pl.when(jnp.logical_not(whole) & jnp.logical_not(touched) & first_visit)
            def _():
                ys_ref[rows, :] = jnp.zeros((half, ys_ref.shape[1]), U32)


def moe_ffn(xs, work, w13_all, w2_all, layer):
    n_slots, dp = xs.shape
    d = 2 * dp
    n_work = work[0].shape[0]
    xmap = lambda w, wb, we, wlo, whi: (wb[w], 0)
    w_map = lambda w, wb, we, wlo, whi: (layer, we[w], 0, 0)
    grid_spec = pltpu.PrefetchScalarGridSpec(
        num_scalar_prefetch=4,
        grid=(n_work,),
        in_specs=[pl.BlockSpec((EXPERT_BLOCK, dp), xmap),
                  pl.BlockSpec((None, None, d, 2 * D_EXPERT), w_map),
                  pl.BlockSpec((None, None, D_EXPERT, d), w_map)],
        out_specs=pl.BlockSpec((EXPERT_BLOCK, dp), xmap),
        scratch_shapes=[pltpu.VMEM((d, 2 * D_EXPERT), BF16), pltpu.VMEM((D_EXPERT, d), BF16)],
    )
    return pl.pallas_call(
        _ffn_kernel,
        out_shape=jax.ShapeDtypeStruct((n_slots, dp), U32),
        grid_spec=grid_spec,
        compiler_params=_cparams(("arbitrary",)),
        name="moe_ffn",
    )(*work, xs, w13_all, w2_all)


def _ffn_work_items(cnt):
    n_slots = 2 * T_ALL
    n_blocks = n_slots // EXPERT_BLOCK
    n_work = n_blocks + N_EXPERTS - 1
    end = jnp.cumsum(cnt)
    start = end - cnt
    first_blk = start // EXPERT_BLOCK
    last_blk = jnp.maximum(end - 1, start) // EXPERT_BLOCK
    n_items = jnp.where(cnt > 0, last_blk - first_blk + 1, 0)
    item_end = jnp.cumsum(n_items)
    item_start = item_end - n_items
    w = jnp.arange(n_work, dtype=jnp.int32)
    used = w < item_end[-1]
    wq = jnp.minimum(w, item_end[-1] - 1)
    e = jnp.sum((item_end[:, None] <= wq[None, :]).astype(jnp.int32), axis=0)
    onehot = e[None, :] == jnp.arange(N_EXPERTS, dtype=jnp.int32)[:, None]
    of_e = lambda table: jnp.sum(jnp.where(onehot, table[:, None], 0), axis=0)
    blk = jnp.where(used, of_e(first_blk) + (w - of_e(item_start)), n_blocks - 1).astype(jnp.int32)
    lo = jnp.maximum(of_e(start), blk * EXPERT_BLOCK) - blk * EXPERT_BLOCK
    hi = jnp.minimum(of_e(end), (blk + 1) * EXPERT_BLOCK) - blk * EXPERT_BLOCK
    lo = jnp.where(used, lo, 0).astype(jnp.int32)
    hi = jnp.where(used, hi, 0).astype(jnp.int32)
    return start, (blk, e, lo, hi)


def _combine_kernel(dest_ref, h_ref, rec_ref, ys_ref, *rest, split):
    out_refs, (g1, g2, sem) = rest[:-3], rest[-3:]
    rows = h_ref.shape[0]
    i = pl.program_id(0)

    def issue(r, c):
        _row_copy(ys_ref, dest_ref[0, 0, r], g1, r, sem).start(priority=0)
        _row_copy(ys_ref, dest_ref[0, 0, rows + r], g2, r, sem).start(priority=1)
        return c

    lax.fori_loop(0, rows, issue, 0, unroll=8)
    for buf in (g1, g2):
        pltpu.make_async_copy(ys_ref.at[pl.ds(0, rows)], buf, sem).wait()
    rec = rec_ref[...]
    res = h_ref[...] + rec[:, 4:5] * g1[...] + rec[:, 5:6] * g2[...]
    if not split:
        out_refs[0][...] = res
    else:
        @pl.when(i < N_PROMPT_TILES)
        def _():
            out_refs[0][...] = res

        @pl.when((i >= N_PROMPT_TILES) & (i < N_PROMPT_TILES + N_SAMPLE_TILES))
        def _():
            out_refs[1][...] = res


def moe_combine(h, rec, ys, dest3, split=False):
    t, d = h.shape
    row = lambda i: (i, 0)
    if split:
        parts = _parts_specs(d)[:2]
        out_shape = (jax.ShapeDtypeStruct((T_PROMPT, d), F32), jax.ShapeDtypeStruct((T_SAMPLE, d), F32))
        out_specs = tuple(parts)
    else:
        out_shape = jax.ShapeDtypeStruct((t, d), F32)
        out_specs = pl.BlockSpec((ROW_TILE, d), row)
    return pl.pallas_call(
        functools.partial(_combine_kernel, split=split),
        out_shape=out_shape,
        grid=(t // ROW_TILE,),
        in_specs=[pl.BlockSpec((1, 1, 2 * ROW_TILE), lambda i: (i, 0, 0), memory_space=pltpu.SMEM),
                  pl.BlockSpec((ROW_TILE, d), row), pl.BlockSpec((ROW_TILE, LANES), row),
                  pl.BlockSpec(memory_space=pl.ANY)],
        out_specs=out_specs,
        scratch_shapes=[pltpu.VMEM((ROW_TILE, d), F32), pltpu.VMEM((ROW_TILE, d), F32),
                        pltpu.SemaphoreType.DMA],
        compiler_params=_cparams(("arbitrary",)),
        name="moe_combine",
    )(dest3, h, rec, ys)


def hier_moe_layer(a, w_out_bf16, x, layer, gain, w_group, b_group, w_expert, b_expert, w13_all, w2_all, utri,
                   finish):
    t = T_ALL
    pad = LANES - N_EXPERTS - N_GROUPS
    w_router = jnp.concatenate([w_expert, w_group, jnp.zeros((D_MODEL, pad), F32)], axis=1)
    b_router = jnp.concatenate([b_expert, b_group, jnp.zeros((pad,), F32)])[:ROUTE_ROWS].reshape(ROUTE_ROWS, 1)
    h, xn, rec, rect, counts = moe_route(a, w_out_bf16, x, gain, w_router, b_router, utri)

    cnt = counts[:N_EXPERTS, 0].astype(jnp.int32)
    start, work = _ffn_work_items(cnt)
    experts = jnp.arange(N_EXPERTS, dtype=jnp.int32)[:, None]

    def slot_of(e_row, rank_row):
        first = jnp.sum(jnp.where(e_row.astype(jnp.int32)[None, :] == experts, start[:, None], 0), axis=0)
        return (first + rank_row.astype(jnp.int32)).reshape(1, t)

    dest_a = slot_of(rect[0], rect[2])
    dest_b = slot_of(rect[1], rect[3])

    xs = moe_dispatch_sc(xn, dest_a, dest_b, 2 * t)
    ys = moe_ffn(xs, work, w13_all, w2_all, layer)
    ga, gb = moe_gather_sc(ys, dest_a, dest_b)
    return finish(h, rec, ga, gb)


def _rope_tables(pos):
    half = ROPE_DIM // 2
    lane = np.arange(LANES) % HEAD_DIM
    rotary = lane < ROPE_DIM
    inv = jnp.where(rotary, jnp.exp(-math.log(ROPE_THETA) * jnp.asarray(lane % half, F32) * (2.0 / ROPE_DIM)), 0.0)
    ang = pos.astype(F32)[:, None] * inv[None, :]
    cos, sin = jnp.cos(ang), jnp.sin(ang)
    first = jnp.asarray(lane < half)
    second = jnp.asarray(rotary & (lane >= half))
    return cos, jnp.where(first, -sin, 0.0), jnp.where(second, sin, 0.0)


def kernel(x_prompt, x_sample, state_hgrn, cache_k_win, cache_v_win, meta_tokens, a_norm, a_w_in, a_lower_logits, a_out_norm, a_w_out, kv_norm, kv_w, k_norm, b_norm, b_wq, b_q_norm, b_sinks, b_w_out, moe_norm, moe_w_group, moe_b_group, moe_w_expert, moe_b_expert, moe_w13, moe_w2):
    tail_rows = T_ALL - OFF_META
    x_parts = (x_prompt.reshape(T_PROMPT, D_MODEL), x_sample.reshape(T_SAMPLE, D_MODEL),
               jnp.concatenate([meta_tokens.astype(F32), jnp.zeros((tail_rows - N_META, D_MODEL), F32)], axis=0))
    pos = jnp.concatenate([N_META + jnp.arange(SEQ, dtype=jnp.int32),
                           jnp.tile(PAST_LEN + jnp.arange(DEC_SEQ, dtype=jnp.int32), ROW_TILE // DEC_SEQ),
                           jnp.arange(N_META, dtype=jnp.int32),
                           jnp.zeros((ROW_TILE - N_META,), jnp.int32)])
    cos_t, sina_t, sinb_t = _rope_tables(pos)
    r256 = np.arange(256)
    hmean = jnp.asarray((r256[:, None] // HEAD_DIM == r256[None, :] // HEAD_DIM).astype(np.float32) / HEAD_DIM, BF16)
    rt = np.arange(ROW_TILE)
    utri = jnp.asarray((rt[:, None] < rt[None, :]).astype(np.float32), BF16)
    lower = jnp.cumsum(jax.nn.softmax(a_lower_logits.astype(F32), axis=0), axis=0)

    moe = functools.partial(hier_moe_layer, w13_all=moe_w13, w2_all=moe_w2, utri=utri)

    z = in_project(x_parts, a_norm[0], a_w_in[0].astype(BF16))
    zero_state = jnp.zeros((1, A_HEADS, A_DK, A_DV), F32)
    o_meta, s_meta = hgrn2_scan(z, zero_state, lower[0], a_out_norm[0],
                                row_off=OFF_META, n_seq=1, seq_len=N_META)
    o_prompt, s_prompt = hgrn2_scan(z, s_meta, lower[0], a_out_norm[0], row_off=0, n_seq=BATCH, seq_len=SEQ)
    o_sample, s_sample = hgrn2_scan(z, state_hgrn[0].astype(F32), lower[0], a_out_norm[0],
                                    row_off=OFF_SAMPLE, n_seq=DEC_BATCH, seq_len=DEC_SEQ, group=SCAN_SAMPLE_GROUP)
    o_tail = jnp.concatenate([o_meta, jnp.zeros((tail_rows - N_META, D_MODEL), BF16)], axis=0)
    finish0 = functools.partial(
        moe_combine_kv_q, kv_gain=kv_norm, kv_w_bf16=kv_w.astype(BF16), q_gain=b_norm[0], wq_bf16=b_wq[0].astype(BF16),
        hmean=hmean, k_hgain=jnp.tile(k_norm, KV_HEADS).reshape(1, KV_DIM),
        q_hgain=jnp.tile(b_q_norm[0], Q_HEADS).reshape(1, D_MODEL), cos_t=cos_t, sina_t=sina_t, sinb_t=sinb_t)
    h, k_all, v_all, q_all = moe((o_prompt, o_sample, o_tail), a_w_out[0].astype(BF16), x_parts, 0, moe_norm[0],
                                 moe_w_group[0], moe_b_group[0], moe_w_expert[0], moe_b_expert[0], finish=finish0)

    meta_blk = lambda a: jnp.concatenate([jnp.zeros((ATT_BLOCK - N_META, KV_DIM), F32),
                                          a[OFF_META:OFF_META + N_META]], axis=0)
    sinks = b_sinks[0].astype(F32)
    att_all = attention_prompt(q_all, k_all, v_all, meta_blk(k_all), meta_blk(v_all), sinks)
    att_all, k_win_s, v_win_s = attention_sample(q_all, cache_k_win.astype(F32), cache_v_win.astype(F32),
                                                 k_all, v_all, sinks, att_all)
    y_p, y_s = moe(att_all, b_w_out[0].astype(BF16), h, 1, moe_norm[1], moe_w_group[1], moe_b_group[1],
                   moe_w_expert[1], moe_b_expert[1], finish=functools.partial(moe_combine_dense, split=True))

    y_prompt = y_p.reshape(BATCH, SEQ, D_MODEL)
    y_sample = y_s.reshape(DEC_BATCH, DEC_SEQ, D_MODEL)
    last = lambda a: jnp.stack([a[(b + 1) * SEQ - WINDOW:(b + 1) * SEQ] for b in range(BATCH)]).reshape(
        BATCH, WINDOW, KV_HEADS, HEAD_DIM)
    kp = last(k_all)
    vp = last(v_all)
    return (y_prompt, y_sample, s_prompt[None], s_sample[None], kp, vp, k_win_s, v_win_s)
```

```python
import functools
import math

import numpy as np
import jax
import jax.numpy as jnp
from jax import lax
from jax.experimental import pallas as pl
from jax.experimental.pallas import tpu as pltpu
from jax.experimental.pallas import tpu_sc as plsc

F32 = jnp.float32
BF16 = jnp.bfloat16
U32 = jnp.uint32

D_MODEL = 1024
BATCH = 4
SEQ = 4096
DEC_BATCH = 128
DEC_SEQ = 8
PAST_LEN = 8192
N_META = 16
A_HEADS = 8
A_DK = 128
A_DV = 128
Q_HEADS = 16
KV_HEADS = 4
HEAD_DIM = 64
KV_DIM = KV_HEADS * HEAD_DIM
WINDOW = 128
ROPE_DIM = 16
ROPE_THETA = 500000.0
N_GROUPS = 4
EXPERTS_PER_GROUP = 8
N_EXPERTS = 32
D_EXPERT = 512
RMS_EPS = 1e-6

LANES = 128
SUBLANES = 8
VMEM_LIMIT = 56 * 1024 * 1024

ROW_TILE = 512
T_PROMPT = BATCH * SEQ
T_SAMPLE = DEC_BATCH * DEC_SEQ
OFF_SAMPLE = T_PROMPT
OFF_META = T_PROMPT + T_SAMPLE
T_REAL = OFF_META + N_META
T_ALL = -(-T_REAL // ROW_TILE) * ROW_TILE
N_TILES = T_ALL // ROW_TILE

SCAN_CHUNK = 128
SCAN_SAMPLE_GROUP = 8
ATT_BLOCK = 128
EXPERT_BLOCK = 512
FFN_CHUNKS = 2


def _cparams(sem):
    return pltpu.CompilerParams(dimension_semantics=sem, vmem_limit_bytes=VMEM_LIMIT)


def _nt_dot(a, b):
    return lax.dot_general(a, b, (((1,), (1,)), ((), ())), preferred_element_type=F32)


def _rms(x, gain):
    ms = jnp.mean(x * x, axis=-1, keepdims=True)
    return x * lax.rsqrt(ms + RMS_EPS) * gain


def _silu(x):
    return x * jax.nn.sigmoid(x)


def _pack_halves(x):
    w = x.shape[1] // 2
    hi = lax.bitcast_convert_type(x[:, :w].astype(BF16).astype(F32), U32)
    lo = lax.bitcast_convert_type(x[:, w:].astype(BF16).astype(F32), U32)
    return hi | (lo >> 16)


def _unpack_halves(p):
    hi = lax.bitcast_convert_type(p & jnp.uint32(0xFFFF0000), F32)
    lo = lax.bitcast_convert_type(p << 16, F32)
    return jnp.concatenate([hi, lo], axis=1)


N_PROMPT_TILES = T_PROMPT // ROW_TILE
N_SAMPLE_TILES = T_SAMPLE // ROW_TILE


def _parts_specs(width):
    return [pl.BlockSpec((ROW_TILE, width), lambda i: (jnp.minimum(i, N_PROMPT_TILES - 1), 0)),
            pl.BlockSpec((ROW_TILE, width), lambda i: (jnp.clip(i - N_PROMPT_TILES, 0, N_SAMPLE_TILES - 1), 0)),
            pl.BlockSpec((ROW_TILE, width), lambda i: (0, 0))]


def _pick_part(i, p_ref, s_ref, t_ref, dtype):
    return jnp.where(i < N_PROMPT_TILES, p_ref[...].astype(dtype),
                     jnp.where(i < N_PROMPT_TILES + N_SAMPLE_TILES, s_ref[...].astype(dtype),
                               t_ref[...].astype(dtype)))


def _in_proj_kernel(xp_ref, xs_ref, xt_ref, g_ref, w_ref, o_ref):
    x = _pick_part(pl.program_id(0), xp_ref, xs_ref, xt_ref, F32)
    xn = _rms(x, g_ref[...])
    o_ref[...] = jnp.dot(xn.astype(BF16), w_ref[...], preferred_element_type=F32)


def in_project(x_parts, gain, w_bf16):
    d, n = w_bf16.shape
    return pl.pallas_call(
        _in_proj_kernel,
        out_shape=jax.ShapeDtypeStruct((T_ALL, n), F32),
        grid=(N_TILES,),
        in_specs=_parts_specs(d) + [pl.BlockSpec((1, d), lambda i: (0, 0)),
                                    pl.BlockSpec((d, n), lambda i: (0, 0))],
        out_specs=pl.BlockSpec((ROW_TILE, n), lambda i: (i, 0)),
        compiler_params=_cparams(("parallel",)),
        name="in_project",
    )(*x_parts, gain.reshape(1, d), w_bf16)


def _mixer_out_kernel(ap_ref, as_ref, at_ref, w_ref, xp_ref, xs_ref, xt_ref, o_ref):
    i = pl.program_id(0)
    a = _pick_part(i, ap_ref, as_ref, at_ref, BF16)
    x = _pick_part(i, xp_ref, xs_ref, xt_ref, F32)
    o_ref[...] = x + jnp.dot(a, w_ref[...], preferred_element_type=F32)


def mixer_out(a_parts, w_bf16, x_parts):
    k, n = w_bf16.shape
    return pl.pallas_call(
        _mixer_out_kernel,
        out_shape=jax.ShapeDtypeStruct((T_ALL, n), F32),
        grid=(N_TILES,),
        in_specs=_parts_specs(k) + [pl.BlockSpec((k, n), lambda i: (0, 0))] + _parts_specs(n),
        out_specs=pl.BlockSpec((ROW_TILE, n), lambda i: (i, 0)),
        compiler_params=_cparams(("parallel",)),
        name="mixer_out",
    )(*a_parts, w_bf16, *x_parts)


def _head_norm_rope(y, hmean_ref, hgain, cos_t, sina_t, sinb_t):
    rows, width = y.shape
    sq = (y * y).astype(BF16)
    parts = []
    for s in range(width // 256):
        parts.append(jnp.dot(sq[:, s * 256:(s + 1) * 256], hmean_ref[...], preferred_element_type=F32))
    ms = parts[0] if len(parts) == 1 else jnp.concatenate(parts, axis=1)
    yn = y * lax.rsqrt(ms + RMS_EPS) * hgain
    reps = width // LANES
    cos_w = jnp.concatenate([cos_t] * reps, axis=1)
    sina_w = jnp.concatenate([sina_t] * reps, axis=1)
    sinb_w = jnp.concatenate([sinb_t] * reps, axis=1)
    half = ROPE_DIM // 2
    nxt = pltpu.roll(yn, width - half, 1)
    prv = pltpu.roll(yn, half, 1)
    return yn * cos_w + nxt * sina_w + prv * sinb_w


def _kv_kernel(x_ref, g_ref, w_ref, hmean_ref, hg_ref, cos_ref, sina_ref, sinb_ref, k_ref, v_ref):
    xn = _rms(x_ref[...], g_ref[...])
    z = jnp.dot(xn.astype(BF16), w_ref[...], preferred_element_type=F32)
    k = _head_norm_rope(z[:, :KV_DIM], hmean_ref, hg_ref[...], cos_ref[...], sina_ref[...], sinb_ref[...])
    k_ref[...] = k
    v_ref[...] = z[:, KV_DIM:]


def _rope_tile(i):
    tiles_per_seq = SEQ // ROW_TILE
    n_prompt_tiles = T_PROMPT // ROW_TILE
    n_sample_tiles = T_SAMPLE // ROW_TILE
    return (jnp.where(i < n_prompt_tiles, i % tiles_per_seq,
                      jnp.where(i < n_prompt_tiles + n_sample_tiles, tiles_per_seq, tiles_per_seq + 1)), 0)


def kv_project(x, gain, w_bf16, hmean, hgain_w, cos_t, sina_t, sinb_t):
    t, d = x.shape
    row = lambda i: (i, 0)
    fix = lambda i: (0, 0)
    return pl.pallas_call(
        _kv_kernel,
        out_shape=(jax.ShapeDtypeStruct((t, KV_DIM), F32), jax.ShapeDtypeStruct((t, KV_DIM), F32)),
        grid=(t // ROW_TILE,),
        in_specs=[pl.BlockSpec((ROW_TILE, d), row), pl.BlockSpec((1, d), fix),
                  pl.BlockSpec((d, 2 * KV_DIM), fix), pl.BlockSpec((256, 256), fix),
                  pl.BlockSpec((1, KV_DIM), fix),
                  pl.BlockSpec((ROW_TILE, LANES), _rope_tile), pl.BlockSpec((ROW_TILE, LANES), _rope_tile),
                  pl.BlockSpec((ROW_TILE, LANES), _rope_tile)],
        out_specs=(pl.BlockSpec((ROW_TILE, KV_DIM), row), pl.BlockSpec((ROW_TILE, KV_DIM), row)),
        compiler_params=_cparams(("parallel",)),
        name="kv_project",
    )(x, gain.reshape(1, d), w_bf16, hmean, hgain_w, cos_t, sina_t, sinb_t)


def _q_kernel(x_ref, g_ref, w_ref, hmean_ref, hg_ref, cos_ref, sina_ref, sinb_ref, q_ref):
    xn = _rms(x_ref[...], g_ref[...])
    z = jnp.dot(xn.astype(BF16), w_ref[...], preferred_element_type=F32)
    q = _head_norm_rope(z, hmean_ref, hg_ref[...], cos_ref[...], sina_ref[...], sinb_ref[...])
    q_ref[...] = (q * HEAD_DIM ** -0.5).astype(q_ref.dtype)


def q_project(x, gain, w_bf16, hmean, hgain_w, cos_t, sina_t, sinb_t):
    t, d = x.shape
    row = lambda i: (i, 0)
    fix = lambda i: (0, 0)
    return pl.pallas_call(
        _q_kernel,
        out_shape=jax.ShapeDtypeStruct((t, d), BF16),
        grid=(t // ROW_TILE,),
        in_specs=[pl.BlockSpec((ROW_TILE, d), row), pl.BlockSpec((1, d), fix),
                  pl.BlockSpec((d, d), fix), pl.BlockSpec((256, 256), fix),
                  pl.BlockSpec((1, d), fix),
                  pl.BlockSpec((ROW_TILE, LANES), _rope_tile), pl.BlockSpec((ROW_TILE, LANES), _rope_tile),
                  pl.BlockSpec((ROW_TILE, LANES), _rope_tile)],
        out_specs=pl.BlockSpec((ROW_TILE, d), row),
        compiler_params=_cparams(("parallel",)),
        name="q_project",
    )(x, gain.reshape(1, d), w_bf16, hmean, hgain_w, cos_t, sina_t, sinb_t)


def _matmul_residual_kernel(a_ref, w_ref, r_ref, o_ref):
    o_ref[...] = r_ref[...] + jnp.dot(a_ref[...], w_ref[...], preferred_element_type=F32)


def matmul_residual(a_bf16, w_bf16, resid):
    t, k = a_bf16.shape
    n = w_bf16.shape[1]
    return pl.pallas_call(
        _matmul_residual_kernel,
        out_shape=jax.ShapeDtypeStruct((t, n), F32),
        grid=(t // ROW_TILE,),
        in_specs=[pl.BlockSpec((ROW_TILE, k), lambda i: (i, 0)),
                  pl.BlockSpec((k, n), lambda i: (0, 0)),
                  pl.BlockSpec((ROW_TILE, n), lambda i: (i, 0))],
        out_specs=pl.BlockSpec((ROW_TILE, n), lambda i: (i, 0)),
        compiler_params=_cparams(("parallel",)),
        name="matmul_residual",
    )(a_bf16, w_bf16, resid)


def _scan_levels(c):
    levels = []
    m = c
    while m >= 2:
        levels.append(m)
        m //= 2
    return levels


LOG2E = 1.4426950408889634


def _scan_kernel(z_ref, s0_ref, lb_ref, og_ref, tri_ref, lmask_ref, sgn_ref, o_ref, sfin_ref, s_scr, b_scr,
                 *, rows, seq_len):
    c_idx = pl.program_id(1)
    levels = _scan_levels(seq_len)
    n_sub = rows // seq_len
    hk = A_HEADS * A_DK

    @pl.when(c_idx == 0)
    def _():
        s_scr[...] = s0_ref[...]

    sub = lax.broadcasted_iota(jnp.int32, (SUBLANES, LANES), 0)
    row = lax.broadcasted_iota(jnp.int32, (LANES, LANES), 0)
    og = og_ref[...]

    def pad_f32(x):
        if x.shape[0] == LANES:
            return x
        return jnp.concatenate([x, jnp.zeros((LANES - x.shape[0], x.shape[1]), x.dtype)], axis=0)

    def pad_rows(x):
        return pad_f32(x).astype(BF16)

    def cols(part, h):
        return slice(part * hk + h * LANES, part * hk + (h + 1) * LANES)

    def gates(h):
        lb = lb_ref[:, cols(0, h)]
        forget = lb + (1.0 - lb) * jax.nn.sigmoid(z_ref[:, cols(1, h)])
        logf = jnp.log(forget)
        hi = logf.astype(BF16).astype(F32)
        r1 = logf - hi
        mid = r1.astype(BF16).astype(F32)
        lo = r1 - mid
        cs = jnp.dot(tri_ref[...], pad_rows(jnp.concatenate([hi, mid, lo], axis=1)),
                     preferred_element_type=F32)
        b = (cs[:rows, :LANES] + cs[:rows, LANES:2 * LANES]) + cs[:rows, 2 * LANES:]
        b_scr[h] = b
        return _silu(z_ref[:, cols(0, h)]), 1.0 - forget, b

    def bref_for(h, m):
        b_rows = b_scr.at[h]
        half = m // 2
        pieces = []
        for g in range(rows // SUBLANES):
            base = g * SUBLANES
            if m >= SUBLANES:
                r = (base // m) * m + half - 1
                piece = jnp.broadcast_to(b_rows[r:r + 1, :], (SUBLANES, LANES))
            else:
                piece = jnp.broadcast_to(b_rows[base + half - 1:base + half, :], (SUBLANES, LANES))
                for blk in range(1, SUBLANES // m):
                    r = base + blk * m + half - 1
                    piece = jnp.where(sub >= blk * m,
                                      jnp.broadcast_to(b_rows[r:r + 1, :], (SUBLANES, LANES)), piece)
            pieces.append(piece)
        return pieces[0] if len(pieces) == 1 else jnp.concatenate(pieces, axis=0)

    heads = range(A_HEADS)
    qkb = [gates(h) for h in heads]
    att = [_nt_dot(pad_rows(qf), pad_rows(kf)) * lmask_ref[len(levels)] for qf, kf, _ in qkb]
    for li, m in enumerate(levels):
        for h in heads:
            qf, kf, b = qkb[h]
            sgn = sgn_ref[li]
            e = jnp.exp2((b - bref_for(h, m)) * sgn)
            w = pad_rows(jnp.where(sgn > 0, qf, kf) * e)
            att[h] = att[h] + _nt_dot(w, w) * lmask_ref[li]

    def finish(h):
        qf, kf, b = qkb[h]
        b_rows = b_scr.at[h]
        v_b = pad_rows(z_ref[:, cols(2, h)])
        o_intra = jnp.dot(att[h].astype(BF16), v_b, preferred_element_type=F32)
        eb = jnp.exp(b)
        qs = qf * eb
        b_end = [jnp.broadcast_to(b_rows[(i + 1) * seq_len - 1:(i + 1) * seq_len, :], (seq_len, LANES))
                 for i in range(n_sub)]
        b_end = b_end[0] if n_sub == 1 else jnp.concatenate(b_end, axis=0)
        kd_t = pad_f32(kf * jnp.exp(b_end - b)).T.astype(BF16)
        eb_t = pad_f32(eb).T
        qs_b = pad_rows(qs)
        o = o_intra
        for i in range(n_sub):
            s_old = s_scr[i, h]
            first, last = i * seq_len, (i + 1) * seq_len - 1
            if n_sub == 1:
                qs_i, v_i = qs_b, v_b
            else:
                mine = (row >= first) & (row <= last)
                qs_i = jnp.where(mine, qs_b, jnp.zeros_like(qs_b))
                v_i = jnp.where(mine, v_b, jnp.zeros_like(v_b))
            o = o + jnp.dot(qs_i, s_old.astype(BF16), preferred_element_type=F32)
            decay = jnp.broadcast_to(eb_t[:, last:last + 1], (LANES, LANES))
            s_scr[i, h] = decay * s_old + jnp.dot(kd_t, v_i, preferred_element_type=F32)
        o = o[:rows]

        on = _rms(o, og) * _silu(z_ref[:, cols(3, h)])
        o_ref[:, cols(0, h)] = on.astype(o_ref.dtype)

    for h in heads:
        finish(h)

    @pl.when(c_idx == pl.num_programs(1) - 1)
    def _():
        sfin_ref[...] = s_scr[...]


def _scan_consts(rows, seq_len):
    levels = _scan_levels(seq_len)
    r = np.arange(LANES)
    t, s = r[:, None], r[None, :]
    live = (t < rows) & (s < rows)
    tri = ((s <= t) & (t // seq_len == s // seq_len) & live).astype(np.float32)
    masks, sgns = [], []
    for m in levels:
        masks.append(((t // m == s // m) & (t % m >= m // 2) & (s % m < m // 2) & live).astype(np.float32))
        sgns.append(np.broadcast_to(np.where(r[:rows, None] % m >= m // 2, LOG2E, -LOG2E), (rows, LANES)))
    masks.append(((t == s) & live).astype(np.float32))
    return jnp.asarray(tri, BF16), jnp.asarray(np.stack(masks), F32), jnp.asarray(np.stack(sgns), F32)


def hgrn2_scan(z, s0, lb, o_gain, *, row_off, n_seq, seq_len, group=1):
    hv = A_HEADS * A_DV
    if seq_len > SCAN_CHUNK:
        assert group == 1
        sub_len, rows, n_chunks, n_steps = SCAN_CHUNK, SCAN_CHUNK, seq_len // SCAN_CHUNK, n_seq
    else:
        sub_len, rows, n_chunks, n_steps = seq_len, group * seq_len, 1, n_seq // group
    blk_off = row_off // rows
    tri, lmask, sgn = _scan_consts(rows, sub_len)
    shared_s0 = s0.shape[0] == 1
    fix2 = lambda s, c: (0, 0)
    fix3 = lambda s, c: (0, 0, 0)
    o, sfin = pl.pallas_call(
        functools.partial(_scan_kernel, rows=rows, seq_len=sub_len),
        out_shape=(jax.ShapeDtypeStruct((n_seq * seq_len, hv), BF16 if rows % 16 == 0 else F32),
                   jax.ShapeDtypeStruct((n_seq, A_HEADS, A_DK, A_DV), F32)),
        grid=(n_steps, n_chunks),
        in_specs=[pl.BlockSpec((rows, 4 * hv), lambda s, c: (blk_off + s * n_chunks + c, 0)),
                  pl.BlockSpec((group, A_HEADS, A_DK, A_DV), (lambda s, c: (0, 0, 0, 0)) if shared_s0
                               else (lambda s, c: (s, 0, 0, 0))),
                  pl.BlockSpec((1, hv), fix2), pl.BlockSpec((1, A_DV), fix2),
                  pl.BlockSpec((LANES, LANES), fix2), pl.BlockSpec(lmask.shape, fix3),
                  pl.BlockSpec(sgn.shape, fix3)],
        out_specs=(pl.BlockSpec((rows, hv), lambda s, c: (s * n_chunks + c, 0)),
                   pl.BlockSpec((group, A_HEADS, A_DK, A_DV), lambda s, c: (s, 0, 0, 0))),
        scratch_shapes=[pltpu.VMEM((group, A_HEADS, A_DK, A_DV), F32), pltpu.VMEM((A_HEADS, rows, LANES), F32)],
        compiler_params=_cparams(("parallel", "arbitrary")),
        name=f"hgrn2_scan_r{rows}",
    )(z, s0, lb.reshape(1, hv), o_gain.reshape(1, A_DV), tri, lmask, sgn)
    return o, sfin


KEYS = 2 * ATT_BLOCK


def _pair_operand(x, kh):
    slab = x[:, (kh // 2) * LANES:(kh // 2 + 1) * LANES]
    lane = lax.broadcasted_iota(jnp.int32, slab.shape, 1)
    if kh % 2 == 0:
        lo = jnp.where(lane < HEAD_DIM, slab, 0.0)
        hi = pltpu.roll(lo, HEAD_DIM, 1)
    else:
        hi = jnp.where(lane >= HEAD_DIM, slab, 0.0)
        lo = pltpu.roll(hi, HEAD_DIM, 1)
    return jnp.concatenate([lo, hi], axis=0).astype(BF16)


def _window_bias(rows, jmin):
    t_i = lax.broadcasted_iota(jnp.int32, (rows, 2 * KEYS), 0)
    c_i = lax.broadcasted_iota(jnp.int32, (rows, 2 * KEYS), 1)
    j_i = c_i & (ATT_BLOCK - 1)
    own = (c_i & ATT_BLOCK) != 0
    ok = (own & (j_i <= t_i)) | (jnp.logical_not(own) & (j_i >= t_i) & (j_i >= jmin))
    return jnp.where(ok, 0.0, -jnp.inf).astype(F32)


def _pair_softmax(s, sink_a, sink_b):
    probs, rinv = [], []
    for hh, sink in enumerate((sink_a, sink_b)):
        sh = s[:, hh * KEYS:(hh + 1) * KEYS]
        m = jnp.maximum(jnp.max(sh, axis=-1, keepdims=True), sink)
        p = jnp.exp(sh - m)
        den = jnp.sum(p, axis=-1, keepdims=True) + jnp.exp(sink - m)
        probs.append(p.astype(BF16))
        rinv.append(1.0 / den)
    lane = lax.broadcasted_iota(jnp.int32, (s.shape[0], LANES), 1)
    return jnp.concatenate(probs, axis=1), jnp.where(lane < HEAD_DIM, rinv[0], rinv[1])


def _attn_prompt_kernel(sink_ref, q_ref, kp_ref, ko_ref, vp_ref, vo_ref, km_ref, vm_ref, o_ref,
                        k2_scr, v2_scr, s_scr, p_scr, r_scr):
    n = pl.program_id(0)
    nbp = SEQ // ATT_BLOCK
    n_pairs = Q_HEADS // 2

    @pl.when(n >= BATCH * nbp)
    def _():
        o_ref[...] = jnp.zeros_like(o_ref)

    @pl.when(n < BATCH * nbp)
    def _():
        first = (n % nbp) == 0
        jmin = jnp.where(first, ATT_BLOCK - N_META, 0)
        k = jnp.concatenate([jnp.where(first, km_ref[...], kp_ref[...]), ko_ref[...]], axis=0)
        v = jnp.concatenate([jnp.where(first, vm_ref[...], vp_ref[...]), vo_ref[...]], axis=0)
        bias = _window_bias(ATT_BLOCK, jmin)
        for kh in range(KV_HEADS):
            k2_scr[kh] = _pair_operand(k, kh)
            v2_scr[kh] = _pair_operand(v, kh)
        for pair in range(n_pairs):
            s_scr[pair] = _nt_dot(q_ref[:, pair * LANES:(pair + 1) * LANES], k2_scr[pair // 2]) + bias
        for pair in range(n_pairs):
            p, rinv = _pair_softmax(s_scr[pair], sink_ref[2 * pair], sink_ref[2 * pair + 1])
            p_scr[pair] = p
            r_scr[pair] = rinv
        for pair in range(n_pairs):
            o = jnp.dot(p_scr[pair], v2_scr[pair // 2], preferred_element_type=F32) * r_scr[pair]
            o_ref[:, pair * LANES:(pair + 1) * LANES] = o.astype(o_ref.dtype)


def attention_prompt(q_all, k_all, v_all, k_meta_blk, v_meta_blk, sinks):
    n_prompt_blocks = T_PROMPT // ATT_BLOCK
    n_blocks = T_ALL // ATT_BLOCK
    n_pairs = Q_HEADS // 2
    own = lambda n, sk: (jnp.minimum(n, n_prompt_blocks - 1), 0)
    prev = lambda n, sk: (jnp.maximum(jnp.minimum(n, n_prompt_blocks - 1) - 1, 0), 0)
    fix = lambda n, sk: (0, 0)
    grid_spec = pltpu.PrefetchScalarGridSpec(
        num_scalar_prefetch=1,
        grid=(n_blocks,),
        in_specs=[pl.BlockSpec((ATT_BLOCK, D_MODEL), own),
                  pl.BlockSpec((ATT_BLOCK, KV_DIM), prev), pl.BlockSpec((ATT_BLOCK, KV_DIM), own),
                  pl.BlockSpec((ATT_BLOCK, KV_DIM), prev), pl.BlockSpec((ATT_BLOCK, KV_DIM), own),
                  pl.BlockSpec((ATT_BLOCK, KV_DIM), fix), pl.BlockSpec((ATT_BLOCK, KV_DIM), fix)],
        out_specs=pl.BlockSpec((ATT_BLOCK, D_MODEL), lambda n, sk: (n, 0)),
        scratch_shapes=[pltpu.VMEM((KV_HEADS, 2 * KEYS, LANES), BF16), pltpu.VMEM((KV_HEADS, 2 * KEYS, LANES), BF16),
                        pltpu.VMEM((n_pairs, ATT_BLOCK, 2 * KEYS), F32),
                        pltpu.VMEM((n_pairs, ATT_BLOCK, 2 * KEYS), BF16),
                        pltpu.VMEM((n_pairs, ATT_BLOCK, LANES), F32)],
    )
    return pl.pallas_call(
        _attn_prompt_kernel,
        out_shape=jax.ShapeDtypeStruct((T_ALL, D_MODEL), BF16),
        grid_spec=grid_spec,
        compiler_params=_cparams(("parallel",)),
        name="attention_prompt",
    )(sinks, q_all, k_all, k_all, v_all, v_all, k_meta_blk, v_meta_blk)


SAMPLE_GROUP = ATT_BLOCK // DEC_SEQ


def _attn_sample_kernel(sink_ref, q_ref, ck_ref, cv_ref, kn_ref, vn_ref, buf_ref, o_ref, kw_ref, vw_ref,
                        qf_scr, of_scr, k2_scr, v2_scr):
    del buf_ref
    qrows = 2 * DEC_SEQ
    qf_scr[...] = q_ref[...].astype(F32)
    bias = _window_bias(qrows, 0)
    zq = jnp.zeros((qrows - DEC_SEQ, D_MODEL), F32)
    zk = jnp.zeros((ATT_BLOCK - DEC_SEQ, KV_DIM), F32)

    def seq_body(i, carry):
        r_new = pl.multiple_of(i * DEC_SEQ, DEC_SEQ)
        q = jnp.concatenate([qf_scr[pl.ds(r_new, DEC_SEQ), :], zq], axis=0).astype(BF16)
        old = lambda c_ref: jnp.concatenate([c_ref[i, :, kh, :] for kh in range(KV_HEADS)], axis=1)
        k = jnp.concatenate([old(ck_ref), kn_ref[pl.ds(r_new, DEC_SEQ), :], zk], axis=0)
        v = jnp.concatenate([old(cv_ref), vn_ref[pl.ds(r_new, DEC_SEQ), :], zk], axis=0)
        for c_ref, n_ref, w_ref in ((ck_ref, kn_ref, kw_ref), (cv_ref, vn_ref, vw_ref)):
            w_ref[i, 0:WINDOW - DEC_SEQ] = c_ref[i, DEC_SEQ:WINDOW]
            for kh in range(KV_HEADS):
                w_ref[i, WINDOW - DEC_SEQ:WINDOW, kh, :] = n_ref[pl.ds(r_new, DEC_SEQ),
                                                                 kh * HEAD_DIM:(kh + 1) * HEAD_DIM]
        for kh in range(KV_HEADS):
            k2_scr[kh] = _pair_operand(k, kh)
            v2_scr[kh] = _pair_operand(v, kh)
        scores = [_nt_dot(q[:, pair * LANES:(pair + 1) * LANES], k2_scr[pair // 2]) + bias
                  for pair in range(Q_HEADS // 2)]
        soft = [_pair_softmax(s, sink_ref[2 * pair], sink_ref[2 * pair + 1]) for pair, s in enumerate(scores)]
        for pair, (p, rinv) in enumerate(soft):
            o = jnp.dot(p, v2_scr[pair // 2], preferred_element_type=F32) * rinv
            of_scr[pl.ds(r_new, DEC_SEQ), pair * LANES:(pair + 1) * LANES] = o[:DEC_SEQ]
        return carry

    lax.fori_loop(0, SAMPLE_GROUP, seq_body, 0)
    o_ref[...] = of_scr[...].astype(o_ref.dtype)


def attention_sample(q_all, cache_k, cache_v, k_all, v_all, sinks, out_buf):
    first_blk = OFF_SAMPLE // ATT_BLOCK
    new = lambda g, sk: (first_blk + g, 0)
    old = pl.BlockSpec((SAMPLE_GROUP, WINDOW, KV_HEADS, HEAD_DIM), lambda g, sk: (g, 0, 0, 0))
    grid_spec = pltpu.PrefetchScalarGridSpec(
        num_scalar_prefetch=1,
        grid=(DEC_BATCH // SAMPLE_GROUP,),
        in_specs=[pl.BlockSpec((ATT_BLOCK, D_MODEL), new),
                  old, old,
                  pl.BlockSpec((ATT_BLOCK, KV_DIM), new), pl.BlockSpec((ATT_BLOCK, KV_DIM), new),
                  pl.BlockSpec(memory_space=pl.ANY)],
        out_specs=(pl.BlockSpec((ATT_BLOCK, D_MODEL), new), old, old),
        scratch_shapes=[pltpu.VMEM((ATT_BLOCK, D_MODEL), F32), pltpu.VMEM((ATT_BLOCK, D_MODEL), F32),
                        pltpu.VMEM((KV_HEADS, 2 * KEYS, LANES), BF16), pltpu.VMEM((KV_HEADS, 2 * KEYS, LANES), BF16)],
    )
    window = jax.ShapeDtypeStruct(cache_k.shape, cache_k.dtype)
    return pl.pallas_call(
        _attn_sample_kernel,
        out_shape=(jax.ShapeDtypeStruct(out_buf.shape, out_buf.dtype), window, window),
        grid_spec=grid_spec,
        input_output_aliases={6: 0},
        compiler_params=_cparams(("parallel",)),
        name="attention_sample",
    )(sinks, q_all, cache_k, cache_v, k_all, v_all, out_buf)


ROUTE_COLS = 8
ROUTE_ROWS = 48


def _route_kernel(*refs, parts):
    i = pl.program_id(0)
    if parts:
        (ap_ref, as_ref, at_ref, w_ref, xp_ref, xs_ref, xt_ref), refs = refs[:7], refs[7:]
        a = _pick_part(i, ap_ref, as_ref, at_ref, BF16)
        x = _pick_part(i, xp_ref, xs_ref, xt_ref, F32)
    else:
        (a_ref, w_ref, x_ref), refs = refs[:3], refs[3:]
        a, x = a_ref[...], x_ref[...]
    g_ref, wh_ref, wl_ref, br_ref, utri_ref, h_ref, xn_ref, rec_ref, rect_ref, cnt_ref, cnt_scr = refs

    @pl.when(i == 0)
    def _():
        cnt_scr[...] = jnp.zeros_like(cnt_scr)

    h = x + jnp.dot(a, w_ref[...], preferred_element_type=F32)
    h_ref[...] = h
    xn = _rms(h, g_ref[...])
    xn_ref[...] = _pack_halves(xn)
    xh = xn.astype(BF16)
    xl = (xn - xh.astype(F32)).astype(BF16)
    logits = (_nt_dot(wh_ref[...], xh) + (_nt_dot(wl_ref[...], xh) + _nt_dot(wh_ref[...], xl)))[:ROUTE_ROWS]
    logits = logits + br_ref[...]
    tokens = logits.shape[1]
    rid = lax.broadcasted_iota(jnp.int32, (ROUTE_ROWS, tokens), 0).astype(F32)
    neg = jnp.float32(-jnp.inf)
    big = jnp.float32(ROUTE_ROWS)

    is_g = (rid >= N_EXPERTS) & (rid < N_EXPERTS + N_GROUPS)
    gl = jnp.where(is_g, logits, neg)
    gmax = jnp.max(gl, axis=0, keepdims=True)
    gsel = jnp.min(jnp.where(gl == gmax, rid, big), axis=0, keepdims=True) - N_EXPERTS
    gden = jnp.sum(jnp.where(is_g, jnp.exp(gl - gmax), 0.0), axis=0, keepdims=True)
    gw = 1.0 / gden

    in_grp = (rid >= gsel * EXPERTS_PER_GROUP) & (rid < (gsel + 1) * EXPERTS_PER_GROUP)
    el = jnp.where(in_grp, logits, neg)
    t1 = jnp.max(el, axis=0, keepdims=True)
    e1 = jnp.min(jnp.where(el == t1, rid, big), axis=0, keepdims=True)
    el2 = jnp.where(rid == e1, neg, el)
    t2 = jnp.max(el2, axis=0, keepdims=True)
    e2 = jnp.min(jnp.where(el2 == t2, rid, big), axis=0, keepdims=True)
    x2 = jnp.exp(t2 - t1)
    w1 = gw / (1.0 + x2)
    w2 = gw * x2 / (1.0 + x2)

    oh1 = (rid == e1).astype(F32)
    oh2 = (rid == e2).astype(F32)
    oh = oh1 + oh2
    before = jnp.dot(oh.astype(BF16), utri_ref[...], preferred_element_type=F32)
    base = cnt_scr[...] + before
    r1 = jnp.sum(base * oh1, axis=0, keepdims=True)
    r2 = jnp.sum(base * oh2, axis=0, keepdims=True)
    cnt_scr[...] = cnt_scr[...] + jnp.sum(oh, axis=1, keepdims=True)

    zero = jnp.zeros_like(w1)
    rect = jnp.concatenate([e1, e2, r1, r2, w1, w2, zero, zero], axis=0)
    rect_ref[...] = rect
    wide = jnp.concatenate([rect, jnp.zeros((LANES - ROUTE_COLS, tokens), F32)], axis=0)
    rec_ref[...] = jnp.concatenate([wide[:, t0:t0 + LANES].T for t0 in range(0, tokens, LANES)], axis=0)
    cnt_ref[...] = cnt_scr[...]


def moe_route(a, w_out_bf16, x, gain, w_router, b_router, utri):
    parts = isinstance(a, tuple)
    t, d = T_ALL, D_MODEL
    row = lambda i: (i, 0)
    fix = lambda i: (0, 0)
    w_t = w_router.T
    w_hi = w_t.astype(BF16)
    w_lo = (w_t - w_hi.astype(F32)).astype(BF16)
    w_spec = pl.BlockSpec((d, d), fix)
    if parts:
        pre_specs = _parts_specs(d) + [w_spec] + _parts_specs(d)
        pre_args = (*a, w_out_bf16, *x)
    else:
        pre_specs = [pl.BlockSpec((ROW_TILE, d), row), w_spec, pl.BlockSpec((ROW_TILE, d), row)]
        pre_args = (a, w_out_bf16, x)
    return pl.pallas_call(
        functools.partial(_route_kernel, parts=parts),
        out_shape=(jax.ShapeDtypeStruct((t, d), F32),
                   jax.ShapeDtypeStruct((t, d // 2), U32), jax.ShapeDtypeStruct((t, LANES), F32),
                   jax.ShapeDtypeStruct((ROUTE_COLS, t), F32), jax.ShapeDtypeStruct((ROUTE_ROWS, 1), F32)),
        grid=(t // ROW_TILE,),
        in_specs=pre_specs + [pl.BlockSpec((1, d), fix),
                              pl.BlockSpec((LANES, d), fix), pl.BlockSpec((LANES, d), fix),
                              pl.BlockSpec((ROUTE_ROWS, 1), fix), pl.BlockSpec((ROW_TILE, ROW_TILE), fix)],
        out_specs=(pl.BlockSpec((ROW_TILE, d), row),
                   pl.BlockSpec((ROW_TILE, d // 2), row), pl.BlockSpec((ROW_TILE, LANES), row),
                   pl.BlockSpec((ROUTE_COLS, ROW_TILE), lambda i: (0, i)), pl.BlockSpec((ROUTE_ROWS, 1), fix)),
        scratch_shapes=[pltpu.VMEM((ROUTE_ROWS, 1), F32)],
        compiler_params=_cparams(("arbitrary",)),
        name="moe_route",
    )(*pre_args, gain.reshape(1, d), w_hi, w_lo, b_router, utri)


def _row_copy(src, src_row, dst, dst_row, sem):
    return pltpu.make_async_copy(src.at[pl.ds(src_row, 1)], dst.at[pl.ds(dst_row, 1)], sem)


def _dispatch_kernel(dest_ref, xn_ref, xs_ref, sem):
    rows = xn_ref.shape[0]

    def issue(r, c):
        _row_copy(xn_ref, r, xs_ref, dest_ref[0, 0, r], sem).start(priority=0)
        _row_copy(xn_ref, r, xs_ref, dest_ref[0, 0, rows + r], sem).start(priority=1)
        return c

    lax.fori_loop(0, rows, issue, 0, unroll=8)
    for _ in range(2):
        pltpu.make_async_copy(xn_ref, xs_ref.at[pl.ds(0, rows)], sem).wait()


def moe_dispatch(xn, dest3, n_slots):
    t, d = xn.shape
    return pl.pallas_call(
        _dispatch_kernel,
        out_shape=jax.ShapeDtypeStruct((n_slots, d), F32),
        grid=(t // ROW_TILE,),
        in_specs=[pl.BlockSpec((1, 1, 2 * ROW_TILE), lambda i: (i, 0, 0), memory_space=pltpu.SMEM),
                  pl.BlockSpec((ROW_TILE, d), lambda i: (i, 0))],
        out_specs=pl.BlockSpec(memory_space=pl.ANY),
        scratch_shapes=[pltpu.SemaphoreType.DMA],
        compiler_params=_cparams(("arbitrary",)),
        name="moe_dispatch",
    )(dest3, xn)


SC_WINDOW = 64
SC_INDEX_WINDOW = 128


def _sc_mesh():
    return plsc.VectorSubcoreMesh(core_axis_name="core", subcore_axis_name="subcore")


def moe_dispatch_sc(xn, dest_a, dest_b, n_slots):
    t, d = xn.shape

    n_parts = SC_INDEX_WINDOW // SC_WINDOW

    @pl.kernel(out_type=jax.ShapeDtypeStruct((n_slots, d), xn.dtype), mesh=_sc_mesh(),
               scratch_types=[pltpu.VMEM((2, SC_WINDOW, d), xn.dtype), pltpu.SemaphoreType.DMA((2,)),
                              pltpu.SemaphoreType.DMA((2,))],
               name="moe_dispatch_sc")
    def run(x_hbm, id_hbm, da_hbm, db_hbm, o_hbm, buf, load_sem, store_sem):
        def body(id_vmem, da_vmem, db_vmem):
            part = lambda j: pl.ds(j * SC_WINDOW, SC_WINDOW)
            load = lambda j: pltpu.make_async_copy(x_hbm.at[id_vmem.at[0, part(j)]], buf.at[j % 2], load_sem.at[j % 2])
            load(0).start()
            for j in range(n_parts):
                load(j).wait()
                if j + 1 < n_parts:
                    load(j + 1).start()
                stores = [pltpu.make_async_copy(buf.at[j % 2], o_hbm.at[dv.at[0, part(j)]], store_sem.at[k])
                          for k, dv in enumerate((da_vmem, db_vmem))]
                for s in stores:
                    s.start()
                for s in stores:
                    s.wait()

        idx_spec = pl.BlockSpec((1, SC_INDEX_WINDOW), lambda i: (0, i))
        pltpu.emit_pipeline(
            body,
            grid=(t // SC_INDEX_WINDOW,),
            in_specs=[idx_spec, idx_spec, idx_spec],
            out_specs=[],
            core_axis_name=("core", "subcore"),
            dimension_semantics=(pltpu.PARALLEL,),
        )(id_hbm, da_hbm, db_hbm)

    return run(xn, jnp.arange(t, dtype=jnp.int32).reshape(1, t), dest_a, dest_b)


def moe_gather_sc(ys, dest_a, dest_b):
    d = ys.shape[1]
    t = dest_a.shape[1]
    out = jax.ShapeDtypeStruct((t, d), ys.dtype)

    n_moves = 2 * (SC_INDEX_WINDOW // SC_WINDOW)

    @pl.kernel(out_type=(out, out), mesh=_sc_mesh(),
               scratch_types=[pltpu.VMEM((2, SC_WINDOW, d), ys.dtype), pltpu.SemaphoreType.DMA((2,)),
                              pltpu.SemaphoreType.DMA((2,))],
               name="moe_gather_sc")
    def run(y_hbm, id_hbm, da_hbm, db_hbm, ga_hbm, gb_hbm, buf, load_sem, store_sem):
        def body(id_vmem, da_vmem, db_vmem):
            part = lambda m: pl.ds((m // 2) * SC_WINDOW, SC_WINDOW)
            src = lambda m: (da_vmem, db_vmem)[m % 2]
            dst = lambda m: (ga_hbm, gb_hbm)[m % 2]
            load = lambda m: pltpu.make_async_copy(y_hbm.at[src(m).at[0, part(m)]], buf.at[m % 2], load_sem.at[m % 2])
            store = lambda m: pltpu.make_async_copy(buf.at[m % 2], dst(m).at[id_vmem.at[0, part(m)]],
                                                    store_sem.at[m % 2])
            load(0).start()
            for m in range(n_moves):
                load(m).wait()
                if m >= 1:
                    store(m - 1).wait()
                if m + 1 < n_moves:
                    load(m + 1).start()
                store(m).start()
            store(n_moves - 1).wait()

        idx_spec = pl.BlockSpec((1, SC_INDEX_WINDOW), lambda i: (0, i))
        pltpu.emit_pipeline(
            body,
            grid=(t // SC_INDEX_WINDOW,),
            in_specs=[idx_spec, idx_spec, idx_spec],
            out_specs=[],
            core_axis_name=("core", "subcore"),
            dimension_semantics=(pltpu.PARALLEL,),
        )(id_hbm, da_hbm, db_hbm)

    return run(ys, jnp.arange(t, dtype=jnp.int32).reshape(1, t), dest_a, dest_b)


def _combine_dense_kernel(h_ref, rec_ref, ga_ref, gb_ref, *out_refs, split):
    i = pl.program_id(0)
    rec = rec_ref[...]
    res = h_ref[...] + rec[:, 4:5] * _unpack_halves(ga_ref[...]) + rec[:, 5:6] * _unpack_halves(gb_ref[...])
    if not split:
        out_refs[0][...] = res
    else:
        @pl.when(i < N_PROMPT_TILES)
        def _():
            out_refs[0][...] = res

        @pl.when((i >= N_PROMPT_TILES) & (i < N_PROMPT_TILES + N_SAMPLE_TILES))
        def _():
            out_refs[1][...] = res


def moe_combine_dense(h, rec, ga, gb, split=False):
    t, d = h.shape
    row = lambda i: (i, 0)
    if split:
        out_shape = (jax.ShapeDtypeStruct((T_PROMPT, d), F32), jax.ShapeDtypeStruct((T_SAMPLE, d), F32))
        out_specs = tuple(_parts_specs(d)[:2])
    else:
        out_shape = jax.ShapeDtypeStruct((t, d), F32)
        out_specs = pl.BlockSpec((ROW_TILE, d), row)
    return pl.pallas_call(
        functools.partial(_combine_dense_kernel, split=split),
        out_shape=out_shape,
        grid=(t // ROW_TILE,),
        in_specs=[pl.BlockSpec((ROW_TILE, d), row), pl.BlockSpec((ROW_TILE, LANES), row),
                  pl.BlockSpec((ROW_TILE, d // 2), row), pl.BlockSpec((ROW_TILE, d // 2), row)],
        out_specs=out_specs,
        compiler_params=_cparams(("arbitrary",)),
        name="moe_combine_dense",
    )(h, rec, ga, gb)


def _combine_kv_q_kernel(h_ref, rec_ref, ga_ref, gb_ref, gkv_ref, wkv_ref, gq_ref, wq_ref, hmean_ref,
                         hgk_ref, hgq_ref, cos_ref, sina_ref, sinb_ref, ho_ref, k_ref, v_ref, q_ref):
    rec = rec_ref[...]
    h = h_ref[...] + rec[:, 4:5] * _unpack_halves(ga_ref[...]) + rec[:, 5:6] * _unpack_halves(gb_ref[...])
    ho_ref[...] = h
    xhat = h * lax.rsqrt(jnp.mean(h * h, axis=-1, keepdims=True) + RMS_EPS)
    tables = (cos_ref[...], sina_ref[...], sinb_ref[...])
    zkv = jnp.dot((xhat * gkv_ref[...]).astype(BF16), wkv_ref[...], preferred_element_type=F32)
    k_ref[...] = _head_norm_rope(zkv[:, :KV_DIM], hmean_ref, hgk_ref[...], *tables)
    v_ref[...] = zkv[:, KV_DIM:]
    zq = jnp.dot((xhat * gq_ref[...]).astype(BF16), wq_ref[...], preferred_element_type=F32)
    q = _head_norm_rope(zq, hmean_ref, hgq_ref[...], *tables)
    q_ref[...] = (q * HEAD_DIM ** -0.5).astype(q_ref.dtype)


def moe_combine_kv_q(h, rec, ga, gb, kv_gain, kv_w_bf16, q_gain, wq_bf16, hmean, k_hgain, q_hgain,
                     cos_t, sina_t, sinb_t):
    t, d = h.shape
    row = lambda i: (i, 0)
    fix = lambda i: (0, 0)
    rope = pl.BlockSpec((ROW_TILE, LANES), _rope_tile)
    return pl.pallas_call(
        _combine_kv_q_kernel,
        out_shape=(jax.ShapeDtypeStruct((t, d), F32), jax.ShapeDtypeStruct((t, KV_DIM), F32),
                   jax.ShapeDtypeStruct((t, KV_DIM), F32), jax.ShapeDtypeStruct((t, d), BF16)),
        grid=(t // ROW_TILE,),
        in_specs=[pl.BlockSpec((ROW_TILE, d), row), pl.BlockSpec((ROW_TILE, LANES), row),
                  pl.BlockSpec((ROW_TILE, d // 2), row), pl.BlockSpec((ROW_TILE, d // 2), row),
                  pl.BlockSpec((1, d), fix), pl.BlockSpec((d, 2 * KV_DIM), fix),
                  pl.BlockSpec((1, d), fix), pl.BlockSpec((d, d), fix),
                  pl.BlockSpec((256, 256), fix), pl.BlockSpec((1, KV_DIM), fix), pl.BlockSpec((1, d), fix),
                  rope, rope, rope],
        out_specs=(pl.BlockSpec((ROW_TILE, d), row), pl.BlockSpec((ROW_TILE, KV_DIM), row),
                   pl.BlockSpec((ROW_TILE, KV_DIM), row), pl.BlockSpec((ROW_TILE, d), row)),
        compiler_params=_cparams(("parallel",)),
        name="moe_combine_kv_q",
    )(h, rec, ga, gb, kv_gain.reshape(1, d), kv_w_bf16, q_gain.reshape(1, d), wq_bf16, hmean, k_hgain, q_hgain,
      cos_t, sina_t, sinb_t)


def _ffn_kernel(wblk_ref, we_ref, wlo_ref, whi_ref, xs_ref, w13_ref, w2_ref, ys_ref, w13b, w2b):
    w = pl.program_id(0)
    prev = jnp.maximum(w - 1, 0)
    first_visit = (w == 0) | (wblk_ref[w] != wblk_ref[prev])
    lo = wlo_ref[w]
    hi = whi_ref[w]

    def ffn(x):
        x = _unpack_halves(x).astype(BF16)
        cw = D_EXPERT // FFN_CHUNKS
        gate_up = []
        for c in range(FFN_CHUNKS):
            a = jnp.dot(x, w13b[:, c * cw:(c + 1) * cw], preferred_element_type=F32)
            u = jnp.dot(x, w13b[:, D_EXPERT + c * cw:D_EXPERT + (c + 1) * cw], preferred_element_type=F32)
            gate_up.append((a, u))
        hmid = jnp.concatenate([(_silu(a) * u).astype(BF16) for a, u in gate_up], axis=1)
        return _pack_halves(jnp.dot(hmid, w2b[...], preferred_element_type=F32))

    @pl.when(hi > lo)
    def _():
        @pl.when((w == 0) | (we_ref[w] != we_ref[prev]))
        def _():
            w13b[...] = w13_ref[...].astype(BF16)
            w2b[...] = w2_ref[...].astype(BF16)

        whole = (lo == 0) & (hi == EXPERT_BLOCK)

        @pl.when(whole)
        def _():
            ys_ref[...] = ffn(xs_ref[...])

        half = EXPERT_BLOCK // 2
        for p in range(2):
            rows = slice(p * half, (p + 1) * half)
            touched = (lo < (p + 1) * half) & (hi > p * half)

            @pl.when(jnp.logical_not(whole) & touched)
            def _():
                y = ffn(xs_ref[rows, :])
                row = lax.broadcasted_iota(jnp.int32, y.shape, 0) + p * half
                mine = (row >= lo) & (row < hi)

                @pl.when(first_visit)
                def _():
                    ys_ref[rows, :] = jnp.where(mine, y, jnp.zeros_like(y))

                @pl.when(jnp.logical_not(first_visit))
                def _():
                    ys_ref[rows, :] = jnp.where(mine, y, ys_ref[rows, :])

            @pl.when(jnp.logical_not(whole) & jnp.logical_not(touched) & first_visit)
            def _():
                ys_ref[rows, :] = jnp.zeros((half, ys_ref.shape[1]), U32)


def moe_ffn(xs, work, w13_all, w2_all, layer):
    n_slots, dp = xs.shape
    d = 2 * dp
    n_work = work[0].shape[0]
    xmap = lambda w, wb, we, wlo, whi: (wb[w], 0)
    w_map = lambda w, wb, we, wlo, whi: (layer, we[w], 0, 0)
    grid_spec = pltpu.PrefetchScalarGridSpec(
        num_scalar_prefetch=4,
        grid=(n_work,),
        in_specs=[pl.BlockSpec((EXPERT_BLOCK, dp), xmap),
                  pl.BlockSpec((None, None, d, 2 * D_EXPERT), w_map),
                  pl.BlockSpec((None, None, D_EXPERT, d), w_map)],
        out_specs=pl.BlockSpec((EXPERT_BLOCK, dp), xmap),
        scratch_shapes=[pltpu.VMEM((d, 2 * D_EXPERT), BF16), pltpu.VMEM((D_EXPERT, d), BF16)],
    )
    return pl.pallas_call(
        _ffn_kernel,
        out_shape=jax.ShapeDtypeStruct((n_slots, dp), U32),
        grid_spec=grid_spec,
        compiler_params=_cparams(("arbitrary",)),
        name="moe_ffn",
    )(*work, xs, w13_all, w2_all)


def _ffn_work_items(cnt):
    n_slots = 2 * T_ALL
    n_blocks = n_slots // EXPERT_BLOCK
    n_work = n_blocks + N_EXPERTS - 1
    end = jnp.cumsum(cnt)
    start = end - cnt
    first_blk = start // EXPERT_BLOCK
    last_blk = jnp.maximum(end - 1, start) // EXPERT_BLOCK
    n_items = jnp.where(cnt > 0, last_blk - first_blk + 1, 0)
    item_end = jnp.cumsum(n_items)
    item_start = item_end - n_items
    w = jnp.arange(n_work, dtype=jnp.int32)
    used = w < item_end[-1]
    wq = jnp.minimum(w, item_end[-1] - 1)
    e = jnp.sum((item_end[:, None] <= wq[None, :]).astype(jnp.int32), axis=0)
    onehot = e[None, :] == jnp.arange(N_EXPERTS, dtype=jnp.int32)[:, None]
    of_e = lambda table: jnp.sum(jnp.where(onehot, table[:, None], 0), axis=0)
    blk = jnp.where(used, of_e(first_blk) + (w - of_e(item_start)), n_blocks - 1).astype(jnp.int32)
    lo = jnp.maximum(of_e(start), blk * EXPERT_BLOCK) - blk * EXPERT_BLOCK
    hi = jnp.minimum(of_e(end), (blk + 1) * EXPERT_BLOCK) - blk * EXPERT_BLOCK
    lo = jnp.where(used, lo, 0).astype(jnp.int32)
    hi = jnp.where(used, hi, 0).astype(jnp.int32)
    return start, (blk, e, lo, hi)


def _combine_kernel(dest_ref, h_ref, rec_ref, ys_ref, *rest, split):
    out_refs, (g1, g2, sem) = rest[:-3], rest[-3:]
    rows = h_ref.shape[0]
    i = pl.program_id(0)

    def issue(r, c):
        _row_copy(ys_ref, dest_ref[0, 0, r], g1, r, sem).start(priority=0)
        _row_copy(ys_ref, dest_ref[0, 0, rows + r], g2, r, sem).start(priority=1)
        return c

    lax.fori_loop(0, rows, issue, 0, unroll=8)
    for buf in (g1, g2):
        pltpu.make_async_copy(ys_ref.at[pl.ds(0, rows)], buf, sem).wait()
    rec = rec_ref[...]
    res = h_ref[...] + rec[:, 4:5] * g1[...] + rec[:, 5:6] * g2[...]
    if not split:
        out_refs[0][...] = res
    else:
        @pl.when(i < N_PROMPT_TILES)
        def _():
            out_refs[0][...] = res

        @pl.when((i >= N_PROMPT_TILES) & (i < N_PROMPT_TILES + N_SAMPLE_TILES))
        def _():
            out_refs[1][...] = res


def moe_combine(h, rec, ys, dest3, split=False):
    t, d = h.shape
    row = lambda i: (i, 0)
    if split:
        parts = _parts_specs(d)[:2]
        out_shape = (jax.ShapeDtypeStruct((T_PROMPT, d), F32), jax.ShapeDtypeStruct((T_SAMPLE, d), F32))
        out_specs = tuple(parts)
    else:
        out_shape = jax.ShapeDtypeStruct((t, d), F32)
        out_specs = pl.BlockSpec((ROW_TILE, d), row)
    return pl.pallas_call(
        functools.partial(_combine_kernel, split=split),
        out_shape=out_shape,
        grid=(t // ROW_TILE,),
        in_specs=[pl.BlockSpec((1, 1, 2 * ROW_TILE), lambda i: (i, 0, 0), memory_space=pltpu.SMEM),
                  pl.BlockSpec((ROW_TILE, d), row), pl.BlockSpec((ROW_TILE, LANES), row),
                  pl.BlockSpec(memory_space=pl.ANY)],
        out_specs=out_specs,
        scratch_shapes=[pltpu.VMEM((ROW_TILE, d), F32), pltpu.VMEM((ROW_TILE, d), F32),
                        pltpu.SemaphoreType.DMA],
        compiler_params=_cparams(("arbitrary",)),
        name="moe_combine",
    )(dest3, h, rec, ys)


def hier_moe_layer(a, w_out_bf16, x, layer, gain, w_group, b_group, w_expert, b_expert, w13_all, w2_all, utri,
                   finish):
    t = T_ALL
    pad = LANES - N_EXPERTS - N_GROUPS
    w_router = jnp.concatenate([w_expert, w_group, jnp.zeros((D_MODEL, pad), F32)], axis=1)
    b_router = jnp.concatenate([b_expert, b_group, jnp.zeros((pad,), F32)])[:ROUTE_ROWS].reshape(ROUTE_ROWS, 1)
    h, xn, rec, rect, counts = moe_route(a, w_out_bf16, x, gain, w_router, b_router, utri)

    cnt = counts[:N_EXPERTS, 0].astype(jnp.int32)
    start, work = _ffn_work_items(cnt)
    experts = jnp.arange(N_EXPERTS, dtype=jnp.int32)[:, None]

    def slot_of(e_row, rank_row):
        first = jnp.sum(jnp.where(e_row.astype(jnp.int32)[None, :] == experts, start[:, None], 0), axis=0)
        return (first + rank_row.astype(jnp.int32)).reshape(1, t)

    dest_a = slot_of(rect[0], rect[2])
    dest_b = slot_of(rect[1], rect[3])

    xs = moe_dispatch_sc(xn, dest_a, dest_b, 2 * t)
    ys = moe_ffn(xs, work, w13_all, w2_all, layer)
    ga, gb = moe_gather_sc(ys, dest_a, dest_b)
    return finish(h, rec, ga, gb)


def _rope_tables(pos):
    half = ROPE_DIM // 2
    lane = np.arange(LANES) % HEAD_DIM
    rotary = lane < ROPE_DIM
    inv = jnp.where(rotary, jnp.exp(-math.log(ROPE_THETA) * jnp.asarray(lane % half, F32) * (2.0 / ROPE_DIM)), 0.0)
    ang = pos.astype(F32)[:, None] * inv[None, :]
    cos, sin = jnp.cos(ang), jnp.sin(ang)
    first = jnp.asarray(lane < half)
    second = jnp.asarray(rotary & (lane >= half))
    return cos, jnp.where(first, -sin, 0.0), jnp.where(second, sin, 0.0)


def kernel(x_prompt, x_sample, state_hgrn, cache_k_win, cache_v_win, meta_tokens, a_norm, a_w_in, a_lower_logits, a_out_norm, a_w_out, kv_norm, kv_w, k_norm, b_norm, b_wq, b_q_norm, b_sinks, b_w_out, moe_norm, moe_w_group, moe_b_group, moe_w_expert, moe_b_expert, moe_w13, moe_w2):
    tail_rows = T_ALL - OFF_META
    x_parts = (x_prompt.reshape(T_PROMPT, D_MODEL), x_sample.reshape(T_SAMPLE, D_MODEL),
               jnp.concatenate([meta_tokens.astype(F32), jnp.zeros((tail_rows - N_META, D_MODEL), F32)], axis=0))
    pos = jnp.concatenate([N_META + jnp.arange(SEQ, dtype=jnp.int32),
                           jnp.tile(PAST_LEN + jnp.arange(DEC_SEQ, dtype=jnp.int32), ROW_TILE // DEC_SEQ),
                           jnp.arange(N_META, dtype=jnp.int32),
                           jnp.zeros((ROW_TILE - N_META,), jnp.int32)])
    cos_t, sina_t, sinb_t = _rope_tables(pos)
    r256 = np.arange(256)
    hmean = jnp.asarray((r256[:, None] // HEAD_DIM == r256[None, :] // HEAD_DIM).astype(np.float32) / HEAD_DIM, BF16)
    rt = np.arange(ROW_TILE)
    utri = jnp.asarray((rt[:, None] < rt[None, :]).astype(np.float32), BF16)
    lower = jnp.cumsum(jax.nn.softmax(a_lower_logits.astype(F32), axis=0), axis=0)

    moe = functools.partial(hier_moe_layer, w13_all=moe_w13, w2_all=moe_w2, utri=utri)

    z = in_project(x_parts, a_norm[0], a_w_in[0].astype(BF16))
    zero_state = jnp.zeros((1, A_HEADS, A_DK, A_DV), F32)
    o_meta, s_meta = hgrn2_scan(z, zero_state, lower[0], a_out_norm[0],
                                row_off=OFF_META, n_seq=1, seq_len=N_META)
    o_prompt, s_prompt = hgrn2_scan(z, s_meta, lower[0], a_out_norm[0], row_off=0, n_seq=BATCH, seq_len=SEQ)
    o_sample, s_sample = hgrn2_scan(z, state_hgrn[0].astype(F32), lower[0], a_out_norm[0],
                                    row_off=OFF_SAMPLE, n_seq=DEC_BATCH, seq_len=DEC_SEQ, group=SCAN_SAMPLE_GROUP)
    o_tail = jnp.concatenate([o_meta, jnp.zeros((tail_rows - N_META, D_MODEL), BF16)], axis=0)
    finish0 = functools.partial(
        moe_combine_kv_q, kv_gain=kv_norm, kv_w_bf16=kv_w.astype(BF16), q_gain=b_norm[0], wq_bf16=b_wq[0].astype(BF16),
        hmean=hmean, k_hgain=jnp.tile(k_norm, KV_HEADS).reshape(1, KV_DIM),
        q_hgain=jnp.tile(b_q_norm[0], Q_HEADS).reshape(1, D_MODEL), cos_t=cos_t, sina_t=sina_t, sinb_t=sinb_t)
    h, k_all, v_all, q_all = moe((o_prompt, o_sample, o_tail), a_w_out[0].astype(BF16), x_parts, 0, moe_norm[0],
                                 moe_w_group[0], moe_b_group[0], moe_w_expert[0], moe_b_expert[0], finish=finish0)

    meta_blk = lambda a: jnp.concatenate([jnp.zeros((ATT_BLOCK - N_META, KV_DIM), F32),
                                          a[OFF_META:OFF_META + N_META]], axis=0)
    sinks = b_sinks[0].astype(F32)
    att_all = attention_prompt(q_all, k_all, v_all, meta_blk(k_all), meta_blk(v_all), sinks)
    att_all, k_win_s, v_win_s = attention_sample(q_all, cache_k_win.astype(F32), cache_v_win.astype(F32),
                                                 k_all, v_all, sinks, att_all)
    y_p, y_s = moe(att_all, b_w_out[0].astype(BF16), h, 1, moe_norm[1], moe_w_group[1], moe_b_group[1],
                   moe_w_expert[1], moe_b_expert[1], finish=functools.partial(moe_combine_dense, split=True))

    y_prompt = y_p.reshape(BATCH, SEQ, D_MODEL)
    y_sample = y_s.reshape(DEC_BATCH, DEC_SEQ, D_MODEL)
    last = lambda a: jnp.stack([a[(b + 1) * SEQ - WINDOW:(b + 1) * SEQ] for b in range(BATCH)]).reshape(
        BATCH, WINDOW, KV_HEADS, HEAD_DIM)
    kp = last(k_all)
    vp = last(v_all)
    return (y_prompt, y_sample, s_prompt[None], s_sample[None], kp, vp, k_win_s, v_win_s)
```

```python
import functools
import math

import numpy as np
import jax
import jax.numpy as jnp
from jax import lax
from jax.experimental import pallas as pl
from jax.experimental.pallas import tpu as pltpu
from jax.experimental.pallas import tpu_sc as plsc

F32 = jnp.float32
BF16 = jnp.bfloat16
U32 = jnp.uint32

D_MODEL = 1024
BATCH = 4
SEQ = 4096
DEC_BATCH = 128
DEC_SEQ = 8
PAST_LEN = 8192
N_META = 16
A_HEADS = 8
A_DK = 128
A_DV = 128
Q_HEADS = 16
KV_HEADS = 4
HEAD_DIM = 64
KV_DIM = KV_HEADS * HEAD_DIM
WINDOW = 128
ROPE_DIM = 16
ROPE_THETA = 500000.0
N_GROUPS = 4
EXPERTS_PER_GROUP = 8
N_EXPERTS = 32
D_EXPERT = 512
RMS_EPS = 1e-6

LANES = 128
SUBLANES = 8
VMEM_LIMIT = 56 * 1024 * 1024

ROW_TILE = 512
T_PROMPT = BATCH * SEQ
T_SAMPLE = DEC_BATCH * DEC_SEQ
OFF_SAMPLE = T_PROMPT
OFF_META = T_PROMPT + T_SAMPLE
T_REAL = OFF_META + N_META
T_ALL = -(-T_REAL // ROW_TILE) * ROW_TILE
N_TILES = T_ALL // ROW_TILE

SCAN_CHUNK = 128
SCAN_SAMPLE_GROUP = 8
ATT_BLOCK = 128
EXPERT_BLOCK = 512
FFN_CHUNKS = 2


def _cparams(sem):
    return pltpu.CompilerParams(dimension_semantics=sem, vmem_limit_bytes=VMEM_LIMIT)


def _nt_dot(a, b):
    return lax.dot_general(a, b, (((1,), (1,)), ((), ())), preferred_element_type=F32)


def _rms(x, gain):
    ms = jnp.mean(x * x, axis=-1, keepdims=True)
    return x * lax.rsqrt(ms + RMS_EPS) * gain


def _silu(x):
    return x * jax.nn.sigmoid(x)


def _pack_halves(x):
    w = x.shape[1] // 2
    hi = lax.bitcast_convert_type(x[:, :w].astype(BF16).astype(F32), U32)
    lo = lax.bitcast_convert_type(x[:, w:].astype(BF16).astype(F32), U32)
    return hi | (lo >> 16)


def _unpack_halves(p):
    hi = lax.bitcast_convert_type(p & jnp.uint32(0xFFFF0000), F32)
    lo = lax.bitcast_convert_type(p << 16, F32)
    return jnp.concatenate([hi, lo], axis=1)


N_PROMPT_TILES = T_PROMPT // ROW_TILE
N_SAMPLE_TILES = T_SAMPLE // ROW_TILE


def _parts_specs(width):
    return [pl.BlockSpec((ROW_TILE, width), lambda i: (jnp.minimum(i, N_PROMPT_TILES - 1), 0)),
            pl.BlockSpec((ROW_TILE, width), lambda i: (jnp.clip(i - N_PROMPT_TILES, 0, N_SAMPLE_TILES - 1), 0)),
            pl.BlockSpec((ROW_TILE, width), lambda i: (0, 0))]


def _pick_part(i, p_ref, s_ref, t_ref, dtype):
    return jnp.where(i < N_PROMPT_TILES, p_ref[...].astype(dtype),
                     jnp.where(i < N_PROMPT_TILES + N_SAMPLE_TILES, s_ref[...].astype(dtype),
                               t_ref[...].astype(dtype)))


def _in_proj_kernel(xp_ref, xs_ref, xt_ref, g_ref, w_ref, o_ref):
    x = _pick_part(pl.program_id(0), xp_ref, xs_ref, xt_ref, F32)
    xn = _rms(x, g_ref[...])
    o_ref[...] = jnp.dot(xn.astype(BF16), w_ref[...], preferred_element_type=F32)


def in_project(x_parts, gain, w_bf16):
    d, n = w_bf16.shape
    return pl.pallas_call(
        _in_proj_kernel,
        out_shape=jax.ShapeDtypeStruct((T_ALL, n), F32),
        grid=(N_TILES,),
        in_specs=_parts_specs(d) + [pl.BlockSpec((1, d), lambda i: (0, 0)),
                                    pl.BlockSpec((d, n), lambda i: (0, 0))],
        out_specs=pl.BlockSpec((ROW_TILE, n), lambda i: (i, 0)),
        compiler_params=_cparams(("parallel",)),
        name="in_project",
    )(*x_parts, gain.reshape(1, d), w_bf16)


def _mixer_out_kernel(ap_ref, as_ref, at_ref, w_ref, xp_ref, xs_ref, xt_ref, o_ref):
    i = pl.program_id(0)
    a = _pick_part(i, ap_ref, as_ref, at_ref, BF16)
    x = _pick_part(i, xp_ref, xs_ref, xt_ref, F32)
    o_ref[...] = x + jnp.dot(a, w_ref[...], preferred_element_type=F32)


def mixer_out(a_parts, w_bf16, x_parts):
    k, n = w_bf16.shape
    return pl.pallas_call(
        _mixer_out_kernel,
        out_shape=jax.ShapeDtypeStruct((T_ALL, n), F32),
        grid=(N_TILES,),
        in_specs=_parts_specs(k) + [pl.BlockSpec((k, n), lambda i: (0, 0))] + _parts_specs(n),
        out_specs=pl.BlockSpec((ROW_TILE, n), lambda i: (i, 0)),
        compiler_params=_cparams(("parallel",)),
        name="mixer_out",
    )(*a_parts, w_bf16, *x_parts)


def _head_norm_rope(y, hmean_ref, hgain, cos_t, sina_t, sinb_t):
    rows, width = y.shape
    sq = (y * y).astype(BF16)
    parts = []
    for s in range(width // 256):
        parts.append(jnp.dot(sq[:, s * 256:(s + 1) * 256], hmean_ref[...], preferred_element_type=F32))
    ms = parts[0] if len(parts) == 1 else jnp.concatenate(parts, axis=1)
    yn = y * lax.rsqrt(ms + RMS_EPS) * hgain
    reps = width // LANES
    cos_w = jnp.concatenate([cos_t] * reps, axis=1)
    sina_w = jnp.concatenate([sina_t] * reps, axis=1)
    sinb_w = jnp.concatenate([sinb_t] * reps, axis=1)
    half = ROPE_DIM // 2
    nxt = pltpu.roll(yn, width - half, 1)
    prv = pltpu.roll(yn, half, 1)
    return yn * cos_w + nxt * sina_w + prv * sinb_w


def _kv_kernel(x_ref, g_ref, w_ref, hmean_ref, hg_ref, cos_ref, sina_ref, sinb_ref, k_ref, v_ref):
    xn = _rms(x_ref[...], g_ref[...])
    z = jnp.dot(xn.astype(BF16), w_ref[...], preferred_element_type=F32)
    k = _head_norm_rope(z[:, :KV_DIM], hmean_ref, hg_ref[...], cos_ref[...], sina_ref[...], sinb_ref[...])
    k_ref[...] = k
    v_ref[...] = z[:, KV_DIM:]


def _rope_tile(i):
    tiles_per_seq = SEQ // ROW_TILE
    n_prompt_tiles = T_PROMPT // ROW_TILE
    n_sample_tiles = T_SAMPLE // ROW_TILE
    return (jnp.where(i < n_prompt_tiles, i % tiles_per_seq,
                      jnp.where(i < n_prompt_tiles + n_sample_tiles, tiles_per_seq, tiles_per_seq + 1)), 0)


def kv_project(x, gain, w_bf16, hmean, hgain_w, cos_t, sina_t, sinb_t):
    t, d = x.shape
    row = lambda i: (i, 0)
    fix = lambda i: (0, 0)
    return pl.pallas_call(
        _kv_kernel,
        out_shape=(jax.ShapeDtypeStruct((t, KV_DIM), F32), jax.ShapeDtypeStruct((t, KV_DIM), F32)),
        grid=(t // ROW_TILE,),
        in_specs=[pl.BlockSpec((ROW_TILE, d), row), pl.BlockSpec((1, d), fix),
                  pl.BlockSpec((d, 2 * KV_DIM), fix), pl.BlockSpec((256, 256), fix),
                  pl.BlockSpec((1, KV_DIM), fix),
                  pl.BlockSpec((ROW_TILE, LANES), _rope_tile), pl.BlockSpec((ROW_TILE, LANES), _rope_tile),
                  pl.BlockSpec((ROW_TILE, LANES), _rope_tile)],
        out_specs=(pl.BlockSpec((ROW_TILE, KV_DIM), row), pl.BlockSpec((ROW_TILE, KV_DIM), row)),
        compiler_params=_cparams(("parallel",)),
        name="kv_project",
    )(x, gain.reshape(1, d), w_bf16, hmean, hgain_w, cos_t, sina_t, sinb_t)


def _q_kernel(x_ref, g_ref, w_ref, hmean_ref, hg_ref, cos_ref, sina_ref, sinb_ref, q_ref):
    xn = _rms(x_ref[...], g_ref[...])
    z = jnp.dot(xn.astype(BF16), w_ref[...], preferred_element_type=F32)
    q = _head_norm_rope(z, hmean_ref, hg_ref[...], cos_ref[...], sina_ref[...], sinb_ref[...])
    q_ref[...] = (q * HEAD_DIM ** -0.5).astype(q_ref.dtype)


def q_project(x, gain, w_bf16, hmean, hgain_w, cos_t, sina_t, sinb_t):
    t, d = x.shape
    row = lambda i: (i, 0)
    fix = lambda i: (0, 0)
    return pl.pallas_call(
        _q_kernel,
        out_shape=jax.ShapeDtypeStruct((t, d), BF16),
        grid=(t // ROW_TILE,),
        in_specs=[pl.BlockSpec((ROW_TILE, d), row), pl.BlockSpec((1, d), fix),
                  pl.BlockSpec((d, d), fix), pl.BlockSpec((256, 256), fix),
                  pl.BlockSpec((1, d), fix),
                  pl.BlockSpec((ROW_TILE, LANES), _rope_tile), pl.BlockSpec((ROW_TILE, LANES), _rope_tile),
                  pl.BlockSpec((ROW_TILE, LANES), _rope_tile)],
        out_specs=pl.BlockSpec((ROW_TILE, d), row),
        compiler_params=_cparams(("parallel",)),
        name="q_project",
    )(x, gain.reshape(1, d), w_bf16, hmean, hgain_w, cos_t, sina_t, sinb_t)


def _matmul_residual_kernel(a_ref, w_ref, r_ref, o_ref):
    o_ref[...] = r_ref[...] + jnp.dot(a_ref[...], w_ref[...], preferred_element_type=F32)


def matmul_residual(a_bf16, w_bf16, resid):
    t, k = a_bf16.shape
    n = w_bf16.shape[1]
    return pl.pallas_call(
        _matmul_residual_kernel,
        out_shape=jax.ShapeDtypeStruct((t, n), F32),
        grid=(t // ROW_TILE,),
        in_specs=[pl.BlockSpec((ROW_TILE, k), lambda i: (i, 0)),
                  pl.BlockSpec((k, n), lambda i: (0, 0)),
                  pl.BlockSpec((ROW_TILE, n), lambda i: (i, 0))],
        out_specs=pl.BlockSpec((ROW_TILE, n), lambda i: (i, 0)),
        compiler_params=_cparams(("parallel",)),
        name="matmul_residual",
    )(a_bf16, w_bf16, resid)


def _scan_levels(c):
    levels = []
    m = c
    while m >= 2:
        levels.append(m)
        m //= 2
    return levels


LOG2E = 1.4426950408889634


def _scan_kernel(z_ref, s0_ref, lb_ref, og_ref, tri_ref, lmask_ref, sgn_ref, o_ref, sfin_ref, s_scr, b_scr,
                 *, rows, seq_len):
    c_idx = pl.program_id(1)
    levels = _scan_levels(seq_len)
    n_sub = rows // seq_len
    hk = A_HEADS * A_DK

    @pl.when(c_idx == 0)
    def _():
        s_scr[...] = s0_ref[...]

    sub = lax.broadcasted_iota(jnp.int32, (SUBLANES, LANES), 0)
    row = lax.broadcasted_iota(jnp.int32, (LANES, LANES), 0)
    og = og_ref[...]

    def pad_f32(x):
        if x.shape[0] == LANES:
            return x
        return jnp.concatenate([x, jnp.zeros((LANES - x.shape[0], x.shape[1]), x.dtype)], axis=0)

    def pad_rows(x):
        return pad_f32(x).astype(BF16)

    def cols(part, h):
        return slice(part * hk + h * LANES, part * hk + (h + 1) * LANES)

    def gates(h):
        lb = lb_ref[:, cols(0, h)]
        forget = lb + (1.0 - lb) * jax.nn.sigmoid(z_ref[:, cols(1, h)])
        logf = jnp.log(forget)
        hi = logf.astype(BF16).astype(F32)
        r1 = logf - hi
        mid = r1.astype(BF16).astype(F32)
        lo = r1 - mid
        cs = jnp.dot(tri_ref[...], pad_rows(jnp.concatenate([hi, mid, lo], axis=1)),
                     preferred_element_type=F32)
        b = (cs[:rows, :LANES] + cs[:rows, LANES:2 * LANES]) + cs[:rows, 2 * LANES:]
        b_scr[h] = b
        return _silu(z_ref[:, cols(0, h)]), 1.0 - forget, b

    def bref_for(h, m):
        b_rows = b_scr.at[h]
        half = m // 2
        pieces = []
        for g in range(rows // SUBLANES):
            base = g * SUBLANES
            if m >= SUBLANES:
                r = (base // m) * m + half - 1
                piece = jnp.broadcast_to(b_rows[r:r + 1, :], (SUBLANES, LANES))
            else:
                piece = jnp.broadcast_to(b_rows[base + half - 1:base + half, :], (SUBLANES, LANES))
                for blk in range(1, SUBLANES // m):
                    r = base + blk * m + half - 1
                    piece = jnp.where(sub >= blk * m,
                                      jnp.broadcast_to(b_rows[r:r + 1, :], (SUBLANES, LANES)), piece)
            pieces.append(piece)
        return pieces[0] if len(pieces) == 1 else jnp.concatenate(pieces, axis=0)

    heads = range(A_HEADS)
    qkb = [gates(h) for h in heads]
    att = [_nt_dot(pad_rows(qf), pad_rows(kf)) * lmask_ref[len(levels)] for qf, kf, _ in qkb]
    for li, m in enumerate(levels):
        for h in heads:
            qf, kf, b = qkb[h]
            sgn = sgn_ref[li]
            e = jnp.exp2((b - bref_for(h, m)) * sgn)
            w = pad_rows(jnp.where(sgn > 0, qf, kf) * e)
            att[h] = att[h] + _nt_dot(w, w) * lmask_ref[li]

    def finish(h):
        qf, kf, b = qkb[h]
        b_rows = b_scr.at[h]
        v_b = pad_rows(z_ref[:, cols(2, h)])
        o_intra = jnp.dot(att[h].astype(BF16), v_b, preferred_element_type=F32)
        eb = jnp.exp(b)
        qs = qf * eb
        b_end = [jnp.broadcast_to(b_rows[(i + 1) * seq_len - 1:(i + 1) * seq_len, :], (seq_len, LANES))
                 for i in range(n_sub)]
        b_end = b_end[0] if n_sub == 1 else jnp.concatenate(b_end, axis=0)
        kd_t = pad_f32(kf * jnp.exp(b_end - b)).T.astype(BF16)
        eb_t = pad_f32(eb).T
        qs_b = pad_rows(qs)
        o = o_intra
        for i in range(n_sub):
            s_old = s_scr[i, h]
            first, last = i * seq_len, (i + 1) * seq_len - 1
            if n_sub == 1:
                qs_i, v_i = qs_b, v_b
            else:
                mine = (row >= first) & (row <= last)
                qs_i = jnp.where(mine, qs_b, jnp.zeros_like(qs_b))
                v_i = jnp.where(mine, v_b, jnp.zeros_like(v_b))
            o = o + jnp.dot(qs_i, s_old.astype(BF16), preferred_element_type=F32)
            decay = jnp.broadcast_to(eb_t[:, last:last + 1], (LANES, LANES))
            s_scr[i, h] = decay * s_old + jnp.dot(kd_t, v_i, preferred_element_type=F32)
        o = o[:rows]

        on = _rms(o, og) * _silu(z_ref[:, cols(3, h)])
        o_ref[:, cols(0, h)] = on.astype(o_ref.dtype)

    for h in heads:
        finish(h)

    @pl.when(c_idx == pl.num_programs(1) - 1)
    def _():
        sfin_ref[...] = s_scr[...]


def _scan_consts(rows, seq_len):
    levels = _scan_levels(seq_len)
    r = np.arange(LANES)
    t, s = r[:, None], r[None, :]
    live = (t < rows) & (s < rows)
    tri = ((s <= t) & (t // seq_len == s // seq_len) & live).astype(np.float32)
    masks, sgns = [], []
    for m in levels:
        masks.append(((t // m == s // m) & (t % m >= m // 2) & (s % m < m // 2) & live).astype(np.float32))
        sgns.append(np.broadcast_to(np.where(r[:rows, None] % m >= m // 2, LOG2E, -LOG2E), (rows, LANES)))
    masks.append(((t == s) & live).astype(np.float32))
    return jnp.asarray(tri, BF16), jnp.asarray(np.stack(masks), F32), jnp.asarray(np.stack(sgns), F32)


def hgrn2_scan(z, s0, lb, o_gain, *, row_off, n_seq, seq_len, group=1):
    hv = A_HEADS * A_DV
    if seq_len > SCAN_CHUNK:
        assert group == 1
        sub_len, rows, n_chunks, n_steps = SCAN_CHUNK, SCAN_CHUNK, seq_len // SCAN_CHUNK, n_seq
    else:
        sub_len, rows, n_chunks, n_steps = seq_len, group * seq_len, 1, n_seq // group
    blk_off = row_off // rows
    tri, lmask, sgn = _scan_consts(rows, sub_len)
    shared_s0 = s0.shape[0] == 1
    fix2 = lambda s, c: (0, 0)
    fix3 = lambda s, c: (0, 0, 0)
    o, sfin = pl.pallas_call(
        functools.partial(_scan_kernel, rows=rows, seq_len=sub_len),
        out_shape=(jax.ShapeDtypeStruct((n_seq * seq_len, hv), BF16 if rows % 16 == 0 else F32),
                   jax.ShapeDtypeStruct((n_seq, A_HEADS, A_DK, A_DV), F32)),
        grid=(n_steps, n_chunks),
        in_specs=[pl.BlockSpec((rows, 4 * hv), lambda s, c: (blk_off + s * n_chunks + c, 0)),
                  pl.BlockSpec((group, A_HEADS, A_DK, A_DV), (lambda s, c: (0, 0, 0, 0)) if shared_s0
                               else (lambda s, c: (s, 0, 0, 0))),
                  pl.BlockSpec((1, hv), fix2), pl.BlockSpec((1, A_DV), fix2),
                  pl.BlockSpec((LANES, LANES), fix2), pl.BlockSpec(lmask.shape, fix3),
                  pl.BlockSpec(sgn.shape, fix3)],
        out_specs=(pl.BlockSpec((rows, hv), lambda s, c: (s * n_chunks + c, 0)),
                   pl.BlockSpec((group, A_HEADS, A_DK, A_DV), lambda s, c: (s, 0, 0, 0))),
        scratch_shapes=[pltpu.VMEM((group, A_HEADS, A_DK, A_DV), F32), pltpu.VMEM((A_HEADS, rows, LANES), F32)],
        compiler_params=_cparams(("parallel", "arbitrary")),
        name=f"hgrn2_scan_r{rows}",
    )(z, s0, lb.reshape(1, hv), o_gain.reshape(1, A_DV), tri, lmask, sgn)
    return o, sfin


KEYS = 2 * ATT_BLOCK


def _pair_operand(x, kh):
    slab = x[:, (kh // 2) * LANES:(kh // 2 + 1) * LANES]
    lane = lax.broadcasted_iota(jnp.int32, slab.shape, 1)
    if kh % 2 == 0:
        lo = jnp.where(lane < HEAD_DIM, slab, 0.0)
        hi = pltpu.roll(lo, HEAD_DIM, 1)
    else:
        hi = jnp.where(lane >= HEAD_DIM, slab, 0.0)
        lo = pltpu.roll(hi, HEAD_DIM, 1)
    return jnp.concatenate([lo, hi], axis=0).astype(BF16)


def _window_bias(rows, jmin):
    t_i = lax.broadcasted_iota(jnp.int32, (rows, 2 * KEYS), 0)
    c_i = lax.broadcasted_iota(jnp.int32, (rows, 2 * KEYS), 1)
    j_i = c_i & (ATT_BLOCK - 1)
    own = (c_i & ATT_BLOCK) != 0
    ok = (own & (j_i <= t_i)) | (jnp.logical_not(own) & (j_i >= t_i) & (j_i >= jmin))
    return jnp.where(ok, 0.0, -jnp.inf).astype(F32)


def _pair_softmax(s, sink_a, sink_b):
    probs, rinv = [], []
    for hh, sink in enumerate((sink_a, sink_b)):
        sh = s[:, hh * KEYS:(hh + 1) * KEYS]
        m = jnp.maximum(jnp.max(sh, axis=-1, keepdims=True), sink)
        p = jnp.exp(sh - m)
        den = jnp.sum(p, axis=-1, keepdims=True) + jnp.exp(sink - m)
        probs.append(p.astype(BF16))
        rinv.append(1.0 / den)
    lane = lax.broadcasted_iota(jnp.int32, (s.shape[0], LANES), 1)
    return jnp.concatenate(probs, axis=1), jnp.where(lane < HEAD_DIM, rinv[0], rinv[1])


def _attn_prompt_kernel(sink_ref, q_ref, kp_ref, ko_ref, vp_ref, vo_ref, km_ref, vm_ref, o_ref,
                        k2_scr, v2_scr, s_scr, p_scr, r_scr):
    n = pl.program_id(0)
    nbp = SEQ // ATT_BLOCK
    n_pairs = Q_HEADS // 2

    @pl.when(n >= BATCH * nbp)
    def _():
        o_ref[...] = jnp.zeros_like(o_ref)

    @pl.when(n < BATCH * nbp)
    def _():
        first = (n % nbp) == 0
        jmin = jnp.where(first, ATT_BLOCK - N_META, 0)
        k = jnp.concatenate([jnp.where(first, km_ref[...], kp_ref[...]), ko_ref[...]], axis=0)
        v = jnp.concatenate([jnp.where(first, vm_ref[...], vp_ref[...]), vo_ref[...]], axis=0)
        bias = _window_bias(ATT_BLOCK, jmin)
        for kh in range(KV_HEADS):
            k2_scr[kh] = _pair_operand(k, kh)
            v2_scr[kh] = _pair_operand(v, kh)
        for pair in range(n_pairs):
            s_scr[pair] = _nt_dot(q_ref[:, pair * LANES:(pair + 1) * LANES], k2_scr[pair // 2]) + bias
        for pair in range(n_pairs):
            p, rinv = _pair_softmax(s_scr[pair], sink_ref[2 * pair], sink_ref[2 * pair + 1])
            p_scr[pair] = p
            r_scr[pair] = rinv
        for pair in range(n_pairs):
            o = jnp.dot(p_scr[pair], v2_scr[pair // 2], preferred_element_type=F32) * r_scr[pair]
            o_ref[:, pair * LANES:(pair + 1) * LANES] = o.astype(o_ref.dtype)


def attention_prompt(q_all, k_all, v_all, k_meta_blk, v_meta_blk, sinks):
    n_prompt_blocks = T_PROMPT // ATT_BLOCK
    n_blocks = T_ALL // ATT_BLOCK
    n_pairs = Q_HEADS // 2
    own = lambda n, sk: (jnp.minimum(n, n_prompt_blocks - 1), 0)
    prev = lambda n, sk: (jnp.maximum(jnp.minimum(n, n_prompt_blocks - 1) - 1, 0), 0)
    fix = lambda n, sk: (0, 0)
    grid_spec = pltpu.PrefetchScalarGridSpec(
        num_scalar_prefetch=1,
        grid=(n_blocks,),
        in_specs=[pl.BlockSpec((ATT_BLOCK, D_MODEL), own),
                  pl.BlockSpec((ATT_BLOCK, KV_DIM), prev), pl.BlockSpec((ATT_BLOCK, KV_DIM), own),
                  pl.BlockSpec((ATT_BLOCK, KV_DIM), prev), pl.BlockSpec((ATT_BLOCK, KV_DIM), own),
                  pl.BlockSpec((ATT_BLOCK, KV_DIM), fix), pl.BlockSpec((ATT_BLOCK, KV_DIM), fix)],
        out_specs=pl.BlockSpec((ATT_BLOCK, D_MODEL), lambda n, sk: (n, 0)),
        scratch_shapes=[pltpu.VMEM((KV_HEADS, 2 * KEYS, LANES), BF16), pltpu.VMEM((KV_HEADS, 2 * KEYS, LANES), BF16),
                        pltpu.VMEM((n_pairs, ATT_BLOCK, 2 * KEYS), F32),
                        pltpu.VMEM((n_pairs, ATT_BLOCK, 2 * KEYS), BF16),
                        pltpu.VMEM((n_pairs, ATT_BLOCK, LANES), F32)],
    )
    return pl.pallas_call(
        _attn_prompt_kernel,
        out_shape=jax.ShapeDtypeStruct((T_ALL, D_MODEL), BF16),
        grid_spec=grid_spec,
        compiler_params=_cparams(("parallel",)),
        name="attention_prompt",
    )(sinks, q_all, k_all, k_all, v_all, v_all, k_meta_blk, v_meta_blk)


SAMPLE_GROUP = ATT_BLOCK // DEC_SEQ


def _attn_sample_kernel(sink_ref, q_ref, ck_ref, cv_ref, kn_ref, vn_ref, buf_ref, o_ref, kw_ref, vw_ref,
                        qf_scr, of_scr, k2_scr, v2_scr):
    del buf_ref
    qrows = 2 * DEC_SEQ
    qf_scr[...] = q_ref[...].astype(F32)
    bias = _window_bias(qrows, 0)
    zq = jnp.zeros((qrows - DEC_SEQ, D_MODEL), F32)
    zk = jnp.zeros((ATT_BLOCK - DEC_SEQ, KV_DIM), F32)

    n_pairs = Q_HEADS // 2
    lanes_of = lambda pair: slice(pair * LANES, (pair + 1) * LANES)

    def seq_pair_body(it, carry):
        seqs = (2 * it, 2 * it + 1)
        r_new = [pl.multiple_of(i * DEC_SEQ, DEC_SEQ) for i in seqs]
        qs = []
        for u, i in enumerate(seqs):
            qs.append(jnp.concatenate([qf_scr[pl.ds(r_new[u], DEC_SEQ), :], zq], axis=0).astype(BF16))
            old = lambda c_ref: jnp.concatenate([c_ref[i, :, kh, :] for kh in range(KV_HEADS)], axis=1)
            k = jnp.concatenate([old(ck_ref), kn_ref[pl.ds(r_new[u], DEC_SEQ), :], zk], axis=0)
            v = jnp.concatenate([old(cv_ref), vn_ref[pl.ds(r_new[u], DEC_SEQ), :], zk], axis=0)
            for c_ref, n_ref, w_ref in ((ck_ref, kn_ref, kw_ref), (cv_ref, vn_ref, vw_ref)):
                w_ref[i, 0:WINDOW - DEC_SEQ] = c_ref[i, DEC_SEQ:WINDOW]
                for kh in range(KV_HEADS):
                    w_ref[i, WINDOW - DEC_SEQ:WINDOW, kh, :] = n_ref[pl.ds(r_new[u], DEC_SEQ),
                                                                     kh * HEAD_DIM:(kh + 1) * HEAD_DIM]
            for kh in range(KV_HEADS):
                k2_scr[u, kh] = _pair_operand(k, kh)
                v2_scr[u, kh] = _pair_operand(v, kh)
        scores = [[_nt_dot(qs[u][:, lanes_of(pair)], k2_scr[u, pair // 2]) + bias for pair in range(n_pairs)]
                  for u in range(2)]
        soft = [[_pair_softmax(s, sink_ref[2 * pair], sink_ref[2 * pair + 1]) for pair, s in enumerate(scores[u])]
                for u in range(2)]
        for u in range(2):
            for pair, (p, rinv) in enumerate(soft[u]):
                o = jnp.dot(p, v2_scr[u, pair // 2], preferred_element_type=F32) * rinv
                of_scr[pl.ds(r_new[u], DEC_SEQ), lanes_of(pair)] = o[:DEC_SEQ]
        return carry

    lax.fori_loop(0, SAMPLE_GROUP // 2, seq_pair_body, 0)
    o_ref[...] = of_scr[...].astype(o_ref.dtype)


def attention_sample(q_all, cache_k, cache_v, k_all, v_all, sinks, out_buf):
    first_blk = OFF_SAMPLE // ATT_BLOCK
    new = lambda g, sk: (first_blk + g, 0)
    old = pl.BlockSpec((SAMPLE_GROUP, WINDOW, KV_HEADS, HEAD_DIM), lambda g, sk: (g, 0, 0, 0))
    grid_spec = pltpu.PrefetchScalarGridSpec(
        num_scalar_prefetch=1,
        grid=(DEC_BATCH // SAMPLE_GROUP,),
        in_specs=[pl.BlockSpec((ATT_BLOCK, D_MODEL), new),
                  old, old,
                  pl.BlockSpec((ATT_BLOCK, KV_DIM), new), pl.BlockSpec((ATT_BLOCK, KV_DIM), new),
                  pl.BlockSpec(memory_space=pl.ANY)],
        out_specs=(pl.BlockSpec((ATT_BLOCK, D_MODEL), new), old, old),
        scratch_shapes=[pltpu.VMEM((ATT_BLOCK, D_MODEL), F32), pltpu.VMEM((ATT_BLOCK, D_MODEL), F32),
                        pltpu.VMEM((2, KV_HEADS, 2 * KEYS, LANES), BF16),
                        pltpu.VMEM((2, KV_HEADS, 2 * KEYS, LANES), BF16)],
    )
    window = jax.ShapeDtypeStruct(cache_k.shape, cache_k.dtype)
    return pl.pallas_call(
        _attn_sample_kernel,
        out_shape=(jax.ShapeDtypeStruct(out_buf.shape, out_buf.dtype), window, window),
        grid_spec=grid_spec,
        input_output_aliases={6: 0},
        compiler_params=_cparams(("parallel",)),
        name="attention_sample",
    )(sinks, q_all, cache_k, cache_v, k_all, v_all, out_buf)


ROUTE_COLS = 8
ROUTE_ROWS = 48


def _route_kernel(*refs, parts):
    i = pl.program_id(0)
    if parts:
        (ap_ref, as_ref, at_ref, w_ref, xp_ref, xs_ref, xt_ref), refs = refs[:7], refs[7:]
        a = _pick_part(i, ap_ref, as_ref, at_ref, BF16)
        x = _pick_part(i, xp_ref, xs_ref, xt_ref, F32)
    else:
        (a_ref, w_ref, x_ref), refs = refs[:3], refs[3:]
        a, x = a_ref[...], x_ref[...]
    g_ref, wh_ref, wl_ref, br_ref, utri_ref, h_ref, xn_ref, rec_ref, rect_ref, cnt_ref, cnt_scr = refs

    @pl.when(i == 0)
    def _():
        cnt_scr[...] = jnp.zeros_like(cnt_scr)

    h = x + jnp.dot(a, w_ref[...], preferred_element_type=F32)
    h_ref[...] = h
    xn = _rms(h, g_ref[...])
    xn_ref[...] = _pack_halves(xn)
    xh = xn.astype(BF16)
    xl = (xn - xh.astype(F32)).astype(BF16)
    logits = (_nt_dot(wh_ref[...], xh) + (_nt_dot(wl_ref[...], xh) + _nt_dot(wh_ref[...], xl)))[:ROUTE_ROWS]
    logits = logits + br_ref[...]
    tokens = logits.shape[1]
    rid = lax.broadcasted_iota(jnp.int32, (ROUTE_ROWS, tokens), 0).astype(F32)
    neg = jnp.float32(-jnp.inf)
    big = jnp.float32(ROUTE_ROWS)

    is_g = (rid >= N_EXPERTS) & (rid < N_EXPERTS + N_GROUPS)
    gl = jnp.where(is_g, logits, neg)
    gmax = jnp.max(gl, axis=0, keepdims=True)
    gsel = jnp.min(jnp.where(gl == gmax, rid, big), axis=0, keepdims=True) - N_EXPERTS
    gden = jnp.sum(jnp.where(is_g, jnp.exp(gl - gmax), 0.0), axis=0, keepdims=True)
    gw = 1.0 / gden

    in_grp = (rid >= gsel * EXPERTS_PER_GROUP) & (rid < (gsel + 1) * EXPERTS_PER_GROUP)
    el = jnp.where(in_grp, logits, neg)
    t1 = jnp.max(el, axis=0, keepdims=True)
    e1 = jnp.min(jnp.where(el == t1, rid, big), axis=0, keepdims=True)
    el2 = jnp.where(rid == e1, neg, el)
    t2 = jnp.max(el2, axis=0, keepdims=True)
    e2 = jnp.min(jnp.where(el2 == t2, rid, big), axis=0, keepdims=True)
    x2 = jnp.exp(t2 - t1)
    w1 = gw / (1.0 + x2)
    w2 = gw * x2 / (1.0 + x2)

    oh1 = (rid == e1).astype(F32)
    oh2 = (rid == e2).astype(F32)
    oh = oh1 + oh2
    before = jnp.dot(oh.astype(BF16), utri_ref[...], preferred_element_type=F32)
    base = cnt_scr[...] + before
    r1 = jnp.sum(base * oh1, axis=0, keepdims=True)
    r2 = jnp.sum(base * oh2, axis=0, keepdims=True)
    cnt_scr[...] = cnt_scr[...] + jnp.sum(oh, axis=1, keepdims=True)

    zero = jnp.zeros_like(w1)
    rect = jnp.concatenate([e1, e2, r1, r2, w1, w2, zero, zero], axis=0)
    rect_ref[...] = rect
    wide = jnp.concatenate([rect, jnp.zeros((LANES - ROUTE_COLS, tokens), F32)], axis=0)
    rec_ref[...] = jnp.concatenate([wide[:, t0:t0 + LANES].T for t0 in range(0, tokens, LANES)], axis=0)
    cnt_ref[...] = cnt_scr[...]


def moe_route(a, w_out_bf16, x, gain, w_router, b_router, utri):
    parts = isinstance(a, tuple)
    t, d = T_ALL, D_MODEL
    row = lambda i: (i, 0)
    fix = lambda i: (0, 0)
    w_t = w_router.T
    w_hi = w_t.astype(BF16)
    w_lo = (w_t - w_hi.astype(F32)).astype(BF16)
    w_spec = pl.BlockSpec((d, d), fix)
    if parts:
        pre_specs = _parts_specs(d) + [w_spec] + _parts_specs(d)
        pre_args = (*a, w_out_bf16, *x)
    else:
        pre_specs = [pl.BlockSpec((ROW_TILE, d), row), w_spec, pl.BlockSpec((ROW_TILE, d), row)]
        pre_args = (a, w_out_bf16, x)
    return pl.pallas_call(
        functools.partial(_route_kernel, parts=parts),
        out_shape=(jax.ShapeDtypeStruct((t, d), F32),
                   jax.ShapeDtypeStruct((t, d // 2), U32), jax.ShapeDtypeStruct((t, LANES), F32),
                   jax.ShapeDtypeStruct((ROUTE_COLS, t), F32), jax.ShapeDtypeStruct((ROUTE_ROWS, 1), F32)),
        grid=(t // ROW_TILE,),
        in_specs=pre_specs + [pl.BlockSpec((1, d), fix),
                              pl.BlockSpec((LANES, d), fix), pl.BlockSpec((LANES, d), fix),
                              pl.BlockSpec((ROUTE_ROWS, 1), fix), pl.BlockSpec((ROW_TILE, ROW_TILE), fix)],
        out_specs=(pl.BlockSpec((ROW_TILE, d), row),
                   pl.BlockSpec((ROW_TILE, d // 2), row), pl.BlockSpec((ROW_TILE, LANES), row),
                   pl.BlockSpec((ROUTE_COLS, ROW_TILE), lambda i: (0, i)), pl.BlockSpec((ROUTE_ROWS, 1), fix)),
        scratch_shapes=[pltpu.VMEM((ROUTE_ROWS, 1), F32)],
        compiler_params=_cparams(("arbitrary",)),
        name="moe_route",
    )(*pre_args, gain.reshape(1, d), w_hi, w_lo, b_router, utri)


def _row_copy(src, src_row, dst, dst_row, sem):
    return pltpu.make_async_copy(src.at[pl.ds(src_row, 1)], dst.at[pl.ds(dst_row, 1)], sem)


def _dispatch_kernel(dest_ref, xn_ref, xs_ref, sem):
    rows = xn_ref.shape[0]

    def issue(r, c):
        _row_copy(xn_ref, r, xs_ref, dest_ref[0, 0, r], sem).start(priority=0)
        _row_copy(xn_ref, r, xs_ref, dest_ref[0, 0, rows + r], sem).start(priority=1)
        return c

    lax.fori_loop(0, rows, issue, 0, unroll=8)
    for _ in range(2):
        pltpu.make_async_copy(xn_ref, xs_ref.at[pl.ds(0, rows)], sem).wait()


def moe_dispatch(xn, dest3, n_slots):
    t, d = xn.shape
    return pl.pallas_call(
        _dispatch_kernel,
        out_shape=jax.ShapeDtypeStruct((n_slots, d), F32),
        grid=(t // ROW_TILE,),
        in_specs=[pl.BlockSpec((1, 1, 2 * ROW_TILE), lambda i: (i, 0, 0), memory_space=pltpu.SMEM),
                  pl.BlockSpec((ROW_TILE, d), lambda i: (i, 0))],
        out_specs=pl.BlockSpec(memory_space=pl.ANY),
        scratch_shapes=[pltpu.SemaphoreType.DMA],
        compiler_params=_cparams(("arbitrary",)),
        name="moe_dispatch",
    )(dest3, xn)


SC_WINDOW = 64
SC_INDEX_WINDOW = 128


def _sc_mesh():
    return plsc.VectorSubcoreMesh(core_axis_name="core", subcore_axis_name="subcore")


def moe_dispatch_sc(xn, dest_a, dest_b, n_slots):
    t, d = xn.shape

    n_parts = SC_INDEX_WINDOW // SC_WINDOW

    @pl.kernel(out_type=jax.ShapeDtypeStruct((n_slots, d), xn.dtype), mesh=_sc_mesh(),
               scratch_types=[pltpu.VMEM((2, SC_WINDOW, d), xn.dtype), pltpu.SemaphoreType.DMA((2,)),
                              pltpu.SemaphoreType.DMA((2,))],
               name="moe_dispatch_sc")
    def run(x_hbm, id_hbm, da_hbm, db_hbm, o_hbm, buf, load_sem, store_sem):
        def body(id_vmem, da_vmem, db_vmem):
            part = lambda j: pl.ds(j * SC_WINDOW, SC_WINDOW)
            load = lambda j: pltpu.make_async_copy(x_hbm.at[id_vmem.at[0, part(j)]], buf.at[j % 2], load_sem.at[j % 2])
            load(0).start()
            for j in range(n_parts):
                load(j).wait()
                if j + 1 < n_parts:
                    load(j + 1).start()
                stores = [pltpu.make_async_copy(buf.at[j % 2], o_hbm.at[dv.at[0, part(j)]], store_sem.at[k])
                          for k, dv in enumerate((da_vmem, db_vmem))]
                for s in stores:
                    s.start()
                for s in stores:
                    s.wait()

        idx_spec = pl.BlockSpec((1, SC_INDEX_WINDOW), lambda i: (0, i))
        pltpu.emit_pipeline(
            body,
            grid=(t // SC_INDEX_WINDOW,),
            in_specs=[idx_spec, idx_spec, idx_spec],
            out_specs=[],
            core_axis_name=("core", "subcore"),
            dimension_semantics=(pltpu.PARALLEL,),
        )(id_hbm, da_hbm, db_hbm)

    return run(xn, jnp.arange(t, dtype=jnp.int32).reshape(1, t), dest_a, dest_b)


def moe_gather_sc(ys, dest_a, dest_b):
    d = ys.shape[1]
    t = dest_a.shape[1]
    out = jax.ShapeDtypeStruct((t, d), ys.dtype)

    n_moves = 2 * (SC_INDEX_WINDOW // SC_WINDOW)

    @pl.kernel(out_type=(out, out), mesh=_sc_mesh(),
               scratch_types=[pltpu.VMEM((2, SC_WINDOW, d), ys.dtype), pltpu.SemaphoreType.DMA((2,)),
                              pltpu.SemaphoreType.DMA((2,))],
               name="moe_gather_sc")
    def run(y_hbm, id_hbm, da_hbm, db_hbm, ga_hbm, gb_hbm, buf, load_sem, store_sem):
        def body(id_vmem, da_vmem, db_vmem):
            part = lambda m: pl.ds((m // 2) * SC_WINDOW, SC_WINDOW)
            src = lambda m: (da_vmem, db_vmem)[m % 2]
            dst = lambda m: (ga_hbm, gb_hbm)[m % 2]
            load = lambda m: pltpu.make_async_copy(y_hbm.at[src(m).at[0, part(m)]], buf.at[m % 2], load_sem.at[m % 2])
            store = lambda m: pltpu.make_async_copy(buf.at[m % 2], dst(m).at[id_vmem.at[0, part(m)]],
                                                    store_sem.at[m % 2])
            load(0).start()
            for m in range(n_moves):
                load(m).wait()
                if m >= 1:
                    store(m - 1).wait()
                if m + 1 < n_moves:
                    load(m + 1).start()
                store(m).start()
            store(n_moves - 1).wait()

        idx_spec = pl.BlockSpec((1, SC_INDEX_WINDOW), lambda i: (0, i))
        pltpu.emit_pipeline(
            body,
            grid=(t // SC_INDEX_WINDOW,),
            in_specs=[idx_spec, idx_spec, idx_spec],
            out_specs=[],
            core_axis_name=("core", "subcore"),
            dimension_semantics=(pltpu.PARALLEL,),
        )(id_hbm, da_hbm, db_hbm)

    return run(ys, jnp.arange(t, dtype=jnp.int32).reshape(1, t), dest_a, dest_b)


def _combine_dense_kernel(h_ref, rec_ref, ga_ref, gb_ref, *out_refs, split):
    i = pl.program_id(0)
    rec = rec_ref[...]
    res = h_ref[...] + rec[:, 4:5] * _unpack_halves(ga_ref[...]) + rec[:, 5:6] * _unpack_halves(gb_ref[...])
    if not split:
        out_refs[0][...] = res
    else:
        @pl.when(i < N_PROMPT_TILES)
        def _():
            out_refs[0][...] = res

        @pl.when((i >= N_PROMPT_TILES) & (i < N_PROMPT_TILES + N_SAMPLE_TILES))
        def _():
            out_refs[1][...] = res


def moe_combine_dense(h, rec, ga, gb, split=False):
    t, d = h.shape
    row = lambda i: (i, 0)
    if split:
        out_shape = (jax.ShapeDtypeStruct((T_PROMPT, d), F32), jax.ShapeDtypeStruct((T_SAMPLE, d), F32))
        out_specs = tuple(_parts_specs(d)[:2])
    else:
        out_shape = jax.ShapeDtypeStruct((t, d), F32)
        out_specs = pl.BlockSpec((ROW_TILE, d), row)
    return pl.pallas_call(
        functools.partial(_combine_dense_kernel, split=split),
        out_shape=out_shape,
        grid=(t // ROW_TILE,),
        in_specs=[pl.BlockSpec((ROW_TILE, d), row), pl.BlockSpec((ROW_TILE, LANES), row),
                  pl.BlockSpec((ROW_TILE, d // 2), row), pl.BlockSpec((ROW_TILE, d // 2), row)],
        out_specs=out_specs,
        compiler_params=_cparams(("arbitrary",)),
        name="moe_combine_dense",
    )(h, rec, ga, gb)


def _combine_kv_q_kernel(h_ref, rec_ref, ga_ref, gb_ref, gkv_ref, wkv_ref, gq_ref, wq_ref, hmean_ref,
                         hgk_ref, hgq_ref, cos_ref, sina_ref, sinb_ref, ho_ref, k_ref, v_ref, q_ref):
    rec = rec_ref[...]
    h = h_ref[...] + rec[:, 4:5] * _unpack_halves(ga_ref[...]) + rec[:, 5:6] * _unpack_halves(gb_ref[...])
    ho_ref[...] = h
    xhat = h * lax.rsqrt(jnp.mean(h * h, axis=-1, keepdims=True) + RMS_EPS)
    tables = (cos_ref[...], sina_ref[...], sinb_ref[...])
    zkv = jnp.dot((xhat * gkv_ref[...]).astype(BF16), wkv_ref[...], preferred_element_type=F32)
    k_ref[...] = _head_norm_rope(zkv[:, :KV_DIM], hmean_ref, hgk_ref[...], *tables)
    v_ref[...] = zkv[:, KV_DIM:]
    zq = jnp.dot((xhat * gq_ref[...]).astype(BF16), wq_ref[...], preferred_element_type=F32)
    q = _head_norm_rope(zq, hmean_ref, hgq_ref[...], *tables)
    q_ref[...] = (q * HEAD_DIM ** -0.5).astype(q_ref.dtype)


def moe_combine_kv_q(h, rec, ga, gb, kv_gain, kv_w_bf16, q_gain, wq_bf16, hmean, k_hgain, q_hgain,
                     cos_t, sina_t, sinb_t):
    t, d = h.shape
    row = lambda i: (i, 0)
    fix = lambda i: (0, 0)
    rope = pl.BlockSpec((ROW_TILE, LANES), _rope_tile)
    return pl.pallas_call(
        _combine_kv_q_kernel,
        out_shape=(jax.ShapeDtypeStruct((t, d), F32), jax.ShapeDtypeStruct((t, KV_DIM), F32),
                   jax.ShapeDtypeStruct((t, KV_DIM), F32), jax.ShapeDtypeStruct((t, d), BF16)),
        grid=(t // ROW_TILE,),
        in_specs=[pl.BlockSpec((ROW_TILE, d), row), pl.BlockSpec((ROW_TILE, LANES), row),
                  pl.BlockSpec((ROW_TILE, d // 2), row), pl.BlockSpec((ROW_TILE, d // 2), row),
                  pl.BlockSpec((1, d), fix), pl.BlockSpec((d, 2 * KV_DIM), fix),
                  pl.BlockSpec((1, d), fix), pl.BlockSpec((d, d), fix),
                  pl.BlockSpec((256, 256), fix), pl.BlockSpec((1, KV_DIM), fix), pl.BlockSpec((1, d), fix),
                  rope, rope, rope],
        out_specs=(pl.BlockSpec((ROW_TILE, d), row), pl.BlockSpec((ROW_TILE, KV_DIM), row),
                   pl.BlockSpec((ROW_TILE, KV_DIM), row), pl.BlockSpec((ROW_TILE, d), row)),
        compiler_params=_cparams(("parallel",)),
        name="moe_combine_kv_q",
    )(h, rec, ga, gb, kv_gain.reshape(1, d), kv_w_bf16, q_gain.reshape(1, d), wq_bf16, hmean, k_hgain, q_hgain,
      cos_t, sina_t, sinb_t)


def _ffn_kernel(wblk_ref, we_ref, wlo_ref, whi_ref, xs_ref, w13_ref, w2_ref, ys_ref, w13b, w2b):
    w = pl.program_id(0)
    prev = jnp.maximum(w - 1, 0)
    first_visit = (w == 0) | (wblk_ref[w] != wblk_ref[prev])
    lo = wlo_ref[w]
    hi = whi_ref[w]

    def ffn(x):
        x = _unpack_halves(x).astype(BF16)
        cw = D_EXPERT // FFN_CHUNKS
        gate_up = []
        for c in range(FFN_CHUNKS):
            a = jnp.dot(x, w13b[:, c * cw:(c + 1) * cw], preferred_element_type=F32)
            u = jnp.dot(x, w13b[:, D_EXPERT + c * cw:D_EXPERT + (c + 1) * cw], preferred_element_type=F32)
            gate_up.append((a, u))
        hmid = jnp.concatenate([(_silu(a) * u).astype(BF16) for a, u in gate_up], axis=1)
        return _pack_halves(jnp.dot(hmid, w2b[...], preferred_element_type=F32))

    @pl.when(hi > lo)
    def _():
        @pl.when((w == 0) | (we_ref[w] != we_ref[prev]))
        def _():
            w13b[...] = w13_ref[...].astype(BF16)
            w2b[...] = w2_ref[...].astype(BF16)

        whole = (lo == 0) & (hi == EXPERT_BLOCK)

        @pl.when(whole)
        def _():
            ys_ref[...] = ffn(xs_ref[...])

        half = EXPERT_BLOCK // 2
        for p in range(2):
            rows = slice(p * half, (p + 1) * half)
            touched = (lo < (p + 1) * half) & (hi > p * half)

            @pl.when(jnp.logical_not(whole) & touched)
            def _():
                y = ffn(xs_ref[rows, :])
                row = lax.broadcasted_iota(jnp.int32, y.shape, 0) + p * half
                mine = (row >= lo) & (row < hi)

                @pl.when(first_visit)
                def _():
                    ys_ref[rows, :] = jnp.where(mine, y, jnp.zeros_like(y))

                @pl.when(jnp.logical_not(first_visit))
                def _():
                    ys_ref[rows, :] = jnp.where(mine, y, ys_ref[rows, :])

            @pl.when(jnp.logical_not(whole) & jnp.logical_not(touched) & first_visit)
            def _():
                ys_ref[rows, :] = jnp.zeros((half, ys_ref.shape[1]), U32)


def moe_ffn(xs, work, w13_all, w2_all, layer):
    n_slots, dp = xs.shape
    d = 2 * dp
    n_work = work[0].shape[0]
    xmap = lambda w, wb, we, wlo, whi: (wb[w], 0)
    w_map = lambda w, wb, we, wlo, whi: (layer, we[w], 0, 0)
    grid_spec = pltpu.PrefetchScalarGridSpec(
        num_scalar_prefetch=4,
        grid=(n_work,),
        in_specs=[pl.BlockSpec((EXPERT_BLOCK, dp), xmap),
                  pl.BlockSpec((None, None, d, 2 * D_EXPERT), w_map),
                  pl.BlockSpec((None, None, D_EXPERT, d), w_map)],
        out_specs=pl.BlockSpec((EXPERT_BLOCK, dp), xmap),
        scratch_shapes=[pltpu.VMEM((d, 2 * D_EXPERT), BF16), pltpu.VMEM((D_EXPERT, d), BF16)],
    )
    return pl.pallas_call(
        _ffn_kernel,
        out_shape=jax.ShapeDtypeStruct((n_slots, dp), U32),
        grid_spec=grid_spec,
        compiler_params=_cparams(("arbitrary",)),
        name="moe_ffn",
    )(*work, xs, w13_all, w2_all)


def _ffn_work_items(cnt):
    n_slots = 2 * T_ALL
    n_blocks = n_slots // EXPERT_BLOCK
    n_work = n_blocks + N_EXPERTS - 1
    end = jnp.cumsum(cnt)
    start = end - cnt
    first_blk = start // EXPERT_BLOCK
    last_blk = jnp.maximum(end - 1, start) // EXPERT_BLOCK
    n_items = jnp.where(cnt > 0, last_blk - first_blk + 1, 0)
    item_end = jnp.cumsum(n_items)
    item_start = item_end - n_items
    w = jnp.arange(n_work, dtype=jnp.int32)
    used = w < item_end[-1]
    wq = jnp.minimum(w, item_end[-1] - 1)
    e = jnp.sum((item_end[:, None] <= wq[None, :]).astype(jnp.int32), axis=0)
    onehot = e[None, :] == jnp.arange(N_EXPERTS, dtype=jnp.int32)[:, None]
    of_e = lambda table: jnp.sum(jnp.where(onehot, table[:, None], 0), axis=0)
    blk = jnp.where(used, of_e(first_blk) + (w - of_e(item_start)), n_blocks - 1).astype(jnp.int32)
    lo = jnp.maximum(of_e(start), blk * EXPERT_BLOCK) - blk * EXPERT_BLOCK
    hi = jnp.minimum(of_e(end), (blk + 1) * EXPERT_BLOCK) - blk * EXPERT_BLOCK
    lo = jnp.where(used, lo, 0).astype(jnp.int32)
    hi = jnp.where(used, hi, 0).astype(jnp.int32)
    return start, (blk, e, lo, hi)


def _combine_kernel(dest_ref, h_ref, rec_ref, ys_ref, *rest, split):
    out_refs, (g1, g2, sem) = rest[:-3], rest[-3:]
    rows = h_ref.shape[0]
    i = pl.program_id(0)

    def issue(r, c):
        _row_copy(ys_ref, dest_ref[0, 0, r], g1, r, sem).start(priority=0)
        _row_copy(ys_ref, dest_ref[0, 0, rows + r], g2, r, sem).start(priority=1)
        return c

    lax.fori_loop(0, rows, issue, 0, unroll=8)
    for buf in (g1, g2):
        pltpu.make_async_copy(ys_ref.at[pl.ds(0, rows)], buf, sem).wait()
    rec = rec_ref[...]
    res = h_ref[...] + rec[:, 4:5] * g1[...] + rec[:, 5:6] * g2[...]
    if not split:
        out_refs[0][...] = res
    else:
        @pl.when(i < N_PROMPT_TILES)
        def _():
            out_refs[0][...] = res

        @pl.when((i >= N_PROMPT_TILES) & (i < N_PROMPT_TILES + N_SAMPLE_TILES))
        def _():
            out_refs[1][...] = res


def moe_combine(h, rec, ys, dest3, split=False):
    t, d = h.shape
    row = lambda i: (i, 0)
    if split:
        parts = _parts_specs(d)[:2]
        out_shape = (jax.ShapeDtypeStruct((T_PROMPT, d), F32), jax.ShapeDtypeStruct((T_SAMPLE, d), F32))
        out_specs = tuple(parts)
    else:
        out_shape = jax.ShapeDtypeStruct((t, d), F32)
        out_specs = pl.BlockSpec((ROW_TILE, d), row)
    return pl.pallas_call(
        functools.partial(_combine_kernel, split=split),
        out_shape=out_shape,
        grid=(t // ROW_TILE,),
        in_specs=[pl.BlockSpec((1, 1, 2 * ROW_TILE), lambda i: (i, 0, 0), memory_space=pltpu.SMEM),
                  pl.BlockSpec((ROW_TILE, d), row), pl.BlockSpec((ROW_TILE, LANES), row),
                  pl.BlockSpec(memory_space=pl.ANY)],
        out_specs=out_specs,
        scratch_shapes=[pltpu.VMEM((ROW_TILE, d), F32), pltpu.VMEM((ROW_TILE, d), F32),
                        pltpu.SemaphoreType.DMA],
        compiler_params=_cparams(("arbitrary",)),
        name="moe_combine",
    )(dest3, h, rec, ys)


def hier_moe_layer(a, w_out_bf16, x, layer, gain, w_group, b_group, w_expert, b_expert, w13_all, w2_all, utri,
                   finish):
    t = T_ALL
    pad = LANES - N_EXPERTS - N_GROUPS
    w_router = jnp.concatenate([w_expert, w_group, jnp.zeros((D_MODEL, pad), F32)], axis=1)
    b_router = jnp.concatenate([b_expert, b_group, jnp.zeros((pad,), F32)])[:ROUTE_ROWS].reshape(ROUTE_ROWS, 1)
    h, xn, rec, rect, counts = moe_route(a, w_out_bf16, x, gain, w_router, b_router, utri)

    cnt = counts[:N_EXPERTS, 0].astype(jnp.int32)
    start, work = _ffn_work_items(cnt)
    experts = jnp.arange(N_EXPERTS, dtype=jnp.int32)[:, None]

    def slot_of(e_row, rank_row):
        first = jnp.sum(jnp.where(e_row.astype(jnp.int32)[None, :] == experts, start[:, None], 0), axis=0)
        return (first + rank_row.astype(jnp.int32)).reshape(1, t)

    dest_a = slot_of(rect[0], rect[2])
    dest_b = slot_of(rect[1], rect[3])

    xs = moe_dispatch_sc(xn, dest_a, dest_b, 2 * t)
    ys = moe_ffn(xs, work, w13_all, w2_all, layer)
    ga, gb = moe_gather_sc(ys, dest_a, dest_b)
    return finish(h, rec, ga, gb)


def _rope_tables(pos):
    half = ROPE_DIM // 2
    lane = np.arange(LANES) % HEAD_DIM
    rotary = lane < ROPE_DIM
    inv = jnp.where(rotary, jnp.exp(-math.log(ROPE_THETA) * jnp.asarray(lane % half, F32) * (2.0 / ROPE_DIM)), 0.0)
    ang = pos.astype(F32)[:, None] * inv[None, :]
    cos, sin = jnp.cos(ang), jnp.sin(ang)
    first = jnp.asarray(lane < half)
    second = jnp.asarray(rotary & (lane >= half))
    return cos, jnp.where(first, -sin, 0.0), jnp.where(second, sin, 0.0)


def kernel(x_prompt, x_sample, state_hgrn, cache_k_win, cache_v_win, meta_tokens, a_norm, a_w_in, a_lower_logits, a_out_norm, a_w_out, kv_norm, kv_w, k_norm, b_norm, b_wq, b_q_norm, b_sinks, b_w_out, moe_norm, moe_w_group, moe_b_group, moe_w_expert, moe_b_expert, moe_w13, moe_w2):
    tail_rows = T_ALL - OFF_META
    x_parts = (x_prompt.reshape(T_PROMPT, D_MODEL), x_sample.reshape(T_SAMPLE, D_MODEL),
               jnp.concatenate([meta_tokens.astype(F32), jnp.zeros((tail_rows - N_META, D_MODEL), F32)], axis=0))
    pos = jnp.concatenate([N_META + jnp.arange(SEQ, dtype=jnp.int32),
                           jnp.tile(PAST_LEN + jnp.arange(DEC_SEQ, dtype=jnp.int32), ROW_TILE // DEC_SEQ),
                           jnp.arange(N_META, dtype=jnp.int32),
                           jnp.zeros((ROW_TILE - N_META,), jnp.int32)])
    cos_t, sina_t, sinb_t = _rope_tables(pos)
    r256 = np.arange(256)
    hmean = jnp.asarray((r256[:, None] // HEAD_DIM == r256[None, :] // HEAD_DIM).astype(np.float32) / HEAD_DIM, BF16)
    rt = np.arange(ROW_TILE)
    utri = jnp.asarray((rt[:, None] < rt[None, :]).astype(np.float32), BF16)
    lower = jnp.cumsum(jax.nn.softmax(a_lower_logits.astype(F32), axis=0), axis=0)

    moe = functools.partial(hier_moe_layer, w13_all=moe_w13, w2_all=moe_w2, utri=utri)

    z = in_project(x_parts, a_norm[0], a_w_in[0].astype(BF16))
    zero_state = jnp.zeros((1, A_HEADS, A_DK, A_DV), F32)
    o_meta, s_meta = hgrn2_scan(z, zero_state, lower[0], a_out_norm[0],
                                row_off=OFF_META, n_seq=1, seq_len=N_META)
    o_prompt, s_prompt = hgrn2_scan(z, s_meta, lower[0], a_out_norm[0], row_off=0, n_seq=BATCH, seq_len=SEQ)
    o_sample, s_sample = hgrn2_scan(z, state_hgrn[0].astype(F32), lower[0], a_out_norm[0],
                                    row_off=OFF_SAMPLE, n_seq=DEC_BATCH, seq_len=DEC_SEQ, group=SCAN_SAMPLE_GROUP)
    o_tail = jnp.concatenate([o_meta, jnp.zeros((tail_rows - N_META, D_MODEL), BF16)], axis=0)
    cache_k, cache_v, o_prompt = lax.optimization_barrier((cache_k_win.astype(F32), cache_v_win.astype(F32), o_prompt))
    finish0 = functools.partial(
        moe_combine_kv_q, kv_gain=kv_norm, kv_w_bf16=kv_w.astype(BF16), q_gain=b_norm[0], wq_bf16=b_wq[0].astype(BF16),
        hmean=hmean, k_hgain=jnp.tile(k_norm, KV_HEADS).reshape(1, KV_DIM),
        q_hgain=jnp.tile(b_q_norm[0], Q_HEADS).reshape(1, D_MODEL), cos_t=cos_t, sina_t=sina_t, sinb_t=sinb_t)
    h, k_all, v_all, q_all = moe((o_prompt, o_sample, o_tail), a_w_out[0].astype(BF16), x_parts, 0, moe_norm[0],
                                 moe_w_group[0], moe_b_group[0], moe_w_expert[0], moe_b_expert[0], finish=finish0)

    meta_blk = lambda a: jnp.concatenate([jnp.zeros((ATT_BLOCK - N_META, KV_DIM), F32),
                                          a[OFF_META:OFF_META + N_META]], axis=0)
    sinks = b_sinks[0].astype(F32)
    att_all = attention_prompt(q_all, k_all, v_all, meta_blk(k_all), meta_blk(v_all), sinks)
    att_all, k_win_s, v_win_s = attention_sample(q_all, cache_k, cache_v, k_all, v_all, sinks, att_all)
    y_p, y_s = moe(att_all, b_w_out[0].astype(BF16), h, 1, moe_norm[1], moe_w_group[1], moe_b_group[1],
                   moe_w_expert[1], moe_b_expert[1], finish=functools.partial(moe_combine_dense, split=True))

    y_prompt = y_p.reshape(BATCH, SEQ, D_MODEL)
    y_sample = y_s.reshape(DEC_BATCH, DEC_SEQ, D_MODEL)
    last = lambda a: jnp.stack([a[(b + 1) * SEQ - WINDOW:(b + 1) * SEQ] for b in range(BATCH)]).reshape(
        BATCH, WINDOW, KV_HEADS, HEAD_DIM)
    kp = last(k_all)
    vp = last(v_all)
    return (y_prompt, y_sample, s_prompt[None], s_sample[None], kp, vp, k_win_s, v_win_s)
```

```python
import functools
import math

import numpy as np
import jax
import jax.numpy as jnp
from jax import lax
from jax.experimental import pallas as pl
from jax.experimental.pallas import tpu as pltpu
from jax.experimental.pallas import tpu_sc as plsc

F32 = jnp.float32
BF16 = jnp.bfloat16
U32 = jnp.uint32

D_MODEL = 1024
BATCH = 4
SEQ = 4096
DEC_BATCH = 128
DEC_SEQ = 8
PAST_LEN = 8192
N_META = 16
A_HEADS = 8
A_DK = 128
A_DV = 128
Q_HEADS = 16
KV_HEADS = 4
HEAD_DIM = 64
KV_DIM = KV_HEADS * HEAD_DIM
WINDOW = 128
ROPE_DIM = 16
ROPE_THETA = 500000.0
N_GROUPS = 4
EXPERTS_PER_GROUP = 8
N_EXPERTS = 32
D_EXPERT = 512
RMS_EPS = 1e-6

LANES = 128
SUBLANES = 8
VMEM_LIMIT = 56 * 1024 * 1024

ROW_TILE = 512
T_PROMPT = BATCH * SEQ
T_SAMPLE = DEC_BATCH * DEC_SEQ
OFF_SAMPLE = T_PROMPT
OFF_META = T_PROMPT + T_SAMPLE
T_REAL = OFF_META + N_META
T_ALL = -(-T_REAL // ROW_TILE) * ROW_TILE
N_TILES = T_ALL // ROW_TILE

SCAN_CHUNK = 128
SCAN_SAMPLE_GROUP = 8
ATT_BLOCK = 128
EXPERT_BLOCK = 512
FFN_CHUNKS = 2


def _cparams(sem):
    return pltpu.CompilerParams(dimension_semantics=sem, vmem_limit_bytes=VMEM_LIMIT)


def _nt_dot(a, b):
    return lax.dot_general(a, b, (((1,), (1,)), ((), ())), preferred_element_type=F32)


def _rms(x, gain):
    ms = jnp.mean(x * x, axis=-1, keepdims=True)
    return x * lax.rsqrt(ms + RMS_EPS) * gain


def _silu(x):
    return x * jax.nn.sigmoid(x)


def _pack_halves(x):
    w = x.shape[1] // 2
    hi = lax.bitcast_convert_type(x[:, :w].astype(BF16).astype(F32), U32)
    lo = lax.bitcast_convert_type(x[:, w:].astype(BF16).astype(F32), U32)
    return hi | (lo >> 16)


def _unpack_halves(p):
    hi = lax.bitcast_convert_type(p & jnp.uint32(0xFFFF0000), F32)
    lo = lax.bitcast_convert_type(p << 16, F32)
    return jnp.concatenate([hi, lo], axis=1)


N_PROMPT_TILES = T_PROMPT // ROW_TILE
N_SAMPLE_TILES = T_SAMPLE // ROW_TILE


def _parts_specs(width):
    return [pl.BlockSpec((ROW_TILE, width), lambda i: (jnp.minimum(i, N_PROMPT_TILES - 1), 0)),
            pl.BlockSpec((ROW_TILE, width), lambda i: (jnp.clip(i - N_PROMPT_TILES, 0, N_SAMPLE_TILES - 1), 0)),
            pl.BlockSpec((ROW_TILE, width), lambda i: (0, 0))]


def _pick_part(i, p_ref, s_ref, t_ref, dtype):
    return jnp.where(i < N_PROMPT_TILES, p_ref[...].astype(dtype),
                     jnp.where(i < N_PROMPT_TILES + N_SAMPLE_TILES, s_ref[...].astype(dtype),
                               t_ref[...].astype(dtype)))


def _in_proj_kernel(xp_ref, xs_ref, xt_ref, g_ref, w_ref, o_ref):
    x = _pick_part(pl.program_id(0), xp_ref, xs_ref, xt_ref, F32)
    xn = _rms(x, g_ref[...])
    o_ref[...] = jnp.dot(xn.astype(BF16), w_ref[...], preferred_element_type=F32)


def in_project(x_parts, gain, w_bf16):
    d, n = w_bf16.shape
    return pl.pallas_call(
        _in_proj_kernel,
        out_shape=jax.ShapeDtypeStruct((T_ALL, n), F32),
        grid=(N_TILES,),
        in_specs=_parts_specs(d) + [pl.BlockSpec((1, d), lambda i: (0, 0)),
                                    pl.BlockSpec((d, n), lambda i: (0, 0))],
        out_specs=pl.BlockSpec((ROW_TILE, n), lambda i: (i, 0)),
        compiler_params=_cparams(("parallel",)),
        name="in_project",
    )(*x_parts, gain.reshape(1, d), w_bf16)


def _mixer_out_kernel(ap_ref, as_ref, at_ref, w_ref, xp_ref, xs_ref, xt_ref, o_ref):
    i = pl.program_id(0)
    a = _pick_part(i, ap_ref, as_ref, at_ref, BF16)
    x = _pick_part(i, xp_ref, xs_ref, xt_ref, F32)
    o_ref[...] = x + jnp.dot(a, w_ref[...], preferred_element_type=F32)


def mixer_out(a_parts, w_bf16, x_parts):
    k, n = w_bf16.shape
    return pl.pallas_call(
        _mixer_out_kernel,
        out_shape=jax.ShapeDtypeStruct((T_ALL, n), F32),
        grid=(N_TILES,),
        in_specs=_parts_specs(k) + [pl.BlockSpec((k, n), lambda i: (0, 0))] + _parts_specs(n),
        out_specs=pl.BlockSpec((ROW_TILE, n), lambda i: (i, 0)),
        compiler_params=_cparams(("parallel",)),
        name="mixer_out",
    )(*a_parts, w_bf16, *x_parts)


def _head_norm_rope(y, hmean_ref, hgain, cos_t, sina_t, sinb_t):
    rows, width = y.shape
    sq = (y * y).astype(BF16)
    parts = []
    for s in range(width // 256):
        parts.append(jnp.dot(sq[:, s * 256:(s + 1) * 256], hmean_ref[...], preferred_element_type=F32))
    ms = parts[0] if len(parts) == 1 else jnp.concatenate(parts, axis=1)
    yn = y * lax.rsqrt(ms + RMS_EPS) * hgain
    reps = width // LANES
    cos_w = jnp.concatenate([cos_t] * reps, axis=1)
    sina_w = jnp.concatenate([sina_t] * reps, axis=1)
    sinb_w = jnp.concatenate([sinb_t] * reps, axis=1)
    half = ROPE_DIM // 2
    nxt = pltpu.roll(yn, width - half, 1)
    prv = pltpu.roll(yn, half, 1)
    return yn * cos_w + nxt * sina_w + prv * sinb_w


def _kv_kernel(x_ref, g_ref, w_ref, hmean_ref, hg_ref, cos_ref, sina_ref, sinb_ref, k_ref, v_ref):
    xn = _rms(x_ref[...], g_ref[...])
    z = jnp.dot(xn.astype(BF16), w_ref[...], preferred_element_type=F32)
    k = _head_norm_rope(z[:, :KV_DIM], hmean_ref, hg_ref[...], cos_ref[...], sina_ref[...], sinb_ref[...])
    k_ref[...] = k
    v_ref[...] = z[:, KV_DIM:]


def _rope_tile(i):
    tiles_per_seq = SEQ // ROW_TILE
    n_prompt_tiles = T_PROMPT // ROW_TILE
    n_sample_tiles = T_SAMPLE // ROW_TILE
    return (jnp.where(i < n_prompt_tiles, i % tiles_per_seq,
                      jnp.where(i < n_prompt_tiles + n_sample_tiles, tiles_per_seq, tiles_per_seq + 1)), 0)


def kv_project(x, gain, w_bf16, hmean, hgain_w, cos_t, sina_t, sinb_t):
    t, d = x.shape
    row = lambda i: (i, 0)
    fix = lambda i: (0, 0)
    return pl.pallas_call(
        _kv_kernel,
        out_shape=(jax.ShapeDtypeStruct((t, KV_DIM), F32), jax.ShapeDtypeStruct((t, KV_DIM), F32)),
        grid=(t // ROW_TILE,),
        in_specs=[pl.BlockSpec((ROW_TILE, d), row), pl.BlockSpec((1, d), fix),
                  pl.BlockSpec((d, 2 * KV_DIM), fix), pl.BlockSpec((256, 256), fix),
                  pl.BlockSpec((1, KV_DIM), fix),
                  pl.BlockSpec((ROW_TILE, LANES), _rope_tile), pl.BlockSpec((ROW_TILE, LANES), _rope_tile),
                  pl.BlockSpec((ROW_TILE, LANES), _rope_tile)],
        out_specs=(pl.BlockSpec((ROW_TILE, KV_DIM), row), pl.BlockSpec((ROW_TILE, KV_DIM), row)),
        compiler_params=_cparams(("parallel",)),
        name="kv_project",
    )(x, gain.reshape(1, d), w_bf16, hmean, hgain_w, cos_t, sina_t, sinb_t)


def _q_kernel(x_ref, g_ref, w_ref, hmean_ref, hg_ref, cos_ref, sina_ref, sinb_ref, q_ref):
    xn = _rms(x_ref[...], g_ref[...])
    z = jnp.dot(xn.astype(BF16), w_ref[...], preferred_element_type=F32)
    q = _head_norm_rope(z, hmean_ref, hg_ref[...], cos_ref[...], sina_ref[...], sinb_ref[...])
    q_ref[...] = (q * HEAD_DIM ** -0.5).astype(q_ref.dtype)


def q_project(x, gain, w_bf16, hmean, hgain_w, cos_t, sina_t, sinb_t):
    t, d = x.shape
    row = lambda i: (i, 0)
    fix = lambda i: (0, 0)
    return pl.pallas_call(
        _q_kernel,
        out_shape=jax.ShapeDtypeStruct((t, d), BF16),
        grid=(t // ROW_TILE,),
        in_specs=[pl.BlockSpec((ROW_TILE, d), row), pl.BlockSpec((1, d), fix),
                  pl.BlockSpec((d, d), fix), pl.BlockSpec((256, 256), fix),
                  pl.BlockSpec((1, d), fix),
                  pl.BlockSpec((ROW_TILE, LANES), _rope_tile), pl.BlockSpec((ROW_TILE, LANES), _rope_tile),
                  pl.BlockSpec((ROW_TILE, LANES), _rope_tile)],
        out_specs=pl.BlockSpec((ROW_TILE, d), row),
        compiler_params=_cparams(("parallel",)),
        name="q_project",
    )(x, gain.reshape(1, d), w_bf16, hmean, hgain_w, cos_t, sina_t, sinb_t)


def _matmul_residual_kernel(a_ref, w_ref, r_ref, o_ref):
    o_ref[...] = r_ref[...] + jnp.dot(a_ref[...], w_ref[...], preferred_element_type=F32)


def matmul_residual(a_bf16, w_bf16, resid):
    t, k = a_bf16.shape
    n = w_bf16.shape[1]
    return pl.pallas_call(
        _matmul_residual_kernel,
        out_shape=jax.ShapeDtypeStruct((t, n), F32),
        grid=(t // ROW_TILE,),
        in_specs=[pl.BlockSpec((ROW_TILE, k), lambda i: (i, 0)),
                  pl.BlockSpec((k, n), lambda i: (0, 0)),
                  pl.BlockSpec((ROW_TILE, n), lambda i: (i, 0))],
        out_specs=pl.BlockSpec((ROW_TILE, n), lambda i: (i, 0)),
        compiler_params=_cparams(("parallel",)),
        name="matmul_residual",
    )(a_bf16, w_bf16, resid)


def _scan_levels(c):
    levels = []
    m = c
    while m >= 2:
        levels.append(m)
        m //= 2
    return levels


LOG2E = 1.4426950408889634


def _scan_kernel(z_ref, s0_ref, lb_ref, og_ref, tri_ref, lmask_ref, sgn_ref, o_ref, sfin_ref, s_scr, b_scr,
                 *, rows, seq_len):
    c_idx = pl.program_id(1)
    levels = _scan_levels(seq_len)
    n_sub = rows // seq_len
    hk = A_HEADS * A_DK

    @pl.when(c_idx == 0)
    def _():
        s_scr[...] = s0_ref[...]

    sub = lax.broadcasted_iota(jnp.int32, (SUBLANES, LANES), 0)
    row = lax.broadcasted_iota(jnp.int32, (LANES, LANES), 0)
    og = og_ref[...]

    def pad_f32(x):
        if x.shape[0] == LANES:
            return x
        return jnp.concatenate([x, jnp.zeros((LANES - x.shape[0], x.shape[1]), x.dtype)], axis=0)

    def pad_rows(x):
        return pad_f32(x).astype(BF16)

    def cols(part, h):
        return slice(part * hk + h * LANES, part * hk + (h + 1) * LANES)

    def gates(h):
        lb = lb_ref[:, cols(0, h)]
        forget = lb + (1.0 - lb) * jax.nn.sigmoid(z_ref[:, cols(1, h)])
        logf = jnp.log(forget)
        hi = logf.astype(BF16).astype(F32)
        r1 = logf - hi
        mid = r1.astype(BF16).astype(F32)
        lo = r1 - mid
        cs = jnp.dot(tri_ref[...], pad_rows(jnp.concatenate([hi, mid, lo], axis=1)),
                     preferred_element_type=F32)
        b = (cs[:rows, :LANES] + cs[:rows, LANES:2 * LANES]) + cs[:rows, 2 * LANES:]
        b_scr[h] = b
        return _silu(z_ref[:, cols(0, h)]), 1.0 - forget, b

    def bref_for(h, m):
        b_rows = b_scr.at[h]
        half = m // 2
        pieces = []
        for g in range(rows // SUBLANES):
            base = g * SUBLANES
            if m >= SUBLANES:
                r = (base // m) * m + half - 1
                piece = jnp.broadcast_to(b_rows[r:r + 1, :], (SUBLANES, LANES))
            else:
                piece = jnp.broadcast_to(b_rows[base + half - 1:base + half, :], (SUBLANES, LANES))
                for blk in range(1, SUBLANES // m):
                    r = base + blk * m + half - 1
                    piece = jnp.where(sub >= blk * m,
                                      jnp.broadcast_to(b_rows[r:r + 1, :], (SUBLANES, LANES)), piece)
            pieces.append(piece)
        return pieces[0] if len(pieces) == 1 else jnp.concatenate(pieces, axis=0)

    heads = range(A_HEADS)
    qkb = [gates(h) for h in heads]
    att = [_nt_dot(pad_rows(qf), pad_rows(kf)) * lmask_ref[len(levels)] for qf, kf, _ in qkb]
    for li, m in enumerate(levels):
        for h in heads:
            qf, kf, b = qkb[h]
            sgn = sgn_ref[li]
            e = jnp.exp2((b - bref_for(h, m)) * sgn)
            w = pad_rows(jnp.where(sgn > 0, qf, kf) * e)
            att[h] = att[h] + _nt_dot(w, w) * lmask_ref[li]

    def finish(h):
        qf, kf, b = qkb[h]
        b_rows = b_scr.at[h]
        v_b = pad_rows(z_ref[:, cols(2, h)])
        o_intra = jnp.dot(att[h].astype(BF16), v_b, preferred_element_type=F32)
        eb = jnp.exp(b)
        qs = qf * eb
        b_end = [jnp.broadcast_to(b_rows[(i + 1) * seq_len - 1:(i + 1) * seq_len, :], (seq_len, LANES))
                 for i in range(n_sub)]
        b_end = b_end[0] if n_sub == 1 else jnp.concatenate(b_end, axis=0)
        kd_t = pad_f32(kf * jnp.exp(b_end - b)).T.astype(BF16)
        eb_t = pad_f32(eb).T
        qs_b = pad_rows(qs)
        o = o_intra
        for i in range(n_sub):
            s_old = s_scr[i, h]
            first, last = i * seq_len, (i + 1) * seq_len - 1
            if n_sub == 1:
                qs_i, v_i = qs_b, v_b
            else:
                mine = (row >= first) & (row <= last)
                qs_i = jnp.where(mine, qs_b, jnp.zeros_like(qs_b))
                v_i = jnp.where(mine, v_b, jnp.zeros_like(v_b))
            o = o + jnp.dot(qs_i, s_old.astype(BF16), preferred_element_type=F32)
            decay = jnp.broadcast_to(eb_t[:, last:last + 1], (LANES, LANES))
            s_scr[i, h] = decay * s_old + jnp.dot(kd_t, v_i, preferred_element_type=F32)
        o = o[:rows]

        on = _rms(o, og) * _silu(z_ref[:, cols(3, h)])
        o_ref[:, cols(0, h)] = on.astype(o_ref.dtype)

    for h in heads:
        finish(h)

    @pl.when(c_idx == pl.num_programs(1) - 1)
    def _():
        sfin_ref[...] = s_scr[...]


def _scan_consts(rows, seq_len):
    levels = _scan_levels(seq_len)
    r = np.arange(LANES)
    t, s = r[:, None], r[None, :]
    live = (t < rows) & (s < rows)
    tri = ((s <= t) & (t // seq_len == s // seq_len) & live).astype(np.float32)
    masks, sgns = [], []
    for m in levels:
        masks.append(((t // m == s // m) & (t % m >= m // 2) & (s % m < m // 2) & live).astype(np.float32))
        sgns.append(np.broadcast_to(np.where(r[:rows, None] % m >= m // 2, LOG2E, -LOG2E), (rows, LANES)))
    masks.append(((t == s) & live).astype(np.float32))
    return jnp.asarray(tri, BF16), jnp.asarray(np.stack(masks), F32), jnp.asarray(np.stack(sgns), F32)


def hgrn2_scan(z, s0, lb, o_gain, *, row_off, n_seq, seq_len, group=1):
    hv = A_HEADS * A_DV
    if seq_len > SCAN_CHUNK:
        assert group == 1
        sub_len, rows, n_chunks, n_steps = SCAN_CHUNK, SCAN_CHUNK, seq_len // SCAN_CHUNK, n_seq
    else:
        sub_len, rows, n_chunks, n_steps = seq_len, group * seq_len, 1, n_seq // group
    blk_off = row_off // rows
    tri, lmask, sgn = _scan_consts(rows, sub_len)
    shared_s0 = s0.shape[0] == 1
    fix2 = lambda s, c: (0, 0)
    fix3 = lambda s, c: (0, 0, 0)
    o, sfin = pl.pallas_call(
        functools.partial(_scan_kernel, rows=rows, seq_len=sub_len),
        out_shape=(jax.ShapeDtypeStruct((n_seq * seq_len, hv), BF16 if rows % 16 == 0 else F32),
                   jax.ShapeDtypeStruct((n_seq, A_HEADS, A_DK, A_DV), F32)),
        grid=(n_steps, n_chunks),
        in_specs=[pl.BlockSpec((rows, 4 * hv), lambda s, c: (blk_off + s * n_chunks + c, 0)),
                  pl.BlockSpec((group, A_HEADS, A_DK, A_DV), (lambda s, c: (0, 0, 0, 0)) if shared_s0
                               else (lambda s, c: (s, 0, 0, 0))),
                  pl.BlockSpec((1, hv), fix2), pl.BlockSpec((1, A_DV), fix2),
                  pl.BlockSpec((LANES, LANES), fix2), pl.BlockSpec(lmask.shape, fix3),
                  pl.BlockSpec(sgn.shape, fix3)],
        out_specs=(pl.BlockSpec((rows, hv), lambda s, c: (s * n_chunks + c, 0)),
                   pl.BlockSpec((group, A_HEADS, A_DK, A_DV), lambda s, c: (s, 0, 0, 0))),
        scratch_shapes=[pltpu.VMEM((group, A_HEADS, A_DK, A_DV), F32), pltpu.VMEM((A_HEADS, rows, LANES), F32)],
        compiler_params=_cparams(("parallel", "arbitrary")),
        name=f"hgrn2_scan_r{rows}",
    )(z, s0, lb.reshape(1, hv), o_gain.reshape(1, A_DV), tri, lmask, sgn)
    return o, sfin


KEYS = 2 * ATT_BLOCK


def _pair_operand(x, kh):
    slab = x[:, (kh // 2) * LANES:(kh // 2 + 1) * LANES]
    lane = lax.broadcasted_iota(jnp.int32, slab.shape, 1)
    if kh % 2 == 0:
        lo = jnp.where(lane < HEAD_DIM, slab, 0.0)
        hi = pltpu.roll(lo, HEAD_DIM, 1)
    else:
        hi = jnp.where(lane >= HEAD_DIM, slab, 0.0)
        lo = pltpu.roll(hi, HEAD_DIM, 1)
    return jnp.concatenate([lo, hi], axis=0).astype(BF16)


def _window_bias(rows, jmin):
    t_i = lax.broadcasted_iota(jnp.int32, (rows, 2 * KEYS), 0)
    c_i = lax.broadcasted_iota(jnp.int32, (rows, 2 * KEYS), 1)
    j_i = c_i & (ATT_BLOCK - 1)
    own = (c_i & ATT_BLOCK) != 0
    ok = (own & (j_i <= t_i)) | (jnp.logical_not(own) & (j_i >= t_i) & (j_i >= jmin))
    return jnp.where(ok, 0.0, -jnp.inf).astype(F32)


def _pair_softmax(s, sink_a, sink_b):
    probs, rinv = [], []
    for hh, sink in enumerate((sink_a, sink_b)):
        sh = s[:, hh * KEYS:(hh + 1) * KEYS]
        m = jnp.maximum(jnp.max(sh, axis=-1, keepdims=True), sink)
        p = jnp.exp(sh - m)
        den = jnp.sum(p, axis=-1, keepdims=True) + jnp.exp(sink - m)
        probs.append(p.astype(BF16))
        rinv.append(1.0 / den)
    lane = lax.broadcasted_iota(jnp.int32, (s.shape[0], LANES), 1)
    return jnp.concatenate(probs, axis=1), jnp.where(lane < HEAD_DIM, rinv[0], rinv[1])


def _attn_prompt_kernel(sink_ref, q_ref, kp_ref, ko_ref, vp_ref, vo_ref, km_ref, vm_ref, o_ref,
                        k2_scr, v2_scr, s_scr, p_scr, r_scr, bias_scr):
    n = pl.program_id(0)
    nbp = SEQ // ATT_BLOCK
    n_pairs = Q_HEADS // 2

    @pl.when(n == 0)
    def _():
        bias_scr[0] = _window_bias(ATT_BLOCK, 0)
        bias_scr[1] = _window_bias(ATT_BLOCK, ATT_BLOCK - N_META)

    @pl.when(n >= BATCH * nbp)
    def _():
        o_ref[...] = jnp.zeros_like(o_ref)

    @pl.when(n < BATCH * nbp)
    def _():
        first = (n % nbp) == 0
        k = jnp.concatenate([jnp.where(first, km_ref[...], kp_ref[...]), ko_ref[...]], axis=0)
        v = jnp.concatenate([jnp.where(first, vm_ref[...], vp_ref[...]), vo_ref[...]], axis=0)
        bias = bias_scr[first.astype(jnp.int32)]
        for kh in range(KV_HEADS):
            k2_scr[kh] = _pair_operand(k, kh)
            v2_scr[kh] = _pair_operand(v, kh)
        for pair in range(n_pairs):
            s_scr[pair] = _nt_dot(q_ref[:, pair * LANES:(pair + 1) * LANES], k2_scr[pair // 2]) + bias
        for pair in range(n_pairs):
            p, rinv = _pair_softmax(s_scr[pair], sink_ref[2 * pair], sink_ref[2 * pair + 1])
            p_scr[pair] = p
            r_scr[pair] = rinv
        for pair in range(n_pairs):
            o = jnp.dot(p_scr[pair], v2_scr[pair // 2], preferred_element_type=F32) * r_scr[pair]
            o_ref[:, pair * LANES:(pair + 1) * LANES] = o.astype(o_ref.dtype)


def attention_prompt(q_all, k_all, v_all, k_meta_blk, v_meta_blk, sinks):
    n_prompt_blocks = T_PROMPT // ATT_BLOCK
    n_blocks = T_ALL // ATT_BLOCK
    n_pairs = Q_HEADS // 2
    own = lambda n, sk: (jnp.minimum(n, n_prompt_blocks - 1), 0)
    prev = lambda n, sk: (jnp.maximum(jnp.minimum(n, n_prompt_blocks - 1) - 1, 0), 0)
    fix = lambda n, sk: (0, 0)
    grid_spec = pltpu.PrefetchScalarGridSpec(
        num_scalar_prefetch=1,
        grid=(n_blocks,),
        in_specs=[pl.BlockSpec((ATT_BLOCK, D_MODEL), own),
                  pl.BlockSpec((ATT_BLOCK, KV_DIM), prev), pl.BlockSpec((ATT_BLOCK, KV_DIM), own),
                  pl.BlockSpec((ATT_BLOCK, KV_DIM), prev), pl.BlockSpec((ATT_BLOCK, KV_DIM), own),
                  pl.BlockSpec((ATT_BLOCK, KV_DIM), fix), pl.BlockSpec((ATT_BLOCK, KV_DIM), fix)],
        out_specs=pl.BlockSpec((ATT_BLOCK, D_MODEL), lambda n, sk: (n, 0)),
        scratch_shapes=[pltpu.VMEM((KV_HEADS, 2 * KEYS, LANES), BF16), pltpu.VMEM((KV_HEADS, 2 * KEYS, LANES), BF16),
                        pltpu.VMEM((n_pairs, ATT_BLOCK, 2 * KEYS), F32),
                        pltpu.VMEM((n_pairs, ATT_BLOCK, 2 * KEYS), BF16),
                        pltpu.VMEM((n_pairs, ATT_BLOCK, LANES), F32),
                        pltpu.VMEM((2, ATT_BLOCK, 2 * KEYS), F32)],
    )
    return pl.pallas_call(
        _attn_prompt_kernel,
        out_shape=jax.ShapeDtypeStruct((T_ALL, D_MODEL), BF16),
        grid_spec=grid_spec,
        compiler_params=_cparams(("arbitrary",)),
        name="attention_prompt",
    )(sinks, q_all, k_all, k_all, v_all, v_all, k_meta_blk, v_meta_blk)


SAMPLE_GROUP = ATT_BLOCK // DEC_SEQ


def _attn_sample_kernel(sink_ref, q_ref, ck_ref, cv_ref, kn_ref, vn_ref, buf_ref, o_ref, kw_ref, vw_ref,
                        qf_scr, of_scr, k2_scr, v2_scr):
    del buf_ref
    qrows = 2 * DEC_SEQ
    qf_scr[...] = q_ref[...].astype(F32)
    bias = _window_bias(qrows, 0)
    zq = jnp.zeros((qrows - DEC_SEQ, D_MODEL), F32)
    zk = jnp.zeros((ATT_BLOCK - DEC_SEQ, KV_DIM), F32)

    n_pairs = Q_HEADS // 2
    lanes_of = lambda pair: slice(pair * LANES, (pair + 1) * LANES)

    def seq_pair_body(it, carry):
        seqs = (2 * it, 2 * it + 1)
        r_new = [pl.multiple_of(i * DEC_SEQ, DEC_SEQ) for i in seqs]
        qs = []
        for u, i in enumerate(seqs):
            qs.append(jnp.concatenate([qf_scr[pl.ds(r_new[u], DEC_SEQ), :], zq], axis=0).astype(BF16))
            old = lambda c_ref: jnp.concatenate([c_ref[i, :, kh, :] for kh in range(KV_HEADS)], axis=1)
            k = jnp.concatenate([old(ck_ref), kn_ref[pl.ds(r_new[u], DEC_SEQ), :], zk], axis=0)
            v = jnp.concatenate([old(cv_ref), vn_ref[pl.ds(r_new[u], DEC_SEQ), :], zk], axis=0)
            for c_ref, n_ref, w_ref in ((ck_ref, kn_ref, kw_ref), (cv_ref, vn_ref, vw_ref)):
                w_ref[i, 0:WINDOW - DEC_SEQ] = c_ref[i, DEC_SEQ:WINDOW]
                for kh in range(KV_HEADS):
                    w_ref[i, WINDOW - DEC_SEQ:WINDOW, kh, :] = n_ref[pl.ds(r_new[u], DEC_SEQ),
                                                                     kh * HEAD_DIM:(kh + 1) * HEAD_DIM]
            for kh in range(KV_HEADS):
                k2_scr[u, kh] = _pair_operand(k, kh)
                v2_scr[u, kh] = _pair_operand(v, kh)
        scores = [[_nt_dot(qs[u][:, lanes_of(pair)], k2_scr[u, pair // 2]) + bias for pair in range(n_pairs)]
                  for u in range(2)]
        soft = [[_pair_softmax(s, sink_ref[2 * pair], sink_ref[2 * pair + 1]) for pair, s in enumerate(scores[u])]
                for u in range(2)]
        for u in range(2):
            for pair, (p, rinv) in enumerate(soft[u]):
                o = jnp.dot(p, v2_scr[u, pair // 2], preferred_element_type=F32) * rinv
                of_scr[pl.ds(r_new[u], DEC_SEQ), lanes_of(pair)] = o[:DEC_SEQ]
        return carry

    lax.fori_loop(0, SAMPLE_GROUP // 2, seq_pair_body, 0)
    o_ref[...] = of_scr[...].astype(o_ref.dtype)


def attention_sample(q_all, cache_k, cache_v, k_all, v_all, sinks, out_buf):
    first_blk = OFF_SAMPLE // ATT_BLOCK
    new = lambda g, sk: (first_blk + g, 0)
    old = pl.BlockSpec((SAMPLE_GROUP, WINDOW, KV_HEADS, HEAD_DIM), lambda g, sk: (g, 0, 0, 0))
    grid_spec = pltpu.PrefetchScalarGridSpec(
        num_scalar_prefetch=1,
        grid=(DEC_BATCH // SAMPLE_GROUP,),
        in_specs=[pl.BlockSpec((ATT_BLOCK, D_MODEL), new),
                  old, old,
                  pl.BlockSpec((ATT_BLOCK, KV_DIM), new), pl.BlockSpec((ATT_BLOCK, KV_DIM), new),
                  pl.BlockSpec(memory_space=pl.ANY)],
        out_specs=(pl.BlockSpec((ATT_BLOCK, D_MODEL), new), old, old),
        scratch_shapes=[pltpu.VMEM((ATT_BLOCK, D_MODEL), F32), pltpu.VMEM((ATT_BLOCK, D_MODEL), F32),
                        pltpu.VMEM((2, KV_HEADS, 2 * KEYS, LANES), BF16),
                        pltpu.VMEM((2, KV_HEADS, 2 * KEYS, LANES), BF16)],
    )
    window = jax.ShapeDtypeStruct(cache_k.shape, cache_k.dtype)
    return pl.pallas_call(
        _attn_sample_kernel,
        out_shape=(jax.ShapeDtypeStruct(out_buf.shape, out_buf.dtype), window, window),
        grid_spec=grid_spec,
        input_output_aliases={6: 0},
        compiler_params=_cparams(("parallel",)),
        name="attention_sample",
    )(sinks, q_all, cache_k, cache_v, k_all, v_all, out_buf)


ROUTE_COLS = 8
ROUTE_ROWS = 48


def _route_kernel(*refs, parts):
    i = pl.program_id(0)
    if parts:
        (ap_ref, as_ref, at_ref, w_ref, xp_ref, xs_ref, xt_ref), refs = refs[:7], refs[7:]
        a = _pick_part(i, ap_ref, as_ref, at_ref, BF16)
        x = _pick_part(i, xp_ref, xs_ref, xt_ref, F32)
    else:
        (a_ref, w_ref, x_ref), refs = refs[:3], refs[3:]
        a, x = a_ref[...], x_ref[...]
    g_ref, wh_ref, wl_ref, br_ref, utri_ref, h_ref, xn_ref, rec_ref, rect_ref, cnt_ref, cnt_scr = refs

    @pl.when(i == 0)
    def _():
        cnt_scr[...] = jnp.zeros_like(cnt_scr)

    h = x + jnp.dot(a, w_ref[...], preferred_element_type=F32)
    h_ref[...] = h
    xn = _rms(h, g_ref[...])
    xn_ref[...] = _pack_halves(xn)
    xh = xn.astype(BF16)
    xl = (xn - xh.astype(F32)).astype(BF16)
    logits = (_nt_dot(wh_ref[...], xh) + (_nt_dot(wl_ref[...], xh) + _nt_dot(wh_ref[...], xl)))[:ROUTE_ROWS]
    logits = logits + br_ref[...]
    tokens = logits.shape[1]
    rid = lax.broadcasted_iota(jnp.int32, (ROUTE_ROWS, tokens), 0).astype(F32)
    neg = jnp.float32(-jnp.inf)
    big = jnp.float32(ROUTE_ROWS)

    is_g = (rid >= N_EXPERTS) & (rid < N_EXPERTS + N_GROUPS)
    gl = jnp.where(is_g, logits, neg)
    gmax = jnp.max(gl, axis=0, keepdims=True)
    gsel = jnp.min(jnp.where(gl == gmax, rid, big), axis=0, keepdims=True) - N_EXPERTS
    gden = jnp.sum(jnp.where(is_g, jnp.exp(gl - gmax), 0.0), axis=0, keepdims=True)
    gw = 1.0 / gden

    in_grp = (rid >= gsel * EXPERTS_PER_GROUP) & (rid < (gsel + 1) * EXPERTS_PER_GROUP)
    el = jnp.where(in_grp, logits, neg)
    t1 = jnp.max(el, axis=0, keepdims=True)
    e1 = jnp.min(jnp.where(el == t1, rid, big), axis=0, keepdims=True)
    el2 = jnp.where(rid == e1, neg, el)
    t2 = jnp.max(el2, axis=0, keepdims=True)
    e2 = jnp.min(jnp.where(el2 == t2, rid, big), axis=0, keepdims=True)
    x2 = jnp.exp(t2 - t1)
    w1 = gw / (1.0 + x2)
    w2 = gw * x2 / (1.0 + x2)

    oh1 = (rid == e1).astype(F32)
    oh2 = (rid == e2).astype(F32)
    oh = oh1 + oh2
    before = jnp.dot(oh.astype(BF16), utri_ref[...], preferred_element_type=F32)
    base = cnt_scr[...] + before
    r1 = jnp.sum(base * oh1, axis=0, keepdims=True)
    r2 = jnp.sum(base * oh2, axis=0, keepdims=True)
    cnt_scr[...] = cnt_scr[...] + jnp.sum(oh, axis=1, keepdims=True)

    zero = jnp.zeros_like(w1)
    rect = jnp.concatenate([e1, e2, r1, r2, w1, w2, zero, zero], axis=0)
    rect_ref[...] = rect
    wide = jnp.concatenate([rect, jnp.zeros((LANES - ROUTE_COLS, tokens), F32)], axis=0)
    rec_ref[...] = jnp.concatenate([wide[:, t0:t0 + LANES].T for t0 in range(0, tokens, LANES)], axis=0)
    cnt_ref[...] = cnt_scr[...]


def moe_route(a, w_out_bf16, x, gain, w_router, b_router, utri):
    parts = isinstance(a, tuple)
    t, d = T_ALL, D_MODEL
    row = lambda i: (i, 0)
    fix = lambda i: (0, 0)
    w_t = w_router.T
    w_hi = w_t.astype(BF16)
    w_lo = (w_t - w_hi.astype(F32)).astype(BF16)
    w_spec = pl.BlockSpec((d, d), fix)
    if parts:
        pre_specs = _parts_specs(d) + [w_spec] + _parts_specs(d)
        pre_args = (*a, w_out_bf16, *x)
    else:
        pre_specs = [pl.BlockSpec((ROW_TILE, d), row), w_spec, pl.BlockSpec((ROW_TILE, d), row)]
        pre_args = (a, w_out_bf16, x)
    return pl.pallas_call(
        functools.partial(_route_kernel, parts=parts),
        out_shape=(jax.ShapeDtypeStruct((t, d), F32),
                   jax.ShapeDtypeStruct((t, d // 2), U32), jax.ShapeDtypeStruct((t, LANES), F32),
                   jax.ShapeDtypeStruct((ROUTE_COLS, t), F32), jax.ShapeDtypeStruct((ROUTE_ROWS, 1), F32)),
        grid=(t // ROW_TILE,),
        in_specs=pre_specs + [pl.BlockSpec((1, d), fix),
                              pl.BlockSpec((LANES, d), fix), pl.BlockSpec((LANES, d), fix),
                              pl.BlockSpec((ROUTE_ROWS, 1), fix), pl.BlockSpec((ROW_TILE, ROW_TILE), fix)],
        out_specs=(pl.BlockSpec((ROW_TILE, d), row),
                   pl.BlockSpec((ROW_TILE, d // 2), row), pl.BlockSpec((ROW_TILE, LANES), row),
                   pl.BlockSpec((ROUTE_COLS, ROW_TILE), lambda i: (0, i)), pl.BlockSpec((ROUTE_ROWS, 1), fix)),
        scratch_shapes=[pltpu.VMEM((ROUTE_ROWS, 1), F32)],
        compiler_params=_cparams(("arbitrary",)),
        name="moe_route",
    )(*pre_args, gain.reshape(1, d), w_hi, w_lo, b_router, utri)


def _row_copy(src, src_row, dst, dst_row, sem):
    return pltpu.make_async_copy(src.at[pl.ds(src_row, 1)], dst.at[pl.ds(dst_row, 1)], sem)


def _dispatch_kernel(dest_ref, xn_ref, xs_ref, sem):
    rows = xn_ref.shape[0]

    def issue(r, c):
        _row_copy(xn_ref, r, xs_ref, dest_ref[0, 0, r], sem).start(priority=0)
        _row_copy(xn_ref, r, xs_ref, dest_ref[0, 0, rows + r], sem).start(priority=1)
        return c

    lax.fori_loop(0, rows, issue, 0, unroll=8)
    for _ in range(2):
        pltpu.make_async_copy(xn_ref, xs_ref.at[pl.ds(0, rows)], sem).wait()


def moe_dispatch(xn, dest3, n_slots):
    t, d = xn.shape
    return pl.pallas_call(
        _dispatch_kernel,
        out_shape=jax.ShapeDtypeStruct((n_slots, d), F32),
        grid=(t // ROW_TILE,),
        in_specs=[pl.BlockSpec((1, 1, 2 * ROW_TILE), lambda i: (i, 0, 0), memory_space=pltpu.SMEM),
                  pl.BlockSpec((ROW_TILE, d), lambda i: (i, 0))],
        out_specs=pl.BlockSpec(memory_space=pl.ANY),
        scratch_shapes=[pltpu.SemaphoreType.DMA],
        compiler_params=_cparams(("arbitrary",)),
        name="moe_dispatch",
    )(dest3, xn)


SC_WINDOW = 64
SC_INDEX_WINDOW = 128


def _sc_mesh():
    return plsc.VectorSubcoreMesh(core_axis_name="core", subcore_axis_name="subcore")


def moe_dispatch_sc(xn, dest_a, dest_b, n_slots):
    t, d = xn.shape

    n_parts = SC_INDEX_WINDOW // SC_WINDOW

    @pl.kernel(out_type=jax.ShapeDtypeStruct((n_slots, d), xn.dtype), mesh=_sc_mesh(),
               scratch_types=[pltpu.VMEM((2, SC_WINDOW, d), xn.dtype), pltpu.SemaphoreType.DMA((2,)),
                              pltpu.SemaphoreType.DMA((2,))],
               name="moe_dispatch_sc")
    def run(x_hbm, id_hbm, da_hbm, db_hbm, o_hbm, buf, load_sem, store_sem):
        def body(id_vmem, da_vmem, db_vmem):
            part = lambda j: pl.ds(j * SC_WINDOW, SC_WINDOW)
            load = lambda j: pltpu.make_async_copy(x_hbm.at[id_vmem.at[0, part(j)]], buf.at[j % 2], load_sem.at[j % 2])
            load(0).start()
            for j in range(n_parts):
                load(j).wait()
                if j + 1 < n_parts:
                    load(j + 1).start()
                stores = [pltpu.make_async_copy(buf.at[j % 2], o_hbm.at[dv.at[0, part(j)]], store_sem.at[k])
                          for k, dv in enumerate((da_vmem, db_vmem))]
                for s in stores:
                    s.start()
                for s in stores:
                    s.wait()

        idx_spec = pl.BlockSpec((1, SC_INDEX_WINDOW), lambda i: (0, i))
        pltpu.emit_pipeline(
            body,
            grid=(t // SC_INDEX_WINDOW,),
            in_specs=[idx_spec, idx_spec, idx_spec],
            out_specs=[],
            core_axis_name=("core", "subcore"),
            dimension_semantics=(pltpu.PARALLEL,),
        )(id_hbm, da_hbm, db_hbm)

    return run(xn, jnp.arange(t, dtype=jnp.int32).reshape(1, t), dest_a, dest_b)


def moe_gather_sc(ys, dest_a, dest_b):
    d = ys.shape[1]
    t = dest_a.shape[1]
    out = jax.ShapeDtypeStruct((t, d), ys.dtype)

    n_moves = 2 * (SC_INDEX_WINDOW // SC_WINDOW)

    @pl.kernel(out_type=(out, out), mesh=_sc_mesh(),
               scratch_types=[pltpu.VMEM((2, SC_WINDOW, d), ys.dtype), pltpu.SemaphoreType.DMA((2,)),
                              pltpu.SemaphoreType.DMA((2,))],
               name="moe_gather_sc")
    def run(y_hbm, id_hbm, da_hbm, db_hbm, ga_hbm, gb_hbm, buf, load_sem, store_sem):
        def body(id_vmem, da_vmem, db_vmem):
            part = lambda m: pl.ds((m // 2) * SC_WINDOW, SC_WINDOW)
            src = lambda m: (da_vmem, db_vmem)[m % 2]
            dst = lambda m: (ga_hbm, gb_hbm)[m % 2]
            load = lambda m: pltpu.make_async_copy(y_hbm.at[src(m).at[0, part(m)]], buf.at[m % 2], load_sem.at[m % 2])
            store = lambda m: pltpu.make_async_copy(buf.at[m % 2], dst(m).at[id_vmem.at[0, part(m)]],
                                                    store_sem.at[m % 2])
            load(0).start()
            for m in range(n_moves):
                load(m).wait()
                if m >= 1:
                    store(m - 1).wait()
                if m + 1 < n_moves:
                    load(m + 1).start()
                store(m).start()
            store(n_moves - 1).wait()

        idx_spec = pl.BlockSpec((1, SC_INDEX_WINDOW), lambda i: (0, i))
        pltpu.emit_pipeline(
            body,
            grid=(t // SC_INDEX_WINDOW,),
            in_specs=[idx_spec, idx_spec, idx_spec],
            out_specs=[],
            core_axis_name=("core", "subcore"),
            dimension_semantics=(pltpu.PARALLEL,),
        )(id_hbm, da_hbm, db_hbm)

    return run(ys, jnp.arange(t, dtype=jnp.int32).reshape(1, t), dest_a, dest_b)


def _combine_dense_kernel(h_ref, rec_ref, ga_ref, gb_ref, *out_refs, split):
    i = pl.program_id(0)
    rec = rec_ref[...]
    res = h_ref[...] + rec[:, 4:5] * _unpack_halves(ga_ref[...]) + rec[:, 5:6] * _unpack_halves(gb_ref[...])
    if not split:
        out_refs[0][...] = res
    else:
        @pl.when(i < N_PROMPT_TILES)
        def _():
            out_refs[0][...] = res

        @pl.when((i >= N_PROMPT_TILES) & (i < N_PROMPT_TILES + N_SAMPLE_TILES))
        def _():
            out_refs[1][...] = res


def moe_combine_dense(h, rec, ga, gb, split=False):
    t, d = h.shape
    row = lambda i: (i, 0)
    if split:
        out_shape = (jax.ShapeDtypeStruct((T_PROMPT, d), F32), jax.ShapeDtypeStruct((T_SAMPLE, d), F32))
        out_specs = tuple(_parts_specs(d)[:2])
    else:
        out_shape = jax.ShapeDtypeStruct((t, d), F32)
        out_specs = pl.BlockSpec((ROW_TILE, d), row)
    return pl.pallas_call(
        functools.partial(_combine_dense_kernel, split=split),
        out_shape=out_shape,
        grid=(t // ROW_TILE,),
        in_specs=[pl.BlockSpec((ROW_TILE, d), row), pl.BlockSpec((ROW_TILE, LANES), row),
                  pl.BlockSpec((ROW_TILE, d // 2), row), pl.BlockSpec((ROW_TILE, d // 2), row)],
        out_specs=out_specs,
        compiler_params=_cparams(("arbitrary",)),
        name="moe_combine_dense",
    )(h, rec, ga, gb)


def _combine_kv_q_kernel(h_ref, rec_ref, ga_ref, gb_ref, gkv_ref, wkv_ref, gq_ref, wq_ref, hmean_ref,
                         hgk_ref, hgq_ref, cos_ref, sina_ref, sinb_ref, ho_ref, k_ref, v_ref, q_ref):
    rec = rec_ref[...]
    h = h_ref[...] + rec[:, 4:5] * _unpack_halves(ga_ref[...]) + rec[:, 5:6] * _unpack_halves(gb_ref[...])
    ho_ref[...] = h
    xhat = h * lax.rsqrt(jnp.mean(h * h, axis=-1, keepdims=True) + RMS_EPS)
    tables = (cos_ref[...], sina_ref[...], sinb_ref[...])
    zkv = jnp.dot((xhat * gkv_ref[...]).astype(BF16), wkv_ref[...], preferred_element_type=F32)
    k_ref[...] = _head_norm_rope(zkv[:, :KV_DIM], hmean_ref, hgk_ref[...], *tables)
    v_ref[...] = zkv[:, KV_DIM:]
    zq = jnp.dot((xhat * gq_ref[...]).astype(BF16), wq_ref[...], preferred_element_type=F32)
    q = _head_norm_rope(zq, hmean_ref, hgq_ref[...], *tables)
    q_ref[...] = (q * HEAD_DIM ** -0.5).astype(q_ref.dtype)


def moe_combine_kv_q(h, rec, ga, gb, kv_gain, kv_w_bf16, q_gain, wq_bf16, hmean, k_hgain, q_hgain,
                     cos_t, sina_t, sinb_t):
    t, d = h.shape
    row = lambda i: (i, 0)
    fix = lambda i: (0, 0)
    rope = pl.BlockSpec((ROW_TILE, LANES), _rope_tile)
    return pl.pallas_call(
        _combine_kv_q_kernel,
        out_shape=(jax.ShapeDtypeStruct((t, d), F32), jax.ShapeDtypeStruct((t, KV_DIM), F32),
                   jax.ShapeDtypeStruct((t, KV_DIM), F32), jax.ShapeDtypeStruct((t, d), BF16)),
        grid=(t // ROW_TILE,),
        in_specs=[pl.BlockSpec((ROW_TILE, d), row), pl.BlockSpec((ROW_TILE, LANES), row),
                  pl.BlockSpec((ROW_TILE, d // 2), row), pl.BlockSpec((ROW_TILE, d // 2), row),
                  pl.BlockSpec((1, d), fix), pl.BlockSpec((d, 2 * KV_DIM), fix),
                  pl.BlockSpec((1, d), fix), pl.BlockSpec((d, d), fix),
                  pl.BlockSpec((256, 256), fix), pl.BlockSpec((1, KV_DIM), fix), pl.BlockSpec((1, d), fix),
                  rope, rope, rope],
        out_specs=(pl.BlockSpec((ROW_TILE, d), row), pl.BlockSpec((ROW_TILE, KV_DIM), row),
                   pl.BlockSpec((ROW_TILE, KV_DIM), row), pl.BlockSpec((ROW_TILE, d), row)),
        compiler_params=_cparams(("parallel",)),
        name="moe_combine_kv_q",
    )(h, rec, ga, gb, kv_gain.reshape(1, d), kv_w_bf16, q_gain.reshape(1, d), wq_bf16, hmean, k_hgain, q_hgain,
      cos_t, sina_t, sinb_t)


def _ffn_kernel(wblk_ref, we_ref, wlo_ref, whi_ref, wnext_ref, wpar_ref, xs_ref, w13_ref, w2_ref, ys_ref,
                w13f, w2f, w13b, w2b, wsem, *, layer):
    w = pl.program_id(0)
    prev = jnp.maximum(w - 1, 0)
    first_visit = (w == 0) | (wblk_ref[w] != wblk_ref[prev])
    lo = wlo_ref[w]
    hi = whi_ref[w]

    def weight_copies(e, par):
        return (pltpu.make_async_copy(w13_ref.at[layer, e], w13f.at[par], wsem.at[par, 0]),
                pltpu.make_async_copy(w2_ref.at[layer, e], w2f.at[par], wsem.at[par, 1]))

    @pl.when(w == 0)
    def _():
        for c in weight_copies(we_ref[0], 0):
            c.start()

    def ffn(x):
        x = _unpack_halves(x).astype(BF16)
        cw = D_EXPERT // FFN_CHUNKS
        gate_up = []
        for c in range(FFN_CHUNKS):
            a = jnp.dot(x, w13b[:, c * cw:(c + 1) * cw], preferred_element_type=F32)
            u = jnp.dot(x, w13b[:, D_EXPERT + c * cw:D_EXPERT + (c + 1) * cw], preferred_element_type=F32)
            gate_up.append((a, u))
        hmid = jnp.concatenate([(_silu(a) * u).astype(BF16) for a, u in gate_up], axis=1)
        return _pack_halves(jnp.dot(hmid, w2b[...], preferred_element_type=F32))

    @pl.when(hi > lo)
    def _():
        @pl.when((w == 0) | (we_ref[w] != we_ref[prev]))
        def _():
            par = wpar_ref[w]
            for c in weight_copies(we_ref[w], par):
                c.wait()
            w13b[...] = w13f[par].astype(BF16)
            w2b[...] = w2f[par].astype(BF16)
            nxt = wnext_ref[w]

            @pl.when(nxt >= 0)
            def _():
                for c in weight_copies(nxt, 1 - par):
                    c.start()

        whole = (lo == 0) & (hi == EXPERT_BLOCK)

        @pl.when(whole)
        def _():
            ys_ref[...] = ffn(xs_ref[...])

        half = EXPERT_BLOCK // 2
        for p in range(2):
            rows = slice(p * half, (p + 1) * half)
            touched = (lo < (p + 1) * half) & (hi > p * half)

            @pl.when(jnp.logical_not(whole) & touched)
            def _():
                y = ffn(xs_ref[rows, :])
                row = lax.broadcasted_iota(jnp.int32, y.shape, 0) + p * half
                mine = (row >= lo) & (row < hi)

                @pl.when(first_visit)
                def _():
                    ys_ref[rows, :] = jnp.where(mine, y, jnp.zeros_like(y))

                @pl.when(jnp.logical_not(first_visit))
                def _():
                    ys_ref[rows, :] = jnp.where(mine, y, ys_ref[rows, :])

            @pl.when(jnp.logical_not(whole) & jnp.logical_not(touched) & first_visit)
            def _():
                ys_ref[rows, :] = jnp.zeros((half, ys_ref.shape[1]), U32)


def moe_ffn(xs, work, w13_all, w2_all, layer):
    n_slots, dp = xs.shape
    d = 2 * dp
    n_work = work[0].shape[0]
    xmap = lambda w, *prefetch: (prefetch[0][w], 0)
    grid_spec = pltpu.PrefetchScalarGridSpec(
        num_scalar_prefetch=len(work),
        grid=(n_work,),
        in_specs=[pl.BlockSpec((EXPERT_BLOCK, dp), xmap),
                  pl.BlockSpec(memory_space=pl.ANY), pl.BlockSpec(memory_space=pl.ANY)],
        out_specs=pl.BlockSpec((EXPERT_BLOCK, dp), xmap),
        scratch_shapes=[pltpu.VMEM((2, d, 2 * D_EXPERT), F32), pltpu.VMEM((2, D_EXPERT, d), F32),
                        pltpu.VMEM((d, 2 * D_EXPERT), BF16), pltpu.VMEM((D_EXPERT, d), BF16),
                        pltpu.SemaphoreType.DMA((2, 2))],
    )
    return pl.pallas_call(
        functools.partial(_ffn_kernel, layer=layer),
        out_shape=jax.ShapeDtypeStruct((n_slots, dp), U32),
        grid_spec=grid_spec,
        compiler_params=_cparams(("arbitrary",)),
        name="moe_ffn",
    )(*work, xs, w13_all, w2_all)


def _ffn_work_items(cnt):
    n_slots = 2 * T_ALL
    n_blocks = n_slots // EXPERT_BLOCK
    n_work = n_blocks + N_EXPERTS - 1
    end = jnp.cumsum(cnt)
    start = end - cnt
    first_blk = start // EXPERT_BLOCK
    last_blk = jnp.maximum(end - 1, start) // EXPERT_BLOCK
    n_items = jnp.where(cnt > 0, last_blk - first_blk + 1, 0)
    item_end = jnp.cumsum(n_items)
    item_start = item_end - n_items
    w = jnp.arange(n_work, dtype=jnp.int32)
    used = w < item_end[-1]
    wq = jnp.minimum(w, item_end[-1] - 1)
    e = jnp.sum((item_end[:, None] <= wq[None, :]).astype(jnp.int32), axis=0)
    onehot = e[None, :] == jnp.arange(N_EXPERTS, dtype=jnp.int32)[:, None]
    of_e = lambda table: jnp.sum(jnp.where(onehot, table[:, None], 0), axis=0)
    blk = jnp.where(used, of_e(first_blk) + (w - of_e(item_start)), n_blocks - 1).astype(jnp.int32)
    lo = jnp.maximum(of_e(start), blk * EXPERT_BLOCK) - blk * EXPERT_BLOCK
    hi = jnp.minimum(of_e(end), (blk + 1) * EXPERT_BLOCK) - blk * EXPERT_BLOCK
    lo = jnp.where(used, lo, 0).astype(jnp.int32)
    hi = jnp.where(used, hi, 0).astype(jnp.int32)
    e_before = jnp.concatenate([jnp.full((1,), -1, jnp.int32), e[:-1]])
    change = used & (e != e_before)
    parity = ((jnp.cumsum(change.astype(jnp.int32)) - 1) % 2).astype(jnp.int32)
    far = jnp.int32(n_work)
    next_change = lax.cummin(jnp.where(change, w, far), axis=0, reverse=True)
    next_change = jnp.concatenate([next_change[1:], jnp.full((1,), far, jnp.int32)])
    e_next = jnp.sum(jnp.where(next_change[None, :] == w[:, None], e[:, None], 0), axis=0)
    e_next = jnp.where(next_change < far, e_next, -1).astype(jnp.int32)
    return start, (blk, e, lo, hi, e_next, parity)


def _combine_kernel(dest_ref, h_ref, rec_ref, ys_ref, *rest, split):
    out_refs, (g1, g2, sem) = rest[:-3], rest[-3:]
    rows = h_ref.shape[0]
    i = pl.program_id(0)

    def issue(r, c):
        _row_copy(ys_ref, dest_ref[0, 0, r], g1, r, sem).start(priority=0)
        _row_copy(ys_ref, dest_ref[0, 0, rows + r], g2, r, sem).start(priority=1)
        return c

    lax.fori_loop(0, rows, issue, 0, unroll=8)
    for buf in (g1, g2):
        pltpu.make_async_copy(ys_ref.at[pl.ds(0, rows)], buf, sem).wait()
    rec = rec_ref[...]
    res = h_ref[...] + rec[:, 4:5] * g1[...] + rec[:, 5:6] * g2[...]
    if not split:
        out_refs[0][...] = res
    else:
        @pl.when(i < N_PROMPT_TILES)
        def _():
            out_refs[0][...] = res

        @pl.when((i >= N_PROMPT_TILES) & (i < N_PROMPT_TILES + N_SAMPLE_TILES))
        def _():
            out_refs[1][...] = res


def moe_combine(h, rec, ys, dest3, split=False):
    t, d = h.shape
    row = lambda i: (i, 0)
    if split:
        parts = _parts_specs(d)[:2]
        out_shape = (jax.ShapeDtypeStruct((T_PROMPT, d), F32), jax.ShapeDtypeStruct((T_SAMPLE, d), F32))
        out_specs = tuple(parts)
    else:
        out_shape = jax.ShapeDtypeStruct((t, d), F32)
        out_specs = pl.BlockSpec((ROW_TILE, d), row)
    return pl.pallas_call(
        functools.partial(_combine_kernel, split=split),
        out_shape=out_shape,
        grid=(t // ROW_TILE,),
        in_specs=[pl.BlockSpec((1, 1, 2 * ROW_TILE), lambda i: (i, 0, 0), memory_space=pltpu.SMEM),
                  pl.BlockSpec((ROW_TILE, d), row), pl.BlockSpec((ROW_TILE, LANES), row),
                  pl.BlockSpec(memory_space=pl.ANY)],
        out_specs=out_specs,
        scratch_shapes=[pltpu.VMEM((ROW_TILE, d), F32), pltpu.VMEM((ROW_TILE, d), F32),
                        pltpu.SemaphoreType.DMA],
        compiler_params=_cparams(("arbitrary",)),
        name="moe_combine",
    )(dest3, h, rec, ys)


def hier_moe_layer(a, w_out_bf16, x, layer, gain, w_group, b_group, w_expert, b_expert, w13_all, w2_all, utri,
                   finish):
    t = T_ALL
    pad = LANES - N_EXPERTS - N_GROUPS
    w_router = jnp.concatenate([w_expert, w_group, jnp.zeros((D_MODEL, pad), F32)], axis=1)
    b_router = jnp.concatenate([b_expert, b_group, jnp.zeros((pad,), F32)])[:ROUTE_ROWS].reshape(ROUTE_ROWS, 1)
    h, xn, rec, rect, counts = moe_route(a, w_out_bf16, x, gain, w_router, b_router, utri)

    cnt = counts[:N_EXPERTS, 0].astype(jnp.int32)
    start, work = _ffn_work_items(cnt)
    experts = jnp.arange(N_EXPERTS, dtype=jnp.int32)[:, None]

    def slot_of(e_row, rank_row):
        first = jnp.sum(jnp.where(e_row.astype(jnp.int32)[None, :] == experts, start[:, None], 0), axis=0)
        return (first + rank_row.astype(jnp.int32)).reshape(1, t)

    dest_a = slot_of(rect[0], rect[2])
    dest_b = slot_of(rect[1], rect[3])

    xs = moe_dispatch_sc(xn, dest_a, dest_b, 2 * t)
    ys = moe_ffn(xs, work, w13_all, w2_all, layer)
    ga, gb = moe_gather_sc(ys, dest_a, dest_b)
    return finish(h, rec, ga, gb)


def _rope_tables(pos):
    half = ROPE_DIM // 2
    lane = np.arange(LANES) % HEAD_DIM
    rotary = lane < ROPE_DIM
    inv = jnp.where(rotary, jnp.exp(-math.log(ROPE_THETA) * jnp.asarray(lane % half, F32) * (2.0 / ROPE_DIM)), 0.0)
    ang = pos.astype(F32)[:, None] * inv[None, :]
    cos, sin = jnp.cos(ang), jnp.sin(ang)
    first = jnp.asarray(lane < half)
    second = jnp.asarray(rotary & (lane >= half))
    return cos, jnp.where(first, -sin, 0.0), jnp.where(second, sin, 0.0)


def kernel(x_prompt, x_sample, state_hgrn, cache_k_win, cache_v_win, meta_tokens, a_norm, a_w_in, a_lower_logits, a_out_norm, a_w_out, kv_norm, kv_w, k_norm, b_norm, b_wq, b_q_norm, b_sinks, b_w_out, moe_norm, moe_w_group, moe_b_group, moe_w_expert, moe_b_expert, moe_w13, moe_w2):
    tail_rows = T_ALL - OFF_META
    x_parts = (x_prompt.reshape(T_PROMPT, D_MODEL), x_sample.reshape(T_SAMPLE, D_MODEL),
               jnp.concatenate([meta_tokens.astype(F32), jnp.zeros((tail_rows - N_META, D_MODEL), F32)], axis=0))
    pos = jnp.concatenate([N_META + jnp.arange(SEQ, dtype=jnp.int32),
                           jnp.tile(PAST_LEN + jnp.arange(DEC_SEQ, dtype=jnp.int32), ROW_TILE // DEC_SEQ),
                           jnp.arange(N_META, dtype=jnp.int32),
                           jnp.zeros((ROW_TILE - N_META,), jnp.int32)])
    cos_t, sina_t, sinb_t = _rope_tables(pos)
    r256 = np.arange(256)
    hmean = jnp.asarray((r256[:, None] // HEAD_DIM == r256[None, :] // HEAD_DIM).astype(np.float32) / HEAD_DIM, BF16)
    rt = np.arange(ROW_TILE)
    utri = jnp.asarray((rt[:, None] < rt[None, :]).astype(np.float32), BF16)
    lower = jnp.cumsum(jax.nn.softmax(a_lower_logits.astype(F32), axis=0), axis=0)

    moe = functools.partial(hier_moe_layer, w13_all=moe_w13, w2_all=moe_w2, utri=utri)

    z = in_project(x_parts, a_norm[0], a_w_in[0].astype(BF16))
    zero_state = jnp.zeros((1, A_HEADS, A_DK, A_DV), F32)
    o_meta, s_meta = hgrn2_scan(z, zero_state, lower[0], a_out_norm[0],
                                row_off=OFF_META, n_seq=1, seq_len=N_META)
    o_prompt, s_prompt = hgrn2_scan(z, s_meta, lower[0], a_out_norm[0], row_off=0, n_seq=BATCH, seq_len=SEQ)
    o_sample, s_sample = hgrn2_scan(z, state_hgrn[0].astype(F32), lower[0], a_out_norm[0],
                                    row_off=OFF_SAMPLE, n_seq=DEC_BATCH, seq_len=DEC_SEQ, group=SCAN_SAMPLE_GROUP)
    o_tail = jnp.concatenate([o_meta, jnp.zeros((tail_rows - N_META, D_MODEL), BF16)], axis=0)
    cache_k, cache_v, o_prompt = lax.optimization_barrier((cache_k_win.astype(F32), cache_v_win.astype(F32), o_prompt))
    finish0 = functools.partial(
        moe_combine_kv_q, kv_gain=kv_norm, kv_w_bf16=kv_w.astype(BF16), q_gain=b_norm[0], wq_bf16=b_wq[0].astype(BF16),
        hmean=hmean, k_hgain=jnp.tile(k_norm, KV_HEADS).reshape(1, KV_DIM),
        q_hgain=jnp.tile(b_q_norm[0], Q_HEADS).reshape(1, D_MODEL), cos_t=cos_t, sina_t=sina_t, sinb_t=sinb_t)
    h, k_all, v_all, q_all = moe((o_prompt, o_sample, o_tail), a_w_out[0].astype(BF16), x_parts, 0, moe_norm[0],
                                 moe_w_group[0], moe_b_group[0], moe_w_expert[0], moe_b_expert[0], finish=finish0)

    meta_blk = lambda a: jnp.concatenate([jnp.zeros((ATT_BLOCK - N_META, KV_DIM), F32),
                                          a[OFF_META:OFF_META + N_META]], axis=0)
    sinks = b_sinks[0].astype(F32)
    att_all = attention_prompt(q_all, k_all, v_all, meta_blk(k_all), meta_blk(v_all), sinks)
    att_all, k_win_s, v_win_s = attention_sample(q_all, cache_k, cache_v, k_all, v_all, sinks, att_all)
    y_p, y_s = moe(att_all, b_w_out[0].astype(BF16), h, 1, moe_norm[1], moe_w_group[1], moe_b_group[1],
                   moe_w_expert[1], moe_b_expert[1], finish=functools.partial(moe_combine_dense, split=True))

    y_prompt = y_p.reshape(BATCH, SEQ, D_MODEL)
    y_sample = y_s.reshape(DEC_BATCH, DEC_SEQ, D_MODEL)
    last = lambda a: jnp.stack([a[(b + 1) * SEQ - WINDOW:(b + 1) * SEQ] for b in range(BATCH)]).reshape(
        BATCH, WINDOW, KV_HEADS, HEAD_DIM)
    kp = last(k_all)
    vp = last(v_all)
    return (y_prompt, y_sample, s_prompt[None], s_sample[None], kp, vp, k_win_s, v_win_s)
```

```python
import functools
import math

import numpy as np
import jax
import jax.numpy as jnp
from jax import lax
from jax.experimental import pallas as pl
from jax.experimental.pallas import tpu as pltpu
from jax.experimental.pallas import tpu_sc as plsc

F32 = jnp.float32
BF16 = jnp.bfloat16
U32 = jnp.uint32

D_MODEL = 1024
BATCH = 4
SEQ = 4096
DEC_BATCH = 128
DEC_SEQ = 8
PAST_LEN = 8192
N_META = 16
A_HEADS = 8
A_DK = 128
A_DV = 128
Q_HEADS = 16
KV_HEADS = 4
HEAD_DIM = 64
KV_DIM = KV_HEADS * HEAD_DIM
WINDOW = 128
ROPE_DIM = 16
ROPE_THETA = 500000.0
N_GROUPS = 4
EXPERTS_PER_GROUP = 8
N_EXPERTS = 32
D_EXPERT = 512
RMS_EPS = 1e-6

LANES = 128
SUBLANES = 8
VMEM_LIMIT = 56 * 1024 * 1024

ROW_TILE = 512
T_PROMPT = BATCH * SEQ
T_SAMPLE = DEC_BATCH * DEC_SEQ
OFF_SAMPLE = T_PROMPT
OFF_META = T_PROMPT + T_SAMPLE
T_REAL = OFF_META + N_META
T_ALL = -(-T_REAL // ROW_TILE) * ROW_TILE
N_TILES = T_ALL // ROW_TILE

SCAN_CHUNK = 128
SCAN_SAMPLE_GROUP = 8
ATT_BLOCK = 128
EXPERT_BLOCK = 512
FFN_CHUNKS = 2


def _cparams(sem):
    return pltpu.CompilerParams(dimension_semantics=sem, vmem_limit_bytes=VMEM_LIMIT)


def _nt_dot(a, b):
    return lax.dot_general(a, b, (((1,), (1,)), ((), ())), preferred_element_type=F32)


def _rms(x, gain):
    ms = jnp.mean(x * x, axis=-1, keepdims=True)
    return x * lax.rsqrt(ms + RMS_EPS) * gain


def _silu(x):
    return x * jax.nn.sigmoid(x)


def _pack_halves(x):
    w = x.shape[1] // 2
    hi = lax.bitcast_convert_type(x[:, :w].astype(BF16).astype(F32), U32)
    lo = lax.bitcast_convert_type(x[:, w:].astype(BF16).astype(F32), U32)
    return hi | (lo >> 16)


def _unpack_halves(p):
    hi = lax.bitcast_convert_type(p & jnp.uint32(0xFFFF0000), F32)
    lo = lax.bitcast_convert_type(p << 16, F32)
    return jnp.concatenate([hi, lo], axis=1)


N_PROMPT_TILES = T_PROMPT // ROW_TILE
N_SAMPLE_TILES = T_SAMPLE // ROW_TILE


def _parts_specs(width):
    return [pl.BlockSpec((ROW_TILE, width), lambda i: (jnp.minimum(i, N_PROMPT_TILES - 1), 0)),
            pl.BlockSpec((ROW_TILE, width), lambda i: (jnp.clip(i - N_PROMPT_TILES, 0, N_SAMPLE_TILES - 1), 0)),
            pl.BlockSpec((ROW_TILE, width), lambda i: (0, 0))]


def _pick_part(i, p_ref, s_ref, t_ref, dtype):
    return jnp.where(i < N_PROMPT_TILES, p_ref[...].astype(dtype),
                     jnp.where(i < N_PROMPT_TILES + N_SAMPLE_TILES, s_ref[...].astype(dtype),
                               t_ref[...].astype(dtype)))


def _in_proj_kernel(xp_ref, xs_ref, xt_ref, g_ref, w_ref, o_ref):
    x = _pick_part(pl.program_id(0), xp_ref, xs_ref, xt_ref, F32)
    xn = _rms(x, g_ref[...])
    o_ref[...] = jnp.dot(xn.astype(BF16), w_ref[...], preferred_element_type=F32)


def in_project(x_parts, gain, w_bf16):
    d, n = w_bf16.shape
    return pl.pallas_call(
        _in_proj_kernel,
        out_shape=jax.ShapeDtypeStruct((T_ALL, n), F32),
        grid=(N_TILES,),
        in_specs=_parts_specs(d) + [pl.BlockSpec((1, d), lambda i: (0, 0)),
                                    pl.BlockSpec((d, n), lambda i: (0, 0))],
        out_specs=pl.BlockSpec((ROW_TILE, n), lambda i: (i, 0)),
        compiler_params=_cparams(("parallel",)),
        name="in_project",
    )(*x_parts, gain.reshape(1, d), w_bf16)


def _head_norm_rope(y, hmean_ref, hgain, cos_t, sina_t, sinb_t):
    rows, width = y.shape
    sq = (y * y).astype(BF16)
    parts = []
    for s in range(width // 256):
        parts.append(jnp.dot(sq[:, s * 256:(s + 1) * 256], hmean_ref[...], preferred_element_type=F32))
    ms = parts[0] if len(parts) == 1 else jnp.concatenate(parts, axis=1)
    yn = y * lax.rsqrt(ms + RMS_EPS) * hgain
    reps = width // LANES
    cos_w = jnp.concatenate([cos_t] * reps, axis=1)
    sina_w = jnp.concatenate([sina_t] * reps, axis=1)
    sinb_w = jnp.concatenate([sinb_t] * reps, axis=1)
    half = ROPE_DIM // 2
    nxt = pltpu.roll(yn, width - half, 1)
    prv = pltpu.roll(yn, half, 1)
    return yn * cos_w + nxt * sina_w + prv * sinb_w


def _rope_tile(i):
    tiles_per_seq = SEQ // ROW_TILE
    n_prompt_tiles = T_PROMPT // ROW_TILE
    n_sample_tiles = T_SAMPLE // ROW_TILE
    return (jnp.where(i < n_prompt_tiles, i % tiles_per_seq,
                      jnp.where(i < n_prompt_tiles + n_sample_tiles, tiles_per_seq, tiles_per_seq + 1)), 0)


def _scan_levels(c):
    levels = []
    m = c
    while m >= 2:
        levels.append(m)
        m //= 2
    return levels


LOG2E = 1.4426950408889634


def _scan_kernel(z_ref, s0_ref, lb_ref, og_ref, tri_ref, lmask_ref, sgn_ref, o_ref, sfin_ref, s_scr, b_scr,
                 *, rows, seq_len):
    c_idx = pl.program_id(1)
    levels = _scan_levels(seq_len)
    n_sub = rows // seq_len
    hk = A_HEADS * A_DK

    @pl.when(c_idx == 0)
    def _():
        s_scr[...] = s0_ref[...]

    sub = lax.broadcasted_iota(jnp.int32, (SUBLANES, LANES), 0)
    row = lax.broadcasted_iota(jnp.int32, (LANES, LANES), 0)
    og = og_ref[...]

    def pad_f32(x):
        if x.shape[0] == LANES:
            return x
        return jnp.concatenate([x, jnp.zeros((LANES - x.shape[0], x.shape[1]), x.dtype)], axis=0)

    def pad_rows(x):
        return pad_f32(x).astype(BF16)

    def cols(part, h):
        return slice(part * hk + h * LANES, part * hk + (h + 1) * LANES)

    def gates(h):
        lb = lb_ref[:, cols(0, h)]
        forget = lb + (1.0 - lb) * jax.nn.sigmoid(z_ref[:, cols(1, h)])
        logf = jnp.log(forget)
        hi = logf.astype(BF16).astype(F32)
        r1 = logf - hi
        mid = r1.astype(BF16).astype(F32)
        lo = r1 - mid
        cs = jnp.dot(tri_ref[...], pad_rows(jnp.concatenate([hi, mid, lo], axis=1)),
                     preferred_element_type=F32)
        b = (cs[:rows, :LANES] + cs[:rows, LANES:2 * LANES]) + cs[:rows, 2 * LANES:]
        b_scr[h] = b
        return _silu(z_ref[:, cols(0, h)]), 1.0 - forget, b

    def bref_for(h, m):
        b_rows = b_scr.at[h]
        half = m // 2
        pieces = []
        for g in range(rows // SUBLANES):
            base = g * SUBLANES
            if m >= SUBLANES:
                r = (base // m) * m + half - 1
                piece = jnp.broadcast_to(b_rows[r:r + 1, :], (SUBLANES, LANES))
            else:
                piece = jnp.broadcast_to(b_rows[base + half - 1:base + half, :], (SUBLANES, LANES))
                for blk in range(1, SUBLANES // m):
                    r = base + blk * m + half - 1
                    piece = jnp.where(sub >= blk * m,
                                      jnp.broadcast_to(b_rows[r:r + 1, :], (SUBLANES, LANES)), piece)
            pieces.append(piece)
        return pieces[0] if len(pieces) == 1 else jnp.concatenate(pieces, axis=0)

    heads = range(A_HEADS)
    qkb = [gates(h) for h in heads]
    att = [_nt_dot(pad_rows(qf), pad_rows(kf)) * lmask_ref[len(levels)] for qf, kf, _ in qkb]
    for li, m in enumerate(levels):
        for h in heads:
            qf, kf, b = qkb[h]
            sgn = sgn_ref[li]
            e = jnp.exp2((b - bref_for(h, m)) * sgn)
            w = pad_rows(jnp.where(sgn > 0, qf, kf) * e)
            att[h] = att[h] + _nt_dot(w, w) * lmask_ref[li]

    def finish(h):
        qf, kf, b = qkb[h]
        b_rows = b_scr.at[h]
        v_b = pad_rows(z_ref[:, cols(2, h)])
        o_intra = jnp.dot(att[h].astype(BF16), v_b, preferred_element_type=F32)
        eb = jnp.exp(b)
        qs = qf * eb
        b_end = [jnp.broadcast_to(b_rows[(i + 1) * seq_len - 1:(i + 1) * seq_len, :], (seq_len, LANES))
                 for i in range(n_sub)]
        b_end = b_end[0] if n_sub == 1 else jnp.concatenate(b_end, axis=0)
        kd_t = pad_f32(kf * jnp.exp(b_end - b)).T.astype(BF16)
        eb_t = pad_f32(eb).T
        qs_b = pad_rows(qs)
        o = o_intra
        for i in range(n_sub):
            s_old = s_scr[i, h]
            first, last = i * seq_len, (i + 1) * seq_len - 1
            if n_sub == 1:
                qs_i, v_i = qs_b, v_b
            else:
                mine = (row >= first) & (row <= last)
                qs_i = jnp.where(mine, qs_b, jnp.zeros_like(qs_b))
                v_i = jnp.where(mine, v_b, jnp.zeros_like(v_b))
            o = o + jnp.dot(qs_i, s_old.astype(BF16), preferred_element_type=F32)
            decay = jnp.broadcast_to(eb_t[:, last:last + 1], (LANES, LANES))
            s_scr[i, h] = decay * s_old + jnp.dot(kd_t, v_i, preferred_element_type=F32)
        o = o[:rows]

        on = _rms(o, og) * _silu(z_ref[:, cols(3, h)])
        o_ref[:, cols(0, h)] = on.astype(o_ref.dtype)

    for h in heads:
        finish(h)

    @pl.when(c_idx == pl.num_programs(1) - 1)
    def _():
        sfin_ref[...] = s_scr[...]


def _scan_consts(rows, seq_len):
    levels = _scan_levels(seq_len)
    r = np.arange(LANES)
    t, s = r[:, None], r[None, :]
    live = (t < rows) & (s < rows)
    tri = ((s <= t) & (t // seq_len == s // seq_len) & live).astype(np.float32)
    masks, sgns = [], []
    for m in levels:
        masks.append(((t // m == s // m) & (t % m >= m // 2) & (s % m < m // 2) & live).astype(np.float32))
        sgns.append(np.broadcast_to(np.where(r[:rows, None] % m >= m // 2, LOG2E, -LOG2E), (rows, LANES)))
    masks.append(((t == s) & live).astype(np.float32))
    return jnp.asarray(tri, BF16), jnp.asarray(np.stack(masks), F32), jnp.asarray(np.stack(sgns), F32)


def hgrn2_scan(z, s0, lb, o_gain, *, row_off, n_seq, seq_len, group=1):
    hv = A_HEADS * A_DV
    if seq_len > SCAN_CHUNK:
        assert group == 1
        sub_len, rows, n_chunks, n_steps = SCAN_CHUNK, SCAN_CHUNK, seq_len // SCAN_CHUNK, n_seq
    else:
        sub_len, rows, n_chunks, n_steps = seq_len, group * seq_len, 1, n_seq // group
    blk_off = row_off // rows
    tri, lmask, sgn = _scan_consts(rows, sub_len)
    shared_s0 = s0.shape[0] == 1
    fix2 = lambda s, c: (0, 0)
    fix3 = lambda s, c: (0, 0, 0)
    o, sfin = pl.pallas_call(
        functools.partial(_scan_kernel, rows=rows, seq_len=sub_len),
        out_shape=(jax.ShapeDtypeStruct((n_seq * seq_len, hv), BF16 if rows % 16 == 0 else F32),
                   jax.ShapeDtypeStruct((n_seq, A_HEADS, A_DK, A_DV), F32)),
        grid=(n_steps, n_chunks),
        in_specs=[pl.BlockSpec((rows, 4 * hv), lambda s, c: (blk_off + s * n_chunks + c, 0)),
                  pl.BlockSpec((group, A_HEADS, A_DK, A_DV), (lambda s, c: (0, 0, 0, 0)) if shared_s0
                               else (lambda s, c: (s, 0, 0, 0))),
                  pl.BlockSpec((1, hv), fix2), pl.BlockSpec((1, A_DV), fix2),
                  pl.BlockSpec((LANES, LANES), fix2), pl.BlockSpec(lmask.shape, fix3),
                  pl.BlockSpec(sgn.shape, fix3)],
        out_specs=(pl.BlockSpec((rows, hv), lambda s, c: (s * n_chunks + c, 0)),
                   pl.BlockSpec((group, A_HEADS, A_DK, A_DV), lambda s, c: (s, 0, 0, 0))),
        scratch_shapes=[pltpu.VMEM((group, A_HEADS, A_DK, A_DV), F32), pltpu.VMEM((A_HEADS, rows, LANES), F32)],
        compiler_params=_cparams(("parallel", "arbitrary")),
        name=f"hgrn2_scan_r{rows}",
    )(z, s0, lb.reshape(1, hv), o_gain.reshape(1, A_DV), tri, lmask, sgn)
    return o, sfin


KEYS = 2 * ATT_BLOCK
ATT_STEP = 1


def _pair_operand(x, kh):
    slab = x[:, (kh // 2) * LANES:(kh // 2 + 1) * LANES]
    lane = lax.broadcasted_iota(jnp.int32, slab.shape, 1)
    if kh % 2 == 0:
        lo = jnp.where(lane < HEAD_DIM, slab, 0.0)
        hi = pltpu.roll(lo, HEAD_DIM, 1)
    else:
        hi = jnp.where(lane >= HEAD_DIM, slab, 0.0)
        lo = pltpu.roll(hi, HEAD_DIM, 1)
    return jnp.concatenate([lo, hi], axis=0).astype(BF16)


def _window_bias(rows, jmin):
    t_i = lax.broadcasted_iota(jnp.int32, (rows, 2 * KEYS), 0)
    c_i = lax.broadcasted_iota(jnp.int32, (rows, 2 * KEYS), 1)
    j_i = c_i & (ATT_BLOCK - 1)
    own = (c_i & ATT_BLOCK) != 0
    ok = (own & (j_i <= t_i)) | (jnp.logical_not(own) & (j_i >= t_i) & (j_i >= jmin))
    return jnp.where(ok, 0.0, -jnp.inf).astype(F32)


def _pair_softmax(s, sink_a, sink_b):
    probs, rinv = [], []
    for hh, sink in enumerate((sink_a, sink_b)):
        sh = s[:, hh * KEYS:(hh + 1) * KEYS]
        m = jnp.maximum(jnp.max(sh, axis=-1, keepdims=True), sink)
        p = jnp.exp(sh - m)
        den = jnp.sum(p, axis=-1, keepdims=True) + jnp.exp(sink - m)
        probs.append(p.astype(BF16))
        rinv.append(1.0 / den)
    lane = lax.broadcasted_iota(jnp.int32, (s.shape[0], LANES), 1)
    return jnp.concatenate(probs, axis=1), jnp.where(lane < HEAD_DIM, rinv[0], rinv[1])


def _attn_prompt_kernel(sink_ref, q_ref, kp_ref, ko_ref, vp_ref, vo_ref, km_ref, vm_ref, o_ref,
                        k2_scr, v2_scr, s_scr, p_scr, r_scr, bias_scr):
    n = pl.program_id(0)
    steps_per_seq = SEQ // (ATT_STEP * ATT_BLOCK)
    n_pairs = Q_HEADS // 2
    blk = ATT_BLOCK

    @pl.when(n == 0)
    def _():
        bias_scr[0] = _window_bias(ATT_BLOCK, 0)
        bias_scr[1] = _window_bias(ATT_BLOCK, ATT_BLOCK - N_META)

    @pl.when(n >= BATCH * steps_per_seq)
    def _():
        o_ref[...] = jnp.zeros_like(o_ref)

    @pl.when(n < BATCH * steps_per_seq)
    def _():
        first = (n % steps_per_seq) == 0
        own_k, own_v = ko_ref[...], vo_ref[...]
        keys = [jnp.where(first, km_ref[...], kp_ref[...])] + [own_k[u * blk:(u + 1) * blk] for u in range(ATT_STEP)]
        vals = [jnp.where(first, vm_ref[...], vp_ref[...])] + [own_v[u * blk:(u + 1) * blk] for u in range(ATT_STEP)]
        biases = [bias_scr[first.astype(jnp.int32)]] + [bias_scr[0]] * (ATT_STEP - 1)
        for u in range(ATT_STEP):
            k = jnp.concatenate(keys[u:u + 2], axis=0)
            v = jnp.concatenate(vals[u:u + 2], axis=0)
            for kh in range(KV_HEADS):
                k2_scr[u, kh] = _pair_operand(k, kh)
                v2_scr[u, kh] = _pair_operand(v, kh)
        units = [(u, pair) for u in range(ATT_STEP) for pair in range(n_pairs)]
        rows = lambda u: slice(u * blk, (u + 1) * blk)
        lanes = lambda pair: slice(pair * LANES, (pair + 1) * LANES)
        for j, (u, pair) in enumerate(units):
            s_scr[j] = _nt_dot(q_ref[rows(u), lanes(pair)], k2_scr[u, pair // 2]) + biases[u]
        for j, (u, pair) in enumerate(units):
            p, rinv = _pair_softmax(s_scr[j], sink_ref[2 * pair], sink_ref[2 * pair + 1])
            p_scr[j] = p
            r_scr[j] = rinv
        for j, (u, pair) in enumerate(units):
            o = jnp.dot(p_scr[j], v2_scr[u, pair // 2], preferred_element_type=F32) * r_scr[j]
            o_ref[rows(u), lanes(pair)] = o.astype(o_ref.dtype)


def attention_prompt(q_all, k_all, v_all, k_meta_blk, v_meta_blk, sinks):
    step_rows = ATT_STEP * ATT_BLOCK
    n_prompt_steps = T_PROMPT // step_rows
    n_steps = T_ALL // step_rows
    n_units = ATT_STEP * (Q_HEADS // 2)
    own = lambda n, sk: (jnp.minimum(n, n_prompt_steps - 1), 0)
    prev = lambda n, sk: (jnp.maximum(ATT_STEP * jnp.minimum(n, n_prompt_steps - 1) - 1, 0), 0)
    fix = lambda n, sk: (0, 0)
    grid_spec = pltpu.PrefetchScalarGridSpec(
        num_scalar_prefetch=1,
        grid=(n_steps,),
        in_specs=[pl.BlockSpec((step_rows, D_MODEL), own),
                  pl.BlockSpec((ATT_BLOCK, KV_DIM), prev), pl.BlockSpec((step_rows, KV_DIM), own),
                  pl.BlockSpec((ATT_BLOCK, KV_DIM), prev), pl.BlockSpec((step_rows, KV_DIM), own),
                  pl.BlockSpec((ATT_BLOCK, KV_DIM), fix), pl.BlockSpec((ATT_BLOCK, KV_DIM), fix)],
        out_specs=pl.BlockSpec((step_rows, D_MODEL), lambda n, sk: (n, 0)),
        scratch_shapes=[pltpu.VMEM((ATT_STEP, KV_HEADS, 2 * KEYS, LANES), BF16),
                        pltpu.VMEM((ATT_STEP, KV_HEADS, 2 * KEYS, LANES), BF16),
                        pltpu.VMEM((n_units, ATT_BLOCK, 2 * KEYS), F32),
                        pltpu.VMEM((n_units, ATT_BLOCK, 2 * KEYS), BF16),
                        pltpu.VMEM((n_units, ATT_BLOCK, LANES), F32),
                        pltpu.VMEM((2, ATT_BLOCK, 2 * KEYS), F32)],
    )
    return pl.pallas_call(
        _attn_prompt_kernel,
        out_shape=jax.ShapeDtypeStruct((T_ALL, D_MODEL), BF16),
        grid_spec=grid_spec,
        compiler_params=_cparams(("arbitrary",)),
        name="attention_prompt",
    )(sinks, q_all, k_all, k_all, v_all, v_all, k_meta_blk, v_meta_blk)


SAMPLE_GROUP = ATT_BLOCK // DEC_SEQ


def _attn_sample_kernel(sink_ref, q_ref, ck_ref, cv_ref, kn_ref, vn_ref, buf_ref, o_ref, kw_ref, vw_ref,
                        qf_scr, of_scr, k2_scr, v2_scr):
    del buf_ref
    qrows = 2 * DEC_SEQ
    qf_scr[...] = q_ref[...].astype(F32)
    bias = _window_bias(qrows, 0)
    zq = jnp.zeros((qrows - DEC_SEQ, D_MODEL), F32)
    zk = jnp.zeros((ATT_BLOCK - DEC_SEQ, KV_DIM), F32)

    n_pairs = Q_HEADS // 2
    lanes_of = lambda pair: slice(pair * LANES, (pair + 1) * LANES)

    def seq_pair_body(it, carry):
        seqs = (2 * it, 2 * it + 1)
        r_new = [pl.multiple_of(i * DEC_SEQ, DEC_SEQ) for i in seqs]
        qs = []
        for u, i in enumerate(seqs):
            qs.append(jnp.concatenate([qf_scr[pl.ds(r_new[u], DEC_SEQ), :], zq], axis=0).astype(BF16))
            old = lambda c_ref: jnp.concatenate([c_ref[i, :, kh, :] for kh in range(KV_HEADS)], axis=1)
            k = jnp.concatenate([old(ck_ref), kn_ref[pl.ds(r_new[u], DEC_SEQ), :], zk], axis=0)
            v = jnp.concatenate([old(cv_ref), vn_ref[pl.ds(r_new[u], DEC_SEQ), :], zk], axis=0)
            for c_ref, n_ref, w_ref in ((ck_ref, kn_ref, kw_ref), (cv_ref, vn_ref, vw_ref)):
                w_ref[i, 0:WINDOW - DEC_SEQ] = c_ref[i, DEC_SEQ:WINDOW]
                for kh in range(KV_HEADS):
                    w_ref[i, WINDOW - DEC_SEQ:WINDOW, kh, :] = n_ref[pl.ds(r_new[u], DEC_SEQ),
                                                                     kh * HEAD_DIM:(kh + 1) * HEAD_DIM]
            for kh in range(KV_HEADS):
                k2_scr[u, kh] = _pair_operand(k, kh)
                v2_scr[u, kh] = _pair_operand(v, kh)
        scores = [[_nt_dot(qs[u][:, lanes_of(pair)], k2_scr[u, pair // 2]) + bias for pair in range(n_pairs)]
                  for u in range(2)]
        soft = [[_pair_softmax(s, sink_ref[2 * pair], sink_ref[2 * pair + 1]) for pair, s in enumerate(scores[u])]
                for u in range(2)]
        for u in range(2):
            for pair, (p, rinv) in enumerate(soft[u]):
                o = jnp.dot(p, v2_scr[u, pair // 2], preferred_element_type=F32) * rinv
                of_scr[pl.ds(r_new[u], DEC_SEQ), lanes_of(pair)] = o[:DEC_SEQ]
        return carry

    lax.fori_loop(0, SAMPLE_GROUP // 2, seq_pair_body, 0)
    o_ref[...] = of_scr[...].astype(o_ref.dtype)


def attention_sample(q_all, cache_k, cache_v, k_all, v_all, sinks, out_buf):
    first_blk = OFF_SAMPLE // ATT_BLOCK
    new = lambda g, sk: (first_blk + g, 0)
    old = pl.BlockSpec((SAMPLE_GROUP, WINDOW, KV_HEADS, HEAD_DIM), lambda g, sk: (g, 0, 0, 0))
    grid_spec = pltpu.PrefetchScalarGridSpec(
        num_scalar_prefetch=1,
        grid=(DEC_BATCH // SAMPLE_GROUP,),
        in_specs=[pl.BlockSpec((ATT_BLOCK, D_MODEL), new),
                  old, old,
                  pl.BlockSpec((ATT_BLOCK, KV_DIM), new), pl.BlockSpec((ATT_BLOCK, KV_DIM), new),
                  pl.BlockSpec(memory_space=pl.ANY)],
        out_specs=(pl.BlockSpec((ATT_BLOCK, D_MODEL), new), old, old),
        scratch_shapes=[pltpu.VMEM((ATT_BLOCK, D_MODEL), F32), pltpu.VMEM((ATT_BLOCK, D_MODEL), F32),
                        pltpu.VMEM((2, KV_HEADS, 2 * KEYS, LANES), BF16),
                        pltpu.VMEM((2, KV_HEADS, 2 * KEYS, LANES), BF16)],
    )
    window = jax.ShapeDtypeStruct(cache_k.shape, cache_k.dtype)
    return pl.pallas_call(
        _attn_sample_kernel,
        out_shape=(jax.ShapeDtypeStruct(out_buf.shape, out_buf.dtype), window, window),
        grid_spec=grid_spec,
        input_output_aliases={6: 0},
        compiler_params=_cparams(("parallel",)),
        name="attention_sample",
    )(sinks, q_all, cache_k, cache_v, k_all, v_all, out_buf)


ROUTE_COLS = 8
ROUTE_ROWS = 48


def _route_kernel(*refs, parts):
    i = pl.program_id(0)
    if parts:
        (ap_ref, as_ref, at_ref, w_ref, xp_ref, xs_ref, xt_ref), refs = refs[:7], refs[7:]
        a = _pick_part(i, ap_ref, as_ref, at_ref, BF16)
        x = _pick_part(i, xp_ref, xs_ref, xt_ref, F32)
    else:
        (a_ref, w_ref, x_ref), refs = refs[:3], refs[3:]
        a, x = a_ref[...], x_ref[...]
    g_ref, wh_ref, wl_ref, br_ref, utri_ref, h_ref, xn_ref, rec_ref, rect_ref, cnt_ref, cnt_scr = refs

    @pl.when(i == 0)
    def _():
        cnt_scr[...] = jnp.zeros_like(cnt_scr)

    h = x + jnp.dot(a, w_ref[...], preferred_element_type=F32)
    h_ref[...] = h
    xn = _rms(h, g_ref[...])
    xn_ref[...] = _pack_halves(xn)
    xh = xn.astype(BF16)
    xl = (xn - xh.astype(F32)).astype(BF16)
    logits = (_nt_dot(wh_ref[...], xh) + (_nt_dot(wl_ref[...], xh) + _nt_dot(wh_ref[...], xl)))[:ROUTE_ROWS]
    logits = logits + br_ref[...]
    tokens = logits.shape[1]
    rid = lax.broadcasted_iota(jnp.int32, (ROUTE_ROWS, tokens), 0).astype(F32)
    neg = jnp.float32(-jnp.inf)
    big = jnp.float32(ROUTE_ROWS)

    is_g = (rid >= N_EXPERTS) & (rid < N_EXPERTS + N_GROUPS)
    gl = jnp.where(is_g, logits, neg)
    gmax = jnp.max(gl, axis=0, keepdims=True)
    gsel = jnp.min(jnp.where(gl == gmax, rid, big), axis=0, keepdims=True) - N_EXPERTS
    gden = jnp.sum(jnp.where(is_g, jnp.exp(gl - gmax), 0.0), axis=0, keepdims=True)
    gw = 1.0 / gden

    in_grp = (rid >= gsel * EXPERTS_PER_GROUP) & (rid < (gsel + 1) * EXPERTS_PER_GROUP)
    el = jnp.where(in_grp, logits, neg)
    t1 = jnp.max(el, axis=0, keepdims=True)
    e1 = jnp.min(jnp.where(el == t1, rid, big), axis=0, keepdims=True)
    el2 = jnp.where(rid == e1, neg, el)
    t2 = jnp.max(el2, axis=0, keepdims=True)
    e2 = jnp.min(jnp.where(el2 == t2, rid, big), axis=0, keepdims=True)
    x2 = jnp.exp(t2 - t1)
    w1 = gw / (1.0 + x2)
    w2 = gw * x2 / (1.0 + x2)

    oh1 = (rid == e1).astype(F32)
    oh2 = (rid == e2).astype(F32)
    oh = oh1 + oh2
    before = jnp.dot(oh.astype(BF16), utri_ref[...], preferred_element_type=F32)
    base = cnt_scr[...] + before
    r1 = jnp.sum(base * oh1, axis=0, keepdims=True)
    r2 = jnp.sum(base * oh2, axis=0, keepdims=True)
    cnt_scr[...] = cnt_scr[...] + jnp.sum(oh, axis=1, keepdims=True)

    zero = jnp.zeros_like(w1)
    rect = jnp.concatenate([e1, e2, r1, r2, w1, w2, zero, zero], axis=0)
    rect_ref[...] = rect
    wide = jnp.concatenate([rect, jnp.zeros((LANES - ROUTE_COLS, tokens), F32)], axis=0)
    rec_ref[...] = jnp.concatenate([wide[:, t0:t0 + LANES].T for t0 in range(0, tokens, LANES)], axis=0)
    cnt_ref[...] = cnt_scr[...]


def moe_route(a, w_out_bf16, x, gain, w_router, b_router, utri):
    parts = isinstance(a, tuple)
    t, d = T_ALL, D_MODEL
    row = lambda i: (i, 0)
    fix = lambda i: (0, 0)
    w_t = w_router.T
    w_hi = w_t.astype(BF16)
    w_lo = (w_t - w_hi.astype(F32)).astype(BF16)
    w_spec = pl.BlockSpec((d, d), fix)
    if parts:
        pre_specs = _parts_specs(d) + [w_spec] + _parts_specs(d)
        pre_args = (*a, w_out_bf16, *x)
    else:
        pre_specs = [pl.BlockSpec((ROW_TILE, d), row), w_spec, pl.BlockSpec((ROW_TILE, d), row)]
        pre_args = (a, w_out_bf16, x)
    return pl.pallas_call(
        functools.partial(_route_kernel, parts=parts),
        out_shape=(jax.ShapeDtypeStruct((t, d), F32),
                   jax.ShapeDtypeStruct((t, d // 2), U32), jax.ShapeDtypeStruct((t, LANES), F32),
                   jax.ShapeDtypeStruct((ROUTE_COLS, t), F32), jax.ShapeDtypeStruct((ROUTE_ROWS, 1), F32)),
        grid=(t // ROW_TILE,),
        in_specs=pre_specs + [pl.BlockSpec((1, d), fix),
                              pl.BlockSpec((LANES, d), fix), pl.BlockSpec((LANES, d), fix),
                              pl.BlockSpec((ROUTE_ROWS, 1), fix), pl.BlockSpec((ROW_TILE, ROW_TILE), fix)],
        out_specs=(pl.BlockSpec((ROW_TILE, d), row),
                   pl.BlockSpec((ROW_TILE, d // 2), row), pl.BlockSpec((ROW_TILE, LANES), row),
                   pl.BlockSpec((ROUTE_COLS, ROW_TILE), lambda i: (0, i)), pl.BlockSpec((ROUTE_ROWS, 1), fix)),
        scratch_shapes=[pltpu.VMEM((ROUTE_ROWS, 1), F32)],
        compiler_params=_cparams(("arbitrary",)),
        name="moe_route",
    )(*pre_args, gain.reshape(1, d), w_hi, w_lo, b_router, utri)


SC_WINDOW = 64
SC_INDEX_WINDOW = 128


def _sc_mesh():
    return plsc.VectorSubcoreMesh(core_axis_name="core", subcore_axis_name="subcore")


def moe_dispatch_sc(xn, dest_a, dest_b, n_slots):
    t, d = xn.shape

    n_parts = SC_INDEX_WINDOW // SC_WINDOW

    @pl.kernel(out_type=jax.ShapeDtypeStruct((n_slots, d), xn.dtype), mesh=_sc_mesh(),
               scratch_types=[pltpu.VMEM((2, SC_WINDOW, d), xn.dtype), pltpu.SemaphoreType.DMA((2,)),
                              pltpu.SemaphoreType.DMA((2,))],
               name="moe_dispatch_sc")
    def run(x_hbm, id_hbm, da_hbm, db_hbm, o_hbm, buf, load_sem, store_sem):
        def body(id_vmem, da_vmem, db_vmem):
            part = lambda j: pl.ds(j * SC_WINDOW, SC_WINDOW)
            load = lambda j: pltpu.make_async_copy(x_hbm.at[id_vmem.at[0, part(j)]], buf.at[j % 2], load_sem.at[j % 2])
            load(0).start()
            for j in range(n_parts):
                load(j).wait()
                if j + 1 < n_parts:
                    load(j + 1).start()
                stores = [pltpu.make_async_copy(buf.at[j % 2], o_hbm.at[dv.at[0, part(j)]], store_sem.at[k])
                          for k, dv in enumerate((da_vmem, db_vmem))]
                for s in stores:
                    s.start()
                for s in stores:
                    s.wait()

        idx_spec = pl.BlockSpec((1, SC_INDEX_WINDOW), lambda i: (0, i))
        pltpu.emit_pipeline(
            body,
            grid=(t // SC_INDEX_WINDOW,),
            in_specs=[idx_spec, idx_spec, idx_spec],
            out_specs=[],
            core_axis_name=("core", "subcore"),
            dimension_semantics=(pltpu.PARALLEL,),
        )(id_hbm, da_hbm, db_hbm)

    return run(xn, jnp.arange(t, dtype=jnp.int32).reshape(1, t), dest_a, dest_b)


def moe_gather_sc(ys, dest_a, dest_b):
    d = ys.shape[1]
    t = dest_a.shape[1]
    out = jax.ShapeDtypeStruct((t, d), ys.dtype)

    n_moves = 2 * (SC_INDEX_WINDOW // SC_WINDOW)

    @pl.kernel(out_type=(out, out), mesh=_sc_mesh(),
               scratch_types=[pltpu.VMEM((2, SC_WINDOW, d), ys.dtype), pltpu.SemaphoreType.DMA((2,)),
                              pltpu.SemaphoreType.DMA((2,))],
               name="moe_gather_sc")
    def run(y_hbm, id_hbm, da_hbm, db_hbm, ga_hbm, gb_hbm, buf, load_sem, store_sem):
        def body(id_vmem, da_vmem, db_vmem):
            part = lambda m: pl.ds((m // 2) * SC_WINDOW, SC_WINDOW)
            src = lambda m: (da_vmem, db_vmem)[m % 2]
            dst = lambda m: (ga_hbm, gb_hbm)[m % 2]
            load = lambda m: pltpu.make_async_copy(y_hbm.at[src(m).at[0, part(m)]], buf.at[m % 2], load_sem.at[m % 2])
            store = lambda m: pltpu.make_async_copy(buf.at[m % 2], dst(m).at[id_vmem.at[0, part(m)]],
                                                    store_sem.at[m % 2])
            load(0).start()
            for m in range(n_moves):
                load(m).wait()
                if m >= 1:
                    store(m - 1).wait()
                if m + 1 < n_moves:
                    load(m + 1).start()
                store(m).start()
            store(n_moves - 1).wait()

        idx_spec = pl.BlockSpec((1, SC_INDEX_WINDOW), lambda i: (0, i))
        pltpu.emit_pipeline(
            body,
            grid=(t // SC_INDEX_WINDOW,),
            in_specs=[idx_spec, idx_spec, idx_spec],
            out_specs=[],
            core_axis_name=("core", "subcore"),
            dimension_semantics=(pltpu.PARALLEL,),
        )(id_hbm, da_hbm, db_hbm)

    return run(ys, jnp.arange(t, dtype=jnp.int32).reshape(1, t), dest_a, dest_b)


def _combine_dense_kernel(h_ref, rec_ref, ga_ref, gb_ref, *out_refs, split):
    i = pl.program_id(0)
    rec = rec_ref[...]
    res = h_ref[...] + rec[:, 4:5] * _unpack_halves(ga_ref[...]) + rec[:, 5:6] * _unpack_halves(gb_ref[...])
    if not split:
        out_refs[0][...] = res
    else:
        @pl.when(i < N_PROMPT_TILES)
        def _():
            out_refs[0][...] = res

        @pl.when((i >= N_PROMPT_TILES) & (i < N_PROMPT_TILES + N_SAMPLE_TILES))
        def _():
            out_refs[1][...] = res


def moe_combine_dense(h, rec, ga, gb, split=False):
    t, d = h.shape
    row = lambda i: (i, 0)
    if split:
        out_shape = (jax.ShapeDtypeStruct((T_PROMPT, d), F32), jax.ShapeDtypeStruct((T_SAMPLE, d), F32))
        out_specs = tuple(_parts_specs(d)[:2])
    else:
        out_shape = jax.ShapeDtypeStruct((t, d), F32)
        out_specs = pl.BlockSpec((ROW_TILE, d), row)
    return pl.pallas_call(
        functools.partial(_combine_dense_kernel, split=split),
        out_shape=out_shape,
        grid=(t // ROW_TILE,),
        in_specs=[pl.BlockSpec((ROW_TILE, d), row), pl.BlockSpec((ROW_TILE, LANES), row),
                  pl.BlockSpec((ROW_TILE, d // 2), row), pl.BlockSpec((ROW_TILE, d // 2), row)],
        out_specs=out_specs,
        compiler_params=_cparams(("arbitrary",)),
        name="moe_combine_dense",
    )(h, rec, ga, gb)


def _combine_kv_q_kernel(h_ref, rec_ref, ga_ref, gb_ref, gkv_ref, wkv_ref, gq_ref, wq_ref, hmean_ref,
                         hgk_ref, hgq_ref, cos_ref, sina_ref, sinb_ref, ho_ref, k_ref, v_ref, q_ref):
    rec = rec_ref[...]
    h = h_ref[...] + rec[:, 4:5] * _unpack_halves(ga_ref[...]) + rec[:, 5:6] * _unpack_halves(gb_ref[...])
    ho_ref[...] = h
    xhat = h * lax.rsqrt(jnp.mean(h * h, axis=-1, keepdims=True) + RMS_EPS)
    tables = (cos_ref[...], sina_ref[...], sinb_ref[...])
    zkv = jnp.dot((xhat * gkv_ref[...]).astype(BF16), wkv_ref[...], preferred_element_type=F32)
    k_ref[...] = _head_norm_rope(zkv[:, :KV_DIM], hmean_ref, hgk_ref[...], *tables)
    v_ref[...] = zkv[:, KV_DIM:]
    zq = jnp.dot((xhat * gq_ref[...]).astype(BF16), wq_ref[...], preferred_element_type=F32)
    q = _head_norm_rope(zq, hmean_ref, hgq_ref[...], *tables)
    q_ref[...] = (q * HEAD_DIM ** -0.5).astype(q_ref.dtype)


def moe_combine_kv_q(h, rec, ga, gb, kv_gain, kv_w_bf16, q_gain, wq_bf16, hmean, k_hgain, q_hgain,
                     cos_t, sina_t, sinb_t):
    t, d = h.shape
    row = lambda i: (i, 0)
    fix = lambda i: (0, 0)
    rope = pl.BlockSpec((ROW_TILE, LANES), _rope_tile)
    return pl.pallas_call(
        _combine_kv_q_kernel,
        out_shape=(jax.ShapeDtypeStruct((t, d), F32), jax.ShapeDtypeStruct((t, KV_DIM), F32),
                   jax.ShapeDtypeStruct((t, KV_DIM), F32), jax.ShapeDtypeStruct((t, d), BF16)),
        grid=(t // ROW_TILE,),
        in_specs=[pl.BlockSpec((ROW_TILE, d), row), pl.BlockSpec((ROW_TILE, LANES), row),
                  pl.BlockSpec((ROW_TILE, d // 2), row), pl.BlockSpec((ROW_TILE, d // 2), row),
                  pl.BlockSpec((1, d), fix), pl.BlockSpec((d, 2 * KV_DIM), fix),
                  pl.BlockSpec((1, d), fix), pl.BlockSpec((d, d), fix),
                  pl.BlockSpec((256, 256), fix), pl.BlockSpec((1, KV_DIM), fix), pl.BlockSpec((1, d), fix),
                  rope, rope, rope],
        out_specs=(pl.BlockSpec((ROW_TILE, d), row), pl.BlockSpec((ROW_TILE, KV_DIM), row),
                   pl.BlockSpec((ROW_TILE, KV_DIM), row), pl.BlockSpec((ROW_TILE, d), row)),
        compiler_params=_cparams(("parallel",)),
        name="moe_combine_kv_q",
    )(h, rec, ga, gb, kv_gain.reshape(1, d), kv_w_bf16, q_gain.reshape(1, d), wq_bf16, hmean, k_hgain, q_hgain,
      cos_t, sina_t, sinb_t)


def _ffn_kernel(wblk_ref, we_ref, wlo_ref, whi_ref, wnext_ref, wpar_ref, xs_ref, w13_ref, w2_ref, ys_ref,
                w13f, w2f, w13b, w2b, wsem, *, layer):
    w = pl.program_id(0)
    prev = jnp.maximum(w - 1, 0)
    first_visit = (w == 0) | (wblk_ref[w] != wblk_ref[prev])
    lo = wlo_ref[w]
    hi = whi_ref[w]

    def weight_copies(e, par):
        return (pltpu.make_async_copy(w13_ref.at[layer, e], w13f.at[par], wsem.at[par, 0]),
                pltpu.make_async_copy(w2_ref.at[layer, e], w2f.at[par], wsem.at[par, 1]))

    @pl.when(w == 0)
    def _():
        for c in weight_copies(we_ref[0], 0):
            c.start()

    def ffn(x):
        x = _unpack_halves(x).astype(BF16)
        cw = D_EXPERT // FFN_CHUNKS
        gate_up = []
        for c in range(FFN_CHUNKS):
            a = jnp.dot(x, w13b[:, c * cw:(c + 1) * cw], preferred_element_type=F32)
            u = jnp.dot(x, w13b[:, D_EXPERT + c * cw:D_EXPERT + (c + 1) * cw], preferred_element_type=F32)
            gate_up.append((a, u))
        hmid = jnp.concatenate([(_silu(a) * u).astype(BF16) for a, u in gate_up], axis=1)
        return _pack_halves(jnp.dot(hmid, w2b[...], preferred_element_type=F32))

    @pl.when(hi > lo)
    def _():
        @pl.when((w == 0) | (we_ref[w] != we_ref[prev]))
        def _():
            par = wpar_ref[w]
            for c in weight_copies(we_ref[w], par):
                c.wait()
            w13b[...] = w13f[par].astype(BF16)
            w2b[...] = w2f[par].astype(BF16)
            nxt = wnext_ref[w]

            @pl.when(nxt >= 0)
            def _():
                for c in weight_copies(nxt, 1 - par):
                    c.start()

        whole = (lo == 0) & (hi == EXPERT_BLOCK)

        @pl.when(whole)
        def _():
            ys_ref[...] = ffn(xs_ref[...])

        half = EXPERT_BLOCK // 2
        for p in range(2):
            rows = slice(p * half, (p + 1) * half)
            touched = (lo < (p + 1) * half) & (hi > p * half)

            @pl.when(jnp.logical_not(whole) & touched)
            def _():
                y = ffn(xs_ref[rows, :])
                row = lax.broadcasted_iota(jnp.int32, y.shape, 0) + p * half
                mine = (row >= lo) & (row < hi)

                @pl.when(first_visit)
                def _():
                    ys_ref[rows, :] = jnp.where(mine, y, jnp.zeros_like(y))

                @pl.when(jnp.logical_not(first_visit))
                def _():
                    ys_ref[rows, :] = jnp.where(mine, y, ys_ref[rows, :])

            @pl.when(jnp.logical_not(whole) & jnp.logical_not(touched) & first_visit)
            def _():
                ys_ref[rows, :] = jnp.zeros((half, ys_ref.shape[1]), U32)


def moe_ffn(xs, work, w13_all, w2_all, layer):
    n_slots, dp = xs.shape
    d = 2 * dp
    n_work = work[0].shape[0]
    xmap = lambda w, *prefetch: (prefetch[0][w], 0)
    grid_spec = pltpu.PrefetchScalarGridSpec(
        num_scalar_prefetch=len(work),
        grid=(n_work,),
        in_specs=[pl.BlockSpec((EXPERT_BLOCK, dp), xmap),
                  pl.BlockSpec(memory_space=pl.ANY), pl.BlockSpec(memory_space=pl.ANY)],
        out_specs=pl.BlockSpec((EXPERT_BLOCK, dp), xmap),
        scratch_shapes=[pltpu.VMEM((2, d, 2 * D_EXPERT), F32), pltpu.VMEM((2, D_EXPERT, d), F32),
                        pltpu.VMEM((d, 2 * D_EXPERT), BF16), pltpu.VMEM((D_EXPERT, d), BF16),
                        pltpu.SemaphoreType.DMA((2, 2))],
    )
    return pl.pallas_call(
        functools.partial(_ffn_kernel, layer=layer),
        out_shape=jax.ShapeDtypeStruct((n_slots, dp), U32),
        grid_spec=grid_spec,
        compiler_params=_cparams(("arbitrary",)),
        name="moe_ffn",
    )(*work, xs, w13_all, w2_all)


def _ffn_work_items(cnt):
    n_slots = 2 * T_ALL
    n_blocks = n_slots // EXPERT_BLOCK
    n_work = n_blocks + N_EXPERTS - 1
    end = jnp.cumsum(cnt)
    start = end - cnt
    first_blk = start // EXPERT_BLOCK
    last_blk = jnp.maximum(end - 1, start) // EXPERT_BLOCK
    n_items = jnp.where(cnt > 0, last_blk - first_blk + 1, 0)
    item_end = jnp.cumsum(n_items)
    item_start = item_end - n_items
    w = jnp.arange(n_work, dtype=jnp.int32)
    used = w < item_end[-1]
    wq = jnp.minimum(w, item_end[-1] - 1)
    e = jnp.sum((item_end[:, None] <= wq[None, :]).astype(jnp.int32), axis=0)
    onehot = e[None, :] == jnp.arange(N_EXPERTS, dtype=jnp.int32)[:, None]
    of_e = lambda table: jnp.sum(jnp.where(onehot, table[:, None], 0), axis=0)
    blk = jnp.where(used, of_e(first_blk) + (w - of_e(item_start)), n_blocks - 1).astype(jnp.int32)
    lo = jnp.maximum(of_e(start), blk * EXPERT_BLOCK) - blk * EXPERT_BLOCK
    hi = jnp.minimum(of_e(end), (blk + 1) * EXPERT_BLOCK) - blk * EXPERT_BLOCK
    lo = jnp.where(used, lo, 0).astype(jnp.int32)
    hi = jnp.where(used, hi, 0).astype(jnp.int32)
    e_before = jnp.concatenate([jnp.full((1,), -1, jnp.int32), e[:-1]])
    change = used & (e != e_before)
    parity = ((jnp.cumsum(change.astype(jnp.int32)) - 1) % 2).astype(jnp.int32)
    far = jnp.int32(n_work)
    next_change = lax.cummin(jnp.where(change, w, far), axis=0, reverse=True)
    next_change = jnp.concatenate([next_change[1:], jnp.full((1,), far, jnp.int32)])
    e_next = jnp.sum(jnp.where(next_change[None, :] == w[:, None], e[:, None], 0), axis=0)
    e_next = jnp.where(next_change < far, e_next, -1).astype(jnp.int32)
    return start, (blk, e, lo, hi, e_next, parity)


def hier_moe_layer(a, w_out_bf16, x, layer, gain, w_group, b_group, w_expert, b_expert, w13_all, w2_all, utri,
                   finish):
    t = T_ALL
    pad = LANES - N_EXPERTS - N_GROUPS
    w_router = jnp.concatenate([w_expert, w_group, jnp.zeros((D_MODEL, pad), F32)], axis=1)
    b_router = jnp.concatenate([b_expert, b_group, jnp.zeros((pad,), F32)])[:ROUTE_ROWS].reshape(ROUTE_ROWS, 1)
    h, xn, rec, rect, counts = moe_route(a, w_out_bf16, x, gain, w_router, b_router, utri)

    cnt = counts[:N_EXPERTS, 0].astype(jnp.int32)
    start, work = _ffn_work_items(cnt)
    experts = jnp.arange(N_EXPERTS, dtype=jnp.int32)[:, None]

    def slot_of(e_row, rank_row):
        first = jnp.sum(jnp.where(e_row.astype(jnp.int32)[None, :] == experts, start[:, None], 0), axis=0)
        return (first + rank_row.astype(jnp.int32)).reshape(1, t)

    dest_a = slot_of(rect[0], rect[2])
    dest_b = slot_of(rect[1], rect[3])

    xs = moe_dispatch_sc(xn, dest_a, dest_b, 2 * t)
    ys = moe_ffn(xs, work, w13_all, w2_all, layer)
    ga, gb = moe_gather_sc(ys, dest_a, dest_b)
    return finish(h, rec, ga, gb)


def _rope_tables(pos):
    half = ROPE_DIM // 2
    lane = np.arange(LANES) % HEAD_DIM
    rotary = lane < ROPE_DIM
    inv = jnp.where(rotary, jnp.exp(-math.log(ROPE_THETA) * jnp.asarray(lane % half, F32) * (2.0 / ROPE_DIM)), 0.0)
    ang = pos.astype(F32)[:, None] * inv[None, :]
    cos, sin = jnp.cos(ang), jnp.sin(ang)
    first = jnp.asarray(lane < half)
    second = jnp.asarray(rotary & (lane >= half))
    return cos, jnp.where(first, -sin, 0.0), jnp.where(second, sin, 0.0)


def kernel(x_prompt, x_sample, state_hgrn, cache_k_win, cache_v_win, meta_tokens, a_norm, a_w_in, a_lower_logits, a_out_norm, a_w_out, kv_norm, kv_w, k_norm, b_norm, b_wq, b_q_norm, b_sinks, b_w_out, moe_norm, moe_w_group, moe_b_group, moe_w_expert, moe_b_expert, moe_w13, moe_w2):
    tail_rows = T_ALL - OFF_META
    x_parts = (x_prompt.reshape(T_PROMPT, D_MODEL), x_sample.reshape(T_SAMPLE, D_MODEL),
               jnp.concatenate([meta_tokens.astype(F32), jnp.zeros((tail_rows - N_META, D_MODEL), F32)], axis=0))
    pos = jnp.concatenate([N_META + jnp.arange(SEQ, dtype=jnp.int32),
                           jnp.tile(PAST_LEN + jnp.arange(DEC_SEQ, dtype=jnp.int32), ROW_TILE // DEC_SEQ),
                           jnp.arange(N_META, dtype=jnp.int32),
                           jnp.zeros((ROW_TILE - N_META,), jnp.int32)])
    cos_t, sina_t, sinb_t = _rope_tables(pos)
    r256 = np.arange(256)
    hmean = jnp.asarray((r256[:, None] // HEAD_DIM == r256[None, :] // HEAD_DIM).astype(np.float32) / HEAD_DIM, BF16)
    rt = np.arange(ROW_TILE)
    utri = jnp.asarray((rt[:, None] < rt[None, :]).astype(np.float32), BF16)
    lower = jnp.cumsum(jax.nn.softmax(a_lower_logits.astype(F32), axis=0), axis=0)

    moe = functools.partial(hier_moe_layer, w13_all=moe_w13, w2_all=moe_w2, utri=utri)

    z = in_project(x_parts, a_norm[0], a_w_in[0].astype(BF16))
    zero_state = jnp.zeros((1, A_HEADS, A_DK, A_DV), F32)
    o_meta, s_meta = hgrn2_scan(z, zero_state, lower[0], a_out_norm[0],
                                row_off=OFF_META, n_seq=1, seq_len=N_META)
    o_prompt, s_prompt = hgrn2_scan(z, s_meta, lower[0], a_out_norm[0], row_off=0, n_seq=BATCH, seq_len=SEQ)
    o_sample, s_sample = hgrn2_scan(z, state_hgrn[0].astype(F32), lower[0], a_out_norm[0],
                                    row_off=OFF_SAMPLE, n_seq=DEC_BATCH, seq_len=DEC_SEQ, group=SCAN_SAMPLE_GROUP)
    o_tail = jnp.concatenate([o_meta, jnp.zeros((tail_rows - N_META, D_MODEL), BF16)], axis=0)
    finish0 = functools.partial(
        moe_combine_kv_q, kv_gain=kv_norm, kv_w_bf16=kv_w.astype(BF16), q_gain=b_norm[0], wq_bf16=b_wq[0].astype(BF16),
        hmean=hmean, k_hgain=jnp.tile(k_norm, KV_HEADS).reshape(1, KV_DIM),
        q_hgain=jnp.tile(b_q_norm[0], Q_HEADS).reshape(1, D_MODEL), cos_t=cos_t, sina_t=sina_t, sinb_t=sinb_t)
    h, k_all, v_all, q_all = moe((o_prompt, o_sample, o_tail), a_w_out[0].astype(BF16), x_parts, 0, moe_norm[0],
                                 moe_w_group[0], moe_b_group[0], moe_w_expert[0], moe_b_expert[0], finish=finish0)

    meta_blk = lambda a: jnp.concatenate([jnp.zeros((ATT_BLOCK - N_META, KV_DIM), F32),
                                          a[OFF_META:OFF_META + N_META]], axis=0)
    sinks = b_sinks[0].astype(F32)
    att_all = attention_prompt(q_all, k_all, v_all, meta_blk(k_all), meta_blk(v_all), sinks)
    att_all, k_win_s, v_win_s = attention_sample(q_all, cache_k_win.astype(F32), cache_v_win.astype(F32),
                                                 k_all, v_all, sinks, att_all)
    y_p, y_s = moe(att_all, b_w_out[0].astype(BF16), h, 1, moe_norm[1], moe_w_group[1], moe_b_group[1],
                   moe_w_expert[1], moe_b_expert[1], finish=functools.partial(moe_combine_dense, split=True))

    y_prompt = y_p.reshape(BATCH, SEQ, D_MODEL)
    y_sample = y_s.reshape(DEC_BATCH, DEC_SEQ, D_MODEL)
    last = lambda a: jnp.stack([a[(b + 1) * SEQ - WINDOW:(b + 1) * SEQ] for b in range(BATCH)]).reshape(
        BATCH, WINDOW, KV_HEADS, HEAD_DIM)
    kp = last(k_all)
    vp = last(v_all)
    return (y_prompt, y_sample, s_prompt[None], s_sample[None], kp, vp, k_win_s, v_win_s)
```

```python
import functools
import math

import numpy as np
import jax
import jax.numpy as jnp
from jax import lax
from jax.experimental import pallas as pl
from jax.experimental.pallas import tpu as pltpu
from jax.experimental.pallas import tpu_sc as plsc

F32 = jnp.float32
BF16 = jnp.bfloat16
U32 = jnp.uint32

D_MODEL = 1024
BATCH = 4
SEQ = 4096
DEC_BATCH = 128
DEC_SEQ = 8
PAST_LEN = 8192
N_META = 16
A_HEADS = 8
A_DK = 128
A_DV = 128
Q_HEADS = 16
KV_HEADS = 4
HEAD_DIM = 64
KV_DIM = KV_HEADS * HEAD_DIM
WINDOW = 128
ROPE_DIM = 16
ROPE_THETA = 500000.0
N_GROUPS = 4
EXPERTS_PER_GROUP = 8
N_EXPERTS = 32
D_EXPERT = 512
RMS_EPS = 1e-6

LANES = 128
SUBLANES = 8
VMEM_LIMIT = 56 * 1024 * 1024

ROW_TILE = 512
T_PROMPT = BATCH * SEQ
T_SAMPLE = DEC_BATCH * DEC_SEQ
OFF_SAMPLE = T_PROMPT
OFF_META = T_PROMPT + T_SAMPLE
T_REAL = OFF_META + N_META
T_ALL = -(-T_REAL // ROW_TILE) * ROW_TILE
N_TILES = T_ALL // ROW_TILE

SCAN_CHUNK = 128
SCAN_SAMPLE_GROUP = 8
ATT_BLOCK = 128
EXPERT_BLOCK = 1024
FFN_PART = 256
FFN_CHUNKS = 2


def _cparams(sem):
    return pltpu.CompilerParams(dimension_semantics=sem, vmem_limit_bytes=VMEM_LIMIT)


def _nt_dot(a, b):
    return lax.dot_general(a, b, (((1,), (1,)), ((), ())), preferred_element_type=F32)


def _rms(x, gain):
    ms = jnp.mean(x * x, axis=-1, keepdims=True)
    return x * lax.rsqrt(ms + RMS_EPS) * gain


def _silu(x):
    return x * jax.nn.sigmoid(x)


def _pack_halves(x):
    w = x.shape[1] // 2
    hi = lax.bitcast_convert_type(x[:, :w].astype(BF16).astype(F32), U32)
    lo = lax.bitcast_convert_type(x[:, w:].astype(BF16).astype(F32), U32)
    return hi | (lo >> 16)


def _unpack_halves(p):
    hi = lax.bitcast_convert_type(p & jnp.uint32(0xFFFF0000), F32)
    lo = lax.bitcast_convert_type(p << 16, F32)
    return jnp.concatenate([hi, lo], axis=1)


N_PROMPT_TILES = T_PROMPT // ROW_TILE
N_SAMPLE_TILES = T_SAMPLE // ROW_TILE


def _parts_specs(width):
    return [pl.BlockSpec((ROW_TILE, width), lambda i: (jnp.minimum(i, N_PROMPT_TILES - 1), 0)),
            pl.BlockSpec((ROW_TILE, width), lambda i: (jnp.clip(i - N_PROMPT_TILES, 0, N_SAMPLE_TILES - 1), 0)),
            pl.BlockSpec((ROW_TILE, width), lambda i: (0, 0))]


def _pick_part(i, p_ref, s_ref, t_ref, dtype):
    return jnp.where(i < N_PROMPT_TILES, p_ref[...].astype(dtype),
                     jnp.where(i < N_PROMPT_TILES + N_SAMPLE_TILES, s_ref[...].astype(dtype),
                               t_ref[...].astype(dtype)))


def _in_proj_kernel(xp_ref, xs_ref, xt_ref, g_ref, w_ref, o_ref):
    x = _pick_part(pl.program_id(0), xp_ref, xs_ref, xt_ref, F32)
    xn = _rms(x, g_ref[...])
    o_ref[...] = jnp.dot(xn.astype(BF16), w_ref[...], preferred_element_type=F32)


def in_project(x_parts, gain, w_bf16):
    d, n = w_bf16.shape
    return pl.pallas_call(
        _in_proj_kernel,
        out_shape=jax.ShapeDtypeStruct((T_ALL, n), F32),
        grid=(N_TILES,),
        in_specs=_parts_specs(d) + [pl.BlockSpec((1, d), lambda i: (0, 0)),
                                    pl.BlockSpec((d, n), lambda i: (0, 0))],
        out_specs=pl.BlockSpec((ROW_TILE, n), lambda i: (i, 0)),
        compiler_params=_cparams(("parallel",)),
        name="in_project",
    )(*x_parts, gain.reshape(1, d), w_bf16)


def _head_norm_rope(y, hmean_ref, hgain, cos_t, sina_t, sinb_t):
    rows, width = y.shape
    sq = (y * y).astype(BF16)
    parts = []
    for s in range(width // 256):
        parts.append(jnp.dot(sq[:, s * 256:(s + 1) * 256], hmean_ref[...], preferred_element_type=F32))
    ms = parts[0] if len(parts) == 1 else jnp.concatenate(parts, axis=1)
    yn = y * lax.rsqrt(ms + RMS_EPS) * hgain
    reps = width // LANES
    cos_w = jnp.concatenate([cos_t] * reps, axis=1)
    sina_w = jnp.concatenate([sina_t] * reps, axis=1)
    sinb_w = jnp.concatenate([sinb_t] * reps, axis=1)
    half = ROPE_DIM // 2
    nxt = pltpu.roll(yn, width - half, 1)
    prv = pltpu.roll(yn, half, 1)
    return yn * cos_w + nxt * sina_w + prv * sinb_w


def _rope_tile(i):
    tiles_per_seq = SEQ // ROW_TILE
    n_prompt_tiles = T_PROMPT // ROW_TILE
    n_sample_tiles = T_SAMPLE // ROW_TILE
    return (jnp.where(i < n_prompt_tiles, i % tiles_per_seq,
                      jnp.where(i < n_prompt_tiles + n_sample_tiles, tiles_per_seq, tiles_per_seq + 1)), 0)


def _scan_levels(c):
    levels = []
    m = c
    while m >= 2:
        levels.append(m)
        m //= 2
    return levels


LOG2E = 1.4426950408889634


def _scan_kernel(z_ref, s0_ref, lb_ref, og_ref, tri_ref, lmask_ref, sgn_ref, o_ref, sfin_ref, s_scr, b_scr,
                 *, rows, seq_len):
    c_idx = pl.program_id(1)
    levels = _scan_levels(seq_len)
    n_sub = rows // seq_len
    hk = A_HEADS * A_DK

    @pl.when(c_idx == 0)
    def _():
        s_scr[...] = s0_ref[...]

    sub = lax.broadcasted_iota(jnp.int32, (SUBLANES, LANES), 0)
    row = lax.broadcasted_iota(jnp.int32, (LANES, LANES), 0)
    og = og_ref[...]

    def pad_f32(x):
        if x.shape[0] == LANES:
            return x
        return jnp.concatenate([x, jnp.zeros((LANES - x.shape[0], x.shape[1]), x.dtype)], axis=0)

    def pad_rows(x):
        return pad_f32(x).astype(BF16)

    def cols(part, h):
        return slice(part * hk + h * LANES, part * hk + (h + 1) * LANES)

    def gates(h):
        lb = lb_ref[:, cols(0, h)]
        forget = lb + (1.0 - lb) * jax.nn.sigmoid(z_ref[:, cols(1, h)])
        logf = jnp.log(forget)
        hi = logf.astype(BF16).astype(F32)
        r1 = logf - hi
        mid = r1.astype(BF16).astype(F32)
        lo = r1 - mid
        cs = jnp.dot(tri_ref[...], pad_rows(jnp.concatenate([hi, mid, lo], axis=1)),
                     preferred_element_type=F32)
        b = (cs[:rows, :LANES] + cs[:rows, LANES:2 * LANES]) + cs[:rows, 2 * LANES:]
        b_scr[h] = b
        return _silu(z_ref[:, cols(0, h)]), 1.0 - forget, b

    def bref_for(h, m):
        b_rows = b_scr.at[h]
        half = m // 2
        pieces = []
        for g in range(rows // SUBLANES):
            base = g * SUBLANES
            if m >= SUBLANES:
                r = (base // m) * m + half - 1
                piece = jnp.broadcast_to(b_rows[r:r + 1, :], (SUBLANES, LANES))
            else:
                piece = jnp.broadcast_to(b_rows[base + half - 1:base + half, :], (SUBLANES, LANES))
                for blk in range(1, SUBLANES // m):
                    r = base + blk * m + half - 1
                    piece = jnp.where(sub >= blk * m,
                                      jnp.broadcast_to(b_rows[r:r + 1, :], (SUBLANES, LANES)), piece)
            pieces.append(piece)
        return pieces[0] if len(pieces) == 1 else jnp.concatenate(pieces, axis=0)

    heads = range(A_HEADS)
    qkb = [gates(h) for h in heads]
    att = [_nt_dot(pad_rows(qf), pad_rows(kf)) * lmask_ref[len(levels)] for qf, kf, _ in qkb]
    for li, m in enumerate(levels):
        for h in heads:
            qf, kf, b = qkb[h]
            sgn = sgn_ref[li]
            e = jnp.exp2((b - bref_for(h, m)) * sgn)
            w = pad_rows(jnp.where(sgn > 0, qf, kf) * e)
            att[h] = att[h] + _nt_dot(w, w) * lmask_ref[li]

    def finish(h):
        qf, kf, b = qkb[h]
        b_rows = b_scr.at[h]
        v_b = pad_rows(z_ref[:, cols(2, h)])
        o_intra = jnp.dot(att[h].astype(BF16), v_b, preferred_element_type=F32)
        eb = jnp.exp(b)
        qs = qf * eb
        b_end = [jnp.broadcast_to(b_rows[(i + 1) * seq_len - 1:(i + 1) * seq_len, :], (seq_len, LANES))
                 for i in range(n_sub)]
        b_end = b_end[0] if n_sub == 1 else jnp.concatenate(b_end, axis=0)
        kd_t = pad_f32(kf * jnp.exp(b_end - b)).T.astype(BF16)
        eb_t = pad_f32(eb).T
        qs_b = pad_rows(qs)
        o = o_intra
        for i in range(n_sub):
            s_old = s_scr[i, h]
            first, last = i * seq_len, (i + 1) * seq_len - 1
            if n_sub == 1:
                qs_i, v_i = qs_b, v_b
            else:
                mine = (row >= first) & (row <= last)
                qs_i = jnp.where(mine, qs_b, jnp.zeros_like(qs_b))
                v_i = jnp.where(mine, v_b, jnp.zeros_like(v_b))
            o = o + jnp.dot(qs_i, s_old.astype(BF16), preferred_element_type=F32)
            decay = jnp.broadcast_to(eb_t[:, last:last + 1], (LANES, LANES))
            s_scr[i, h] = decay * s_old + jnp.dot(kd_t, v_i, preferred_element_type=F32)
        o = o[:rows]

        on = _rms(o, og) * _silu(z_ref[:, cols(3, h)])
        o_ref[:, cols(0, h)] = on.astype(o_ref.dtype)

    for h in heads:
        finish(h)

    @pl.when(c_idx == pl.num_programs(1) - 1)
    def _():
        sfin_ref[...] = s_scr[...]


def _scan_consts(rows, seq_len):
    levels = _scan_levels(seq_len)
    r = np.arange(LANES)
    t, s = r[:, None], r[None, :]
    live = (t < rows) & (s < rows)
    tri = ((s <= t) & (t // seq_len == s // seq_len) & live).astype(np.float32)
    masks, sgns = [], []
    for m in levels:
        masks.append(((t // m == s // m) & (t % m >= m // 2) & (s % m < m // 2) & live).astype(np.float32))
        sgns.append(np.broadcast_to(np.where(r[:rows, None] % m >= m // 2, LOG2E, -LOG2E), (rows, LANES)))
    masks.append(((t == s) & live).astype(np.float32))
    return jnp.asarray(tri, BF16), jnp.asarray(np.stack(masks), F32), jnp.asarray(np.stack(sgns), F32)


def hgrn2_scan(z, s0, lb, o_gain, *, row_off, n_seq, seq_len, group=1):
    hv = A_HEADS * A_DV
    if seq_len > SCAN_CHUNK:
        assert group == 1
        sub_len, rows, n_chunks, n_steps = SCAN_CHUNK, SCAN_CHUNK, seq_len // SCAN_CHUNK, n_seq
    else:
        sub_len, rows, n_chunks, n_steps = seq_len, group * seq_len, 1, n_seq // group
    blk_off = row_off // rows
    tri, lmask, sgn = _scan_consts(rows, sub_len)
    shared_s0 = s0.shape[0] == 1
    fix2 = lambda s, c: (0, 0)
    fix3 = lambda s, c: (0, 0, 0)
    o, sfin = pl.pallas_call(
        functools.partial(_scan_kernel, rows=rows, seq_len=sub_len),
        out_shape=(jax.ShapeDtypeStruct((n_seq * seq_len, hv), BF16 if rows % 16 == 0 else F32),
                   jax.ShapeDtypeStruct((n_seq, A_HEADS, A_DK, A_DV), F32)),
        grid=(n_steps, n_chunks),
        in_specs=[pl.BlockSpec((rows, 4 * hv), lambda s, c: (blk_off + s * n_chunks + c, 0)),
                  pl.BlockSpec((group, A_HEADS, A_DK, A_DV), (lambda s, c: (0, 0, 0, 0)) if shared_s0
                               else (lambda s, c: (s, 0, 0, 0))),
                  pl.BlockSpec((1, hv), fix2), pl.BlockSpec((1, A_DV), fix2),
                  pl.BlockSpec((LANES, LANES), fix2), pl.BlockSpec(lmask.shape, fix3),
                  pl.BlockSpec(sgn.shape, fix3)],
        out_specs=(pl.BlockSpec((rows, hv), lambda s, c: (s * n_chunks + c, 0)),
                   pl.BlockSpec((group, A_HEADS, A_DK, A_DV), lambda s, c: (s, 0, 0, 0))),
        scratch_shapes=[pltpu.VMEM((group, A_HEADS, A_DK, A_DV), F32), pltpu.VMEM((A_HEADS, rows, LANES), F32)],
        compiler_params=_cparams(("parallel", "arbitrary")),
        name=f"hgrn2_scan_r{rows}",
    )(z, s0, lb.reshape(1, hv), o_gain.reshape(1, A_DV), tri, lmask, sgn)
    return o, sfin


KEYS = 2 * ATT_BLOCK
ATT_STEP = 1


def _pair_operand(x, kh):
    slab = x[:, (kh // 2) * LANES:(kh // 2 + 1) * LANES]
    lane = lax.broadcasted_iota(jnp.int32, slab.shape, 1)
    if kh % 2 == 0:
        lo = jnp.where(lane < HEAD_DIM, slab, 0.0)
        hi = pltpu.roll(lo, HEAD_DIM, 1)
    else:
        hi = jnp.where(lane >= HEAD_DIM, slab, 0.0)
        lo = pltpu.roll(hi, HEAD_DIM, 1)
    return jnp.concatenate([lo, hi], axis=0).astype(BF16)


def _window_bias(rows, jmin):
    t_i = lax.broadcasted_iota(jnp.int32, (rows, 2 * KEYS), 0)
    c_i = lax.broadcasted_iota(jnp.int32, (rows, 2 * KEYS), 1)
    j_i = c_i & (ATT_BLOCK - 1)
    own = (c_i & ATT_BLOCK) != 0
    ok = (own & (j_i <= t_i)) | (jnp.logical_not(own) & (j_i >= t_i) & (j_i >= jmin))
    return jnp.where(ok, 0.0, -jnp.inf).astype(F32)


def _pair_softmax(s, sink_a, sink_b):
    probs, rinv = [], []
    for hh, sink in enumerate((sink_a, sink_b)):
        sh = s[:, hh * KEYS:(hh + 1) * KEYS]
        m = jnp.maximum(jnp.max(sh, axis=-1, keepdims=True), sink)
        p = jnp.exp(sh - m)
        den = jnp.sum(p, axis=-1, keepdims=True) + jnp.exp(sink - m)
        probs.append(p.astype(BF16))
        rinv.append(1.0 / den)
    lane = lax.broadcasted_iota(jnp.int32, (s.shape[0], LANES), 1)
    return jnp.concatenate(probs, axis=1), jnp.where(lane < HEAD_DIM, rinv[0], rinv[1])


def _attn_prompt_kernel(sink_ref, q_ref, kp_ref, ko_ref, vp_ref, vo_ref, km_ref, vm_ref, o_ref,
                        k2_scr, v2_scr, s_scr, p_scr, r_scr, bias_scr):
    n = pl.program_id(0)
    steps_per_seq = SEQ // (ATT_STEP * ATT_BLOCK)
    n_pairs = Q_HEADS // 2
    blk = ATT_BLOCK

    @pl.when(n == 0)
    def _():
        bias_scr[0] = _window_bias(ATT_BLOCK, 0)
        bias_scr[1] = _window_bias(ATT_BLOCK, ATT_BLOCK - N_META)

    @pl.when(n >= BATCH * steps_per_seq)
    def _():
        o_ref[...] = jnp.zeros_like(o_ref)

    @pl.when(n < BATCH * steps_per_seq)
    def _():
        first = (n % steps_per_seq) == 0
        own_k, own_v = ko_ref[...], vo_ref[...]
        keys = [jnp.where(first, km_ref[...], kp_ref[...])] + [own_k[u * blk:(u + 1) * blk] for u in range(ATT_STEP)]
        vals = [jnp.where(first, vm_ref[...], vp_ref[...])] + [own_v[u * blk:(u + 1) * blk] for u in range(ATT_STEP)]
        biases = [bias_scr[first.astype(jnp.int32)]] + [bias_scr[0]] * (ATT_STEP - 1)
        for u in range(ATT_STEP):
            k = jnp.concatenate(keys[u:u + 2], axis=0)
            v = jnp.concatenate(vals[u:u + 2], axis=0)
            for kh in range(KV_HEADS):
                k2_scr[u, kh] = _pair_operand(k, kh)
                v2_scr[u, kh] = _pair_operand(v, kh)
        units = [(u, pair) for u in range(ATT_STEP) for pair in range(n_pairs)]
        rows = lambda u: slice(u * blk, (u + 1) * blk)
        lanes = lambda pair: slice(pair * LANES, (pair + 1) * LANES)
        for j, (u, pair) in enumerate(units):
            s_scr[j] = _nt_dot(q_ref[rows(u), lanes(pair)], k2_scr[u, pair // 2]) + biases[u]
        for j, (u, pair) in enumerate(units):
            p, rinv = _pair_softmax(s_scr[j], sink_ref[2 * pair], sink_ref[2 * pair + 1])
            p_scr[j] = p
            r_scr[j] = rinv
        for j, (u, pair) in enumerate(units):
            o = jnp.dot(p_scr[j], v2_scr[u, pair // 2], preferred_element_type=F32) * r_scr[j]
            o_ref[rows(u), lanes(pair)] = o.astype(o_ref.dtype)


def attention_prompt(q_all, k_all, v_all, k_meta_blk, v_meta_blk, sinks):
    step_rows = ATT_STEP * ATT_BLOCK
    n_prompt_steps = T_PROMPT // step_rows
    n_steps = T_ALL // step_rows
    n_units = ATT_STEP * (Q_HEADS // 2)
    own = lambda n, sk: (jnp.minimum(n, n_prompt_steps - 1), 0)
    prev = lambda n, sk: (jnp.maximum(ATT_STEP * jnp.minimum(n, n_prompt_steps - 1) - 1, 0), 0)
    fix = lambda n, sk: (0, 0)
    grid_spec = pltpu.PrefetchScalarGridSpec(
        num_scalar_prefetch=1,
        grid=(n_steps,),
        in_specs=[pl.BlockSpec((step_rows, D_MODEL), own),
                  pl.BlockSpec((ATT_BLOCK, KV_DIM), prev), pl.BlockSpec((step_rows, KV_DIM), own),
                  pl.BlockSpec((ATT_BLOCK, KV_DIM), prev), pl.BlockSpec((step_rows, KV_DIM), own),
                  pl.BlockSpec((ATT_BLOCK, KV_DIM), fix), pl.BlockSpec((ATT_BLOCK, KV_DIM), fix)],
        out_specs=pl.BlockSpec((step_rows, D_MODEL), lambda n, sk: (n, 0)),
        scratch_shapes=[pltpu.VMEM((ATT_STEP, KV_HEADS, 2 * KEYS, LANES), BF16),
                        pltpu.VMEM((ATT_STEP, KV_HEADS, 2 * KEYS, LANES), BF16),
                        pltpu.VMEM((n_units, ATT_BLOCK, 2 * KEYS), F32),
                        pltpu.VMEM((n_units, ATT_BLOCK, 2 * KEYS), BF16),
                        pltpu.VMEM((n_units, ATT_BLOCK, LANES), F32),
                        pltpu.VMEM((2, ATT_BLOCK, 2 * KEYS), F32)],
    )
    return pl.pallas_call(
        _attn_prompt_kernel,
        out_shape=jax.ShapeDtypeStruct((T_ALL, D_MODEL), BF16),
        grid_spec=grid_spec,
        compiler_params=_cparams(("arbitrary",)),
        name="attention_prompt",
    )(sinks, q_all, k_all, k_all, v_all, v_all, k_meta_blk, v_meta_blk)


SAMPLE_GROUP = ATT_BLOCK // DEC_SEQ


def _attn_sample_kernel(sink_ref, q_ref, ck_ref, cv_ref, kn_ref, vn_ref, buf_ref, o_ref, kw_ref, vw_ref,
                        qf_scr, of_scr, k2_scr, v2_scr):
    del buf_ref
    qrows = 2 * DEC_SEQ
    qf_scr[...] = q_ref[...].astype(F32)
    bias = _window_bias(qrows, 0)
    zq = jnp.zeros((qrows - DEC_SEQ, D_MODEL), F32)
    zk = jnp.zeros((ATT_BLOCK - DEC_SEQ, KV_DIM), F32)

    n_pairs = Q_HEADS // 2
    lanes_of = lambda pair: slice(pair * LANES, (pair + 1) * LANES)

    def seq_pair_body(it, carry):
        seqs = (2 * it, 2 * it + 1)
        r_new = [pl.multiple_of(i * DEC_SEQ, DEC_SEQ) for i in seqs]
        qs = []
        for u, i in enumerate(seqs):
            qs.append(jnp.concatenate([qf_scr[pl.ds(r_new[u], DEC_SEQ), :], zq], axis=0).astype(BF16))
            old = lambda c_ref: jnp.concatenate([c_ref[i, :, kh, :] for kh in range(KV_HEADS)], axis=1)
            k = jnp.concatenate([old(ck_ref), kn_ref[pl.ds(r_new[u], DEC_SEQ), :], zk], axis=0)
            v = jnp.concatenate([old(cv_ref), vn_ref[pl.ds(r_new[u], DEC_SEQ), :], zk], axis=0)
            for c_ref, n_ref, w_ref in ((ck_ref, kn_ref, kw_ref), (cv_ref, vn_ref, vw_ref)):
                w_ref[i, 0:WINDOW - DEC_SEQ] = c_ref[i, DEC_SEQ:WINDOW]
                for kh in range(KV_HEADS):
                    w_ref[i, WINDOW - DEC_SEQ:WINDOW, kh, :] = n_ref[pl.ds(r_new[u], DEC_SEQ),
                                                                     kh * HEAD_DIM:(kh + 1) * HEAD_DIM]
            for kh in range(KV_HEADS):
                k2_scr[u, kh] = _pair_operand(k, kh)
                v2_scr[u, kh] = _pair_operand(v, kh)
        scores = [[_nt_dot(qs[u][:, lanes_of(pair)], k2_scr[u, pair // 2]) + bias for pair in range(n_pairs)]
                  for u in range(2)]
        soft = [[_pair_softmax(s, sink_ref[2 * pair], sink_ref[2 * pair + 1]) for pair, s in enumerate(scores[u])]
                for u in range(2)]
        for u in range(2):
            for pair, (p, rinv) in enumerate(soft[u]):
                o = jnp.dot(p, v2_scr[u, pair // 2], preferred_element_type=F32) * rinv
                of_scr[pl.ds(r_new[u], DEC_SEQ), lanes_of(pair)] = o[:DEC_SEQ]
        return carry

    lax.fori_loop(0, SAMPLE_GROUP // 2, seq_pair_body, 0)
    o_ref[...] = of_scr[...].astype(o_ref.dtype)


def attention_sample(q_all, cache_k, cache_v, k_all, v_all, sinks, out_buf):
    first_blk = OFF_SAMPLE // ATT_BLOCK
    new = lambda g, sk: (first_blk + g, 0)
    old = pl.BlockSpec((SAMPLE_GROUP, WINDOW, KV_HEADS, HEAD_DIM), lambda g, sk: (g, 0, 0, 0))
    grid_spec = pltpu.PrefetchScalarGridSpec(
        num_scalar_prefetch=1,
        grid=(DEC_BATCH // SAMPLE_GROUP,),
        in_specs=[pl.BlockSpec((ATT_BLOCK, D_MODEL), new),
                  old, old,
                  pl.BlockSpec((ATT_BLOCK, KV_DIM), new), pl.BlockSpec((ATT_BLOCK, KV_DIM), new),
                  pl.BlockSpec(memory_space=pl.ANY)],
        out_specs=(pl.BlockSpec((ATT_BLOCK, D_MODEL), new), old, old),
        scratch_shapes=[pltpu.VMEM((ATT_BLOCK, D_MODEL), F32), pltpu.VMEM((ATT_BLOCK, D_MODEL), F32),
                        pltpu.VMEM((2, KV_HEADS, 2 * KEYS, LANES), BF16),
                        pltpu.VMEM((2, KV_HEADS, 2 * KEYS, LANES), BF16)],
    )
    window = jax.ShapeDtypeStruct(cache_k.shape, cache_k.dtype)
    return pl.pallas_call(
        _attn_sample_kernel,
        out_shape=(jax.ShapeDtypeStruct(out_buf.shape, out_buf.dtype), window, window),
        grid_spec=grid_spec,
        input_output_aliases={6: 0},
        compiler_params=_cparams(("parallel",)),
        name="attention_sample",
    )(sinks, q_all, cache_k, cache_v, k_all, v_all, out_buf)


ROUTE_COLS = 8
ROUTE_ROWS = 48


def _route_kernel(*refs, parts):
    i = pl.program_id(0)
    if parts:
        (ap_ref, as_ref, at_ref, w_ref, xp_ref, xs_ref, xt_ref), refs = refs[:7], refs[7:]
        a = _pick_part(i, ap_ref, as_ref, at_ref, BF16)
        x = _pick_part(i, xp_ref, xs_ref, xt_ref, F32)
    else:
        (a_ref, w_ref, x_ref), refs = refs[:3], refs[3:]
        a, x = a_ref[...], x_ref[...]
    g_ref, wh_ref, wl_ref, br_ref, utri_ref, h_ref, xn_ref, rec_ref, rect_ref, cnt_ref, cnt_scr = refs

    @pl.when(i == 0)
    def _():
        cnt_scr[...] = jnp.zeros_like(cnt_scr)

    h = x + jnp.dot(a, w_ref[...], preferred_element_type=F32)
    h_ref[...] = h
    xn = _rms(h, g_ref[...])
    xn_ref[...] = _pack_halves(xn)
    xh = xn.astype(BF16)
    xl = (xn - xh.astype(F32)).astype(BF16)
    logits = (_nt_dot(wh_ref[...], xh) + (_nt_dot(wl_ref[...], xh) + _nt_dot(wh_ref[...], xl)))[:ROUTE_ROWS]
    logits = logits + br_ref[...]
    tokens = logits.shape[1]
    rid = lax.broadcasted_iota(jnp.int32, (ROUTE_ROWS, tokens), 0).astype(F32)
    neg = jnp.float32(-jnp.inf)
    big = jnp.float32(ROUTE_ROWS)

    is_g = (rid >= N_EXPERTS) & (rid < N_EXPERTS + N_GROUPS)
    gl = jnp.where(is_g, logits, neg)
    gmax = jnp.max(gl, axis=0, keepdims=True)
    gsel = jnp.min(jnp.where(gl == gmax, rid, big), axis=0, keepdims=True) - N_EXPERTS
    gden = jnp.sum(jnp.where(is_g, jnp.exp(gl - gmax), 0.0), axis=0, keepdims=True)
    gw = 1.0 / gden

    in_grp = (rid >= gsel * EXPERTS_PER_GROUP) & (rid < (gsel + 1) * EXPERTS_PER_GROUP)
    el = jnp.where(in_grp, logits, neg)
    t1 = jnp.max(el, axis=0, keepdims=True)
    e1 = jnp.min(jnp.where(el == t1, rid, big), axis=0, keepdims=True)
    el2 = jnp.where(rid == e1, neg, el)
    t2 = jnp.max(el2, axis=0, keepdims=True)
    e2 = jnp.min(jnp.where(el2 == t2, rid, big), axis=0, keepdims=True)
    x2 = jnp.exp(t2 - t1)
    w1 = gw / (1.0 + x2)
    w2 = gw * x2 / (1.0 + x2)

    oh1 = (rid == e1).astype(F32)
    oh2 = (rid == e2).astype(F32)
    oh = oh1 + oh2
    before = jnp.dot(oh.astype(BF16), utri_ref[...], preferred_element_type=F32)
    base = cnt_scr[...] + before
    r1 = jnp.sum(base * oh1, axis=0, keepdims=True)
    r2 = jnp.sum(base * oh2, axis=0, keepdims=True)
    cnt_scr[...] = cnt_scr[...] + jnp.sum(oh, axis=1, keepdims=True)

    zero = jnp.zeros_like(w1)
    rect = jnp.concatenate([e1, e2, r1, r2, w1, w2, zero, zero], axis=0)
    rect_ref[...] = rect
    wide = jnp.concatenate([rect, jnp.zeros((LANES - ROUTE_COLS, tokens), F32)], axis=0)
    rec_ref[...] = jnp.concatenate([wide[:, t0:t0 + LANES].T for t0 in range(0, tokens, LANES)], axis=0)
    cnt_ref[...] = cnt_scr[...]


def moe_route(a, w_out_bf16, x, gain, w_router, b_router, utri):
    parts = isinstance(a, tuple)
    t, d = T_ALL, D_MODEL
    row = lambda i: (i, 0)
    fix = lambda i: (0, 0)
    w_t = w_router.T
    w_hi = w_t.astype(BF16)
    w_lo = (w_t - w_hi.astype(F32)).astype(BF16)
    w_spec = pl.BlockSpec((d, d), fix)
    if parts:
        pre_specs = _parts_specs(d) + [w_spec] + _parts_specs(d)
        pre_args = (*a, w_out_bf16, *x)
    else:
        pre_specs = [pl.BlockSpec((ROW_TILE, d), row), w_spec, pl.BlockSpec((ROW_TILE, d), row)]
        pre_args = (a, w_out_bf16, x)
    return pl.pallas_call(
        functools.partial(_route_kernel, parts=parts),
        out_shape=(jax.ShapeDtypeStruct((t, d), F32),
                   jax.ShapeDtypeStruct((t, d // 2), U32), jax.ShapeDtypeStruct((t, LANES), F32),
                   jax.ShapeDtypeStruct((ROUTE_COLS, t), F32), jax.ShapeDtypeStruct((ROUTE_ROWS, 1), F32)),
        grid=(t // ROW_TILE,),
        in_specs=pre_specs + [pl.BlockSpec((1, d), fix),
                              pl.BlockSpec((LANES, d), fix), pl.BlockSpec((LANES, d), fix),
                              pl.BlockSpec((ROUTE_ROWS, 1), fix), pl.BlockSpec((ROW_TILE, ROW_TILE), fix)],
        out_specs=(pl.BlockSpec((ROW_TILE, d), row),
                   pl.BlockSpec((ROW_TILE, d // 2), row), pl.BlockSpec((ROW_TILE, LANES), row),
                   pl.BlockSpec((ROUTE_COLS, ROW_TILE), lambda i: (0, i)), pl.BlockSpec((ROUTE_ROWS, 1), fix)),
        scratch_shapes=[pltpu.VMEM((ROUTE_ROWS, 1), F32)],
        compiler_params=_cparams(("arbitrary",)),
        name="moe_route",
    )(*pre_args, gain.reshape(1, d), w_hi, w_lo, b_router, utri)


SC_WINDOW = 64
SC_INDEX_WINDOW = 128


def _sc_mesh():
    return plsc.VectorSubcoreMesh(core_axis_name="core", subcore_axis_name="subcore")


def moe_dispatch_sc(xn, dest_a, dest_b, n_slots):
    t, d = xn.shape

    n_parts = SC_INDEX_WINDOW // SC_WINDOW

    @pl.kernel(out_type=jax.ShapeDtypeStruct((n_slots, d), xn.dtype), mesh=_sc_mesh(),
               scratch_types=[pltpu.VMEM((2, SC_WINDOW, d), xn.dtype), pltpu.SemaphoreType.DMA((2,)),
                              pltpu.SemaphoreType.DMA((2,))],
               name="moe_dispatch_sc")
    def run(x_hbm, id_hbm, da_hbm, db_hbm, o_hbm, buf, load_sem, store_sem):
        def body(id_vmem, da_vmem, db_vmem):
            part = lambda j: pl.ds(j * SC_WINDOW, SC_WINDOW)
            load = lambda j: pltpu.make_async_copy(x_hbm.at[id_vmem.at[0, part(j)]], buf.at[j % 2], load_sem.at[j % 2])
            load(0).start()
            for j in range(n_parts):
                load(j).wait()
                if j + 1 < n_parts:
                    load(j + 1).start()
                stores = [pltpu.make_async_copy(buf.at[j % 2], o_hbm.at[dv.at[0, part(j)]], store_sem.at[k])
                          for k, dv in enumerate((da_vmem, db_vmem))]
                for s in stores:
                    s.start()
                for s in stores:
                    s.wait()

        idx_spec = pl.BlockSpec((1, SC_INDEX_WINDOW), lambda i: (0, i))
        pltpu.emit_pipeline(
            body,
            grid=(t // SC_INDEX_WINDOW,),
            in_specs=[idx_spec, idx_spec, idx_spec],
            out_specs=[],
            core_axis_name=("core", "subcore"),
            dimension_semantics=(pltpu.PARALLEL,),
        )(id_hbm, da_hbm, db_hbm)

    return run(xn, jnp.arange(t, dtype=jnp.int32).reshape(1, t), dest_a, dest_b)


def moe_gather_sc(ys, dest_a, dest_b):
    d = ys.shape[1]
    t = dest_a.shape[1]
    out = jax.ShapeDtypeStruct((t, d), ys.dtype)

    n_moves = 2 * (SC_INDEX_WINDOW // SC_WINDOW)

    @pl.kernel(out_type=(out, out), mesh=_sc_mesh(),
               scratch_types=[pltpu.VMEM((2, SC_WINDOW, d), ys.dtype), pltpu.SemaphoreType.DMA((2,)),
                              pltpu.SemaphoreType.DMA((2,))],
               name="moe_gather_sc")
    def run(y_hbm, id_hbm, da_hbm, db_hbm, ga_hbm, gb_hbm, buf, load_sem, store_sem):
        def body(id_vmem, da_vmem, db_vmem):
            part = lambda m: pl.ds((m // 2) * SC_WINDOW, SC_WINDOW)
            src = lambda m: (da_vmem, db_vmem)[m % 2]
            dst = lambda m: (ga_hbm, gb_hbm)[m % 2]
            load = lambda m: pltpu.make_async_copy(y_hbm.at[src(m).at[0, part(m)]], buf.at[m % 2], load_sem.at[m % 2])
            store = lambda m: pltpu.make_async_copy(buf.at[m % 2], dst(m).at[id_vmem.at[0, part(m)]],
                                                    store_sem.at[m % 2])
            load(0).start()
            for m in range(n_moves):
                load(m).wait()
                if m >= 1:
                    store(m - 1).wait()
                if m + 1 < n_moves:
                    load(m + 1).start()
                store(m).start()
            store(n_moves - 1).wait()

        idx_spec = pl.BlockSpec((1, SC_INDEX_WINDOW), lambda i: (0, i))
        pltpu.emit_pipeline(
            body,
            grid=(t // SC_INDEX_WINDOW,),
            in_specs=[idx_spec, idx_spec, idx_spec],
            out_specs=[],
            core_axis_name=("core", "subcore"),
            dimension_semantics=(pltpu.PARALLEL,),
        )(id_hbm, da_hbm, db_hbm)

    return run(ys, jnp.arange(t, dtype=jnp.int32).reshape(1, t), dest_a, dest_b)


def _combine_dense_kernel(h_ref, rec_ref, ga_ref, gb_ref, *out_refs, split):
    i = pl.program_id(0)
    rec = rec_ref[...]
    res = h_ref[...] + rec[:, 4:5] * _unpack_halves(ga_ref[...]) + rec[:, 5:6] * _unpack_halves(gb_ref[...])
    if not split:
        out_refs[0][...] = res
    else:
        @pl.when(i < N_PROMPT_TILES)
        def _():
            out_refs[0][...] = res

        @pl.when((i >= N_PROMPT_TILES) & (i < N_PROMPT_TILES + N_SAMPLE_TILES))
        def _():
            out_refs[1][...] = res


def moe_combine_dense(h, rec, ga, gb, split=False):
    t, d = h.shape
    row = lambda i: (i, 0)
    if split:
        out_shape = (jax.ShapeDtypeStruct((T_PROMPT, d), F32), jax.ShapeDtypeStruct((T_SAMPLE, d), F32))
        out_specs = tuple(_parts_specs(d)[:2])
    else:
        out_shape = jax.ShapeDtypeStruct((t, d), F32)
        out_specs = pl.BlockSpec((ROW_TILE, d), row)
    return pl.pallas_call(
        functools.partial(_combine_dense_kernel, split=split),
        out_shape=out_shape,
        grid=(t // ROW_TILE,),
        in_specs=[pl.BlockSpec((ROW_TILE, d), row), pl.BlockSpec((ROW_TILE, LANES), row),
                  pl.BlockSpec((ROW_TILE, d // 2), row), pl.BlockSpec((ROW_TILE, d // 2), row)],
        out_specs=out_specs,
        compiler_params=_cparams(("arbitrary",)),
        name="moe_combine_dense",
    )(h, rec, ga, gb)


def _combine_kv_q_kernel(h_ref, rec_ref, ga_ref, gb_ref, gkv_ref, wkv_ref, gq_ref, wq_ref, hmean_ref,
                         hgk_ref, hgq_ref, cos_ref, sina_ref, sinb_ref, ho_ref, k_ref, v_ref, q_ref):
    rec = rec_ref[...]
    h = h_ref[...] + rec[:, 4:5] * _unpack_halves(ga_ref[...]) + rec[:, 5:6] * _unpack_halves(gb_ref[...])
    ho_ref[...] = h
    xhat = h * lax.rsqrt(jnp.mean(h * h, axis=-1, keepdims=True) + RMS_EPS)
    tables = (cos_ref[...], sina_ref[...], sinb_ref[...])
    zkv = jnp.dot((xhat * gkv_ref[...]).astype(BF16), wkv_ref[...], preferred_element_type=F32)
    k_ref[...] = _head_norm_rope(zkv[:, :KV_DIM], hmean_ref, hgk_ref[...], *tables)
    v_ref[...] = zkv[:, KV_DIM:]
    zq = jnp.dot((xhat * gq_ref[...]).astype(BF16), wq_ref[...], preferred_element_type=F32)
    q = _head_norm_rope(zq, hmean_ref, hgq_ref[...], *tables)
    q_ref[...] = (q * HEAD_DIM ** -0.5).astype(q_ref.dtype)


def moe_combine_kv_q(h, rec, ga, gb, kv_gain, kv_w_bf16, q_gain, wq_bf16, hmean, k_hgain, q_hgain,
                     cos_t, sina_t, sinb_t):
    t, d = h.shape
    row = lambda i: (i, 0)
    fix = lambda i: (0, 0)
    rope = pl.BlockSpec((ROW_TILE, LANES), _rope_tile)
    return pl.pallas_call(
        _combine_kv_q_kernel,
        out_shape=(jax.ShapeDtypeStruct((t, d), F32), jax.ShapeDtypeStruct((t, KV_DIM), F32),
                   jax.ShapeDtypeStruct((t, KV_DIM), F32), jax.ShapeDtypeStruct((t, d), BF16)),
        grid=(t // ROW_TILE,),
        in_specs=[pl.BlockSpec((ROW_TILE, d), row), pl.BlockSpec((ROW_TILE, LANES), row),
                  pl.BlockSpec((ROW_TILE, d // 2), row), pl.BlockSpec((ROW_TILE, d // 2), row),
                  pl.BlockSpec((1, d), fix), pl.BlockSpec((d, 2 * KV_DIM), fix),
                  pl.BlockSpec((1, d), fix), pl.BlockSpec((d, d), fix),
                  pl.BlockSpec((256, 256), fix), pl.BlockSpec((1, KV_DIM), fix), pl.BlockSpec((1, d), fix),
                  rope, rope, rope],
        out_specs=(pl.BlockSpec((ROW_TILE, d), row), pl.BlockSpec((ROW_TILE, KV_DIM), row),
                   pl.BlockSpec((ROW_TILE, KV_DIM), row), pl.BlockSpec((ROW_TILE, d), row)),
        compiler_params=_cparams(("parallel",)),
        name="moe_combine_kv_q",
    )(h, rec, ga, gb, kv_gain.reshape(1, d), kv_w_bf16, q_gain.reshape(1, d), wq_bf16, hmean, k_hgain, q_hgain,
      cos_t, sina_t, sinb_t)


def _ffn_kernel(wblk_ref, we_ref, wlo_ref, whi_ref, wnext_ref, wpar_ref, xs_ref, w13_ref, w2_ref, ys_ref,
                w13f, w2f, w13b, w2b, wsem, *, layer):
    w = pl.program_id(0)
    prev = jnp.maximum(w - 1, 0)
    first_visit = (w == 0) | (wblk_ref[w] != wblk_ref[prev])
    lo = wlo_ref[w]
    hi = whi_ref[w]

    def weight_copies(e, par):
        return (pltpu.make_async_copy(w13_ref.at[layer, e], w13f.at[par], wsem.at[par, 0]),
                pltpu.make_async_copy(w2_ref.at[layer, e], w2f.at[par], wsem.at[par, 1]))

    @pl.when(w == 0)
    def _():
        for c in weight_copies(we_ref[0], 0):
            c.start()

    def ffn(x):
        x = _unpack_halves(x).astype(BF16)
        cw = D_EXPERT // FFN_CHUNKS
        gate_up = []
        for c in range(FFN_CHUNKS):
            a = jnp.dot(x, w13b[:, c * cw:(c + 1) * cw], preferred_element_type=F32)
            u = jnp.dot(x, w13b[:, D_EXPERT + c * cw:D_EXPERT + (c + 1) * cw], preferred_element_type=F32)
            gate_up.append((a, u))
        hmid = jnp.concatenate([(_silu(a) * u).astype(BF16) for a, u in gate_up], axis=1)
        return _pack_halves(jnp.dot(hmid, w2b[...], preferred_element_type=F32))

    @pl.when(hi > lo)
    def _():
        @pl.when((w == 0) | (we_ref[w] != we_ref[prev]))
        def _():
            par = wpar_ref[w]
            for c in weight_copies(we_ref[w], par):
                c.wait()
            w13b[...] = w13f[par].astype(BF16)
            w2b[...] = w2f[par].astype(BF16)
            nxt = wnext_ref[w]

            @pl.when(nxt >= 0)
            def _():
                for c in weight_copies(nxt, 1 - par):
                    c.start()

        whole = (lo == 0) & (hi == EXPERT_BLOCK)

        @pl.when(whole)
        def _():
            ys_ref[...] = ffn(xs_ref[...])

        half = FFN_PART
        for p in range(EXPERT_BLOCK // FFN_PART):
            rows = slice(p * half, (p + 1) * half)
            touched = (lo < (p + 1) * half) & (hi > p * half)

            @pl.when(jnp.logical_not(whole) & touched)
            def _():
                y = ffn(xs_ref[rows, :])
                row = lax.broadcasted_iota(jnp.int32, y.shape, 0) + p * half
                mine = (row >= lo) & (row < hi)

                @pl.when(first_visit)
                def _():
                    ys_ref[rows, :] = jnp.where(mine, y, jnp.zeros_like(y))

                @pl.when(jnp.logical_not(first_visit))
                def _():
                    ys_ref[rows, :] = jnp.where(mine, y, ys_ref[rows, :])

            @pl.when(jnp.logical_not(whole) & jnp.logical_not(touched) & first_visit)
            def _():
                ys_ref[rows, :] = jnp.zeros((half, ys_ref.shape[1]), U32)


def moe_ffn(xs, work, w13_all, w2_all, layer):
    n_slots, dp = xs.shape
    d = 2 * dp
    n_work = work[0].shape[0]
    xmap = lambda w, *prefetch: (prefetch[0][w], 0)
    grid_spec = pltpu.PrefetchScalarGridSpec(
        num_scalar_prefetch=len(work),
        grid=(n_work,),
        in_specs=[pl.BlockSpec((EXPERT_BLOCK, dp), xmap),
                  pl.BlockSpec(memory_space=pl.ANY), pl.BlockSpec(memory_space=pl.ANY)],
        out_specs=pl.BlockSpec((EXPERT_BLOCK, dp), xmap),
        scratch_shapes=[pltpu.VMEM((2, d, 2 * D_EXPERT), F32), pltpu.VMEM((2, D_EXPERT, d), F32),
                        pltpu.VMEM((d, 2 * D_EXPERT), BF16), pltpu.VMEM((D_EXPERT, d), BF16),
                        pltpu.SemaphoreType.DMA((2, 2))],
    )
    return pl.pallas_call(
        functools.partial(_ffn_kernel, layer=layer),
        out_shape=jax.ShapeDtypeStruct((n_slots, dp), U32),
        grid_spec=grid_spec,
        compiler_params=_cparams(("arbitrary",)),
        name="moe_ffn",
    )(*work, xs, w13_all, w2_all)


def _ffn_work_items(cnt):
    n_slots = 2 * T_ALL
    n_blocks = n_slots // EXPERT_BLOCK
    n_work = n_blocks + N_EXPERTS - 1
    end = jnp.cumsum(cnt)
    start = end - cnt
    first_blk = start // EXPERT_BLOCK
    last_blk = jnp.maximum(end - 1, start) // EXPERT_BLOCK
    n_items = jnp.where(cnt > 0, last_blk - first_blk + 1, 0)
    item_end = jnp.cumsum(n_items)
    item_start = item_end - n_items
    w = jnp.arange(n_work, dtype=jnp.int32)
    used = w < item_end[-1]
    wq = jnp.minimum(w, item_end[-1] - 1)
    e = jnp.sum((item_end[:, None] <= wq[None, :]).astype(jnp.int32), axis=0)
    onehot = e[None, :] == jnp.arange(N_EXPERTS, dtype=jnp.int32)[:, None]
    of_e = lambda table: jnp.sum(jnp.where(onehot, table[:, None], 0), axis=0)
    blk = jnp.where(used, of_e(first_blk) + (w - of_e(item_start)), n_blocks - 1).astype(jnp.int32)
    lo = jnp.maximum(of_e(start), blk * EXPERT_BLOCK) - blk * EXPERT_BLOCK
    hi = jnp.minimum(of_e(end), (blk + 1) * EXPERT_BLOCK) - blk * EXPERT_BLOCK
    lo = jnp.where(used, lo, 0).astype(jnp.int32)
    hi = jnp.where(used, hi, 0).astype(jnp.int32)
    e_before = jnp.concatenate([jnp.full((1,), -1, jnp.int32), e[:-1]])
    change = used & (e != e_before)
    parity = ((jnp.cumsum(change.astype(jnp.int32)) - 1) % 2).astype(jnp.int32)
    far = jnp.int32(n_work)
    next_change = lax.cummin(jnp.where(change, w, far), axis=0, reverse=True)
    next_change = jnp.concatenate([next_change[1:], jnp.full((1,), far, jnp.int32)])
    e_next = jnp.sum(jnp.where(next_change[None, :] == w[:, None], e[:, None], 0), axis=0)
    e_next = jnp.where(next_change < far, e_next, -1).astype(jnp.int32)
    return start, (blk, e, lo, hi, e_next, parity)


def hier_moe_layer(a, w_out_bf16, x, layer, gain, w_group, b_group, w_expert, b_expert, w13_all, w2_all, utri,
                   finish):
    t = T_ALL
    pad = LANES - N_EXPERTS - N_GROUPS
    w_router = jnp.concatenate([w_expert, w_group, jnp.zeros((D_MODEL, pad), F32)], axis=1)
    b_router = jnp.concatenate([b_expert, b_group, jnp.zeros((pad,), F32)])[:ROUTE_ROWS].reshape(ROUTE_ROWS, 1)
    h, xn, rec, rect, counts = moe_route(a, w_out_bf16, x, gain, w_router, b_router, utri)

    cnt = counts[:N_EXPERTS, 0].astype(jnp.int32)
    start, work = _ffn_work_items(cnt)
    experts = jnp.arange(N_EXPERTS, dtype=jnp.int32)[:, None]

    def slot_of(e_row, rank_row):
        first = jnp.sum(jnp.where(e_row.astype(jnp.int32)[None, :] == experts, start[:, None], 0), axis=0)
        return (first + rank_row.astype(jnp.int32)).reshape(1, t)

    dest_a = slot_of(rect[0], rect[2])
    dest_b = slot_of(rect[1], rect[3])

    xs = moe_dispatch_sc(xn, dest_a, dest_b, 2 * t)
    ys = moe_ffn(xs, work, w13_all, w2_all, layer)
    ga, gb = moe_gather_sc(ys, dest_a, dest_b)
    return finish(h, rec, ga, gb)


def _rope_tables(pos):
    half = ROPE_DIM // 2
    lane = np.arange(LANES) % HEAD_DIM
    rotary = lane < ROPE_DIM
    inv = jnp.where(rotary, jnp.exp(-math.log(ROPE_THETA) * jnp.asarray(lane % half, F32) * (2.0 / ROPE_DIM)), 0.0)
    ang = pos.astype(F32)[:, None] * inv[None, :]
    cos, sin = jnp.cos(ang), jnp.sin(ang)
    first = jnp.asarray(lane < half)
    second = jnp.asarray(rotary & (lane >= half))
    return cos, jnp.where(first, -sin, 0.0), jnp.where(second, sin, 0.0)


def kernel(x_prompt, x_sample, state_hgrn, cache_k_win, cache_v_win, meta_tokens, a_norm, a_w_in, a_lower_logits, a_out_norm, a_w_out, kv_norm, kv_w, k_norm, b_norm, b_wq, b_q_norm, b_sinks, b_w_out, moe_norm, moe_w_group, moe_b_group, moe_w_expert, moe_b_expert, moe_w13, moe_w2):
    tail_rows = T_ALL - OFF_META
    x_parts = (x_prompt.reshape(T_PROMPT, D_MODEL), x_sample.reshape(T_SAMPLE, D_MODEL),
               jnp.concatenate([meta_tokens.astype(F32), jnp.zeros((tail_rows - N_META, D_MODEL), F32)], axis=0))
    pos = jnp.concatenate([N_META + jnp.arange(SEQ, dtype=jnp.int32),
                           jnp.tile(PAST_LEN + jnp.arange(DEC_SEQ, dtype=jnp.int32), ROW_TILE // DEC_SEQ),
                           jnp.arange(N_META, dtype=jnp.int32),
                           jnp.zeros((ROW_TILE - N_META,), jnp.int32)])
    cos_t, sina_t, sinb_t = _rope_tables(pos)
    r256 = np.arange(256)
    hmean = jnp.asarray((r256[:, None] // HEAD_DIM == r256[None, :] // HEAD_DIM).astype(np.float32) / HEAD_DIM, BF16)
    rt = np.arange(ROW_TILE)
    utri = jnp.asarray((rt[:, None] < rt[None, :]).astype(np.float32), BF16)
    lower = jnp.cumsum(jax.nn.softmax(a_lower_logits.astype(F32), axis=0), axis=0)

    moe = functools.partial(hier_moe_layer, w13_all=moe_w13, w2_all=moe_w2, utri=utri)

    z = in_project(x_parts, a_norm[0], a_w_in[0].astype(BF16))
    zero_state = jnp.zeros((1, A_HEADS, A_DK, A_DV), F32)
    o_meta, s_meta = hgrn2_scan(z, zero_state, lower[0], a_out_norm[0],
                                row_off=OFF_META, n_seq=1, seq_len=N_META)
    o_prompt, s_prompt = hgrn2_scan(z, s_meta, lower[0], a_out_norm[0], row_off=0, n_seq=BATCH, seq_len=SEQ)
    o_sample, s_sample = hgrn2_scan(z, state_hgrn[0].astype(F32), lower[0], a_out_norm[0],
                                    row_off=OFF_SAMPLE, n_seq=DEC_BATCH, seq_len=DEC_SEQ, group=SCAN_SAMPLE_GROUP)
    o_tail = jnp.concatenate([o_meta, jnp.zeros((tail_rows - N_META, D_MODEL), BF16)], axis=0)
    finish0 = functools.partial(
        moe_combine_kv_q, kv_gain=kv_norm, kv_w_bf16=kv_w.astype(BF16), q_gain=b_norm[0], wq_bf16=b_wq[0].astype(BF16),
        hmean=hmean, k_hgain=jnp.tile(k_norm, KV_HEADS).reshape(1, KV_DIM),
        q_hgain=jnp.tile(b_q_norm[0], Q_HEADS).reshape(1, D_MODEL), cos_t=cos_t, sina_t=sina_t, sinb_t=sinb_t)
    h, k_all, v_all, q_all = moe((o_prompt, o_sample, o_tail), a_w_out[0].astype(BF16), x_parts, 0, moe_norm[0],
                                 moe_w_group[0], moe_b_group[0], moe_w_expert[0], moe_b_expert[0], finish=finish0)

    meta_blk = lambda a: jnp.concatenate([jnp.zeros((ATT_BLOCK - N_META, KV_DIM), F32),
                                          a[OFF_META:OFF_META + N_META]], axis=0)
    sinks = b_sinks[0].astype(F32)
    att_all = attention_prompt(q_all, k_all, v_all, meta_blk(k_all), meta_blk(v_all), sinks)
    att_all, k_win_s, v_win_s = attention_sample(q_all, cache_k_win.astype(F32), cache_v_win.astype(F32),
                                                 k_all, v_all, sinks, att_all)
    y_p, y_s = moe(att_all, b_w_out[0].astype(BF16), h, 1, moe_norm[1], moe_w_group[1], moe_b_group[1],
                   moe_w_expert[1], moe_b_expert[1], finish=functools.partial(moe_combine_dense, split=True))

    y_prompt = y_p.reshape(BATCH, SEQ, D_MODEL)
    y_sample = y_s.reshape(DEC_BATCH, DEC_SEQ, D_MODEL)
    last = lambda a: jnp.stack([a[(b + 1) * SEQ - WINDOW:(b + 1) * SEQ] for b in range(BATCH)]).reshape(
        BATCH, WINDOW, KV_HEADS, HEAD_DIM)
    kp = last(k_all)
    vp = last(v_all)
    return (y_prompt, y_sample, s_prompt[None], s_sample[None], kp, vp, k_win_s, v_win_s)
```

```python
import functools
import math

import numpy as np
import jax
import jax.numpy as jnp
from jax import lax
from jax.experimental import pallas as pl
from jax.experimental.pallas import tpu as pltpu
from jax.experimental.pallas import tpu_sc as plsc

F32 = jnp.float32
BF16 = jnp.bfloat16
U32 = jnp.uint32

D_MODEL = 1024
BATCH = 4
SEQ = 4096
DEC_BATCH = 128
DEC_SEQ = 8
PAST_LEN = 8192
N_META = 16
A_HEADS = 8
A_DK = 128
A_DV = 128
Q_HEADS = 16
KV_HEADS = 4
HEAD_DIM = 64
KV_DIM = KV_HEADS * HEAD_DIM
WINDOW = 128
ROPE_DIM = 16
ROPE_THETA = 500000.0
N_GROUPS = 4
EXPERTS_PER_GROUP = 8
N_EXPERTS = 32
D_EXPERT = 512
RMS_EPS = 1e-6

LANES = 128
SUBLANES = 8
VMEM_LIMIT = 56 * 1024 * 1024

ROW_TILE = 512
T_PROMPT = BATCH * SEQ
T_SAMPLE = DEC_BATCH * DEC_SEQ
OFF_SAMPLE = T_PROMPT
OFF_META = T_PROMPT + T_SAMPLE
T_REAL = OFF_META + N_META
T_ALL = -(-T_REAL // ROW_TILE) * ROW_TILE
N_TILES = T_ALL // ROW_TILE

SCAN_CHUNK = 128
SCAN_SAMPLE_GROUP = 16
ATT_BLOCK = 128
EXPERT_BLOCK = 512
FFN_PART = 256
FFN_CHUNKS = 2


def _cparams(sem):
    return pltpu.CompilerParams(dimension_semantics=sem, vmem_limit_bytes=VMEM_LIMIT)


def _nt_dot(a, b):
    return lax.dot_general(a, b, (((1,), (1,)), ((), ())), preferred_element_type=F32)


def _rms(x, gain):
    ms = jnp.mean(x * x, axis=-1, keepdims=True)
    return x * lax.rsqrt(ms + RMS_EPS) * gain


def _silu(x):
    return x * jax.nn.sigmoid(x)


def _pack_halves(x):
    w = x.shape[1] // 2
    hi = lax.bitcast_convert_type(x[:, :w].astype(BF16).astype(F32), U32)
    lo = lax.bitcast_convert_type(x[:, w:].astype(BF16).astype(F32), U32)
    return hi | (lo >> 16)


def _unpack_halves(p):
    hi = lax.bitcast_convert_type(p & jnp.uint32(0xFFFF0000), F32)
    lo = lax.bitcast_convert_type(p << 16, F32)
    return jnp.concatenate([hi, lo], axis=1)


N_PROMPT_TILES = T_PROMPT // ROW_TILE
N_SAMPLE_TILES = T_SAMPLE // ROW_TILE


def _parts_specs(width):
    return [pl.BlockSpec((ROW_TILE, width), lambda i: (jnp.minimum(i, N_PROMPT_TILES - 1), 0)),
            pl.BlockSpec((ROW_TILE, width), lambda i: (jnp.clip(i - N_PROMPT_TILES, 0, N_SAMPLE_TILES - 1), 0)),
            pl.BlockSpec((ROW_TILE, width), lambda i: (0, 0))]


def _pick_part(i, p_ref, s_ref, t_ref, dtype):
    return jnp.where(i < N_PROMPT_TILES, p_ref[...].astype(dtype),
                     jnp.where(i < N_PROMPT_TILES + N_SAMPLE_TILES, s_ref[...].astype(dtype),
                               t_ref[...].astype(dtype)))


def _in_proj_kernel(xp_ref, xs_ref, xt_ref, g_ref, w_ref, o_ref):
    x = _pick_part(pl.program_id(0), xp_ref, xs_ref, xt_ref, F32)
    xn = _rms(x, g_ref[...])
    o_ref[...] = jnp.dot(xn.astype(BF16), w_ref[...], preferred_element_type=F32)


def in_project(x_parts, gain, w_bf16):
    d, n = w_bf16.shape
    return pl.pallas_call(
        _in_proj_kernel,
        out_shape=jax.ShapeDtypeStruct((T_ALL, n), F32),
        grid=(N_TILES,),
        in_specs=_parts_specs(d) + [pl.BlockSpec((1, d), lambda i: (0, 0)),
                                    pl.BlockSpec((d, n), lambda i: (0, 0))],
        out_specs=pl.BlockSpec((ROW_TILE, n), lambda i: (i, 0)),
        compiler_params=_cparams(("parallel",)),
        name="in_project",
    )(*x_parts, gain.reshape(1, d), w_bf16)


def _head_norm_rope(y, hmean_ref, hgain, cos_t, sina_t, sinb_t):
    rows, width = y.shape
    sq = (y * y).astype(BF16)
    parts = []
    for s in range(width // 256):
        parts.append(jnp.dot(sq[:, s * 256:(s + 1) * 256], hmean_ref[...], preferred_element_type=F32))
    ms = parts[0] if len(parts) == 1 else jnp.concatenate(parts, axis=1)
    yn = y * lax.rsqrt(ms + RMS_EPS) * hgain
    reps = width // LANES
    cos_w = jnp.concatenate([cos_t] * reps, axis=1)
    sina_w = jnp.concatenate([sina_t] * reps, axis=1)
    sinb_w = jnp.concatenate([sinb_t] * reps, axis=1)
    half = ROPE_DIM // 2
    nxt = pltpu.roll(yn, width - half, 1)
    prv = pltpu.roll(yn, half, 1)
    return yn * cos_w + nxt * sina_w + prv * sinb_w


def _rope_tile(i):
    tiles_per_seq = SEQ // ROW_TILE
    n_prompt_tiles = T_PROMPT // ROW_TILE
    n_sample_tiles = T_SAMPLE // ROW_TILE
    return (jnp.where(i < n_prompt_tiles, i % tiles_per_seq,
                      jnp.where(i < n_prompt_tiles + n_sample_tiles, tiles_per_seq, tiles_per_seq + 1)), 0)


def _scan_levels(c):
    levels = []
    m = c
    while m >= 2:
        levels.append(m)
        m //= 2
    return levels


LOG2E = 1.4426950408889634


def _scan_kernel(z_ref, s0_ref, lb_ref, og_ref, tri_ref, lmask_ref, sgn_ref, o_ref, sfin_ref, s_scr, b_scr,
                 *, rows, seq_len):
    c_idx = pl.program_id(1)
    levels = _scan_levels(seq_len)
    n_sub = rows // seq_len
    hk = A_HEADS * A_DK

    @pl.when(c_idx == 0)
    def _():
        s_scr[...] = s0_ref[...]

    sub = lax.broadcasted_iota(jnp.int32, (SUBLANES, LANES), 0)
    row = lax.broadcasted_iota(jnp.int32, (LANES, LANES), 0)
    og = og_ref[...]

    def pad_f32(x):
        if x.shape[0] == LANES:
            return x
        return jnp.concatenate([x, jnp.zeros((LANES - x.shape[0], x.shape[1]), x.dtype)], axis=0)

    def pad_rows(x):
        return pad_f32(x).astype(BF16)

    def cols(part, h):
        return slice(part * hk + h * LANES, part * hk + (h + 1) * LANES)

    def gates(h):
        lb = lb_ref[:, cols(0, h)]
        forget = lb + (1.0 - lb) * jax.nn.sigmoid(z_ref[:, cols(1, h)])
        logf = jnp.log(forget)
        hi = logf.astype(BF16).astype(F32)
        r1 = logf - hi
        mid = r1.astype(BF16).astype(F32)
        lo = r1 - mid
        cs = jnp.dot(tri_ref[...], pad_rows(jnp.concatenate([hi, mid, lo], axis=1)),
                     preferred_element_type=F32)
        b = (cs[:rows, :LANES] + cs[:rows, LANES:2 * LANES]) + cs[:rows, 2 * LANES:]
        b_scr[h] = b
        return _silu(z_ref[:, cols(0, h)]), 1.0 - forget, b

    def bref_for(h, m):
        b_rows = b_scr.at[h]
        half = m // 2
        pieces = []
        for g in range(rows // SUBLANES):
            base = g * SUBLANES
            if m >= SUBLANES:
                r = (base // m) * m + half - 1
                piece = jnp.broadcast_to(b_rows[r:r + 1, :], (SUBLANES, LANES))
            else:
                piece = jnp.broadcast_to(b_rows[base + half - 1:base + half, :], (SUBLANES, LANES))
                for blk in range(1, SUBLANES // m):
                    r = base + blk * m + half - 1
                    piece = jnp.where(sub >= blk * m,
                                      jnp.broadcast_to(b_rows[r:r + 1, :], (SUBLANES, LANES)), piece)
            pieces.append(piece)
        return pieces[0] if len(pieces) == 1 else jnp.concatenate(pieces, axis=0)

    heads = range(A_HEADS)
    qkb = [gates(h) for h in heads]
    att = [_nt_dot(pad_rows(qf), pad_rows(kf)) * lmask_ref[len(levels)] for qf, kf, _ in qkb]
    for li, m in enumerate(levels):
        for h in heads:
            qf, kf, b = qkb[h]
            sgn = sgn_ref[li]
            e = jnp.exp2((b - bref_for(h, m)) * sgn)
            w = pad_rows(jnp.where(sgn > 0, qf, kf) * e)
            att[h] = att[h] + _nt_dot(w, w) * lmask_ref[li]

    def finish(h):
        qf, kf, b = qkb[h]
        b_rows = b_scr.at[h]
        v_b = pad_rows(z_ref[:, cols(2, h)])
        o_intra = jnp.dot(att[h].astype(BF16), v_b, preferred_element_type=F32)
        eb = jnp.exp(b)
        qs = qf * eb
        b_end = [jnp.broadcast_to(b_rows[(i + 1) * seq_len - 1:(i + 1) * seq_len, :], (seq_len, LANES))
                 for i in range(n_sub)]
        b_end = b_end[0] if n_sub == 1 else jnp.concatenate(b_end, axis=0)
        kd_t = pad_f32(kf * jnp.exp(b_end - b)).T.astype(BF16)
        eb_t = pad_f32(eb).T
        qs_b = pad_rows(qs)
        o = o_intra
        for i in range(n_sub):
            s_old = s_scr[i, h]
            first, last = i * seq_len, (i + 1) * seq_len - 1
            if n_sub == 1:
                qs_i, v_i = qs_b, v_b
            else:
                mine = (row >= first) & (row <= last)
                qs_i = jnp.where(mine, qs_b, jnp.zeros_like(qs_b))
                v_i = jnp.where(mine, v_b, jnp.zeros_like(v_b))
            o = o + jnp.dot(qs_i, s_old.astype(BF16), preferred_element_type=F32)
            decay = jnp.broadcast_to(eb_t[:, last:last + 1], (LANES, LANES))
            s_scr[i, h] = decay * s_old + jnp.dot(kd_t, v_i, preferred_element_type=F32)
        o = o[:rows]

        on = _rms(o, og) * _silu(z_ref[:, cols(3, h)])
        o_ref[:, cols(0, h)] = on.astype(o_ref.dtype)

    for h in heads:
        finish(h)

    @pl.when(c_idx == pl.num_programs(1) - 1)
    def _():
        sfin_ref[...] = s_scr[...]


def _scan_consts(rows, seq_len):
    levels = _scan_levels(seq_len)
    r = np.arange(LANES)
    t, s = r[:, None], r[None, :]
    live = (t < rows) & (s < rows)
    tri = ((s <= t) & (t // seq_len == s // seq_len) & live).astype(np.float32)
    masks, sgns = [], []
    for m in levels:
        masks.append(((t // m == s // m) & (t % m >= m // 2) & (s % m < m // 2) & live).astype(np.float32))
        sgns.append(np.broadcast_to(np.where(r[:rows, None] % m >= m // 2, LOG2E, -LOG2E), (rows, LANES)))
    masks.append(((t == s) & live).astype(np.float32))
    return jnp.asarray(tri, BF16), jnp.asarray(np.stack(masks), F32), jnp.asarray(np.stack(sgns), F32)


def hgrn2_scan(z, s0, lb, o_gain, *, row_off, n_seq, seq_len, group=1):
    hv = A_HEADS * A_DV
    if seq_len > SCAN_CHUNK:
        assert group == 1
        sub_len, rows, n_chunks, n_steps = SCAN_CHUNK, SCAN_CHUNK, seq_len // SCAN_CHUNK, n_seq
    else:
        sub_len, rows, n_chunks, n_steps = seq_len, group * seq_len, 1, n_seq // group
    blk_off = row_off // rows
    tri, lmask, sgn = _scan_consts(rows, sub_len)
    shared_s0 = s0.shape[0] == 1
    fix2 = lambda s, c: (0, 0)
    fix3 = lambda s, c: (0, 0, 0)
    o, sfin = pl.pallas_call(
        functools.partial(_scan_kernel, rows=rows, seq_len=sub_len),
        out_shape=(jax.ShapeDtypeStruct((n_seq * seq_len, hv), BF16 if rows % 16 == 0 else F32),
                   jax.ShapeDtypeStruct((n_seq, A_HEADS, A_DK, A_DV), F32)),
        grid=(n_steps, n_chunks),
        in_specs=[pl.BlockSpec((rows, 4 * hv), lambda s, c: (blk_off + s * n_chunks + c, 0)),
                  pl.BlockSpec((group, A_HEADS, A_DK, A_DV), (lambda s, c: (0, 0, 0, 0)) if shared_s0
                               else (lambda s, c: (s, 0, 0, 0))),
                  pl.BlockSpec((1, hv), fix2), pl.BlockSpec((1, A_DV), fix2),
                  pl.BlockSpec((LANES, LANES), fix2), pl.BlockSpec(lmask.shape, fix3),
                  pl.BlockSpec(sgn.shape, fix3)],
        out_specs=(pl.BlockSpec((rows, hv), lambda s, c: (s * n_chunks + c, 0)),
                   pl.BlockSpec((group, A_HEADS, A_DK, A_DV), lambda s, c: (s, 0, 0, 0))),
        scratch_shapes=[pltpu.VMEM((group, A_HEADS, A_DK, A_DV), F32), pltpu.VMEM((A_HEADS, rows, LANES), F32)],
        compiler_params=_cparams(("parallel", "arbitrary")),
        name=f"hgrn2_scan_r{rows}",
    )(z, s0, lb.reshape(1, hv), o_gain.reshape(1, A_DV), tri, lmask, sgn)
    return o, sfin


KEYS = 2 * ATT_BLOCK
ATT_STEP = 1


def _pair_operand(x, kh):
    slab = x[:, (kh // 2) * LANES:(kh // 2 + 1) * LANES]
    lane = lax.broadcasted_iota(jnp.int32, slab.shape, 1)
    if kh % 2 == 0:
        lo = jnp.where(lane < HEAD_DIM, slab, 0.0)
        hi = pltpu.roll(lo, HEAD_DIM, 1)
    else:
        hi = jnp.where(lane >= HEAD_DIM, slab, 0.0)
        lo = pltpu.roll(hi, HEAD_DIM, 1)
    return jnp.concatenate([lo, hi], axis=0).astype(BF16)


def _window_bias(rows, jmin):
    t_i = lax.broadcasted_iota(jnp.int32, (rows, 2 * KEYS), 0)
    c_i = lax.broadcasted_iota(jnp.int32, (rows, 2 * KEYS), 1)
    j_i = c_i & (ATT_BLOCK - 1)
    own = (c_i & ATT_BLOCK) != 0
    ok = (own & (j_i <= t_i)) | (jnp.logical_not(own) & (j_i >= t_i) & (j_i >= jmin))
    return jnp.where(ok, 0.0, -jnp.inf).astype(F32)


def _pair_softmax(s, sink_a, sink_b):
    probs, rinv = [], []
    for hh, sink in enumerate((sink_a, sink_b)):
        sh = s[:, hh * KEYS:(hh + 1) * KEYS]
        m = jnp.maximum(jnp.max(sh, axis=-1, keepdims=True), sink)
        p = jnp.exp(sh - m)
        den = jnp.sum(p, axis=-1, keepdims=True) + jnp.exp(sink - m)
        probs.append(p.astype(BF16))
        rinv.append(1.0 / den)
    lane = lax.broadcasted_iota(jnp.int32, (s.shape[0], LANES), 1)
    return jnp.concatenate(probs, axis=1), jnp.where(lane < HEAD_DIM, rinv[0], rinv[1])


def _attn_prompt_kernel(sink_ref, q_ref, kp_ref, ko_ref, vp_ref, vo_ref, km_ref, vm_ref, o_ref,
                        k2_scr, v2_scr, s_scr, p_scr, r_scr, bias_scr):
    n = pl.program_id(0)
    steps_per_seq = SEQ // (ATT_STEP * ATT_BLOCK)
    n_pairs = Q_HEADS // 2
    blk = ATT_BLOCK

    @pl.when(n == 0)
    def _():
        bias_scr[0] = _window_bias(ATT_BLOCK, 0)
        bias_scr[1] = _window_bias(ATT_BLOCK, ATT_BLOCK - N_META)

    @pl.when(n >= BATCH * steps_per_seq)
    def _():
        o_ref[...] = jnp.zeros_like(o_ref)

    @pl.when(n < BATCH * steps_per_seq)
    def _():
        first = (n % steps_per_seq) == 0
        own_k, own_v = ko_ref[...], vo_ref[...]
        keys = [jnp.where(first, km_ref[...], kp_ref[...])] + [own_k[u * blk:(u + 1) * blk] for u in range(ATT_STEP)]
        vals = [jnp.where(first, vm_ref[...], vp_ref[...])] + [own_v[u * blk:(u + 1) * blk] for u in range(ATT_STEP)]
        biases = [bias_scr[first.astype(jnp.int32)]] + [bias_scr[0]] * (ATT_STEP - 1)
        for u in range(ATT_STEP):
            k = jnp.concatenate(keys[u:u + 2], axis=0)
            v = jnp.concatenate(vals[u:u + 2], axis=0)
            for kh in range(KV_HEADS):
                k2_scr[u, kh] = _pair_operand(k, kh)
                v2_scr[u, kh] = _pair_operand(v, kh)
        units = [(u, pair) for u in range(ATT_STEP) for pair in range(n_pairs)]
        rows = lambda u: slice(u * blk, (u + 1) * blk)
        lanes = lambda pair: slice(pair * LANES, (pair + 1) * LANES)
        for j, (u, pair) in enumerate(units):
            s_scr[j] = _nt_dot(q_ref[rows(u), lanes(pair)], k2_scr[u, pair // 2]) + biases[u]
        for j, (u, pair) in enumerate(units):
            p, rinv = _pair_softmax(s_scr[j], sink_ref[2 * pair], sink_ref[2 * pair + 1])
            p_scr[j] = p
            r_scr[j] = rinv
        for j, (u, pair) in enumerate(units):
            o = jnp.dot(p_scr[j], v2_scr[u, pair // 2], preferred_element_type=F32) * r_scr[j]
            o_ref[rows(u), lanes(pair)] = o.astype(o_ref.dtype)


def attention_prompt(q_all, k_all, v_all, k_meta_blk, v_meta_blk, sinks):
    step_rows = ATT_STEP * ATT_BLOCK
    n_prompt_steps = T_PROMPT // step_rows
    n_steps = T_ALL // step_rows
    n_units = ATT_STEP * (Q_HEADS // 2)
    own = lambda n, sk: (jnp.minimum(n, n_prompt_steps - 1), 0)
    prev = lambda n, sk: (jnp.maximum(ATT_STEP * jnp.minimum(n, n_prompt_steps - 1) - 1, 0), 0)
    fix = lambda n, sk: (0, 0)
    grid_spec = pltpu.PrefetchScalarGridSpec(
        num_scalar_prefetch=1,
        grid=(n_steps,),
        in_specs=[pl.BlockSpec((step_rows, D_MODEL), own),
                  pl.BlockSpec((ATT_BLOCK, KV_DIM), prev), pl.BlockSpec((step_rows, KV_DIM), own),
                  pl.BlockSpec((ATT_BLOCK, KV_DIM), prev), pl.BlockSpec((step_rows, KV_DIM), own),
                  pl.BlockSpec((ATT_BLOCK, KV_DIM), fix), pl.BlockSpec((ATT_BLOCK, KV_DIM), fix)],
        out_specs=pl.BlockSpec((step_rows, D_MODEL), lambda n, sk: (n, 0)),
        scratch_shapes=[pltpu.VMEM((ATT_STEP, KV_HEADS, 2 * KEYS, LANES), BF16),
                        pltpu.VMEM((ATT_STEP, KV_HEADS, 2 * KEYS, LANES), BF16),
                        pltpu.VMEM((n_units, ATT_BLOCK, 2 * KEYS), F32),
                        pltpu.VMEM((n_units, ATT_BLOCK, 2 * KEYS), BF16),
                        pltpu.VMEM((n_units, ATT_BLOCK, LANES), F32),
                        pltpu.VMEM((2, ATT_BLOCK, 2 * KEYS), F32)],
    )
    return pl.pallas_call(
        _attn_prompt_kernel,
        out_shape=jax.ShapeDtypeStruct((T_ALL, D_MODEL), BF16),
        grid_spec=grid_spec,
        compiler_params=_cparams(("arbitrary",)),
        name="attention_prompt",
    )(sinks, q_all, k_all, k_all, v_all, v_all, k_meta_blk, v_meta_blk)


SAMPLE_GROUP = ATT_BLOCK // DEC_SEQ


def _attn_sample_kernel(sink_ref, q_ref, ck_ref, cv_ref, kn_ref, vn_ref, buf_ref, o_ref, kw_ref, vw_ref,
                        qf_scr, of_scr, k2_scr, v2_scr):
    del buf_ref
    qrows = 2 * DEC_SEQ
    qf_scr[...] = q_ref[...].astype(F32)
    bias = _window_bias(qrows, 0)
    zq = jnp.zeros((qrows - DEC_SEQ, D_MODEL), F32)
    zk = jnp.zeros((ATT_BLOCK - DEC_SEQ, KV_DIM), F32)

    n_pairs = Q_HEADS // 2
    lanes_of = lambda pair: slice(pair * LANES, (pair + 1) * LANES)

    def seq_pair_body(it, carry):
        seqs = (2 * it, 2 * it + 1)
        r_new = [pl.multiple_of(i * DEC_SEQ, DEC_SEQ) for i in seqs]
        qs = []
        for u, i in enumerate(seqs):
            qs.append(jnp.concatenate([qf_scr[pl.ds(r_new[u], DEC_SEQ), :], zq], axis=0).astype(BF16))
            old = lambda c_ref: jnp.concatenate([c_ref[i, :, kh, :] for kh in range(KV_HEADS)], axis=1)
            k = jnp.concatenate([old(ck_ref), kn_ref[pl.ds(r_new[u], DEC_SEQ), :], zk], axis=0)
            v = jnp.concatenate([old(cv_ref), vn_ref[pl.ds(r_new[u], DEC_SEQ), :], zk], axis=0)
            for c_ref, n_ref, w_ref in ((ck_ref, kn_ref, kw_ref), (cv_ref, vn_ref, vw_ref)):
                w_ref[i, 0:WINDOW - DEC_SEQ] = c_ref[i, DEC_SEQ:WINDOW]
                for kh in range(KV_HEADS):
                    w_ref[i, WINDOW - DEC_SEQ:WINDOW, kh, :] = n_ref[pl.ds(r_new[u], DEC_SEQ),
                                                                     kh * HEAD_DIM:(kh + 1) * HEAD_DIM]
            for kh in range(KV_HEADS):
                k2_scr[u, kh] = _pair_operand(k, kh)
                v2_scr[u, kh] = _pair_operand(v, kh)
        scores = [[_nt_dot(qs[u][:, lanes_of(pair)], k2_scr[u, pair // 2]) + bias for pair in range(n_pairs)]
                  for u in range(2)]
        soft = [[_pair_softmax(s, sink_ref[2 * pair], sink_ref[2 * pair + 1]) for pair, s in enumerate(scores[u])]
                for u in range(2)]
        for u in range(2):
            for pair, (p, rinv) in enumerate(soft[u]):
                o = jnp.dot(p, v2_scr[u, pair // 2], preferred_element_type=F32) * rinv
                of_scr[pl.ds(r_new[u], DEC_SEQ), lanes_of(pair)] = o[:DEC_SEQ]
        return carry

    lax.fori_loop(0, SAMPLE_GROUP // 2, seq_pair_body, 0)
    o_ref[...] = of_scr[...].astype(o_ref.dtype)


def attention_sample(q_all, cache_k, cache_v, k_all, v_all, sinks, out_buf):
    first_blk = OFF_SAMPLE // ATT_BLOCK
    new = lambda g, sk: (first_blk + g, 0)
    old = pl.BlockSpec((SAMPLE_GROUP, WINDOW, KV_HEADS, HEAD_DIM), lambda g, sk: (g, 0, 0, 0))
    grid_spec = pltpu.PrefetchScalarGridSpec(
        num_scalar_prefetch=1,
        grid=(DEC_BATCH // SAMPLE_GROUP,),
        in_specs=[pl.BlockSpec((ATT_BLOCK, D_MODEL), new),
                  old, old,
                  pl.BlockSpec((ATT_BLOCK, KV_DIM), new), pl.BlockSpec((ATT_BLOCK, KV_DIM), new),
                  pl.BlockSpec(memory_space=pl.ANY)],
        out_specs=(pl.BlockSpec((ATT_BLOCK, D_MODEL), new), old, old),
        scratch_shapes=[pltpu.VMEM((ATT_BLOCK, D_MODEL), F32), pltpu.VMEM((ATT_BLOCK, D_MODEL), F32),
                        pltpu.VMEM((2, KV_HEADS, 2 * KEYS, LANES), BF16),
                        pltpu.VMEM((2, KV_HEADS, 2 * KEYS, LANES), BF16)],
    )
    window = jax.ShapeDtypeStruct(cache_k.shape, cache_k.dtype)
    return pl.pallas_call(
        _attn_sample_kernel,
        out_shape=(jax.ShapeDtypeStruct(out_buf.shape, out_buf.dtype), window, window),
        grid_spec=grid_spec,
        input_output_aliases={6: 0},
        compiler_params=_cparams(("parallel",)),
        name="attention_sample",
    )(sinks, q_all, cache_k, cache_v, k_all, v_all, out_buf)


ROUTE_COLS = 8
ROUTE_ROWS = 48


def _route_kernel(*refs, parts):
    i = pl.program_id(0)
    if parts:
        (ap_ref, as_ref, at_ref, w_ref, xp_ref, xs_ref, xt_ref), refs = refs[:7], refs[7:]
        a = _pick_part(i, ap_ref, as_ref, at_ref, BF16)
        x = _pick_part(i, xp_ref, xs_ref, xt_ref, F32)
    else:
        (a_ref, w_ref, x_ref), refs = refs[:3], refs[3:]
        a, x = a_ref[...], x_ref[...]
    g_ref, whl_ref, br_ref, utri_ref, h_ref, xn_ref, rec_ref, rect_ref, cnt_ref, cnt_scr = refs

    @pl.when(i == 0)
    def _():
        cnt_scr[...] = jnp.zeros_like(cnt_scr)

    h = x + jnp.dot(a, w_ref[...], preferred_element_type=F32)
    h_ref[...] = h
    xn = _rms(h, g_ref[...])
    xn_ref[...] = _pack_halves(xn)
    xh = xn.astype(BF16)
    xl = (xn - xh.astype(F32)).astype(BF16)
    by_xh = _nt_dot(whl_ref[...], xh)
    logits = (by_xh[:LANES] + (by_xh[LANES:] + _nt_dot(whl_ref[:LANES, :], xl)))[:ROUTE_ROWS]
    logits = logits + br_ref[...]
    tokens = logits.shape[1]
    rid = lax.broadcasted_iota(jnp.int32, (ROUTE_ROWS, tokens), 0).astype(F32)
    neg = jnp.float32(-jnp.inf)
    big = jnp.float32(ROUTE_ROWS)

    is_g = (rid >= N_EXPERTS) & (rid < N_EXPERTS + N_GROUPS)
    gl = jnp.where(is_g, logits, neg)
    gmax = jnp.max(gl, axis=0, keepdims=True)
    gsel = jnp.min(jnp.where(gl == gmax, rid, big), axis=0, keepdims=True) - N_EXPERTS
    gden = jnp.sum(jnp.where(is_g, jnp.exp(gl - gmax), 0.0), axis=0, keepdims=True)
    gw = 1.0 / gden

    in_grp = (rid >= gsel * EXPERTS_PER_GROUP) & (rid < (gsel + 1) * EXPERTS_PER_GROUP)
    el = jnp.where(in_grp, logits, neg)
    t1 = jnp.max(el, axis=0, keepdims=True)
    e1 = jnp.min(jnp.where(el == t1, rid, big), axis=0, keepdims=True)
    el2 = jnp.where(rid == e1, neg, el)
    t2 = jnp.max(el2, axis=0, keepdims=True)
    e2 = jnp.min(jnp.where(el2 == t2, rid, big), axis=0, keepdims=True)
    x2 = jnp.exp(t2 - t1)
    w1 = gw / (1.0 + x2)
    w2 = gw * x2 / (1.0 + x2)

    oh1 = (rid == e1).astype(F32)
    oh2 = (rid == e2).astype(F32)
    oh = oh1 + oh2
    before = jnp.dot(oh.astype(BF16), utri_ref[...], preferred_element_type=F32)
    base = cnt_scr[...] + before
    r1 = jnp.sum(base * oh1, axis=0, keepdims=True)
    r2 = jnp.sum(base * oh2, axis=0, keepdims=True)
    cnt_scr[...] = cnt_scr[...] + jnp.sum(oh, axis=1, keepdims=True)

    zero = jnp.zeros_like(w1)
    rect = jnp.concatenate([e1, e2, r1, r2, w1, w2, zero, zero], axis=0)
    rect_ref[...] = rect
    wide = jnp.concatenate([rect, jnp.zeros((LANES - ROUTE_COLS, tokens), F32)], axis=0)
    rec_ref[...] = jnp.concatenate([wide[:, t0:t0 + LANES].T for t0 in range(0, tokens, LANES)], axis=0)
    cnt_ref[...] = cnt_scr[...]


def moe_route(a, w_out_bf16, x, gain, w_router, b_router, utri):
    parts = isinstance(a, tuple)
    t, d = T_ALL, D_MODEL
    row = lambda i: (i, 0)
    fix = lambda i: (0, 0)
    w_t = w_router.T
    w_hi = w_t.astype(BF16)
    w_lo = (w_t - w_hi.astype(F32)).astype(BF16)
    w_spec = pl.BlockSpec((d, d), fix)
    if parts:
        pre_specs = _parts_specs(d) + [w_spec] + _parts_specs(d)
        pre_args = (*a, w_out_bf16, *x)
    else:
        pre_specs = [pl.BlockSpec((ROW_TILE, d), row), w_spec, pl.BlockSpec((ROW_TILE, d), row)]
        pre_args = (a, w_out_bf16, x)
    return pl.pallas_call(
        functools.partial(_route_kernel, parts=parts),
        out_shape=(jax.ShapeDtypeStruct((t, d), F32),
                   jax.ShapeDtypeStruct((t, d // 2), U32), jax.ShapeDtypeStruct((t, LANES), F32),
                   jax.ShapeDtypeStruct((ROUTE_COLS, t), F32), jax.ShapeDtypeStruct((ROUTE_ROWS, 1), F32)),
        grid=(t // ROW_TILE,),
        in_specs=pre_specs + [pl.BlockSpec((1, d), fix), pl.BlockSpec((2 * LANES, d), fix),
                              pl.BlockSpec((ROUTE_ROWS, 1), fix), pl.BlockSpec((ROW_TILE, ROW_TILE), fix)],
        out_specs=(pl.BlockSpec((ROW_TILE, d), row),
                   pl.BlockSpec((ROW_TILE, d // 2), row), pl.BlockSpec((ROW_TILE, LANES), row),
                   pl.BlockSpec((ROUTE_COLS, ROW_TILE), lambda i: (0, i)), pl.BlockSpec((ROUTE_ROWS, 1), fix)),
        scratch_shapes=[pltpu.VMEM((ROUTE_ROWS, 1), F32)],
        compiler_params=_cparams(("arbitrary",)),
        name="moe_route",
    )(*pre_args, gain.reshape(1, d), jnp.concatenate([w_hi, w_lo], axis=0), b_router, utri)


SC_WINDOW = 64
SC_INDEX_WINDOW = 128


def _sc_mesh():
    return plsc.VectorSubcoreMesh(core_axis_name="core", subcore_axis_name="subcore")


def moe_dispatch_sc(xn, dest_a, dest_b, n_slots):
    t, d = xn.shape

    n_parts = SC_INDEX_WINDOW // SC_WINDOW

    @pl.kernel(out_type=jax.ShapeDtypeStruct((n_slots, d), xn.dtype), mesh=_sc_mesh(),
               scratch_types=[pltpu.VMEM((2, SC_WINDOW, d), xn.dtype), pltpu.SemaphoreType.DMA((2,)),
                              pltpu.SemaphoreType.DMA((2,))],
               name="moe_dispatch_sc")
    def run(x_hbm, id_hbm, da_hbm, db_hbm, o_hbm, buf, load_sem, store_sem):
        def body(id_vmem, da_vmem, db_vmem):
            part = lambda j: pl.ds(j * SC_WINDOW, SC_WINDOW)
            load = lambda j: pltpu.make_async_copy(x_hbm.at[id_vmem.at[0, part(j)]], buf.at[j % 2], load_sem.at[j % 2])
            load(0).start()
            for j in range(n_parts):
                load(j).wait()
                if j + 1 < n_parts:
                    load(j + 1).start()
                stores = [pltpu.make_async_copy(buf.at[j % 2], o_hbm.at[dv.at[0, part(j)]], store_sem.at[k])
                          for k, dv in enumerate((da_vmem, db_vmem))]
                for s in stores:
                    s.start()
                for s in stores:
                    s.wait()

        idx_spec = pl.BlockSpec((1, SC_INDEX_WINDOW), lambda i: (0, i))
        pltpu.emit_pipeline(
            body,
            grid=(t // SC_INDEX_WINDOW,),
            in_specs=[idx_spec, idx_spec, idx_spec],
            out_specs=[],
            core_axis_name=("core", "subcore"),
            dimension_semantics=(pltpu.PARALLEL,),
        )(id_hbm, da_hbm, db_hbm)

    return run(xn, jnp.arange(t, dtype=jnp.int32).reshape(1, t), dest_a, dest_b)


def moe_gather_sc(ys, dest_a, dest_b):
    d = ys.shape[1]
    t = dest_a.shape[1]
    out = jax.ShapeDtypeStruct((t, d), ys.dtype)

    n_moves = 2 * (SC_INDEX_WINDOW // SC_WINDOW)

    @pl.kernel(out_type=(out, out), mesh=_sc_mesh(),
               scratch_types=[pltpu.VMEM((2, SC_WINDOW, d), ys.dtype), pltpu.SemaphoreType.DMA((2,)),
                              pltpu.SemaphoreType.DMA((2,))],
               name="moe_gather_sc")
    def run(y_hbm, id_hbm, da_hbm, db_hbm, ga_hbm, gb_hbm, buf, load_sem, store_sem):
        def body(id_vmem, da_vmem, db_vmem):
            part = lambda m: pl.ds((m // 2) * SC_WINDOW, SC_WINDOW)
            src = lambda m: (da_vmem, db_vmem)[m % 2]
            dst = lambda m: (ga_hbm, gb_hbm)[m % 2]
            load = lambda m: pltpu.make_async_copy(y_hbm.at[src(m).at[0, part(m)]], buf.at[m % 2], load_sem.at[m % 2])
            store = lambda m: pltpu.make_async_copy(buf.at[m % 2], dst(m).at[id_vmem.at[0, part(m)]],
                                                    store_sem.at[m % 2])
            load(0).start()
            for m in range(n_moves):
                load(m).wait()
                if m >= 1:
                    store(m - 1).wait()
                if m + 1 < n_moves:
                    load(m + 1).start()
                store(m).start()
            store(n_moves - 1).wait()

        idx_spec = pl.BlockSpec((1, SC_INDEX_WINDOW), lambda i: (0, i))
        pltpu.emit_pipeline(
            body,
            grid=(t // SC_INDEX_WINDOW,),
            in_specs=[idx_spec, idx_spec, idx_spec],
            out_specs=[],
            core_axis_name=("core", "subcore"),
            dimension_semantics=(pltpu.PARALLEL,),
        )(id_hbm, da_hbm, db_hbm)

    return run(ys, jnp.arange(t, dtype=jnp.int32).reshape(1, t), dest_a, dest_b)


def _combine_dense_kernel(h_ref, rec_ref, ga_ref, gb_ref, *out_refs, split):
    i = pl.program_id(0)
    rec = rec_ref[...]
    res = h_ref[...] + rec[:, 4:5] * _unpack_halves(ga_ref[...]) + rec[:, 5:6] * _unpack_halves(gb_ref[...])
    if not split:
        out_refs[0][...] = res
    else:
        @pl.when(i < N_PROMPT_TILES)
        def _():
            out_refs[0][...] = res

        @pl.when((i >= N_PROMPT_TILES) & (i < N_PROMPT_TILES + N_SAMPLE_TILES))
        def _():
            out_refs[1][...] = res


def moe_combine_dense(h, rec, ga, gb, split=False):
    t, d = h.shape
    row = lambda i: (i, 0)
    if split:
        out_shape = (jax.ShapeDtypeStruct((T_PROMPT, d), F32), jax.ShapeDtypeStruct((T_SAMPLE, d), F32))
        out_specs = tuple(_parts_specs(d)[:2])
    else:
        out_shape = jax.ShapeDtypeStruct((t, d), F32)
        out_specs = pl.BlockSpec((ROW_TILE, d), row)
    return pl.pallas_call(
        functools.partial(_combine_dense_kernel, split=split),
        out_shape=out_shape,
        grid=(t // ROW_TILE,),
        in_specs=[pl.BlockSpec((ROW_TILE, d), row), pl.BlockSpec((ROW_TILE, LANES), row),
                  pl.BlockSpec((ROW_TILE, d // 2), row), pl.BlockSpec((ROW_TILE, d // 2), row)],
        out_specs=out_specs,
        compiler_params=_cparams(("arbitrary",)),
        name="moe_combine_dense",
    )(h, rec, ga, gb)


def _combine_kv_q_kernel(h_ref, rec_ref, ga_ref, gb_ref, gkv_ref, wkv_ref, gq_ref, wq_ref, hmean_ref,
                         hgk_ref, hgq_ref, cos_ref, sina_ref, sinb_ref, ho_ref, k_ref, v_ref, q_ref):
    rec = rec_ref[...]
    h = h_ref[...] + rec[:, 4:5] * _unpack_halves(ga_ref[...]) + rec[:, 5:6] * _unpack_halves(gb_ref[...])
    ho_ref[...] = h
    xhat = h * lax.rsqrt(jnp.mean(h * h, axis=-1, keepdims=True) + RMS_EPS)
    tables = (cos_ref[...], sina_ref[...], sinb_ref[...])
    zkv = jnp.dot((xhat * gkv_ref[...]).astype(BF16), wkv_ref[...], preferred_element_type=F32)
    k_ref[...] = _head_norm_rope(zkv[:, :KV_DIM], hmean_ref, hgk_ref[...], *tables)
    v_ref[...] = zkv[:, KV_DIM:]
    zq = jnp.dot((xhat * gq_ref[...]).astype(BF16), wq_ref[...], preferred_element_type=F32)
    q = _head_norm_rope(zq, hmean_ref, hgq_ref[...], *tables)
    q_ref[...] = (q * HEAD_DIM ** -0.5).astype(q_ref.dtype)


def moe_combine_kv_q(h, rec, ga, gb, kv_gain, kv_w_bf16, q_gain, wq_bf16, hmean, k_hgain, q_hgain,
                     cos_t, sina_t, sinb_t):
    t, d = h.shape
    row = lambda i: (i, 0)
    fix = lambda i: (0, 0)
    rope = pl.BlockSpec((ROW_TILE, LANES), _rope_tile)
    return pl.pallas_call(
        _combine_kv_q_kernel,
        out_shape=(jax.ShapeDtypeStruct((t, d), F32), jax.ShapeDtypeStruct((t, KV_DIM), F32),
                   jax.ShapeDtypeStruct((t, KV_DIM), F32), jax.ShapeDtypeStruct((t, d), BF16)),
        grid=(t // ROW_TILE,),
        in_specs=[pl.BlockSpec((ROW_TILE, d), row), pl.BlockSpec((ROW_TILE, LANES), row),
                  pl.BlockSpec((ROW_TILE, d // 2), row), pl.BlockSpec((ROW_TILE, d // 2), row),
                  pl.BlockSpec((1, d), fix), pl.BlockSpec((d, 2 * KV_DIM), fix),
                  pl.BlockSpec((1, d), fix), pl.BlockSpec((d, d), fix),
                  pl.BlockSpec((256, 256), fix), pl.BlockSpec((1, KV_DIM), fix), pl.BlockSpec((1, d), fix),
                  rope, rope, rope],
        out_specs=(pl.BlockSpec((ROW_TILE, d), row), pl.BlockSpec((ROW_TILE, KV_DIM), row),
                   pl.BlockSpec((ROW_TILE, KV_DIM), row), pl.BlockSpec((ROW_TILE, d), row)),
        compiler_params=_cparams(("parallel",)),
        name="moe_combine_kv_q",
    )(h, rec, ga, gb, kv_gain.reshape(1, d), kv_w_bf16, q_gain.reshape(1, d), wq_bf16, hmean, k_hgain, q_hgain,
      cos_t, sina_t, sinb_t)


def _ffn_kernel(wblk_ref, we_ref, wlo_ref, whi_ref, wnext_ref, wpar_ref, xs_ref, w13_ref, w2_ref, ys_ref,
                w13f, w2f, w13b, w2b, wsem, *, layer):
    w = pl.program_id(0)
    prev = jnp.maximum(w - 1, 0)
    first_visit = (w == 0) | (wblk_ref[w] != wblk_ref[prev])
    lo = wlo_ref[w]
    hi = whi_ref[w]

    def weight_copies(e, par):
        return (pltpu.make_async_copy(w13_ref.at[layer, e], w13f.at[par], wsem.at[par, 0]),
                pltpu.make_async_copy(w2_ref.at[layer, e], w2f.at[par], wsem.at[par, 1]))

    @pl.when(w == 0)
    def _():
        for c in weight_copies(we_ref[0], 0):
            c.start()

    def ffn(x):
        x = _unpack_halves(x).astype(BF16)
        cw = D_EXPERT // FFN_CHUNKS
        gate_up = []
        for c in range(FFN_CHUNKS):
            a = jnp.dot(x, w13b[:, c * cw:(c + 1) * cw], preferred_element_type=F32)
            u = jnp.dot(x, w13b[:, D_EXPERT + c * cw:D_EXPERT + (c + 1) * cw], preferred_element_type=F32)
            gate_up.append((a, u))
        hmid = jnp.concatenate([(_silu(a) * u).astype(BF16) for a, u in gate_up], axis=1)
        return _pack_halves(jnp.dot(hmid, w2b[...], preferred_element_type=F32))

    @pl.when(hi > lo)
    def _():
        @pl.when((w == 0) | (we_ref[w] != we_ref[prev]))
        def _():
            par = wpar_ref[w]
            for c in weight_copies(we_ref[w], par):
                c.wait()
            w13b[...] = w13f[par].astype(BF16)
            w2b[...] = w2f[par].astype(BF16)
            nxt = wnext_ref[w]

            @pl.when(nxt >= 0)
            def _():
                for c in weight_copies(nxt, 1 - par):
                    c.start()

        whole = (lo == 0) & (hi == EXPERT_BLOCK)

        @pl.when(whole)
        def _():
            ys_ref[...] = ffn(xs_ref[...])

        half = FFN_PART
        for p in range(EXPERT_BLOCK // FFN_PART):
            rows = slice(p * half, (p + 1) * half)
            touched = (lo < (p + 1) * half) & (hi > p * half)

            @pl.when(jnp.logical_not(whole) & touched)
            def _():
                y = ffn(xs_ref[rows, :])
                row = lax.broadcasted_iota(jnp.int32, y.shape, 0) + p * half
                mine = (row >= lo) & (row < hi)

                @pl.when(first_visit)
                def _():
                    ys_ref[rows, :] = jnp.where(mine, y, jnp.zeros_like(y))

                @pl.when(jnp.logical_not(first_visit))
                def _():
                    ys_ref[rows, :] = jnp.where(mine, y, ys_ref[rows, :])

            @pl.when(jnp.logical_not(whole) & jnp.logical_not(touched) & first_visit)
            def _():
                ys_ref[rows, :] = jnp.zeros((half, ys_ref.shape[1]), U32)


def moe_ffn(xs, work, w13_all, w2_all, layer):
    n_slots, dp = xs.shape
    d = 2 * dp
    n_work = work[0].shape[0]
    xmap = lambda w, *prefetch: (prefetch[0][w], 0)
    grid_spec = pltpu.PrefetchScalarGridSpec(
        num_scalar_prefetch=len(work),
        grid=(n_work,),
        in_specs=[pl.BlockSpec((EXPERT_BLOCK, dp), xmap),
                  pl.BlockSpec(memory_space=pl.ANY), pl.BlockSpec(memory_space=pl.ANY)],
        out_specs=pl.BlockSpec((EXPERT_BLOCK, dp), xmap),
        scratch_shapes=[pltpu.VMEM((2, d, 2 * D_EXPERT), F32), pltpu.VMEM((2, D_EXPERT, d), F32),
                        pltpu.VMEM((d, 2 * D_EXPERT), BF16), pltpu.VMEM((D_EXPERT, d), BF16),
                        pltpu.SemaphoreType.DMA((2, 2))],
    )
    return pl.pallas_call(
        functools.partial(_ffn_kernel, layer=layer),
        out_shape=jax.ShapeDtypeStruct((n_slots, dp), U32),
        grid_spec=grid_spec,
        compiler_params=_cparams(("arbitrary",)),
        name="moe_ffn",
    )(*work, xs, w13_all, w2_all)


def _ffn_work_items(cnt):
    n_slots = 2 * T_ALL
    n_blocks = n_slots // EXPERT_BLOCK
    n_work = n_blocks + N_EXPERTS - 1
    end = jnp.cumsum(cnt)
    start = end - cnt
    first_blk = start // EXPERT_BLOCK
    last_blk = jnp.maximum(end - 1, start) // EXPERT_BLOCK
    n_items = jnp.where(cnt > 0, last_blk - first_blk + 1, 0)
    item_end = jnp.cumsum(n_items)
    item_start = item_end - n_items
    w = jnp.arange(n_work, dtype=jnp.int32)
    used = w < item_end[-1]
    wq = jnp.minimum(w, item_end[-1] - 1)
    e = jnp.sum((item_end[:, None] <= wq[None, :]).astype(jnp.int32), axis=0)
    onehot = e[None, :] == jnp.arange(N_EXPERTS, dtype=jnp.int32)[:, None]
    of_e = lambda table: jnp.sum(jnp.where(onehot, table[:, None], 0), axis=0)
    blk = jnp.where(used, of_e(first_blk) + (w - of_e(item_start)), n_blocks - 1).astype(jnp.int32)
    lo = jnp.maximum(of_e(start), blk * EXPERT_BLOCK) - blk * EXPERT_BLOCK
    hi = jnp.minimum(of_e(end), (blk + 1) * EXPERT_BLOCK) - blk * EXPERT_BLOCK
    lo = jnp.where(used, lo, 0).astype(jnp.int32)
    hi = jnp.where(used, hi, 0).astype(jnp.int32)
    e_before = jnp.concatenate([jnp.full((1,), -1, jnp.int32), e[:-1]])
    change = used & (e != e_before)
    parity = ((jnp.cumsum(change.astype(jnp.int32)) - 1) % 2).astype(jnp.int32)
    far = jnp.int32(n_work)
    next_change = lax.cummin(jnp.where(change, w, far), axis=0, reverse=True)
    next_change = jnp.concatenate([next_change[1:], jnp.full((1,), far, jnp.int32)])
    e_next = jnp.sum(jnp.where(next_change[None, :] == w[:, None], e[:, None], 0), axis=0)
    e_next = jnp.where(next_change < far, e_next, -1).astype(jnp.int32)
    return start, (blk, e, lo, hi, e_next, parity)


def hier_moe_layer(a, w_out_bf16, x, layer, gain, w_group, b_group, w_expert, b_expert, w13_all, w2_all, utri,
                   finish):
    t = T_ALL
    pad = LANES - N_EXPERTS - N_GROUPS
    w_router = jnp.concatenate([w_expert, w_group, jnp.zeros((D_MODEL, pad), F32)], axis=1)
    b_router = jnp.concatenate([b_expert, b_group, jnp.zeros((pad,), F32)])[:ROUTE_ROWS].reshape(ROUTE_ROWS, 1)
    h, xn, rec, rect, counts = moe_route(a, w_out_bf16, x, gain, w_router, b_router, utri)

    cnt = counts[:N_EXPERTS, 0].astype(jnp.int32)
    start, work = _ffn_work_items(cnt)
    experts = jnp.arange(N_EXPERTS, dtype=jnp.int32)[:, None]

    def slot_of(e_row, rank_row):
        first = jnp.sum(jnp.where(e_row.astype(jnp.int32)[None, :] == experts, start[:, None], 0), axis=0)
        return (first + rank_row.astype(jnp.int32)).reshape(1, t)

    dest_a = slot_of(rect[0], rect[2])
    dest_b = slot_of(rect[1], rect[3])

    xs = moe_dispatch_sc(xn, dest_a, dest_b, 2 * t)
    ys = moe_ffn(xs, work, w13_all, w2_all, layer)
    ga, gb = moe_gather_sc(ys, dest_a, dest_b)
    return finish(h, rec, ga, gb)


def _rope_tables(pos):
    half = ROPE_DIM // 2
    lane = np.arange(LANES) % HEAD_DIM
    rotary = lane < ROPE_DIM
    inv = jnp.where(rotary, jnp.exp(-math.log(ROPE_THETA) * jnp.asarray(lane % half, F32) * (2.0 / ROPE_DIM)), 0.0)
    ang = pos.astype(F32)[:, None] * inv[None, :]
    cos, sin = jnp.cos(ang), jnp.sin(ang)
    first = jnp.asarray(lane < half)
    second = jnp.asarray(rotary & (lane >= half))
    return cos, jnp.where(first, -sin, 0.0), jnp.where(second, sin, 0.0)


def kernel(x_prompt, x_sample, state_hgrn, cache_k_win, cache_v_win, meta_tokens, a_norm, a_w_in, a_lower_logits, a_out_norm, a_w_out, kv_norm, kv_w, k_norm, b_norm, b_wq, b_q_norm, b_sinks, b_w_out, moe_norm, moe_w_group, moe_b_group, moe_w_expert, moe_b_expert, moe_w13, moe_w2):
    tail_rows = T_ALL - OFF_META
    x_parts = (x_prompt.reshape(T_PROMPT, D_MODEL), x_sample.reshape(T_SAMPLE, D_MODEL),
               jnp.concatenate([meta_tokens.astype(F32), jnp.zeros((tail_rows - N_META, D_MODEL), F32)], axis=0))
    pos = jnp.concatenate([N_META + jnp.arange(SEQ, dtype=jnp.int32),
                           jnp.tile(PAST_LEN + jnp.arange(DEC_SEQ, dtype=jnp.int32), ROW_TILE // DEC_SEQ),
                           jnp.arange(N_META, dtype=jnp.int32),
                           jnp.zeros((ROW_TILE - N_META,), jnp.int32)])
    cos_t, sina_t, sinb_t = _rope_tables(pos)
    r256 = np.arange(256)
    hmean = jnp.asarray((r256[:, None] // HEAD_DIM == r256[None, :] // HEAD_DIM).astype(np.float32) / HEAD_DIM, BF16)
    rt = np.arange(ROW_TILE)
    utri = jnp.asarray((rt[:, None] < rt[None, :]).astype(np.float32), BF16)
    lower = jnp.cumsum(jax.nn.softmax(a_lower_logits.astype(F32), axis=0), axis=0)

    moe = functools.partial(hier_moe_layer, w13_all=moe_w13, w2_all=moe_w2, utri=utri)

    z = in_project(x_parts, a_norm[0], a_w_in[0].astype(BF16))
    zero_state = jnp.zeros((1, A_HEADS, A_DK, A_DV), F32)
    o_meta, s_meta = hgrn2_scan(z, zero_state, lower[0], a_out_norm[0],
                                row_off=OFF_META, n_seq=1, seq_len=N_META)
    o_prompt, s_prompt = hgrn2_scan(z, s_meta, lower[0], a_out_norm[0], row_off=0, n_seq=BATCH, seq_len=SEQ)
    o_sample, s_sample = hgrn2_scan(z, state_hgrn[0].astype(F32), lower[0], a_out_norm[0],
                                    row_off=OFF_SAMPLE, n_seq=DEC_BATCH, seq_len=DEC_SEQ, group=SCAN_SAMPLE_GROUP)
    o_tail = jnp.concatenate([o_meta, jnp.zeros((tail_rows - N_META, D_MODEL), BF16)], axis=0)
    finish0 = functools.partial(
        moe_combine_kv_q, kv_gain=kv_norm, kv_w_bf16=kv_w.astype(BF16), q_gain=b_norm[0], wq_bf16=b_wq[0].astype(BF16),
        hmean=hmean, k_hgain=jnp.tile(k_norm, KV_HEADS).reshape(1, KV_DIM),
        q_hgain=jnp.tile(b_q_norm[0], Q_HEADS).reshape(1, D_MODEL), cos_t=cos_t, sina_t=sina_t, sinb_t=sinb_t)
    h, k_all, v_all, q_all = moe((o_prompt, o_sample, o_tail), a_w_out[0].astype(BF16), x_parts, 0, moe_norm[0],
                                 moe_w_group[0], moe_b_group[0], moe_w_expert[0], moe_b_expert[0], finish=finish0)

    meta_blk = lambda a: jnp.concatenate([jnp.zeros((ATT_BLOCK - N_META, KV_DIM), F32),
                                          a[OFF_META:OFF_META + N_META]], axis=0)
    sinks = b_sinks[0].astype(F32)
    att_all = attention_prompt(q_all, k_all, v_all, meta_blk(k_all), meta_blk(v_all), sinks)
    att_all, k_win_s, v_win_s = attention_sample(q_all, cache_k_win.astype(F32), cache_v_win.astype(F32),
                                                 k_all, v_all, sinks, att_all)
    y_p, y_s = moe(att_all, b_w_out[0].astype(BF16), h, 1, moe_norm[1], moe_w_group[1], moe_b_group[1],
                   moe_w_expert[1], moe_b_expert[1], finish=functools.partial(moe_combine_dense, split=True))

    y_prompt = y_p.reshape(BATCH, SEQ, D_MODEL)
    y_sample = y_s.reshape(DEC_BATCH, DEC_SEQ, D_MODEL)
    last = lambda a: jnp.stack([a[(b + 1) * SEQ - WINDOW:(b + 1) * SEQ] for b in range(BATCH)]).reshape(
        BATCH, WINDOW, KV_HEADS, HEAD_DIM)
    kp = last(k_all)
    vp = last(v_all)
    return (y_prompt, y_sample, s_prompt[None], s_sample[None], kp, vp, k_win_s, v_win_s)
```

```python
import functools
import math

import numpy as np
import jax
import jax.numpy as jnp
from jax import lax
from jax.experimental import pallas as pl
from jax.experimental.pallas import tpu as pltpu
from jax.experimental.pallas import tpu_sc as plsc

F32 = jnp.float32
BF16 = jnp.bfloat16
U32 = jnp.uint32

D_MODEL = 1024
BATCH = 4
SEQ = 4096
DEC_BATCH = 128
DEC_SEQ = 8
PAST_LEN = 8192
N_META = 16
A_HEADS = 8
A_DK = 128
A_DV = 128
Q_HEADS = 16
KV_HEADS = 4
HEAD_DIM = 64
KV_DIM = KV_HEADS * HEAD_DIM
WINDOW = 128
ROPE_DIM = 16
ROPE_THETA = 500000.0
N_GROUPS = 4
EXPERTS_PER_GROUP = 8
N_EXPERTS = 32
D_EXPERT = 512
RMS_EPS = 1e-6

LANES = 128
SUBLANES = 8
MXU_DIM = 256
VMEM_BYTES = 64 * 1024 * 1024
VMEM_LIMIT = VMEM_BYTES - 8 * 1024 * 1024
NORM_SLAB = MXU_DIM

ROW_TILE = 512
T_PROMPT = BATCH * SEQ
T_SAMPLE = DEC_BATCH * DEC_SEQ
OFF_SAMPLE = T_PROMPT
OFF_META = T_PROMPT + T_SAMPLE
T_REAL = OFF_META + N_META
T_ALL = -(-T_REAL // ROW_TILE) * ROW_TILE
N_TILES = T_ALL // ROW_TILE

SCAN_CHUNK = 128
SCAN_SAMPLE_GROUP = 16
ATT_BLOCK = 128
EXPERT_BLOCK = 512
FFN_PART = 256
FFN_CHUNKS = 2


def _cparams(sem):
    return pltpu.CompilerParams(dimension_semantics=sem, vmem_limit_bytes=VMEM_LIMIT)


def _nt_dot(a, b):
    return lax.dot_general(a, b, (((1,), (1,)), ((), ())), preferred_element_type=F32)


def _rms(x, gain):
    ms = jnp.mean(x * x, axis=-1, keepdims=True)
    return x * lax.rsqrt(ms + RMS_EPS) * gain


def _silu(x):
    return x * jax.nn.sigmoid(x)


def _pack_halves(x):
    w = x.shape[1] // 2
    hi = lax.bitcast_convert_type(x[:, :w].astype(BF16).astype(F32), U32)
    lo = lax.bitcast_convert_type(x[:, w:].astype(BF16).astype(F32), U32)
    return hi | (lo >> 16)


def _unpack_halves(p):
    hi = lax.bitcast_convert_type(p & jnp.uint32(0xFFFF0000), F32)
    lo = lax.bitcast_convert_type(p << 16, F32)
    return jnp.concatenate([hi, lo], axis=1)


N_PROMPT_TILES = T_PROMPT // ROW_TILE
N_SAMPLE_TILES = T_SAMPLE // ROW_TILE


def _parts_specs(width):
    return [pl.BlockSpec((ROW_TILE, width), lambda i: (jnp.minimum(i, N_PROMPT_TILES - 1), 0)),
            pl.BlockSpec((ROW_TILE, width), lambda i: (jnp.clip(i - N_PROMPT_TILES, 0, N_SAMPLE_TILES - 1), 0)),
            pl.BlockSpec((ROW_TILE, width), lambda i: (0, 0))]


def _pick_part(i, p_ref, s_ref, t_ref, dtype):
    return jnp.where(i < N_PROMPT_TILES, p_ref[...].astype(dtype),
                     jnp.where(i < N_PROMPT_TILES + N_SAMPLE_TILES, s_ref[...].astype(dtype),
                               t_ref[...].astype(dtype)))


def _in_proj_kernel(xp_ref, xs_ref, xt_ref, g_ref, w_ref, o_ref):
    x = _pick_part(pl.program_id(0), xp_ref, xs_ref, xt_ref, F32)
    xn = _rms(x, g_ref[...])
    o_ref[...] = jnp.dot(xn.astype(BF16), w_ref[...], preferred_element_type=F32)


def in_project(x_parts, gain, w_bf16):
    d, n = w_bf16.shape
    return pl.pallas_call(
        _in_proj_kernel,
        out_shape=jax.ShapeDtypeStruct((T_ALL, n), F32),
        grid=(N_TILES,),
        in_specs=_parts_specs(d) + [pl.BlockSpec((1, d), lambda i: (0, 0)),
                                    pl.BlockSpec((d, n), lambda i: (0, 0))],
        out_specs=pl.BlockSpec((ROW_TILE, n), lambda i: (i, 0)),
        compiler_params=_cparams(("parallel",)),
        name="in_project",
    )(*x_parts, gain.reshape(1, d), w_bf16)


def _head_norm_rope(y, hmean_ref, hgain, cos_t, sina_t, sinb_t):
    rows, width = y.shape
    sq = (y * y).astype(BF16)
    parts = []
    for s in range(width // NORM_SLAB):
        parts.append(jnp.dot(sq[:, s * NORM_SLAB:(s + 1) * NORM_SLAB], hmean_ref[...], preferred_element_type=F32))
    ms = parts[0] if len(parts) == 1 else jnp.concatenate(parts, axis=1)
    yn = y * lax.rsqrt(ms + RMS_EPS) * hgain
    reps = width // LANES
    cos_w = jnp.concatenate([cos_t] * reps, axis=1)
    sina_w = jnp.concatenate([sina_t] * reps, axis=1)
    sinb_w = jnp.concatenate([sinb_t] * reps, axis=1)
    half = ROPE_DIM // 2
    nxt = pltpu.roll(yn, width - half, 1)
    prv = pltpu.roll(yn, half, 1)
    return yn * cos_w + nxt * sina_w + prv * sinb_w


def _rope_tile(i):
    tiles_per_seq = SEQ // ROW_TILE
    n_prompt_tiles = T_PROMPT // ROW_TILE
    n_sample_tiles = T_SAMPLE // ROW_TILE
    return (jnp.where(i < n_prompt_tiles, i % tiles_per_seq,
                      jnp.where(i < n_prompt_tiles + n_sample_tiles, tiles_per_seq, tiles_per_seq + 1)), 0)


def _scan_levels(c):
    levels = []
    m = c
    while m >= 2:
        levels.append(m)
        m //= 2
    return levels


LOG2E = 1.4426950408889634


def _scan_kernel(z_ref, s0_ref, lb_ref, og_ref, tri_ref, lmask_ref, sgn_ref, o_ref, sfin_ref, s_scr, b_scr,
                 *, rows, seq_len):
    c_idx = pl.program_id(1)
    levels = _scan_levels(seq_len)
    n_sub = rows // seq_len
    hk = A_HEADS * A_DK

    @pl.when(c_idx == 0)
    def _():
        s_scr[...] = s0_ref[...]

    sub = lax.broadcasted_iota(jnp.int32, (SUBLANES, LANES), 0)
    row = lax.broadcasted_iota(jnp.int32, (LANES, LANES), 0)
    og = og_ref[...]

    def pad_f32(x):
        if x.shape[0] == LANES:
            return x
        return jnp.concatenate([x, jnp.zeros((LANES - x.shape[0], x.shape[1]), x.dtype)], axis=0)

    def pad_rows(x):
        return pad_f32(x).astype(BF16)

    def cols(part, h):
        return slice(part * hk + h * LANES, part * hk + (h + 1) * LANES)

    def gates(h):
        lb = lb_ref[:, cols(0, h)]
        forget = lb + (1.0 - lb) * jax.nn.sigmoid(z_ref[:, cols(1, h)])
        logf = jnp.log(forget)
        hi = logf.astype(BF16).astype(F32)
        r1 = logf - hi
        mid = r1.astype(BF16).astype(F32)
        lo = r1 - mid
        cs = jnp.dot(tri_ref[...], pad_rows(jnp.concatenate([hi, mid, lo], axis=1)),
                     preferred_element_type=F32)
        b = (cs[:rows, :LANES] + cs[:rows, LANES:2 * LANES]) + cs[:rows, 2 * LANES:]
        b_scr[h] = b
        return _silu(z_ref[:, cols(0, h)]), 1.0 - forget, b

    def bref_for(h, m):
        b_rows = b_scr.at[h]
        half = m // 2
        pieces = []
        for g in range(rows // SUBLANES):
            base = g * SUBLANES
            if m >= SUBLANES:
                r = (base // m) * m + half - 1
                piece = jnp.broadcast_to(b_rows[r:r + 1, :], (SUBLANES, LANES))
            else:
                piece = jnp.broadcast_to(b_rows[base + half - 1:base + half, :], (SUBLANES, LANES))
                for blk in range(1, SUBLANES // m):
                    r = base + blk * m + half - 1
                    piece = jnp.where(sub >= blk * m,
                                      jnp.broadcast_to(b_rows[r:r + 1, :], (SUBLANES, LANES)), piece)
            pieces.append(piece)
        return pieces[0] if len(pieces) == 1 else jnp.concatenate(pieces, axis=0)

    heads = range(A_HEADS)
    qkb = [gates(h) for h in heads]
    att = [_nt_dot(pad_rows(qf), pad_rows(kf)) * lmask_ref[len(levels)] for qf, kf, _ in qkb]
    for li, m in enumerate(levels):
        for h in heads:
            qf, kf, b = qkb[h]
            sgn = sgn_ref[li]
            e = jnp.exp2((b - bref_for(h, m)) * sgn)
            w = pad_rows(jnp.where(sgn > 0, qf, kf) * e)
            att[h] = att[h] + _nt_dot(w, w) * lmask_ref[li]

    def finish(h):
        qf, kf, b = qkb[h]
        b_rows = b_scr.at[h]
        v_b = pad_rows(z_ref[:, cols(2, h)])
        o_intra = jnp.dot(att[h].astype(BF16), v_b, preferred_element_type=F32)
        eb = jnp.exp(b)
        qs = qf * eb
        b_end = [jnp.broadcast_to(b_rows[(i + 1) * seq_len - 1:(i + 1) * seq_len, :], (seq_len, LANES))
                 for i in range(n_sub)]
        b_end = b_end[0] if n_sub == 1 else jnp.concatenate(b_end, axis=0)
        kd_t = pad_f32(kf * jnp.exp(b_end - b)).T.astype(BF16)
        eb_t = pad_f32(eb).T
        qs_b = pad_rows(qs)
        o = o_intra
        for i in range(n_sub):
            s_old = s_scr[i, h]
            first, last = i * seq_len, (i + 1) * seq_len - 1
            if n_sub == 1:
                qs_i, v_i = qs_b, v_b
            else:
                mine = (row >= first) & (row <= last)
                qs_i = jnp.where(mine, qs_b, jnp.zeros_like(qs_b))
                v_i = jnp.where(mine, v_b, jnp.zeros_like(v_b))
            o = o + jnp.dot(qs_i, s_old.astype(BF16), preferred_element_type=F32)
            decay = jnp.broadcast_to(eb_t[:, last:last + 1], (LANES, LANES))
            s_scr[i, h] = decay * s_old + jnp.dot(kd_t, v_i, preferred_element_type=F32)
        o = o[:rows]

        on = _rms(o, og) * _silu(z_ref[:, cols(3, h)])
        o_ref[:, cols(0, h)] = on.astype(o_ref.dtype)

    for h in heads:
        finish(h)

    @pl.when(c_idx == pl.num_programs(1) - 1)
    def _():
        sfin_ref[...] = s_scr[...]


def _scan_consts(rows, seq_len):
    levels = _scan_levels(seq_len)
    r = np.arange(LANES)
    t, s = r[:, None], r[None, :]
    live = (t < rows) & (s < rows)
    tri = ((s <= t) & (t // seq_len == s // seq_len) & live).astype(np.float32)
    masks, sgns = [], []
    for m in levels:
        masks.append(((t // m == s // m) & (t % m >= m // 2) & (s % m < m // 2) & live).astype(np.float32))
        sgns.append(np.broadcast_to(np.where(r[:rows, None] % m >= m // 2, LOG2E, -LOG2E), (rows, LANES)))
    masks.append(((t == s) & live).astype(np.float32))
    return jnp.asarray(tri, BF16), jnp.asarray(np.stack(masks), F32), jnp.asarray(np.stack(sgns), F32)


def hgrn2_scan(z, s0, lb, o_gain, *, row_off, n_seq, seq_len, group=1):
    hv = A_HEADS * A_DV
    if seq_len > SCAN_CHUNK:
        assert group == 1
        sub_len, rows, n_chunks, n_steps = SCAN_CHUNK, SCAN_CHUNK, seq_len // SCAN_CHUNK, n_seq
    else:
        sub_len, rows, n_chunks, n_steps = seq_len, group * seq_len, 1, n_seq // group
    blk_off = row_off // rows
    tri, lmask, sgn = _scan_consts(rows, sub_len)
    shared_s0 = s0.shape[0] == 1
    fix2 = lambda s, c: (0, 0)
    fix3 = lambda s, c: (0, 0, 0)
    o, sfin = pl.pallas_call(
        functools.partial(_scan_kernel, rows=rows, seq_len=sub_len),
        out_shape=(jax.ShapeDtypeStruct((n_seq * seq_len, hv), BF16 if rows % 16 == 0 else F32),
                   jax.ShapeDtypeStruct((n_seq, A_HEADS, A_DK, A_DV), F32)),
        grid=(n_steps, n_chunks),
        in_specs=[pl.BlockSpec((rows, 4 * hv), lambda s, c: (blk_off + s * n_chunks + c, 0)),
                  pl.BlockSpec((group, A_HEADS, A_DK, A_DV), (lambda s, c: (0, 0, 0, 0)) if shared_s0
                               else (lambda s, c: (s, 0, 0, 0))),
                  pl.BlockSpec((1, hv), fix2), pl.BlockSpec((1, A_DV), fix2),
                  pl.BlockSpec((LANES, LANES), fix2), pl.BlockSpec(lmask.shape, fix3),
                  pl.BlockSpec(sgn.shape, fix3)],
        out_specs=(pl.BlockSpec((rows, hv), lambda s, c: (s * n_chunks + c, 0)),
                   pl.BlockSpec((group, A_HEADS, A_DK, A_DV), lambda s, c: (s, 0, 0, 0))),
        scratch_shapes=[pltpu.VMEM((group, A_HEADS, A_DK, A_DV), F32), pltpu.VMEM((A_HEADS, rows, LANES), F32)],
        compiler_params=_cparams(("parallel", "arbitrary")),
        name=f"hgrn2_scan_r{rows}",
    )(z, s0, lb.reshape(1, hv), o_gain.reshape(1, A_DV), tri, lmask, sgn)
    return o, sfin


KEYS = 2 * ATT_BLOCK
ATT_STEP = 1


def _pair_operand(x, kh):
    slab = x[:, (kh // 2) * LANES:(kh // 2 + 1) * LANES]
    lane = lax.broadcasted_iota(jnp.int32, slab.shape, 1)
    if kh % 2 == 0:
        lo = jnp.where(lane < HEAD_DIM, slab, 0.0)
        hi = pltpu.roll(lo, HEAD_DIM, 1)
    else:
        hi = jnp.where(lane >= HEAD_DIM, slab, 0.0)
        lo = pltpu.roll(hi, HEAD_DIM, 1)
    return jnp.concatenate([lo, hi], axis=0).astype(BF16)


def _window_bias(rows, jmin):
    t_i = lax.broadcasted_iota(jnp.int32, (rows, 2 * KEYS), 0)
    c_i = lax.broadcasted_iota(jnp.int32, (rows, 2 * KEYS), 1)
    j_i = c_i & (ATT_BLOCK - 1)
    own = (c_i & ATT_BLOCK) != 0
    ok = (own & (j_i <= t_i)) | (jnp.logical_not(own) & (j_i >= t_i) & (j_i >= jmin))
    return jnp.where(ok, 0.0, -jnp.inf).astype(F32)


def _pair_softmax(s, sink_a, sink_b):
    probs, rinv = [], []
    for hh, sink in enumerate((sink_a, sink_b)):
        sh = s[:, hh * KEYS:(hh + 1) * KEYS]
        m = jnp.maximum(jnp.max(sh, axis=-1, keepdims=True), sink)
        p = jnp.exp(sh - m)
        den = jnp.sum(p, axis=-1, keepdims=True) + jnp.exp(sink - m)
        probs.append(p.astype(BF16))
        rinv.append(1.0 / den)
    lane = lax.broadcasted_iota(jnp.int32, (s.shape[0], LANES), 1)
    return jnp.concatenate(probs, axis=1), jnp.where(lane < HEAD_DIM, rinv[0], rinv[1])


def _attn_prompt_kernel(sink_ref, q_ref, kp_ref, ko_ref, vp_ref, vo_ref, km_ref, vm_ref, o_ref,
                        k2_scr, v2_scr, s_scr, p_scr, r_scr, bias_scr):
    n = pl.program_id(0)
    steps_per_seq = SEQ // (ATT_STEP * ATT_BLOCK)
    n_pairs = Q_HEADS // 2
    blk = ATT_BLOCK

    @pl.when(n == 0)
    def _():
        bias_scr[0] = _window_bias(ATT_BLOCK, 0)
        bias_scr[1] = _window_bias(ATT_BLOCK, ATT_BLOCK - N_META)

    @pl.when(n >= BATCH * steps_per_seq)
    def _():
        o_ref[...] = jnp.zeros_like(o_ref)

    @pl.when(n < BATCH * steps_per_seq)
    def _():
        first = (n % steps_per_seq) == 0
        own_k, own_v = ko_ref[...], vo_ref[...]
        keys = [jnp.where(first, km_ref[...], kp_ref[...])] + [own_k[u * blk:(u + 1) * blk] for u in range(ATT_STEP)]
        vals = [jnp.where(first, vm_ref[...], vp_ref[...])] + [own_v[u * blk:(u + 1) * blk] for u in range(ATT_STEP)]
        biases = [bias_scr[first.astype(jnp.int32)]] + [bias_scr[0]] * (ATT_STEP - 1)
        for u in range(ATT_STEP):
            k = jnp.concatenate(keys[u:u + 2], axis=0)
            v = jnp.concatenate(vals[u:u + 2], axis=0)
            for kh in range(KV_HEADS):
                k2_scr[u, kh] = _pair_operand(k, kh)
                v2_scr[u, kh] = _pair_operand(v, kh)
        units = [(u, pair) for u in range(ATT_STEP) for pair in range(n_pairs)]
        rows = lambda u: slice(u * blk, (u + 1) * blk)
        lanes = lambda pair: slice(pair * LANES, (pair + 1) * LANES)
        for j, (u, pair) in enumerate(units):
            s_scr[j] = _nt_dot(q_ref[rows(u), lanes(pair)], k2_scr[u, pair // 2]) + biases[u]
        for j, (u, pair) in enumerate(units):
            p, rinv = _pair_softmax(s_scr[j], sink_ref[2 * pair], sink_ref[2 * pair + 1])
            p_scr[j] = p
            r_scr[j] = rinv
        for j, (u, pair) in enumerate(units):
            o = jnp.dot(p_scr[j], v2_scr[u, pair // 2], preferred_element_type=F32) * r_scr[j]
            o_ref[rows(u), lanes(pair)] = o.astype(o_ref.dtype)


def attention_prompt(q_all, k_all, v_all, k_meta_blk, v_meta_blk, sinks):
    step_rows = ATT_STEP * ATT_BLOCK
    n_prompt_steps = T_PROMPT // step_rows
    n_steps = T_ALL // step_rows
    n_units = ATT_STEP * (Q_HEADS // 2)
    own = lambda n, sk: (jnp.minimum(n, n_prompt_steps - 1), 0)
    prev = lambda n, sk: (jnp.maximum(ATT_STEP * jnp.minimum(n, n_prompt_steps - 1) - 1, 0), 0)
    fix = lambda n, sk: (0, 0)
    grid_spec = pltpu.PrefetchScalarGridSpec(
        num_scalar_prefetch=1,
        grid=(n_steps,),
        in_specs=[pl.BlockSpec((step_rows, D_MODEL), own),
                  pl.BlockSpec((ATT_BLOCK, KV_DIM), prev), pl.BlockSpec((step_rows, KV_DIM), own),
                  pl.BlockSpec((ATT_BLOCK, KV_DIM), prev), pl.BlockSpec((step_rows, KV_DIM), own),
                  pl.BlockSpec((ATT_BLOCK, KV_DIM), fix), pl.BlockSpec((ATT_BLOCK, KV_DIM), fix)],
        out_specs=pl.BlockSpec((step_rows, D_MODEL), lambda n, sk: (n, 0)),
        scratch_shapes=[pltpu.VMEM((ATT_STEP, KV_HEADS, 2 * KEYS, LANES), BF16),
                        pltpu.VMEM((ATT_STEP, KV_HEADS, 2 * KEYS, LANES), BF16),
                        pltpu.VMEM((n_units, ATT_BLOCK, 2 * KEYS), F32),
                        pltpu.VMEM((n_units, ATT_BLOCK, 2 * KEYS), BF16),
                        pltpu.VMEM((n_units, ATT_BLOCK, LANES), F32),
                        pltpu.VMEM((2, ATT_BLOCK, 2 * KEYS), F32)],
    )
    return pl.pallas_call(
        _attn_prompt_kernel,
        out_shape=jax.ShapeDtypeStruct((T_ALL, D_MODEL), BF16),
        grid_spec=grid_spec,
        compiler_params=_cparams(("arbitrary",)),
        name="attention_prompt",
    )(sinks, q_all, k_all, k_all, v_all, v_all, k_meta_blk, v_meta_blk)


SAMPLE_GROUP = ATT_BLOCK // DEC_SEQ


def _attn_sample_kernel(sink_ref, q_ref, ck_ref, cv_ref, kn_ref, vn_ref, buf_ref, o_ref, kw_ref, vw_ref,
                        qf_scr, of_scr, k2_scr, v2_scr):
    del buf_ref
    qrows = 2 * DEC_SEQ
    qf_scr[...] = q_ref[...].astype(F32)
    bias = _window_bias(qrows, 0)
    zq = jnp.zeros((qrows - DEC_SEQ, D_MODEL), F32)
    zk = jnp.zeros((ATT_BLOCK - DEC_SEQ, KV_DIM), F32)

    n_pairs = Q_HEADS // 2
    lanes_of = lambda pair: slice(pair * LANES, (pair + 1) * LANES)

    def seq_pair_body(it, carry):
        seqs = (2 * it, 2 * it + 1)
        r_new = [pl.multiple_of(i * DEC_SEQ, DEC_SEQ) for i in seqs]
        qs = []
        for u, i in enumerate(seqs):
            qs.append(jnp.concatenate([qf_scr[pl.ds(r_new[u], DEC_SEQ), :], zq], axis=0).astype(BF16))
            old = lambda c_ref: jnp.concatenate([c_ref[i, :, kh, :] for kh in range(KV_HEADS)], axis=1)
            k = jnp.concatenate([old(ck_ref), kn_ref[pl.ds(r_new[u], DEC_SEQ), :], zk], axis=0)
            v = jnp.concatenate([old(cv_ref), vn_ref[pl.ds(r_new[u], DEC_SEQ), :], zk], axis=0)
            for c_ref, n_ref, w_ref in ((ck_ref, kn_ref, kw_ref), (cv_ref, vn_ref, vw_ref)):
                w_ref[i, 0:WINDOW - DEC_SEQ] = c_ref[i, DEC_SEQ:WINDOW]
                for kh in range(KV_HEADS):
                    w_ref[i, WINDOW - DEC_SEQ:WINDOW, kh, :] = n_ref[pl.ds(r_new[u], DEC_SEQ),
                                                                     kh * HEAD_DIM:(kh + 1) * HEAD_DIM]
            for kh in range(KV_HEADS):
                k2_scr[u, kh] = _pair_operand(k, kh)
                v2_scr[u, kh] = _pair_operand(v, kh)
        scores = [[_nt_dot(qs[u][:, lanes_of(pair)], k2_scr[u, pair // 2]) + bias for pair in range(n_pairs)]
                  for u in range(2)]
        soft = [[_pair_softmax(s, sink_ref[2 * pair], sink_ref[2 * pair + 1]) for pair, s in enumerate(scores[u])]
                for u in range(2)]
        for u in range(2):
            for pair, (p, rinv) in enumerate(soft[u]):
                o = jnp.dot(p, v2_scr[u, pair // 2], preferred_element_type=F32) * rinv
                of_scr[pl.ds(r_new[u], DEC_SEQ), lanes_of(pair)] = o[:DEC_SEQ]
        return carry

    lax.fori_loop(0, SAMPLE_GROUP // 2, seq_pair_body, 0)
    o_ref[...] = of_scr[...].astype(o_ref.dtype)


def attention_sample(q_all, cache_k, cache_v, k_all, v_all, sinks, out_buf):
    first_blk = OFF_SAMPLE // ATT_BLOCK
    new = lambda g, sk: (first_blk + g, 0)
    old = pl.BlockSpec((SAMPLE_GROUP, WINDOW, KV_HEADS, HEAD_DIM), lambda g, sk: (g, 0, 0, 0))
    grid_spec = pltpu.PrefetchScalarGridSpec(
        num_scalar_prefetch=1,
        grid=(DEC_BATCH // SAMPLE_GROUP,),
        in_specs=[pl.BlockSpec((ATT_BLOCK, D_MODEL), new),
                  old, old,
                  pl.BlockSpec((ATT_BLOCK, KV_DIM), new), pl.BlockSpec((ATT_BLOCK, KV_DIM), new),
                  pl.BlockSpec(memory_space=pl.ANY)],
        out_specs=(pl.BlockSpec((ATT_BLOCK, D_MODEL), new), old, old),
        scratch_shapes=[pltpu.VMEM((ATT_BLOCK, D_MODEL), F32), pltpu.VMEM((ATT_BLOCK, D_MODEL), F32),
                        pltpu.VMEM((2, KV_HEADS, 2 * KEYS, LANES), BF16),
                        pltpu.VMEM((2, KV_HEADS, 2 * KEYS, LANES), BF16)],
    )
    window = jax.ShapeDtypeStruct(cache_k.shape, cache_k.dtype)
    return pl.pallas_call(
        _attn_sample_kernel,
        out_shape=(jax.ShapeDtypeStruct(out_buf.shape, out_buf.dtype), window, window),
        grid_spec=grid_spec,
        input_output_aliases={6: 0},
        compiler_params=_cparams(("parallel",)),
        name="attention_sample",
    )(sinks, q_all, cache_k, cache_v, k_all, v_all, out_buf)


ROUTE_COLS = 8
ROUTE_ROWS = 48


def _route_kernel(*refs, parts):
    i = pl.program_id(0)
    if parts:
        (ap_ref, as_ref, at_ref, w_ref, xp_ref, xs_ref, xt_ref), refs = refs[:7], refs[7:]
        a = _pick_part(i, ap_ref, as_ref, at_ref, BF16)
        x = _pick_part(i, xp_ref, xs_ref, xt_ref, F32)
    else:
        (a_ref, w_ref, x_ref), refs = refs[:3], refs[3:]
        a, x = a_ref[...], x_ref[...]
    g_ref, whl_ref, br_ref, utri_ref, h_ref, xn_ref, rec_ref, rect_ref, cnt_ref, cnt_scr = refs

    @pl.when(i == 0)
    def _():
        cnt_scr[...] = jnp.zeros_like(cnt_scr)

    h = x + jnp.dot(a, w_ref[...], preferred_element_type=F32)
    h_ref[...] = h
    xn = _rms(h, g_ref[...])
    xn_ref[...] = _pack_halves(xn)
    xh = xn.astype(BF16)
    xl = (xn - xh.astype(F32)).astype(BF16)
    by_xh = _nt_dot(whl_ref[...], xh)
    logits = (by_xh[:LANES] + (by_xh[LANES:] + _nt_dot(whl_ref[:LANES, :], xl)))[:ROUTE_ROWS]
    logits = logits + br_ref[...]
    tokens = logits.shape[1]
    rid = lax.broadcasted_iota(jnp.int32, (ROUTE_ROWS, tokens), 0).astype(F32)
    neg = jnp.float32(-jnp.inf)
    big = jnp.float32(ROUTE_ROWS)

    is_g = (rid >= N_EXPERTS) & (rid < N_EXPERTS + N_GROUPS)
    gl = jnp.where(is_g, logits, neg)
    gmax = jnp.max(gl, axis=0, keepdims=True)
    gsel = jnp.min(jnp.where(gl == gmax, rid, big), axis=0, keepdims=True) - N_EXPERTS
    gden = jnp.sum(jnp.where(is_g, jnp.exp(gl - gmax), 0.0), axis=0, keepdims=True)
    gw = 1.0 / gden

    in_grp = (rid >= gsel * EXPERTS_PER_GROUP) & (rid < (gsel + 1) * EXPERTS_PER_GROUP)
    el = jnp.where(in_grp, logits, neg)
    t1 = jnp.max(el, axis=0, keepdims=True)
    e1 = jnp.min(jnp.where(el == t1, rid, big), axis=0, keepdims=True)
    el2 = jnp.where(rid == e1, neg, el)
    t2 = jnp.max(el2, axis=0, keepdims=True)
    e2 = jnp.min(jnp.where(el2 == t2, rid, big), axis=0, keepdims=True)
    x2 = jnp.exp(t2 - t1)
    w1 = gw / (1.0 + x2)
    w2 = gw * x2 / (1.0 + x2)

    oh1 = (rid == e1).astype(F32)
    oh2 = (rid == e2).astype(F32)
    oh = oh1 + oh2
    before = jnp.dot(oh.astype(BF16), utri_ref[...], preferred_element_type=F32)
    base = cnt_scr[...] + before
    r1 = jnp.sum(base * oh1, axis=0, keepdims=True)
    r2 = jnp.sum(base * oh2, axis=0, keepdims=True)
    cnt_scr[...] = cnt_scr[...] + jnp.sum(oh, axis=1, keepdims=True)

    zero = jnp.zeros_like(w1)
    rect = jnp.concatenate([e1, e2, r1, r2, w1, w2, zero, zero], axis=0)
    rect_ref[...] = rect
    wide = jnp.concatenate([rect, jnp.zeros((LANES - ROUTE_COLS, tokens), F32)], axis=0)
    rec_ref[...] = jnp.concatenate([wide[:, t0:t0 + LANES].T for t0 in range(0, tokens, LANES)], axis=0)
    cnt_ref[...] = cnt_scr[...]


def moe_route(a, w_out_bf16, x, gain, w_router, b_router, utri):
    parts = isinstance(a, tuple)
    t, d = T_ALL, D_MODEL
    row = lambda i: (i, 0)
    fix = lambda i: (0, 0)
    w_t = w_router.T
    w_hi = w_t.astype(BF16)
    w_lo = (w_t - w_hi.astype(F32)).astype(BF16)
    w_spec = pl.BlockSpec((d, d), fix)
    if parts:
        pre_specs = _parts_specs(d) + [w_spec] + _parts_specs(d)
        pre_args = (*a, w_out_bf16, *x)
    else:
        pre_specs = [pl.BlockSpec((ROW_TILE, d), row), w_spec, pl.BlockSpec((ROW_TILE, d), row)]
        pre_args = (a, w_out_bf16, x)
    return pl.pallas_call(
        functools.partial(_route_kernel, parts=parts),
        out_shape=(jax.ShapeDtypeStruct((t, d), F32),
                   jax.ShapeDtypeStruct((t, d // 2), U32), jax.ShapeDtypeStruct((t, LANES), F32),
                   jax.ShapeDtypeStruct((ROUTE_COLS, t), F32), jax.ShapeDtypeStruct((ROUTE_ROWS, 1), F32)),
        grid=(t // ROW_TILE,),
        in_specs=pre_specs + [pl.BlockSpec((1, d), fix), pl.BlockSpec((2 * LANES, d), fix),
                              pl.BlockSpec((ROUTE_ROWS, 1), fix), pl.BlockSpec((ROW_TILE, ROW_TILE), fix)],
        out_specs=(pl.BlockSpec((ROW_TILE, d), row),
                   pl.BlockSpec((ROW_TILE, d // 2), row), pl.BlockSpec((ROW_TILE, LANES), row),
                   pl.BlockSpec((ROUTE_COLS, ROW_TILE), lambda i: (0, i)), pl.BlockSpec((ROUTE_ROWS, 1), fix)),
        scratch_shapes=[pltpu.VMEM((ROUTE_ROWS, 1), F32)],
        compiler_params=_cparams(("arbitrary",)),
        name="moe_route",
    )(*pre_args, gain.reshape(1, d), jnp.concatenate([w_hi, w_lo], axis=0), b_router, utri)


SC_WINDOW = 64
SC_INDEX_WINDOW = 128


def _sc_mesh():
    return plsc.VectorSubcoreMesh(core_axis_name="core", subcore_axis_name="subcore")


def moe_dispatch_sc(xn, dest_a, dest_b, n_slots):
    t, d = xn.shape

    n_parts = SC_INDEX_WINDOW // SC_WINDOW

    @pl.kernel(out_type=jax.ShapeDtypeStruct((n_slots, d), xn.dtype), mesh=_sc_mesh(),
               scratch_types=[pltpu.VMEM((2, SC_WINDOW, d), xn.dtype), pltpu.SemaphoreType.DMA((2,)),
                              pltpu.SemaphoreType.DMA((2,))],
               name="moe_dispatch_sc")
    def run(x_hbm, id_hbm, da_hbm, db_hbm, o_hbm, buf, load_sem, store_sem):
        def body(id_vmem, da_vmem, db_vmem):
            part = lambda j: pl.ds(j * SC_WINDOW, SC_WINDOW)
            load = lambda j: pltpu.make_async_copy(x_hbm.at[id_vmem.at[0, part(j)]], buf.at[j % 2], load_sem.at[j % 2])
            load(0).start()
            for j in range(n_parts):
                load(j).wait()
                if j + 1 < n_parts:
                    load(j + 1).start()
                stores = [pltpu.make_async_copy(buf.at[j % 2], o_hbm.at[dv.at[0, part(j)]], store_sem.at[k])
                          for k, dv in enumerate((da_vmem, db_vmem))]
                for s in stores:
                    s.start()
                for s in stores:
                    s.wait()

        idx_spec = pl.BlockSpec((1, SC_INDEX_WINDOW), lambda i: (0, i))
        pltpu.emit_pipeline(
            body,
            grid=(t // SC_INDEX_WINDOW,),
            in_specs=[idx_spec, idx_spec, idx_spec],
            out_specs=[],
            core_axis_name=("core", "subcore"),
            dimension_semantics=(pltpu.PARALLEL,),
        )(id_hbm, da_hbm, db_hbm)

    return run(xn, jnp.arange(t, dtype=jnp.int32).reshape(1, t), dest_a, dest_b)


def moe_gather_sc(ys, dest_a, dest_b):
    d = ys.shape[1]
    t = dest_a.shape[1]
    out = jax.ShapeDtypeStruct((t, d), ys.dtype)

    n_moves = 2 * (SC_INDEX_WINDOW // SC_WINDOW)

    @pl.kernel(out_type=(out, out), mesh=_sc_mesh(),
               scratch_types=[pltpu.VMEM((2, SC_WINDOW, d), ys.dtype), pltpu.SemaphoreType.DMA((2,)),
                              pltpu.SemaphoreType.DMA((2,))],
               name="moe_gather_sc")
    def run(y_hbm, id_hbm, da_hbm, db_hbm, ga_hbm, gb_hbm, buf, load_sem, store_sem):
        def body(id_vmem, da_vmem, db_vmem):
            part = lambda m: pl.ds((m // 2) * SC_WINDOW, SC_WINDOW)
            src = lambda m: (da_vmem, db_vmem)[m % 2]
            dst = lambda m: (ga_hbm, gb_hbm)[m % 2]
            load = lambda m: pltpu.make_async_copy(y_hbm.at[src(m).at[0, part(m)]], buf.at[m % 2], load_sem.at[m % 2])
            store = lambda m: pltpu.make_async_copy(buf.at[m % 2], dst(m).at[id_vmem.at[0, part(m)]],
                                                    store_sem.at[m % 2])
            load(0).start()
            for m in range(n_moves):
                load(m).wait()
                if m >= 1:
                    store(m - 1).wait()
                if m + 1 < n_moves:
                    load(m + 1).start()
                store(m).start()
            store(n_moves - 1).wait()

        idx_spec = pl.BlockSpec((1, SC_INDEX_WINDOW), lambda i: (0, i))
        pltpu.emit_pipeline(
            body,
            grid=(t // SC_INDEX_WINDOW,),
            in_specs=[idx_spec, idx_spec, idx_spec],
            out_specs=[],
            core_axis_name=("core", "subcore"),
            dimension_semantics=(pltpu.PARALLEL,),
        )(id_hbm, da_hbm, db_hbm)

    return run(ys, jnp.arange(t, dtype=jnp.int32).reshape(1, t), dest_a, dest_b)


def _combine_dense_kernel(h_ref, rec_ref, ga_ref, gb_ref, *out_refs, split):
    i = pl.program_id(0)
    rec = rec_ref[...]
    res = h_ref[...] + rec[:, 4:5] * _unpack_halves(ga_ref[...]) + rec[:, 5:6] * _unpack_halves(gb_ref[...])
    if not split:
        out_refs[0][...] = res
    else:
        @pl.when(i < N_PROMPT_TILES)
        def _():
            out_refs[0][...] = res

        @pl.when((i >= N_PROMPT_TILES) & (i < N_PROMPT_TILES + N_SAMPLE_TILES))
        def _():
            out_refs[1][...] = res


def moe_combine_dense(h, rec, ga, gb, split=False):
    t, d = h.shape
    row = lambda i: (i, 0)
    if split:
        out_shape = (jax.ShapeDtypeStruct((T_PROMPT, d), F32), jax.ShapeDtypeStruct((T_SAMPLE, d), F32))
        out_specs = tuple(_parts_specs(d)[:2])
    else:
        out_shape = jax.ShapeDtypeStruct((t, d), F32)
        out_specs = pl.BlockSpec((ROW_TILE, d), row)
    return pl.pallas_call(
        functools.partial(_combine_dense_kernel, split=split),
        out_shape=out_shape,
        grid=(t // ROW_TILE,),
        in_specs=[pl.BlockSpec((ROW_TILE, d), row), pl.BlockSpec((ROW_TILE, LANES), row),
                  pl.BlockSpec((ROW_TILE, d // 2), row), pl.BlockSpec((ROW_TILE, d // 2), row)],
        out_specs=out_specs,
        compiler_params=_cparams(("arbitrary",)),
        name="moe_combine_dense",
    )(h, rec, ga, gb)


def _combine_kv_q_kernel(h_ref, rec_ref, ga_ref, gb_ref, gkv_ref, wkv_ref, gq_ref, wq_ref, hmean_ref,
                         hgk_ref, hgq_ref, cos_ref, sina_ref, sinb_ref, ho_ref, k_ref, v_ref, q_ref):
    rec = rec_ref[...]
    h = h_ref[...] + rec[:, 4:5] * _unpack_halves(ga_ref[...]) + rec[:, 5:6] * _unpack_halves(gb_ref[...])
    ho_ref[...] = h
    xhat = h * lax.rsqrt(jnp.mean(h * h, axis=-1, keepdims=True) + RMS_EPS)
    tables = (cos_ref[...], sina_ref[...], sinb_ref[...])
    zkv = jnp.dot((xhat * gkv_ref[...]).astype(BF16), wkv_ref[...], preferred_element_type=F32)
    k_ref[...] = _head_norm_rope(zkv[:, :KV_DIM], hmean_ref, hgk_ref[...], *tables)
    v_ref[...] = zkv[:, KV_DIM:]
    zq = jnp.dot((xhat * gq_ref[...]).astype(BF16), wq_ref[...], preferred_element_type=F32)
    q = _head_norm_rope(zq, hmean_ref, hgq_ref[...], *tables)
    q_ref[...] = (q * HEAD_DIM ** -0.5).astype(q_ref.dtype)


def moe_combine_kv_q(h, rec, ga, gb, kv_gain, kv_w_bf16, q_gain, wq_bf16, hmean, k_hgain, q_hgain,
                     cos_t, sina_t, sinb_t):
    t, d = h.shape
    row = lambda i: (i, 0)
    fix = lambda i: (0, 0)
    rope = pl.BlockSpec((ROW_TILE, LANES), _rope_tile)
    return pl.pallas_call(
        _combine_kv_q_kernel,
        out_shape=(jax.ShapeDtypeStruct((t, d), F32), jax.ShapeDtypeStruct((t, KV_DIM), F32),
                   jax.ShapeDtypeStruct((t, KV_DIM), F32), jax.ShapeDtypeStruct((t, d), BF16)),
        grid=(t // ROW_TILE,),
        in_specs=[pl.BlockSpec((ROW_TILE, d), row), pl.BlockSpec((ROW_TILE, LANES), row),
                  pl.BlockSpec((ROW_TILE, d // 2), row), pl.BlockSpec((ROW_TILE, d // 2), row),
                  pl.BlockSpec((1, d), fix), pl.BlockSpec((d, 2 * KV_DIM), fix),
                  pl.BlockSpec((1, d), fix), pl.BlockSpec((d, d), fix),
                  pl.BlockSpec((NORM_SLAB, NORM_SLAB), fix), pl.BlockSpec((1, KV_DIM), fix), pl.BlockSpec((1, d), fix),
                  rope, rope, rope],
        out_specs=(pl.BlockSpec((ROW_TILE, d), row), pl.BlockSpec((ROW_TILE, KV_DIM), row),
                   pl.BlockSpec((ROW_TILE, KV_DIM), row), pl.BlockSpec((ROW_TILE, d), row)),
        compiler_params=_cparams(("parallel",)),
        name="moe_combine_kv_q",
    )(h, rec, ga, gb, kv_gain.reshape(1, d), kv_w_bf16, q_gain.reshape(1, d), wq_bf16, hmean, k_hgain, q_hgain,
      cos_t, sina_t, sinb_t)


def _ffn_kernel(wblk_ref, we_ref, wlo_ref, whi_ref, wnext_ref, wpar_ref, xs_ref, w13_ref, w2_ref, ys_ref,
                w13f, w2f, w13b, w2b, wsem, *, layer):
    w = pl.program_id(0)
    prev = jnp.maximum(w - 1, 0)
    first_visit = (w == 0) | (wblk_ref[w] != wblk_ref[prev])
    lo = wlo_ref[w]
    hi = whi_ref[w]

    def weight_copies(e, par):
        return (pltpu.make_async_copy(w13_ref.at[layer, e], w13f.at[par], wsem.at[par, 0]),
                pltpu.make_async_copy(w2_ref.at[layer, e], w2f.at[par], wsem.at[par, 1]))

    @pl.when(w == 0)
    def _():
        for c in weight_copies(we_ref[0], 0):
            c.start()

    def ffn(x):
        x = _unpack_halves(x).astype(BF16)
        cw = D_EXPERT // FFN_CHUNKS
        gate_up = []
        for c in range(FFN_CHUNKS):
            a = jnp.dot(x, w13b[:, c * cw:(c + 1) * cw], preferred_element_type=F32)
            u = jnp.dot(x, w13b[:, D_EXPERT + c * cw:D_EXPERT + (c + 1) * cw], preferred_element_type=F32)
            gate_up.append((a, u))
        hmid = jnp.concatenate([(_silu(a) * u).astype(BF16) for a, u in gate_up], axis=1)
        return _pack_halves(jnp.dot(hmid, w2b[...], preferred_element_type=F32))

    @pl.when(hi > lo)
    def _():
        @pl.when((w == 0) | (we_ref[w] != we_ref[prev]))
        def _():
            par = wpar_ref[w]
            for c in weight_copies(we_ref[w], par):
                c.wait()
            w13b[...] = w13f[par].astype(BF16)
            w2b[...] = w2f[par].astype(BF16)
            nxt = wnext_ref[w]

            @pl.when(nxt >= 0)
            def _():
                for c in weight_copies(nxt, 1 - par):
                    c.start()

        whole = (lo == 0) & (hi == EXPERT_BLOCK)

        @pl.when(whole)
        def _():
            ys_ref[...] = ffn(xs_ref[...])

        half = FFN_PART
        for p in range(EXPERT_BLOCK // FFN_PART):
            rows = slice(p * half, (p + 1) * half)
            touched = (lo < (p + 1) * half) & (hi > p * half)

            @pl.when(jnp.logical_not(whole) & touched)
            def _():
                y = ffn(xs_ref[rows, :])
                row = lax.broadcasted_iota(jnp.int32, y.shape, 0) + p * half
                mine = (row >= lo) & (row < hi)

                @pl.when(first_visit)
                def _():
                    ys_ref[rows, :] = jnp.where(mine, y, jnp.zeros_like(y))

                @pl.when(jnp.logical_not(first_visit))
                def _():
                    ys_ref[rows, :] = jnp.where(mine, y, ys_ref[rows, :])

            @pl.when(jnp.logical_not(whole) & jnp.logical_not(touched) & first_visit)
            def _():
                ys_ref[rows, :] = jnp.zeros((half, ys_ref.shape[1]), U32)


def moe_ffn(xs, work, w13_all, w2_all, layer):
    n_slots, dp = xs.shape
    d = 2 * dp
    n_work = work[0].shape[0]
    xmap = lambda w, *prefetch: (prefetch[0][w], 0)
    grid_spec = pltpu.PrefetchScalarGridSpec(
        num_scalar_prefetch=len(work),
        grid=(n_work,),
        in_specs=[pl.BlockSpec((EXPERT_BLOCK, dp), xmap),
                  pl.BlockSpec(memory_space=pl.ANY), pl.BlockSpec(memory_space=pl.ANY)],
        out_specs=pl.BlockSpec((EXPERT_BLOCK, dp), xmap),
        scratch_shapes=[pltpu.VMEM((2, d, 2 * D_EXPERT), F32), pltpu.VMEM((2, D_EXPERT, d), F32),
                        pltpu.VMEM((d, 2 * D_EXPERT), BF16), pltpu.VMEM((D_EXPERT, d), BF16),
                        pltpu.SemaphoreType.DMA((2, 2))],
    )
    return pl.pallas_call(
        functools.partial(_ffn_kernel, layer=layer),
        out_shape=jax.ShapeDtypeStruct((n_slots, dp), U32),
        grid_spec=grid_spec,
        compiler_params=_cparams(("arbitrary",)),
        name="moe_ffn",
    )(*work, xs, w13_all, w2_all)


def _ffn_work_items(cnt):
    n_slots = 2 * T_ALL
    n_blocks = n_slots // EXPERT_BLOCK
    n_work = n_blocks + N_EXPERTS - 1
    end = jnp.cumsum(cnt)
    start = end - cnt
    first_blk = start // EXPERT_BLOCK
    last_blk = jnp.maximum(end - 1, start) // EXPERT_BLOCK
    n_items = jnp.where(cnt > 0, last_blk - first_blk + 1, 0)
    item_end = jnp.cumsum(n_items)
    item_start = item_end - n_items
    w = jnp.arange(n_work, dtype=jnp.int32)
    used = w < item_end[-1]
    wq = jnp.minimum(w, item_end[-1] - 1)
    e = jnp.sum((item_end[:, None] <= wq[None, :]).astype(jnp.int32), axis=0)
    onehot = e[None, :] == jnp.arange(N_EXPERTS, dtype=jnp.int32)[:, None]
    of_e = lambda table: jnp.sum(jnp.where(onehot, table[:, None], 0), axis=0)
    blk = jnp.where(used, of_e(first_blk) + (w - of_e(item_start)), n_blocks - 1).astype(jnp.int32)
    lo = jnp.maximum(of_e(start), blk * EXPERT_BLOCK) - blk * EXPERT_BLOCK
    hi = jnp.minimum(of_e(end), (blk + 1) * EXPERT_BLOCK) - blk * EXPERT_BLOCK
    lo = jnp.where(used, lo, 0).astype(jnp.int32)
    hi = jnp.where(used, hi, 0).astype(jnp.int32)
    e_before = jnp.concatenate([jnp.full((1,), -1, jnp.int32), e[:-1]])
    change = used & (e != e_before)
    parity = ((jnp.cumsum(change.astype(jnp.int32)) - 1) % 2).astype(jnp.int32)
    far = jnp.int32(n_work)
    next_change = lax.cummin(jnp.where(change, w, far), axis=0, reverse=True)
    next_change = jnp.concatenate([next_change[1:], jnp.full((1,), far, jnp.int32)])
    e_next = jnp.sum(jnp.where(next_change[None, :] == w[:, None], e[:, None], 0), axis=0)
    e_next = jnp.where(next_change < far, e_next, -1).astype(jnp.int32)
    return start, (blk, e, lo, hi, e_next, parity)


def hier_moe_layer(a, w_out_bf16, x, layer, gain, w_group, b_group, w_expert, b_expert, w13_all, w2_all, utri,
                   finish):
    t = T_ALL
    pad = LANES - N_EXPERTS - N_GROUPS
    w_router = jnp.concatenate([w_expert, w_group, jnp.zeros((D_MODEL, pad), F32)], axis=1)
    b_router = jnp.concatenate([b_expert, b_group, jnp.zeros((pad,), F32)])[:ROUTE_ROWS].reshape(ROUTE_ROWS, 1)
    h, xn, rec, rect, counts = moe_route(a, w_out_bf16, x, gain, w_router, b_router, utri)

    cnt = counts[:N_EXPERTS, 0].astype(jnp.int32)
    start, work = _ffn_work_items(cnt)
    experts = jnp.arange(N_EXPERTS, dtype=jnp.int32)[:, None]

    def slot_of(e_row, rank_row):
        first = jnp.sum(jnp.where(e_row.astype(jnp.int32)[None, :] == experts, start[:, None], 0), axis=0)
        return (first + rank_row.astype(jnp.int32)).reshape(1, t)

    dest_a = slot_of(rect[0], rect[2])
    dest_b = slot_of(rect[1], rect[3])

    xs = moe_dispatch_sc(xn, dest_a, dest_b, 2 * t)
    ys = moe_ffn(xs, work, w13_all, w2_all, layer)
    ga, gb = moe_gather_sc(ys, dest_a, dest_b)
    return finish(h, rec, ga, gb)


def _rope_tables(pos):
    half = ROPE_DIM // 2
    lane = np.arange(LANES) % HEAD_DIM
    rotary = lane < ROPE_DIM
    inv = jnp.where(rotary, jnp.exp(-math.log(ROPE_THETA) * jnp.asarray(lane % half, F32) * (2.0 / ROPE_DIM)), 0.0)
    ang = pos.astype(F32)[:, None] * inv[None, :]
    cos, sin = jnp.cos(ang), jnp.sin(ang)
    first = jnp.asarray(lane < half)
    second = jnp.asarray(rotary & (lane >= half))
    return cos, jnp.where(first, -sin, 0.0), jnp.where(second, sin, 0.0)


def kernel(x_prompt, x_sample, state_hgrn, cache_k_win, cache_v_win, meta_tokens, a_norm, a_w_in, a_lower_logits, a_out_norm, a_w_out, kv_norm, kv_w, k_norm, b_norm, b_wq, b_q_norm, b_sinks, b_w_out, moe_norm, moe_w_group, moe_b_group, moe_w_expert, moe_b_expert, moe_w13, moe_w2):
    tail_rows = T_ALL - OFF_META
    x_parts = (x_prompt.reshape(T_PROMPT, D_MODEL), x_sample.reshape(T_SAMPLE, D_MODEL),
               jnp.concatenate([meta_tokens.astype(F32), jnp.zeros((tail_rows - N_META, D_MODEL), F32)], axis=0))
    pos = jnp.concatenate([N_META + jnp.arange(SEQ, dtype=jnp.int32),
                           jnp.tile(PAST_LEN + jnp.arange(DEC_SEQ, dtype=jnp.int32), ROW_TILE // DEC_SEQ),
                           jnp.arange(N_META, dtype=jnp.int32),
                           jnp.zeros((ROW_TILE - N_META,), jnp.int32)])
    cos_t, sina_t, sinb_t = _rope_tables(pos)
    rs = np.arange(NORM_SLAB)
    hmean = jnp.asarray((rs[:, None] // HEAD_DIM == rs[None, :] // HEAD_DIM).astype(np.float32) / HEAD_DIM, BF16)
    rt = np.arange(ROW_TILE)
    utri = jnp.asarray((rt[:, None] < rt[None, :]).astype(np.float32), BF16)
    lower = jnp.cumsum(jax.nn.softmax(a_lower_logits.astype(F32), axis=0), axis=0)

    moe = functools.partial(hier_moe_layer, w13_all=moe_w13, w2_all=moe_w2, utri=utri)

    z = in_project(x_parts, a_norm[0], a_w_in[0].astype(BF16))
    zero_state = jnp.zeros((1, A_HEADS, A_DK, A_DV), F32)
    o_meta, s_meta = hgrn2_scan(z, zero_state, lower[0], a_out_norm[0],
                                row_off=OFF_META, n_seq=1, seq_len=N_META)
    o_prompt, s_prompt = hgrn2_scan(z, s_meta, lower[0], a_out_norm[0], row_off=0, n_seq=BATCH, seq_len=SEQ)
    o_sample, s_sample = hgrn2_scan(z, state_hgrn[0].astype(F32), lower[0], a_out_norm[0],
                                    row_off=OFF_SAMPLE, n_seq=DEC_BATCH, seq_len=DEC_SEQ, group=SCAN_SAMPLE_GROUP)
    o_tail = jnp.concatenate([o_meta, jnp.zeros((tail_rows - N_META, D_MODEL), BF16)], axis=0)
    finish0 = functools.partial(
        moe_combine_kv_q, kv_gain=kv_norm, kv_w_bf16=kv_w.astype(BF16), q_gain=b_norm[0], wq_bf16=b_wq[0].astype(BF16),
        hmean=hmean, k_hgain=jnp.tile(k_norm, KV_HEADS).reshape(1, KV_DIM),
        q_hgain=jnp.tile(b_q_norm[0], Q_HEADS).reshape(1, D_MODEL), cos_t=cos_t, sina_t=sina_t, sinb_t=sinb_t)
    h, k_all, v_all, q_all = moe((o_prompt, o_sample, o_tail), a_w_out[0].astype(BF16), x_parts, 0, moe_norm[0],
                                 moe_w_group[0], moe_b_group[0], moe_w_expert[0], moe_b_expert[0], finish=finish0)

    meta_blk = lambda a: jnp.concatenate([jnp.zeros((ATT_BLOCK - N_META, KV_DIM), F32),
                                          a[OFF_META:OFF_META + N_META]], axis=0)
    sinks = b_sinks[0].astype(F32)
    att_all = attention_prompt(q_all, k_all, v_all, meta_blk(k_all), meta_blk(v_all), sinks)
    att_all, k_win_s, v_win_s = attention_sample(q_all, cache_k_win.astype(F32), cache_v_win.astype(F32),
                                                 k_all, v_all, sinks, att_all)
    y_p, y_s = moe(att_all, b_w_out[0].astype(BF16), h, 1, moe_norm[1], moe_w_group[1], moe_b_group[1],
                   moe_w_expert[1], moe_b_expert[1], finish=functools.partial(moe_combine_dense, split=True))

    y_prompt = y_p.reshape(BATCH, SEQ, D_MODEL)
    y_sample = y_s.reshape(DEC_BATCH, DEC_SEQ, D_MODEL)
    last = lambda a: jnp.stack([a[(b + 1) * SEQ - WINDOW:(b + 1) * SEQ] for b in range(BATCH)]).reshape(
        BATCH, WINDOW, KV_HEADS, HEAD_DIM)
    kp = last(k_all)
    vp = last(v_all)
    return (y_prompt, y_sample, s_prompt[None], s_sample[None], kp, vp, k_win_s, v_win_s)
```

```python
import functools
import math

import numpy as np
import jax
import jax.numpy as jnp
from jax import lax
from jax.experimental import pallas as pl
from jax.experimental.pallas import tpu as pltpu
from jax.experimental.pallas import tpu_sc as plsc

F32 = jnp.float32
BF16 = jnp.bfloat16
U32 = jnp.uint32

D_MODEL = 1024
BATCH = 4
SEQ = 4096
DEC_BATCH = 128
DEC_SEQ = 8
PAST_LEN = 8192
N_META = 16
A_HEADS = 8
A_DK = 128
A_DV = 128
Q_HEADS = 16
KV_HEADS = 4
HEAD_DIM = 64
KV_DIM = KV_HEADS * HEAD_DIM
WINDOW = 128
ROPE_DIM = 16
ROPE_THETA = 500000.0
N_GROUPS = 4
EXPERTS_PER_GROUP = 8
N_EXPERTS = 32
D_EXPERT = 512
RMS_EPS = 1e-6

LANES = 128
SUBLANES = 8
MXU_DIM = 256
VMEM_BYTES = 64 * 1024 * 1024
VMEM_LIMIT = VMEM_BYTES - 8 * 1024 * 1024
NORM_SLAB = MXU_DIM

ROW_TILE = 512
T_PROMPT = BATCH * SEQ
T_SAMPLE = DEC_BATCH * DEC_SEQ
OFF_SAMPLE = T_PROMPT
OFF_META = T_PROMPT + T_SAMPLE
T_REAL = OFF_META + N_META
T_ALL = -(-T_REAL // ROW_TILE) * ROW_TILE
N_TILES = T_ALL // ROW_TILE

SCAN_CHUNK = 128
SCAN_SAMPLE_GROUP = 16
ATT_BLOCK = 128
EXPERT_BLOCK = 512
FFN_PART = 256
FFN_CHUNKS = 2


def _cparams(sem):
    return pltpu.CompilerParams(dimension_semantics=sem, vmem_limit_bytes=VMEM_LIMIT)


def _nt_dot(a, b):
    return lax.dot_general(a, b, (((1,), (1,)), ((), ())), preferred_element_type=F32)


def _rms(x, gain):
    ms = jnp.mean(x * x, axis=-1, keepdims=True)
    return x * lax.rsqrt(ms + RMS_EPS) * gain


def _silu(x):
    return x * jax.nn.sigmoid(x)


def _pack_halves(x):
    w = x.shape[1] // 2
    hi = lax.bitcast_convert_type(x[:, :w].astype(BF16).astype(F32), U32)
    lo = lax.bitcast_convert_type(x[:, w:].astype(BF16).astype(F32), U32)
    return hi | (lo >> 16)


def _unpack_halves(p):
    hi = lax.bitcast_convert_type(p & jnp.uint32(0xFFFF0000), F32)
    lo = lax.bitcast_convert_type(p << 16, F32)
    return jnp.concatenate([hi, lo], axis=1)


N_PROMPT_TILES = T_PROMPT // ROW_TILE
N_SAMPLE_TILES = T_SAMPLE // ROW_TILE


def _parts_specs(width):
    return [pl.BlockSpec((ROW_TILE, width), lambda i: (jnp.minimum(i, N_PROMPT_TILES - 1), 0)),
            pl.BlockSpec((ROW_TILE, width), lambda i: (jnp.clip(i - N_PROMPT_TILES, 0, N_SAMPLE_TILES - 1), 0)),
            pl.BlockSpec((ROW_TILE, width), lambda i: (0, 0))]


def _pick_part(i, p_ref, s_ref, t_ref, dtype):
    return jnp.where(i < N_PROMPT_TILES, p_ref[...].astype(dtype),
                     jnp.where(i < N_PROMPT_TILES + N_SAMPLE_TILES, s_ref[...].astype(dtype),
                               t_ref[...].astype(dtype)))


def _in_proj_kernel(xp_ref, xs_ref, xt_ref, g_ref, w_ref, o_ref):
    x = _pick_part(pl.program_id(0), xp_ref, xs_ref, xt_ref, F32)
    xn = _rms(x, g_ref[...])
    o_ref[...] = jnp.dot(xn.astype(BF16), w_ref[...], preferred_element_type=F32)


def in_project(x_parts, gain, w_bf16):
    d, n = w_bf16.shape
    return pl.pallas_call(
        _in_proj_kernel,
        out_shape=jax.ShapeDtypeStruct((T_ALL, n), F32),
        grid=(N_TILES,),
        in_specs=_parts_specs(d) + [pl.BlockSpec((1, d), lambda i: (0, 0)),
                                    pl.BlockSpec((d, n), lambda i: (0, 0))],
        out_specs=pl.BlockSpec((ROW_TILE, n), lambda i: (i, 0)),
        compiler_params=_cparams(("parallel",)),
        name="in_project",
    )(*x_parts, gain.reshape(1, d), w_bf16)


def _head_norm_rope(y, hmean_ref, hgain, cos_t, sina_t, sinb_t):
    rows, width = y.shape
    sq = (y * y).astype(BF16)
    parts = []
    for s in range(width // NORM_SLAB):
        parts.append(jnp.dot(sq[:, s * NORM_SLAB:(s + 1) * NORM_SLAB], hmean_ref[...], preferred_element_type=F32))
    ms = parts[0] if len(parts) == 1 else jnp.concatenate(parts, axis=1)
    yn = y * lax.rsqrt(ms + RMS_EPS) * hgain
    reps = width // LANES
    cos_w = jnp.concatenate([cos_t] * reps, axis=1)
    sina_w = jnp.concatenate([sina_t] * reps, axis=1)
    sinb_w = jnp.concatenate([sinb_t] * reps, axis=1)
    half = ROPE_DIM // 2
    nxt = pltpu.roll(yn, width - half, 1)
    prv = pltpu.roll(yn, half, 1)
    return yn * cos_w + nxt * sina_w + prv * sinb_w


def _rope_tile(i):
    tiles_per_seq = SEQ // ROW_TILE
    n_prompt_tiles = T_PROMPT // ROW_TILE
    n_sample_tiles = T_SAMPLE // ROW_TILE
    return (jnp.where(i < n_prompt_tiles, i % tiles_per_seq,
                      jnp.where(i < n_prompt_tiles + n_sample_tiles, tiles_per_seq, tiles_per_seq + 1)), 0)


def _scan_levels(c):
    levels = []
    m = c
    while m >= 2:
        levels.append(m)
        m //= 2
    return levels


LOG2E = 1.4426950408889634


def _scan_kernel(z_ref, s0_ref, lb_ref, og_ref, tri_ref, lmask_ref, sgn_ref, o_ref, sfin_ref, s_scr, b_scr,
                 *, rows, seq_len):
    c_idx = pl.program_id(1)
    levels = _scan_levels(seq_len)
    n_sub = rows // seq_len
    hk = A_HEADS * A_DK

    @pl.when(c_idx == 0)
    def _():
        s_scr[...] = s0_ref[...]

    sub = lax.broadcasted_iota(jnp.int32, (SUBLANES, LANES), 0)
    row = lax.broadcasted_iota(jnp.int32, (LANES, LANES), 0)
    og = og_ref[...]

    def pad_f32(x):
        if x.shape[0] == LANES:
            return x
        return jnp.concatenate([x, jnp.zeros((LANES - x.shape[0], x.shape[1]), x.dtype)], axis=0)

    def pad_rows(x):
        return pad_f32(x).astype(BF16)

    def cols(part, h):
        return slice(part * hk + h * LANES, part * hk + (h + 1) * LANES)

    def gates(h):
        lb = lb_ref[:, cols(0, h)]
        forget = lb + (1.0 - lb) * jax.nn.sigmoid(z_ref[:, cols(1, h)])
        logf = jnp.log(forget)
        hi = logf.astype(BF16).astype(F32)
        r1 = logf - hi
        mid = r1.astype(BF16).astype(F32)
        lo = r1 - mid
        cs = jnp.dot(tri_ref[...], pad_rows(jnp.concatenate([hi, mid, lo], axis=1)),
                     preferred_element_type=F32)
        b = (cs[:rows, :LANES] + cs[:rows, LANES:2 * LANES]) + cs[:rows, 2 * LANES:]
        b_scr[h] = b
        return _silu(z_ref[:, cols(0, h)]), 1.0 - forget, b

    def bref_for(h, m):
        b_rows = b_scr.at[h]
        half = m // 2
        pieces = []
        for g in range(rows // SUBLANES):
            base = g * SUBLANES
            if m >= SUBLANES:
                r = (base // m) * m + half - 1
                piece = jnp.broadcast_to(b_rows[r:r + 1, :], (SUBLANES, LANES))
            else:
                piece = jnp.broadcast_to(b_rows[base + half - 1:base + half, :], (SUBLANES, LANES))
                for blk in range(1, SUBLANES // m):
                    r = base + blk * m + half - 1
                    piece = jnp.where(sub >= blk * m,
                                      jnp.broadcast_to(b_rows[r:r + 1, :], (SUBLANES, LANES)), piece)
            pieces.append(piece)
        return pieces[0] if len(pieces) == 1 else jnp.concatenate(pieces, axis=0)

    heads = range(A_HEADS)
    qkb = [gates(h) for h in heads]
    att = [_nt_dot(pad_rows(qf), pad_rows(kf)) * lmask_ref[len(levels)] for qf, kf, _ in qkb]
    for li, m in enumerate(levels):
        for h in heads:
            qf, kf, b = qkb[h]
            sgn = sgn_ref[li]
            e = jnp.exp2((b - bref_for(h, m)) * sgn)
            w = pad_rows(jnp.where(sgn > 0, qf, kf) * e)
            att[h] = att[h] + _nt_dot(w, w) * lmask_ref[li]

    def finish(h):
        qf, kf, b = qkb[h]
        b_rows = b_scr.at[h]
        v_b = pad_rows(z_ref[:, cols(2, h)])
        o_intra = jnp.dot(att[h].astype(BF16), v_b, preferred_element_type=F32)
        eb = jnp.exp(b)
        qs = qf * eb
        b_end = [jnp.broadcast_to(b_rows[(i + 1) * seq_len - 1:(i + 1) * seq_len, :], (seq_len, LANES))
                 for i in range(n_sub)]
        b_end = b_end[0] if n_sub == 1 else jnp.concatenate(b_end, axis=0)
        kd_t = pad_f32(kf * jnp.exp(b_end - b)).T.astype(BF16)
        eb_t = pad_f32(eb).T
        qs_b = pad_rows(qs)
        o = o_intra
        for i in range(n_sub):
            s_old = s_scr[i, h]
            first, last = i * seq_len, (i + 1) * seq_len - 1
            if n_sub == 1:
                qs_i, v_i = qs_b, v_b
            else:
                mine = (row >= first) & (row <= last)
                qs_i = jnp.where(mine, qs_b, jnp.zeros_like(qs_b))
                v_i = jnp.where(mine, v_b, jnp.zeros_like(v_b))
            o = o + jnp.dot(qs_i, s_old.astype(BF16), preferred_element_type=F32)
            decay = jnp.broadcast_to(eb_t[:, last:last + 1], (LANES, LANES))
            s_scr[i, h] = decay * s_old + jnp.dot(kd_t, v_i, preferred_element_type=F32)
        o = o[:rows]

        on = _rms(o, og) * _silu(z_ref[:, cols(3, h)])
        o_ref[:, cols(0, h)] = on.astype(o_ref.dtype)

    for h in heads:
        finish(h)

    @pl.when(c_idx == pl.num_programs(1) - 1)
    def _():
        sfin_ref[...] = s_scr[...]


def _scan_consts(rows, seq_len):
    levels = _scan_levels(seq_len)
    r = np.arange(LANES)
    t, s = r[:, None], r[None, :]
    live = (t < rows) & (s < rows)
    tri = ((s <= t) & (t // seq_len == s // seq_len) & live).astype(np.float32)
    masks, sgns = [], []
    for m in levels:
        masks.append(((t // m == s // m) & (t % m >= m // 2) & (s % m < m // 2) & live).astype(np.float32))
        sgns.append(np.broadcast_to(np.where(r[:rows, None] % m >= m // 2, LOG2E, -LOG2E), (rows, LANES)))
    masks.append(((t == s) & live).astype(np.float32))
    return jnp.asarray(tri, BF16), jnp.asarray(np.stack(masks), F32), jnp.asarray(np.stack(sgns), F32)


def hgrn2_scan(z, s0, lb, o_gain, *, row_off, n_seq, seq_len, group=1):
    hv = A_HEADS * A_DV
    if seq_len > SCAN_CHUNK:
        assert group == 1
        sub_len, rows, n_chunks, n_steps = SCAN_CHUNK, SCAN_CHUNK, seq_len // SCAN_CHUNK, n_seq
    else:
        sub_len, rows, n_chunks, n_steps = seq_len, group * seq_len, 1, n_seq // group
    blk_off = row_off // rows
    tri, lmask, sgn = _scan_consts(rows, sub_len)
    shared_s0 = s0.shape[0] == 1
    fix2 = lambda s, c: (0, 0)
    fix3 = lambda s, c: (0, 0, 0)
    o, sfin = pl.pallas_call(
        functools.partial(_scan_kernel, rows=rows, seq_len=sub_len),
        out_shape=(jax.ShapeDtypeStruct((n_seq * seq_len, hv), BF16 if rows % 16 == 0 else F32),
                   jax.ShapeDtypeStruct((n_seq, A_HEADS, A_DK, A_DV), F32)),
        grid=(n_steps, n_chunks),
        in_specs=[pl.BlockSpec((rows, 4 * hv), lambda s, c: (blk_off + s * n_chunks + c, 0)),
                  pl.BlockSpec((group, A_HEADS, A_DK, A_DV), (lambda s, c: (0, 0, 0, 0)) if shared_s0
                               else (lambda s, c: (s, 0, 0, 0))),
                  pl.BlockSpec((1, hv), fix2), pl.BlockSpec((1, A_DV), fix2),
                  pl.BlockSpec((LANES, LANES), fix2), pl.BlockSpec(lmask.shape, fix3),
                  pl.BlockSpec(sgn.shape, fix3)],
        out_specs=(pl.BlockSpec((rows, hv), lambda s, c: (s * n_chunks + c, 0)),
                   pl.BlockSpec((group, A_HEADS, A_DK, A_DV), lambda s, c: (s, 0, 0, 0))),
        scratch_shapes=[pltpu.VMEM((group, A_HEADS, A_DK, A_DV), F32), pltpu.VMEM((A_HEADS, rows, LANES), F32)],
        compiler_params=_cparams(("parallel", "arbitrary")),
        name=f"hgrn2_scan_r{rows}",
    )(z, s0, lb.reshape(1, hv), o_gain.reshape(1, A_DV), tri, lmask, sgn)
    return o, sfin


KEYS = 2 * ATT_BLOCK
ATT_STEP = 1


def _pair_operand(x, kh):
    slab = x[:, (kh // 2) * LANES:(kh // 2 + 1) * LANES]
    lane = lax.broadcasted_iota(jnp.int32, slab.shape, 1)
    if kh % 2 == 0:
        lo = jnp.where(lane < HEAD_DIM, slab, 0.0)
        hi = pltpu.roll(lo, HEAD_DIM, 1)
    else:
        hi = jnp.where(lane >= HEAD_DIM, slab, 0.0)
        lo = pltpu.roll(hi, HEAD_DIM, 1)
    return jnp.concatenate([lo, hi], axis=0).astype(BF16)


def _window_bias(rows, jmin):
    t_i = lax.broadcasted_iota(jnp.int32, (rows, 2 * KEYS), 0)
    c_i = lax.broadcasted_iota(jnp.int32, (rows, 2 * KEYS), 1)
    j_i = c_i & (ATT_BLOCK - 1)
    own = (c_i & ATT_BLOCK) != 0
    ok = (own & (j_i <= t_i)) | (jnp.logical_not(own) & (j_i >= t_i) & (j_i >= jmin))
    return jnp.where(ok, 0.0, -jnp.inf).astype(F32)


def _pair_softmax(s, sink_a, sink_b):
    probs, rinv = [], []
    for hh, sink in enumerate((sink_a, sink_b)):
        sh = s[:, hh * KEYS:(hh + 1) * KEYS]
        m = jnp.maximum(jnp.max(sh, axis=-1, keepdims=True), sink)
        p = jnp.exp(sh - m)
        den = jnp.sum(p, axis=-1, keepdims=True) + jnp.exp(sink - m)
        probs.append(p.astype(BF16))
        rinv.append(1.0 / den)
    lane = lax.broadcasted_iota(jnp.int32, (s.shape[0], LANES), 1)
    return jnp.concatenate(probs, axis=1), jnp.where(lane < HEAD_DIM, rinv[0], rinv[1])


def _attn_prompt_kernel(sink_ref, q_ref, kp_ref, ko_ref, vp_ref, vo_ref, km_ref, vm_ref, o_ref,
                        k2_scr, v2_scr, s_scr, p_scr, r_scr, bias_scr):
    n = pl.program_id(0)
    steps_per_seq = SEQ // (ATT_STEP * ATT_BLOCK)
    n_pairs = Q_HEADS // 2
    blk = ATT_BLOCK

    @pl.when(n == 0)
    def _():
        bias_scr[0] = _window_bias(ATT_BLOCK, 0)
        bias_scr[1] = _window_bias(ATT_BLOCK, ATT_BLOCK - N_META)

    @pl.when(n >= BATCH * steps_per_seq)
    def _():
        o_ref[...] = jnp.zeros_like(o_ref)

    @pl.when(n < BATCH * steps_per_seq)
    def _():
        first = (n % steps_per_seq) == 0
        own_k, own_v = ko_ref[...], vo_ref[...]
        keys = [jnp.where(first, km_ref[...], kp_ref[...])] + [own_k[u * blk:(u + 1) * blk] for u in range(ATT_STEP)]
        vals = [jnp.where(first, vm_ref[...], vp_ref[...])] + [own_v[u * blk:(u + 1) * blk] for u in range(ATT_STEP)]
        biases = [bias_scr[first.astype(jnp.int32)]] + [bias_scr[0]] * (ATT_STEP - 1)
        for u in range(ATT_STEP):
            k = jnp.concatenate(keys[u:u + 2], axis=0)
            v = jnp.concatenate(vals[u:u + 2], axis=0)
            for kh in range(KV_HEADS):
                k2_scr[u, kh] = _pair_operand(k, kh)
                v2_scr[u, kh] = _pair_operand(v, kh)
        units = [(u, pair) for u in range(ATT_STEP) for pair in range(n_pairs)]
        rows = lambda u: slice(u * blk, (u + 1) * blk)
        lanes = lambda pair: slice(pair * LANES, (pair + 1) * LANES)
        for j, (u, pair) in enumerate(units):
            s_scr[j] = _nt_dot(q_ref[rows(u), lanes(pair)], k2_scr[u, pair // 2]) + biases[u]
        for j, (u, pair) in enumerate(units):
            p, rinv = _pair_softmax(s_scr[j], sink_ref[2 * pair], sink_ref[2 * pair + 1])
            p_scr[j] = p
            r_scr[j] = rinv
        for j, (u, pair) in enumerate(units):
            o = jnp.dot(p_scr[j], v2_scr[u, pair // 2], preferred_element_type=F32) * r_scr[j]
            o_ref[rows(u), lanes(pair)] = o.astype(o_ref.dtype)


def attention_prompt(q_all, k_all, v_all, k_meta_blk, v_meta_blk, sinks):
    step_rows = ATT_STEP * ATT_BLOCK
    n_prompt_steps = T_PROMPT // step_rows
    n_steps = T_ALL // step_rows
    n_units = ATT_STEP * (Q_HEADS // 2)
    own = lambda n, sk: (jnp.minimum(n, n_prompt_steps - 1), 0)
    prev = lambda n, sk: (jnp.maximum(ATT_STEP * jnp.minimum(n, n_prompt_steps - 1) - 1, 0), 0)
    fix = lambda n, sk: (0, 0)
    grid_spec = pltpu.PrefetchScalarGridSpec(
        num_scalar_prefetch=1,
        grid=(n_steps,),
        in_specs=[pl.BlockSpec((step_rows, D_MODEL), own),
                  pl.BlockSpec((ATT_BLOCK, KV_DIM), prev), pl.BlockSpec((step_rows, KV_DIM), own),
                  pl.BlockSpec((ATT_BLOCK, KV_DIM), prev), pl.BlockSpec((step_rows, KV_DIM), own),
                  pl.BlockSpec((ATT_BLOCK, KV_DIM), fix), pl.BlockSpec((ATT_BLOCK, KV_DIM), fix)],
        out_specs=pl.BlockSpec((step_rows, D_MODEL), lambda n, sk: (n, 0)),
        scratch_shapes=[pltpu.VMEM((ATT_STEP, KV_HEADS, 2 * KEYS, LANES), BF16),
                        pltpu.VMEM((ATT_STEP, KV_HEADS, 2 * KEYS, LANES), BF16),
                        pltpu.VMEM((n_units, ATT_BLOCK, 2 * KEYS), F32),
                        pltpu.VMEM((n_units, ATT_BLOCK, 2 * KEYS), BF16),
                        pltpu.VMEM((n_units, ATT_BLOCK, LANES), F32),
                        pltpu.VMEM((2, ATT_BLOCK, 2 * KEYS), F32)],
    )
    return pl.pallas_call(
        _attn_prompt_kernel,
        out_shape=jax.ShapeDtypeStruct((T_ALL, D_MODEL), BF16),
        grid_spec=grid_spec,
        compiler_params=_cparams(("arbitrary",)),
        name="attention_prompt",
    )(sinks, q_all, k_all, k_all, v_all, v_all, k_meta_blk, v_meta_blk)


SAMPLE_GROUP = ATT_BLOCK // DEC_SEQ
SAMPLE_UNROLL = 4


def _attn_sample_kernel(sink_ref, q_ref, ck_ref, cv_ref, kn_ref, vn_ref, buf_ref, o_ref, kw_ref, vw_ref,
                        qf_scr, of_scr, k2_scr, v2_scr):
    del buf_ref
    qrows = 2 * DEC_SEQ
    qf_scr[...] = q_ref[...].astype(F32)
    bias = _window_bias(qrows, 0)
    zq = jnp.zeros((qrows - DEC_SEQ, D_MODEL), F32)
    zk = jnp.zeros((ATT_BLOCK - DEC_SEQ, KV_DIM), F32)

    n_pairs = Q_HEADS // 2
    lanes_of = lambda pair: slice(pair * LANES, (pair + 1) * LANES)

    def seq_group_body(it, carry):
        seqs = tuple(SAMPLE_UNROLL * it + u for u in range(SAMPLE_UNROLL))
        r_new = [pl.multiple_of(i * DEC_SEQ, DEC_SEQ) for i in seqs]
        qs = []
        for u, i in enumerate(seqs):
            qs.append(jnp.concatenate([qf_scr[pl.ds(r_new[u], DEC_SEQ), :], zq], axis=0).astype(BF16))
            old = lambda c_ref: jnp.concatenate([c_ref[i, :, kh, :] for kh in range(KV_HEADS)], axis=1)
            k = jnp.concatenate([old(ck_ref), kn_ref[pl.ds(r_new[u], DEC_SEQ), :], zk], axis=0)
            v = jnp.concatenate([old(cv_ref), vn_ref[pl.ds(r_new[u], DEC_SEQ), :], zk], axis=0)
            for c_ref, n_ref, w_ref in ((ck_ref, kn_ref, kw_ref), (cv_ref, vn_ref, vw_ref)):
                w_ref[i, 0:WINDOW - DEC_SEQ] = c_ref[i, DEC_SEQ:WINDOW]
                for kh in range(KV_HEADS):
                    w_ref[i, WINDOW - DEC_SEQ:WINDOW, kh, :] = n_ref[pl.ds(r_new[u], DEC_SEQ),
                                                                     kh * HEAD_DIM:(kh + 1) * HEAD_DIM]
            for kh in range(KV_HEADS):
                k2_scr[u, kh] = _pair_operand(k, kh)
                v2_scr[u, kh] = _pair_operand(v, kh)
        scores = [[_nt_dot(qs[u][:, lanes_of(pair)], k2_scr[u, pair // 2]) + bias for pair in range(n_pairs)]
                  for u in range(SAMPLE_UNROLL)]
        soft = [[_pair_softmax(s, sink_ref[2 * pair], sink_ref[2 * pair + 1]) for pair, s in enumerate(scores[u])]
                for u in range(SAMPLE_UNROLL)]
        for u in range(SAMPLE_UNROLL):
            for pair, (p, rinv) in enumerate(soft[u]):
                o = jnp.dot(p, v2_scr[u, pair // 2], preferred_element_type=F32) * rinv
                of_scr[pl.ds(r_new[u], DEC_SEQ), lanes_of(pair)] = o[:DEC_SEQ]
        return carry

    lax.fori_loop(0, SAMPLE_GROUP // SAMPLE_UNROLL, seq_group_body, 0)
    o_ref[...] = of_scr[...].astype(o_ref.dtype)


def attention_sample(q_all, cache_k, cache_v, k_all, v_all, sinks, out_buf):
    first_blk = OFF_SAMPLE // ATT_BLOCK
    new = lambda g, sk: (first_blk + g, 0)
    old = pl.BlockSpec((SAMPLE_GROUP, WINDOW, KV_HEADS, HEAD_DIM), lambda g, sk: (g, 0, 0, 0))
    grid_spec = pltpu.PrefetchScalarGridSpec(
        num_scalar_prefetch=1,
        grid=(DEC_BATCH // SAMPLE_GROUP,),
        in_specs=[pl.BlockSpec((ATT_BLOCK, D_MODEL), new),
                  old, old,
                  pl.BlockSpec((ATT_BLOCK, KV_DIM), new), pl.BlockSpec((ATT_BLOCK, KV_DIM), new),
                  pl.BlockSpec(memory_space=pl.ANY)],
        out_specs=(pl.BlockSpec((ATT_BLOCK, D_MODEL), new), old, old),
        scratch_shapes=[pltpu.VMEM((ATT_BLOCK, D_MODEL), F32), pltpu.VMEM((ATT_BLOCK, D_MODEL), F32),
                        pltpu.VMEM((SAMPLE_UNROLL, KV_HEADS, 2 * KEYS, LANES), BF16),
                        pltpu.VMEM((SAMPLE_UNROLL, KV_HEADS, 2 * KEYS, LANES), BF16)],
    )
    window = jax.ShapeDtypeStruct(cache_k.shape, cache_k.dtype)
    return pl.pallas_call(
        _attn_sample_kernel,
        out_shape=(jax.ShapeDtypeStruct(out_buf.shape, out_buf.dtype), window, window),
        grid_spec=grid_spec,
        input_output_aliases={6: 0},
        compiler_params=_cparams(("parallel",)),
        name="attention_sample",
    )(sinks, q_all, cache_k, cache_v, k_all, v_all, out_buf)


ROUTE_COLS = 8
ROUTE_ROWS = 48


def _route_kernel(*refs, parts):
    i = pl.program_id(0)
    if parts:
        (ap_ref, as_ref, at_ref, w_ref, xp_ref, xs_ref, xt_ref), refs = refs[:7], refs[7:]
        a = _pick_part(i, ap_ref, as_ref, at_ref, BF16)
        x = _pick_part(i, xp_ref, xs_ref, xt_ref, F32)
    else:
        (a_ref, w_ref, x_ref), refs = refs[:3], refs[3:]
        a, x = a_ref[...], x_ref[...]
    g_ref, whl_ref, br_ref, utri_ref, h_ref, xn_ref, rec_ref, rect_ref, cnt_ref, cnt_scr = refs

    @pl.when(i == 0)
    def _():
        cnt_scr[...] = jnp.zeros_like(cnt_scr)

    h = x + jnp.dot(a, w_ref[...], preferred_element_type=F32)
    h_ref[...] = h
    xn = _rms(h, g_ref[...])
    xn_ref[...] = _pack_halves(xn)
    xh = xn.astype(BF16)
    xl = (xn - xh.astype(F32)).astype(BF16)
    by_xh = _nt_dot(whl_ref[...], xh)
    logits = (by_xh[:LANES] + (by_xh[LANES:] + _nt_dot(whl_ref[:LANES, :], xl)))[:ROUTE_ROWS]
    logits = logits + br_ref[...]
    tokens = logits.shape[1]
    rid = lax.broadcasted_iota(jnp.int32, (ROUTE_ROWS, tokens), 0).astype(F32)
    neg = jnp.float32(-jnp.inf)
    big = jnp.float32(ROUTE_ROWS)

    is_g = (rid >= N_EXPERTS) & (rid < N_EXPERTS + N_GROUPS)
    gl = jnp.where(is_g, logits, neg)
    gmax = jnp.max(gl, axis=0, keepdims=True)
    gsel = jnp.min(jnp.where(gl == gmax, rid, big), axis=0, keepdims=True) - N_EXPERTS
    gden = jnp.sum(jnp.where(is_g, jnp.exp(gl - gmax), 0.0), axis=0, keepdims=True)
    gw = 1.0 / gden

    in_grp = (rid >= gsel * EXPERTS_PER_GROUP) & (rid < (gsel + 1) * EXPERTS_PER_GROUP)
    el = jnp.where(in_grp, logits, neg)
    t1 = jnp.max(el, axis=0, keepdims=True)
    e1 = jnp.min(jnp.where(el == t1, rid, big), axis=0, keepdims=True)
    el2 = jnp.where(rid == e1, neg, el)
    t2 = jnp.max(el2, axis=0, keepdims=True)
    e2 = jnp.min(jnp.where(el2 == t2, rid, big), axis=0, keepdims=True)
    x2 = jnp.exp(t2 - t1)
    w1 = gw / (1.0 + x2)
    w2 = gw * x2 / (1.0 + x2)

    oh1 = (rid == e1).astype(F32)
    oh2 = (rid == e2).astype(F32)
    oh = oh1 + oh2
    before = jnp.dot(oh.astype(BF16), utri_ref[...], preferred_element_type=F32)
    base = cnt_scr[...] + before
    r1 = jnp.sum(base * oh1, axis=0, keepdims=True)
    r2 = jnp.sum(base * oh2, axis=0, keepdims=True)
    cnt_scr[...] = cnt_scr[...] + jnp.sum(oh, axis=1, keepdims=True)

    zero = jnp.zeros_like(w1)
    rect = jnp.concatenate([e1, e2, r1, r2, w1, w2, zero, zero], axis=0)
    rect_ref[...] = rect
    wide = jnp.concatenate([rect, jnp.zeros((LANES - ROUTE_COLS, tokens), F32)], axis=0)
    rec_ref[...] = jnp.concatenate([wide[:, t0:t0 + LANES].T for t0 in range(0, tokens, LANES)], axis=0)
    cnt_ref[...] = cnt_scr[...]


def moe_route(a, w_out_bf16, x, gain, w_router, b_router, utri):
    parts = isinstance(a, tuple)
    t, d = T_ALL, D_MODEL
    row = lambda i: (i, 0)
    fix = lambda i: (0, 0)
    w_t = w_router.T
    w_hi = w_t.astype(BF16)
    w_lo = (w_t - w_hi.astype(F32)).astype(BF16)
    w_spec = pl.BlockSpec((d, d), fix)
    if parts:
        pre_specs = _parts_specs(d) + [w_spec] + _parts_specs(d)
        pre_args = (*a, w_out_bf16, *x)
    else:
        pre_specs = [pl.BlockSpec((ROW_TILE, d), row), w_spec, pl.BlockSpec((ROW_TILE, d), row)]
        pre_args = (a, w_out_bf16, x)
    return pl.pallas_call(
        functools.partial(_route_kernel, parts=parts),
        out_shape=(jax.ShapeDtypeStruct((t, d), F32),
                   jax.ShapeDtypeStruct((t, d // 2), U32), jax.ShapeDtypeStruct((t, LANES), F32),
                   jax.ShapeDtypeStruct((ROUTE_COLS, t), F32), jax.ShapeDtypeStruct((ROUTE_ROWS, 1), F32)),
        grid=(t // ROW_TILE,),
        in_specs=pre_specs + [pl.BlockSpec((1, d), fix), pl.BlockSpec((2 * LANES, d), fix),
                              pl.BlockSpec((ROUTE_ROWS, 1), fix), pl.BlockSpec((ROW_TILE, ROW_TILE), fix)],
        out_specs=(pl.BlockSpec((ROW_TILE, d), row),
                   pl.BlockSpec((ROW_TILE, d // 2), row), pl.BlockSpec((ROW_TILE, LANES), row),
                   pl.BlockSpec((ROUTE_COLS, ROW_TILE), lambda i: (0, i)), pl.BlockSpec((ROUTE_ROWS, 1), fix)),
        scratch_shapes=[pltpu.VMEM((ROUTE_ROWS, 1), F32)],
        compiler_params=_cparams(("arbitrary",)),
        name="moe_route",
    )(*pre_args, gain.reshape(1, d), jnp.concatenate([w_hi, w_lo], axis=0), b_router, utri)


SC_WINDOW = 64
SC_INDEX_WINDOW = 128


def _sc_mesh():
    return plsc.VectorSubcoreMesh(core_axis_name="core", subcore_axis_name="subcore")


def moe_dispatch_sc(xn, dest_a, dest_b, n_slots):
    t, d = xn.shape

    n_parts = SC_INDEX_WINDOW // SC_WINDOW

    @pl.kernel(out_type=jax.ShapeDtypeStruct((n_slots, d), xn.dtype), mesh=_sc_mesh(),
               scratch_types=[pltpu.VMEM((2, SC_WINDOW, d), xn.dtype), pltpu.SemaphoreType.DMA((2,)),
                              pltpu.SemaphoreType.DMA((2,))],
               name="moe_dispatch_sc")
    def run(x_hbm, id_hbm, da_hbm, db_hbm, o_hbm, buf, load_sem, store_sem):
        def body(id_vmem, da_vmem, db_vmem):
            part = lambda j: pl.ds(j * SC_WINDOW, SC_WINDOW)
            load = lambda j: pltpu.make_async_copy(x_hbm.at[id_vmem.at[0, part(j)]], buf.at[j % 2], load_sem.at[j % 2])
            load(0).start()
            for j in range(n_parts):
                load(j).wait()
                if j + 1 < n_parts:
                    load(j + 1).start()
                stores = [pltpu.make_async_copy(buf.at[j % 2], o_hbm.at[dv.at[0, part(j)]], store_sem.at[k])
                          for k, dv in enumerate((da_vmem, db_vmem))]
                for s in stores:
                    s.start()
                for s in stores:
                    s.wait()

        idx_spec = pl.BlockSpec((1, SC_INDEX_WINDOW), lambda i: (0, i))
        pltpu.emit_pipeline(
            body,
            grid=(t // SC_INDEX_WINDOW,),
            in_specs=[idx_spec, idx_spec, idx_spec],
            out_specs=[],
            core_axis_name=("core", "subcore"),
            dimension_semantics=(pltpu.PARALLEL,),
        )(id_hbm, da_hbm, db_hbm)

    return run(xn, jnp.arange(t, dtype=jnp.int32).reshape(1, t), dest_a, dest_b)


def moe_gather_sc(ys, dest_a, dest_b):
    d = ys.shape[1]
    t = dest_a.shape[1]
    out = jax.ShapeDtypeStruct((t, d), ys.dtype)

    n_moves = 2 * (SC_INDEX_WINDOW // SC_WINDOW)

    @pl.kernel(out_type=(out, out), mesh=_sc_mesh(),
               scratch_types=[pltpu.VMEM((2, SC_WINDOW, d), ys.dtype), pltpu.SemaphoreType.DMA((2,)),
                              pltpu.SemaphoreType.DMA((2,))],
               name="moe_gather_sc")
    def run(y_hbm, id_hbm, da_hbm, db_hbm, ga_hbm, gb_hbm, buf, load_sem, store_sem):
        def body(id_vmem, da_vmem, db_vmem):
            part = lambda m: pl.ds((m // 2) * SC_WINDOW, SC_WINDOW)
            src = lambda m: (da_vmem, db_vmem)[m % 2]
            dst = lambda m: (ga_hbm, gb_hbm)[m % 2]
            load = lambda m: pltpu.make_async_copy(y_hbm.at[src(m).at[0, part(m)]], buf.at[m % 2], load_sem.at[m % 2])
            store = lambda m: pltpu.make_async_copy(buf.at[m % 2], dst(m).at[id_vmem.at[0, part(m)]],
                                                    store_sem.at[m % 2])
            load(0).start()
            for m in range(n_moves):
                load(m).wait()
                if m >= 1:
                    store(m - 1).wait()
                if m + 1 < n_moves:
                    load(m + 1).start()
                store(m).start()
            store(n_moves - 1).wait()

        idx_spec = pl.BlockSpec((1, SC_INDEX_WINDOW), lambda i: (0, i))
        pltpu.emit_pipeline(
            body,
            grid=(t // SC_INDEX_WINDOW,),
            in_specs=[idx_spec, idx_spec, idx_spec],
            out_specs=[],
            core_axis_name=("core", "subcore"),
            dimension_semantics=(pltpu.PARALLEL,),
        )(id_hbm, da_hbm, db_hbm)

    return run(ys, jnp.arange(t, dtype=jnp.int32).reshape(1, t), dest_a, dest_b)


def _combine_dense_kernel(h_ref, rec_ref, ga_ref, gb_ref, *out_refs, split):
    i = pl.program_id(0)
    rec = rec_ref[...]
    res = h_ref[...] + rec[:, 4:5] * _unpack_halves(ga_ref[...]) + rec[:, 5:6] * _unpack_halves(gb_ref[...])
    if not split:
        out_refs[0][...] = res
    else:
        @pl.when(i < N_PROMPT_TILES)
        def _():
            out_refs[0][...] = res

        @pl.when((i >= N_PROMPT_TILES) & (i < N_PROMPT_TILES + N_SAMPLE_TILES))
        def _():
            out_refs[1][...] = res


def moe_combine_dense(h, rec, ga, gb, split=False):
    t, d = h.shape
    row = lambda i: (i, 0)
    if split:
        out_shape = (jax.ShapeDtypeStruct((T_PROMPT, d), F32), jax.ShapeDtypeStruct((T_SAMPLE, d), F32))
        out_specs = tuple(_parts_specs(d)[:2])
    else:
        out_shape = jax.ShapeDtypeStruct((t, d), F32)
        out_specs = pl.BlockSpec((ROW_TILE, d), row)
    return pl.pallas_call(
        functools.partial(_combine_dense_kernel, split=split),
        out_shape=out_shape,
        grid=(t // ROW_TILE,),
        in_specs=[pl.BlockSpec((ROW_TILE, d), row), pl.BlockSpec((ROW_TILE, LANES), row),
                  pl.BlockSpec((ROW_TILE, d // 2), row), pl.BlockSpec((ROW_TILE, d // 2), row)],
        out_specs=out_specs,
        compiler_params=_cparams(("arbitrary",)),
        name="moe_combine_dense",
    )(h, rec, ga, gb)


def _combine_kv_q_kernel(h_ref, rec_ref, ga_ref, gb_ref, gkv_ref, wkv_ref, gq_ref, wq_ref, hmean_ref,
                         hgk_ref, hgq_ref, cos_ref, sina_ref, sinb_ref, ho_ref, k_ref, v_ref, q_ref):
    rec = rec_ref[...]
    h = h_ref[...] + rec[:, 4:5] * _unpack_halves(ga_ref[...]) + rec[:, 5:6] * _unpack_halves(gb_ref[...])
    ho_ref[...] = h
    xhat = h * lax.rsqrt(jnp.mean(h * h, axis=-1, keepdims=True) + RMS_EPS)
    tables = (cos_ref[...], sina_ref[...], sinb_ref[...])
    zkv = jnp.dot((xhat * gkv_ref[...]).astype(BF16), wkv_ref[...], preferred_element_type=F32)
    k_ref[...] = _head_norm_rope(zkv[:, :KV_DIM], hmean_ref, hgk_ref[...], *tables)
    v_ref[...] = zkv[:, KV_DIM:]
    zq = jnp.dot((xhat * gq_ref[...]).astype(BF16), wq_ref[...], preferred_element_type=F32)
    q = _head_norm_rope(zq, hmean_ref, hgq_ref[...], *tables)
    q_ref[...] = (q * HEAD_DIM ** -0.5).astype(q_ref.dtype)


def moe_combine_kv_q(h, rec, ga, gb, kv_gain, kv_w_bf16, q_gain, wq_bf16, hmean, k_hgain, q_hgain,
                     cos_t, sina_t, sinb_t):
    t, d = h.shape
    row = lambda i: (i, 0)
    fix = lambda i: (0, 0)
    rope = pl.BlockSpec((ROW_TILE, LANES), _rope_tile)
    return pl.pallas_call(
        _combine_kv_q_kernel,
        out_shape=(jax.ShapeDtypeStruct((t, d), F32), jax.ShapeDtypeStruct((t, KV_DIM), F32),
                   jax.ShapeDtypeStruct((t, KV_DIM), F32), jax.ShapeDtypeStruct((t, d), BF16)),
        grid=(t // ROW_TILE,),
        in_specs=[pl.BlockSpec((ROW_TILE, d), row), pl.BlockSpec((ROW_TILE, LANES), row),
                  pl.BlockSpec((ROW_TILE, d // 2), row), pl.BlockSpec((ROW_TILE, d // 2), row),
                  pl.BlockSpec((1, d), fix), pl.BlockSpec((d, 2 * KV_DIM), fix),
                  pl.BlockSpec((1, d), fix), pl.BlockSpec((d, d), fix),
                  pl.BlockSpec((NORM_SLAB, NORM_SLAB), fix), pl.BlockSpec((1, KV_DIM), fix), pl.BlockSpec((1, d), fix),
                  rope, rope, rope],
        out_specs=(pl.BlockSpec((ROW_TILE, d), row), pl.BlockSpec((ROW_TILE, KV_DIM), row),
                   pl.BlockSpec((ROW_TILE, KV_DIM), row), pl.BlockSpec((ROW_TILE, d), row)),
        compiler_params=_cparams(("parallel",)),
        name="moe_combine_kv_q",
    )(h, rec, ga, gb, kv_gain.reshape(1, d), kv_w_bf16, q_gain.reshape(1, d), wq_bf16, hmean, k_hgain, q_hgain,
      cos_t, sina_t, sinb_t)


def _ffn_kernel(wblk_ref, we_ref, wlo_ref, whi_ref, wnext_ref, wpar_ref, xs_ref, w13_ref, w2_ref, ys_ref,
                w13f, w2f, w13b, w2b, wsem, *, layer):
    w = pl.program_id(0)
    prev = jnp.maximum(w - 1, 0)
    first_visit = (w == 0) | (wblk_ref[w] != wblk_ref[prev])
    lo = wlo_ref[w]
    hi = whi_ref[w]

    def weight_copies(e, par):
        return (pltpu.make_async_copy(w13_ref.at[layer, e], w13f.at[par], wsem.at[par, 0]),
                pltpu.make_async_copy(w2_ref.at[layer, e], w2f.at[par], wsem.at[par, 1]))

    @pl.when(w == 0)
    def _():
        for c in weight_copies(we_ref[0], 0):
            c.start()

    def ffn(x):
        x = _unpack_halves(x).astype(BF16)
        cw = D_EXPERT // FFN_CHUNKS
        gate_up = []
        for c in range(FFN_CHUNKS):
            a = jnp.dot(x, w13b[:, c * cw:(c + 1) * cw], preferred_element_type=F32)
            u = jnp.dot(x, w13b[:, D_EXPERT + c * cw:D_EXPERT + (c + 1) * cw], preferred_element_type=F32)
            gate_up.append((a, u))
        hmid = jnp.concatenate([(_silu(a) * u).astype(BF16) for a, u in gate_up], axis=1)
        return _pack_halves(jnp.dot(hmid, w2b[...], preferred_element_type=F32))

    @pl.when(hi > lo)
    def _():
        @pl.when((w == 0) | (we_ref[w] != we_ref[prev]))
        def _():
            par = wpar_ref[w]
            for c in weight_copies(we_ref[w], par):
                c.wait()
            w13b[...] = w13f[par].astype(BF16)
            w2b[...] = w2f[par].astype(BF16)
            nxt = wnext_ref[w]

            @pl.when(nxt >= 0)
            def _():
                for c in weight_copies(nxt, 1 - par):
                    c.start()

        whole = (lo == 0) & (hi == EXPERT_BLOCK)

        @pl.when(whole)
        def _():
            ys_ref[...] = ffn(xs_ref[...])

        half = FFN_PART
        for p in range(EXPERT_BLOCK // FFN_PART):
            rows = slice(p * half, (p + 1) * half)
            touched = (lo < (p + 1) * half) & (hi > p * half)

            @pl.when(jnp.logical_not(whole) & touched)
            def _():
                y = ffn(xs_ref[rows, :])
                row = lax.broadcasted_iota(jnp.int32, y.shape, 0) + p * half
                mine = (row >= lo) & (row < hi)

                @pl.when(first_visit)
                def _():
                    ys_ref[rows, :] = jnp.where(mine, y, jnp.zeros_like(y))

                @pl.when(jnp.logical_not(first_visit))
                def _():
                    ys_ref[rows, :] = jnp.where(mine, y, ys_ref[rows, :])

            @pl.when(jnp.logical_not(whole) & jnp.logical_not(touched) & first_visit)
            def _():
                ys_ref[rows, :] = jnp.zeros((half, ys_ref.shape[1]), U32)


def moe_ffn(xs, work, w13_all, w2_all, layer):
    n_slots, dp = xs.shape
    d = 2 * dp
    n_work = work[0].shape[0]
    xmap = lambda w, *prefetch: (prefetch[0][w], 0)
    grid_spec = pltpu.PrefetchScalarGridSpec(
        num_scalar_prefetch=len(work),
        grid=(n_work,),
        in_specs=[pl.BlockSpec((EXPERT_BLOCK, dp), xmap),
                  pl.BlockSpec(memory_space=pl.ANY), pl.BlockSpec(memory_space=pl.ANY)],
        out_specs=pl.BlockSpec((EXPERT_BLOCK, dp), xmap),
        scratch_shapes=[pltpu.VMEM((2, d, 2 * D_EXPERT), F32), pltpu.VMEM((2, D_EXPERT, d), F32),
                        pltpu.VMEM((d, 2 * D_EXPERT), BF16), pltpu.VMEM((D_EXPERT, d), BF16),
                        pltpu.SemaphoreType.DMA((2, 2))],
    )
    return pl.pallas_call(
        functools.partial(_ffn_kernel, layer=layer),
        out_shape=jax.ShapeDtypeStruct((n_slots, dp), U32),
        grid_spec=grid_spec,
        compiler_params=_cparams(("arbitrary",)),
        name="moe_ffn",
    )(*work, xs, w13_all, w2_all)


def _ffn_work_items(cnt):
    n_slots = 2 * T_ALL
    n_blocks = n_slots // EXPERT_BLOCK
    n_work = n_blocks + N_EXPERTS - 1
    end = jnp.cumsum(cnt)
    start = end - cnt
    first_blk = start // EXPERT_BLOCK
    last_blk = jnp.maximum(end - 1, start) // EXPERT_BLOCK
    n_items = jnp.where(cnt > 0, last_blk - first_blk + 1, 0)
    item_end = jnp.cumsum(n_items)
    item_start = item_end - n_items
    w = jnp.arange(n_work, dtype=jnp.int32)
    used = w < item_end[-1]
    wq = jnp.minimum(w, item_end[-1] - 1)
    e = jnp.sum((item_end[:, None] <= wq[None, :]).astype(jnp.int32), axis=0)
    onehot = e[None, :] == jnp.arange(N_EXPERTS, dtype=jnp.int32)[:, None]
    of_e = lambda table: jnp.sum(jnp.where(onehot, table[:, None], 0), axis=0)
    blk = jnp.where(used, of_e(first_blk) + (w - of_e(item_start)), n_blocks - 1).astype(jnp.int32)
    lo = jnp.maximum(of_e(start), blk * EXPERT_BLOCK) - blk * EXPERT_BLOCK
    hi = jnp.minimum(of_e(end), (blk + 1) * EXPERT_BLOCK) - blk * EXPERT_BLOCK
    lo = jnp.where(used, lo, 0).astype(jnp.int32)
    hi = jnp.where(used, hi, 0).astype(jnp.int32)
    e_before = jnp.concatenate([jnp.full((1,), -1, jnp.int32), e[:-1]])
    change = used & (e != e_before)
    parity = ((jnp.cumsum(change.astype(jnp.int32)) - 1) % 2).astype(jnp.int32)
    far = jnp.int32(n_work)
    next_change = lax.cummin(jnp.where(change, w, far), axis=0, reverse=True)
    next_change = jnp.concatenate([next_change[1:], jnp.full((1,), far, jnp.int32)])
    e_next = jnp.sum(jnp.where(next_change[None, :] == w[:, None], e[:, None], 0), axis=0)
    e_next = jnp.where(next_change < far, e_next, -1).astype(jnp.int32)
    return start, (blk, e, lo, hi, e_next, parity)


def hier_moe_layer(a, w_out_bf16, x, layer, gain, w_group, b_group, w_expert, b_expert, w13_all, w2_all, utri,
                   finish):
    t = T_ALL
    pad = LANES - N_EXPERTS - N_GROUPS
    w_router = jnp.concatenate([w_expert, w_group, jnp.zeros((D_MODEL, pad), F32)], axis=1)
    b_router = jnp.concatenate([b_expert, b_group, jnp.zeros((pad,), F32)])[:ROUTE_ROWS].reshape(ROUTE_ROWS, 1)
    h, xn, rec, rect, counts = moe_route(a, w_out_bf16, x, gain, w_router, b_router, utri)

    cnt = counts[:N_EXPERTS, 0].astype(jnp.int32)
    start, work = _ffn_work_items(cnt)
    experts = jnp.arange(N_EXPERTS, dtype=jnp.int32)[:, None]

    def slot_of(e_row, rank_row):
        first = jnp.sum(jnp.where(e_row.astype(jnp.int32)[None, :] == experts, start[:, None], 0), axis=0)
        return (first + rank_row.astype(jnp.int32)).reshape(1, t)

    dest_a = slot_of(rect[0], rect[2])
    dest_b = slot_of(rect[1], rect[3])

    xs = moe_dispatch_sc(xn, dest_a, dest_b, 2 * t)
    ys = moe_ffn(xs, work, w13_all, w2_all, layer)
    ga, gb = moe_gather_sc(ys, dest_a, dest_b)
    return finish(h, rec, ga, gb)


def _rope_tables(pos):
    half = ROPE_DIM // 2
    lane = np.arange(LANES) % HEAD_DIM
    rotary = lane < ROPE_DIM
    inv = jnp.where(rotary, jnp.exp(-math.log(ROPE_THETA) * jnp.asarray(lane % half, F32) * (2.0 / ROPE_DIM)), 0.0)
    ang = pos.astype(F32)[:, None] * inv[None, :]
    cos, sin = jnp.cos(ang), jnp.sin(ang)
    first = jnp.asarray(lane < half)
    second = jnp.asarray(rotary & (lane >= half))
    return cos, jnp.where(first, -sin, 0.0), jnp.where(second, sin, 0.0)


def kernel(x_prompt, x_sample, state_hgrn, cache_k_win, cache_v_win, meta_tokens, a_norm, a_w_in, a_lower_logits, a_out_norm, a_w_out, kv_norm, kv_w, k_norm, b_norm, b_wq, b_q_norm, b_sinks, b_w_out, moe_norm, moe_w_group, moe_b_group, moe_w_expert, moe_b_expert, moe_w13, moe_w2):
    tail_rows = T_ALL - OFF_META
    x_parts = (x_prompt.reshape(T_PROMPT, D_MODEL), x_sample.reshape(T_SAMPLE, D_MODEL),
               jnp.concatenate([meta_tokens.astype(F32), jnp.zeros((tail_rows - N_META, D_MODEL), F32)], axis=0))
    pos = jnp.concatenate([N_META + jnp.arange(SEQ, dtype=jnp.int32),
                           jnp.tile(PAST_LEN + jnp.arange(DEC_SEQ, dtype=jnp.int32), ROW_TILE // DEC_SEQ),
                           jnp.arange(N_META, dtype=jnp.int32),
                           jnp.zeros((ROW_TILE - N_META,), jnp.int32)])
    cos_t, sina_t, sinb_t = _rope_tables(pos)
    rs = np.arange(NORM_SLAB)
    hmean = jnp.asarray((rs[:, None] // HEAD_DIM == rs[None, :] // HEAD_DIM).astype(np.float32) / HEAD_DIM, BF16)
    rt = np.arange(ROW_TILE)
    utri = jnp.asarray((rt[:, None] < rt[None, :]).astype(np.float32), BF16)
    lower = jnp.cumsum(jax.nn.softmax(a_lower_logits.astype(F32), axis=0), axis=0)

    moe = functools.partial(hier_moe_layer, w13_all=moe_w13, w2_all=moe_w2, utri=utri)

    z = in_project(x_parts, a_norm[0], a_w_in[0].astype(BF16))
    zero_state = jnp.zeros((1, A_HEADS, A_DK, A_DV), F32)
    o_meta, s_meta = hgrn2_scan(z, zero_state, lower[0], a_out_norm[0],
                                row_off=OFF_META, n_seq=1, seq_len=N_META)
    o_prompt, s_prompt = hgrn2_scan(z, s_meta, lower[0], a_out_norm[0], row_off=0, n_seq=BATCH, seq_len=SEQ)
    o_sample, s_sample = hgrn2_scan(z, state_hgrn[0].astype(F32), lower[0], a_out_norm[0],
                                    row_off=OFF_SAMPLE, n_seq=DEC_BATCH, seq_len=DEC_SEQ, group=SCAN_SAMPLE_GROUP)
    o_tail = jnp.concatenate([o_meta, jnp.zeros((tail_rows - N_META, D_MODEL), BF16)], axis=0)
    finish0 = functools.partial(
        moe_combine_kv_q, kv_gain=kv_norm, kv_w_bf16=kv_w.astype(BF16), q_gain=b_norm[0], wq_bf16=b_wq[0].astype(BF16),
        hmean=hmean, k_hgain=jnp.tile(k_norm, KV_HEADS).reshape(1, KV_DIM),
        q_hgain=jnp.tile(b_q_norm[0], Q_HEADS).reshape(1, D_MODEL), cos_t=cos_t, sina_t=sina_t, sinb_t=sinb_t)
    h, k_all, v_all, q_all = moe((o_prompt, o_sample, o_tail), a_w_out[0].astype(BF16), x_parts, 0, moe_norm[0],
                                 moe_w_group[0], moe_b_group[0], moe_w_expert[0], moe_b_expert[0], finish=finish0)

    meta_blk = lambda a: jnp.concatenate([jnp.zeros((ATT_BLOCK - N_META, KV_DIM), F32),
                                          a[OFF_META:OFF_META + N_META]], axis=0)
    sinks = b_sinks[0].astype(F32)
    att_all = attention_prompt(q_all, k_all, v_all, meta_blk(k_all), meta_blk(v_all), sinks)
    att_all, k_win_s, v_win_s = attention_sample(q_all, cache_k_win.astype(F32), cache_v_win.astype(F32),
                                                 k_all, v_all, sinks, att_all)
    y_p, y_s = moe(att_all, b_w_out[0].astype(BF16), h, 1, moe_norm[1], moe_w_group[1], moe_b_group[1],
                   moe_w_expert[1], moe_b_expert[1], finish=functools.partial(moe_combine_dense, split=True))

    y_prompt = y_p.reshape(BATCH, SEQ, D_MODEL)
    y_sample = y_s.reshape(DEC_BATCH, DEC_SEQ, D_MODEL)
    last = lambda a: jnp.stack([a[(b + 1) * SEQ - WINDOW:(b + 1) * SEQ] for b in range(BATCH)]).reshape(
        BATCH, WINDOW, KV_HEADS, HEAD_DIM)
    kp = last(k_all)
    vp = last(v_all)
    return (y_prompt, y_sample, s_prompt[None], s_sample[None], kp, vp, k_win_s, v_win_s)
```

```python
import functools
import math

import numpy as np
import jax
import jax.numpy as jnp
from jax import lax
from jax.experimental import pallas as pl
from jax.experimental.pallas import tpu as pltpu
from jax.experimental.pallas import tpu_sc as plsc

F32 = jnp.float32
BF16 = jnp.bfloat16
U32 = jnp.uint32

D_MODEL = 1024
BATCH = 4
SEQ = 4096
DEC_BATCH = 128
DEC_SEQ = 8
PAST_LEN = 8192
N_META = 16
A_HEADS = 8
A_DK = 128
A_DV = 128
Q_HEADS = 16
KV_HEADS = 4
HEAD_DIM = 64
KV_DIM = KV_HEADS * HEAD_DIM
WINDOW = 128
ROPE_DIM = 16
ROPE_THETA = 500000.0
N_GROUPS = 4
EXPERTS_PER_GROUP = 8
N_EXPERTS = 32
D_EXPERT = 512
RMS_EPS = 1e-6

LANES = 128
SUBLANES = 8
MXU_DIM = 256
VMEM_BYTES = 64 * 1024 * 1024
VMEM_LIMIT = VMEM_BYTES - 8 * 1024 * 1024
NORM_SLAB = MXU_DIM

ROW_TILE = 512
T_PROMPT = BATCH * SEQ
T_SAMPLE = DEC_BATCH * DEC_SEQ
OFF_SAMPLE = T_PROMPT
OFF_META = T_PROMPT + T_SAMPLE
T_REAL = OFF_META + N_META
T_ALL = -(-T_REAL // ROW_TILE) * ROW_TILE
N_TILES = T_ALL // ROW_TILE

SCAN_CHUNK = 128
SCAN_SAMPLE_GROUP = 16
ATT_BLOCK = 128
EXPERT_BLOCK = 512
FFN_PART = 256
FFN_CHUNKS = 2


def _cparams(sem):
    return pltpu.CompilerParams(dimension_semantics=sem, vmem_limit_bytes=VMEM_LIMIT)


def _nt_dot(a, b):
    return lax.dot_general(a, b, (((1,), (1,)), ((), ())), preferred_element_type=F32)


def _rms(x, gain):
    ms = jnp.mean(x * x, axis=-1, keepdims=True)
    return x * lax.rsqrt(ms + RMS_EPS) * gain


def _silu(x):
    return x * jax.nn.sigmoid(x)


def _pack_halves(x):
    w = x.shape[1] // 2
    hi = lax.bitcast_convert_type(x[:, :w].astype(BF16).astype(F32), U32)
    lo = lax.bitcast_convert_type(x[:, w:].astype(BF16).astype(F32), U32)
    return hi | (lo >> 16)


def _unpack_halves(p):
    hi = lax.bitcast_convert_type(p & jnp.uint32(0xFFFF0000), F32)
    lo = lax.bitcast_convert_type(p << 16, F32)
    return jnp.concatenate([hi, lo], axis=1)


N_PROMPT_TILES = T_PROMPT // ROW_TILE
N_SAMPLE_TILES = T_SAMPLE // ROW_TILE


def _parts_specs(width):
    return [pl.BlockSpec((ROW_TILE, width), lambda i: (jnp.minimum(i, N_PROMPT_TILES - 1), 0)),
            pl.BlockSpec((ROW_TILE, width), lambda i: (jnp.clip(i - N_PROMPT_TILES, 0, N_SAMPLE_TILES - 1), 0)),
            pl.BlockSpec((ROW_TILE, width), lambda i: (0, 0))]


def _pick_part(i, p_ref, s_ref, t_ref, dtype):
    return jnp.where(i < N_PROMPT_TILES, p_ref[...].astype(dtype),
                     jnp.where(i < N_PROMPT_TILES + N_SAMPLE_TILES, s_ref[...].astype(dtype),
                               t_ref[...].astype(dtype)))


def _in_proj_kernel(xp_ref, xs_ref, xt_ref, g_ref, w_ref, o_ref):
    x = _pick_part(pl.program_id(0), xp_ref, xs_ref, xt_ref, F32)
    xn = _rms(x, g_ref[...])
    o_ref[...] = jnp.dot(xn.astype(BF16), w_ref[...], preferred_element_type=F32)


def in_project(x_parts, gain, w_bf16):
    d, n = w_bf16.shape
    return pl.pallas_call(
        _in_proj_kernel,
        out_shape=jax.ShapeDtypeStruct((T_ALL, n), F32),
        grid=(N_TILES,),
        in_specs=_parts_specs(d) + [pl.BlockSpec((1, d), lambda i: (0, 0)),
                                    pl.BlockSpec((d, n), lambda i: (0, 0))],
        out_specs=pl.BlockSpec((ROW_TILE, n), lambda i: (i, 0)),
        compiler_params=_cparams(("parallel",)),
        name="in_project",
    )(*x_parts, gain.reshape(1, d), w_bf16)


def _head_norm_rope(y, hmean_ref, hgain, cos_t, sina_t, sinb_t):
    rows, width = y.shape
    sq = (y * y).astype(BF16)
    parts = []
    for s in range(width // NORM_SLAB):
        parts.append(jnp.dot(sq[:, s * NORM_SLAB:(s + 1) * NORM_SLAB], hmean_ref[...], preferred_element_type=F32))
    ms = parts[0] if len(parts) == 1 else jnp.concatenate(parts, axis=1)
    yn = y * lax.rsqrt(ms + RMS_EPS) * hgain
    reps = width // LANES
    cos_w = jnp.concatenate([cos_t] * reps, axis=1)
    sina_w = jnp.concatenate([sina_t] * reps, axis=1)
    sinb_w = jnp.concatenate([sinb_t] * reps, axis=1)
    half = ROPE_DIM // 2
    nxt = pltpu.roll(yn, width - half, 1)
    prv = pltpu.roll(yn, half, 1)
    return yn * cos_w + nxt * sina_w + prv * sinb_w


def _rope_tile(i):
    tiles_per_seq = SEQ // ROW_TILE
    n_prompt_tiles = T_PROMPT // ROW_TILE
    n_sample_tiles = T_SAMPLE // ROW_TILE
    return (jnp.where(i < n_prompt_tiles, i % tiles_per_seq,
                      jnp.where(i < n_prompt_tiles + n_sample_tiles, tiles_per_seq, tiles_per_seq + 1)), 0)


def _scan_levels(c):
    levels = []
    m = c
    while m >= 2:
        levels.append(m)
        m //= 2
    return levels


LOG2E = 1.4426950408889634


def _scan_kernel(z_ref, s0_ref, lb_ref, og_ref, tri_ref, lmask_ref, sgn_ref, o_ref, sfin_ref, s_scr, b_scr,
                 *, rows, seq_len):
    c_idx = pl.program_id(1)
    levels = _scan_levels(seq_len)
    n_sub = rows // seq_len
    hk = A_HEADS * A_DK

    @pl.when(c_idx == 0)
    def _():
        s_scr[...] = s0_ref[...]

    sub = lax.broadcasted_iota(jnp.int32, (SUBLANES, LANES), 0)
    row = lax.broadcasted_iota(jnp.int32, (LANES, LANES), 0)
    og = og_ref[...]

    def pad_f32(x):
        if x.shape[0] == LANES:
            return x
        return jnp.concatenate([x, jnp.zeros((LANES - x.shape[0], x.shape[1]), x.dtype)], axis=0)

    def pad_rows(x):
        return pad_f32(x).astype(BF16)

    def cols(part, h):
        return slice(part * hk + h * LANES, part * hk + (h + 1) * LANES)

    def gates(h):
        lb = lb_ref[:, cols(0, h)]
        forget = lb + (1.0 - lb) * jax.nn.sigmoid(z_ref[:, cols(1, h)])
        logf = jnp.log(forget)
        hi = logf.astype(BF16).astype(F32)
        r1 = logf - hi
        mid = r1.astype(BF16).astype(F32)
        lo = r1 - mid
        cs = jnp.dot(tri_ref[...], pad_rows(jnp.concatenate([hi, mid, lo], axis=1)),
                     preferred_element_type=F32)
        b = (cs[:rows, :LANES] + cs[:rows, LANES:2 * LANES]) + cs[:rows, 2 * LANES:]
        b_scr[h] = b
        return _silu(z_ref[:, cols(0, h)]), 1.0 - forget, b

    def bref_for(h, m):
        b_rows = b_scr.at[h]
        half = m // 2
        pieces = []
        for g in range(rows // SUBLANES):
            base = g * SUBLANES
            if m >= SUBLANES:
                r = (base // m) * m + half - 1
                piece = jnp.broadcast_to(b_rows[r:r + 1, :], (SUBLANES, LANES))
            else:
                piece = jnp.broadcast_to(b_rows[base + half - 1:base + half, :], (SUBLANES, LANES))
                for blk in range(1, SUBLANES // m):
                    r = base + blk * m + half - 1
                    piece = jnp.where(sub >= blk * m,
                                      jnp.broadcast_to(b_rows[r:r + 1, :], (SUBLANES, LANES)), piece)
            pieces.append(piece)
        return pieces[0] if len(pieces) == 1 else jnp.concatenate(pieces, axis=0)

    heads = range(A_HEADS)
    qkb = [gates(h) for h in heads]
    att = [_nt_dot(pad_rows(qf), pad_rows(kf)) * lmask_ref[len(levels)] for qf, kf, _ in qkb]
    for li, m in enumerate(levels):
        for h in heads:
            qf, kf, b = qkb[h]
            sgn = sgn_ref[li]
            e = jnp.exp2((b - bref_for(h, m)) * sgn)
            w = pad_rows(jnp.where(sgn > 0, qf, kf) * e)
            att[h] = att[h] + _nt_dot(w, w) * lmask_ref[li]

    def finish(h):
        qf, kf, b = qkb[h]
        b_rows = b_scr.at[h]
        v_b = pad_rows(z_ref[:, cols(2, h)])
        o_intra = jnp.dot(att[h].astype(BF16), v_b, preferred_element_type=F32)
        eb = jnp.exp(b)
        qs = qf * eb
        b_end = [jnp.broadcast_to(b_rows[(i + 1) * seq_len - 1:(i + 1) * seq_len, :], (seq_len, LANES))
                 for i in range(n_sub)]
        b_end = b_end[0] if n_sub == 1 else jnp.concatenate(b_end, axis=0)
        kd_t = pad_f32(kf * jnp.exp(b_end - b)).T.astype(BF16)
        eb_t = pad_f32(eb).T
        qs_b = pad_rows(qs)
        o = o_intra
        for i in range(n_sub):
            s_old = s_scr[i, h]
            first, last = i * seq_len, (i + 1) * seq_len - 1
            if n_sub == 1:
                qs_i, v_i = qs_b, v_b
            else:
                mine = (row >= first) & (row <= last)
                qs_i = jnp.where(mine, qs_b, jnp.zeros_like(qs_b))
                v_i = jnp.where(mine, v_b, jnp.zeros_like(v_b))
            o = o + jnp.dot(qs_i, s_old.astype(BF16), preferred_element_type=F32)
            decay = jnp.broadcast_to(eb_t[:, last:last + 1], (LANES, LANES))
            s_scr[i, h] = decay * s_old + jnp.dot(kd_t, v_i, preferred_element_type=F32)
        o = o[:rows]

        on = _rms(o, og) * _silu(z_ref[:, cols(3, h)])
        o_ref[:, cols(0, h)] = on.astype(o_ref.dtype)

    for h in heads:
        finish(h)

    @pl.when(c_idx == pl.num_programs(1) - 1)
    def _():
        sfin_ref[...] = s_scr[...]


def _scan_consts(rows, seq_len):
    levels = _scan_levels(seq_len)
    r = np.arange(LANES)
    t, s = r[:, None], r[None, :]
    live = (t < rows) & (s < rows)
    tri = ((s <= t) & (t // seq_len == s // seq_len) & live).astype(np.float32)
    masks, sgns = [], []
    for m in levels:
        masks.append(((t // m == s // m) & (t % m >= m // 2) & (s % m < m // 2) & live).astype(np.float32))
        sgns.append(np.broadcast_to(np.where(r[:rows, None] % m >= m // 2, LOG2E, -LOG2E), (rows, LANES)))
    masks.append(((t == s) & live).astype(np.float32))
    return jnp.asarray(tri, BF16), jnp.asarray(np.stack(masks), F32), jnp.asarray(np.stack(sgns), F32)


def hgrn2_scan(z, s0, lb, o_gain, *, row_off, n_seq, seq_len, group=1):
    hv = A_HEADS * A_DV
    if seq_len > SCAN_CHUNK:
        assert group == 1
        sub_len, rows, n_chunks, n_steps = SCAN_CHUNK, SCAN_CHUNK, seq_len // SCAN_CHUNK, n_seq
    else:
        sub_len, rows, n_chunks, n_steps = seq_len, group * seq_len, 1, n_seq // group
    blk_off = row_off // rows
    tri, lmask, sgn = _scan_consts(rows, sub_len)
    shared_s0 = s0.shape[0] == 1
    fix2 = lambda s, c: (0, 0)
    fix3 = lambda s, c: (0, 0, 0)
    o, sfin = pl.pallas_call(
        functools.partial(_scan_kernel, rows=rows, seq_len=sub_len),
        out_shape=(jax.ShapeDtypeStruct((n_seq * seq_len, hv), BF16 if rows % 16 == 0 else F32),
                   jax.ShapeDtypeStruct((n_seq, A_HEADS, A_DK, A_DV), F32)),
        grid=(n_steps, n_chunks),
        in_specs=[pl.BlockSpec((rows, 4 * hv), lambda s, c: (blk_off + s * n_chunks + c, 0)),
                  pl.BlockSpec((group, A_HEADS, A_DK, A_DV), (lambda s, c: (0, 0, 0, 0)) if shared_s0
                               else (lambda s, c: (s, 0, 0, 0))),
                  pl.BlockSpec((1, hv), fix2), pl.BlockSpec((1, A_DV), fix2),
                  pl.BlockSpec((LANES, LANES), fix2), pl.BlockSpec(lmask.shape, fix3),
                  pl.BlockSpec(sgn.shape, fix3)],
        out_specs=(pl.BlockSpec((rows, hv), lambda s, c: (s * n_chunks + c, 0)),
                   pl.BlockSpec((group, A_HEADS, A_DK, A_DV), lambda s, c: (s, 0, 0, 0))),
        scratch_shapes=[pltpu.VMEM((group, A_HEADS, A_DK, A_DV), F32), pltpu.VMEM((A_HEADS, rows, LANES), F32)],
        compiler_params=_cparams(("parallel", "arbitrary")),
        name=f"hgrn2_scan_r{rows}",
    )(z, s0, lb.reshape(1, hv), o_gain.reshape(1, A_DV), tri, lmask, sgn)
    return o, sfin


KEYS = 2 * ATT_BLOCK
ATT_STEP = 1


def _pair_operand(x, kh):
    slab = x[:, (kh // 2) * LANES:(kh // 2 + 1) * LANES]
    lane = lax.broadcasted_iota(jnp.int32, slab.shape, 1)
    if kh % 2 == 0:
        lo = jnp.where(lane < HEAD_DIM, slab, 0.0)
        hi = pltpu.roll(lo, HEAD_DIM, 1)
    else:
        hi = jnp.where(lane >= HEAD_DIM, slab, 0.0)
        lo = pltpu.roll(hi, HEAD_DIM, 1)
    return jnp.concatenate([lo, hi], axis=0).astype(BF16)


def _window_bias(rows, jmin):
    t_i = lax.broadcasted_iota(jnp.int32, (rows, 2 * KEYS), 0)
    c_i = lax.broadcasted_iota(jnp.int32, (rows, 2 * KEYS), 1)
    j_i = c_i & (ATT_BLOCK - 1)
    own = (c_i & ATT_BLOCK) != 0
    ok = (own & (j_i <= t_i)) | (jnp.logical_not(own) & (j_i >= t_i) & (j_i >= jmin))
    return jnp.where(ok, 0.0, -jnp.inf).astype(F32)


def _pair_softmax(s, sink_a, sink_b):
    probs, rinv = [], []
    for hh, sink in enumerate((sink_a, sink_b)):
        sh = s[:, hh * KEYS:(hh + 1) * KEYS]
        m = jnp.maximum(jnp.max(sh, axis=-1, keepdims=True), sink)
        p = jnp.exp(sh - m)
        den = jnp.sum(p, axis=-1, keepdims=True) + jnp.exp(sink - m)
        probs.append(p.astype(BF16))
        rinv.append(1.0 / den)
    lane = lax.broadcasted_iota(jnp.int32, (s.shape[0], LANES), 1)
    return jnp.concatenate(probs, axis=1), jnp.where(lane < HEAD_DIM, rinv[0], rinv[1])


def _attn_prompt_kernel(sink_ref, q_ref, kp_ref, ko_ref, vp_ref, vo_ref, km_ref, vm_ref, o_ref,
                        k2_scr, v2_scr, s_scr, p_scr, r_scr, bias_scr):
    n = pl.program_id(0)
    steps_per_seq = SEQ // (ATT_STEP * ATT_BLOCK)
    n_pairs = Q_HEADS // 2
    blk = ATT_BLOCK

    @pl.when(n == 0)
    def _():
        bias_scr[0] = _window_bias(ATT_BLOCK, 0)
        bias_scr[1] = _window_bias(ATT_BLOCK, ATT_BLOCK - N_META)

    @pl.when(n >= BATCH * steps_per_seq)
    def _():
        o_ref[...] = jnp.zeros_like(o_ref)

    @pl.when(n < BATCH * steps_per_seq)
    def _():
        first = (n % steps_per_seq) == 0
        own_k, own_v = ko_ref[...], vo_ref[...]
        keys = [jnp.where(first, km_ref[...], kp_ref[...])] + [own_k[u * blk:(u + 1) * blk] for u in range(ATT_STEP)]
        vals = [jnp.where(first, vm_ref[...], vp_ref[...])] + [own_v[u * blk:(u + 1) * blk] for u in range(ATT_STEP)]
        biases = [bias_scr[first.astype(jnp.int32)]] + [bias_scr[0]] * (ATT_STEP - 1)
        for u in range(ATT_STEP):
            k = jnp.concatenate(keys[u:u + 2], axis=0)
            v = jnp.concatenate(vals[u:u + 2], axis=0)
            for kh in range(KV_HEADS):
                k2_scr[u, kh] = _pair_operand(k, kh)
                v2_scr[u, kh] = _pair_operand(v, kh)
        units = [(u, pair) for u in range(ATT_STEP) for pair in range(n_pairs)]
        rows = lambda u: slice(u * blk, (u + 1) * blk)
        lanes = lambda pair: slice(pair * LANES, (pair + 1) * LANES)
        for j, (u, pair) in enumerate(units):
            s_scr[j] = _nt_dot(q_ref[rows(u), lanes(pair)], k2_scr[u, pair // 2]) + biases[u]
        for j, (u, pair) in enumerate(units):
            p, rinv = _pair_softmax(s_scr[j], sink_ref[2 * pair], sink_ref[2 * pair + 1])
            p_scr[j] = p
            r_scr[j] = rinv
        for j, (u, pair) in enumerate(units):
            o = jnp.dot(p_scr[j], v2_scr[u, pair // 2], preferred_element_type=F32) * r_scr[j]
            o_ref[rows(u), lanes(pair)] = o.astype(o_ref.dtype)


def attention_prompt(q_all, k_all, v_all, k_meta_blk, v_meta_blk, sinks):
    step_rows = ATT_STEP * ATT_BLOCK
    n_prompt_steps = T_PROMPT // step_rows
    n_steps = T_ALL // step_rows
    n_units = ATT_STEP * (Q_HEADS // 2)
    own = lambda n, sk: (jnp.minimum(n, n_prompt_steps - 1), 0)
    prev = lambda n, sk: (jnp.maximum(ATT_STEP * jnp.minimum(n, n_prompt_steps - 1) - 1, 0), 0)
    fix = lambda n, sk: (0, 0)
    grid_spec = pltpu.PrefetchScalarGridSpec(
        num_scalar_prefetch=1,
        grid=(n_steps,),
        in_specs=[pl.BlockSpec((step_rows, D_MODEL), own),
                  pl.BlockSpec((ATT_BLOCK, KV_DIM), prev), pl.BlockSpec((step_rows, KV_DIM), own),
                  pl.BlockSpec((ATT_BLOCK, KV_DIM), prev), pl.BlockSpec((step_rows, KV_DIM), own),
                  pl.BlockSpec((ATT_BLOCK, KV_DIM), fix), pl.BlockSpec((ATT_BLOCK, KV_DIM), fix)],
        out_specs=pl.BlockSpec((step_rows, D_MODEL), lambda n, sk: (n, 0)),
        scratch_shapes=[pltpu.VMEM((ATT_STEP, KV_HEADS, 2 * KEYS, LANES), BF16),
                        pltpu.VMEM((ATT_STEP, KV_HEADS, 2 * KEYS, LANES), BF16),
                        pltpu.VMEM((n_units, ATT_BLOCK, 2 * KEYS), F32),
                        pltpu.VMEM((n_units, ATT_BLOCK, 2 * KEYS), BF16),
                        pltpu.VMEM((n_units, ATT_BLOCK, LANES), F32),
                        pltpu.VMEM((2, ATT_BLOCK, 2 * KEYS), F32)],
    )
    return pl.pallas_call(
        _attn_prompt_kernel,
        out_shape=jax.ShapeDtypeStruct((T_ALL, D_MODEL), BF16),
        grid_spec=grid_spec,
        compiler_params=_cparams(("arbitrary",)),
        name="attention_prompt",
    )(sinks, q_all, k_all, k_all, v_all, v_all, k_meta_blk, v_meta_blk)


SAMPLE_GROUP = ATT_BLOCK // DEC_SEQ
SAMPLE_UNROLL = 4


def _attn_sample_kernel(sink_ref, q_ref, ck_ref, cv_ref, kn_ref, vn_ref, buf_ref, o_ref, kw_ref, vw_ref,
                        qf_scr, of_scr, k2_scr, v2_scr):
    del buf_ref
    qrows = 2 * DEC_SEQ
    qf_scr[...] = q_ref[...].astype(F32)
    bias = _window_bias(qrows, 0)
    zq = jnp.zeros((qrows - DEC_SEQ, D_MODEL), F32)
    zk = jnp.zeros((ATT_BLOCK - DEC_SEQ, KV_DIM), F32)

    n_pairs = Q_HEADS // 2
    lanes_of = lambda pair: slice(pair * LANES, (pair + 1) * LANES)

    def seq_group_body(it, carry):
        seqs = tuple(SAMPLE_UNROLL * it + u for u in range(SAMPLE_UNROLL))
        r_new = [pl.multiple_of(i * DEC_SEQ, DEC_SEQ) for i in seqs]
        qs = []
        for u, i in enumerate(seqs):
            qs.append(jnp.concatenate([qf_scr[pl.ds(r_new[u], DEC_SEQ), :], zq], axis=0).astype(BF16))
            old = lambda c_ref: jnp.concatenate([c_ref[i, :, kh, :] for kh in range(KV_HEADS)], axis=1)
            k = jnp.concatenate([old(ck_ref), kn_ref[pl.ds(r_new[u], DEC_SEQ), :], zk], axis=0)
            v = jnp.concatenate([old(cv_ref), vn_ref[pl.ds(r_new[u], DEC_SEQ), :], zk], axis=0)
            for c_ref, n_ref, w_ref in ((ck_ref, kn_ref, kw_ref), (cv_ref, vn_ref, vw_ref)):
                w_ref[i, 0:WINDOW - DEC_SEQ] = c_ref[i, DEC_SEQ:WINDOW]
                for kh in range(KV_HEADS):
                    w_ref[i, WINDOW - DEC_SEQ:WINDOW, kh, :] = n_ref[pl.ds(r_new[u], DEC_SEQ),
                                                                     kh * HEAD_DIM:(kh + 1) * HEAD_DIM]
            for kh in range(KV_HEADS):
                k2_scr[u, kh] = _pair_operand(k, kh)
                v2_scr[u, kh] = _pair_operand(v, kh)
        scores = [[_nt_dot(qs[u][:, lanes_of(pair)], k2_scr[u, pair // 2]) + bias for pair in range(n_pairs)]
                  for u in range(SAMPLE_UNROLL)]
        soft = [[_pair_softmax(s, sink_ref[2 * pair], sink_ref[2 * pair + 1]) for pair, s in enumerate(scores[u])]
                for u in range(SAMPLE_UNROLL)]
        for u in range(SAMPLE_UNROLL):
            for pair, (p, rinv) in enumerate(soft[u]):
                o = jnp.dot(p, v2_scr[u, pair // 2], preferred_element_type=F32) * rinv
                of_scr[pl.ds(r_new[u], DEC_SEQ), lanes_of(pair)] = o[:DEC_SEQ]
        return carry

    lax.fori_loop(0, SAMPLE_GROUP // SAMPLE_UNROLL, seq_group_body, 0)
    o_ref[...] = of_scr[...].astype(o_ref.dtype)


def attention_sample(q_all, cache_k, cache_v, k_all, v_all, sinks, out_buf):
    first_blk = OFF_SAMPLE // ATT_BLOCK
    new = lambda g, sk: (first_blk + g, 0)
    old = pl.BlockSpec((SAMPLE_GROUP, WINDOW, KV_HEADS, HEAD_DIM), lambda g, sk: (g, 0, 0, 0))
    grid_spec = pltpu.PrefetchScalarGridSpec(
        num_scalar_prefetch=1,
        grid=(DEC_BATCH // SAMPLE_GROUP,),
        in_specs=[pl.BlockSpec((ATT_BLOCK, D_MODEL), new),
                  old, old,
                  pl.BlockSpec((ATT_BLOCK, KV_DIM), new), pl.BlockSpec((ATT_BLOCK, KV_DIM), new),
                  pl.BlockSpec(memory_space=pl.ANY)],
        out_specs=(pl.BlockSpec((ATT_BLOCK, D_MODEL), new), old, old),
        scratch_shapes=[pltpu.VMEM((ATT_BLOCK, D_MODEL), F32), pltpu.VMEM((ATT_BLOCK, D_MODEL), F32),
                        pltpu.VMEM((SAMPLE_UNROLL, KV_HEADS, 2 * KEYS, LANES), BF16),
                        pltpu.VMEM((SAMPLE_UNROLL, KV_HEADS, 2 * KEYS, LANES), BF16)],
    )
    window = jax.ShapeDtypeStruct(cache_k.shape, cache_k.dtype)
    return pl.pallas_call(
        _attn_sample_kernel,
        out_shape=(jax.ShapeDtypeStruct(out_buf.shape, out_buf.dtype), window, window),
        grid_spec=grid_spec,
        input_output_aliases={6: 0},
        compiler_params=_cparams(("parallel",)),
        name="attention_sample",
    )(sinks, q_all, cache_k, cache_v, k_all, v_all, out_buf)


ROUTE_COLS = 8
ROUTE_ROWS = 48


def _route_kernel(*refs, parts):
    i = pl.program_id(0)
    if parts:
        (ap_ref, as_ref, at_ref, w_ref, xp_ref, xs_ref, xt_ref), refs = refs[:7], refs[7:]
        a = _pick_part(i, ap_ref, as_ref, at_ref, BF16)
        x = _pick_part(i, xp_ref, xs_ref, xt_ref, F32)
    else:
        (a_ref, w_ref, x_ref), refs = refs[:3], refs[3:]
        a, x = a_ref[...], x_ref[...]
    g_ref, whl_ref, br_ref, utri_ref, h_ref, xn_ref, rec_ref, rect_ref, cnt_ref, cnt_scr = refs

    @pl.when(i == 0)
    def _():
        cnt_scr[...] = jnp.zeros_like(cnt_scr)

    h = x + jnp.dot(a, w_ref[...], preferred_element_type=F32)
    h_ref[...] = h
    xn = _rms(h, g_ref[...])
    xn_ref[...] = _pack_halves(xn)
    xh = xn.astype(BF16)
    xl = (xn - xh.astype(F32)).astype(BF16)
    by_xh = _nt_dot(whl_ref[...], xh)
    logits = (by_xh[:LANES] + (by_xh[LANES:] + _nt_dot(whl_ref[:LANES, :], xl)))[:ROUTE_ROWS]
    logits = logits + br_ref[...]
    tokens = logits.shape[1]
    rid = lax.broadcasted_iota(jnp.int32, (ROUTE_ROWS, tokens), 0).astype(F32)
    neg = jnp.float32(-jnp.inf)
    big = jnp.float32(ROUTE_ROWS)

    is_g = (rid >= N_EXPERTS) & (rid < N_EXPERTS + N_GROUPS)
    gl = jnp.where(is_g, logits, neg)
    gmax = jnp.max(gl, axis=0, keepdims=True)
    gsel = jnp.min(jnp.where(gl == gmax, rid, big), axis=0, keepdims=True) - N_EXPERTS
    gden = jnp.sum(jnp.where(is_g, jnp.exp(gl - gmax), 0.0), axis=0, keepdims=True)
    gw = 1.0 / gden

    in_grp = (rid >= gsel * EXPERTS_PER_GROUP) & (rid < (gsel + 1) * EXPERTS_PER_GROUP)
    el = jnp.where(in_grp, logits, neg)
    t1 = jnp.max(el, axis=0, keepdims=True)
    e1 = jnp.min(jnp.where(el == t1, rid, big), axis=0, keepdims=True)
    el2 = jnp.where(rid == e1, neg, el)
    t2 = jnp.max(el2, axis=0, keepdims=True)
    e2 = jnp.min(jnp.where(el2 == t2, rid, big), axis=0, keepdims=True)
    x2 = jnp.exp(t2 - t1)
    w1 = gw / (1.0 + x2)
    w2 = gw * x2 / (1.0 + x2)

    oh1 = (rid == e1).astype(F32)
    oh2 = (rid == e2).astype(F32)
    oh = oh1 + oh2
    before = jnp.dot(oh.astype(BF16), utri_ref[...], preferred_element_type=F32)
    base = cnt_scr[...] + before
    r1 = jnp.sum(base * oh1, axis=0, keepdims=True)
    r2 = jnp.sum(base * oh2, axis=0, keepdims=True)
    cnt_scr[...] = cnt_scr[...] + jnp.sum(oh, axis=1, keepdims=True)

    zero = jnp.zeros_like(w1)
    rect = jnp.concatenate([e1, e2, r1, r2, w1, w2, zero, zero], axis=0)
    rect_ref[...] = rect
    wide = jnp.concatenate([rect, jnp.zeros((LANES - ROUTE_COLS, tokens), F32)], axis=0)
    rec_ref[...] = jnp.concatenate([wide[:, t0:t0 + LANES].T for t0 in range(0, tokens, LANES)], axis=0)
    cnt_ref[...] = cnt_scr[...]


def moe_route(a, w_out_bf16, x, gain, w_router, b_router, utri):
    parts = isinstance(a, tuple)
    t, d = T_ALL, D_MODEL
    row = lambda i: (i, 0)
    fix = lambda i: (0, 0)
    w_t = w_router.T
    w_hi = w_t.astype(BF16)
    w_lo = (w_t - w_hi.astype(F32)).astype(BF16)
    w_spec = pl.BlockSpec((d, d), fix)
    if parts:
        pre_specs = _parts_specs(d) + [w_spec] + _parts_specs(d)
        pre_args = (*a, w_out_bf16, *x)
    else:
        pre_specs = [pl.BlockSpec((ROW_TILE, d), row), w_spec, pl.BlockSpec((ROW_TILE, d), row)]
        pre_args = (a, w_out_bf16, x)
    return pl.pallas_call(
        functools.partial(_route_kernel, parts=parts),
        out_shape=(jax.ShapeDtypeStruct((t, d), F32),
                   jax.ShapeDtypeStruct((t, d // 2), U32), jax.ShapeDtypeStruct((t, LANES), F32),
                   jax.ShapeDtypeStruct((ROUTE_COLS, t), F32), jax.ShapeDtypeStruct((ROUTE_ROWS, 1), F32)),
        grid=(t // ROW_TILE,),
        in_specs=pre_specs + [pl.BlockSpec((1, d), fix), pl.BlockSpec((2 * LANES, d), fix),
                              pl.BlockSpec((ROUTE_ROWS, 1), fix), pl.BlockSpec((ROW_TILE, ROW_TILE), fix)],
        out_specs=(pl.BlockSpec((ROW_TILE, d), row),
                   pl.BlockSpec((ROW_TILE, d // 2), row), pl.BlockSpec((ROW_TILE, LANES), row),
                   pl.BlockSpec((ROUTE_COLS, ROW_TILE), lambda i: (0, i)), pl.BlockSpec((ROUTE_ROWS, 1), fix)),
        scratch_shapes=[pltpu.VMEM((ROUTE_ROWS, 1), F32)],
        compiler_params=_cparams(("arbitrary",)),
        name="moe_route",
    )(*pre_args, gain.reshape(1, d), jnp.concatenate([w_hi, w_lo], axis=0), b_router, utri)


SC_WINDOW = 64
SC_INDEX_WINDOW = 256


def _sc_mesh():
    return plsc.VectorSubcoreMesh(core_axis_name="core", subcore_axis_name="subcore")


def moe_dispatch_sc(xn, dest_a, dest_b, n_slots):
    t, d = xn.shape

    n_parts = SC_INDEX_WINDOW // SC_WINDOW

    @pl.kernel(out_type=jax.ShapeDtypeStruct((n_slots, d), xn.dtype), mesh=_sc_mesh(),
               scratch_types=[pltpu.VMEM((2, SC_WINDOW, d), xn.dtype), pltpu.SemaphoreType.DMA((2,)),
                              pltpu.SemaphoreType.DMA((2,))],
               name="moe_dispatch_sc")
    def run(x_hbm, id_hbm, da_hbm, db_hbm, o_hbm, buf, load_sem, store_sem):
        def body(id_vmem, da_vmem, db_vmem):
            part = lambda j: pl.ds(j * SC_WINDOW, SC_WINDOW)
            load = lambda j: pltpu.make_async_copy(x_hbm.at[id_vmem.at[0, part(j)]], buf.at[j % 2], load_sem.at[j % 2])
            load(0).start()
            for j in range(n_parts):
                load(j).wait()
                if j + 1 < n_parts:
                    load(j + 1).start()
                stores = [pltpu.make_async_copy(buf.at[j % 2], o_hbm.at[dv.at[0, part(j)]], store_sem.at[k])
                          for k, dv in enumerate((da_vmem, db_vmem))]
                for s in stores:
                    s.start()
                for s in stores:
                    s.wait()

        idx_spec = pl.BlockSpec((1, SC_INDEX_WINDOW), lambda i: (0, i))
        pltpu.emit_pipeline(
            body,
            grid=(t // SC_INDEX_WINDOW,),
            in_specs=[idx_spec, idx_spec, idx_spec],
            out_specs=[],
            core_axis_name=("core", "subcore"),
            dimension_semantics=(pltpu.PARALLEL,),
        )(id_hbm, da_hbm, db_hbm)

    return run(xn, jnp.arange(t, dtype=jnp.int32).reshape(1, t), dest_a, dest_b)


def moe_gather_sc(ys, dest_a, dest_b):
    d = ys.shape[1]
    t = dest_a.shape[1]
    out = jax.ShapeDtypeStruct((t, d), ys.dtype)

    n_moves = 2 * (SC_INDEX_WINDOW // SC_WINDOW)

    @pl.kernel(out_type=(out, out), mesh=_sc_mesh(),
               scratch_types=[pltpu.VMEM((2, SC_WINDOW, d), ys.dtype), pltpu.SemaphoreType.DMA((2,)),
                              pltpu.SemaphoreType.DMA((2,))],
               name="moe_gather_sc")
    def run(y_hbm, id_hbm, da_hbm, db_hbm, ga_hbm, gb_hbm, buf, load_sem, store_sem):
        def body(id_vmem, da_vmem, db_vmem):
            part = lambda m: pl.ds((m // 2) * SC_WINDOW, SC_WINDOW)
            src = lambda m: (da_vmem, db_vmem)[m % 2]
            dst = lambda m: (ga_hbm, gb_hbm)[m % 2]
            load = lambda m: pltpu.make_async_copy(y_hbm.at[src(m).at[0, part(m)]], buf.at[m % 2], load_sem.at[m % 2])
            store = lambda m: pltpu.make_async_copy(buf.at[m % 2], dst(m).at[id_vmem.at[0, part(m)]],
                                                    store_sem.at[m % 2])
            load(0).start()
            for m in range(n_moves):
                load(m).wait()
                if m >= 1:
                    store(m - 1).wait()
                if m + 1 < n_moves:
                    load(m + 1).start()
                store(m).start()
            store(n_moves - 1).wait()

        idx_spec = pl.BlockSpec((1, SC_INDEX_WINDOW), lambda i: (0, i))
        pltpu.emit_pipeline(
            body,
            grid=(t // SC_INDEX_WINDOW,),
            in_specs=[idx_spec, idx_spec, idx_spec],
            out_specs=[],
            core_axis_name=("core", "subcore"),
            dimension_semantics=(pltpu.PARALLEL,),
        )(id_hbm, da_hbm, db_hbm)

    return run(ys, jnp.arange(t, dtype=jnp.int32).reshape(1, t), dest_a, dest_b)


def _combine_dense_kernel(h_ref, rec_ref, ga_ref, gb_ref, *out_refs, split):
    i = pl.program_id(0)
    rec = rec_ref[...]
    res = h_ref[...] + rec[:, 4:5] * _unpack_halves(ga_ref[...]) + rec[:, 5:6] * _unpack_halves(gb_ref[...])
    if not split:
        out_refs[0][...] = res
    else:
        @pl.when(i < N_PROMPT_TILES)
        def _():
            out_refs[0][...] = res

        @pl.when((i >= N_PROMPT_TILES) & (i < N_PROMPT_TILES + N_SAMPLE_TILES))
        def _():
            out_refs[1][...] = res


def moe_combine_dense(h, rec, ga, gb, split=False):
    t, d = h.shape
    row = lambda i: (i, 0)
    if split:
        out_shape = (jax.ShapeDtypeStruct((T_PROMPT, d), F32), jax.ShapeDtypeStruct((T_SAMPLE, d), F32))
        out_specs = tuple(_parts_specs(d)[:2])
    else:
        out_shape = jax.ShapeDtypeStruct((t, d), F32)
        out_specs = pl.BlockSpec((ROW_TILE, d), row)
    return pl.pallas_call(
        functools.partial(_combine_dense_kernel, split=split),
        out_shape=out_shape,
        grid=(t // ROW_TILE,),
        in_specs=[pl.BlockSpec((ROW_TILE, d), row), pl.BlockSpec((ROW_TILE, LANES), row),
                  pl.BlockSpec((ROW_TILE, d // 2), row), pl.BlockSpec((ROW_TILE, d // 2), row)],
        out_specs=out_specs,
        compiler_params=_cparams(("arbitrary",)),
        name="moe_combine_dense",
    )(h, rec, ga, gb)


def _combine_kv_q_kernel(h_ref, rec_ref, ga_ref, gb_ref, gkv_ref, wkv_ref, gq_ref, wq_ref, hmean_ref,
                         hgk_ref, hgq_ref, cos_ref, sina_ref, sinb_ref, ho_ref, k_ref, v_ref, q_ref):
    rec = rec_ref[...]
    h = h_ref[...] + rec[:, 4:5] * _unpack_halves(ga_ref[...]) + rec[:, 5:6] * _unpack_halves(gb_ref[...])
    ho_ref[...] = h
    xhat = h * lax.rsqrt(jnp.mean(h * h, axis=-1, keepdims=True) + RMS_EPS)
    tables = (cos_ref[...], sina_ref[...], sinb_ref[...])
    zkv = jnp.dot((xhat * gkv_ref[...]).astype(BF16), wkv_ref[...], preferred_element_type=F32)
    k_ref[...] = _head_norm_rope(zkv[:, :KV_DIM], hmean_ref, hgk_ref[...], *tables)
    v_ref[...] = zkv[:, KV_DIM:]
    zq = jnp.dot((xhat * gq_ref[...]).astype(BF16), wq_ref[...], preferred_element_type=F32)
    q = _head_norm_rope(zq, hmean_ref, hgq_ref[...], *tables)
    q_ref[...] = (q * HEAD_DIM ** -0.5).astype(q_ref.dtype)


def moe_combine_kv_q(h, rec, ga, gb, kv_gain, kv_w_bf16, q_gain, wq_bf16, hmean, k_hgain, q_hgain,
                     cos_t, sina_t, sinb_t):
    t, d = h.shape
    row = lambda i: (i, 0)
    fix = lambda i: (0, 0)
    rope = pl.BlockSpec((ROW_TILE, LANES), _rope_tile)
    return pl.pallas_call(
        _combine_kv_q_kernel,
        out_shape=(jax.ShapeDtypeStruct((t, d), F32), jax.ShapeDtypeStruct((t, KV_DIM), F32),
                   jax.ShapeDtypeStruct((t, KV_DIM), F32), jax.ShapeDtypeStruct((t, d), BF16)),
        grid=(t // ROW_TILE,),
        in_specs=[pl.BlockSpec((ROW_TILE, d), row), pl.BlockSpec((ROW_TILE, LANES), row),
                  pl.BlockSpec((ROW_TILE, d // 2), row), pl.BlockSpec((ROW_TILE, d // 2), row),
                  pl.BlockSpec((1, d), fix), pl.BlockSpec((d, 2 * KV_DIM), fix),
                  pl.BlockSpec((1, d), fix), pl.BlockSpec((d, d), fix),
                  pl.BlockSpec((NORM_SLAB, NORM_SLAB), fix), pl.BlockSpec((1, KV_DIM), fix), pl.BlockSpec((1, d), fix),
                  rope, rope, rope],
        out_specs=(pl.BlockSpec((ROW_TILE, d), row), pl.BlockSpec((ROW_TILE, KV_DIM), row),
                   pl.BlockSpec((ROW_TILE, KV_DIM), row), pl.BlockSpec((ROW_TILE, d), row)),
        compiler_params=_cparams(("parallel",)),
        name="moe_combine_kv_q",
    )(h, rec, ga, gb, kv_gain.reshape(1, d), kv_w_bf16, q_gain.reshape(1, d), wq_bf16, hmean, k_hgain, q_hgain,
      cos_t, sina_t, sinb_t)


def _ffn_kernel(wblk_ref, we_ref, wlo_ref, whi_ref, wnext_ref, wpar_ref, xs_ref, w13_ref, w2_ref, ys_ref,
                w13f, w2f, w13b, w2b, wsem, *, layer):
    w = pl.program_id(0)
    prev = jnp.maximum(w - 1, 0)
    first_visit = (w == 0) | (wblk_ref[w] != wblk_ref[prev])
    lo = wlo_ref[w]
    hi = whi_ref[w]

    def weight_copies(e, par):
        return (pltpu.make_async_copy(w13_ref.at[layer, e], w13f.at[par], wsem.at[par, 0]),
                pltpu.make_async_copy(w2_ref.at[layer, e], w2f.at[par], wsem.at[par, 1]))

    @pl.when(w == 0)
    def _():
        for c in weight_copies(we_ref[0], 0):
            c.start()

    def ffn(x):
        x = _unpack_halves(x).astype(BF16)
        cw = D_EXPERT // FFN_CHUNKS
        gate_up = []
        for c in range(FFN_CHUNKS):
            a = jnp.dot(x, w13b[:, c * cw:(c + 1) * cw], preferred_element_type=F32)
            u = jnp.dot(x, w13b[:, D_EXPERT + c * cw:D_EXPERT + (c + 1) * cw], preferred_element_type=F32)
            gate_up.append((a, u))
        hmid = jnp.concatenate([(_silu(a) * u).astype(BF16) for a, u in gate_up], axis=1)
        return _pack_halves(jnp.dot(hmid, w2b[...], preferred_element_type=F32))

    @pl.when(hi > lo)
    def _():
        @pl.when((w == 0) | (we_ref[w] != we_ref[prev]))
        def _():
            par = wpar_ref[w]
            for c in weight_copies(we_ref[w], par):
                c.wait()
            w13b[...] = w13f[par].astype(BF16)
            w2b[...] = w2f[par].astype(BF16)
            nxt = wnext_ref[w]

            @pl.when(nxt >= 0)
            def _():
                for c in weight_copies(nxt, 1 - par):
                    c.start()

        whole = (lo == 0) & (hi == EXPERT_BLOCK)

        @pl.when(whole)
        def _():
            ys_ref[...] = ffn(xs_ref[...])

        half = FFN_PART
        for p in range(EXPERT_BLOCK // FFN_PART):
            rows = slice(p * half, (p + 1) * half)
            touched = (lo < (p + 1) * half) & (hi > p * half)

            @pl.when(jnp.logical_not(whole) & touched)
            def _():
                y = ffn(xs_ref[rows, :])
                row = lax.broadcasted_iota(jnp.int32, y.shape, 0) + p * half
                mine = (row >= lo) & (row < hi)

                @pl.when(first_visit)
                def _():
                    ys_ref[rows, :] = jnp.where(mine, y, jnp.zeros_like(y))

                @pl.when(jnp.logical_not(first_visit))
                def _():
                    ys_ref[rows, :] = jnp.where(mine, y, ys_ref[rows, :])

            @pl.when(jnp.logical_not(whole) & jnp.logical_not(touched) & first_visit)
            def _():
                ys_ref[rows, :] = jnp.zeros((half, ys_ref.shape[1]), U32)


def moe_ffn(xs, work, w13_all, w2_all, layer):
    n_slots, dp = xs.shape
    d = 2 * dp
    n_work = work[0].shape[0]
    xmap = lambda w, *prefetch: (prefetch[0][w], 0)
    grid_spec = pltpu.PrefetchScalarGridSpec(
        num_scalar_prefetch=len(work),
        grid=(n_work,),
        in_specs=[pl.BlockSpec((EXPERT_BLOCK, dp), xmap),
                  pl.BlockSpec(memory_space=pl.ANY), pl.BlockSpec(memory_space=pl.ANY)],
        out_specs=pl.BlockSpec((EXPERT_BLOCK, dp), xmap),
        scratch_shapes=[pltpu.VMEM((2, d, 2 * D_EXPERT), F32), pltpu.VMEM((2, D_EXPERT, d), F32),
                        pltpu.VMEM((d, 2 * D_EXPERT), BF16), pltpu.VMEM((D_EXPERT, d), BF16),
                        pltpu.SemaphoreType.DMA((2, 2))],
    )
    return pl.pallas_call(
        functools.partial(_ffn_kernel, layer=layer),
        out_shape=jax.ShapeDtypeStruct((n_slots, dp), U32),
        grid_spec=grid_spec,
        compiler_params=_cparams(("arbitrary",)),
        name="moe_ffn",
    )(*work, xs, w13_all, w2_all)


def _ffn_work_items(cnt):
    n_slots = 2 * T_ALL
    n_blocks = n_slots // EXPERT_BLOCK
    n_work = n_blocks + N_EXPERTS - 1
    end = jnp.cumsum(cnt)
    start = end - cnt
    first_blk = start // EXPERT_BLOCK
    last_blk = jnp.maximum(end - 1, start) // EXPERT_BLOCK
    n_items = jnp.where(cnt > 0, last_blk - first_blk + 1, 0)
    item_end = jnp.cumsum(n_items)
    item_start = item_end - n_items
    w = jnp.arange(n_work, dtype=jnp.int32)
    used = w < item_end[-1]
    wq = jnp.minimum(w, item_end[-1] - 1)
    e = jnp.sum((item_end[:, None] <= wq[None, :]).astype(jnp.int32), axis=0)
    onehot = e[None, :] == jnp.arange(N_EXPERTS, dtype=jnp.int32)[:, None]
    of_e = lambda table: jnp.sum(jnp.where(onehot, table[:, None], 0), axis=0)
    blk = jnp.where(used, of_e(first_blk) + (w - of_e(item_start)), n_blocks - 1).astype(jnp.int32)
    lo = jnp.maximum(of_e(start), blk * EXPERT_BLOCK) - blk * EXPERT_BLOCK
    hi = jnp.minimum(of_e(end), (blk + 1) * EXPERT_BLOCK) - blk * EXPERT_BLOCK
    lo = jnp.where(used, lo, 0).astype(jnp.int32)
    hi = jnp.where(used, hi, 0).astype(jnp.int32)
    e_before = jnp.concatenate([jnp.full((1,), -1, jnp.int32), e[:-1]])
    change = used & (e != e_before)
    parity = ((jnp.cumsum(change.astype(jnp.int32)) - 1) % 2).astype(jnp.int32)
    far = jnp.int32(n_work)
    next_change = lax.cummin(jnp.where(change, w, far), axis=0, reverse=True)
    next_change = jnp.concatenate([next_change[1:], jnp.full((1,), far, jnp.int32)])
    e_next = jnp.sum(jnp.where(next_change[None, :] == w[:, None], e[:, None], 0), axis=0)
    e_next = jnp.where(next_change < far, e_next, -1).astype(jnp.int32)
    return start, (blk, e, lo, hi, e_next, parity)


def hier_moe_layer(a, w_out_bf16, x, layer, gain, w_group, b_group, w_expert, b_expert, w13_all, w2_all, utri,
                   finish):
    t = T_ALL
    pad = LANES - N_EXPERTS - N_GROUPS
    w_router = jnp.concatenate([w_expert, w_group, jnp.zeros((D_MODEL, pad), F32)], axis=1)
    b_router = jnp.concatenate([b_expert, b_group, jnp.zeros((pad,), F32)])[:ROUTE_ROWS].reshape(ROUTE_ROWS, 1)
    h, xn, rec, rect, counts = moe_route(a, w_out_bf16, x, gain, w_router, b_router, utri)

    cnt = counts[:N_EXPERTS, 0].astype(jnp.int32)
    start, work = _ffn_work_items(cnt)
    experts = jnp.arange(N_EXPERTS, dtype=jnp.int32)[:, None]

    def slot_of(e_row, rank_row):
        first = jnp.sum(jnp.where(e_row.astype(jnp.int32)[None, :] == experts, start[:, None], 0), axis=0)
        return (first + rank_row.astype(jnp.int32)).reshape(1, t)

    dest_a = slot_of(rect[0], rect[2])
    dest_b = slot_of(rect[1], rect[3])

    xs = moe_dispatch_sc(xn, dest_a, dest_b, 2 * t)
    ys = moe_ffn(xs, work, w13_all, w2_all, layer)
    ga, gb = moe_gather_sc(ys, dest_a, dest_b)
    return finish(h, rec, ga, gb)


def _rope_tables(pos):
    half = ROPE_DIM // 2
    lane = np.arange(LANES) % HEAD_DIM
    rotary = lane < ROPE_DIM
    inv = jnp.where(rotary, jnp.exp(-math.log(ROPE_THETA) * jnp.asarray(lane % half, F32) * (2.0 / ROPE_DIM)), 0.0)
    ang = pos.astype(F32)[:, None] * inv[None, :]
    cos, sin = jnp.cos(ang), jnp.sin(ang)
    first = jnp.asarray(lane < half)
    second = jnp.asarray(rotary & (lane >= half))
    return cos, jnp.where(first, -sin, 0.0), jnp.where(second, sin, 0.0)


def kernel(x_prompt, x_sample, state_hgrn, cache_k_win, cache_v_win, meta_tokens, a_norm, a_w_in, a_lower_logits, a_out_norm, a_w_out, kv_norm, kv_w, k_norm, b_norm, b_wq, b_q_norm, b_sinks, b_w_out, moe_norm, moe_w_group, moe_b_group, moe_w_expert, moe_b_expert, moe_w13, moe_w2):
    tail_rows = T_ALL - OFF_META
    x_parts = (x_prompt.reshape(T_PROMPT, D_MODEL), x_sample.reshape(T_SAMPLE, D_MODEL),
               jnp.concatenate([meta_tokens.astype(F32), jnp.zeros((tail_rows - N_META, D_MODEL), F32)], axis=0))
    pos = jnp.concatenate([N_META + jnp.arange(SEQ, dtype=jnp.int32),
                           jnp.tile(PAST_LEN + jnp.arange(DEC_SEQ, dtype=jnp.int32), ROW_TILE // DEC_SEQ),
                           jnp.arange(N_META, dtype=jnp.int32),
                           jnp.zeros((ROW_TILE - N_META,), jnp.int32)])
    cos_t, sina_t, sinb_t = _rope_tables(pos)
    rs = np.arange(NORM_SLAB)
    hmean = jnp.asarray((rs[:, None] // HEAD_DIM == rs[None, :] // HEAD_DIM).astype(np.float32) / HEAD_DIM, BF16)
    rt = np.arange(ROW_TILE)
    utri = jnp.asarray((rt[:, None] < rt[None, :]).astype(np.float32), BF16)
    lower = jnp.cumsum(jax.nn.softmax(a_lower_logits.astype(F32), axis=0), axis=0)

    moe = functools.partial(hier_moe_layer, w13_all=moe_w13, w2_all=moe_w2, utri=utri)

    z = in_project(x_parts, a_norm[0], a_w_in[0].astype(BF16))
    zero_state = jnp.zeros((1, A_HEADS, A_DK, A_DV), F32)
    o_meta, s_meta = hgrn2_scan(z, zero_state, lower[0], a_out_norm[0],
                                row_off=OFF_META, n_seq=1, seq_len=N_META)
    o_prompt, s_prompt = hgrn2_scan(z, s_meta, lower[0], a_out_norm[0], row_off=0, n_seq=BATCH, seq_len=SEQ)
    o_sample, s_sample = hgrn2_scan(z, state_hgrn[0].astype(F32), lower[0], a_out_norm[0],
                                    row_off=OFF_SAMPLE, n_seq=DEC_BATCH, seq_len=DEC_SEQ, group=SCAN_SAMPLE_GROUP)
    o_tail = jnp.concatenate([o_meta, jnp.zeros((tail_rows - N_META, D_MODEL), BF16)], axis=0)
    finish0 = functools.partial(
        moe_combine_kv_q, kv_gain=kv_norm, kv_w_bf16=kv_w.astype(BF16), q_gain=b_norm[0], wq_bf16=b_wq[0].astype(BF16),
        hmean=hmean, k_hgain=jnp.tile(k_norm, KV_HEADS).reshape(1, KV_DIM),
        q_hgain=jnp.tile(b_q_norm[0], Q_HEADS).reshape(1, D_MODEL), cos_t=cos_t, sina_t=sina_t, sinb_t=sinb_t)
    h, k_all, v_all, q_all = moe((o_prompt, o_sample, o_tail), a_w_out[0].astype(BF16), x_parts, 0, moe_norm[0],
                                 moe_w_group[0], moe_b_group[0], moe_w_expert[0], moe_b_expert[0], finish=finish0)

    meta_blk = lambda a: jnp.concatenate([jnp.zeros((ATT_BLOCK - N_META, KV_DIM), F32),
                                          a[OFF_META:OFF_META + N_META]], axis=0)
    sinks = b_sinks[0].astype(F32)
    att_all = attention_prompt(q_all, k_all, v_all, meta_blk(k_all), meta_blk(v_all), sinks)
    att_all, k_win_s, v_win_s = attention_sample(q_all, cache_k_win.astype(F32), cache_v_win.astype(F32),
                                                 k_all, v_all, sinks, att_all)
    y_p, y_s = moe(att_all, b_w_out[0].astype(BF16), h, 1, moe_norm[1], moe_w_group[1], moe_b_group[1],
                   moe_w_expert[1], moe_b_expert[1], finish=functools.partial(moe_combine_dense, split=True))

    y_prompt = y_p.reshape(BATCH, SEQ, D_MODEL)
    y_sample = y_s.reshape(DEC_BATCH, DEC_SEQ, D_MODEL)
    last = lambda a: jnp.stack([a[(b + 1) * SEQ - WINDOW:(b + 1) * SEQ] for b in range(BATCH)]).reshape(
        BATCH, WINDOW, KV_HEADS, HEAD_DIM)
    kp = last(k_all)
    vp = last(v_all)
    return (y_prompt, y_sample, s_prompt[None], s_sample[None], kp, vp, k_win_s, v_win_s)
```

```python
import functools
import math

import numpy as np
import jax
import jax.numpy as jnp
from jax import lax
from jax.experimental import pallas as pl
from jax.experimental.pallas import tpu as pltpu
from jax.experimental.pallas import tpu_sc as plsc

F32 = jnp.float32
BF16 = jnp.bfloat16
U32 = jnp.uint32

D_MODEL = 1024
BATCH = 4
SEQ = 4096
DEC_BATCH = 128
DEC_SEQ = 8
PAST_LEN = 8192
N_META = 16
A_HEADS = 8
A_DK = 128
A_DV = 128
Q_HEADS = 16
KV_HEADS = 4
HEAD_DIM = 64
KV_DIM = KV_HEADS * HEAD_DIM
WINDOW = 128
ROPE_DIM = 16
ROPE_THETA = 500000.0
N_GROUPS = 4
EXPERTS_PER_GROUP = 8
N_EXPERTS = 32
D_EXPERT = 512
RMS_EPS = 1e-6

LANES = 128
SUBLANES = 8
MXU_DIM = 256
VMEM_BYTES = 64 * 1024 * 1024
VMEM_LIMIT = VMEM_BYTES - 8 * 1024 * 1024
NORM_SLAB = MXU_DIM

ROW_TILE = 512
T_PROMPT = BATCH * SEQ
T_SAMPLE = DEC_BATCH * DEC_SEQ
OFF_SAMPLE = T_PROMPT
OFF_META = T_PROMPT + T_SAMPLE
T_REAL = OFF_META + N_META
T_ALL = -(-T_REAL // ROW_TILE) * ROW_TILE
N_TILES = T_ALL // ROW_TILE

SCAN_CHUNK = 128
SCAN_SAMPLE_GROUP = 16
ATT_BLOCK = 128
EXPERT_BLOCK = 512
FFN_PART = 256
FFN_CHUNKS = 2


def _cparams(sem):
    return pltpu.CompilerParams(dimension_semantics=sem, vmem_limit_bytes=VMEM_LIMIT)


def _nt_dot(a, b):
    return lax.dot_general(a, b, (((1,), (1,)), ((), ())), preferred_element_type=F32)


def _rms(x, gain):
    ms = jnp.mean(x * x, axis=-1, keepdims=True)
    return x * lax.rsqrt(ms + RMS_EPS) * gain


def _silu(x):
    return x * jax.nn.sigmoid(x)


def _pack_halves(x):
    w = x.shape[1] // 2
    hi = lax.bitcast_convert_type(x[:, :w].astype(BF16).astype(F32), U32)
    lo = lax.bitcast_convert_type(x[:, w:].astype(BF16).astype(F32), U32)
    return hi | (lo >> 16)


def _unpack_halves(p):
    hi = lax.bitcast_convert_type(p & jnp.uint32(0xFFFF0000), F32)
    lo = lax.bitcast_convert_type(p << 16, F32)
    return jnp.concatenate([hi, lo], axis=1)


N_PROMPT_TILES = T_PROMPT // ROW_TILE
N_SAMPLE_TILES = T_SAMPLE // ROW_TILE


def _parts_specs(width):
    return [pl.BlockSpec((ROW_TILE, width), lambda i: (jnp.minimum(i, N_PROMPT_TILES - 1), 0)),
            pl.BlockSpec((ROW_TILE, width), lambda i: (jnp.clip(i - N_PROMPT_TILES, 0, N_SAMPLE_TILES - 1), 0)),
            pl.BlockSpec((ROW_TILE, width), lambda i: (0, 0))]


def _pick_part(i, p_ref, s_ref, t_ref, dtype):
    return jnp.where(i < N_PROMPT_TILES, p_ref[...].astype(dtype),
                     jnp.where(i < N_PROMPT_TILES + N_SAMPLE_TILES, s_ref[...].astype(dtype),
                               t_ref[...].astype(dtype)))


def _in_proj_kernel(xp_ref, xs_ref, xt_ref, g_ref, w_ref, o_ref):
    x = _pick_part(pl.program_id(0), xp_ref, xs_ref, xt_ref, F32)
    xn = _rms(x, g_ref[...])
    o_ref[...] = jnp.dot(xn.astype(BF16), w_ref[...], preferred_element_type=F32)


def in_project(x_parts, gain, w_bf16):
    d, n = w_bf16.shape
    return pl.pallas_call(
        _in_proj_kernel,
        out_shape=jax.ShapeDtypeStruct((T_ALL, n), F32),
        grid=(N_TILES,),
        in_specs=_parts_specs(d) + [pl.BlockSpec((1, d), lambda i: (0, 0)),
                                    pl.BlockSpec((d, n), lambda i: (0, 0))],
        out_specs=pl.BlockSpec((ROW_TILE, n), lambda i: (i, 0)),
        compiler_params=_cparams(("parallel",)),
        name="in_project",
    )(*x_parts, gain.reshape(1, d), w_bf16)


def _head_norm_rope(y, hmean_ref, hgain, cos_t, sina_t, sinb_t):
    rows, width = y.shape
    sq = (y * y).astype(BF16)
    parts = []
    for s in range(width // NORM_SLAB):
        parts.append(jnp.dot(sq[:, s * NORM_SLAB:(s + 1) * NORM_SLAB], hmean_ref[...], preferred_element_type=F32))
    ms = parts[0] if len(parts) == 1 else jnp.concatenate(parts, axis=1)
    yn = y * lax.rsqrt(ms + RMS_EPS) * hgain
    reps = width // LANES
    cos_w = jnp.concatenate([cos_t] * reps, axis=1)
    sina_w = jnp.concatenate([sina_t] * reps, axis=1)
    sinb_w = jnp.concatenate([sinb_t] * reps, axis=1)
    half = ROPE_DIM // 2
    nxt = pltpu.roll(yn, width - half, 1)
    prv = pltpu.roll(yn, half, 1)
    return yn * cos_w + nxt * sina_w + prv * sinb_w


def _rope_tile(i):
    tiles_per_seq = SEQ // ROW_TILE
    n_prompt_tiles = T_PROMPT // ROW_TILE
    n_sample_tiles = T_SAMPLE // ROW_TILE
    return (jnp.where(i < n_prompt_tiles, i % tiles_per_seq,
                      jnp.where(i < n_prompt_tiles + n_sample_tiles, tiles_per_seq, tiles_per_seq + 1)), 0)


def _scan_levels(c):
    levels = []
    m = c
    while m >= 2:
        levels.append(m)
        m //= 2
    return levels


LOG2E = 1.4426950408889634


def _scan_kernel(z_ref, s0_ref, lb_ref, og_ref, tri_ref, lmask_ref, sgn_ref, o_ref, sfin_ref, s_scr, b_scr,
                 *, rows, seq_len):
    c_idx = pl.program_id(1)
    levels = _scan_levels(seq_len)
    n_sub = rows // seq_len
    hk = A_HEADS * A_DK

    @pl.when(c_idx == 0)
    def _():
        s_scr[...] = s0_ref[...]

    sub = lax.broadcasted_iota(jnp.int32, (SUBLANES, LANES), 0)
    row = lax.broadcasted_iota(jnp.int32, (LANES, LANES), 0)
    og = og_ref[...]

    def pad_f32(x):
        if x.shape[0] == LANES:
            return x
        return jnp.concatenate([x, jnp.zeros((LANES - x.shape[0], x.shape[1]), x.dtype)], axis=0)

    def pad_rows(x):
        return pad_f32(x).astype(BF16)

    def cols(part, h):
        return slice(part * hk + h * LANES, part * hk + (h + 1) * LANES)

    def gates(h):
        lb = lb_ref[:, cols(0, h)]
        forget = lb + (1.0 - lb) * jax.nn.sigmoid(z_ref[:, cols(1, h)])
        logf = jnp.log(forget)
        hi = logf.astype(BF16).astype(F32)
        r1 = logf - hi
        mid = r1.astype(BF16).astype(F32)
        lo = r1 - mid
        cs = jnp.dot(tri_ref[...], pad_rows(jnp.concatenate([hi, mid, lo], axis=1)),
                     preferred_element_type=F32)
        b = (cs[:rows, :LANES] + cs[:rows, LANES:2 * LANES]) + cs[:rows, 2 * LANES:]
        b_scr[h] = b
        return _silu(z_ref[:, cols(0, h)]), 1.0 - forget, b

    def bref_for(h, m):
        b_rows = b_scr.at[h]
        half = m // 2
        pieces = []
        for g in range(rows // SUBLANES):
            base = g * SUBLANES
            if m >= SUBLANES:
                r = (base // m) * m + half - 1
                piece = jnp.broadcast_to(b_rows[r:r + 1, :], (SUBLANES, LANES))
            else:
                piece = jnp.broadcast_to(b_rows[base + half - 1:base + half, :], (SUBLANES, LANES))
                for blk in range(1, SUBLANES // m):
                    r = base + blk * m + half - 1
                    piece = jnp.where(sub >= blk * m,
                                      jnp.broadcast_to(b_rows[r:r + 1, :], (SUBLANES, LANES)), piece)
            pieces.append(piece)
        return pieces[0] if len(pieces) == 1 else jnp.concatenate(pieces, axis=0)

    heads = range(A_HEADS)
    qkb = [gates(h) for h in heads]
    att = [_nt_dot(pad_rows(qf), pad_rows(kf)) * lmask_ref[len(levels)] for qf, kf, _ in qkb]
    for li, m in enumerate(levels):
        for h in heads:
            qf, kf, b = qkb[h]
            sgn = sgn_ref[li]
            e = jnp.exp2((b - bref_for(h, m)) * sgn)
            w = pad_rows(jnp.where(sgn > 0, qf, kf) * e)
            att[h] = att[h] + _nt_dot(w, w) * lmask_ref[li]

    def finish(h):
        qf, kf, b = qkb[h]
        b_rows = b_scr.at[h]
        v_b = pad_rows(z_ref[:, cols(2, h)])
        o_intra = jnp.dot(att[h].astype(BF16), v_b, preferred_element_type=F32)
        eb = jnp.exp(b)
        qs = qf * eb
        b_end = [jnp.broadcast_to(b_rows[(i + 1) * seq_len - 1:(i + 1) * seq_len, :], (seq_len, LANES))
                 for i in range(n_sub)]
        b_end = b_end[0] if n_sub == 1 else jnp.concatenate(b_end, axis=0)
        kd_t = pad_f32(kf * jnp.exp(b_end - b)).T.astype(BF16)
        eb_t = pad_f32(eb).T
        qs_b = pad_rows(qs)
        o = o_intra
        for i in range(n_sub):
            s_old = s_scr[i, h]
            first, last = i * seq_len, (i + 1) * seq_len - 1
            if n_sub == 1:
                qs_i, v_i = qs_b, v_b
            else:
                mine = (row >= first) & (row <= last)
                qs_i = jnp.where(mine, qs_b, jnp.zeros_like(qs_b))
                v_i = jnp.where(mine, v_b, jnp.zeros_like(v_b))
            o = o + jnp.dot(qs_i, s_old.astype(BF16), preferred_element_type=F32)
            decay = jnp.broadcast_to(eb_t[:, last:last + 1], (LANES, LANES))
            s_scr[i, h] = decay * s_old + jnp.dot(kd_t, v_i, preferred_element_type=F32)
        o = o[:rows]

        on = _rms(o, og) * _silu(z_ref[:, cols(3, h)])
        o_ref[:, cols(0, h)] = on.astype(o_ref.dtype)

    for h in heads:
        finish(h)

    @pl.when(c_idx == pl.num_programs(1) - 1)
    def _():
        sfin_ref[...] = s_scr[...]


def _scan_consts(rows, seq_len):
    levels = _scan_levels(seq_len)
    r = np.arange(LANES)
    t, s = r[:, None], r[None, :]
    live = (t < rows) & (s < rows)
    tri = ((s <= t) & (t // seq_len == s // seq_len) & live).astype(np.float32)
    masks, sgns = [], []
    for m in levels:
        masks.append(((t // m == s // m) & (t % m >= m // 2) & (s % m < m // 2) & live).astype(np.float32))
        sgns.append(np.broadcast_to(np.where(r[:rows, None] % m >= m // 2, LOG2E, -LOG2E), (rows, LANES)))
    masks.append(((t == s) & live).astype(np.float32))
    return jnp.asarray(tri, BF16), jnp.asarray(np.stack(masks), F32), jnp.asarray(np.stack(sgns), F32)


def hgrn2_scan(z, s0, lb, o_gain, *, row_off, n_seq, seq_len, group=1):
    hv = A_HEADS * A_DV
    if seq_len > SCAN_CHUNK:
        assert group == 1
        sub_len, rows, n_chunks, n_steps = SCAN_CHUNK, SCAN_CHUNK, seq_len // SCAN_CHUNK, n_seq
    else:
        sub_len, rows, n_chunks, n_steps = seq_len, group * seq_len, 1, n_seq // group
    blk_off = row_off // rows
    tri, lmask, sgn = _scan_consts(rows, sub_len)
    shared_s0 = s0.shape[0] == 1
    fix2 = lambda s, c: (0, 0)
    fix3 = lambda s, c: (0, 0, 0)
    o, sfin = pl.pallas_call(
        functools.partial(_scan_kernel, rows=rows, seq_len=sub_len),
        out_shape=(jax.ShapeDtypeStruct((n_seq * seq_len, hv), BF16 if rows % 16 == 0 else F32),
                   jax.ShapeDtypeStruct((n_seq, A_HEADS, A_DK, A_DV), F32)),
        grid=(n_steps, n_chunks),
        in_specs=[pl.BlockSpec((rows, 4 * hv), lambda s, c: (blk_off + s * n_chunks + c, 0)),
                  pl.BlockSpec((group, A_HEADS, A_DK, A_DV), (lambda s, c: (0, 0, 0, 0)) if shared_s0
                               else (lambda s, c: (s, 0, 0, 0))),
                  pl.BlockSpec((1, hv), fix2), pl.BlockSpec((1, A_DV), fix2),
                  pl.BlockSpec((LANES, LANES), fix2), pl.BlockSpec(lmask.shape, fix3),
                  pl.BlockSpec(sgn.shape, fix3)],
        out_specs=(pl.BlockSpec((rows, hv), lambda s, c: (s * n_chunks + c, 0)),
                   pl.BlockSpec((group, A_HEADS, A_DK, A_DV), lambda s, c: (s, 0, 0, 0))),
        scratch_shapes=[pltpu.VMEM((group, A_HEADS, A_DK, A_DV), F32), pltpu.VMEM((A_HEADS, rows, LANES), F32)],
        compiler_params=_cparams(("parallel", "arbitrary")),
        name=f"hgrn2_scan_r{rows}",
    )(z, s0, lb.reshape(1, hv), o_gain.reshape(1, A_DV), tri, lmask, sgn)
    return o, sfin


KEYS = 2 * ATT_BLOCK
ATT_STEP = 1


def _pair_operand(x, kh):
    slab = x[:, (kh // 2) * LANES:(kh // 2 + 1) * LANES]
    lane = lax.broadcasted_iota(jnp.int32, slab.shape, 1)
    if kh % 2 == 0:
        lo = jnp.where(lane < HEAD_DIM, slab, 0.0)
        hi = pltpu.roll(lo, HEAD_DIM, 1)
    else:
        hi = jnp.where(lane >= HEAD_DIM, slab, 0.0)
        lo = pltpu.roll(hi, HEAD_DIM, 1)
    return jnp.concatenate([lo, hi], axis=0).astype(BF16)


def _window_bias(rows, jmin):
    t_i = lax.broadcasted_iota(jnp.int32, (rows, 2 * KEYS), 0)
    c_i = lax.broadcasted_iota(jnp.int32, (rows, 2 * KEYS), 1)
    j_i = c_i & (ATT_BLOCK - 1)
    own = (c_i & ATT_BLOCK) != 0
    ok = (own & (j_i <= t_i)) | (jnp.logical_not(own) & (j_i >= t_i) & (j_i >= jmin))
    return jnp.where(ok, 0.0, -jnp.inf).astype(F32)


def _pair_softmax(s, sink_a, sink_b):
    probs, rinv = [], []
    for hh, sink in enumerate((sink_a, sink_b)):
        sh = s[:, hh * KEYS:(hh + 1) * KEYS]
        m = jnp.maximum(jnp.max(sh, axis=-1, keepdims=True), sink)
        p = jnp.exp(sh - m)
        den = jnp.sum(p, axis=-1, keepdims=True) + jnp.exp(sink - m)
        probs.append(p.astype(BF16))
        rinv.append(1.0 / den)
    lane = lax.broadcasted_iota(jnp.int32, (s.shape[0], LANES), 1)
    return jnp.concatenate(probs, axis=1), jnp.where(lane < HEAD_DIM, rinv[0], rinv[1])


def _attn_prompt_kernel(sink_ref, q_ref, kp_ref, ko_ref, vp_ref, vo_ref, km_ref, vm_ref, o_ref,
                        k2_scr, v2_scr, s_scr, p_scr, r_scr, bias_scr):
    n = pl.program_id(0)
    steps_per_seq = SEQ // (ATT_STEP * ATT_BLOCK)
    n_pairs = Q_HEADS // 2
    blk = ATT_BLOCK

    @pl.when(n == 0)
    def _():
        bias_scr[0] = _window_bias(ATT_BLOCK, 0)
        bias_scr[1] = _window_bias(ATT_BLOCK, ATT_BLOCK - N_META)

    @pl.when(n >= BATCH * steps_per_seq)
    def _():
        o_ref[...] = jnp.zeros_like(o_ref)

    @pl.when(n < BATCH * steps_per_seq)
    def _():
        first = (n % steps_per_seq) == 0
        own_k, own_v = ko_ref[...], vo_ref[...]
        keys = [jnp.where(first, km_ref[...], kp_ref[...])] + [own_k[u * blk:(u + 1) * blk] for u in range(ATT_STEP)]
        vals = [jnp.where(first, vm_ref[...], vp_ref[...])] + [own_v[u * blk:(u + 1) * blk] for u in range(ATT_STEP)]
        biases = [bias_scr[first.astype(jnp.int32)]] + [bias_scr[0]] * (ATT_STEP - 1)
        for u in range(ATT_STEP):
            k = jnp.concatenate(keys[u:u + 2], axis=0)
            v = jnp.concatenate(vals[u:u + 2], axis=0)
            for kh in range(KV_HEADS):
                k2_scr[u, kh] = _pair_operand(k, kh)
                v2_scr[u, kh] = _pair_operand(v, kh)
        units = [(u, pair) for u in range(ATT_STEP) for pair in range(n_pairs)]
        rows = lambda u: slice(u * blk, (u + 1) * blk)
        lanes = lambda pair: slice(pair * LANES, (pair + 1) * LANES)
        for j, (u, pair) in enumerate(units):
            s_scr[j] = _nt_dot(q_ref[rows(u), lanes(pair)], k2_scr[u, pair // 2]) + biases[u]
        for j, (u, pair) in enumerate(units):
            p, rinv = _pair_softmax(s_scr[j], sink_ref[2 * pair], sink_ref[2 * pair + 1])
            p_scr[j] = p
            r_scr[j] = rinv
        for j, (u, pair) in enumerate(units):
            o = jnp.dot(p_scr[j], v2_scr[u, pair // 2], preferred_element_type=F32) * r_scr[j]
            o_ref[rows(u), lanes(pair)] = o.astype(o_ref.dtype)


def attention_prompt(q_all, k_all, v_all, k_meta_blk, v_meta_blk, sinks):
    step_rows = ATT_STEP * ATT_BLOCK
    n_prompt_steps = T_PROMPT // step_rows
    n_steps = T_ALL // step_rows
    n_units = ATT_STEP * (Q_HEADS // 2)
    own = lambda n, sk: (jnp.minimum(n, n_prompt_steps - 1), 0)
    prev = lambda n, sk: (jnp.maximum(ATT_STEP * jnp.minimum(n, n_prompt_steps - 1) - 1, 0), 0)
    fix = lambda n, sk: (0, 0)
    grid_spec = pltpu.PrefetchScalarGridSpec(
        num_scalar_prefetch=1,
        grid=(n_steps,),
        in_specs=[pl.BlockSpec((step_rows, D_MODEL), own),
                  pl.BlockSpec((ATT_BLOCK, KV_DIM), prev), pl.BlockSpec((step_rows, KV_DIM), own),
                  pl.BlockSpec((ATT_BLOCK, KV_DIM), prev), pl.BlockSpec((step_rows, KV_DIM), own),
                  pl.BlockSpec((ATT_BLOCK, KV_DIM), fix), pl.BlockSpec((ATT_BLOCK, KV_DIM), fix)],
        out_specs=pl.BlockSpec((step_rows, D_MODEL), lambda n, sk: (n, 0)),
        scratch_shapes=[pltpu.VMEM((ATT_STEP, KV_HEADS, 2 * KEYS, LANES), BF16),
                        pltpu.VMEM((ATT_STEP, KV_HEADS, 2 * KEYS, LANES), BF16),
                        pltpu.VMEM((n_units, ATT_BLOCK, 2 * KEYS), F32),
                        pltpu.VMEM((n_units, ATT_BLOCK, 2 * KEYS), BF16),
                        pltpu.VMEM((n_units, ATT_BLOCK, LANES), F32),
                        pltpu.VMEM((2, ATT_BLOCK, 2 * KEYS), F32)],
    )
    return pl.pallas_call(
        _attn_prompt_kernel,
        out_shape=jax.ShapeDtypeStruct((T_ALL, D_MODEL), BF16),
        grid_spec=grid_spec,
        compiler_params=_cparams(("arbitrary",)),
        name="attention_prompt",
    )(sinks, q_all, k_all, k_all, v_all, v_all, k_meta_blk, v_meta_blk)


SAMPLE_GROUP = ATT_BLOCK // DEC_SEQ
SAMPLE_UNROLL = 4


def _attn_sample_kernel(sink_ref, q_ref, ck_ref, cv_ref, kn_ref, vn_ref, buf_ref, o_ref, kw_ref, vw_ref,
                        qf_scr, of_scr, k2_scr, v2_scr):
    del buf_ref
    qrows = 2 * DEC_SEQ
    qf_scr[...] = q_ref[...].astype(F32)
    bias = _window_bias(qrows, 0)
    zq = jnp.zeros((qrows - DEC_SEQ, D_MODEL), F32)
    zk = jnp.zeros((ATT_BLOCK - DEC_SEQ, KV_DIM), F32)

    n_pairs = Q_HEADS // 2
    lanes_of = lambda pair: slice(pair * LANES, (pair + 1) * LANES)

    def seq_group_body(it, carry):
        seqs = tuple(SAMPLE_UNROLL * it + u for u in range(SAMPLE_UNROLL))
        r_new = [pl.multiple_of(i * DEC_SEQ, DEC_SEQ) for i in seqs]
        qs = []
        for u, i in enumerate(seqs):
            qs.append(jnp.concatenate([qf_scr[pl.ds(r_new[u], DEC_SEQ), :], zq], axis=0).astype(BF16))
            old = lambda c_ref: jnp.concatenate([c_ref[i, :, kh, :] for kh in range(KV_HEADS)], axis=1)
            k = jnp.concatenate([old(ck_ref), kn_ref[pl.ds(r_new[u], DEC_SEQ), :], zk], axis=0)
            v = jnp.concatenate([old(cv_ref), vn_ref[pl.ds(r_new[u], DEC_SEQ), :], zk], axis=0)
            for c_ref, n_ref, w_ref in ((ck_ref, kn_ref, kw_ref), (cv_ref, vn_ref, vw_ref)):
                w_ref[i, 0:WINDOW - DEC_SEQ] = c_ref[i, DEC_SEQ:WINDOW]
                for kh in range(KV_HEADS):
                    w_ref[i, WINDOW - DEC_SEQ:WINDOW, kh, :] = n_ref[pl.ds(r_new[u], DEC_SEQ),
                                                                     kh * HEAD_DIM:(kh + 1) * HEAD_DIM]
            for kh in range(KV_HEADS):
                k2_scr[u, kh] = _pair_operand(k, kh)
                v2_scr[u, kh] = _pair_operand(v, kh)
        scores = [[_nt_dot(qs[u][:, lanes_of(pair)], k2_scr[u, pair // 2]) + bias for pair in range(n_pairs)]
                  for u in range(SAMPLE_UNROLL)]
        soft = [[_pair_softmax(s, sink_ref[2 * pair], sink_ref[2 * pair + 1]) for pair, s in enumerate(scores[u])]
                for u in range(SAMPLE_UNROLL)]
        for u in range(SAMPLE_UNROLL):
            for pair, (p, rinv) in enumerate(soft[u]):
                o = jnp.dot(p, v2_scr[u, pair // 2], preferred_element_type=F32) * rinv
                of_scr[pl.ds(r_new[u], DEC_SEQ), lanes_of(pair)] = o[:DEC_SEQ]
        return carry

    lax.fori_loop(0, SAMPLE_GROUP // SAMPLE_UNROLL, seq_group_body, 0)
    o_ref[...] = of_scr[...].astype(o_ref.dtype)


def attention_sample(q_all, cache_k, cache_v, k_all, v_all, sinks, out_buf):
    first_blk = OFF_SAMPLE // ATT_BLOCK
    new = lambda g, sk: (first_blk + g, 0)
    old = pl.BlockSpec((SAMPLE_GROUP, WINDOW, KV_HEADS, HEAD_DIM), lambda g, sk: (g, 0, 0, 0))
    grid_spec = pltpu.PrefetchScalarGridSpec(
        num_scalar_prefetch=1,
        grid=(DEC_BATCH // SAMPLE_GROUP,),
        in_specs=[pl.BlockSpec((ATT_BLOCK, D_MODEL), new),
                  old, old,
                  pl.BlockSpec((ATT_BLOCK, KV_DIM), new), pl.BlockSpec((ATT_BLOCK, KV_DIM), new),
                  pl.BlockSpec(memory_space=pl.ANY)],
        out_specs=(pl.BlockSpec((ATT_BLOCK, D_MODEL), new), old, old),
        scratch_shapes=[pltpu.VMEM((ATT_BLOCK, D_MODEL), F32), pltpu.VMEM((ATT_BLOCK, D_MODEL), F32),
                        pltpu.VMEM((SAMPLE_UNROLL, KV_HEADS, 2 * KEYS, LANES), BF16),
                        pltpu.VMEM((SAMPLE_UNROLL, KV_HEADS, 2 * KEYS, LANES), BF16)],
    )
    window = jax.ShapeDtypeStruct(cache_k.shape, cache_k.dtype)
    return pl.pallas_call(
        _attn_sample_kernel,
        out_shape=(jax.ShapeDtypeStruct(out_buf.shape, out_buf.dtype), window, window),
        grid_spec=grid_spec,
        input_output_aliases={6: 0},
        compiler_params=_cparams(("parallel",)),
        name="attention_sample",
    )(sinks, q_all, cache_k, cache_v, k_all, v_all, out_buf)


ROUTE_COLS = 8
ROUTE_ROWS = 48


def _route_kernel(*refs, parts):
    i = pl.program_id(0)
    if parts:
        (ap_ref, as_ref, at_ref, w_ref, xp_ref, xs_ref, xt_ref), refs = refs[:7], refs[7:]
        a = _pick_part(i, ap_ref, as_ref, at_ref, BF16)
        x = _pick_part(i, xp_ref, xs_ref, xt_ref, F32)
    else:
        (a_ref, w_ref, x_ref), refs = refs[:3], refs[3:]
        a, x = a_ref[...], x_ref[...]
    g_ref, whl_ref, br_ref, utri_ref, h_ref, xn_ref, rec_ref, rect_ref, cnt_ref, cnt_scr = refs

    @pl.when(i == 0)
    def _():
        cnt_scr[...] = jnp.zeros_like(cnt_scr)

    h = x + jnp.dot(a, w_ref[...], preferred_element_type=F32)
    h_ref[...] = h
    xn = _rms(h, g_ref[...])
    xn_ref[...] = _pack_halves(xn)
    xh = xn.astype(BF16)
    xl = (xn - xh.astype(F32)).astype(BF16)
    by_xh = _nt_dot(whl_ref[...], xh)
    logits = (by_xh[:LANES] + (by_xh[LANES:] + _nt_dot(whl_ref[:LANES, :], xl)))[:ROUTE_ROWS]
    logits = logits + br_ref[...]
    tokens = logits.shape[1]
    rid = lax.broadcasted_iota(jnp.int32, (ROUTE_ROWS, tokens), 0).astype(F32)
    neg = jnp.float32(-jnp.inf)
    big = jnp.float32(ROUTE_ROWS)

    is_g = (rid >= N_EXPERTS) & (rid < N_EXPERTS + N_GROUPS)
    gl = jnp.where(is_g, logits, neg)
    gmax = jnp.max(gl, axis=0, keepdims=True)
    gsel = jnp.min(jnp.where(gl == gmax, rid, big), axis=0, keepdims=True) - N_EXPERTS
    gden = jnp.sum(jnp.where(is_g, jnp.exp(gl - gmax), 0.0), axis=0, keepdims=True)
    gw = 1.0 / gden

    in_grp = (rid >= gsel * EXPERTS_PER_GROUP) & (rid < (gsel + 1) * EXPERTS_PER_GROUP)
    el = jnp.where(in_grp, logits, neg)
    t1 = jnp.max(el, axis=0, keepdims=True)
    e1 = jnp.min(jnp.where(el == t1, rid, big), axis=0, keepdims=True)
    el2 = jnp.where(rid == e1, neg, el)
    t2 = jnp.max(el2, axis=0, keepdims=True)
    e2 = jnp.min(jnp.where(el2 == t2, rid, big), axis=0, keepdims=True)
    x2 = jnp.exp(t2 - t1)
    w1 = gw / (1.0 + x2)
    w2 = gw * x2 / (1.0 + x2)

    oh1 = (rid == e1).astype(F32)
    oh2 = (rid == e2).astype(F32)
    oh = oh1 + oh2
    before = jnp.dot(oh.astype(BF16), utri_ref[...], preferred_element_type=F32)
    base = cnt_scr[...] + before
    r1 = jnp.sum(base * oh1, axis=0, keepdims=True)
    r2 = jnp.sum(base * oh2, axis=0, keepdims=True)
    cnt_scr[...] = cnt_scr[...] + jnp.sum(oh, axis=1, keepdims=True)

    zero = jnp.zeros_like(w1)
    rect = jnp.concatenate([e1, e2, r1, r2, w1, w2, zero, zero], axis=0)
    rect_ref[...] = rect
    wide = jnp.concatenate([rect, jnp.zeros((LANES - ROUTE_COLS, tokens), F32)], axis=0)
    rec_ref[...] = jnp.concatenate([wide[:, t0:t0 + LANES].T for t0 in range(0, tokens, LANES)], axis=0)
    cnt_ref[...] = cnt_scr[...]


def moe_route(a, w_out_bf16, x, gain, w_router, b_router, utri):
    parts = isinstance(a, tuple)
    t, d = T_ALL, D_MODEL
    row = lambda i: (i, 0)
    fix = lambda i: (0, 0)
    w_t = w_router.T
    w_hi = w_t.astype(BF16)
    w_lo = (w_t - w_hi.astype(F32)).astype(BF16)
    w_spec = pl.BlockSpec((d, d), fix)
    if parts:
        pre_specs = _parts_specs(d) + [w_spec] + _parts_specs(d)
        pre_args = (*a, w_out_bf16, *x)
    else:
        pre_specs = [pl.BlockSpec((ROW_TILE, d), row), w_spec, pl.BlockSpec((ROW_TILE, d), row)]
        pre_args = (a, w_out_bf16, x)
    return pl.pallas_call(
        functools.partial(_route_kernel, parts=parts),
        out_shape=(jax.ShapeDtypeStruct((t, d), F32),
                   jax.ShapeDtypeStruct((t, d // 2), U32), jax.ShapeDtypeStruct((t, LANES), F32),
                   jax.ShapeDtypeStruct((ROUTE_COLS, t), F32), jax.ShapeDtypeStruct((ROUTE_ROWS, 1), F32)),
        grid=(t // ROW_TILE,),
        in_specs=pre_specs + [pl.BlockSpec((1, d), fix), pl.BlockSpec((2 * LANES, d), fix),
                              pl.BlockSpec((ROUTE_ROWS, 1), fix), pl.BlockSpec((ROW_TILE, ROW_TILE), fix)],
        out_specs=(pl.BlockSpec((ROW_TILE, d), row),
                   pl.BlockSpec((ROW_TILE, d // 2), row), pl.BlockSpec((ROW_TILE, LANES), row),
                   pl.BlockSpec((ROUTE_COLS, ROW_TILE), lambda i: (0, i)), pl.BlockSpec((ROUTE_ROWS, 1), fix)),
        scratch_shapes=[pltpu.VMEM((ROUTE_ROWS, 1), F32)],
        compiler_params=_cparams(("arbitrary",)),
        name="moe_route",
    )(*pre_args, gain.reshape(1, d), jnp.concatenate([w_hi, w_lo], axis=0), b_router, utri)


SC_WINDOW = 32
SC_INDEX_WINDOW = 128


def _sc_mesh():
    return plsc.VectorSubcoreMesh(core_axis_name="core", subcore_axis_name="subcore")


def moe_dispatch_sc(xn, dest_a, dest_b, n_slots):
    t, d = xn.shape

    n_parts = SC_INDEX_WINDOW // SC_WINDOW

    @pl.kernel(out_type=jax.ShapeDtypeStruct((n_slots, d), xn.dtype), mesh=_sc_mesh(),
               scratch_types=[pltpu.VMEM((2, SC_WINDOW, d), xn.dtype), pltpu.SemaphoreType.DMA((2,)),
                              pltpu.SemaphoreType.DMA((2,))],
               name="moe_dispatch_sc")
    def run(x_hbm, id_hbm, da_hbm, db_hbm, o_hbm, buf, load_sem, store_sem):
        def body(id_vmem, da_vmem, db_vmem):
            part = lambda j: pl.ds(j * SC_WINDOW, SC_WINDOW)
            load = lambda j: pltpu.make_async_copy(x_hbm.at[id_vmem.at[0, part(j)]], buf.at[j % 2], load_sem.at[j % 2])
            load(0).start()
            for j in range(n_parts):
                load(j).wait()
                if j + 1 < n_parts:
                    load(j + 1).start()
                stores = [pltpu.make_async_copy(buf.at[j % 2], o_hbm.at[dv.at[0, part(j)]], store_sem.at[k])
                          for k, dv in enumerate((da_vmem, db_vmem))]
                for s in stores:
                    s.start()
                for s in stores:
                    s.wait()

        idx_spec = pl.BlockSpec((1, SC_INDEX_WINDOW), lambda i: (0, i))
        pltpu.emit_pipeline(
            body,
            grid=(t // SC_INDEX_WINDOW,),
            in_specs=[idx_spec, idx_spec, idx_spec],
            out_specs=[],
            core_axis_name=("core", "subcore"),
            dimension_semantics=(pltpu.PARALLEL,),
        )(id_hbm, da_hbm, db_hbm)

    return run(xn, jnp.arange(t, dtype=jnp.int32).reshape(1, t), dest_a, dest_b)


def moe_gather_sc(ys, dest_a, dest_b):
    d = ys.shape[1]
    t = dest_a.shape[1]
    out = jax.ShapeDtypeStruct((t, d), ys.dtype)

    n_moves = 2 * (SC_INDEX_WINDOW // SC_WINDOW)

    @pl.kernel(out_type=(out, out), mesh=_sc_mesh(),
               scratch_types=[pltpu.VMEM((2, SC_WINDOW, d), ys.dtype), pltpu.SemaphoreType.DMA((2,)),
                              pltpu.SemaphoreType.DMA((2,))],
               name="moe_gather_sc")
    def run(y_hbm, id_hbm, da_hbm, db_hbm, ga_hbm, gb_hbm, buf, load_sem, store_sem):
        def body(id_vmem, da_vmem, db_vmem):
            part = lambda m: pl.ds((m // 2) * SC_WINDOW, SC_WINDOW)
            src = lambda m: (da_vmem, db_vmem)[m % 2]
            dst = lambda m: (ga_hbm, gb_hbm)[m % 2]
            load = lambda m: pltpu.make_async_copy(y_hbm.at[src(m).at[0, part(m)]], buf.at[m % 2], load_sem.at[m % 2])
            store = lambda m: pltpu.make_async_copy(buf.at[m % 2], dst(m).at[id_vmem.at[0, part(m)]],
                                                    store_sem.at[m % 2])
            load(0).start()
            for m in range(n_moves):
                load(m).wait()
                if m >= 1:
                    store(m - 1).wait()
                if m + 1 < n_moves:
                    load(m + 1).start()
                store(m).start()
            store(n_moves - 1).wait()

        idx_spec = pl.BlockSpec((1, SC_INDEX_WINDOW), lambda i: (0, i))
        pltpu.emit_pipeline(
            body,
            grid=(t // SC_INDEX_WINDOW,),
            in_specs=[idx_spec, idx_spec, idx_spec],
            out_specs=[],
            core_axis_name=("core", "subcore"),
            dimension_semantics=(pltpu.PARALLEL,),
        )(id_hbm, da_hbm, db_hbm)

    return run(ys, jnp.arange(t, dtype=jnp.int32).reshape(1, t), dest_a, dest_b)


def _combine_dense_kernel(h_ref, rec_ref, ga_ref, gb_ref, *out_refs, split):
    i = pl.program_id(0)
    rec = rec_ref[...]
    res = h_ref[...] + rec[:, 4:5] * _unpack_halves(ga_ref[...]) + rec[:, 5:6] * _unpack_halves(gb_ref[...])
    if not split:
        out_refs[0][...] = res
    else:
        @pl.when(i < N_PROMPT_TILES)
        def _():
            out_refs[0][...] = res

        @pl.when((i >= N_PROMPT_TILES) & (i < N_PROMPT_TILES + N_SAMPLE_TILES))
        def _():
            out_refs[1][...] = res


def moe_combine_dense(h, rec, ga, gb, split=False):
    t, d = h.shape
    row = lambda i: (i, 0)
    if split:
        out_shape = (jax.ShapeDtypeStruct((T_PROMPT, d), F32), jax.ShapeDtypeStruct((T_SAMPLE, d), F32))
        out_specs = tuple(_parts_specs(d)[:2])
    else:
        out_shape = jax.ShapeDtypeStruct((t, d), F32)
        out_specs = pl.BlockSpec((ROW_TILE, d), row)
    return pl.pallas_call(
        functools.partial(_combine_dense_kernel, split=split),
        out_shape=out_shape,
        grid=(t // ROW_TILE,),
        in_specs=[pl.BlockSpec((ROW_TILE, d), row), pl.BlockSpec((ROW_TILE, LANES), row),
                  pl.BlockSpec((ROW_TILE, d // 2), row), pl.BlockSpec((ROW_TILE, d // 2), row)],
        out_specs=out_specs,
        compiler_params=_cparams(("arbitrary",)),
        name="moe_combine_dense",
    )(h, rec, ga, gb)


def _combine_kv_q_kernel(h_ref, rec_ref, ga_ref, gb_ref, gkv_ref, wkv_ref, gq_ref, wq_ref, hmean_ref,
                         hgk_ref, hgq_ref, cos_ref, sina_ref, sinb_ref, ho_ref, k_ref, v_ref, q_ref):
    rec = rec_ref[...]
    h = h_ref[...] + rec[:, 4:5] * _unpack_halves(ga_ref[...]) + rec[:, 5:6] * _unpack_halves(gb_ref[...])
    ho_ref[...] = h
    xhat = h * lax.rsqrt(jnp.mean(h * h, axis=-1, keepdims=True) + RMS_EPS)
    tables = (cos_ref[...], sina_ref[...], sinb_ref[...])
    zkv = jnp.dot((xhat * gkv_ref[...]).astype(BF16), wkv_ref[...], preferred_element_type=F32)
    k_ref[...] = _head_norm_rope(zkv[:, :KV_DIM], hmean_ref, hgk_ref[...], *tables)
    v_ref[...] = zkv[:, KV_DIM:]
    zq = jnp.dot((xhat * gq_ref[...]).astype(BF16), wq_ref[...], preferred_element_type=F32)
    q = _head_norm_rope(zq, hmean_ref, hgq_ref[...], *tables)
    q_ref[...] = (q * HEAD_DIM ** -0.5).astype(q_ref.dtype)


def moe_combine_kv_q(h, rec, ga, gb, kv_gain, kv_w_bf16, q_gain, wq_bf16, hmean, k_hgain, q_hgain,
                     cos_t, sina_t, sinb_t):
    t, d = h.shape
    row = lambda i: (i, 0)
    fix = lambda i: (0, 0)
    rope = pl.BlockSpec((ROW_TILE, LANES), _rope_tile)
    return pl.pallas_call(
        _combine_kv_q_kernel,
        out_shape=(jax.ShapeDtypeStruct((t, d), F32), jax.ShapeDtypeStruct((t, KV_DIM), F32),
                   jax.ShapeDtypeStruct((t, KV_DIM), F32), jax.ShapeDtypeStruct((t, d), BF16)),
        grid=(t // ROW_TILE,),
        in_specs=[pl.BlockSpec((ROW_TILE, d), row), pl.BlockSpec((ROW_TILE, LANES), row),
                  pl.BlockSpec((ROW_TILE, d // 2), row), pl.BlockSpec((ROW_TILE, d // 2), row),
                  pl.BlockSpec((1, d), fix), pl.BlockSpec((d, 2 * KV_DIM), fix),
                  pl.BlockSpec((1, d), fix), pl.BlockSpec((d, d), fix),
                  pl.BlockSpec((NORM_SLAB, NORM_SLAB), fix), pl.BlockSpec((1, KV_DIM), fix), pl.BlockSpec((1, d), fix),
                  rope, rope, rope],
        out_specs=(pl.BlockSpec((ROW_TILE, d), row), pl.BlockSpec((ROW_TILE, KV_DIM), row),
                   pl.BlockSpec((ROW_TILE, KV_DIM), row), pl.BlockSpec((ROW_TILE, d), row)),
        compiler_params=_cparams(("parallel",)),
        name="moe_combine_kv_q",
    )(h, rec, ga, gb, kv_gain.reshape(1, d), kv_w_bf16, q_gain.reshape(1, d), wq_bf16, hmean, k_hgain, q_hgain,
      cos_t, sina_t, sinb_t)


def _ffn_kernel(wblk_ref, we_ref, wlo_ref, whi_ref, wnext_ref, wpar_ref, xs_ref, w13_ref, w2_ref, ys_ref,
                w13f, w2f, w13b, w2b, wsem, *, layer):
    w = pl.program_id(0)
    prev = jnp.maximum(w - 1, 0)
    first_visit = (w == 0) | (wblk_ref[w] != wblk_ref[prev])
    lo = wlo_ref[w]
    hi = whi_ref[w]

    def weight_copies(e, par):
        return (pltpu.make_async_copy(w13_ref.at[layer, e], w13f.at[par], wsem.at[par, 0]),
                pltpu.make_async_copy(w2_ref.at[layer, e], w2f.at[par], wsem.at[par, 1]))

    @pl.when(w == 0)
    def _():
        for c in weight_copies(we_ref[0], 0):
            c.start()

    def ffn(x):
        x = _unpack_halves(x).astype(BF16)
        cw = D_EXPERT // FFN_CHUNKS
        gate_up = []
        for c in range(FFN_CHUNKS):
            a = jnp.dot(x, w13b[:, c * cw:(c + 1) * cw], preferred_element_type=F32)
            u = jnp.dot(x, w13b[:, D_EXPERT + c * cw:D_EXPERT + (c + 1) * cw], preferred_element_type=F32)
            gate_up.append((a, u))
        hmid = jnp.concatenate([(_silu(a) * u).astype(BF16) for a, u in gate_up], axis=1)
        return _pack_halves(jnp.dot(hmid, w2b[...], preferred_element_type=F32))

    @pl.when(hi > lo)
    def _():
        @pl.when((w == 0) | (we_ref[w] != we_ref[prev]))
        def _():
            par = wpar_ref[w]
            for c in weight_copies(we_ref[w], par):
                c.wait()
            w13b[...] = w13f[par].astype(BF16)
            w2b[...] = w2f[par].astype(BF16)
            nxt = wnext_ref[w]

            @pl.when(nxt >= 0)
            def _():
                for c in weight_copies(nxt, 1 - par):
                    c.start()

        whole = (lo == 0) & (hi == EXPERT_BLOCK)

        @pl.when(whole)
        def _():
            ys_ref[...] = ffn(xs_ref[...])

        half = FFN_PART
        for p in range(EXPERT_BLOCK // FFN_PART):
            rows = slice(p * half, (p + 1) * half)
            touched = (lo < (p + 1) * half) & (hi > p * half)

            @pl.when(jnp.logical_not(whole) & touched)
            def _():
                y = ffn(xs_ref[rows, :])
                row = lax.broadcasted_iota(jnp.int32, y.shape, 0) + p * half
                mine = (row >= lo) & (row < hi)

                @pl.when(first_visit)
                def _():
                    ys_ref[rows, :] = jnp.where(mine, y, jnp.zeros_like(y))

                @pl.when(jnp.logical_not(first_visit))
                def _():
                    ys_ref[rows, :] = jnp.where(mine, y, ys_ref[rows, :])

            @pl.when(jnp.logical_not(whole) & jnp.logical_not(touched) & first_visit)
            def _():
                ys_ref[rows, :] = jnp.zeros((half, ys_ref.shape[1]), U32)


def moe_ffn(xs, work, w13_all, w2_all, layer):
    n_slots, dp = xs.shape
    d = 2 * dp
    n_work = work[0].shape[0]
    xmap = lambda w, *prefetch: (prefetch[0][w], 0)
    grid_spec = pltpu.PrefetchScalarGridSpec(
        num_scalar_prefetch=len(work),
        grid=(n_work,),
        in_specs=[pl.BlockSpec((EXPERT_BLOCK, dp), xmap),
                  pl.BlockSpec(memory_space=pl.ANY), pl.BlockSpec(memory_space=pl.ANY)],
        out_specs=pl.BlockSpec((EXPERT_BLOCK, dp), xmap),
        scratch_shapes=[pltpu.VMEM((2, d, 2 * D_EXPERT), F32), pltpu.VMEM((2, D_EXPERT, d), F32),
                        pltpu.VMEM((d, 2 * D_EXPERT), BF16), pltpu.VMEM((D_EXPERT, d), BF16),
                        pltpu.SemaphoreType.DMA((2, 2))],
    )
    return pl.pallas_call(
        functools.partial(_ffn_kernel, layer=layer),
        out_shape=jax.ShapeDtypeStruct((n_slots, dp), U32),
        grid_spec=grid_spec,
        compiler_params=_cparams(("arbitrary",)),
        name="moe_ffn",
    )(*work, xs, w13_all, w2_all)


def _ffn_work_items(cnt):
    n_slots = 2 * T_ALL
    n_blocks = n_slots // EXPERT_BLOCK
    n_work = n_blocks + N_EXPERTS - 1
    end = jnp.cumsum(cnt)
    start = end - cnt
    first_blk = start // EXPERT_BLOCK
    last_blk = jnp.maximum(end - 1, start) // EXPERT_BLOCK
    n_items = jnp.where(cnt > 0, last_blk - first_blk + 1, 0)
    item_end = jnp.cumsum(n_items)
    item_start = item_end - n_items
    w = jnp.arange(n_work, dtype=jnp.int32)
    used = w < item_end[-1]
    wq = jnp.minimum(w, item_end[-1] - 1)
    e = jnp.sum((item_end[:, None] <= wq[None, :]).astype(jnp.int32), axis=0)
    onehot = e[None, :] == jnp.arange(N_EXPERTS, dtype=jnp.int32)[:, None]
    of_e = lambda table: jnp.sum(jnp.where(onehot, table[:, None], 0), axis=0)
    blk = jnp.where(used, of_e(first_blk) + (w - of_e(item_start)), n_blocks - 1).astype(jnp.int32)
    lo = jnp.maximum(of_e(start), blk * EXPERT_BLOCK) - blk * EXPERT_BLOCK
    hi = jnp.minimum(of_e(end), (blk + 1) * EXPERT_BLOCK) - blk * EXPERT_BLOCK
    lo = jnp.where(used, lo, 0).astype(jnp.int32)
    hi = jnp.where(used, hi, 0).astype(jnp.int32)
    e_before = jnp.concatenate([jnp.full((1,), -1, jnp.int32), e[:-1]])
    change = used & (e != e_before)
    parity = ((jnp.cumsum(change.astype(jnp.int32)) - 1) % 2).astype(jnp.int32)
    far = jnp.int32(n_work)
    next_change = lax.cummin(jnp.where(change, w, far), axis=0, reverse=True)
    next_change = jnp.concatenate([next_change[1:], jnp.full((1,), far, jnp.int32)])
    e_next = jnp.sum(jnp.where(next_change[None, :] == w[:, None], e[:, None], 0), axis=0)
    e_next = jnp.where(next_change < far, e_next, -1).astype(jnp.int32)
    return start, (blk, e, lo, hi, e_next, parity)


def hier_moe_layer(a, w_out_bf16, x, layer, gain, w_group, b_group, w_expert, b_expert, w13_all, w2_all, utri,
                   finish):
    t = T_ALL
    pad = LANES - N_EXPERTS - N_GROUPS
    w_router = jnp.concatenate([w_expert, w_group, jnp.zeros((D_MODEL, pad), F32)], axis=1)
    b_router = jnp.concatenate([b_expert, b_group, jnp.zeros((pad,), F32)])[:ROUTE_ROWS].reshape(ROUTE_ROWS, 1)
    h, xn, rec, rect, counts = moe_route(a, w_out_bf16, x, gain, w_router, b_router, utri)

    cnt = counts[:N_EXPERTS, 0].astype(jnp.int32)
    start, work = _ffn_work_items(cnt)
    experts = jnp.arange(N_EXPERTS, dtype=jnp.int32)[:, None]

    def slot_of(e_row, rank_row):
        first = jnp.sum(jnp.where(e_row.astype(jnp.int32)[None, :] == experts, start[:, None], 0), axis=0)
        return (first + rank_row.astype(jnp.int32)).reshape(1, t)

    dest_a = slot_of(rect[0], rect[2])
    dest_b = slot_of(rect[1], rect[3])

    xs = moe_dispatch_sc(xn, dest_a, dest_b, 2 * t)
    ys = moe_ffn(xs, work, w13_all, w2_all, layer)
    ga, gb = moe_gather_sc(ys, dest_a, dest_b)
    return finish(h, rec, ga, gb)


def _rope_tables(pos):
    half = ROPE_DIM // 2
    lane = np.arange(LANES) % HEAD_DIM
    rotary = lane < ROPE_DIM
    inv = jnp.where(rotary, jnp.exp(-math.log(ROPE_THETA) * jnp.asarray(lane % half, F32) * (2.0 / ROPE_DIM)), 0.0)
    ang = pos.astype(F32)[:, None] * inv[None, :]
    cos, sin = jnp.cos(ang), jnp.sin(ang)
    first = jnp.asarray(lane < half)
    second = jnp.asarray(rotary & (lane >= half))
    return cos, jnp.where(first, -sin, 0.0), jnp.where(second, sin, 0.0)


def kernel(x_prompt, x_sample, state_hgrn, cache_k_win, cache_v_win, meta_tokens, a_norm, a_w_in, a_lower_logits, a_out_norm, a_w_out, kv_norm, kv_w, k_norm, b_norm, b_wq, b_q_norm, b_sinks, b_w_out, moe_norm, moe_w_group, moe_b_group, moe_w_expert, moe_b_expert, moe_w13, moe_w2):
    tail_rows = T_ALL - OFF_META
    x_parts = (x_prompt.reshape(T_PROMPT, D_MODEL), x_sample.reshape(T_SAMPLE, D_MODEL),
               jnp.concatenate([meta_tokens.astype(F32), jnp.zeros((tail_rows - N_META, D_MODEL), F32)], axis=0))
    pos = jnp.concatenate([N_META + jnp.arange(SEQ, dtype=jnp.int32),
                           jnp.tile(PAST_LEN + jnp.arange(DEC_SEQ, dtype=jnp.int32), ROW_TILE // DEC_SEQ),
                           jnp.arange(N_META, dtype=jnp.int32),
                           jnp.zeros((ROW_TILE - N_META,), jnp.int32)])
    cos_t, sina_t, sinb_t = _rope_tables(pos)
    rs = np.arange(NORM_SLAB)
    hmean = jnp.asarray((rs[:, None] // HEAD_DIM == rs[None, :] // HEAD_DIM).astype(np.float32) / HEAD_DIM, BF16)
    rt = np.arange(ROW_TILE)
    utri = jnp.asarray((rt[:, None] < rt[None, :]).astype(np.float32), BF16)
    lower = jnp.cumsum(jax.nn.softmax(a_lower_logits.astype(F32), axis=0), axis=0)

    moe = functools.partial(hier_moe_layer, w13_all=moe_w13, w2_all=moe_w2, utri=utri)

    z = in_project(x_parts, a_norm[0], a_w_in[0].astype(BF16))
    zero_state = jnp.zeros((1, A_HEADS, A_DK, A_DV), F32)
    o_meta, s_meta = hgrn2_scan(z, zero_state, lower[0], a_out_norm[0],
                                row_off=OFF_META, n_seq=1, seq_len=N_META)
    o_prompt, s_prompt = hgrn2_scan(z, s_meta, lower[0], a_out_norm[0], row_off=0, n_seq=BATCH, seq_len=SEQ)
    o_sample, s_sample = hgrn2_scan(z, state_hgrn[0].astype(F32), lower[0], a_out_norm[0],
                                    row_off=OFF_SAMPLE, n_seq=DEC_BATCH, seq_len=DEC_SEQ, group=SCAN_SAMPLE_GROUP)
    o_tail = jnp.concatenate([o_meta, jnp.zeros((tail_rows - N_META, D_MODEL), BF16)], axis=0)
    finish0 = functools.partial(
        moe_combine_kv_q, kv_gain=kv_norm, kv_w_bf16=kv_w.astype(BF16), q_gain=b_norm[0], wq_bf16=b_wq[0].astype(BF16),
        hmean=hmean, k_hgain=jnp.tile(k_norm, KV_HEADS).reshape(1, KV_DIM),
        q_hgain=jnp.tile(b_q_norm[0], Q_HEADS).reshape(1, D_MODEL), cos_t=cos_t, sina_t=sina_t, sinb_t=sinb_t)
    h, k_all, v_all, q_all = moe((o_prompt, o_sample, o_tail), a_w_out[0].astype(BF16), x_parts, 0, moe_norm[0],
                                 moe_w_group[0], moe_b_group[0], moe_w_expert[0], moe_b_expert[0], finish=finish0)

    meta_blk = lambda a: jnp.concatenate([jnp.zeros((ATT_BLOCK - N_META, KV_DIM), F32),
                                          a[OFF_META:OFF_META + N_META]], axis=0)
    sinks = b_sinks[0].astype(F32)
    att_all = attention_prompt(q_all, k_all, v_all, meta_blk(k_all), meta_blk(v_all), sinks)
    att_all, k_win_s, v_win_s = attention_sample(q_all, cache_k_win.astype(F32), cache_v_win.astype(F32),
                                                 k_all, v_all, sinks, att_all)
    y_p, y_s = moe(att_all, b_w_out[0].astype(BF16), h, 1, moe_norm[1], moe_w_group[1], moe_b_group[1],
                   moe_w_expert[1], moe_b_expert[1], finish=functools.partial(moe_combine_dense, split=True))

    y_prompt = y_p.reshape(BATCH, SEQ, D_MODEL)
    y_sample = y_s.reshape(DEC_BATCH, DEC_SEQ, D_MODEL)
    last = lambda a: jnp.stack([a[(b + 1) * SEQ - WINDOW:(b + 1) * SEQ] for b in range(BATCH)]).reshape(
        BATCH, WINDOW, KV_HEADS, HEAD_DIM)
    kp = last(k_all)
    vp = last(v_all)
    return (y_prompt, y_sample, s_prompt[None], s_sample[None], kp, vp, k_win_s, v_win_s)
```

```python
import functools
import math

import numpy as np
import jax
import jax.numpy as jnp
from jax import lax
from jax.experimental import pallas as pl
from jax.experimental.pallas import tpu as pltpu
from jax.experimental.pallas import tpu_sc as plsc

F32 = jnp.float32
BF16 = jnp.bfloat16
U32 = jnp.uint32

D_MODEL = 1024
BATCH = 4
SEQ = 4096
DEC_BATCH = 128
DEC_SEQ = 8
PAST_LEN = 8192
N_META = 16
A_HEADS = 8
A_DK = 128
A_DV = 128
Q_HEADS = 16
KV_HEADS = 4
HEAD_DIM = 64
KV_DIM = KV_HEADS * HEAD_DIM
WINDOW = 128
ROPE_DIM = 16
ROPE_THETA = 500000.0
N_GROUPS = 4
EXPERTS_PER_GROUP = 8
N_EXPERTS = 32
D_EXPERT = 512
RMS_EPS = 1e-6

LANES = 128
SUBLANES = 8
MXU_DIM = 256
VMEM_BYTES = 64 * 1024 * 1024
VMEM_LIMIT = VMEM_BYTES - 8 * 1024 * 1024
NORM_SLAB = MXU_DIM

ROW_TILE = 512
T_PROMPT = BATCH * SEQ
T_SAMPLE = DEC_BATCH * DEC_SEQ
OFF_SAMPLE = T_PROMPT
OFF_META = T_PROMPT + T_SAMPLE
T_REAL = OFF_META + N_META
T_ALL = -(-T_REAL // ROW_TILE) * ROW_TILE
N_TILES = T_ALL // ROW_TILE

SCAN_CHUNK = 128
SCAN_SAMPLE_GROUP = 16
ATT_BLOCK = 128
EXPERT_BLOCK = 512
FFN_PART = 256
FFN_CHUNKS = 2


def _cparams(sem):
    return pltpu.CompilerParams(dimension_semantics=sem, vmem_limit_bytes=VMEM_LIMIT)


def _nt_dot(a, b):
    return lax.dot_general(a, b, (((1,), (1,)), ((), ())), preferred_element_type=F32)


def _rms(x, gain):
    ms = jnp.mean(x * x, axis=-1, keepdims=True)
    return x * lax.rsqrt(ms + RMS_EPS) * gain


def _silu(x):
    return x * jax.nn.sigmoid(x)


def _pack_halves(x):
    w = x.shape[1] // 2
    hi = lax.bitcast_convert_type(x[:, :w].astype(BF16).astype(F32), U32)
    lo = lax.bitcast_convert_type(x[:, w:].astype(BF16).astype(F32), U32)
    return hi | (lo >> 16)


def _unpack_halves(p):
    hi = lax.bitcast_convert_type(p & jnp.uint32(0xFFFF0000), F32)
    lo = lax.bitcast_convert_type(p << 16, F32)
    return jnp.concatenate([hi, lo], axis=1)


N_PROMPT_TILES = T_PROMPT // ROW_TILE
N_SAMPLE_TILES = T_SAMPLE // ROW_TILE


def _parts_specs(width):
    return [pl.BlockSpec((ROW_TILE, width), lambda i: (jnp.minimum(i, N_PROMPT_TILES - 1), 0)),
            pl.BlockSpec((ROW_TILE, width), lambda i: (jnp.clip(i - N_PROMPT_TILES, 0, N_SAMPLE_TILES - 1), 0)),
            pl.BlockSpec((ROW_TILE, width), lambda i: (0, 0))]


def _pick_part(i, p_ref, s_ref, t_ref, dtype):
    return jnp.where(i < N_PROMPT_TILES, p_ref[...].astype(dtype),
                     jnp.where(i < N_PROMPT_TILES + N_SAMPLE_TILES, s_ref[...].astype(dtype),
                               t_ref[...].astype(dtype)))


def _in_proj_kernel(xp_ref, xs_ref, xt_ref, g_ref, w_ref, o_ref):
    x = _pick_part(pl.program_id(0), xp_ref, xs_ref, xt_ref, F32)
    xn = _rms(x, g_ref[...])
    o_ref[...] = jnp.dot(xn.astype(BF16), w_ref[...], preferred_element_type=F32)


def in_project(x_parts, gain, w_bf16):
    d, n = w_bf16.shape
    return pl.pallas_call(
        _in_proj_kernel,
        out_shape=jax.ShapeDtypeStruct((T_ALL, n), F32),
        grid=(N_TILES,),
        in_specs=_parts_specs(d) + [pl.BlockSpec((1, d), lambda i: (0, 0)),
                                    pl.BlockSpec((d, n), lambda i: (0, 0))],
        out_specs=pl.BlockSpec((ROW_TILE, n), lambda i: (i, 0)),
        compiler_params=_cparams(("parallel",)),
        name="in_project",
    )(*x_parts, gain.reshape(1, d), w_bf16)


def _head_norm_rope(y, hmean_ref, hgain, cos_t, sina_t, sinb_t):
    rows, width = y.shape
    sq = (y * y).astype(BF16)
    parts = []
    for s in range(width // NORM_SLAB):
        parts.append(jnp.dot(sq[:, s * NORM_SLAB:(s + 1) * NORM_SLAB], hmean_ref[...], preferred_element_type=F32))
    ms = parts[0] if len(parts) == 1 else jnp.concatenate(parts, axis=1)
    yn = y * lax.rsqrt(ms + RMS_EPS) * hgain
    reps = width // LANES
    cos_w = jnp.concatenate([cos_t] * reps, axis=1)
    sina_w = jnp.concatenate([sina_t] * reps, axis=1)
    sinb_w = jnp.concatenate([sinb_t] * reps, axis=1)
    half = ROPE_DIM // 2
    nxt = pltpu.roll(yn, width - half, 1)
    prv = pltpu.roll(yn, half, 1)
    return yn * cos_w + nxt * sina_w + prv * sinb_w


def _rope_tile(i):
    tiles_per_seq = SEQ // ROW_TILE
    n_prompt_tiles = T_PROMPT // ROW_TILE
    n_sample_tiles = T_SAMPLE // ROW_TILE
    return (jnp.where(i < n_prompt_tiles, i % tiles_per_seq,
                      jnp.where(i < n_prompt_tiles + n_sample_tiles, tiles_per_seq, tiles_per_seq + 1)), 0)


def _scan_levels(c):
    levels = []
    m = c
    while m >= 2:
        levels.append(m)
        m //= 2
    return levels


LOG2E = 1.4426950408889634


def _scan_kernel(z_ref, s0_ref, lb_ref, og_ref, tri_ref, lmask_ref, sgn_ref, o_ref, sfin_ref, s_scr, b_scr,
                 *, rows, seq_len):
    c_idx = pl.program_id(1)
    levels = _scan_levels(seq_len)
    n_sub = rows // seq_len
    hk = A_HEADS * A_DK

    @pl.when(c_idx == 0)
    def _():
        s_scr[...] = s0_ref[...]

    sub = lax.broadcasted_iota(jnp.int32, (SUBLANES, LANES), 0)
    row = lax.broadcasted_iota(jnp.int32, (LANES, LANES), 0)
    og = og_ref[...]

    def pad_f32(x):
        if x.shape[0] == LANES:
            return x
        return jnp.concatenate([x, jnp.zeros((LANES - x.shape[0], x.shape[1]), x.dtype)], axis=0)

    def pad_rows(x):
        return pad_f32(x).astype(BF16)

    def cols(part, h):
        return slice(part * hk + h * LANES, part * hk + (h + 1) * LANES)

    def gates(h):
        lb = lb_ref[:, cols(0, h)]
        forget = lb + (1.0 - lb) * jax.nn.sigmoid(z_ref[:, cols(1, h)])
        logf = jnp.log(forget)
        hi = logf.astype(BF16).astype(F32)
        lo = logf - hi
        cs = jnp.dot(tri_ref[...], pad_rows(jnp.concatenate([hi, lo], axis=1)),
                     preferred_element_type=F32)
        b = cs[:rows, :LANES] + cs[:rows, LANES:]
        b_scr[h] = b
        return _silu(z_ref[:, cols(0, h)]), 1.0 - forget, b

    def bref_for(h, m):
        b_rows = b_scr.at[h]
        half = m // 2
        pieces = []
        for g in range(rows // SUBLANES):
            base = g * SUBLANES
            if m >= SUBLANES:
                r = (base // m) * m + half - 1
                piece = jnp.broadcast_to(b_rows[r:r + 1, :], (SUBLANES, LANES))
            else:
                piece = jnp.broadcast_to(b_rows[base + half - 1:base + half, :], (SUBLANES, LANES))
                for blk in range(1, SUBLANES // m):
                    r = base + blk * m + half - 1
                    piece = jnp.where(sub >= blk * m,
                                      jnp.broadcast_to(b_rows[r:r + 1, :], (SUBLANES, LANES)), piece)
            pieces.append(piece)
        return pieces[0] if len(pieces) == 1 else jnp.concatenate(pieces, axis=0)

    heads = range(A_HEADS)
    qkb = [gates(h) for h in heads]
    att = [_nt_dot(pad_rows(qf), pad_rows(kf)) * lmask_ref[len(levels)] for qf, kf, _ in qkb]
    for li, m in enumerate(levels):
        for h in heads:
            qf, kf, b = qkb[h]
            sgn = sgn_ref[li]
            e = jnp.exp2((b - bref_for(h, m)) * sgn)
            w = pad_rows(jnp.where(sgn > 0, qf, kf) * e)
            att[h] = att[h] + _nt_dot(w, w) * lmask_ref[li]

    def finish(h):
        qf, kf, b = qkb[h]
        b_rows = b_scr.at[h]
        v_b = pad_rows(z_ref[:, cols(2, h)])
        o_intra = jnp.dot(att[h].astype(BF16), v_b, preferred_element_type=F32)
        eb = jnp.exp(b)
        qs = qf * eb
        b_end = [jnp.broadcast_to(b_rows[(i + 1) * seq_len - 1:(i + 1) * seq_len, :], (seq_len, LANES))
                 for i in range(n_sub)]
        b_end = b_end[0] if n_sub == 1 else jnp.concatenate(b_end, axis=0)
        kd_t = pad_f32(kf * jnp.exp(b_end - b)).T.astype(BF16)
        eb_t = pad_f32(eb).T
        qs_b = pad_rows(qs)
        o = o_intra
        for i in range(n_sub):
            s_old = s_scr[i, h]
            first, last = i * seq_len, (i + 1) * seq_len - 1
            if n_sub == 1:
                qs_i, v_i = qs_b, v_b
            else:
                mine = (row >= first) & (row <= last)
                qs_i = jnp.where(mine, qs_b, jnp.zeros_like(qs_b))
                v_i = jnp.where(mine, v_b, jnp.zeros_like(v_b))
            o = o + jnp.dot(qs_i, s_old.astype(BF16), preferred_element_type=F32)
            decay = jnp.broadcast_to(eb_t[:, last:last + 1], (LANES, LANES))
            s_scr[i, h] = decay * s_old + jnp.dot(kd_t, v_i, preferred_element_type=F32)
        o = o[:rows]

        on = _rms(o, og) * _silu(z_ref[:, cols(3, h)])
        o_ref[:, cols(0, h)] = on.astype(o_ref.dtype)

    for h in heads:
        finish(h)

    @pl.when(c_idx == pl.num_programs(1) - 1)
    def _():
        sfin_ref[...] = s_scr[...]


def _scan_consts(rows, seq_len):
    levels = _scan_levels(seq_len)
    r = np.arange(LANES)
    t, s = r[:, None], r[None, :]
    live = (t < rows) & (s < rows)
    tri = ((s <= t) & (t // seq_len == s // seq_len) & live).astype(np.float32)
    masks, sgns = [], []
    for m in levels:
        masks.append(((t // m == s // m) & (t % m >= m // 2) & (s % m < m // 2) & live).astype(np.float32))
        sgns.append(np.broadcast_to(np.where(r[:rows, None] % m >= m // 2, LOG2E, -LOG2E), (rows, LANES)))
    masks.append(((t == s) & live).astype(np.float32))
    return jnp.asarray(tri, BF16), jnp.asarray(np.stack(masks), F32), jnp.asarray(np.stack(sgns), F32)


def hgrn2_scan(z, s0, lb, o_gain, *, row_off, n_seq, seq_len, group=1):
    hv = A_HEADS * A_DV
    if seq_len > SCAN_CHUNK:
        assert group == 1
        sub_len, rows, n_chunks, n_steps = SCAN_CHUNK, SCAN_CHUNK, seq_len // SCAN_CHUNK, n_seq
    else:
        sub_len, rows, n_chunks, n_steps = seq_len, group * seq_len, 1, n_seq // group
    blk_off = row_off // rows
    tri, lmask, sgn = _scan_consts(rows, sub_len)
    shared_s0 = s0.shape[0] == 1
    fix2 = lambda s, c: (0, 0)
    fix3 = lambda s, c: (0, 0, 0)
    o, sfin = pl.pallas_call(
        functools.partial(_scan_kernel, rows=rows, seq_len=sub_len),
        out_shape=(jax.ShapeDtypeStruct((n_seq * seq_len, hv), BF16 if rows % 16 == 0 else F32),
                   jax.ShapeDtypeStruct((n_seq, A_HEADS, A_DK, A_DV), F32)),
        grid=(n_steps, n_chunks),
        in_specs=[pl.BlockSpec((rows, 4 * hv), lambda s, c: (blk_off + s * n_chunks + c, 0)),
                  pl.BlockSpec((group, A_HEADS, A_DK, A_DV), (lambda s, c: (0, 0, 0, 0)) if shared_s0
                               else (lambda s, c: (s, 0, 0, 0))),
                  pl.BlockSpec((1, hv), fix2), pl.BlockSpec((1, A_DV), fix2),
                  pl.BlockSpec((LANES, LANES), fix2), pl.BlockSpec(lmask.shape, fix3),
                  pl.BlockSpec(sgn.shape, fix3)],
        out_specs=(pl.BlockSpec((rows, hv), lambda s, c: (s * n_chunks + c, 0)),
                   pl.BlockSpec((group, A_HEADS, A_DK, A_DV), lambda s, c: (s, 0, 0, 0))),
        scratch_shapes=[pltpu.VMEM((group, A_HEADS, A_DK, A_DV), F32), pltpu.VMEM((A_HEADS, rows, LANES), F32)],
        compiler_params=_cparams(("parallel", "arbitrary")),
        name=f"hgrn2_scan_r{rows}",
    )(z, s0, lb.reshape(1, hv), o_gain.reshape(1, A_DV), tri, lmask, sgn)
    return o, sfin


KEYS = 2 * ATT_BLOCK
ATT_STEP = 1


def _pair_operand(x, kh):
    slab = x[:, (kh // 2) * LANES:(kh // 2 + 1) * LANES]
    lane = lax.broadcasted_iota(jnp.int32, slab.shape, 1)
    if kh % 2 == 0:
        lo = jnp.where(lane < HEAD_DIM, slab, 0.0)
        hi = pltpu.roll(lo, HEAD_DIM, 1)
    else:
        hi = jnp.where(lane >= HEAD_DIM, slab, 0.0)
        lo = pltpu.roll(hi, HEAD_DIM, 1)
    return jnp.concatenate([lo, hi], axis=0).astype(BF16)


def _window_bias(rows, jmin):
    t_i = lax.broadcasted_iota(jnp.int32, (rows, 2 * KEYS), 0)
    c_i = lax.broadcasted_iota(jnp.int32, (rows, 2 * KEYS), 1)
    j_i = c_i & (ATT_BLOCK - 1)
    own = (c_i & ATT_BLOCK) != 0
    ok = (own & (j_i <= t_i)) | (jnp.logical_not(own) & (j_i >= t_i) & (j_i >= jmin))
    return jnp.where(ok, 0.0, -jnp.inf).astype(F32)


def _pair_softmax(s, sink_a, sink_b):
    probs, rinv = [], []
    for hh, sink in enumerate((sink_a, sink_b)):
        sh = s[:, hh * KEYS:(hh + 1) * KEYS]
        m = jnp.maximum(jnp.max(sh, axis=-1, keepdims=True), sink)
        p = jnp.exp(sh - m)
        den = jnp.sum(p, axis=-1, keepdims=True) + jnp.exp(sink - m)
        probs.append(p.astype(BF16))
        rinv.append(1.0 / den)
    lane = lax.broadcasted_iota(jnp.int32, (s.shape[0], LANES), 1)
    return jnp.concatenate(probs, axis=1), jnp.where(lane < HEAD_DIM, rinv[0], rinv[1])


def _attn_prompt_kernel(sink_ref, q_ref, kp_ref, ko_ref, vp_ref, vo_ref, km_ref, vm_ref, o_ref,
                        k2_scr, v2_scr, s_scr, p_scr, r_scr, bias_scr):
    n = pl.program_id(0)
    steps_per_seq = SEQ // (ATT_STEP * ATT_BLOCK)
    n_pairs = Q_HEADS // 2
    blk = ATT_BLOCK

    @pl.when(n == 0)
    def _():
        bias_scr[0] = _window_bias(ATT_BLOCK, 0)
        bias_scr[1] = _window_bias(ATT_BLOCK, ATT_BLOCK - N_META)

    @pl.when(n >= BATCH * steps_per_seq)
    def _():
        o_ref[...] = jnp.zeros_like(o_ref)

    @pl.when(n < BATCH * steps_per_seq)
    def _():
        first = (n % steps_per_seq) == 0
        own_k, own_v = ko_ref[...], vo_ref[...]
        keys = [jnp.where(first, km_ref[...], kp_ref[...])] + [own_k[u * blk:(u + 1) * blk] for u in range(ATT_STEP)]
        vals = [jnp.where(first, vm_ref[...], vp_ref[...])] + [own_v[u * blk:(u + 1) * blk] for u in range(ATT_STEP)]
        biases = [bias_scr[first.astype(jnp.int32)]] + [bias_scr[0]] * (ATT_STEP - 1)
        for u in range(ATT_STEP):
            k = jnp.concatenate(keys[u:u + 2], axis=0)
            v = jnp.concatenate(vals[u:u + 2], axis=0)
            for kh in range(KV_HEADS):
                k2_scr[u, kh] = _pair_operand(k, kh)
                v2_scr[u, kh] = _pair_operand(v, kh)
        units = [(u, pair) for u in range(ATT_STEP) for pair in range(n_pairs)]
        rows = lambda u: slice(u * blk, (u + 1) * blk)
        lanes = lambda pair: slice(pair * LANES, (pair + 1) * LANES)
        for j, (u, pair) in enumerate(units):
            s_scr[j] = _nt_dot(q_ref[rows(u), lanes(pair)], k2_scr[u, pair // 2]) + biases[u]
        for j, (u, pair) in enumerate(units):
            p, rinv = _pair_softmax(s_scr[j], sink_ref[2 * pair], sink_ref[2 * pair + 1])
            p_scr[j] = p
            r_scr[j] = rinv
        for j, (u, pair) in enumerate(units):
            o = jnp.dot(p_scr[j], v2_scr[u, pair // 2], preferred_element_type=F32) * r_scr[j]
            o_ref[rows(u), lanes(pair)] = o.astype(o_ref.dtype)


def attention_prompt(q_all, k_all, v_all, k_meta_blk, v_meta_blk, sinks):
    step_rows = ATT_STEP * ATT_BLOCK
    n_prompt_steps = T_PROMPT // step_rows
    n_steps = T_ALL // step_rows
    n_units = ATT_STEP * (Q_HEADS // 2)
    own = lambda n, sk: (jnp.minimum(n, n_prompt_steps - 1), 0)
    prev = lambda n, sk: (jnp.maximum(ATT_STEP * jnp.minimum(n, n_prompt_steps - 1) - 1, 0), 0)
    fix = lambda n, sk: (0, 0)
    grid_spec = pltpu.PrefetchScalarGridSpec(
        num_scalar_prefetch=1,
        grid=(n_steps,),
        in_specs=[pl.BlockSpec((step_rows, D_MODEL), own),
                  pl.BlockSpec((ATT_BLOCK, KV_DIM), prev), pl.BlockSpec((step_rows, KV_DIM), own),
                  pl.BlockSpec((ATT_BLOCK, KV_DIM), prev), pl.BlockSpec((step_rows, KV_DIM), own),
                  pl.BlockSpec((ATT_BLOCK, KV_DIM), fix), pl.BlockSpec((ATT_BLOCK, KV_DIM), fix)],
        out_specs=pl.BlockSpec((step_rows, D_MODEL), lambda n, sk: (n, 0)),
        scratch_shapes=[pltpu.VMEM((ATT_STEP, KV_HEADS, 2 * KEYS, LANES), BF16),
                        pltpu.VMEM((ATT_STEP, KV_HEADS, 2 * KEYS, LANES), BF16),
                        pltpu.VMEM((n_units, ATT_BLOCK, 2 * KEYS), F32),
                        pltpu.VMEM((n_units, ATT_BLOCK, 2 * KEYS), BF16),
                        pltpu.VMEM((n_units, ATT_BLOCK, LANES), F32),
                        pltpu.VMEM((2, ATT_BLOCK, 2 * KEYS), F32)],
    )
    return pl.pallas_call(
        _attn_prompt_kernel,
        out_shape=jax.ShapeDtypeStruct((T_ALL, D_MODEL), BF16),
        grid_spec=grid_spec,
        compiler_params=_cparams(("arbitrary",)),
        name="attention_prompt",
    )(sinks, q_all, k_all, k_all, v_all, v_all, k_meta_blk, v_meta_blk)


SAMPLE_GROUP = ATT_BLOCK // DEC_SEQ
SAMPLE_UNROLL = 4


def _attn_sample_kernel(sink_ref, q_ref, ck_ref, cv_ref, kn_ref, vn_ref, buf_ref, o_ref, kw_ref, vw_ref,
                        qf_scr, of_scr, k2_scr, v2_scr):
    del buf_ref
    qrows = 2 * DEC_SEQ
    qf_scr[...] = q_ref[...].astype(F32)
    bias = _window_bias(qrows, 0)
    zq = jnp.zeros((qrows - DEC_SEQ, D_MODEL), F32)
    zk = jnp.zeros((ATT_BLOCK - DEC_SEQ, KV_DIM), F32)

    n_pairs = Q_HEADS // 2
    lanes_of = lambda pair: slice(pair * LANES, (pair + 1) * LANES)

    def seq_group_body(it, carry):
        seqs = tuple(SAMPLE_UNROLL * it + u for u in range(SAMPLE_UNROLL))
        r_new = [pl.multiple_of(i * DEC_SEQ, DEC_SEQ) for i in seqs]
        qs = []
        for u, i in enumerate(seqs):
            qs.append(jnp.concatenate([qf_scr[pl.ds(r_new[u], DEC_SEQ), :], zq], axis=0).astype(BF16))
            old = lambda c_ref: jnp.concatenate([c_ref[i, :, kh, :] for kh in range(KV_HEADS)], axis=1)
            k = jnp.concatenate([old(ck_ref), kn_ref[pl.ds(r_new[u], DEC_SEQ), :], zk], axis=0)
            v = jnp.concatenate([old(cv_ref), vn_ref[pl.ds(r_new[u], DEC_SEQ), :], zk], axis=0)
            for c_ref, n_ref, w_ref in ((ck_ref, kn_ref, kw_ref), (cv_ref, vn_ref, vw_ref)):
                w_ref[i, 0:WINDOW - DEC_SEQ] = c_ref[i, DEC_SEQ:WINDOW]
                for kh in range(KV_HEADS):
                    w_ref[i, WINDOW - DEC_SEQ:WINDOW, kh, :] = n_ref[pl.ds(r_new[u], DEC_SEQ),
                                                                     kh * HEAD_DIM:(kh + 1) * HEAD_DIM]
            for kh in range(KV_HEADS):
                k2_scr[u, kh] = _pair_operand(k, kh)
                v2_scr[u, kh] = _pair_operand(v, kh)
        scores = [[_nt_dot(qs[u][:, lanes_of(pair)], k2_scr[u, pair // 2]) + bias for pair in range(n_pairs)]
                  for u in range(SAMPLE_UNROLL)]
        soft = [[_pair_softmax(s, sink_ref[2 * pair], sink_ref[2 * pair + 1]) for pair, s in enumerate(scores[u])]
                for u in range(SAMPLE_UNROLL)]
        for u in range(SAMPLE_UNROLL):
            for pair, (p, rinv) in enumerate(soft[u]):
                o = jnp.dot(p, v2_scr[u, pair // 2], preferred_element_type=F32) * rinv
                of_scr[pl.ds(r_new[u], DEC_SEQ), lanes_of(pair)] = o[:DEC_SEQ]
        return carry

    lax.fori_loop(0, SAMPLE_GROUP // SAMPLE_UNROLL, seq_group_body, 0)
    o_ref[...] = of_scr[...].astype(o_ref.dtype)


def attention_sample(q_all, cache_k, cache_v, k_all, v_all, sinks, out_buf):
    first_blk = OFF_SAMPLE // ATT_BLOCK
    new = lambda g, sk: (first_blk + g, 0)
    old = pl.BlockSpec((SAMPLE_GROUP, WINDOW, KV_HEADS, HEAD_DIM), lambda g, sk: (g, 0, 0, 0))
    grid_spec = pltpu.PrefetchScalarGridSpec(
        num_scalar_prefetch=1,
        grid=(DEC_BATCH // SAMPLE_GROUP,),
        in_specs=[pl.BlockSpec((ATT_BLOCK, D_MODEL), new),
                  old, old,
                  pl.BlockSpec((ATT_BLOCK, KV_DIM), new), pl.BlockSpec((ATT_BLOCK, KV_DIM), new),
                  pl.BlockSpec(memory_space=pl.ANY)],
        out_specs=(pl.BlockSpec((ATT_BLOCK, D_MODEL), new), old, old),
        scratch_shapes=[pltpu.VMEM((ATT_BLOCK, D_MODEL), F32), pltpu.VMEM((ATT_BLOCK, D_MODEL), F32),
                        pltpu.VMEM((SAMPLE_UNROLL, KV_HEADS, 2 * KEYS, LANES), BF16),
                        pltpu.VMEM((SAMPLE_UNROLL, KV_HEADS, 2 * KEYS, LANES), BF16)],
    )
    window = jax.ShapeDtypeStruct(cache_k.shape, cache_k.dtype)
    return pl.pallas_call(
        _attn_sample_kernel,
        out_shape=(jax.ShapeDtypeStruct(out_buf.shape, out_buf.dtype), window, window),
        grid_spec=grid_spec,
        input_output_aliases={6: 0},
        compiler_params=_cparams(("parallel",)),
        name="attention_sample",
    )(sinks, q_all, cache_k, cache_v, k_all, v_all, out_buf)


ROUTE_COLS = 8
ROUTE_ROWS = 48


def _route_kernel(*refs, parts):
    i = pl.program_id(0)
    if parts:
        (ap_ref, as_ref, at_ref, w_ref, xp_ref, xs_ref, xt_ref), refs = refs[:7], refs[7:]
        a = _pick_part(i, ap_ref, as_ref, at_ref, BF16)
        x = _pick_part(i, xp_ref, xs_ref, xt_ref, F32)
    else:
        (a_ref, w_ref, x_ref), refs = refs[:3], refs[3:]
        a, x = a_ref[...], x_ref[...]
    g_ref, whl_ref, br_ref, utri_ref, h_ref, xn_ref, rec_ref, rect_ref, cnt_ref, cnt_scr = refs

    @pl.when(i == 0)
    def _():
        cnt_scr[...] = jnp.zeros_like(cnt_scr)

    h = x + jnp.dot(a, w_ref[...], preferred_element_type=F32)
    h_ref[...] = h
    xn = _rms(h, g_ref[...])
    xn_ref[...] = _pack_halves(xn)
    xh = xn.astype(BF16)
    xl = (xn - xh.astype(F32)).astype(BF16)
    by_xh = _nt_dot(whl_ref[...], xh)
    logits = (by_xh[:LANES] + (by_xh[LANES:] + _nt_dot(whl_ref[:LANES, :], xl)))[:ROUTE_ROWS]
    logits = logits + br_ref[...]
    tokens = logits.shape[1]
    rid = lax.broadcasted_iota(jnp.int32, (ROUTE_ROWS, tokens), 0).astype(F32)
    neg = jnp.float32(-jnp.inf)
    big = jnp.float32(ROUTE_ROWS)

    is_g = (rid >= N_EXPERTS) & (rid < N_EXPERTS + N_GROUPS)
    gl = jnp.where(is_g, logits, neg)
    gmax = jnp.max(gl, axis=0, keepdims=True)
    gsel = jnp.min(jnp.where(gl == gmax, rid, big), axis=0, keepdims=True) - N_EXPERTS
    gden = jnp.sum(jnp.where(is_g, jnp.exp(gl - gmax), 0.0), axis=0, keepdims=True)
    gw = 1.0 / gden

    in_grp = (rid >= gsel * EXPERTS_PER_GROUP) & (rid < (gsel + 1) * EXPERTS_PER_GROUP)
    el = jnp.where(in_grp, logits, neg)
    t1 = jnp.max(el, axis=0, keepdims=True)
    e1 = jnp.min(jnp.where(el == t1, rid, big), axis=0, keepdims=True)
    el2 = jnp.where(rid == e1, neg, el)
    t2 = jnp.max(el2, axis=0, keepdims=True)
    e2 = jnp.min(jnp.where(el2 == t2, rid, big), axis=0, keepdims=True)
    x2 = jnp.exp(t2 - t1)
    w1 = gw / (1.0 + x2)
    w2 = gw * x2 / (1.0 + x2)

    oh1 = (rid == e1).astype(F32)
    oh2 = (rid == e2).astype(F32)
    oh = oh1 + oh2
    before = jnp.dot(oh.astype(BF16), utri_ref[...], preferred_element_type=F32)
    base = cnt_scr[...] + before
    r1 = jnp.sum(base * oh1, axis=0, keepdims=True)
    r2 = jnp.sum(base * oh2, axis=0, keepdims=True)
    cnt_scr[...] = cnt_scr[...] + jnp.sum(oh, axis=1, keepdims=True)

    zero = jnp.zeros_like(w1)
    rect = jnp.concatenate([e1, e2, r1, r2, w1, w2, zero, zero], axis=0)
    rect_ref[...] = rect
    wide = jnp.concatenate([rect, jnp.zeros((LANES - ROUTE_COLS, tokens), F32)], axis=0)
    rec_ref[...] = jnp.concatenate([wide[:, t0:t0 + LANES].T for t0 in range(0, tokens, LANES)], axis=0)
    cnt_ref[...] = cnt_scr[...]


def moe_route(a, w_out_bf16, x, gain, w_router, b_router, utri):
    parts = isinstance(a, tuple)
    t, d = T_ALL, D_MODEL
    row = lambda i: (i, 0)
    fix = lambda i: (0, 0)
    w_t = w_router.T
    w_hi = w_t.astype(BF16)
    w_lo = (w_t - w_hi.astype(F32)).astype(BF16)
    w_spec = pl.BlockSpec((d, d), fix)
    if parts:
        pre_specs = _parts_specs(d) + [w_spec] + _parts_specs(d)
        pre_args = (*a, w_out_bf16, *x)
    else:
        pre_specs = [pl.BlockSpec((ROW_TILE, d), row), w_spec, pl.BlockSpec((ROW_TILE, d), row)]
        pre_args = (a, w_out_bf16, x)
    return pl.pallas_call(
        functools.partial(_route_kernel, parts=parts),
        out_shape=(jax.ShapeDtypeStruct((t, d), F32),
                   jax.ShapeDtypeStruct((t, d // 2), U32), jax.ShapeDtypeStruct((t, LANES), F32),
                   jax.ShapeDtypeStruct((ROUTE_COLS, t), F32), jax.ShapeDtypeStruct((ROUTE_ROWS, 1), F32)),
        grid=(t // ROW_TILE,),
        in_specs=pre_specs + [pl.BlockSpec((1, d), fix), pl.BlockSpec((2 * LANES, d), fix),
                              pl.BlockSpec((ROUTE_ROWS, 1), fix), pl.BlockSpec((ROW_TILE, ROW_TILE), fix)],
        out_specs=(pl.BlockSpec((ROW_TILE, d), row),
                   pl.BlockSpec((ROW_TILE, d // 2), row), pl.BlockSpec((ROW_TILE, LANES), row),
                   pl.BlockSpec((ROUTE_COLS, ROW_TILE), lambda i: (0, i)), pl.BlockSpec((ROUTE_ROWS, 1), fix)),
        scratch_shapes=[pltpu.VMEM((ROUTE_ROWS, 1), F32)],
        compiler_params=_cparams(("arbitrary",)),
        name="moe_route",
    )(*pre_args, gain.reshape(1, d), jnp.concatenate([w_hi, w_lo], axis=0), b_router, utri)


SC_WINDOW = 64
SC_INDEX_WINDOW = 128


def _sc_mesh():
    return plsc.VectorSubcoreMesh(core_axis_name="core", subcore_axis_name="subcore")


def moe_dispatch_sc(xn, dest_a, dest_b, n_slots):
    t, d = xn.shape

    n_parts = SC_INDEX_WINDOW // SC_WINDOW

    @pl.kernel(out_type=jax.ShapeDtypeStruct((n_slots, d), xn.dtype), mesh=_sc_mesh(),
               scratch_types=[pltpu.VMEM((2, SC_WINDOW, d), xn.dtype), pltpu.SemaphoreType.DMA((2,)),
                              pltpu.SemaphoreType.DMA((2,))],
               name="moe_dispatch_sc")
    def run(x_hbm, id_hbm, da_hbm, db_hbm, o_hbm, buf, load_sem, store_sem):
        def body(id_vmem, da_vmem, db_vmem):
            part = lambda j: pl.ds(j * SC_WINDOW, SC_WINDOW)
            load = lambda j: pltpu.make_async_copy(x_hbm.at[id_vmem.at[0, part(j)]], buf.at[j % 2], load_sem.at[j % 2])
            load(0).start()
            for j in range(n_parts):
                load(j).wait()
                if j + 1 < n_parts:
                    load(j + 1).start()
                stores = [pltpu.make_async_copy(buf.at[j % 2], o_hbm.at[dv.at[0, part(j)]], store_sem.at[k])
                          for k, dv in enumerate((da_vmem, db_vmem))]
                for s in stores:
                    s.start()
                for s in stores:
                    s.wait()

        idx_spec = pl.BlockSpec((1, SC_INDEX_WINDOW), lambda i: (0, i))
        pltpu.emit_pipeline(
            body,
            grid=(t // SC_INDEX_WINDOW,),
            in_specs=[idx_spec, idx_spec, idx_spec],
            out_specs=[],
            core_axis_name=("core", "subcore"),
            dimension_semantics=(pltpu.PARALLEL,),
        )(id_hbm, da_hbm, db_hbm)

    return run(xn, jnp.arange(t, dtype=jnp.int32).reshape(1, t), dest_a, dest_b)


def moe_gather_sc(ys, dest_a, dest_b):
    d = ys.shape[1]
    t = dest_a.shape[1]
    out = jax.ShapeDtypeStruct((t, d), ys.dtype)

    n_moves = 2 * (SC_INDEX_WINDOW // SC_WINDOW)

    @pl.kernel(out_type=(out, out), mesh=_sc_mesh(),
               scratch_types=[pltpu.VMEM((2, SC_WINDOW, d), ys.dtype), pltpu.SemaphoreType.DMA((2,)),
                              pltpu.SemaphoreType.DMA((2,))],
               name="moe_gather_sc")
    def run(y_hbm, id_hbm, da_hbm, db_hbm, ga_hbm, gb_hbm, buf, load_sem, store_sem):
        def body(id_vmem, da_vmem, db_vmem):
            part = lambda m: pl.ds((m // 2) * SC_WINDOW, SC_WINDOW)
            src = lambda m: (da_vmem, db_vmem)[m % 2]
            dst = lambda m: (ga_hbm, gb_hbm)[m % 2]
            load = lambda m: pltpu.make_async_copy(y_hbm.at[src(m).at[0, part(m)]], buf.at[m % 2], load_sem.at[m % 2])
            store = lambda m: pltpu.make_async_copy(buf.at[m % 2], dst(m).at[id_vmem.at[0, part(m)]],
                                                    store_sem.at[m % 2])
            load(0).start()
            for m in range(n_moves):
                load(m).wait()
                if m >= 1:
                    store(m - 1).wait()
                if m + 1 < n_moves:
                    load(m + 1).start()
                store(m).start()
            store(n_moves - 1).wait()

        idx_spec = pl.BlockSpec((1, SC_INDEX_WINDOW), lambda i: (0, i))
        pltpu.emit_pipeline(
            body,
            grid=(t // SC_INDEX_WINDOW,),
            in_specs=[idx_spec, idx_spec, idx_spec],
            out_specs=[],
            core_axis_name=("core", "subcore"),
            dimension_semantics=(pltpu.PARALLEL,),
        )(id_hbm, da_hbm, db_hbm)

    return run(ys, jnp.arange(t, dtype=jnp.int32).reshape(1, t), dest_a, dest_b)


def _combine_dense_kernel(h_ref, rec_ref, ga_ref, gb_ref, *out_refs, split):
    i = pl.program_id(0)
    rec = rec_ref[...]
    res = h_ref[...] + rec[:, 4:5] * _unpack_halves(ga_ref[...]) + rec[:, 5:6] * _unpack_halves(gb_ref[...])
    if not split:
        out_refs[0][...] = res
    else:
        @pl.when(i < N_PROMPT_TILES)
        def _():
            out_refs[0][...] = res

        @pl.when((i >= N_PROMPT_TILES) & (i < N_PROMPT_TILES + N_SAMPLE_TILES))
        def _():
            out_refs[1][...] = res


def moe_combine_dense(h, rec, ga, gb, split=False):
    t, d = h.shape
    row = lambda i: (i, 0)
    if split:
        out_shape = (jax.ShapeDtypeStruct((T_PROMPT, d), F32), jax.ShapeDtypeStruct((T_SAMPLE, d), F32))
        out_specs = tuple(_parts_specs(d)[:2])
    else:
        out_shape = jax.ShapeDtypeStruct((t, d), F32)
        out_specs = pl.BlockSpec((ROW_TILE, d), row)
    return pl.pallas_call(
        functools.partial(_combine_dense_kernel, split=split),
        out_shape=out_shape,
        grid=(t // ROW_TILE,),
        in_specs=[pl.BlockSpec((ROW_TILE, d), row), pl.BlockSpec((ROW_TILE, LANES), row),
                  pl.BlockSpec((ROW_TILE, d // 2), row), pl.BlockSpec((ROW_TILE, d // 2), row)],
        out_specs=out_specs,
        compiler_params=_cparams(("arbitrary",)),
        name="moe_combine_dense",
    )(h, rec, ga, gb)


def _combine_kv_q_kernel(h_ref, rec_ref, ga_ref, gb_ref, gkv_ref, wkv_ref, gq_ref, wq_ref, hmean_ref,
                         hgk_ref, hgq_ref, cos_ref, sina_ref, sinb_ref, ho_ref, k_ref, v_ref, q_ref):
    rec = rec_ref[...]
    h = h_ref[...] + rec[:, 4:5] * _unpack_halves(ga_ref[...]) + rec[:, 5:6] * _unpack_halves(gb_ref[...])
    ho_ref[...] = h
    xhat = h * lax.rsqrt(jnp.mean(h * h, axis=-1, keepdims=True) + RMS_EPS)
    tables = (cos_ref[...], sina_ref[...], sinb_ref[...])
    zkv = jnp.dot((xhat * gkv_ref[...]).astype(BF16), wkv_ref[...], preferred_element_type=F32)
    k_ref[...] = _head_norm_rope(zkv[:, :KV_DIM], hmean_ref, hgk_ref[...], *tables)
    v_ref[...] = zkv[:, KV_DIM:]
    zq = jnp.dot((xhat * gq_ref[...]).astype(BF16), wq_ref[...], preferred_element_type=F32)
    q = _head_norm_rope(zq, hmean_ref, hgq_ref[...], *tables)
    q_ref[...] = (q * HEAD_DIM ** -0.5).astype(q_ref.dtype)


def moe_combine_kv_q(h, rec, ga, gb, kv_gain, kv_w_bf16, q_gain, wq_bf16, hmean, k_hgain, q_hgain,
                     cos_t, sina_t, sinb_t):
    t, d = h.shape
    row = lambda i: (i, 0)
    fix = lambda i: (0, 0)
    rope = pl.BlockSpec((ROW_TILE, LANES), _rope_tile)
    return pl.pallas_call(
        _combine_kv_q_kernel,
        out_shape=(jax.ShapeDtypeStruct((t, d), F32), jax.ShapeDtypeStruct((t, KV_DIM), F32),
                   jax.ShapeDtypeStruct((t, KV_DIM), F32), jax.ShapeDtypeStruct((t, d), BF16)),
        grid=(t // ROW_TILE,),
        in_specs=[pl.BlockSpec((ROW_TILE, d), row), pl.BlockSpec((ROW_TILE, LANES), row),
                  pl.BlockSpec((ROW_TILE, d // 2), row), pl.BlockSpec((ROW_TILE, d // 2), row),
                  pl.BlockSpec((1, d), fix), pl.BlockSpec((d, 2 * KV_DIM), fix),
                  pl.BlockSpec((1, d), fix), pl.BlockSpec((d, d), fix),
                  pl.BlockSpec((NORM_SLAB, NORM_SLAB), fix), pl.BlockSpec((1, KV_DIM), fix), pl.BlockSpec((1, d), fix),
                  rope, rope, rope],
        out_specs=(pl.BlockSpec((ROW_TILE, d), row), pl.BlockSpec((ROW_TILE, KV_DIM), row),
                   pl.BlockSpec((ROW_TILE, KV_DIM), row), pl.BlockSpec((ROW_TILE, d), row)),
        compiler_params=_cparams(("parallel",)),
        name="moe_combine_kv_q",
    )(h, rec, ga, gb, kv_gain.reshape(1, d), kv_w_bf16, q_gain.reshape(1, d), wq_bf16, hmean, k_hgain, q_hgain,
      cos_t, sina_t, sinb_t)


def _ffn_kernel(wblk_ref, we_ref, wlo_ref, whi_ref, wnext_ref, wpar_ref, xs_ref, w13_ref, w2_ref, ys_ref,
                w13f, w2f, w13b, w2b, wsem, *, layer):
    w = pl.program_id(0)
    prev = jnp.maximum(w - 1, 0)
    first_visit = (w == 0) | (wblk_ref[w] != wblk_ref[prev])
    lo = wlo_ref[w]
    hi = whi_ref[w]

    def weight_copies(e, par):
        return (pltpu.make_async_copy(w13_ref.at[layer, e], w13f.at[par], wsem.at[par, 0]),
                pltpu.make_async_copy(w2_ref.at[layer, e], w2f.at[par], wsem.at[par, 1]))

    @pl.when(w == 0)
    def _():
        for c in weight_copies(we_ref[0], 0):
            c.start()

    def ffn(x):
        x = _unpack_halves(x).astype(BF16)
        cw = D_EXPERT // FFN_CHUNKS
        gate_up = []
        for c in range(FFN_CHUNKS):
            a = jnp.dot(x, w13b[:, c * cw:(c + 1) * cw], preferred_element_type=F32)
            u = jnp.dot(x, w13b[:, D_EXPERT + c * cw:D_EXPERT + (c + 1) * cw], preferred_element_type=F32)
            gate_up.append((a, u))
        hmid = jnp.concatenate([(_silu(a) * u).astype(BF16) for a, u in gate_up], axis=1)
        return _pack_halves(jnp.dot(hmid, w2b[...], preferred_element_type=F32))

    @pl.when(hi > lo)
    def _():
        @pl.when((w == 0) | (we_ref[w] != we_ref[prev]))
        def _():
            par = wpar_ref[w]
            for c in weight_copies(we_ref[w], par):
                c.wait()
            w13b[...] = w13f[par].astype(BF16)
            w2b[...] = w2f[par].astype(BF16)
            nxt = wnext_ref[w]

            @pl.when(nxt >= 0)
            def _():
                for c in weight_copies(nxt, 1 - par):
                    c.start()

        whole = (lo == 0) & (hi == EXPERT_BLOCK)

        @pl.when(whole)
        def _():
            ys_ref[...] = ffn(xs_ref[...])

        half = FFN_PART
        for p in range(EXPERT_BLOCK // FFN_PART):
            rows = slice(p * half, (p + 1) * half)
            touched = (lo < (p + 1) * half) & (hi > p * half)

            @pl.when(jnp.logical_not(whole) & touched)
            def _():
                y = ffn(xs_ref[rows, :])
                row = lax.broadcasted_iota(jnp.int32, y.shape, 0) + p * half
                mine = (row >= lo) & (row < hi)

                @pl.when(first_visit)
                def _():
                    ys_ref[rows, :] = jnp.where(mine, y, jnp.zeros_like(y))

                @pl.when(jnp.logical_not(first_visit))
                def _():
                    ys_ref[rows, :] = jnp.where(mine, y, ys_ref[rows, :])

            @pl.when(jnp.logical_not(whole) & jnp.logical_not(touched) & first_visit)
            def _():
                ys_ref[rows, :] = jnp.zeros((half, ys_ref.shape[1]), U32)


def moe_ffn(xs, work, w13_all, w2_all, layer):
    n_slots, dp = xs.shape
    d = 2 * dp
    n_work = work[0].shape[0]
    xmap = lambda w, *prefetch: (prefetch[0][w], 0)
    grid_spec = pltpu.PrefetchScalarGridSpec(
        num_scalar_prefetch=len(work),
        grid=(n_work,),
        in_specs=[pl.BlockSpec((EXPERT_BLOCK, dp), xmap),
                  pl.BlockSpec(memory_space=pl.ANY), pl.BlockSpec(memory_space=pl.ANY)],
        out_specs=pl.BlockSpec((EXPERT_BLOCK, dp), xmap),
        scratch_shapes=[pltpu.VMEM((2, d, 2 * D_EXPERT), F32), pltpu.VMEM((2, D_EXPERT, d), F32),
                        pltpu.VMEM((d, 2 * D_EXPERT), BF16), pltpu.VMEM((D_EXPERT, d), BF16),
                        pltpu.SemaphoreType.DMA((2, 2))],
    )
    return pl.pallas_call(
        functools.partial(_ffn_kernel, layer=layer),
        out_shape=jax.ShapeDtypeStruct((n_slots, dp), U32),
        grid_spec=grid_spec,
        compiler_params=_cparams(("arbitrary",)),
        name="moe_ffn",
    )(*work, xs, w13_all, w2_all)


def _ffn_work_items(cnt):
    n_slots = 2 * T_ALL
    n_blocks = n_slots // EXPERT_BLOCK
    n_work = n_blocks + N_EXPERTS - 1
    end = jnp.cumsum(cnt)
    start = end - cnt
    first_blk = start // EXPERT_BLOCK
    last_blk = jnp.maximum(end - 1, start) // EXPERT_BLOCK
    n_items = jnp.where(cnt > 0, last_blk - first_blk + 1, 0)
    item_end = jnp.cumsum(n_items)
    item_start = item_end - n_items
    w = jnp.arange(n_work, dtype=jnp.int32)
    used = w < item_end[-1]
    wq = jnp.minimum(w, item_end[-1] - 1)
    e = jnp.sum((item_end[:, None] <= wq[None, :]).astype(jnp.int32), axis=0)
    onehot = e[None, :] == jnp.arange(N_EXPERTS, dtype=jnp.int32)[:, None]
    of_e = lambda table: jnp.sum(jnp.where(onehot, table[:, None], 0), axis=0)
    blk = jnp.where(used, of_e(first_blk) + (w - of_e(item_start)), n_blocks - 1).astype(jnp.int32)
    lo = jnp.maximum(of_e(start), blk * EXPERT_BLOCK) - blk * EXPERT_BLOCK
    hi = jnp.minimum(of_e(end), (blk + 1) * EXPERT_BLOCK) - blk * EXPERT_BLOCK
    lo = jnp.where(used, lo, 0).astype(jnp.int32)
    hi = jnp.where(used, hi, 0).astype(jnp.int32)
    e_before = jnp.concatenate([jnp.full((1,), -1, jnp.int32), e[:-1]])
    change = used & (e != e_before)
    parity = ((jnp.cumsum(change.astype(jnp.int32)) - 1) % 2).astype(jnp.int32)
    far = jnp.int32(n_work)
    next_change = lax.cummin(jnp.where(change, w, far), axis=0, reverse=True)
    next_change = jnp.concatenate([next_change[1:], jnp.full((1,), far, jnp.int32)])
    e_next = jnp.sum(jnp.where(next_change[None, :] == w[:, None], e[:, None], 0), axis=0)
    e_next = jnp.where(next_change < far, e_next, -1).astype(jnp.int32)
    return start, (blk, e, lo, hi, e_next, parity)


def hier_moe_layer(a, w_out_bf16, x, layer, gain, w_group, b_group, w_expert, b_expert, w13_all, w2_all, utri,
                   finish):
    t = T_ALL
    pad = LANES - N_EXPERTS - N_GROUPS
    w_router = jnp.concatenate([w_expert, w_group, jnp.zeros((D_MODEL, pad), F32)], axis=1)
    b_router = jnp.concatenate([b_expert, b_group, jnp.zeros((pad,), F32)])[:ROUTE_ROWS].reshape(ROUTE_ROWS, 1)
    h, xn, rec, rect, counts = moe_route(a, w_out_bf16, x, gain, w_router, b_router, utri)

    cnt = counts[:N_EXPERTS, 0].astype(jnp.int32)
    start, work = _ffn_work_items(cnt)
    experts = jnp.arange(N_EXPERTS, dtype=jnp.int32)[:, None]

    def slot_of(e_row, rank_row):
        first = jnp.sum(jnp.where(e_row.astype(jnp.int32)[None, :] == experts, start[:, None], 0), axis=0)
        return (first + rank_row.astype(jnp.int32)).reshape(1, t)

    dest_a = slot_of(rect[0], rect[2])
    dest_b = slot_of(rect[1], rect[3])

    xs = moe_dispatch_sc(xn, dest_a, dest_b, 2 * t)
    ys = moe_ffn(xs, work, w13_all, w2_all, layer)
    ga, gb = moe_gather_sc(ys, dest_a, dest_b)
    return finish(h, rec, ga, gb)


def _rope_tables(pos):
    half = ROPE_DIM // 2
    lane = np.arange(LANES) % HEAD_DIM
    rotary = lane < ROPE_DIM
    inv = jnp.where(rotary, jnp.exp(-math.log(ROPE_THETA) * jnp.asarray(lane % half, F32) * (2.0 / ROPE_DIM)), 0.0)
    ang = pos.astype(F32)[:, None] * inv[None, :]
    cos, sin = jnp.cos(ang), jnp.sin(ang)
    first = jnp.asarray(lane < half)
    second = jnp.asarray(rotary & (lane >= half))
    return cos, jnp.where(first, -sin, 0.0), jnp.where(second, sin, 0.0)


def kernel(x_prompt, x_sample, state_hgrn, cache_k_win, cache_v_win, meta_tokens, a_norm, a_w_in, a_lower_logits, a_out_norm, a_w_out, kv_norm, kv_w, k_norm, b_norm, b_wq, b_q_norm, b_sinks, b_w_out, moe_norm, moe_w_group, moe_b_group, moe_w_expert, moe_b_expert, moe_w13, moe_w2):
    tail_rows = T_ALL - OFF_META
    x_parts = (x_prompt.reshape(T_PROMPT, D_MODEL), x_sample.reshape(T_SAMPLE, D_MODEL),
               jnp.concatenate([meta_tokens.astype(F32), jnp.zeros((tail_rows - N_META, D_MODEL), F32)], axis=0))
    pos = jnp.concatenate([N_META + jnp.arange(SEQ, dtype=jnp.int32),
                           jnp.tile(PAST_LEN + jnp.arange(DEC_SEQ, dtype=jnp.int32), ROW_TILE // DEC_SEQ),
                           jnp.arange(N_META, dtype=jnp.int32),
                           jnp.zeros((ROW_TILE - N_META,), jnp.int32)])
    cos_t, sina_t, sinb_t = _rope_tables(pos)
    rs = np.arange(NORM_SLAB)
    hmean = jnp.asarray((rs[:, None] // HEAD_DIM == rs[None, :] // HEAD_DIM).astype(np.float32) / HEAD_DIM, BF16)
    rt = np.arange(ROW_TILE)
    utri = jnp.asarray((rt[:, None] < rt[None, :]).astype(np.float32), BF16)
    lower = jnp.cumsum(jax.nn.softmax(a_lower_logits.astype(F32), axis=0), axis=0)

    moe = functools.partial(hier_moe_layer, w13_all=moe_w13, w2_all=moe_w2, utri=utri)

    z = in_project(x_parts, a_norm[0], a_w_in[0].astype(BF16))
    zero_state = jnp.zeros((1, A_HEADS, A_DK, A_DV), F32)
    o_meta, s_meta = hgrn2_scan(z, zero_state, lower[0], a_out_norm[0],
                                row_off=OFF_META, n_seq=1, seq_len=N_META)
    o_prompt, s_prompt = hgrn2_scan(z, s_meta, lower[0], a_out_norm[0], row_off=0, n_seq=BATCH, seq_len=SEQ)
    o_sample, s_sample = hgrn2_scan(z, state_hgrn[0].astype(F32), lower[0], a_out_norm[0],
                                    row_off=OFF_SAMPLE, n_seq=DEC_BATCH, seq_len=DEC_SEQ, group=SCAN_SAMPLE_GROUP)
    o_tail = jnp.concatenate([o_meta, jnp.zeros((tail_rows - N_META, D_MODEL), BF16)], axis=0)
    finish0 = functools.partial(
        moe_combine_kv_q, kv_gain=kv_norm, kv_w_bf16=kv_w.astype(BF16), q_gain=b_norm[0], wq_bf16=b_wq[0].astype(BF16),
        hmean=hmean, k_hgain=jnp.tile(k_norm, KV_HEADS).reshape(1, KV_DIM),
        q_hgain=jnp.tile(b_q_norm[0], Q_HEADS).reshape(1, D_MODEL), cos_t=cos_t, sina_t=sina_t, sinb_t=sinb_t)
    h, k_all, v_all, q_all = moe((o_prompt, o_sample, o_tail), a_w_out[0].astype(BF16), x_parts, 0, moe_norm[0],
                                 moe_w_group[0], moe_b_group[0], moe_w_expert[0], moe_b_expert[0], finish=finish0)

    meta_blk = lambda a: jnp.concatenate([jnp.zeros((ATT_BLOCK - N_META, KV_DIM), F32),
                                          a[OFF_META:OFF_META + N_META]], axis=0)
    sinks = b_sinks[0].astype(F32)
    att_all = attention_prompt(q_all, k_all, v_all, meta_blk(k_all), meta_blk(v_all), sinks)
    att_all, k_win_s, v_win_s = attention_sample(q_all, cache_k_win.astype(F32), cache_v_win.astype(F32),
                                                 k_all, v_all, sinks, att_all)
    y_p, y_s = moe(att_all, b_w_out[0].astype(BF16), h, 1, moe_norm[1], moe_w_group[1], moe_b_group[1],
                   moe_w_expert[1], moe_b_expert[1], finish=functools.partial(moe_combine_dense, split=True))

    y_prompt = y_p.reshape(BATCH, SEQ, D_MODEL)
    y_sample = y_s.reshape(DEC_BATCH, DEC_SEQ, D_MODEL)
    last = lambda a: jnp.stack([a[(b + 1) * SEQ - WINDOW:(b + 1) * SEQ] for b in range(BATCH)]).reshape(
        BATCH, WINDOW, KV_HEADS, HEAD_DIM)
    kp = last(k_all)
    vp = last(v_all)
    return (y_prompt, y_sample, s_prompt[None], s_sample[None], kp, vp, k_win_s, v_win_s)
```
